```python
import math
import jax, jax.numpy as jnp
from jax import lax
import numpy as np

D_MODEL = 1024
BATCH = 8
SEQ = 8192
DEPTH = 1

MIX_WIDTH = D_MODEL
ATTN_WIDTH = MIX_WIDTH // 2
SSM_WIDTH = MIX_WIDTH - ATTN_WIDTH
HEAD_DIM = 64
N_HEADS = ATTN_WIDTH // HEAD_DIM
DILATED_BRANCHES = ((128, 1), (512, 4), (2048, 16))
BLOCK = 128
SSM_GROUP = 16
N_SSM_GROUPS = SSM_WIDTH // SSM_GROUP
STATE_DIM = 64
D_FF = 2816
IN_WIDTH = 3 * ATTN_WIDTH + SSM_WIDTH
NORM_EPS = 1e-6
DT_MIN = 1e-3
DT_MAX = 1e-1

kernel_name = "hybrid_dilated_alibi_attn_s5_macaron_layer"


def rms_norm(x, g):
    xf = x.astype(jnp.float32)
    y = xf * lax.rsqrt(jnp.mean(xf * xf, axis=-1, keepdims=True) + NORM_EPS)
    return (y * g.astype(jnp.float32)).astype(x.dtype)


def swiglu(x, w_in, w_out):
    gate, up = jnp.split(x @ w_in, 2, axis=-1)
    return (jax.nn.silu(gate) * up) @ w_out


def alibi_slopes(n_heads):
    return 2.0 ** (-8.0 * jnp.arange(1, n_heads + 1, dtype=jnp.float32) / n_heads)


def dilated_window_branch(q, k, v, slopes, window, dilation):
    B, S, H, E = q.shape
    n_back = window // dilation
    L = -(-S // dilation)
    nb = -(-L // BLOCK)
    Lp = nb * BLOCK

    def to_blocks(t):
        t = jnp.pad(t, ((0, 0), (0, L * dilation - S), (0, 0), (0, 0)))
        t = t.reshape(B, L, dilation, H, E).transpose(0, 2, 1, 3, 4)
        t = jnp.pad(t, ((0, 0), (0, 0), (0, Lp - L), (0, 0), (0, 0)))
        return t.reshape(B, dilation, nb, BLOCK, H, E)

    def with_prev(t):
        prev = jnp.pad(t[:, :, :-1], ((0, 0), (0, 0), (1, 0), (0, 0), (0, 0), (0, 0)))
        return jnp.concatenate([prev, t], axis=3)

    def from_blocks(t):
        tail = t.shape[4:]
        t = t.reshape((B, dilation, Lp) + tail)[:, :, :L]
        t = jnp.moveaxis(t, 1, 2).reshape((B, L * dilation) + tail)
        return t[:, :S]

    qb, kb, vb = to_blocks(q), to_blocks(k), to_blocks(v)
    kk, vv = with_prev(kb), with_prev(vb)
    s = jnp.einsum('brnqhe,brnkhe->brnhqk', qb, kk) * (HEAD_DIM ** -0.5)

    qi = jnp.arange(BLOCK)[:, None]
    ci = jnp.arange(2 * BLOCK)[None, :]
    steps = BLOCK + qi - ci
    key_pos = (jnp.arange(nb)[:, None, None] - 1) * BLOCK + ci[None]
    valid = ((steps >= 0) & (steps <= n_back))[None] & (key_pos >= 0)
    dist = (steps * dilation).astype(jnp.float32)
    bias = -slopes[:, None, None] * dist[None]
    s = jnp.where(valid[None, None, :, None], s + bias, -jnp.inf)

    m = jnp.max(s, axis=-1, keepdims=True)
    p = jnp.exp(s - m)
    denom = jnp.sum(p, axis=-1, keepdims=True)
    o = jnp.einsum('brnhqk,brnkhe->brnqhe', p, vv)
    o = o * jnp.swapaxes(1.0 / denom[..., 0], -1, -2)[..., None]
    lse = jnp.swapaxes((m + jnp.log(denom))[..., 0], -1, -2)
    return from_blocks(o), from_blocks(lse)


def dilated_attention(q, k, v):
    B, S, _ = q.shape
    q, k, v = (t.astype(jnp.float32).reshape(B, S, N_HEADS, HEAD_DIM) for t in (q, k, v))
    slopes = alibi_slopes(N_HEADS)
    outs, lses = [], []
    for window, dilation in DILATED_BRANCHES:
        o, l = dilated_window_branch(q, k, v, slopes, window, dilation)
        outs.append(o)
        lses.append(l)
    w = jax.nn.softmax(jnp.stack(lses, axis=-1), axis=-1)
    o = jnp.einsum('bshn,nbshe->bshe', w, jnp.stack(outs, axis=0))
    return o.reshape(B, S, ATTN_WIDTH)


def s5_mixer(u, a_re, a_im, log_dt, b_re, b_im, c_re, c_im, d_skip, w_glu, b_glu):
    B, S, _ = u.shape
    f32 = jnp.float32
    uf = u.astype(f32).reshape(B, S, N_SSM_GROUPS, SSM_GROUP)
    dt = jnp.exp(log_dt.astype(f32))[:, None]
    a = lax.complex(a_re.astype(f32), a_im.astype(f32))
    a_bar = jnp.exp(dt * a)
    b = lax.complex(b_re.astype(f32), b_im.astype(f32))
    b_bar = ((a_bar - 1.0) / a)[..., None] * b
    bu = jnp.einsum('bsgc,gpc->bsgp', uf.astype(jnp.complex64), b_bar)
    a_seq = jnp.broadcast_to(a_bar, bu.shape)

    def combine(left, right):
        a_l, x_l = left
        a_r, x_r = right
        return a_r * a_l, a_r * x_l + x_r

    _, states = lax.associative_scan(combine, (a_seq, bu), axis=1)
    c = lax.complex(c_re.astype(f32), c_im.astype(f32))
    y = jnp.real(jnp.einsum('bsgp,gcp->bsgc', states, c))
    y = y + d_skip.astype(f32).reshape(N_SSM_GROUPS, SSM_GROUP) * uf
    y = jax.nn.gelu(y.reshape(B, S, SSM_WIDTH))
    return y * jax.nn.sigmoid(y @ w_glu.astype(f32) + b_glu.astype(f32))


def _fwd_setup_inputs(seed: int = 0) -> dict:
    key = jax.random.key(seed)
    ks = jax.random.split(key, 24)
    f32 = jnp.float32
    L = DEPTH

    def nrm(k, shape, scale):
        return jax.random.normal(k, shape, f32) * scale

    def gain(k):
        return 1.0 + 0.05 * jax.random.normal(k, (L, D_MODEL), f32)

    n_idx = jnp.arange(STATE_DIM, dtype=f32)
    a_re = -0.5 + 0.01 * jax.random.normal(ks[9], (L, N_SSM_GROUPS, STATE_DIM), f32)
    a_im = math.pi * n_idx + 0.01 * jax.random.normal(ks[10], (L, N_SSM_GROUPS, STATE_DIM), f32)
    log_dt = jax.random.uniform(ks[11], (L, N_SSM_GROUPS), f32,
                                math.log(DT_MIN), math.log(DT_MAX))
    return {
        "x": jax.random.normal(ks[0], (BATCH, SEQ, D_MODEL), f32),
        "ffn1_pre_g": gain(ks[1]),
        "ffn1_w_in": nrm(ks[2], (L, D_MODEL, 2 * D_FF), D_MODEL ** -0.5),
        "ffn1_w_out": nrm(ks[3], (L, D_FF, D_MODEL), D_FF ** -0.5),
        "ffn1_post_g": gain(ks[4]),
        "mix_pre_g": gain(ks[5]),
        "w_mix_in": nrm(ks[6], (L, D_MODEL, IN_WIDTH), D_MODEL ** -0.5),
        "a_re": a_re,
        "a_im": a_im,
        "log_dt": log_dt,
        "b_re": nrm(ks[12], (L, N_SSM_GROUPS, STATE_DIM, SSM_GROUP), (2 * SSM_GROUP) ** -0.5),
        "b_im": nrm(ks[13], (L, N_SSM_GROUPS, STATE_DIM, SSM_GROUP), (2 * SSM_GROUP) ** -0.5),
        "c_re": nrm(ks[14], (L, N_SSM_GROUPS, SSM_GROUP, STATE_DIM), (2 * STATE_DIM) ** -0.5),
        "c_im": nrm(ks[15], (L, N_SSM_GROUPS, SSM_GROUP, STATE_DIM), (2 * STATE_DIM) ** -0.5),
        "d_skip": nrm(ks[16], (L, SSM_WIDTH), 1.0),
        "w_glu": nrm(ks[17], (L, SSM_WIDTH, SSM_WIDTH), SSM_WIDTH ** -0.5),
        "b_glu": nrm(ks[18], (L, SSM_WIDTH), 0.01),
        "w_mix_out": nrm(ks[19], (L, MIX_WIDTH, D_MODEL), MIX_WIDTH ** -0.5),
        "mix_post_g": gain(ks[20]),
        "ffn2_pre_g": gain(ks[21]),
        "ffn2_w_in": nrm(ks[22], (L, D_MODEL, 2 * D_FF), D_MODEL ** -0.5),
        "ffn2_w_out": nrm(ks[23], (L, D_FF, D_MODEL), D_FF ** -0.5),
        "ffn2_post_g": gain(ks[7]),
    }


def _fwd_reference(x, ffn1_pre_g, ffn1_w_in, ffn1_w_out, ffn1_post_g, mix_pre_g, w_mix_in,
              a_re, a_im, log_dt, b_re, b_im, c_re, c_im, d_skip, w_glu, b_glu,
              w_mix_out, mix_post_g, ffn2_pre_g, ffn2_w_in, ffn2_w_out, ffn2_post_g):
    for l in range(DEPTH):
        h = rms_norm(x, ffn1_pre_g[l])
        x = x + 0.5 * rms_norm(swiglu(h, ffn1_w_in[l], ffn1_w_out[l]), ffn1_post_g[l])
        h = rms_norm(x, mix_pre_g[l])
        proj = h @ w_mix_in[l]
        q, k, v, u = jnp.split(proj, [ATTN_WIDTH, 2 * ATTN_WIDTH, 3 * ATTN_WIDTH], axis=-1)
        attn = dilated_attention(q, k, v).astype(x.dtype)
        ssm = s5_mixer(u, a_re[l], a_im[l], log_dt[l], b_re[l], b_im[l], c_re[l], c_im[l],
                       d_skip[l], w_glu[l], b_glu[l]).astype(x.dtype)
        mixed = jnp.concatenate([attn, ssm], axis=-1) @ w_mix_out[l]
        x = x + rms_norm(mixed, mix_post_g[l])
        h = rms_norm(x, ffn2_pre_g[l])
        x = x + 0.5 * rms_norm(swiglu(h, ffn2_w_in[l], ffn2_w_out[l]), ffn2_post_g[l])
    return x


import jax as _jax
import jax.numpy as _jnp

TWIN_FORMAT = 'train_step'
FWD_PARAMS = ['x', 'ffn1_pre_g', 'ffn1_w_in', 'ffn1_w_out', 'ffn1_post_g', 'mix_pre_g', 'w_mix_in', 'a_re', 'a_im', 'log_dt', 'b_re', 'b_im', 'c_re', 'c_im', 'd_skip', 'w_glu', 'b_glu', 'w_mix_out', 'mix_post_g', 'ffn2_pre_g', 'ffn2_w_in', 'ffn2_w_out', 'ffn2_post_g']
TWIN_WEIGHTS = ['ffn1_pre_g', 'ffn1_w_in', 'ffn1_w_out', 'ffn1_post_g', 'mix_pre_g', 'w_mix_in', 'a_re', 'a_im', 'log_dt', 'b_re', 'b_im', 'c_re', 'c_im', 'd_skip', 'w_glu', 'b_glu', 'w_mix_out', 'mix_post_g', 'ffn2_pre_g', 'ffn2_w_in', 'ffn2_w_out', 'ffn2_post_g']
TWIN_DIFF_INPUT = 'x'
TWIN_INPUTS = ['x', 'ffn1_pre_g', 'ffn1_w_in', 'ffn1_w_out', 'ffn1_post_g', 'mix_pre_g', 'w_mix_in', 'a_re', 'a_im', 'log_dt', 'b_re', 'b_im', 'c_re', 'c_im', 'd_skip', 'w_glu', 'b_glu', 'w_mix_out', 'mix_post_g', 'ffn2_pre_g', 'ffn2_w_in', 'ffn2_w_out', 'ffn2_post_g', 'loss_target', 'm_ffn1_pre_g', 'm_ffn1_w_in', 'm_ffn1_w_out', 'm_ffn1_post_g', 'm_mix_pre_g', 'm_w_mix_in', 'm_a_re', 'm_a_im', 'm_log_dt', 'm_b_re', 'm_b_im', 'm_c_re', 'm_c_im', 'm_d_skip', 'm_w_glu', 'm_b_glu', 'm_w_mix_out', 'm_mix_post_g', 'm_ffn2_pre_g', 'm_ffn2_w_in', 'm_ffn2_w_out', 'm_ffn2_post_g', 'v_ffn1_pre_g', 'v_ffn1_w_in', 'v_ffn1_w_out', 'v_ffn1_post_g', 'v_mix_pre_g', 'v_w_mix_in', 'v_a_re', 'v_a_im', 'v_log_dt', 'v_b_re', 'v_b_im', 'v_c_re', 'v_c_im', 'v_d_skip', 'v_w_glu', 'v_b_glu', 'v_w_mix_out', 'v_mix_post_g', 'v_ffn2_pre_g', 'v_ffn2_w_in', 'v_ffn2_w_out', 'v_ffn2_post_g']
TWIN_OUTPUTS = ['loss', 'grad_x', 'grad_ffn1_pre_g', 'grad_ffn1_w_in', 'grad_ffn1_w_out', 'grad_ffn1_post_g', 'grad_mix_pre_g', 'grad_w_mix_in', 'grad_a_re', 'grad_a_im', 'grad_log_dt', 'grad_b_re', 'grad_b_im', 'grad_c_re', 'grad_c_im', 'grad_d_skip', 'grad_w_glu', 'grad_b_glu', 'grad_w_mix_out', 'grad_mix_post_g', 'grad_ffn2_pre_g', 'grad_ffn2_w_in', 'grad_ffn2_w_out', 'grad_ffn2_post_g', 'delta_ffn1_pre_g', 'delta_ffn1_w_in', 'delta_ffn1_w_out', 'delta_ffn1_post_g', 'delta_mix_pre_g', 'delta_w_mix_in', 'delta_a_re', 'delta_a_im', 'delta_log_dt', 'delta_b_re', 'delta_b_im', 'delta_c_re', 'delta_c_im', 'delta_d_skip', 'delta_w_glu', 'delta_b_glu', 'delta_w_mix_out', 'delta_mix_post_g', 'delta_ffn2_pre_g', 'delta_ffn2_w_in', 'delta_ffn2_w_out', 'delta_ffn2_post_g', 'new_m_ffn1_pre_g', 'new_m_ffn1_w_in', 'new_m_ffn1_w_out', 'new_m_ffn1_post_g', 'new_m_mix_pre_g', 'new_m_w_mix_in', 'new_m_a_re', 'new_m_a_im', 'new_m_log_dt', 'new_m_b_re', 'new_m_b_im', 'new_m_c_re', 'new_m_c_im', 'new_m_d_skip', 'new_m_w_glu', 'new_m_b_glu', 'new_m_w_mix_out', 'new_m_mix_post_g', 'new_m_ffn2_pre_g', 'new_m_ffn2_w_in', 'new_m_ffn2_w_out', 'new_m_ffn2_post_g', 'new_v_ffn1_pre_g', 'new_v_ffn1_w_in', 'new_v_ffn1_w_out', 'new_v_ffn1_post_g', 'new_v_mix_pre_g', 'new_v_w_mix_in', 'new_v_a_re', 'new_v_a_im', 'new_v_log_dt', 'new_v_b_re', 'new_v_b_im', 'new_v_c_re', 'new_v_c_im', 'new_v_d_skip', 'new_v_w_glu', 'new_v_b_glu', 'new_v_w_mix_out', 'new_v_mix_post_g', 'new_v_ffn2_pre_g', 'new_v_ffn2_w_in', 'new_v_ffn2_w_out', 'new_v_ffn2_post_g']
TWIN_LEAF_KINDS = {'loss': 'loss', 'grad_x': 'grad_x', 'grad_ffn1_pre_g': 'grad_w', 'grad_ffn1_w_in': 'grad_w', 'grad_ffn1_w_out': 'grad_w', 'grad_ffn1_post_g': 'grad_w', 'grad_mix_pre_g': 'grad_w', 'grad_w_mix_in': 'grad_w', 'grad_a_re': 'grad_w', 'grad_a_im': 'grad_w', 'grad_log_dt': 'grad_w', 'grad_b_re': 'grad_w', 'grad_b_im': 'grad_w', 'grad_c_re': 'grad_w', 'grad_c_im': 'grad_w', 'grad_d_skip': 'grad_w', 'grad_w_glu': 'grad_w', 'grad_b_glu': 'grad_w', 'grad_w_mix_out': 'grad_w', 'grad_mix_post_g': 'grad_w', 'grad_ffn2_pre_g': 'grad_w', 'grad_ffn2_w_in': 'grad_w', 'grad_ffn2_w_out': 'grad_w', 'grad_ffn2_post_g': 'grad_w', 'delta_ffn1_pre_g': 'delta_w', 'delta_ffn1_w_in': 'delta_w', 'delta_ffn1_w_out': 'delta_w', 'delta_ffn1_post_g': 'delta_w', 'delta_mix_pre_g': 'delta_w', 'delta_w_mix_in': 'delta_w', 'delta_a_re': 'delta_w', 'delta_a_im': 'delta_w', 'delta_log_dt': 'delta_w', 'delta_b_re': 'delta_w', 'delta_b_im': 'delta_w', 'delta_c_re': 'delta_w', 'delta_c_im': 'delta_w', 'delta_d_skip': 'delta_w', 'delta_w_glu': 'delta_w', 'delta_b_glu': 'delta_w', 'delta_w_mix_out': 'delta_w', 'delta_mix_post_g': 'delta_w', 'delta_ffn2_pre_g': 'delta_w', 'delta_ffn2_w_in': 'delta_w', 'delta_ffn2_w_out': 'delta_w', 'delta_ffn2_post_g': 'delta_w', 'new_m_ffn1_pre_g': 'new_m', 'new_m_ffn1_w_in': 'new_m', 'new_m_ffn1_w_out': 'new_m', 'new_m_ffn1_post_g': 'new_m', 'new_m_mix_pre_g': 'new_m', 'new_m_w_mix_in': 'new_m', 'new_m_a_re': 'new_m', 'new_m_a_im': 'new_m', 'new_m_log_dt': 'new_m', 'new_m_b_re': 'new_m', 'new_m_b_im': 'new_m', 'new_m_c_re': 'new_m', 'new_m_c_im': 'new_m', 'new_m_d_skip': 'new_m', 'new_m_w_glu': 'new_m', 'new_m_b_glu': 'new_m', 'new_m_w_mix_out': 'new_m', 'new_m_mix_post_g': 'new_m', 'new_m_ffn2_pre_g': 'new_m', 'new_m_ffn2_w_in': 'new_m', 'new_m_ffn2_w_out': 'new_m', 'new_m_ffn2_post_g': 'new_m', 'new_v_ffn1_pre_g': 'new_v', 'new_v_ffn1_w_in': 'new_v', 'new_v_ffn1_w_out': 'new_v', 'new_v_ffn1_post_g': 'new_v', 'new_v_mix_pre_g': 'new_v', 'new_v_w_mix_in': 'new_v', 'new_v_a_re': 'new_v', 'new_v_a_im': 'new_v', 'new_v_log_dt': 'new_v', 'new_v_b_re': 'new_v', 'new_v_b_im': 'new_v', 'new_v_c_re': 'new_v', 'new_v_c_im': 'new_v', 'new_v_d_skip': 'new_v', 'new_v_w_glu': 'new_v', 'new_v_b_glu': 'new_v', 'new_v_w_mix_out': 'new_v', 'new_v_mix_post_g': 'new_v', 'new_v_ffn2_pre_g': 'new_v', 'new_v_ffn2_w_in': 'new_v', 'new_v_ffn2_w_out': 'new_v', 'new_v_ffn2_post_g': 'new_v'}


def _forward(args):
    return _fwd_reference(*[args[k] for k in FWD_PARAMS])


def _output_shape():
    def fwd():
        inp = _fwd_setup_inputs(0)
        return _fwd_reference(*[inp[k] for k in FWD_PARAMS])
    out = _jax.eval_shape(fwd)
    return out.shape, out.dtype

N_MICROBATCH = 1
ADAM_LR = 0.001
ADAM_B1 = 0.9
ADAM_B2 = 0.999
ADAM_EPS = 1e-08
ADAM_WD = 0.01
ADAM_STEP = 10
PER_EXAMPLE_BATCH_AXIS = {'x': 0, 'loss_target': 0}
SHARED_INPUTS = []
_WEIGHT_DTYPES = {'ffn1_pre_g': _jnp.float32, 'ffn1_w_in': _jnp.float32, 'ffn1_w_out': _jnp.float32, 'ffn1_post_g': _jnp.float32, 'mix_pre_g': _jnp.float32, 'w_mix_in': _jnp.float32, 'a_re': _jnp.float32, 'a_im': _jnp.float32, 'log_dt': _jnp.float32, 'b_re': _jnp.float32, 'b_im': _jnp.float32, 'c_re': _jnp.float32, 'c_im': _jnp.float32, 'd_skip': _jnp.float32, 'w_glu': _jnp.float32, 'b_glu': _jnp.float32, 'w_mix_out': _jnp.float32, 'mix_post_g': _jnp.float32, 'ffn2_pre_g': _jnp.float32, 'ffn2_w_in': _jnp.float32, 'ffn2_w_out': _jnp.float32, 'ffn2_post_g': _jnp.float32}
MOMENT_SCALE = {'ffn1_pre_g': 6.686813e-01, 'ffn1_w_in': 2.633517e-01, 'ffn1_w_out': 5.052387e-01, 'ffn1_post_g': 1.605908e+01, 'mix_pre_g': 1.312847e+00, 'w_mix_in': 8.542174e-01, 'a_re': 2.354159e-02, 'a_im': 2.809437e-02, 'log_dt': 1.393982e+01, 'b_re': 1.811555e-02, 'b_im': 1.852514e-02, 'c_re': 3.635514e-02, 'c_im': 3.650090e-02, 'd_skip': 2.687889e+00, 'w_glu': 3.781416e-01, 'b_glu': 1.226759e+00, 'w_mix_out': 1.892289e+00, 'mix_post_g': 6.522197e+01, 'ffn2_pre_g': 1.174296e+00, 'ffn2_w_in': 4.900191e-01, 'ffn2_w_out': 1.020593e+00, 'ffn2_post_g': 1.609103e+01}


def _to_microbatches(a, axis):
    t = _jnp.moveaxis(a, axis, 0)
    t = t.reshape((N_MICROBATCH, t.shape[0] // N_MICROBATCH) + t.shape[1:])
    return _jnp.moveaxis(t, 1, axis + 1)


def setup_inputs(seed: int = 0) -> dict:
    inp = _fwd_setup_inputs(seed)
    key = _jax.random.fold_in(_jax.random.key(seed), 7919)
    shape, _ = _output_shape()
    out = dict(inp)
    out["loss_target"] = _jax.random.normal(_jax.random.fold_in(key, 0), shape, _jnp.float32)
    for i, name in enumerate(TWIN_WEIGHTS):
        w = inp[name].astype(_jnp.float32)
        if MOMENT_SCALE is None:
            s = _jnp.sqrt(_jnp.mean(_jnp.square(w)) + 1e-30)
        else:
            s = MOMENT_SCALE[name]
        km, kv = _jax.random.split(_jax.random.fold_in(key, i + 1))
        out[name] = w
        out["m_" + name] = s * _jax.random.normal(km, w.shape, _jnp.float32)
        out["v_" + name] = (s * s) * _jax.random.uniform(kv, w.shape, _jnp.float32, 0.5, 1.5)
    if N_MICROBATCH > 1:
        for name, axis in PER_EXAMPLE_BATCH_AXIS.items():
            out[name] = _to_microbatches(out[name], axis)
    return {'x': out['x'], 'ffn1_pre_g': out['ffn1_pre_g'], 'ffn1_w_in': out['ffn1_w_in'], 'ffn1_w_out': out['ffn1_w_out'], 'ffn1_post_g': out['ffn1_post_g'], 'mix_pre_g': out['mix_pre_g'], 'w_mix_in': out['w_mix_in'], 'a_re': out['a_re'], 'a_im': out['a_im'], 'log_dt': out['log_dt'], 'b_re': out['b_re'], 'b_im': out['b_im'], 'c_re': out['c_re'], 'c_im': out['c_im'], 'd_skip': out['d_skip'], 'w_glu': out['w_glu'], 'b_glu': out['b_glu'], 'w_mix_out': out['w_mix_out'], 'mix_post_g': out['mix_post_g'], 'ffn2_pre_g': out['ffn2_pre_g'], 'ffn2_w_in': out['ffn2_w_in'], 'ffn2_w_out': out['ffn2_w_out'], 'ffn2_post_g': out['ffn2_post_g'], 'loss_target': out['loss_target'], 'm_ffn1_pre_g': out['m_ffn1_pre_g'], 'm_ffn1_w_in': out['m_ffn1_w_in'], 'm_ffn1_w_out': out['m_ffn1_w_out'], 'm_ffn1_post_g': out['m_ffn1_post_g'], 'm_mix_pre_g': out['m_mix_pre_g'], 'm_w_mix_in': out['m_w_mix_in'], 'm_a_re': out['m_a_re'], 'm_a_im': out['m_a_im'], 'm_log_dt': out['m_log_dt'], 'm_b_re': out['m_b_re'], 'm_b_im': out['m_b_im'], 'm_c_re': out['m_c_re'], 'm_c_im': out['m_c_im'], 'm_d_skip': out['m_d_skip'], 'm_w_glu': out['m_w_glu'], 'm_b_glu': out['m_b_glu'], 'm_w_mix_out': out['m_w_mix_out'], 'm_mix_post_g': out['m_mix_post_g'], 'm_ffn2_pre_g': out['m_ffn2_pre_g'], 'm_ffn2_w_in': out['m_ffn2_w_in'], 'm_ffn2_w_out': out['m_ffn2_w_out'], 'm_ffn2_post_g': out['m_ffn2_post_g'], 'v_ffn1_pre_g': out['v_ffn1_pre_g'], 'v_ffn1_w_in': out['v_ffn1_w_in'], 'v_ffn1_w_out': out['v_ffn1_w_out'], 'v_ffn1_post_g': out['v_ffn1_post_g'], 'v_mix_pre_g': out['v_mix_pre_g'], 'v_w_mix_in': out['v_w_mix_in'], 'v_a_re': out['v_a_re'], 'v_a_im': out['v_a_im'], 'v_log_dt': out['v_log_dt'], 'v_b_re': out['v_b_re'], 'v_b_im': out['v_b_im'], 'v_c_re': out['v_c_re'], 'v_c_im': out['v_c_im'], 'v_d_skip': out['v_d_skip'], 'v_w_glu': out['v_w_glu'], 'v_b_glu': out['v_b_glu'], 'v_w_mix_out': out['v_w_mix_out'], 'v_mix_post_g': out['v_mix_post_g'], 'v_ffn2_pre_g': out['v_ffn2_pre_g'], 'v_ffn2_w_in': out['v_ffn2_w_in'], 'v_ffn2_w_out': out['v_ffn2_w_out'], 'v_ffn2_post_g': out['v_ffn2_post_g']}


def _loss(weights, diff, rest, loss_target):
    with _jax.named_scope("forward"):
        args = {**rest, TWIN_DIFF_INPUT: diff, **{k: w.astype(_WEIGHT_DTYPES[k]) for k, w in weights.items()}}
        y = _forward(args)
    with _jax.named_scope("loss_head"):
        err = _jnp.square(y.astype(_jnp.float32) - loss_target)
        return 0.5 * _jnp.sum(_jnp.mean(err, axis=-1)) if err.ndim else 0.5 * err


def _adamw(w, g, m, v):
    m = ADAM_B1 * m + (1.0 - ADAM_B1) * g
    v = ADAM_B2 * v + (1.0 - ADAM_B2) * _jnp.square(g)
    m_hat = m / (1.0 - ADAM_B1 ** ADAM_STEP)
    v_hat = v / (1.0 - ADAM_B2 ** ADAM_STEP)
    delta = -ADAM_LR * (m_hat / (_jnp.sqrt(v_hat) + ADAM_EPS) + ADAM_WD * w)
    return delta, m, v


def reference(x, ffn1_pre_g, ffn1_w_in, ffn1_w_out, ffn1_post_g, mix_pre_g, w_mix_in, a_re, a_im, log_dt, b_re, b_im, c_re, c_im, d_skip, w_glu, b_glu, w_mix_out, mix_post_g, ffn2_pre_g, ffn2_w_in, ffn2_w_out, ffn2_post_g, loss_target, m_ffn1_pre_g, m_ffn1_w_in, m_ffn1_w_out, m_ffn1_post_g, m_mix_pre_g, m_w_mix_in, m_a_re, m_a_im, m_log_dt, m_b_re, m_b_im, m_c_re, m_c_im, m_d_skip, m_w_glu, m_b_glu, m_w_mix_out, m_mix_post_g, m_ffn2_pre_g, m_ffn2_w_in, m_ffn2_w_out, m_ffn2_post_g, v_ffn1_pre_g, v_ffn1_w_in, v_ffn1_w_out, v_ffn1_post_g, v_mix_pre_g, v_w_mix_in, v_a_re, v_a_im, v_log_dt, v_b_re, v_b_im, v_c_re, v_c_im, v_d_skip, v_w_glu, v_b_glu, v_w_mix_out, v_mix_post_g, v_ffn2_pre_g, v_ffn2_w_in, v_ffn2_w_out, v_ffn2_post_g):
    given = dict(x=x, ffn1_pre_g=ffn1_pre_g, ffn1_w_in=ffn1_w_in, ffn1_w_out=ffn1_w_out, ffn1_post_g=ffn1_post_g, mix_pre_g=mix_pre_g, w_mix_in=w_mix_in, a_re=a_re, a_im=a_im, log_dt=log_dt, b_re=b_re, b_im=b_im, c_re=c_re, c_im=c_im, d_skip=d_skip, w_glu=w_glu, b_glu=b_glu, w_mix_out=w_mix_out, mix_post_g=mix_post_g, ffn2_pre_g=ffn2_pre_g, ffn2_w_in=ffn2_w_in, ffn2_w_out=ffn2_w_out, ffn2_post_g=ffn2_post_g, loss_target=loss_target, m_ffn1_pre_g=m_ffn1_pre_g, m_ffn1_w_in=m_ffn1_w_in, m_ffn1_w_out=m_ffn1_w_out, m_ffn1_post_g=m_ffn1_post_g, m_mix_pre_g=m_mix_pre_g, m_w_mix_in=m_w_mix_in, m_a_re=m_a_re, m_a_im=m_a_im, m_log_dt=m_log_dt, m_b_re=m_b_re, m_b_im=m_b_im, m_c_re=m_c_re, m_c_im=m_c_im, m_d_skip=m_d_skip, m_w_glu=m_w_glu, m_b_glu=m_b_glu, m_w_mix_out=m_w_mix_out, m_mix_post_g=m_mix_post_g, m_ffn2_pre_g=m_ffn2_pre_g, m_ffn2_w_in=m_ffn2_w_in, m_ffn2_w_out=m_ffn2_w_out, m_ffn2_post_g=m_ffn2_post_g, v_ffn1_pre_g=v_ffn1_pre_g, v_ffn1_w_in=v_ffn1_w_in, v_ffn1_w_out=v_ffn1_w_out, v_ffn1_post_g=v_ffn1_post_g, v_mix_pre_g=v_mix_pre_g, v_w_mix_in=v_w_mix_in, v_a_re=v_a_re, v_a_im=v_a_im, v_log_dt=v_log_dt, v_b_re=v_b_re, v_b_im=v_b_im, v_c_re=v_c_re, v_c_im=v_c_im, v_d_skip=v_d_skip, v_w_glu=v_w_glu, v_b_glu=v_b_glu, v_w_mix_out=v_w_mix_out, v_mix_post_g=v_mix_post_g, v_ffn2_pre_g=v_ffn2_pre_g, v_ffn2_w_in=v_ffn2_w_in, v_ffn2_w_out=v_ffn2_w_out, v_ffn2_post_g=v_ffn2_post_g)
    weights = {n: given[n] for n in TWIN_WEIGHTS}
    shared = {n: given[n] for n in SHARED_INPUTS}
    per_example = {n: given[n] for n in ['x']}
    grad_fn = _jax.value_and_grad(_loss, argnums=(0, 1))

    def one_microbatch(ex, loss_target):
        ex = dict(ex)
        diff = ex.pop(TWIN_DIFF_INPUT)
        return grad_fn(weights, diff, {**shared, **ex}, loss_target)

    if N_MICROBATCH == 1:
        loss, (grad_w, grad_x) = one_microbatch(per_example, given["loss_target"])
    else:
        def body(carry, xs):
            loss_sum, grad_sum = carry
            l_k, (gw_k, gx_k) = one_microbatch(xs[0], xs[1])
            with _jax.named_scope("update"):
                return (loss_sum + l_k, _jax.tree.map(_jnp.add, grad_sum, gw_k)), gx_k

        init = (_jnp.zeros((), _jnp.float32), _jax.tree.map(_jnp.zeros_like, weights))
        (loss, grad_w), grad_x = _jax.lax.scan(body, init, (per_example, given["loss_target"]))
    with _jax.named_scope("update"):
        delta_w, new_m, new_v = {}, {}, {}
        for n in TWIN_WEIGHTS:
            delta_w[n], new_m[n], new_v[n] = _adamw(weights[n], grad_w[n], given["m_" + n], given["v_" + n])
    return (loss, grad_x, *[grad_w[n] for n in TWIN_WEIGHTS], *[delta_w[n] for n in TWIN_WEIGHTS],
            *[new_m[n] for n in TWIN_WEIGHTS], *[new_v[n] for n in TWIN_WEIGHTS])
```

```python
import functools
import math

import jax
import jax.numpy as jnp
from jax import lax
from jax.experimental import pallas as pl
from jax.experimental.pallas import tpu as pltpu

F32, BF16 = jnp.float32, jnp.bfloat16

D_MODEL = 1024
D_FF = 2816
N_CHIPS = 4
FF_BLK = 2 * D_FF // N_CHIPS
ATTN_W = 512
SSM_W = 512
HEAD_DIM = 64
N_HEADS = ATTN_W // HEAD_DIM
DILATIONS = (1, 4, 16)
N_BACK = 128
QBLK = 128
N_GROUPS = 32
GROUP_CH = 16
STATE = 64
N_STATE = N_GROUPS * STATE
EPS = 1e-6
NEG = -1e30
GELU_C = math.sqrt(2.0 / math.pi)

ADAM_LR, ADAM_B1, ADAM_B2, ADAM_EPS, ADAM_WD, ADAM_STEP = 0.001, 0.9, 0.999, 1e-08, 0.01, 10

VMEM_LIMIT_V7X = 60 * 1024 * 1024
ROW_TILE = 256


def _params(*sem):
    return pltpu.CompilerParams(dimension_semantics=sem, vmem_limit_bytes=VMEM_LIMIT_V7X)


def _dot(a, b):
    return jnp.dot(a.astype(BF16), b.astype(BF16), preferred_element_type=F32)


def _dot_nt(a, b):
    return lax.dot_general(a.astype(BF16), b.astype(BF16), (((1,), (1,)), ((), ())), preferred_element_type=F32)


def _dot_tn(a, b):
    return lax.dot_general(a.astype(BF16), b.astype(BF16), (((0,), (0,)), ((), ())), preferred_element_type=F32)


def _full(shape):
    return pl.BlockSpec(shape, lambda *_: (0,) * len(shape))


def _rows(tm, width):
    return pl.BlockSpec((tm, width), lambda i: (i, 0))


ANY = pl.BlockSpec(memory_space=pl.ANY)


def _load_once(pairs, sems):
    copies = [pltpu.make_async_copy(src, dst, sems.at[k]) for k, (src, dst) in enumerate(pairs)]
    for c in copies:
        c.start()
    for c in copies:
        c.wait()


def _rms(x):
    return lax.rsqrt(jnp.mean(x * x, axis=-1, keepdims=True) + EPS)


def _rms_bwd(dy_g, xn, r):
    return r * (dy_g - xn * jnp.mean(dy_g * xn, axis=-1, keepdims=True))


def _ffn_fwd(x, g_pre, w_in, w_out, g_post, target, *, name):
    T = x.shape[0]
    tm = ROW_TILE
    with_loss = target is not None

    def body(*refs):
        if with_loss:
            x_ref, gpre_ref, gpost_ref, tgt_ref, win_hbm, wout_hbm, o_ref, loss_ref, z_ref, f_ref, win_v, wout_v, sems = refs
        else:
            x_ref, gpre_ref, gpost_ref, win_hbm, wout_hbm, o_ref, z_ref, f_ref, win_v, wout_v, sems = refs

        @pl.when(pl.program_id(0) == 0)
        def _():
            _load_once([(win_hbm, win_v), (wout_hbm, wout_v)], sems)
            if with_loss:
                loss_ref[...] = jnp.zeros_like(loss_ref)

        xv = x_ref[...]
        h = (xv * _rms(xv) * gpre_ref[...]).astype(BF16)
        f = jnp.zeros((tm, D_MODEL), F32)
        for k in range(2):
            gate = _dot(h, win_v[k])
            up = _dot(h, win_v[k + 2])
            z_ref[:, k * FF_BLK:(k + 1) * FF_BLK] = gate.astype(BF16)
            z_ref[:, D_FF + k * FF_BLK:D_FF + (k + 1) * FF_BLK] = up.astype(BF16)
            f = f + _dot(gate * jax.nn.sigmoid(gate) * up, wout_v[k])
        f_ref[...] = f
        out = xv + 0.5 * (f * _rms(f) * gpost_ref[...])
        if with_loss:
            err = out - tgt_ref[...]
            o_ref[...] = err * (1.0 / D_MODEL)
            loss_ref[...] += jnp.sum(err * err) * (0.5 / D_MODEL)
        else:
            o_ref[...] = out

    row = _rows(tm, D_MODEL)
    vec = _full((1, D_MODEL))
    in_specs = [row, vec, vec] + ([row] if with_loss else []) + [ANY, ANY]
    out_shape = [jax.ShapeDtypeStruct((T, D_MODEL), F32)]
    out_specs = [row]
    if with_loss:
        out_shape.append(jax.ShapeDtypeStruct((8, 128), F32))
        out_specs.append(_full((8, 128)))
    out_shape += [jax.ShapeDtypeStruct((T, 2 * D_FF), BF16), jax.ShapeDtypeStruct((T, D_MODEL), F32)]
    out_specs += [_rows(tm, 2 * D_FF), row]
    args = (x, g_pre, g_post) + ((target,) if with_loss else ()) + (w_in, w_out)
    return pl.pallas_call(
        body, name=name, grid=(T // tm,), in_specs=in_specs, out_specs=out_specs, out_shape=out_shape,
        scratch_shapes=[pltpu.VMEM(w_in.shape, BF16), pltpu.VMEM(w_out.shape, BF16), pltpu.SemaphoreType.DMA((2,))],
        compiler_params=_params("arbitrary"),
    )(*args)


def _ffn_bwd_out(dout, f, z, w_out, g_post, *, name):
    T = dout.shape[0]
    tm = ROW_TILE
    nt = T // tm

    def body(dout_ref, f_ref, z_ref, gpost_ref, wout_hbm, dz_ref, dgpost_ref, dwout_hbm, wout_v, dwout_v, sems):
        i = pl.program_id(0)

        @pl.when(i == 0)
        def _():
            _load_once([(wout_hbm, wout_v)], sems)
            dwout_v[...] = jnp.zeros_like(dwout_v)
            dgpost_ref[...] = jnp.zeros_like(dgpost_ref)

        dy = 0.5 * dout_ref[...]
        f = f_ref[...]
        r = _rms(f)
        fn = f * r
        dgpost_ref[...] += jnp.sum(dy * fn, axis=0, keepdims=True)
        df = _rms_bwd(dy * gpost_ref[...], fn, r).astype(BF16)
        for k in range(2):
            gate = z_ref[:, k * FF_BLK:(k + 1) * FF_BLK].astype(F32)
            up = z_ref[:, D_FF + k * FF_BLK:D_FF + (k + 1) * FF_BLK].astype(F32)
            sg = jax.nn.sigmoid(gate)
            silu = gate * sg
            dwout_v[k] += _dot_tn(silu * up, df)
            da = _dot_nt(df, wout_v[k])
            dz_ref[:, k * FF_BLK:(k + 1) * FF_BLK] = (da * up * (sg * (1.0 + gate * (1.0 - sg)))).astype(BF16)
            dz_ref[:, D_FF + k * FF_BLK:D_FF + (k + 1) * FF_BLK] = (da * silu).astype(BF16)

        @pl.when(i == nt - 1)
        def _():
            c = pltpu.make_async_copy(dwout_v, dwout_hbm, sems.at[0])
            c.start()
            c.wait()

    row = _rows(tm, D_MODEL)
    return pl.pallas_call(
        body, name=name, grid=(nt,),
        in_specs=[row, row, _rows(tm, 2 * D_FF), _full((1, D_MODEL)), ANY],
        out_specs=[_rows(tm, 2 * D_FF), _full((1, D_MODEL)), ANY],
        out_shape=[jax.ShapeDtypeStruct((T, 2 * D_FF), BF16), jax.ShapeDtypeStruct((1, D_MODEL), F32),
                   jax.ShapeDtypeStruct(w_out.shape, F32)],
        scratch_shapes=[pltpu.VMEM(w_out.shape, BF16), pltpu.VMEM(w_out.shape, F32), pltpu.SemaphoreType.DMA((1,))],
        compiler_params=_params("arbitrary"),
    )(dout, f, z, g_post, w_out)


def _norm_matmul_bwd(dz, x, dres, g, w, *, name):
    T = x.shape[0]
    nb, _, bw = w.shape
    tm = ROW_TILE
    nt = T // tm

    def body(dz_ref, x_ref, dres_ref, g_ref, w_hbm, dx_ref, dg_ref, dw_hbm, w_v, dw_v, sems):
        i = pl.program_id(0)

        @pl.when(i == 0)
        def _():
            _load_once([(w_hbm, w_v)], sems)
            dw_v[...] = jnp.zeros_like(dw_v)
            dg_ref[...] = jnp.zeros_like(dg_ref)

        xv = x_ref[...]
        r = _rms(xv)
        xn = xv * r
        gv = g_ref[...]
        h = (xn * gv).astype(BF16)
        dh = jnp.zeros((tm, D_MODEL), F32)
        for j in range(nb):
            dzj = dz_ref[:, j * bw:(j + 1) * bw]
            dw_v[j] += _dot_tn(h, dzj)
            dh = dh + _dot_nt(dzj, w_v[j])
        dg_ref[...] += jnp.sum(dh * xn, axis=0, keepdims=True)
        dx_ref[...] = _rms_bwd(dh * gv, xn, r) + dres_ref[...]

        @pl.when(i == nt - 1)
        def _():
            c = pltpu.make_async_copy(dw_v, dw_hbm, sems.at[0])
            c.start()
            c.wait()

    row = _rows(tm, D_MODEL)
    return pl.pallas_call(
        body, name=name, grid=(nt,),
        in_specs=[_rows(tm, nb * bw), row, row, _full((1, D_MODEL)), ANY],
        out_specs=[row, _full((1, D_MODEL)), ANY],
        out_shape=[jax.ShapeDtypeStruct((T, D_MODEL), F32), jax.ShapeDtypeStruct((1, D_MODEL), F32),
                   jax.ShapeDtypeStruct(w.shape, F32)],
        scratch_shapes=[pltpu.VMEM(w.shape, BF16), pltpu.VMEM(w.shape, F32), pltpu.SemaphoreType.DMA((1,))],
        compiler_params=_params("arbitrary"),
    )(dz, x, dres, g, w)


def _mix_in_fwd(x, g, w):
    T = x.shape[0]
    tm = ROW_TILE

    def body(x_ref, g_ref, w_ref, q_ref, k_ref, v_ref, u_ref):
        xv = x_ref[...]
        h = (xv * _rms(xv) * g_ref[...]).astype(BF16)
        for j, o_ref in enumerate((q_ref, k_ref, v_ref, u_ref)):
            o_ref[...] = _dot(h, w_ref[j])

    col = _rows(tm, ATTN_W)
    return pl.pallas_call(
        body, name="mix_in_fwd", grid=(T // tm,),
        in_specs=[_rows(tm, D_MODEL), _full((1, D_MODEL)), _full(w.shape)],
        out_specs=[col] * 4, out_shape=[jax.ShapeDtypeStruct((T, ATTN_W), F32)] * 4,
        compiler_params=_params("parallel"),
    )(x, g, w)


def _mix_out_fwd(x, attn, ssm, w, g):
    T = x.shape[0]
    tm = ROW_TILE

    def body(x_ref, a_ref, s_ref, w_ref, g_ref, o_ref, m_ref):
        mixed = _dot(a_ref[...], w_ref[0]) + _dot(s_ref[...], w_ref[1])
        m_ref[...] = mixed
        o_ref[...] = x_ref[...] + mixed * _rms(mixed) * g_ref[...]

    row, col = _rows(tm, D_MODEL), _rows(tm, ATTN_W)
    return pl.pallas_call(
        body, name="mix_out_fwd", grid=(T // tm,),
        in_specs=[row, col, col, _full(w.shape), _full((1, D_MODEL))],
        out_specs=[row, row], out_shape=[jax.ShapeDtypeStruct((T, D_MODEL), F32)] * 2,
        compiler_params=_params("parallel"),
    )(x, attn, ssm, w, g)


def _mix_out_bwd(dout, mixed, attn, ssm, w, g):
    T = dout.shape[0]
    tm = ROW_TILE

    def body(dout_ref, m_ref, a_ref, s_ref, w_ref, g_ref, da_ref, ds_ref, dw_ref, dg_ref):
        @pl.when(pl.program_id(0) == 0)
        def _():
            dw_ref[...] = jnp.zeros_like(dw_ref)
            dg_ref[...] = jnp.zeros_like(dg_ref)

        dy = dout_ref[...]
        mixed = m_ref[...]
        r = _rms(mixed)
        mn = mixed * r
        dg_ref[...] += jnp.sum(dy * mn, axis=0, keepdims=True)
        dm = _rms_bwd(dy * g_ref[...], mn, r).astype(BF16)
        da_ref[...] = _dot_nt(dm, w_ref[0])
        ds_ref[...] = _dot_nt(dm, w_ref[1])
        dw_ref[0] += _dot_tn(a_ref[...], dm)
        dw_ref[1] += _dot_tn(s_ref[...], dm)

    row, col = _rows(tm, D_MODEL), _rows(tm, ATTN_W)
    return pl.pallas_call(
        body, name="mix_out_bwd", grid=(T // tm,),
        in_specs=[row, row, col, col, _full(w.shape), _full((1, D_MODEL))],
        out_specs=[col, col, _full(w.shape), _full((1, D_MODEL))],
        out_shape=[jax.ShapeDtypeStruct((T, ATTN_W), F32)] * 2
        + [jax.ShapeDtypeStruct(w.shape, F32), jax.ShapeDtypeStruct((1, D_MODEL), F32)],
        compiler_params=_params("arbitrary"),
    )(dout, mixed, attn, ssm, w, g)


def _branch_view(t, d):
    return t.reshape(t.shape[0] // d, d * t.shape[1])


def _attn_branch_fwd(q, k, v, d):
    T = q.shape[0]
    nb = T // d // QBLK
    scale = HEAD_DIM ** -0.5

    def body(q_ref, kc_ref, kp_ref, vc_ref, vp_ref, o_ref, l_ref):
        j = pl.program_id(1)
        qi = lax.broadcasted_iota(jnp.int32, (QBLK, 2 * QBLK), 0)
        ci = lax.broadcasted_iota(jnp.int32, (QBLK, 2 * QBLK), 1)
        steps = QBLK + qi - ci
        valid = (steps >= 0) & (steps <= N_BACK) & ((ci >= QBLK) | (j > 0))
        dist = (steps * d).astype(F32)
        for h in range(N_HEADS):
            sl = slice(h * HEAD_DIM, (h + 1) * HEAD_DIM)
            kk = jnp.concatenate([kp_ref[:, sl], kc_ref[:, sl]], axis=0)
            vv = jnp.concatenate([vp_ref[:, sl], vc_ref[:, sl]], axis=0)
            s = _dot_nt(q_ref[:, sl], kk) * scale - (2.0 ** -(h + 1)) * dist
            s = jnp.where(valid, s, NEG)
            m = jnp.max(s, axis=-1, keepdims=True)
            p = jnp.exp(s - m)
            den = jnp.sum(p, axis=-1, keepdims=True)
            o_ref[:, sl] = _dot(p, vv) / den
            l_ref[:, sl] = jnp.broadcast_to(m + jnp.log(den), (QBLK, HEAD_DIM))

    cur = pl.BlockSpec((QBLK, ATTN_W), lambda r, j: (j, r))
    prev = pl.BlockSpec((QBLK, ATTN_W), lambda r, j: (jnp.maximum(j - 1, 0), r))
    shape = jax.ShapeDtypeStruct((T // d, d * ATTN_W), F32)
    o, l = pl.pallas_call(
        body, name=f"attn_fwd_d{d}", grid=(d, nb),
        in_specs=[cur, cur, prev, cur, prev], out_specs=[cur, cur], out_shape=[shape, shape],
        compiler_params=_params("parallel", "parallel"),
    )(_branch_view(q, d), _branch_view(k, d), _branch_view(k, d), _branch_view(v, d), _branch_view(v, d))
    return o.reshape(T, ATTN_W), l.reshape(T, ATTN_W)


def _attn_merge(outs, lses):
    T = outs[0].shape[0]
    tm = 512

    def body(o1, o2, o3, l1, l2, l3, a_ref, lse_ref):
        ls = [l1[...], l2[...], l3[...]]
        m = jnp.maximum(jnp.maximum(ls[0], ls[1]), ls[2])
        lse = m + jnp.log(jnp.exp(ls[0] - m) + jnp.exp(ls[1] - m) + jnp.exp(ls[2] - m))
        lse_ref[...] = lse
        a_ref[...] = jnp.exp(ls[0] - lse) * o1[...] + jnp.exp(ls[1] - lse) * o2[...] + jnp.exp(ls[2] - lse) * o3[...]

    col = _rows(tm, ATTN_W)
    return pl.pallas_call(
        body, name="attn_merge", grid=(T // tm,), in_specs=[col] * 6, out_specs=[col, col],
        out_shape=[jax.ShapeDtypeStruct((T, ATTN_W), F32)] * 2, compiler_params=_params("parallel"),
    )(*outs, *lses)


def _attn_branch_bwd(q, k, v, o, lse, do, d):
    T = q.shape[0]
    nb = T // d // QBLK
    scale = HEAD_DIM ** -0.5

    def body(qc_ref, qn_ref, kc_ref, kp_ref, vc_ref, vp_ref, oc_ref, on_ref, lc_ref, ln_ref, doc_ref, don_ref,
             dq_ref, dk_ref, dv_ref):
        j = pl.program_id(1)
        qi = lax.broadcasted_iota(jnp.int32, (QBLK, 2 * QBLK), 0)
        ci = lax.broadcasted_iota(jnp.int32, (QBLK, 2 * QBLK), 1)
        steps_q = QBLK + qi - ci
        valid_q = (steps_q >= 0) & (steps_q <= N_BACK) & ((ci >= QBLK) | (j > 0))
        dist_q = (steps_q * d).astype(F32)
        ri = lax.broadcasted_iota(jnp.int32, (2 * QBLK, QBLK), 0)
        ki = lax.broadcasted_iota(jnp.int32, (2 * QBLK, QBLK), 1)
        steps_k = ri - ki
        valid_k = (steps_k >= 0) & (steps_k <= N_BACK) & ((ri < QBLK) | (j < nb - 1))
        dist_k = (steps_k * d).astype(F32)
        for h in range(N_HEADS):
            sl = slice(h * HEAD_DIM, (h + 1) * HEAD_DIM)
            one = slice(h * HEAD_DIM, h * HEAD_DIM + 1)
            slope = 2.0 ** -(h + 1)
            qc, qn, kc, vc = qc_ref[:, sl], qn_ref[:, sl], kc_ref[:, sl], vc_ref[:, sl]
            doc, don = doc_ref[:, sl], don_ref[:, sl]
            lse_c, lse_n = lc_ref[:, one], ln_ref[:, one]
            delta_c = jnp.sum(doc * oc_ref[:, sl], axis=-1, keepdims=True)
            delta_n = jnp.sum(don * on_ref[:, sl], axis=-1, keepdims=True)

            kk = jnp.concatenate([kp_ref[:, sl], kc], axis=0)
            vv = jnp.concatenate([vp_ref[:, sl], vc], axis=0)
            s = jnp.where(valid_q, _dot_nt(qc, kk) * scale - slope * dist_q, NEG)
            p = jnp.exp(s - lse_c)
            ds = p * (_dot_nt(doc, vv) - delta_c)
            dq_ref[:, sl] = _dot(ds, kk) * scale

            qq = jnp.concatenate([qc, qn], axis=0)
            doo = jnp.concatenate([doc, don], axis=0)
            s2 = jnp.where(valid_k, _dot_nt(qq, kc) * scale - slope * dist_k, NEG)
            p2 = jnp.exp(s2 - jnp.concatenate([lse_c, lse_n], axis=0))
            dv_ref[:, sl] = _dot_tn(p2, doo)
            ds2 = p2 * (_dot_nt(doo, vc) - jnp.concatenate([delta_c, delta_n], axis=0))
            dk_ref[:, sl] = _dot_tn(ds2, qq) * scale

    cur = pl.BlockSpec((QBLK, ATTN_W), lambda r, j: (j, r))
    prev = pl.BlockSpec((QBLK, ATTN_W), lambda r, j: (jnp.maximum(j - 1, 0), r))
    nxt = pl.BlockSpec((QBLK, ATTN_W), lambda r, j: (jnp.minimum(j + 1, nb - 1), r))
    shape = jax.ShapeDtypeStruct((T // d, d * ATTN_W), F32)
    bv = functools.partial(_branch_view, d=d)
    dq, dk, dv = pl.pallas_call(
        body, name=f"attn_bwd_d{d}", grid=(d, nb),
        in_specs=[cur, nxt, cur, prev, cur, prev, cur, nxt, cur, nxt, cur, nxt],
        out_specs=[cur] * 3, out_shape=[shape] * 3, compiler_params=_params("parallel", "parallel"),
    )(bv(q), bv(q), bv(k), bv(k), bv(v), bv(v), bv(o), bv(o), bv(lse), bv(lse), bv(do), bv(do))
    return dq.reshape(T, ATTN_W), dk.reshape(T, ATTN_W), dv.reshape(T, ATTN_W)


def _dproj_merge(dqs, dks, dvs, du):
    T = du.shape[0]
    tm = 512

    def body(*refs):
        o_ref = refs[-1]
        for part in range(3):
            a, b, c = refs[3 * part:3 * part + 3]
            o_ref[:, part * ATTN_W:(part + 1) * ATTN_W] = (a[...] + b[...] + c[...]).astype(BF16)
        o_ref[:, 3 * ATTN_W:] = refs[9][...].astype(BF16)

    col = _rows(tm, ATTN_W)
    return pl.pallas_call(
        body, name="dproj_merge", grid=(T // tm,), in_specs=[col] * 10, out_specs=_rows(tm, 4 * ATTN_W),
        out_shape=jax.ShapeDtypeStruct((T, 4 * ATTN_W), BF16), compiler_params=_params("parallel"),
    )(*dqs, *dks, *dvs, du)


def _attention_fwd(q, k, v):
    res = [_attn_branch_fwd(q, k, v, d) for d in DILATIONS]
    return _attn_merge([r[0] for r in res], [r[1] for r in res])


def _attention_bwd(q, k, v, attn, lse, dattn):
    res = [_attn_branch_bwd(q, k, v, attn, lse, dattn, d) for d in DILATIONS]
    return [r[0] for r in res], [r[1] for r in res], [r[2] for r in res]


SCAN_ROWS = 8
SCAN_LANES = 512
SSM_CHUNK = 256


def _cmul(ar, ai, br, bi):
    return ar * br - ai * bi, ar * bi + ai * br


def _ssm_discretize(a_re, a_im, log_dt, b_re, b_im):
    def body(ar_ref, ai_ref, ldt_ref, br_ref, bi_ref, abr_ref, abi_ref, er_ref, ei_ref, bbr_ref, bbi_ref, pr_ref, pi_ref):
        ar, ai = ar_ref[...], ai_ref[...]
        dt = jnp.exp(ldt_ref[...])
        n = lax.broadcasted_iota(jnp.int32, (1, SCAN_ROWS), 1).astype(F32) + 1.0
        mag, ang = jnp.exp(dt * ar), dt * ai
        abr, abi = mag * jnp.cos(ang), mag * jnp.sin(ang)
        abr_ref[...], abi_ref[...] = abr, abi
        pr_ref[...] = jnp.exp(dt * ar * n) * jnp.cos(ang * n)
        pi_ref[...] = jnp.exp(dt * ar * n) * jnp.sin(ang * n)
        den = ar * ar + ai * ai
        er = ((abr - 1.0) * ar + abi * ai) / den
        ei = (abi * ar - (abr - 1.0) * ai) / den
        er_ref[...], ei_ref[...] = er, ei
        bbr_ref[...], bbi_ref[...] = _cmul(er, ei, br_ref[...], bi_ref[...])

    col = jax.ShapeDtypeStruct((N_STATE, 1), F32)
    mat = jax.ShapeDtypeStruct((N_STATE, GROUP_CH), F32)
    pw = jax.ShapeDtypeStruct((N_STATE, SCAN_ROWS), F32)
    return pl.pallas_call(body, name="ssm_discretize", out_shape=[col] * 4 + [mat] * 2 + [pw] * 2)(
        a_re, a_im, log_dt, b_re, b_im)


def _ssm_discretize_bwd(a_re, a_im, log_dt, b_re, b_im, ab_re, ab_im, e_re, e_im, dab_re, dab_im, dbb_re, dbb_im):
    def body(ar_ref, ai_ref, ldt_ref, br_ref, bi_ref, abr_ref, abi_ref, er_ref, ei_ref, dabr_ref, dabi_ref,
             dbbr_ref, dbbi_ref, dar_ref, dai_ref, ddt_ref, dbr_ref, dbi_ref):
        ar, ai, dt = ar_ref[...], ai_ref[...], jnp.exp(ldt_ref[...])
        er, ei = er_ref[...], ei_ref[...]
        gbr, gbi = dbbr_ref[...], dbbi_ref[...]
        dbr_ref[...], dbi_ref[...] = _cmul(er, -ei, gbr, gbi)
        br, bi = br_ref[...], bi_ref[...]
        der = jnp.sum(br * gbr + bi * gbi, axis=-1, keepdims=True)
        dei = jnp.sum(br * gbi - bi * gbr, axis=-1, keepdims=True)
        den = ar * ar + ai * ai
        inv_r, inv_i = ar / den, -ai / den
        t_r, t_i = _cmul(der, dei, inv_r, -inv_i)
        gab_r, gab_i = dabr_ref[...] + t_r, dabi_ref[...] + t_i
        q_r, q_i = _cmul(er, ei, inv_r, inv_i)
        dl_r, dl_i = _cmul(der, dei, q_r, -q_i)
        dl_r, dl_i = -dl_r, -dl_i
        gw_r, gw_i = _cmul(gab_r, gab_i, abr_ref[...], -abi_ref[...])
        dar_ref[...] = dl_r + dt * gw_r
        dai_ref[...] = dl_i + dt * gw_i
        ddt_ref[...] = (gw_r * ar + gw_i * ai) * dt

    col = jax.ShapeDtypeStruct((N_STATE, 1), F32)
    mat = jax.ShapeDtypeStruct((N_STATE, GROUP_CH), F32)
    return pl.pallas_call(body, name="ssm_discretize_bwd", out_shape=[col] * 3 + [mat] * 2)(
        a_re, a_im, log_dt, b_re, b_im, ab_re, ab_im, e_re, e_im, dab_re, dab_im, dbb_re, dbb_im)


def _scan_tables(p_re, p_im, reverse):
    pr, pi = p_re.T, p_im.T
    if reverse:
        pi = -pi
    bc = lambda t, n: jnp.broadcast_to(t[n - 1], (SCAN_ROWS, N_STATE))
    carry = (pr[::-1], pi[::-1]) if reverse else (pr, pi)
    return jnp.stack([bc(pr, 1), bc(pi, 1), bc(pr, 2), bc(pi, 2), bc(pr, 4), bc(pi, 4), carry[0], carry[1]])


def _scan_group(xr, xi, tab_ref, ls, carry_r, carry_i, reverse):
    row = lax.broadcasted_iota(jnp.int32, (SCAN_ROWS, SCAN_LANES), 0)
    for n, s in enumerate((1, 2, 4)):
        if reverse:
            keep = row < SCAN_ROWS - s
            shr, shi = pltpu.roll(xr, SCAN_ROWS - s, 0), pltpu.roll(xi, SCAN_ROWS - s, 0)
        else:
            keep = row >= s
            shr, shi = pltpu.roll(xr, s, 0), pltpu.roll(xi, s, 0)
        shr, shi = jnp.where(keep, shr, 0.0), jnp.where(keep, shi, 0.0)
        mr, mi = _cmul(tab_ref[2 * n, :, ls], tab_ref[2 * n + 1, :, ls], shr, shi)
        xr, xi = xr + mr, xi + mi
    mr, mi = _cmul(tab_ref[6, :, ls], tab_ref[7, :, ls], carry_r, carry_i)
    return xr + mr, xi + mi


def _gelu(y):
    t = jnp.tanh(GELU_C * (y + 0.044715 * y * y * y))
    return 0.5 * y * (1.0 + t), t


def _ssm_fwd(u, tab, bd_re, bd_im, cd_re, cd_im, d_skip, w_glu, b_glu):
    T = u.shape[0]
    tc = SSM_CHUNK

    def body(u_ref, tab_ref, bdr_ref, bdi_ref, cdr_ref, cdi_ref, dsk_ref, wg_ref, bg_ref,
             sr_ref, si_ref, yp_ref, o_ref, car_r, car_i):
        @pl.when(pl.program_id(0) == 0)
        def _():
            car_r[...] = jnp.zeros_like(car_r)
            car_i[...] = jnp.zeros_like(car_i)

        uv = u_ref[...]
        sr_ref[...] = _dot(uv, bdr_ref[...])
        si_ref[...] = _dot(uv, bdi_ref[...])
        for lb in range(N_STATE // SCAN_LANES):
            ls = pl.ds(lb * SCAN_LANES, SCAN_LANES)

            def step(g, carry):
                rows = pl.ds(pl.multiple_of(g * SCAN_ROWS, SCAN_ROWS), SCAN_ROWS)
                xr, xi = _scan_group(sr_ref[rows, ls], si_ref[rows, ls], tab_ref, ls, carry[0], carry[1], False)
                sr_ref[rows, ls] = xr
                si_ref[rows, ls] = xi
                last = slice(SCAN_ROWS - 1, SCAN_ROWS)
                return (jnp.broadcast_to(xr[last], xr.shape), jnp.broadcast_to(xi[last], xi.shape))

            cr, ci = lax.fori_loop(0, tc // SCAN_ROWS, step, (car_r[:, ls], car_i[:, ls]))
            car_r[:, ls] = cr
            car_i[:, ls] = ci
        y = _dot(sr_ref[...], cdr_ref[...]) - _dot(si_ref[...], cdi_ref[...]) + dsk_ref[...] * uv
        yp_ref[...] = y
        gy, _ = _gelu(y)
        o_ref[...] = gy * jax.nn.sigmoid(_dot(gy, wg_ref[...]) + bg_ref[...])

    col, st = _rows(tc, SSM_W), _rows(tc, N_STATE)
    vec = _full((1, SSM_W))
    return pl.pallas_call(
        body, name="ssm_fwd", grid=(T // tc,),
        in_specs=[col, _full(tab.shape), _full(bd_re.shape), _full(bd_im.shape), _full(cd_re.shape), _full(cd_im.shape),
                  vec, _full(w_glu.shape), vec],
        out_specs=[st, st, col, col],
        out_shape=[jax.ShapeDtypeStruct((T, N_STATE), F32)] * 2 + [jax.ShapeDtypeStruct((T, SSM_W), F32)] * 2,
        scratch_shapes=[pltpu.VMEM((SCAN_ROWS, N_STATE), F32)] * 2,
        compiler_params=_params("arbitrary"),
    )(u, tab, bd_re, bd_im, cd_re, cd_im, d_skip, w_glu, b_glu)


def _ssm_bwd(dout, u, yp, s_re, s_im, tab, bd_re, bd_im, cd_re, cd_im, d_skip, w_glu, b_glu):
    T = u.shape[0]
    tc = SSM_CHUNK
    nt = T // tc
    rows_per_chunk = tc // SCAN_ROWS

    def body(do_ref, u_ref, yp_ref, sr_ref, si_ref, pr_ref, pi_ref, tab_ref, dsk_ref, wg_ref, bg_ref,
             bdr_hbm, bdi_hbm, cdr_hbm, cdi_hbm,
             du_ref, dsk_out, dbg_out, dwg_out, da_out, dbdr_hbm, dbdi_hbm, dcdr_hbm, dcdi_hbm,
             bdr_v, bdi_v, cdr_v, cdi_v, dbdr_v, dbdi_v, dcdr_v, dcdi_v, gr_v, gi_v, car_r, car_i, sems):
        i = pl.program_id(0)

        @pl.when(i == 0)
        def _():
            _load_once([(bdr_hbm, bdr_v), (bdi_hbm, bdi_v), (cdr_hbm, cdr_v), (cdi_hbm, cdi_v)], sems)
            for ref in (dbdr_v, dbdi_v, dcdr_v, dcdi_v, car_r, car_i, dsk_out, dbg_out, dwg_out, da_out):
                ref[...] = jnp.zeros_like(ref)

        uv, y, dout_v = u_ref[...], yp_ref[...], do_ref[...]
        gy, t = _gelu(y)
        sg = jax.nn.sigmoid(_dot(gy, wg_ref[...]) + bg_ref[...])
        dzg = dout_v * gy * sg * (1.0 - sg)
        dgy = dout_v * sg + _dot_nt(dzg, wg_ref[...])
        dwg_out[...] += _dot_tn(gy, dzg)
        dbg_out[...] += jnp.sum(dzg, axis=0, keepdims=True)
        dy = dgy * (0.5 * (1.0 + t) + 0.5 * y * (1.0 - t * t) * GELU_C * (1.0 + 3 * 0.044715 * y * y))
        dsk_out[...] += jnp.sum(dy * uv, axis=0, keepdims=True)

        gr_v[...] = _dot_nt(dy, cdr_v[...])
        gi_v[...] = -_dot_nt(dy, cdi_v[...])
        dcdr_v[...] += _dot_tn(sr_ref[...], dy)
        dcdi_v[...] -= _dot_tn(si_ref[...], dy)

        row = lax.broadcasted_iota(jnp.int32, (SCAN_ROWS, SCAN_LANES), 0)
        first_chunk = i == nt - 1
        for lb in range(N_STATE // SCAN_LANES):
            ls = pl.ds(lb * SCAN_LANES, SCAN_LANES)

            def step(n, carry):
                g = rows_per_chunk - 1 - n
                rows = pl.ds(pl.multiple_of(g * SCAN_ROWS, SCAN_ROWS), SCAN_ROWS)
                before = pl.ds(pl.multiple_of(jnp.maximum(g - 1, 0) * SCAN_ROWS, SCAN_ROWS), SCAN_ROWS)
                xr, xi = _scan_group(gr_v[rows, ls], gi_v[rows, ls], tab_ref, ls, carry[0], carry[1], True)
                gr_v[rows, ls] = xr
                gi_v[rows, ls] = xi
                last = slice(SCAN_ROWS - 1, SCAN_ROWS)
                edge_r = jnp.where(g > 0, sr_ref[before, ls][last], jnp.where(first_chunk, 0.0, pr_ref[:, ls][last]))
                edge_i = jnp.where(g > 0, si_ref[before, ls][last], jnp.where(first_chunk, 0.0, pi_ref[:, ls][last]))
                spr = jnp.where(row >= 1, pltpu.roll(sr_ref[rows, ls], 1, 0), edge_r)
                spi = jnp.where(row >= 1, pltpu.roll(si_ref[rows, ls], 1, 0), edge_i)
                first = slice(0, 1)
                return (jnp.broadcast_to(xr[first], xr.shape), jnp.broadcast_to(xi[first], xi.shape),
                        carry[2] + xr * spr + xi * spi, carry[3] + xi * spr - xr * spi)

            zero = jnp.zeros((SCAN_ROWS, SCAN_LANES), F32)
            cr, ci, dar, dai = lax.fori_loop(0, rows_per_chunk, step, (car_r[:, ls], car_i[:, ls], zero, zero))
            car_r[:, ls] = cr
            car_i[:, ls] = ci
            da_out[0, :, ls] += dar
            da_out[1, :, ls] += dai

        du_ref[...] = dsk_ref[...] * dy + _dot_nt(gr_v[...], bdr_v[...]) + _dot_nt(gi_v[...], bdi_v[...])
        dbdr_v[...] += _dot_tn(uv, gr_v[...])
        dbdi_v[...] += _dot_tn(uv, gi_v[...])

        @pl.when(i == nt - 1)
        def _():
            outs = [(dbdr_v, dbdr_hbm), (dbdi_v, dbdi_hbm), (dcdr_v, dcdr_hbm), (dcdi_v, dcdi_hbm)]
            copies = [pltpu.make_async_copy(src, dst, sems.at[k]) for k, (src, dst) in enumerate(outs)]
            for c in copies:
                c.start()
            for c in copies:
                c.wait()

    rev = lambda i: (nt - 1 - i, 0)
    col = pl.BlockSpec((tc, SSM_W), rev)
    st = pl.BlockSpec((tc, N_STATE), rev)
    st_before = pl.BlockSpec((SCAN_ROWS, N_STATE), lambda i: (jnp.maximum((nt - 1 - i) * rows_per_chunk - 1, 0), 0))
    vec = _full((1, SSM_W))
    bd = jax.ShapeDtypeStruct(bd_re.shape, F32)
    cd = jax.ShapeDtypeStruct(cd_re.shape, F32)
    return pl.pallas_call(
        body, name="ssm_bwd", grid=(nt,),
        in_specs=[col, col, col, st, st, st_before, st_before, _full(tab.shape), vec, _full(w_glu.shape), vec,
                  ANY, ANY, ANY, ANY],
        out_specs=[col, vec, vec, _full(w_glu.shape), _full((2, SCAN_ROWS, N_STATE)), ANY, ANY, ANY, ANY],
        out_shape=[jax.ShapeDtypeStruct((T, SSM_W), F32), jax.ShapeDtypeStruct((1, SSM_W), F32),
                   jax.ShapeDtypeStruct((1, SSM_W), F32), jax.ShapeDtypeStruct(w_glu.shape, F32),
                   jax.ShapeDtypeStruct((2, SCAN_ROWS, N_STATE), F32), bd, bd, cd, cd],
        scratch_shapes=[pltpu.VMEM(bd_re.shape, BF16)] * 2 + [pltpu.VMEM(cd_re.shape, BF16)] * 2
        + [pltpu.VMEM(bd_re.shape, F32)] * 2 + [pltpu.VMEM(cd_re.shape, F32)] * 2
        + [pltpu.VMEM((tc, N_STATE), F32)] * 2 + [pltpu.VMEM((SCAN_ROWS, N_STATE), F32)] * 2
        + [pltpu.SemaphoreType.DMA((4,))],
        compiler_params=_params("arbitrary"),
    )(dout, u, yp, s_re, s_im, s_re, s_im, tab, d_skip, w_glu, b_glu, bd_re, bd_im, cd_re, cd_im)


def _block_diag(t):
    g, a, b = t.shape
    eye = jnp.eye(N_GROUPS, dtype=t.dtype)
    return (t[:, :, None, :] * eye[:, None, :, None]).reshape(g * a, g * b)


def _diag_blocks(m, a, b):
    eye = jnp.eye(N_GROUPS, dtype=m.dtype)
    return jnp.sum(m.reshape(N_GROUPS, a, N_GROUPS, b) * eye[:, None, :, None], axis=2)


def _ssm_prepare(a_re, a_im, log_dt, b_re, b_im, c_re, c_im):
    col = lambda t: t.reshape(N_STATE, 1)
    ldt = jnp.broadcast_to(log_dt.reshape(N_GROUPS, 1), (N_GROUPS, STATE)).reshape(N_STATE, 1)
    b2r, b2i = b_re.reshape(N_STATE, GROUP_CH), b_im.reshape(N_STATE, GROUP_CH)
    ab_r, ab_i, e_r, e_i, bb_r, bb_i, p_r, p_i = _ssm_discretize(col(a_re), col(a_im), ldt, b2r, b2i)
    bd = [_block_diag(jnp.swapaxes(t.reshape(N_GROUPS, STATE, GROUP_CH), 1, 2)).astype(BF16) for t in (bb_r, bb_i)]
    cd = [_block_diag(jnp.swapaxes(t.reshape(N_GROUPS, GROUP_CH, STATE), 1, 2)).astype(BF16) for t in (c_re, c_im)]
    saved = dict(a_re=col(a_re), a_im=col(a_im), log_dt=ldt, b_re=b2r, b_im=b2i, ab_re=ab_r, ab_im=ab_i, e_re=e_r, e_im=e_i)
    return _scan_tables(p_r, p_i, False), _scan_tables(p_r, p_i, True), bd, cd, saved


BIG = ("ffn1_w_in", "ffn1_w_out", "w_mix_in", "w_glu", "w_mix_out", "ffn2_w_in", "ffn2_w_out")
SMALL = ("ffn1_pre_g", "ffn1_post_g", "mix_pre_g", "a_re", "a_im", "log_dt", "b_re", "b_im", "c_re", "c_im",
         "d_skip", "b_glu", "mix_post_g", "ffn2_pre_g", "ffn2_post_g")


def _local_step(x, target, p, w):
    vec = lambda t: t.reshape(1, -1)
    w1_in, w2_in = w["ffn1_w_in"], w["ffn2_w_in"]
    w1_out, w2_out = (w[n].reshape(2, FF_BLK, D_MODEL) for n in ("ffn1_w_out", "ffn2_w_out"))
    w_mi = w["w_mix_in"]
    w_glu = w["w_glu"].reshape(SSM_W, SSM_W)
    w_mo = w["w_mix_out"].reshape(2, ATTN_W, D_MODEL)

    x1, z1, f1 = _ffn_fwd(x, vec(p["ffn1_pre_g"]), w1_in, w1_out, vec(p["ffn1_post_g"]), None, name="ffn1_fwd")
    q, k, v, u = _mix_in_fwd(x1, vec(p["mix_pre_g"]), w_mi)
    attn, lse = _attention_fwd(q, k, v)
    tab_f, tab_b, bd, cd, sv = _ssm_prepare(p["a_re"], p["a_im"], p["log_dt"], p["b_re"], p["b_im"], p["c_re"], p["c_im"])
    ssm_args = (bd[0], bd[1], cd[0], cd[1], vec(p["d_skip"]), w_glu, vec(p["b_glu"]))
    s_re, s_im, yp, ssm = _ssm_fwd(u, tab_f, *ssm_args)
    x2, mixed = _mix_out_fwd(x1, attn, ssm, w_mo, vec(p["mix_post_g"]))
    dx3, loss_rows, z2, f2 = _ffn_fwd(x2, vec(p["ffn2_pre_g"]), w2_in, w2_out, vec(p["ffn2_post_g"]), target, name="ffn2_fwd")

    g = {}
    dz2, g["ffn2_post_g"], dw2_out = _ffn_bwd_out(dx3, f2, z2, w2_out, vec(p["ffn2_post_g"]), name="ffn2_bwd_out")
    dx2, g["ffn2_pre_g"], g["ffn2_w_in"] = _norm_matmul_bwd(dz2, x2, dx3, vec(p["ffn2_pre_g"]), w2_in, name="ffn2_bwd_in")
    dattn, dssm, dw_mo, g["mix_post_g"] = _mix_out_bwd(dx2, mixed, attn, ssm, w_mo, vec(p["mix_post_g"]))
    du, g["d_skip"], g["b_glu"], dw_glu, da, dbd_re, dbd_im, dcd_re, dcd_im = _ssm_bwd(dssm, u, yp, s_re, s_im, tab_b, *ssm_args)
    dqs, dks, dvs = _attention_bwd(q, k, v, attn, lse, dattn)
    dproj = _dproj_merge(dqs, dks, dvs, du)
    dx1, g["mix_pre_g"], g["w_mix_in"] = _norm_matmul_bwd(dproj, x1, dx2, vec(p["mix_pre_g"]), w_mi, name="mix_bwd_in")
    dz1, g["ffn1_post_g"], dw1_out = _ffn_bwd_out(dx1, f1, z1, w1_out, vec(p["ffn1_post_g"]), name="ffn1_bwd_out")
    grad_x, g["ffn1_pre_g"], g["ffn1_w_in"] = _norm_matmul_bwd(dz1, x, dx1, vec(p["ffn1_pre_g"]), w1_in, name="ffn1_bwd_in")

    g["c_re"], g["c_im"] = (jnp.swapaxes(_diag_blocks(m, STATE, GROUP_CH), 1, 2) for m in (dcd_re, dcd_im))
    dbb = [jnp.swapaxes(_diag_blocks(m, GROUP_CH, STATE), 1, 2).reshape(N_STATE, GROUP_CH) for m in (dbd_re, dbd_im)]
    dab = [jnp.sum(da[n], axis=0).reshape(N_STATE, 1) for n in range(2)]
    da_re, da_im, dldt, db_re, db_im = _ssm_discretize_bwd(
        sv["a_re"], sv["a_im"], sv["log_dt"], sv["b_re"], sv["b_im"], sv["ab_re"], sv["ab_im"], sv["e_re"], sv["e_im"],
        dab[0], dab[1], dbb[0], dbb[1])
    g["a_re"], g["a_im"] = da_re.reshape(N_GROUPS, STATE), da_im.reshape(N_GROUPS, STATE)
    g["log_dt"] = jnp.sum(dldt.reshape(N_GROUPS, STATE), axis=-1)
    g["b_re"], g["b_im"] = (t.reshape(N_GROUPS, STATE, GROUP_CH) for t in (db_re, db_im))

    g["ffn1_w_out"] = dw1_out.reshape(w["ffn1_w_out"].shape)
    g["ffn2_w_out"] = dw2_out.reshape(w["ffn2_w_out"].shape)
    g["w_glu"] = dw_glu.reshape(w["w_glu"].shape)
    g["w_mix_out"] = dw_mo.reshape(w["w_mix_out"].shape)
    return loss_rows, grad_x, g


MESH = pl.DeviceIdType.MESH
N_REL = 3


def _place():
    x, y, c = lax.axis_index("x"), lax.axis_index("y"), lax.axis_index("c")
    return x, y, c, [(1 - x, y), (x, 1 - y), (1 - x, 1 - y)]


def _remote(src, dst, send_sems, recv_sems, idx, to):
    return pltpu.make_async_remote_copy(src_ref=src, dst_ref=dst, send_sem=send_sems.at[idx], recv_sem=recv_sems.at[idx],
                                        device_id=to, device_id_type=MESH)


def _half(rows, who):
    return pl.ds(who * (rows // 2), rows // 2)


def _gather_weights(shards):
    n = len(shards)

    def body(*refs):
        ins, outs = refs[:n], refs[n:2 * n]
        send_sems, recv_sems, local_sems = refs[2 * n:]
        x, y, c, chips = _place()
        me, sibling = 2 * x + y, (x, y, 1 - c)
        local = [pltpu.make_async_copy(ins[i], outs[i].at[me], local_sems.at[i]) for i in range(n)]
        for cp in local:
            cp.start()
        sent = []
        for i in range(n):
            mine = _half(shards[i].shape[0], c)
            for k, chip in enumerate(chips):
                sent.append(_remote(ins[i].at[mine], outs[i].at[me, mine], send_sems, recv_sems, (i, k), (*chip, c)))
                sent[-1].start()
        for i in range(n):
            mine = _half(shards[i].shape[0], c)
            for k, (px, py) in enumerate(chips):
                landed = outs[i].at[2 * px + py, mine]
                _remote(landed, landed, send_sems, recv_sems, (i, k), (px, py, c)).wait_recv()
                sent.append(_remote(landed, landed, send_sems, recv_sems, (i, N_REL + k), sibling))
                sent[-1].start()
        for i in range(n):
            theirs = _half(shards[i].shape[0], 1 - c)
            for k, (px, py) in enumerate(chips):
                landed = outs[i].at[2 * px + py, theirs]
                _remote(landed, landed, send_sems, recv_sems, (i, N_REL + k), sibling).wait_recv()
        for cp in sent:
            cp.wait_send()
        for cp in local:
            cp.wait()

    return pl.pallas_call(
        body, name="gather_weights", in_specs=[ANY] * n, out_specs=[ANY] * n,
        out_shape=[jax.ShapeDtypeStruct((N_CHIPS,) + s.shape, s.dtype) for s in shards],
        scratch_shapes=[pltpu.SemaphoreType.DMA((n, 2 * N_REL)), pltpu.SemaphoreType.DMA((n, 2 * N_REL)),
                        pltpu.SemaphoreType.DMA((n,))],
    )(*shards)


def _pair_exchange(grads):
    n = len(grads)

    def body(*refs):
        ins, outs = refs[:n], refs[n:2 * n]
        send_sems, recv_sems = refs[2 * n:]
        x, y, c, _ = _place()
        copies = []
        for i in range(n):
            theirs = _half(grads[i].shape[1], 1 - c)
            copies.append(_remote(ins[i].at[:, theirs], outs[i], send_sems, recv_sems, i, (x, y, 1 - c)))
            copies[-1].start()
        for cp in copies:
            cp.wait()

    return pl.pallas_call(
        body, name="pair_exchange", in_specs=[ANY] * n, out_specs=[ANY] * n,
        out_shape=[jax.ShapeDtypeStruct((N_CHIPS, g.shape[1] // 2, g.shape[2]), g.dtype) for g in grads],
        scratch_shapes=[pltpu.SemaphoreType.DMA((n,)), pltpu.SemaphoreType.DMA((n,))],
    )(*grads)


def _chip_scatter(sums):
    n = len(sums)

    def body(*refs):
        ins, outs = refs[:n], refs[n:2 * n]
        send_sems, recv_sems = refs[2 * n:]
        x, y, c, chips = _place()
        copies = []
        for i in range(n):
            for k, (px, py) in enumerate(chips):
                copies.append(_remote(ins[i].at[2 * px + py], outs[i].at[k], send_sems, recv_sems, (i, k), (px, py, c)))
                copies[-1].start()
        for cp in copies:
            cp.wait()

    return pl.pallas_call(
        body, name="chip_scatter", in_specs=[ANY] * n, out_specs=[ANY] * n,
        out_shape=[jax.ShapeDtypeStruct((N_REL,) + s.shape[1:], s.dtype) for s in sums],
        scratch_shapes=[pltpu.SemaphoreType.DMA((n, N_REL)), pltpu.SemaphoreType.DMA((n, N_REL))],
    )(*sums)


def _pair_gather(halves):
    n = len(halves)

    def body(*refs):
        ins, outs = refs[:n], refs[n:2 * n]
        send_sems, recv_sems, local_sems = refs[2 * n:]
        x, y, c, _ = _place()
        copies = []
        for i in range(n):
            mine = _half(2 * halves[i].shape[0], c)
            copies.append(pltpu.make_async_copy(ins[i], outs[i].at[mine], local_sems.at[i]))
            copies.append(_remote(ins[i], outs[i].at[mine], send_sems, recv_sems, i, (x, y, 1 - c)))
        for cp in copies:
            cp.start()
        for cp in copies:
            cp.wait()

    return pl.pallas_call(
        body, name="pair_gather", in_specs=[ANY] * n, out_specs=[ANY] * n,
        out_shape=[jax.ShapeDtypeStruct((2 * h.shape[0], h.shape[1]), h.dtype) for h in halves],
        scratch_shapes=[pltpu.SemaphoreType.DMA((n,)), pltpu.SemaphoreType.DMA((n,)), pltpu.SemaphoreType.DMA((n,))],
    )(*halves)


def _allreduce_small(packed):
    rows = packed.shape[0]
    n_dev = 2 * N_CHIPS

    def body(x_ref, o_ref, buf, send_sems, recv_sems):
        x, y, c, chips = _place()
        sibling = (x, y, 1 - c)

        def slot(px, py, pc):
            return buf.at[4 * px + 2 * py + pc]

        buf[4 * x + 2 * y + c] = x_ref[...]
        first = [_remote(x_ref, slot(x, y, c), send_sems, recv_sems, 0, sibling)]
        first += [_remote(x_ref, slot(x, y, c), send_sems, recv_sems, 1 + k, (*chip, c)) for k, chip in enumerate(chips)]
        for cp in first:
            cp.start()
        passed = []
        for k, chip in enumerate(chips):
            landed = slot(*chip, c)
            _remote(landed, landed, send_sems, recv_sems, 1 + k, (*chip, c)).wait_recv()
            passed.append(_remote(landed, landed, send_sems, recv_sems, 1 + N_REL + k, sibling))
            passed[-1].start()
        _remote(slot(*sibling), slot(*sibling), send_sems, recv_sems, 0, sibling).wait_recv()
        for k, chip in enumerate(chips):
            landed = slot(*chip, 1 - c)
            _remote(landed, landed, send_sems, recv_sems, 1 + N_REL + k, sibling).wait_recv()
        for cp in first + passed:
            cp.wait_send()
        total = buf[0]
        for d in range(1, n_dev):
            total = total + buf[d]
        o_ref[...] = total

    vm = pl.BlockSpec(memory_space=pltpu.VMEM)
    return pl.pallas_call(
        body, name="allreduce_small", in_specs=[vm], out_specs=vm, out_shape=jax.ShapeDtypeStruct(packed.shape, F32),
        scratch_shapes=[pltpu.VMEM((n_dev, rows, 128), F32), pltpu.SemaphoreType.DMA((1 + 2 * N_REL,)),
                        pltpu.SemaphoreType.DMA((1 + 2 * N_REL,))],
    )(packed)


def _row_tile(rows, cap=256):
    return max(t for t in range(8, cap + 1, 8) if rows % t == 0)


def _pair_sum(grad, got, c, name):
    _, half, cols = got.shape
    tr = _row_tile(half)
    nt = half // tr

    def body(c_ref, g_ref, r_ref, o_ref):
        o_ref[...] = g_ref[...] + r_ref[...]

    blk = (1, tr, cols)
    return pl.pallas_call(
        body, name=name,
        grid_spec=pltpu.PrefetchScalarGridSpec(
            num_scalar_prefetch=1, grid=(N_CHIPS, nt),
            in_specs=[pl.BlockSpec(blk, lambda j, t, c_ref: (j, c_ref[0] * nt + t, 0)),
                      pl.BlockSpec(blk, lambda j, t, c_ref: (j, t, 0))],
            out_specs=pl.BlockSpec(blk, lambda j, t, c_ref: (j, t, 0))),
        out_shape=jax.ShapeDtypeStruct(got.shape, F32), compiler_params=_params("parallel", "parallel"),
    )(c, grad, got)


def _chip_sum(sums, got, chip, name):
    _, half, cols = got.shape
    tr = _row_tile(half)

    def body(chip_ref, s_ref, a_ref, b_ref, c_ref, o_ref):
        o_ref[...] = ((s_ref[0] + a_ref[0]) + b_ref[0]) + c_ref[0]

    blk = (1, tr, cols)
    other = lambda k: pl.BlockSpec(blk, lambda t, chip_ref: (k, t, 0))
    return pl.pallas_call(
        body, name=name,
        grid_spec=pltpu.PrefetchScalarGridSpec(
            num_scalar_prefetch=1, grid=(half // tr,),
            in_specs=[pl.BlockSpec(blk, lambda t, chip_ref: (chip_ref[0], t, 0)), other(0), other(1), other(2)],
            out_specs=pl.BlockSpec((tr, cols), lambda t, chip_ref: (t, 0))),
        out_shape=jax.ShapeDtypeStruct((half, cols), F32), compiler_params=_params("parallel"),
    )(chip, sums, got, got, got)


def _adamw(w, g, m, v, name):
    rows, cols = w.shape
    tr = _row_tile(rows) if rows % 8 == 0 else rows

    def body(w_ref, g_ref, m_ref, v_ref, d_ref, mo_ref, vo_ref):
        gv = g_ref[...]
        m2 = ADAM_B1 * m_ref[...] + (1.0 - ADAM_B1) * gv
        v2 = ADAM_B2 * v_ref[...] + (1.0 - ADAM_B2) * (gv * gv)
        mo_ref[...] = m2
        vo_ref[...] = v2
        m_hat = m2 / (1.0 - ADAM_B1 ** ADAM_STEP)
        v_hat = v2 / (1.0 - ADAM_B2 ** ADAM_STEP)
        d_ref[...] = -ADAM_LR * (m_hat / (jnp.sqrt(v_hat) + ADAM_EPS) + ADAM_WD * w_ref[...])

    blk = _rows(tr, cols)
    return pl.pallas_call(
        body, name=name, grid=(rows // tr,), in_specs=[blk] * 4, out_specs=[blk] * 3,
        out_shape=[jax.ShapeDtypeStruct(w.shape, F32)] * 3, compiler_params=_params("parallel"),
    )(w, g, m, v)


def _pack(parts):
    flat = []
    for t in parts:
        t = t.reshape(-1).astype(F32)
        flat.append(jnp.pad(t, (0, -t.shape[0] % 128)))
    flat = jnp.concatenate(flat)
    return jnp.pad(flat, (0, -flat.shape[0] % 1024)).reshape(-1, 128)


def _unpack(buf, shapes):
    flat, out, at = buf.reshape(-1), [], 0
    for s in shapes:
        size = math.prod(s)
        out.append(flat[at:at + size].reshape(s))
        at += size + (-size % 128)
    return out


def kernel(x, ffn1_pre_g, ffn1_w_in, ffn1_w_out, ffn1_post_g, mix_pre_g, w_mix_in, a_re, a_im, log_dt, b_re, b_im, c_re, c_im, d_skip, w_glu, b_glu, w_mix_out, mix_post_g, ffn2_pre_g, ffn2_w_in, ffn2_w_out, ffn2_post_g, loss_target, m_ffn1_pre_g, m_ffn1_w_in, m_ffn1_w_out, m_ffn1_post_g, m_mix_pre_g, m_w_mix_in, m_a_re, m_a_im, m_log_dt, m_b_re, m_b_im, m_c_re, m_c_im, m_d_skip, m_w_glu, m_b_glu, m_w_mix_out, m_mix_post_g, m_ffn2_pre_g, m_ffn2_w_in, m_ffn2_w_out, m_ffn2_post_g, v_ffn1_pre_g, v_ffn1_w_in, v_ffn1_w_out, v_ffn1_post_g, v_mix_pre_g, v_w_mix_in, v_a_re, v_a_im, v_log_dt, v_b_re, v_b_im, v_c_re, v_c_im, v_d_skip, v_w_glu, v_b_glu, v_w_mix_out, v_mix_post_g, v_ffn2_pre_g, v_ffn2_w_in, v_ffn2_w_out, v_ffn2_post_g):
    given = dict(locals())
    order = ("ffn1_pre_g", "ffn1_w_in", "ffn1_w_out", "ffn1_post_g", "mix_pre_g", "w_mix_in", "a_re", "a_im", "log_dt",
             "b_re", "b_im", "c_re", "c_im", "d_skip", "w_glu", "b_glu", "w_mix_out", "mix_post_g", "ffn2_pre_g",
             "ffn2_w_in", "ffn2_w_out", "ffn2_post_g")
    chip = (2 * lax.axis_index("x") + lax.axis_index("y")).astype(jnp.int32).reshape(1)
    core = lax.axis_index("c").astype(jnp.int32).reshape(1)

    shards = [given[n][0] for n in BIG]
    gathered = _gather_weights([s.astype(BF16) for s in shards])
    small = {n: given[n][0] for n in SMALL}
    loss_rows, grad_x, g = _local_step(x[0], loss_target[0], small, dict(zip(BIG, gathered)))

    local = [g[n] for n in BIG]
    from_sibling = _pair_exchange(local)
    pair = [_pair_sum(a, b, core, name=f"pair_sum_{n}") for n, a, b in zip(BIG, local, from_sibling)]
    from_chips = _chip_scatter(pair)
    halves = [_chip_sum(a, b, chip, name=f"chip_sum_{n}") for n, a, b in zip(BIG, pair, from_chips)]
    grads = dict(zip(BIG, _pair_gather(halves)))

    total = _allreduce_small(_pack([g[n] for n in SMALL] + [loss_rows[0, :1]]))
    parts = _unpack(total, [small[n].shape for n in SMALL] + [(1,)])
    grads.update(zip(SMALL, parts[:-1]))
    loss = parts[-1][0]

    delta, new_m, new_v = {}, {}, {}
    for n, w in zip(BIG, shards):
        delta[n], new_m[n], new_v[n] = _adamw(w, grads[n], given["m_" + n][0], given["v_" + n][0], name=f"adamw_{n}")
    packed = [_pack([given[pre + n] for n in SMALL]) for pre in ("", "m_", "v_")]
    outs = _adamw(packed[0], _pack([grads[n] for n in SMALL]), packed[1], packed[2], name="adamw_small")
    for store, buf in zip((delta, new_m, new_v), outs):
        store.update(zip(SMALL, _unpack(buf, [small[n].shape for n in SMALL])))

    lead = lambda d: [d[n][None] for n in order]
    return (loss, grad_x[None], *lead(grads), *lead(delta), *lead(new_m), *lead(new_v))
```

```python
import functools
import math

import jax
import jax.numpy as jnp
from jax import lax
from jax.experimental import pallas as pl
from jax.experimental.pallas import tpu as pltpu

F32, BF16 = jnp.float32, jnp.bfloat16

D_MODEL = 1024
D_FF = 2816
N_CHIPS = 4
FF_BLK = 2 * D_FF // N_CHIPS
ATTN_W = 512
SSM_W = 512
HEAD_DIM = 64
N_HEADS = ATTN_W // HEAD_DIM
DILATIONS = (1, 4, 16)
N_BACK = 128
QBLK = 128
N_GROUPS = 32
GROUP_CH = 16
STATE = 64
N_STATE = N_GROUPS * STATE
EPS = 1e-6
NEG = -1e30
GELU_C = math.sqrt(2.0 / math.pi)

ADAM_LR, ADAM_B1, ADAM_B2, ADAM_EPS, ADAM_WD, ADAM_STEP = 0.001, 0.9, 0.999, 1e-08, 0.01, 10

VMEM_LIMIT_V7X = 60 * 1024 * 1024
ROW_TILE = 256


def _params(*sem):
    return pltpu.CompilerParams(dimension_semantics=sem, vmem_limit_bytes=VMEM_LIMIT_V7X)


def _dot(a, b):
    return jnp.dot(a.astype(BF16), b.astype(BF16), preferred_element_type=F32)


def _dot_nt(a, b):
    return lax.dot_general(a.astype(BF16), b.astype(BF16), (((1,), (1,)), ((), ())), preferred_element_type=F32)


def _dot_tn(a, b):
    return lax.dot_general(a.astype(BF16), b.astype(BF16), (((0,), (0,)), ((), ())), preferred_element_type=F32)


def _full(shape):
    return pl.BlockSpec(shape, lambda *_: (0,) * len(shape))


def _rows(tm, width):
    return pl.BlockSpec((tm, width), lambda i: (i, 0))


ANY = pl.BlockSpec(memory_space=pl.ANY)


def _load_once(pairs, sems):
    copies = [pltpu.make_async_copy(src, dst, sems.at[k]) for k, (src, dst) in enumerate(pairs)]
    for c in copies:
        c.start()
    for c in copies:
        c.wait()


def _rms(x):
    return lax.rsqrt(jnp.mean(x * x, axis=-1, keepdims=True) + EPS)


def _rms_bwd(dy_g, xn, r):
    return r * (dy_g - xn * jnp.mean(dy_g * xn, axis=-1, keepdims=True))


def _ffn_fwd(x, g_pre, w_in, w_out, g_post, target, *, name):
    T = x.shape[0]
    tm = ROW_TILE
    with_loss = target is not None

    def body(*refs):
        if with_loss:
            x_ref, gpre_ref, gpost_ref, tgt_ref, win_hbm, wout_hbm, o_ref, loss_ref, z_ref, f_ref, win_v, wout_v, sems = refs
        else:
            x_ref, gpre_ref, gpost_ref, win_hbm, wout_hbm, o_ref, z_ref, f_ref, win_v, wout_v, sems = refs

        @pl.when(pl.program_id(0) == 0)
        def _():
            _load_once([(win_hbm, win_v), (wout_hbm, wout_v)], sems)
            if with_loss:
                loss_ref[...] = jnp.zeros_like(loss_ref)

        xv = x_ref[...]
        h = (xv * _rms(xv) * gpre_ref[...]).astype(BF16)
        f = jnp.zeros((tm, D_MODEL), F32)
        for k in range(2):
            gate = _dot(h, win_v[k])
            up = _dot(h, win_v[k + 2])
            z_ref[:, k * FF_BLK:(k + 1) * FF_BLK] = gate.astype(BF16)
            z_ref[:, D_FF + k * FF_BLK:D_FF + (k + 1) * FF_BLK] = up.astype(BF16)
            f = f + _dot(gate * jax.nn.sigmoid(gate) * up, wout_v[k])
        f_ref[...] = f
        out = xv + 0.5 * (f * _rms(f) * gpost_ref[...])
        if with_loss:
            err = out - tgt_ref[...]
            o_ref[...] = err * (1.0 / D_MODEL)
            loss_ref[...] += jnp.sum(err * err) * (0.5 / D_MODEL)
        else:
            o_ref[...] = out

    row = _rows(tm, D_MODEL)
    vec = _full((1, D_MODEL))
    in_specs = [row, vec, vec] + ([row] if with_loss else []) + [ANY, ANY]
    out_shape = [jax.ShapeDtypeStruct((T, D_MODEL), F32)]
    out_specs = [row]
    if with_loss:
        out_shape.append(jax.ShapeDtypeStruct((8, 128), F32))
        out_specs.append(_full((8, 128)))
    out_shape += [jax.ShapeDtypeStruct((T, 2 * D_FF), BF16), jax.ShapeDtypeStruct((T, D_MODEL), F32)]
    out_specs += [_rows(tm, 2 * D_FF), row]
    args = (x, g_pre, g_post) + ((target,) if with_loss else ()) + (w_in, w_out)
    return pl.pallas_call(
        body, name=name, grid=(T // tm,), in_specs=in_specs, out_specs=out_specs, out_shape=out_shape,
        scratch_shapes=[pltpu.VMEM(w_in.shape, BF16), pltpu.VMEM(w_out.shape, BF16), pltpu.SemaphoreType.DMA((2,))],
        compiler_params=_params("arbitrary"),
    )(*args)


def _ffn_bwd_out(dout, f, z, w_out, g_post, *, name):
    T = dout.shape[0]
    tm = ROW_TILE
    nt = T // tm

    def body(dout_ref, f_ref, z_ref, gpost_ref, wout_hbm, dz_ref, dgpost_ref, dwout_hbm, wout_v, dwout_v, sems):
        i = pl.program_id(0)

        @pl.when(i == 0)
        def _():
            _load_once([(wout_hbm, wout_v)], sems)
            dwout_v[...] = jnp.zeros_like(dwout_v)
            dgpost_ref[...] = jnp.zeros_like(dgpost_ref)

        dy = 0.5 * dout_ref[...]
        f = f_ref[...]
        r = _rms(f)
        fn = f * r
        dgpost_ref[...] += jnp.sum(dy * fn, axis=0, keepdims=True)
        df = _rms_bwd(dy * gpost_ref[...], fn, r).astype(BF16)
        for k in range(2):
            gate = z_ref[:, k * FF_BLK:(k + 1) * FF_BLK].astype(F32)
            up = z_ref[:, D_FF + k * FF_BLK:D_FF + (k + 1) * FF_BLK].astype(F32)
            sg = jax.nn.sigmoid(gate)
            silu = gate * sg
            dwout_v[k] += _dot_tn(silu * up, df)
            da = _dot_nt(df, wout_v[k])
            dz_ref[:, k * FF_BLK:(k + 1) * FF_BLK] = (da * up * (sg * (1.0 + gate * (1.0 - sg)))).astype(BF16)
            dz_ref[:, D_FF + k * FF_BLK:D_FF + (k + 1) * FF_BLK] = (da * silu).astype(BF16)

        @pl.when(i == nt - 1)
        def _():
            c = pltpu.make_async_copy(dwout_v, dwout_hbm, sems.at[0])
            c.start()
            c.wait()

    row = _rows(tm, D_MODEL)
    return pl.pallas_call(
        body, name=name, grid=(nt,),
        in_specs=[row, row, _rows(tm, 2 * D_FF), _full((1, D_MODEL)), ANY],
        out_specs=[_rows(tm, 2 * D_FF), _full((1, D_MODEL)), ANY],
        out_shape=[jax.ShapeDtypeStruct((T, 2 * D_FF), BF16), jax.ShapeDtypeStruct((1, D_MODEL), F32),
                   jax.ShapeDtypeStruct(w_out.shape, F32)],
        scratch_shapes=[pltpu.VMEM(w_out.shape, BF16), pltpu.VMEM(w_out.shape, F32), pltpu.SemaphoreType.DMA((1,))],
        compiler_params=_params("arbitrary"),
    )(dout, f, z, g_post, w_out)


def _norm_matmul_bwd(dz, x, dres, g, w, *, name):
    T = x.shape[0]
    nb, _, bw = w.shape
    tm = ROW_TILE
    nt = T // tm

    def body(dz_ref, x_ref, dres_ref, g_ref, w_hbm, dx_ref, dg_ref, dw_hbm, w_v, dw_v, sems):
        i = pl.program_id(0)

        @pl.when(i == 0)
        def _():
            _load_once([(w_hbm, w_v)], sems)
            dw_v[...] = jnp.zeros_like(dw_v)
            dg_ref[...] = jnp.zeros_like(dg_ref)

        xv = x_ref[...]
        r = _rms(xv)
        xn = xv * r
        gv = g_ref[...]
        h = (xn * gv).astype(BF16)
        dh = jnp.zeros((tm, D_MODEL), F32)
        for j in range(nb):
            dzj = dz_ref[:, j * bw:(j + 1) * bw]
            dw_v[j] += _dot_tn(h, dzj)
            dh = dh + _dot_nt(dzj, w_v[j])
        dg_ref[...] += jnp.sum(dh * xn, axis=0, keepdims=True)
        dx_ref[...] = _rms_bwd(dh * gv, xn, r) + dres_ref[...]

        @pl.when(i == nt - 1)
        def _():
            c = pltpu.make_async_copy(dw_v, dw_hbm, sems.at[0])
            c.start()
            c.wait()

    row = _rows(tm, D_MODEL)
    return pl.pallas_call(
        body, name=name, grid=(nt,),
        in_specs=[_rows(tm, nb * bw), row, row, _full((1, D_MODEL)), ANY],
        out_specs=[row, _full((1, D_MODEL)), ANY],
        out_shape=[jax.ShapeDtypeStruct((T, D_MODEL), F32), jax.ShapeDtypeStruct((1, D_MODEL), F32),
                   jax.ShapeDtypeStruct(w.shape, F32)],
        scratch_shapes=[pltpu.VMEM(w.shape, BF16), pltpu.VMEM(w.shape, F32), pltpu.SemaphoreType.DMA((1,))],
        compiler_params=_params("arbitrary"),
    )(dz, x, dres, g, w)


def _mix_in_fwd(x, g, w):
    T = x.shape[0]
    tm = ROW_TILE

    def body(x_ref, g_ref, w_ref, q_ref, k_ref, v_ref, u_ref):
        xv = x_ref[...]
        h = (xv * _rms(xv) * g_ref[...]).astype(BF16)
        for j, o_ref in enumerate((q_ref, k_ref, v_ref, u_ref)):
            o_ref[...] = _dot(h, w_ref[j])

    col = _rows(tm, ATTN_W)
    return pl.pallas_call(
        body, name="mix_in_fwd", grid=(T // tm,),
        in_specs=[_rows(tm, D_MODEL), _full((1, D_MODEL)), _full(w.shape)],
        out_specs=[col] * 4, out_shape=[jax.ShapeDtypeStruct((T, ATTN_W), F32)] * 4,
        compiler_params=_params("parallel"),
    )(x, g, w)


def _mix_out_fwd(x, attn, ssm, w, g):
    T = x.shape[0]
    tm = ROW_TILE

    def body(x_ref, a_ref, s_ref, w_ref, g_ref, o_ref, m_ref):
        mixed = _dot(a_ref[...], w_ref[0]) + _dot(s_ref[...], w_ref[1])
        m_ref[...] = mixed
        o_ref[...] = x_ref[...] + mixed * _rms(mixed) * g_ref[...]

    row, col = _rows(tm, D_MODEL), _rows(tm, ATTN_W)
    return pl.pallas_call(
        body, name="mix_out_fwd", grid=(T // tm,),
        in_specs=[row, col, col, _full(w.shape), _full((1, D_MODEL))],
        out_specs=[row, row], out_shape=[jax.ShapeDtypeStruct((T, D_MODEL), F32)] * 2,
        compiler_params=_params("parallel"),
    )(x, attn, ssm, w, g)


def _mix_out_bwd(dout, mixed, attn, ssm, w, g):
    T = dout.shape[0]
    tm = ROW_TILE

    def body(dout_ref, m_ref, a_ref, s_ref, w_ref, g_ref, da_ref, ds_ref, dw_ref, dg_ref):
        @pl.when(pl.program_id(0) == 0)
        def _():
            dw_ref[...] = jnp.zeros_like(dw_ref)
            dg_ref[...] = jnp.zeros_like(dg_ref)

        dy = dout_ref[...]
        mixed = m_ref[...]
        r = _rms(mixed)
        mn = mixed * r
        dg_ref[...] += jnp.sum(dy * mn, axis=0, keepdims=True)
        dm = _rms_bwd(dy * g_ref[...], mn, r).astype(BF16)
        da_ref[...] = _dot_nt(dm, w_ref[0])
        ds_ref[...] = _dot_nt(dm, w_ref[1])
        dw_ref[0] += _dot_tn(a_ref[...], dm)
        dw_ref[1] += _dot_tn(s_ref[...], dm)

    row, col = _rows(tm, D_MODEL), _rows(tm, ATTN_W)
    return pl.pallas_call(
        body, name="mix_out_bwd", grid=(T // tm,),
        in_specs=[row, row, col, col, _full(w.shape), _full((1, D_MODEL))],
        out_specs=[col, col, _full(w.shape), _full((1, D_MODEL))],
        out_shape=[jax.ShapeDtypeStruct((T, ATTN_W), F32)] * 2
        + [jax.ShapeDtypeStruct(w.shape, F32), jax.ShapeDtypeStruct((1, D_MODEL), F32)],
        compiler_params=_params("arbitrary"),
    )(dout, mixed, attn, ssm, w, g)


PAIR_W = 2 * HEAD_DIM


def _class_rows(d, r):
    return (pl.ds(r, QBLK, stride=d), slice(None)) if d > 1 else (slice(None), slice(None))


def _for_each_class(d, fn):
    if d == 1:
        fn(0)
    else:
        lax.fori_loop(0, d, lambda r, carry: (fn(r), carry)[1], 0)


def _pair_slopes(pair):
    return [jnp.exp2(-jnp.full((1, 1), 2 * pair + hh + 1, jnp.int32).astype(F32)) for hh in range(2)]


def _attn_specs(d, nb):
    blk = (QBLK * d, PAIR_W)
    cur = pl.BlockSpec(blk, lambda j, hp: (j, hp))
    prev = pl.BlockSpec(blk, lambda j, hp: (jnp.maximum(j - 1, 0), hp))
    nxt = pl.BlockSpec(blk, lambda j, hp: (jnp.minimum(j + 1, nb - 1), hp))
    return cur, prev, nxt


def _attn_branch_fwd(q, k, v, d):
    T = q.shape[0]
    nb = T // (d * QBLK)
    scale = HEAD_DIM ** -0.5

    def body(q_ref, kc_ref, kp_ref, vc_ref, vp_ref, o_ref, l_ref, q_s, kk_s, vv_s, o_s, l_s):
        j = pl.program_id(0)
        qi = lax.broadcasted_iota(jnp.int32, (QBLK, 2 * QBLK), 0)
        ci = lax.broadcasted_iota(jnp.int32, (QBLK, 2 * QBLK), 1)
        steps = QBLK + qi - ci
        valid = (steps >= 0) & (steps <= N_BACK) & ((ci >= QBLK) | (j > 0))
        dist = (steps * d).astype(F32)
        slopes = _pair_slopes(pl.program_id(1))

        def one_class(r):
            rows = _class_rows(d, r)
            q_s[...] = q_ref[rows]
            kk_s[:QBLK], kk_s[QBLK:] = kp_ref[rows], kc_ref[rows]
            vv_s[:QBLK], vv_s[QBLK:] = vp_ref[rows], vc_ref[rows]
            for hh in range(2):
                sl = slice(hh * HEAD_DIM, (hh + 1) * HEAD_DIM)
                s = _dot_nt(q_s[:, sl], kk_s[:, sl]) * scale - slopes[hh] * dist
                s = jnp.where(valid, s, NEG)
                m = jnp.max(s, axis=-1, keepdims=True)
                p = jnp.exp(s - m)
                den = jnp.sum(p, axis=-1, keepdims=True)
                o_s[:, sl] = _dot(p, vv_s[:, sl]) / den
                l_s[:, sl] = jnp.broadcast_to(m + jnp.log(den), (QBLK, HEAD_DIM))
            o_ref[rows] = o_s[...]
            l_ref[rows] = l_s[...]

        _for_each_class(d, one_class)

    cur, prev, _ = _attn_specs(d, nb)
    shape = jax.ShapeDtypeStruct((T, ATTN_W), F32)
    one, two = pltpu.VMEM((QBLK, PAIR_W), F32), pltpu.VMEM((2 * QBLK, PAIR_W), F32)
    return pl.pallas_call(
        body, name=f"attn_fwd_d{d}", grid=(nb, ATTN_W // PAIR_W),
        in_specs=[cur, cur, prev, cur, prev], out_specs=[cur, cur], out_shape=[shape, shape],
        scratch_shapes=[one, two, two, one, one], compiler_params=_params("parallel", "parallel"),
    )(q, k, k, v, v)


def _attn_merge(outs, lses):
    T = outs[0].shape[0]
    tm = 512

    def body(o1, o2, o3, l1, l2, l3, a_ref, lse_ref):
        ls = [l1[...], l2[...], l3[...]]
        m = jnp.maximum(jnp.maximum(ls[0], ls[1]), ls[2])
        lse = m + jnp.log(jnp.exp(ls[0] - m) + jnp.exp(ls[1] - m) + jnp.exp(ls[2] - m))
        lse_ref[...] = lse
        a_ref[...] = jnp.exp(ls[0] - lse) * o1[...] + jnp.exp(ls[1] - lse) * o2[...] + jnp.exp(ls[2] - lse) * o3[...]

    col = _rows(tm, ATTN_W)
    return pl.pallas_call(
        body, name="attn_merge", grid=(T // tm,), in_specs=[col] * 6, out_specs=[col, col],
        out_shape=[jax.ShapeDtypeStruct((T, ATTN_W), F32)] * 2, compiler_params=_params("parallel"),
    )(*outs, *lses)


def _attn_branch_bwd(q, k, v, o, lse, do, d):
    T = q.shape[0]
    nb = T // (d * QBLK)
    scale = HEAD_DIM ** -0.5

    def body(qc_ref, qn_ref, kc_ref, kp_ref, vc_ref, vp_ref, oc_ref, on_ref, lc_ref, ln_ref, doc_ref, don_ref,
             dq_ref, dk_ref, dv_ref, qq_s, kk_s, vv_s, oo_s, ll_s, doo_s, dq_s, dk_s, dv_s):
        j = pl.program_id(0)
        qi = lax.broadcasted_iota(jnp.int32, (QBLK, 2 * QBLK), 0)
        ci = lax.broadcasted_iota(jnp.int32, (QBLK, 2 * QBLK), 1)
        steps_q = QBLK + qi - ci
        valid_q = (steps_q >= 0) & (steps_q <= N_BACK) & ((ci >= QBLK) | (j > 0))
        dist_q = (steps_q * d).astype(F32)
        ri = lax.broadcasted_iota(jnp.int32, (2 * QBLK, QBLK), 0)
        ki = lax.broadcasted_iota(jnp.int32, (2 * QBLK, QBLK), 1)
        steps_k = ri - ki
        valid_k = (steps_k >= 0) & (steps_k <= N_BACK) & ((ri < QBLK) | (j < nb - 1))
        dist_k = (steps_k * d).astype(F32)
        lo, hi = slice(0, QBLK), slice(QBLK, 2 * QBLK)
        slopes = _pair_slopes(pl.program_id(1))

        def one_class(r):
            rows = _class_rows(d, r)
            qq_s[lo], qq_s[hi] = qc_ref[rows], qn_ref[rows]
            oo_s[lo], oo_s[hi] = oc_ref[rows], on_ref[rows]
            ll_s[lo], ll_s[hi] = lc_ref[rows], ln_ref[rows]
            doo_s[lo], doo_s[hi] = doc_ref[rows], don_ref[rows]
            kk_s[lo], kk_s[hi] = kp_ref[rows], kc_ref[rows]
            vv_s[lo], vv_s[hi] = vp_ref[rows], vc_ref[rows]
            for hh in range(2):
                sl = slice(hh * HEAD_DIM, (hh + 1) * HEAD_DIM)
                slope = slopes[hh]
                qq, doo, kk, vv = qq_s[:, sl], doo_s[:, sl], kk_s[:, sl], vv_s[:, sl]
                lse2 = ll_s[:, hh * HEAD_DIM:hh * HEAD_DIM + 1]
                delta2 = jnp.sum(doo * oo_s[:, sl], axis=-1, keepdims=True)

                s = jnp.where(valid_q, _dot_nt(qq[lo], kk) * scale - slope * dist_q, NEG)
                p = jnp.exp(s - lse2[lo])
                ds = p * (_dot_nt(doo[lo], vv) - delta2[lo])
                dq_s[:, sl] = _dot(ds, kk) * scale

                s2 = jnp.where(valid_k, _dot_nt(qq, kk[hi]) * scale - slope * dist_k, NEG)
                p2 = jnp.exp(s2 - lse2)
                dv_s[:, sl] = _dot_tn(p2, doo)
                ds2 = p2 * (_dot_nt(doo, vv[hi]) - delta2)
                dk_s[:, sl] = _dot_tn(ds2, qq) * scale
            dq_ref[rows] = dq_s[...]
            dk_ref[rows] = dk_s[...]
            dv_ref[rows] = dv_s[...]

        _for_each_class(d, one_class)

    cur, prev, nxt = _attn_specs(d, nb)
    shape = jax.ShapeDtypeStruct((T, ATTN_W), F32)
    one, two = pltpu.VMEM((QBLK, PAIR_W), F32), pltpu.VMEM((2 * QBLK, PAIR_W), F32)
    return pl.pallas_call(
        body, name=f"attn_bwd_d{d}", grid=(nb, ATTN_W // PAIR_W),
        in_specs=[cur, nxt, cur, prev, cur, prev, cur, nxt, cur, nxt, cur, nxt],
        out_specs=[cur] * 3, out_shape=[shape] * 3, scratch_shapes=[two] * 6 + [one] * 3,
        compiler_params=_params("parallel", "parallel"),
    )(q, q, k, k, v, v, o, o, lse, lse, do, do)


def _dproj_merge(dqs, dks, dvs, du):
    T = du.shape[0]
    tm = 512

    def body(*refs):
        o_ref = refs[-1]
        for part in range(3):
            a, b, c = refs[3 * part:3 * part + 3]
            o_ref[:, part * ATTN_W:(part + 1) * ATTN_W] = (a[...] + b[...] + c[...]).astype(BF16)
        o_ref[:, 3 * ATTN_W:] = refs[9][...].astype(BF16)

    col = _rows(tm, ATTN_W)
    return pl.pallas_call(
        body, name="dproj_merge", grid=(T // tm,), in_specs=[col] * 10, out_specs=_rows(tm, 4 * ATTN_W),
        out_shape=jax.ShapeDtypeStruct((T, 4 * ATTN_W), BF16), compiler_params=_params("parallel"),
    )(*dqs, *dks, *dvs, du)


def _attention_fwd(q, k, v):
    res = [_attn_branch_fwd(q, k, v, d) for d in DILATIONS]
    return _attn_merge([r[0] for r in res], [r[1] for r in res])


def _attention_bwd(q, k, v, attn, lse, dattn):
    res = [_attn_branch_bwd(q, k, v, attn, lse, dattn, d) for d in DILATIONS]
    return [r[0] for r in res], [r[1] for r in res], [r[2] for r in res]


SCAN_ROWS = 8
SCAN_LANES = 512
SSM_CHUNK = 256


def _cmul(ar, ai, br, bi):
    return ar * br - ai * bi, ar * bi + ai * br


def _ssm_discretize(a_re, a_im, log_dt, b_re, b_im):
    def body(ar_ref, ai_ref, ldt_ref, br_ref, bi_ref, abr_ref, abi_ref, er_ref, ei_ref, bbr_ref, bbi_ref, pr_ref, pi_ref):
        ar, ai = ar_ref[...], ai_ref[...]
        dt = jnp.exp(ldt_ref[...])
        n = lax.broadcasted_iota(jnp.int32, (1, SCAN_ROWS), 1).astype(F32) + 1.0
        mag, ang = jnp.exp(dt * ar), dt * ai
        abr, abi = mag * jnp.cos(ang), mag * jnp.sin(ang)
        abr_ref[...], abi_ref[...] = abr, abi
        pr_ref[...] = jnp.exp(dt * ar * n) * jnp.cos(ang * n)
        pi_ref[...] = jnp.exp(dt * ar * n) * jnp.sin(ang * n)
        den = ar * ar + ai * ai
        er = ((abr - 1.0) * ar + abi * ai) / den
        ei = (abi * ar - (abr - 1.0) * ai) / den
        er_ref[...], ei_ref[...] = er, ei
        bbr_ref[...], bbi_ref[...] = _cmul(er, ei, br_ref[...], bi_ref[...])

    col = jax.ShapeDtypeStruct((N_STATE, 1), F32)
    mat = jax.ShapeDtypeStruct((N_STATE, GROUP_CH), F32)
    pw = jax.ShapeDtypeStruct((N_STATE, SCAN_ROWS), F32)
    return pl.pallas_call(body, name="ssm_discretize", out_shape=[col] * 4 + [mat] * 2 + [pw] * 2)(
        a_re, a_im, log_dt, b_re, b_im)


def _ssm_discretize_bwd(a_re, a_im, log_dt, b_re, b_im, ab_re, ab_im, e_re, e_im, dab_re, dab_im, dbb_re, dbb_im):
    def body(ar_ref, ai_ref, ldt_ref, br_ref, bi_ref, abr_ref, abi_ref, er_ref, ei_ref, dabr_ref, dabi_ref,
             dbbr_ref, dbbi_ref, dar_ref, dai_ref, ddt_ref, dbr_ref, dbi_ref):
        ar, ai, dt = ar_ref[...], ai_ref[...], jnp.exp(ldt_ref[...])
        er, ei = er_ref[...], ei_ref[...]
        gbr, gbi = dbbr_ref[...], dbbi_ref[...]
        dbr_ref[...], dbi_ref[...] = _cmul(er, -ei, gbr, gbi)
        br, bi = br_ref[...], bi_ref[...]
        der = jnp.sum(br * gbr + bi * gbi, axis=-1, keepdims=True)
        dei = jnp.sum(br * gbi - bi * gbr, axis=-1, keepdims=True)
        den = ar * ar + ai * ai
        inv_r, inv_i = ar / den, -ai / den
        t_r, t_i = _cmul(der, dei, inv_r, -inv_i)
        gab_r, gab_i = dabr_ref[...] + t_r, dabi_ref[...] + t_i
        q_r, q_i = _cmul(er, ei, inv_r, inv_i)
        dl_r, dl_i = _cmul(der, dei, q_r, -q_i)
        dl_r, dl_i = -dl_r, -dl_i
        gw_r, gw_i = _cmul(gab_r, gab_i, abr_ref[...], -abi_ref[...])
        dar_ref[...] = dl_r + dt * gw_r
        dai_ref[...] = dl_i + dt * gw_i
        ddt_ref[...] = (gw_r * ar + gw_i * ai) * dt

    col = jax.ShapeDtypeStruct((N_STATE, 1), F32)
    mat = jax.ShapeDtypeStruct((N_STATE, GROUP_CH), F32)
    return pl.pallas_call(body, name="ssm_discretize_bwd", out_shape=[col] * 3 + [mat] * 2)(
        a_re, a_im, log_dt, b_re, b_im, ab_re, ab_im, e_re, e_im, dab_re, dab_im, dbb_re, dbb_im)


def _scan_tables(p_re, p_im, reverse):
    pr, pi = p_re.T, p_im.T
    if reverse:
        pi = -pi
    bc = lambda t, n: jnp.broadcast_to(t[n - 1], (SCAN_ROWS, N_STATE))
    carry = (pr[::-1], pi[::-1]) if reverse else (pr, pi)
    return jnp.stack([bc(pr, 1), bc(pi, 1), bc(pr, 2), bc(pi, 2), bc(pr, 4), bc(pi, 4), carry[0], carry[1]])


def _scan_group(xr, xi, tab_ref, ls, carry_r, carry_i, reverse):
    row = lax.broadcasted_iota(jnp.int32, (SCAN_ROWS, SCAN_LANES), 0)
    for n, s in enumerate((1, 2, 4)):
        if reverse:
            keep = row < SCAN_ROWS - s
            shr, shi = pltpu.roll(xr, SCAN_ROWS - s, 0), pltpu.roll(xi, SCAN_ROWS - s, 0)
        else:
            keep = row >= s
            shr, shi = pltpu.roll(xr, s, 0), pltpu.roll(xi, s, 0)
        shr, shi = jnp.where(keep, shr, 0.0), jnp.where(keep, shi, 0.0)
        mr, mi = _cmul(tab_ref[2 * n, :, ls], tab_ref[2 * n + 1, :, ls], shr, shi)
        xr, xi = xr + mr, xi + mi
    mr, mi = _cmul(tab_ref[6, :, ls], tab_ref[7, :, ls], carry_r, carry_i)
    return xr + mr, xi + mi


def _gelu(y):
    t = jnp.tanh(GELU_C * (y + 0.044715 * y * y * y))
    return 0.5 * y * (1.0 + t), t


def _ssm_fwd(u, tab, bd_re, bd_im, cd_re, cd_im, d_skip, w_glu, b_glu):
    T = u.shape[0]
    tc = SSM_CHUNK

    def body(u_ref, tab_ref, bdr_ref, bdi_ref, cdr_ref, cdi_ref, dsk_ref, wg_ref, bg_ref,
             sr_ref, si_ref, yp_ref, o_ref, car_r, car_i):
        @pl.when(pl.program_id(0) == 0)
        def _():
            car_r[...] = jnp.zeros_like(car_r)
            car_i[...] = jnp.zeros_like(car_i)

        uv = u_ref[...]
        sr_ref[...] = _dot(uv, bdr_ref[...])
        si_ref[...] = _dot(uv, bdi_ref[...])
        for lb in range(N_STATE // SCAN_LANES):
            ls = pl.ds(lb * SCAN_LANES, SCAN_LANES)

            def step(g, carry):
                rows = pl.ds(pl.multiple_of(g * SCAN_ROWS, SCAN_ROWS), SCAN_ROWS)
                xr, xi = _scan_group(sr_ref[rows, ls], si_ref[rows, ls], tab_ref, ls, carry[0], carry[1], False)
                sr_ref[rows, ls] = xr
                si_ref[rows, ls] = xi
                last = slice(SCAN_ROWS - 1, SCAN_ROWS)
                return (jnp.broadcast_to(xr[last], xr.shape), jnp.broadcast_to(xi[last], xi.shape))

            cr, ci = lax.fori_loop(0, tc // SCAN_ROWS, step, (car_r[:, ls], car_i[:, ls]))
            car_r[:, ls] = cr
            car_i[:, ls] = ci
        y = _dot(sr_ref[...], cdr_ref[...]) - _dot(si_ref[...], cdi_ref[...]) + dsk_ref[...] * uv
        yp_ref[...] = y
        gy, _ = _gelu(y)
        o_ref[...] = gy * jax.nn.sigmoid(_dot(gy, wg_ref[...]) + bg_ref[...])

    col, st = _rows(tc, SSM_W), _rows(tc, N_STATE)
    vec = _full((1, SSM_W))
    return pl.pallas_call(
        body, name="ssm_fwd", grid=(T // tc,),
        in_specs=[col, _full(tab.shape), _full(bd_re.shape), _full(bd_im.shape), _full(cd_re.shape), _full(cd_im.shape),
                  vec, _full(w_glu.shape), vec],
        out_specs=[st, st, col, col],
        out_shape=[jax.ShapeDtypeStruct((T, N_STATE), F32)] * 2 + [jax.ShapeDtypeStruct((T, SSM_W), F32)] * 2,
        scratch_shapes=[pltpu.VMEM((SCAN_ROWS, N_STATE), F32)] * 2,
        compiler_params=_params("arbitrary"),
    )(u, tab, bd_re, bd_im, cd_re, cd_im, d_skip, w_glu, b_glu)


def _ssm_bwd(dout, u, yp, s_re, s_im, tab, bd_re, bd_im, cd_re, cd_im, d_skip, w_glu, b_glu):
    T = u.shape[0]
    tc = SSM_CHUNK
    nt = T // tc
    rows_per_chunk = tc // SCAN_ROWS

    def body(do_ref, u_ref, yp_ref, sr_ref, si_ref, pr_ref, pi_ref, tab_ref, dsk_ref, wg_ref, bg_ref,
             bdr_hbm, bdi_hbm, cdr_hbm, cdi_hbm,
             du_ref, dsk_out, dbg_out, dwg_out, da_out, dbdr_hbm, dbdi_hbm, dcdr_hbm, dcdi_hbm,
             bdr_v, bdi_v, cdr_v, cdi_v, dbdr_v, dbdi_v, dcdr_v, dcdi_v, gr_v, gi_v, car_r, car_i, sems):
        i = pl.program_id(0)

        @pl.when(i == 0)
        def _():
            _load_once([(bdr_hbm, bdr_v), (bdi_hbm, bdi_v), (cdr_hbm, cdr_v), (cdi_hbm, cdi_v)], sems)
            for ref in (dbdr_v, dbdi_v, dcdr_v, dcdi_v, car_r, car_i, dsk_out, dbg_out, dwg_out, da_out):
                ref[...] = jnp.zeros_like(ref)

        uv, y, dout_v = u_ref[...], yp_ref[...], do_ref[...]
        gy, t = _gelu(y)
        sg = jax.nn.sigmoid(_dot(gy, wg_ref[...]) + bg_ref[...])
        dzg = dout_v * gy * sg * (1.0 - sg)
        dgy = dout_v * sg + _dot_nt(dzg, wg_ref[...])
        dwg_out[...] += _dot_tn(gy, dzg)
        dbg_out[...] += jnp.sum(dzg, axis=0, keepdims=True)
        dy = dgy * (0.5 * (1.0 + t) + 0.5 * y * (1.0 - t * t) * GELU_C * (1.0 + 3 * 0.044715 * y * y))
        dsk_out[...] += jnp.sum(dy * uv, axis=0, keepdims=True)

        gr_v[...] = _dot_nt(dy, cdr_v[...])
        gi_v[...] = -_dot_nt(dy, cdi_v[...])
        dcdr_v[...] += _dot_tn(sr_ref[...], dy)
        dcdi_v[...] -= _dot_tn(si_ref[...], dy)

        row = lax.broadcasted_iota(jnp.int32, (SCAN_ROWS, SCAN_LANES), 0)
        first_chunk = i == nt - 1
        for lb in range(N_STATE // SCAN_LANES):
            ls = pl.ds(lb * SCAN_LANES, SCAN_LANES)

            def step(n, carry):
                g = rows_per_chunk - 1 - n
                rows = pl.ds(pl.multiple_of(g * SCAN_ROWS, SCAN_ROWS), SCAN_ROWS)
                before = pl.ds(pl.multiple_of(jnp.maximum(g - 1, 0) * SCAN_ROWS, SCAN_ROWS), SCAN_ROWS)
                xr, xi = _scan_group(gr_v[rows, ls], gi_v[rows, ls], tab_ref, ls, carry[0], carry[1], True)
                gr_v[rows, ls] = xr
                gi_v[rows, ls] = xi
                last = slice(SCAN_ROWS - 1, SCAN_ROWS)
                edge_r = jnp.where(g > 0, sr_ref[before, ls][last], jnp.where(first_chunk, 0.0, pr_ref[:, ls][last]))
                edge_i = jnp.where(g > 0, si_ref[before, ls][last], jnp.where(first_chunk, 0.0, pi_ref[:, ls][last]))
                spr = jnp.where(row >= 1, pltpu.roll(sr_ref[rows, ls], 1, 0), edge_r)
                spi = jnp.where(row >= 1, pltpu.roll(si_ref[rows, ls], 1, 0), edge_i)
                first = slice(0, 1)
                return (jnp.broadcast_to(xr[first], xr.shape), jnp.broadcast_to(xi[first], xi.shape),
                        carry[2] + xr * spr + xi * spi, carry[3] + xi * spr - xr * spi)

            zero = jnp.zeros((SCAN_ROWS, SCAN_LANES), F32)
            cr, ci, dar, dai = lax.fori_loop(0, rows_per_chunk, step, (car_r[:, ls], car_i[:, ls], zero, zero))
            car_r[:, ls] = cr
            car_i[:, ls] = ci
            da_out[0, :, ls] += dar
            da_out[1, :, ls] += dai

        du_ref[...] = dsk_ref[...] * dy + _dot_nt(gr_v[...], bdr_v[...]) + _dot_nt(gi_v[...], bdi_v[...])
        dbdr_v[...] += _dot_tn(uv, gr_v[...])
        dbdi_v[...] += _dot_tn(uv, gi_v[...])

        @pl.when(i == nt - 1)
        def _():
            outs = [(dbdr_v, dbdr_hbm), (dbdi_v, dbdi_hbm), (dcdr_v, dcdr_hbm), (dcdi_v, dcdi_hbm)]
            copies = [pltpu.make_async_copy(src, dst, sems.at[k]) for k, (src, dst) in enumerate(outs)]
            for c in copies:
                c.start()
            for c in copies:
                c.wait()

    rev = lambda i: (nt - 1 - i, 0)
    col = pl.BlockSpec((tc, SSM_W), rev)
    st = pl.BlockSpec((tc, N_STATE), rev)
    st_before = pl.BlockSpec((SCAN_ROWS, N_STATE), lambda i: (jnp.maximum((nt - 1 - i) * rows_per_chunk - 1, 0), 0))
    vec = _full((1, SSM_W))
    bd = jax.ShapeDtypeStruct(bd_re.shape, F32)
    cd = jax.ShapeDtypeStruct(cd_re.shape, F32)
    return pl.pallas_call(
        body, name="ssm_bwd", grid=(nt,),
        in_specs=[col, col, col, st, st, st_before, st_before, _full(tab.shape), vec, _full(w_glu.shape), vec,
                  ANY, ANY, ANY, ANY],
        out_specs=[col, vec, vec, _full(w_glu.shape), _full((2, SCAN_ROWS, N_STATE)), ANY, ANY, ANY, ANY],
        out_shape=[jax.ShapeDtypeStruct((T, SSM_W), F32), jax.ShapeDtypeStruct((1, SSM_W), F32),
                   jax.ShapeDtypeStruct((1, SSM_W), F32), jax.ShapeDtypeStruct(w_glu.shape, F32),
                   jax.ShapeDtypeStruct((2, SCAN_ROWS, N_STATE), F32), bd, bd, cd, cd],
        scratch_shapes=[pltpu.VMEM(bd_re.shape, BF16)] * 2 + [pltpu.VMEM(cd_re.shape, BF16)] * 2
        + [pltpu.VMEM(bd_re.shape, F32)] * 2 + [pltpu.VMEM(cd_re.shape, F32)] * 2
        + [pltpu.VMEM((tc, N_STATE), F32)] * 2 + [pltpu.VMEM((SCAN_ROWS, N_STATE), F32)] * 2
        + [pltpu.SemaphoreType.DMA((4,))],
        compiler_params=_params("arbitrary"),
    )(dout, u, yp, s_re, s_im, s_re, s_im, tab, d_skip, w_glu, b_glu, bd_re, bd_im, cd_re, cd_im)


def _block_diag(t):
    g, a, b = t.shape
    eye = jnp.eye(N_GROUPS, dtype=t.dtype)
    return (t[:, :, None, :] * eye[:, None, :, None]).reshape(g * a, g * b)


def _diag_blocks(m, a, b):
    eye = jnp.eye(N_GROUPS, dtype=m.dtype)
    return jnp.sum(m.reshape(N_GROUPS, a, N_GROUPS, b) * eye[:, None, :, None], axis=2)


def _ssm_prepare(a_re, a_im, log_dt, b_re, b_im, c_re, c_im):
    col = lambda t: t.reshape(N_STATE, 1)
    ldt = jnp.broadcast_to(log_dt.reshape(N_GROUPS, 1), (N_GROUPS, STATE)).reshape(N_STATE, 1)
    b2r, b2i = b_re.reshape(N_STATE, GROUP_CH), b_im.reshape(N_STATE, GROUP_CH)
    ab_r, ab_i, e_r, e_i, bb_r, bb_i, p_r, p_i = _ssm_discretize(col(a_re), col(a_im), ldt, b2r, b2i)
    bd = [_block_diag(jnp.swapaxes(t.reshape(N_GROUPS, STATE, GROUP_CH), 1, 2)).astype(BF16) for t in (bb_r, bb_i)]
    cd = [_block_diag(jnp.swapaxes(t.reshape(N_GROUPS, GROUP_CH, STATE), 1, 2)).astype(BF16) for t in (c_re, c_im)]
    saved = dict(a_re=col(a_re), a_im=col(a_im), log_dt=ldt, b_re=b2r, b_im=b2i, ab_re=ab_r, ab_im=ab_i, e_re=e_r, e_im=e_i)
    return _scan_tables(p_r, p_i, False), _scan_tables(p_r, p_i, True), bd, cd, saved


BIG = ("ffn1_w_in", "ffn1_w_out", "w_mix_in", "w_glu", "w_mix_out", "ffn2_w_in", "ffn2_w_out")
SMALL = ("ffn1_pre_g", "ffn1_post_g", "mix_pre_g", "a_re", "a_im", "log_dt", "b_re", "b_im", "c_re", "c_im",
         "d_skip", "b_glu", "mix_post_g", "ffn2_pre_g", "ffn2_post_g")


def _local_step(x, target, p, w):
    vec = lambda t: t.reshape(1, -1)
    w1_in, w2_in = w["ffn1_w_in"], w["ffn2_w_in"]
    w1_out, w2_out = (w[n].reshape(2, FF_BLK, D_MODEL) for n in ("ffn1_w_out", "ffn2_w_out"))
    w_mi = w["w_mix_in"]
    w_glu = w["w_glu"].reshape(SSM_W, SSM_W)
    w_mo = w["w_mix_out"].reshape(2, ATTN_W, D_MODEL)

    x1, z1, f1 = _ffn_fwd(x, vec(p["ffn1_pre_g"]), w1_in, w1_out, vec(p["ffn1_post_g"]), None, name="ffn1_fwd")
    q, k, v, u = _mix_in_fwd(x1, vec(p["mix_pre_g"]), w_mi)
    attn, lse = _attention_fwd(q, k, v)
    tab_f, tab_b, bd, cd, sv = _ssm_prepare(p["a_re"], p["a_im"], p["log_dt"], p["b_re"], p["b_im"], p["c_re"], p["c_im"])
    ssm_args = (bd[0], bd[1], cd[0], cd[1], vec(p["d_skip"]), w_glu, vec(p["b_glu"]))
    s_re, s_im, yp, ssm = _ssm_fwd(u, tab_f, *ssm_args)
    x2, mixed = _mix_out_fwd(x1, attn, ssm, w_mo, vec(p["mix_post_g"]))
    dx3, loss_rows, z2, f2 = _ffn_fwd(x2, vec(p["ffn2_pre_g"]), w2_in, w2_out, vec(p["ffn2_post_g"]), target, name="ffn2_fwd")

    g = {}
    dz2, g["ffn2_post_g"], dw2_out = _ffn_bwd_out(dx3, f2, z2, w2_out, vec(p["ffn2_post_g"]), name="ffn2_bwd_out")
    dx2, g["ffn2_pre_g"], g["ffn2_w_in"] = _norm_matmul_bwd(dz2, x2, dx3, vec(p["ffn2_pre_g"]), w2_in, name="ffn2_bwd_in")
    dattn, dssm, dw_mo, g["mix_post_g"] = _mix_out_bwd(dx2, mixed, attn, ssm, w_mo, vec(p["mix_post_g"]))
    du, g["d_skip"], g["b_glu"], dw_glu, da, dbd_re, dbd_im, dcd_re, dcd_im = _ssm_bwd(dssm, u, yp, s_re, s_im, tab_b, *ssm_args)
    dqs, dks, dvs = _attention_bwd(q, k, v, attn, lse, dattn)
    dproj = _dproj_merge(dqs, dks, dvs, du)
    dx1, g["mix_pre_g"], g["w_mix_in"] = _norm_matmul_bwd(dproj, x1, dx2, vec(p["mix_pre_g"]), w_mi, name="mix_bwd_in")
    dz1, g["ffn1_post_g"], dw1_out = _ffn_bwd_out(dx1, f1, z1, w1_out, vec(p["ffn1_post_g"]), name="ffn1_bwd_out")
    grad_x, g["ffn1_pre_g"], g["ffn1_w_in"] = _norm_matmul_bwd(dz1, x, dx1, vec(p["ffn1_pre_g"]), w1_in, name="ffn1_bwd_in")

    g["c_re"], g["c_im"] = (jnp.swapaxes(_diag_blocks(m, STATE, GROUP_CH), 1, 2) for m in (dcd_re, dcd_im))
    dbb = [jnp.swapaxes(_diag_blocks(m, GROUP_CH, STATE), 1, 2).reshape(N_STATE, GROUP_CH) for m in (dbd_re, dbd_im)]
    dab = [jnp.sum(da[n], axis=0).reshape(N_STATE, 1) for n in range(2)]
    da_re, da_im, dldt, db_re, db_im = _ssm_discretize_bwd(
        sv["a_re"], sv["a_im"], sv["log_dt"], sv["b_re"], sv["b_im"], sv["ab_re"], sv["ab_im"], sv["e_re"], sv["e_im"],
        dab[0], dab[1], dbb[0], dbb[1])
    g["a_re"], g["a_im"] = da_re.reshape(N_GROUPS, STATE), da_im.reshape(N_GROUPS, STATE)
    g["log_dt"] = jnp.sum(dldt.reshape(N_GROUPS, STATE), axis=-1)
    g["b_re"], g["b_im"] = (t.reshape(N_GROUPS, STATE, GROUP_CH) for t in (db_re, db_im))

    g["ffn1_w_out"] = dw1_out.reshape(w["ffn1_w_out"].shape)
    g["ffn2_w_out"] = dw2_out.reshape(w["ffn2_w_out"].shape)
    g["w_glu"] = dw_glu.reshape(w["w_glu"].shape)
    g["w_mix_out"] = dw_mo.reshape(w["w_mix_out"].shape)
    return loss_rows, grad_x, g


MESH = pl.DeviceIdType.MESH
N_REL = 3


def _place():
    x, y, c = lax.axis_index("x"), lax.axis_index("y"), lax.axis_index("c")
    return x, y, c, [(1 - x, y), (x, 1 - y), (1 - x, 1 - y)]


def _remote(src, dst, send_sems, recv_sems, idx, to):
    return pltpu.make_async_remote_copy(src_ref=src, dst_ref=dst, send_sem=send_sems.at[idx], recv_sem=recv_sems.at[idx],
                                        device_id=to, device_id_type=MESH)


def _half(rows, who):
    return pl.ds(who * (rows // 2), rows // 2)


class _Copies:
    def __init__(self, send_sems, recv_sems, local_sems):
        self.send_sems, self.recv_sems, self.local_sems = send_sems, recv_sems, local_sems
        self.n_remote = self.n_local = 0

    def remote(self, src, dst, to):
        k, self.n_remote = self.n_remote, self.n_remote + 1
        return pltpu.make_async_remote_copy(src_ref=src, dst_ref=dst, send_sem=self.send_sems.at[k],
                                            recv_sem=self.recv_sems.at[k], device_id=to, device_id_type=MESH)

    def local(self, src, dst):
        k, self.n_local = self.n_local, self.n_local + 1
        return pltpu.make_async_copy(src, dst, self.local_sems.at[k])


def _exchange(name, plan, ins, out_shapes, n_remote, n_local=1):
    n_in = len(ins)

    def body(*refs):
        in_refs, out_refs = refs[:n_in], refs[n_in:-3]
        for phase in plan(in_refs, out_refs, _Copies(*refs[-3:])):
            for cp in phase:
                cp.start()
            for cp in phase:
                cp.wait()

    return pl.pallas_call(
        body, name=name, in_specs=[ANY] * n_in, out_specs=[ANY] * len(out_shapes), out_shape=out_shapes,
        scratch_shapes=[pltpu.SemaphoreType.DMA((n_remote,)), pltpu.SemaphoreType.DMA((n_remote,)),
                        pltpu.SemaphoreType.DMA((n_local,))],
    )(*ins)


def _gather_weights(shards, name):
    def plan(ins, outs, mk):
        x, y, c = lax.axis_index("x"), lax.axis_index("y"), lax.axis_index("c")
        me, sibling = 2 * x + y, (x, y, 1 - c)
        x_nb, y_nb, diag = (1 - x, y), (x, 1 - y), (1 - x, 1 - y)
        index = lambda chip: 2 * chip[0] + chip[1]
        first, second, third = [], [], []
        for i, shard in enumerate(shards):
            rows = shard.shape[0]
            mine = _half(rows, c)
            quarter = lambda which: pl.ds(c * (rows // 2) + which * (rows // 4), rows // 4)
            first.append(mk.local(ins[i], outs[i].at[me]))
            for nb in (x_nb, y_nb):
                first.append(mk.remote(ins[i].at[mine], outs[i].at[me, mine], (*nb, c)))
            for nb in (x_nb, y_nb):
                landed = outs[i].at[index(nb), mine]
                second.append(mk.remote(landed, landed, sibling))
            for nb, other, which in ((x_nb, y_nb, 0), (y_nb, x_nb, 1)):
                landed = outs[i].at[index(nb), quarter(which)]
                second.append(mk.remote(landed, landed, (*other, c)))
            landed = outs[i].at[index(diag), mine]
            third.append(mk.remote(landed, landed, sibling))
        return [first, second, third]

    n = len(shards)
    return _exchange(name, plan, shards, [jax.ShapeDtypeStruct((N_CHIPS,) + s.shape, s.dtype) for s in shards],
                     n_remote=7 * n, n_local=n)


def _pair_exchange(grads):
    def plan(ins, outs, mk):
        x, y, c = lax.axis_index("x"), lax.axis_index("y"), lax.axis_index("c")
        return [[mk.remote(ins[i].at[:, _half(g.shape[1], 1 - c)], outs[i], (x, y, 1 - c)) for i, g in enumerate(grads)]]

    return _exchange("pair_exchange", plan, grads,
                     [jax.ShapeDtypeStruct((N_CHIPS, g.shape[1] // 2, g.shape[2]), g.dtype) for g in grads], len(grads))


def _reduce_first(pair):
    def plan(ins, outs, mk):
        x, y, c = lax.axis_index("x"), lax.axis_index("y"), lax.axis_index("c")
        phase = []
        for i, p in enumerate(pair):
            q = p.shape[1] // 2
            phase.append(mk.remote(ins[i].at[pl.ds(2 * (1 - x), 2), pl.ds(0, q)], outs[i].at[0], (1 - x, y, c)))
            for jx in range(2):
                phase.append(mk.remote(ins[i].at[2 * jx + 1 - y, pl.ds(q, q)], outs[i].at[1, jx], (x, 1 - y, c)))
        return [phase]

    return _exchange("reduce_first", plan, pair,
                     [jax.ShapeDtypeStruct((2, 2, p.shape[1] // 2, p.shape[2]), p.dtype) for p in pair], 3 * len(pair))


def _reduce_second(sums):
    def plan(ins, outs, mk):
        x, y, c = lax.axis_index("x"), lax.axis_index("y"), lax.axis_index("c")
        phase = []
        for i in range(len(sums)):
            phase.append(mk.remote(ins[i].at[0, 1 - y], outs[i].at[0], (x, 1 - y, c)))
            phase.append(mk.remote(ins[i].at[1, 1 - x], outs[i].at[1], (1 - x, y, c)))
        return [phase]

    return _exchange("reduce_second", plan, sums,
                     [jax.ShapeDtypeStruct((2,) + s.shape[2:], s.dtype) for s in sums], 2 * len(sums))


def _pair_gather(halves):
    def plan(ins, outs, mk):
        x, y, c = lax.axis_index("x"), lax.axis_index("y"), lax.axis_index("c")
        phase = []
        for i, h in enumerate(halves):
            mine = _half(2 * h.shape[0], c)
            phase.append(mk.local(ins[i], outs[i].at[mine]))
            phase.append(mk.remote(ins[i], outs[i].at[mine], (x, y, 1 - c)))
        return [phase]

    n = len(halves)
    return _exchange("pair_gather", plan, halves,
                     [jax.ShapeDtypeStruct((2 * h.shape[0], h.shape[1]), h.dtype) for h in halves], n, n)


def _allreduce_small(packed):
    rows = packed.shape[0]
    n_dev = 2 * N_CHIPS

    def body(x_ref, o_ref, buf, send_sems, recv_sems):
        x, y, c, chips = _place()
        sibling = (x, y, 1 - c)

        def slot(px, py, pc):
            return buf.at[4 * px + 2 * py + pc]

        buf[4 * x + 2 * y + c] = x_ref[...]
        first = [_remote(x_ref, slot(x, y, c), send_sems, recv_sems, 0, sibling)]
        first += [_remote(x_ref, slot(x, y, c), send_sems, recv_sems, 1 + k, (*chip, c)) for k, chip in enumerate(chips)]
        for cp in first:
            cp.start()
        passed = []
        for k, chip in enumerate(chips):
            landed = slot(*chip, c)
            _remote(landed, landed, send_sems, recv_sems, 1 + k, (*chip, c)).wait_recv()
            passed.append(_remote(landed, landed, send_sems, recv_sems, 1 + N_REL + k, sibling))
            passed[-1].start()
        _remote(slot(*sibling), slot(*sibling), send_sems, recv_sems, 0, sibling).wait_recv()
        for k, chip in enumerate(chips):
            landed = slot(*chip, 1 - c)
            _remote(landed, landed, send_sems, recv_sems, 1 + N_REL + k, sibling).wait_recv()
        for cp in first + passed:
            cp.wait_send()
        total = buf[0]
        for d in range(1, n_dev):
            total = total + buf[d]
        o_ref[...] = total

    vm = pl.BlockSpec(memory_space=pltpu.VMEM)
    return pl.pallas_call(
        body, name="allreduce_small", in_specs=[vm], out_specs=vm, out_shape=jax.ShapeDtypeStruct(packed.shape, F32),
        scratch_shapes=[pltpu.VMEM((n_dev, rows, 128), F32), pltpu.SemaphoreType.DMA((1 + 2 * N_REL,)),
                        pltpu.SemaphoreType.DMA((1 + 2 * N_REL,))],
    )(packed)


def _row_tile(rows, cap=256):
    return max(t for t in range(8, cap + 1, 8) if rows % t == 0)


def _pair_sum(grad, got, c, name):
    _, half, cols = got.shape
    tr = _row_tile(half)
    nt = half // tr

    def body(c_ref, g_ref, r_ref, o_ref):
        o_ref[...] = g_ref[...] + r_ref[...]

    blk = (1, tr, cols)
    return pl.pallas_call(
        body, name=name,
        grid_spec=pltpu.PrefetchScalarGridSpec(
            num_scalar_prefetch=1, grid=(N_CHIPS, nt),
            in_specs=[pl.BlockSpec(blk, lambda j, t, c_ref: (j, c_ref[0] * nt + t, 0)),
                      pl.BlockSpec(blk, lambda j, t, c_ref: (j, t, 0))],
            out_specs=pl.BlockSpec(blk, lambda j, t, c_ref: (j, t, 0))),
        out_shape=jax.ShapeDtypeStruct(got.shape, F32), compiler_params=_params("parallel", "parallel"),
    )(c, grad, got)


def _reduce_sum_first(pair, got, sel, name):
    _, _, q, cols = got.shape
    tr = _row_tile(q)
    nt = q // tr

    def body(sel_ref, p_ref, r_ref, o_ref):
        o_ref[0, 0] = p_ref[0] + r_ref[0, 0]

    blk = (1, 1, tr, cols)
    return pl.pallas_call(
        body, name=name,
        grid_spec=pltpu.PrefetchScalarGridSpec(
            num_scalar_prefetch=1, grid=(2, 2, nt),
            in_specs=[pl.BlockSpec((1, tr, cols), lambda p, k, t, s: (s[2 * p] + s[2 * p + 1] * k, p * nt + t, 0)),
                      pl.BlockSpec(blk, lambda p, k, t, s: (p, k, t, 0))],
            out_specs=pl.BlockSpec(blk, lambda p, k, t, s: (p, k, t, 0))),
        out_shape=jax.ShapeDtypeStruct(got.shape, F32), compiler_params=_params("parallel", "parallel", "parallel"),
    )(sel, pair, got)


def _reduce_sum_second(sums, got, sel, name):
    _, q, cols = got.shape
    tr = _row_tile(q)

    def body(sel_ref, s_ref, r_ref, o_ref):
        o_ref[0] = s_ref[0, 0] + r_ref[0]

    blk = (1, tr, cols)
    return pl.pallas_call(
        body, name=name,
        grid_spec=pltpu.PrefetchScalarGridSpec(
            num_scalar_prefetch=1, grid=(2, q // tr),
            in_specs=[pl.BlockSpec((1, 1, tr, cols), lambda p, t, s: (p, s[p], t, 0)),
                      pl.BlockSpec(blk, lambda p, t, s: (p, t, 0))],
            out_specs=pl.BlockSpec(blk, lambda p, t, s: (p, t, 0))),
        out_shape=jax.ShapeDtypeStruct(got.shape, F32), compiler_params=_params("parallel", "parallel"),
    )(sel, sums, got)


def _adamw(w, g, m, v, name):
    rows, cols = w.shape
    tr = _row_tile(rows) if rows % 8 == 0 else rows

    def body(w_ref, g_ref, m_ref, v_ref, d_ref, mo_ref, vo_ref):
        gv = g_ref[...]
        m2 = ADAM_B1 * m_ref[...] + (1.0 - ADAM_B1) * gv
        v2 = ADAM_B2 * v_ref[...] + (1.0 - ADAM_B2) * (gv * gv)
        mo_ref[...] = m2
        vo_ref[...] = v2
        m_hat = m2 / (1.0 - ADAM_B1 ** ADAM_STEP)
        v_hat = v2 / (1.0 - ADAM_B2 ** ADAM_STEP)
        d_ref[...] = -ADAM_LR * (m_hat / (jnp.sqrt(v_hat) + ADAM_EPS) + ADAM_WD * w_ref[...])

    blk = _rows(tr, cols)
    return pl.pallas_call(
        body, name=name, grid=(rows // tr,), in_specs=[blk] * 4, out_specs=[blk] * 3,
        out_shape=[jax.ShapeDtypeStruct(w.shape, F32)] * 3, compiler_params=_params("parallel"),
    )(w, g, m, v)


def _pack(parts):
    flat = []
    for t in parts:
        t = t.reshape(-1).astype(F32)
        flat.append(jnp.pad(t, (0, -t.shape[0] % 128)))
    flat = jnp.concatenate(flat)
    return jnp.pad(flat, (0, -flat.shape[0] % 1024)).reshape(-1, 128)


def _unpack(buf, shapes):
    flat, out, at = buf.reshape(-1), [], 0
    for s in shapes:
        size = math.prod(s)
        out.append(flat[at:at + size].reshape(s))
        at += size + (-size % 128)
    return out


def kernel(x, ffn1_pre_g, ffn1_w_in, ffn1_w_out, ffn1_post_g, mix_pre_g, w_mix_in, a_re, a_im, log_dt, b_re, b_im, c_re, c_im, d_skip, w_glu, b_glu, w_mix_out, mix_post_g, ffn2_pre_g, ffn2_w_in, ffn2_w_out, ffn2_post_g, loss_target, m_ffn1_pre_g, m_ffn1_w_in, m_ffn1_w_out, m_ffn1_post_g, m_mix_pre_g, m_w_mix_in, m_a_re, m_a_im, m_log_dt, m_b_re, m_b_im, m_c_re, m_c_im, m_d_skip, m_w_glu, m_b_glu, m_w_mix_out, m_mix_post_g, m_ffn2_pre_g, m_ffn2_w_in, m_ffn2_w_out, m_ffn2_post_g, v_ffn1_pre_g, v_ffn1_w_in, v_ffn1_w_out, v_ffn1_post_g, v_mix_pre_g, v_w_mix_in, v_a_re, v_a_im, v_log_dt, v_b_re, v_b_im, v_c_re, v_c_im, v_d_skip, v_w_glu, v_b_glu, v_w_mix_out, v_mix_post_g, v_ffn2_pre_g, v_ffn2_w_in, v_ffn2_w_out, v_ffn2_post_g):
    given = dict(locals())
    order = ("ffn1_pre_g", "ffn1_w_in", "ffn1_w_out", "ffn1_post_g", "mix_pre_g", "w_mix_in", "a_re", "a_im", "log_dt",
             "b_re", "b_im", "c_re", "c_im", "d_skip", "w_glu", "b_glu", "w_mix_out", "mix_post_g", "ffn2_pre_g",
             "ffn2_w_in", "ffn2_w_out", "ffn2_post_g")
    at_x, at_y, at_c = (lax.axis_index(a).astype(jnp.int32) for a in ("x", "y", "c"))
    core = at_c.reshape(1)
    sel_first = jnp.stack([2 * at_x, jnp.int32(1), at_y, jnp.int32(2)])
    sel_second = jnp.stack([at_y, at_x])

    shards = [given[n][0] for n in BIG]
    gathered = _gather_weights([s.astype(BF16) for s in shards], name="gather_weights")
    small = {n: given[n][0] for n in SMALL}
    loss_rows, grad_x, g = _local_step(x[0], loss_target[0], small, dict(zip(BIG, gathered)))

    local = [g[n] for n in BIG]
    pair = [_pair_sum(a, b, core, name=f"pair_sum_{n}") for n, a, b in zip(BIG, local, _pair_exchange(local))]
    sums = [_reduce_sum_first(a, b, sel_first, name=f"sum_first_{n}") for n, a, b in zip(BIG, pair, _reduce_first(pair))]
    halves = [_reduce_sum_second(a, b, sel_second, name=f"sum_second_{n}").reshape(a.shape[2] * 2, a.shape[3])
              for n, a, b in zip(BIG, sums, _reduce_second(sums))]
    grads = dict(zip(BIG, _pair_gather(halves)))

    total = _allreduce_small(_pack([g[n] for n in SMALL] + [loss_rows[0, :1]]))
    parts = _unpack(total, [small[n].shape for n in SMALL] + [(1,)])
    grads.update(zip(SMALL, parts[:-1]))
    loss = parts[-1][0]

    delta, new_m, new_v = {}, {}, {}
    for n, w in zip(BIG, shards):
        delta[n], new_m[n], new_v[n] = _adamw(w, grads[n], given["m_" + n][0], given["v_" + n][0], name=f"adamw_{n}")
    packed = [_pack([given[pre + n] for n in SMALL]) for pre in ("", "m_", "v_")]
    outs = _adamw(packed[0], _pack([grads[n] for n in SMALL]), packed[1], packed[2], name="adamw_small")
    for store, buf in zip((delta, new_m, new_v), outs):
        store.update(zip(SMALL, _unpack(buf, [small[n].shape for n in SMALL])))

    lead = lambda d: [d[n][None] for n in order]
    return (loss, grad_x[None], *lead(grads), *lead(delta), *lead(new_m), *lead(new_v))
```

```python
import functools
import math

import jax
import jax.numpy as jnp
from jax import lax
from jax.experimental import pallas as pl
from jax.experimental.pallas import tpu as pltpu

F32, BF16 = jnp.float32, jnp.bfloat16

D_MODEL = 1024
D_FF = 2816
N_CHIPS = 4
FF_BLK = 2 * D_FF // N_CHIPS
ATTN_W = 512
SSM_W = 512
HEAD_DIM = 64
N_HEADS = ATTN_W // HEAD_DIM
DILATIONS = (1, 4, 16)
N_BACK = 128
QBLK = 128
N_GROUPS = 32
GROUP_CH = 16
STATE = 64
N_STATE = N_GROUPS * STATE
EPS = 1e-6
NEG = -1e30
GELU_C = math.sqrt(2.0 / math.pi)

ADAM_LR, ADAM_B1, ADAM_B2, ADAM_EPS, ADAM_WD, ADAM_STEP = 0.001, 0.9, 0.999, 1e-08, 0.01, 10

VMEM_LIMIT_V7X = 60 * 1024 * 1024
ROW_TILE = 256


def _params(*sem):
    return pltpu.CompilerParams(dimension_semantics=sem, vmem_limit_bytes=VMEM_LIMIT_V7X)


def _dot(a, b):
    return jnp.dot(a.astype(BF16), b.astype(BF16), preferred_element_type=F32)


def _dot_nt(a, b):
    return lax.dot_general(a.astype(BF16), b.astype(BF16), (((1,), (1,)), ((), ())), preferred_element_type=F32)


def _dot_tn(a, b):
    return lax.dot_general(a.astype(BF16), b.astype(BF16), (((0,), (0,)), ((), ())), preferred_element_type=F32)


def _full(shape):
    return pl.BlockSpec(shape, lambda *_: (0,) * len(shape))


def _rows(tm, width):
    return pl.BlockSpec((tm, width), lambda i: (i, 0))


ANY = pl.BlockSpec(memory_space=pl.ANY)


def _load_once(pairs, sems):
    copies = [pltpu.make_async_copy(src, dst, sems.at[k]) for k, (src, dst) in enumerate(pairs)]
    for c in copies:
        c.start()
    for c in copies:
        c.wait()


def _rms(x):
    return lax.rsqrt(jnp.mean(x * x, axis=-1, keepdims=True) + EPS)


def _rms_bwd(dy_g, xn, r):
    return r * (dy_g - xn * jnp.mean(dy_g * xn, axis=-1, keepdims=True))


def _ffn_fwd(x, g_pre, w_in, w_out, g_post, target, *, name):
    T = x.shape[0]
    tm = ROW_TILE
    with_loss = target is not None

    def body(*refs):
        if with_loss:
            x_ref, gpre_ref, gpost_ref, tgt_ref, win_hbm, wout_hbm, o_ref, loss_ref, z_ref, f_ref, win_v, wout_v, sems = refs
        else:
            x_ref, gpre_ref, gpost_ref, win_hbm, wout_hbm, o_ref, z_ref, f_ref, win_v, wout_v, sems = refs

        @pl.when(pl.program_id(0) == 0)
        def _():
            _load_once([(win_hbm, win_v), (wout_hbm, wout_v)], sems)
            if with_loss:
                loss_ref[...] = jnp.zeros_like(loss_ref)

        xv = x_ref[...]
        h = (xv * _rms(xv) * gpre_ref[...]).astype(BF16)
        f = jnp.zeros((tm, D_MODEL), F32)
        for k in range(2):
            gate = _dot(h, win_v[k])
            up = _dot(h, win_v[k + 2])
            z_ref[:, k * FF_BLK:(k + 1) * FF_BLK] = gate.astype(BF16)
            z_ref[:, D_FF + k * FF_BLK:D_FF + (k + 1) * FF_BLK] = up.astype(BF16)
            f = f + _dot(gate * jax.nn.sigmoid(gate) * up, wout_v[k])
        f_ref[...] = f
        out = xv + 0.5 * (f * _rms(f) * gpost_ref[...])
        if with_loss:
            err = out - tgt_ref[...]
            o_ref[...] = err * (1.0 / D_MODEL)
            loss_ref[...] += jnp.sum(err * err) * (0.5 / D_MODEL)
        else:
            o_ref[...] = out

    row = _rows(tm, D_MODEL)
    vec = _full((1, D_MODEL))
    in_specs = [row, vec, vec] + ([row] if with_loss else []) + [ANY, ANY]
    out_shape = [jax.ShapeDtypeStruct((T, D_MODEL), F32)]
    out_specs = [row]
    if with_loss:
        out_shape.append(jax.ShapeDtypeStruct((8, 128), F32))
        out_specs.append(_full((8, 128)))
    out_shape += [jax.ShapeDtypeStruct((T, 2 * D_FF), BF16), jax.ShapeDtypeStruct((T, D_MODEL), F32)]
    out_specs += [_rows(tm, 2 * D_FF), row]
    args = (x, g_pre, g_post) + ((target,) if with_loss else ()) + (w_in, w_out)
    return pl.pallas_call(
        body, name=name, grid=(T // tm,), in_specs=in_specs, out_specs=out_specs, out_shape=out_shape,
        scratch_shapes=[pltpu.VMEM(w_in.shape, BF16), pltpu.VMEM(w_out.shape, BF16), pltpu.SemaphoreType.DMA((2,))],
        compiler_params=_params("arbitrary"),
    )(*args)


def _ffn_bwd_out(dout, f, z, w_out, g_post, *, name):
    T = dout.shape[0]
    tm = ROW_TILE
    nt = T // tm

    def body(dout_ref, f_ref, z_ref, gpost_ref, wout_hbm, dz_ref, dgpost_ref, dwout_hbm, wout_v, dwout_v, sems):
        i = pl.program_id(0)

        @pl.when(i == 0)
        def _():
            _load_once([(wout_hbm, wout_v)], sems)
            dwout_v[...] = jnp.zeros_like(dwout_v)
            dgpost_ref[...] = jnp.zeros_like(dgpost_ref)

        dy = 0.5 * dout_ref[...]
        f = f_ref[...]
        r = _rms(f)
        fn = f * r
        dgpost_ref[...] += jnp.sum(dy * fn, axis=0, keepdims=True)
        df = _rms_bwd(dy * gpost_ref[...], fn, r).astype(BF16)
        for k in range(2):
            gate = z_ref[:, k * FF_BLK:(k + 1) * FF_BLK].astype(F32)
            up = z_ref[:, D_FF + k * FF_BLK:D_FF + (k + 1) * FF_BLK].astype(F32)
            sg = jax.nn.sigmoid(gate)
            silu = gate * sg
            dwout_v[k] += _dot_tn(silu * up, df)
            da = _dot_nt(df, wout_v[k])
            dz_ref[:, k * FF_BLK:(k + 1) * FF_BLK] = (da * up * (sg * (1.0 + gate * (1.0 - sg)))).astype(BF16)
            dz_ref[:, D_FF + k * FF_BLK:D_FF + (k + 1) * FF_BLK] = (da * silu).astype(BF16)

        @pl.when(i == nt - 1)
        def _():
            c = pltpu.make_async_copy(dwout_v, dwout_hbm, sems.at[0])
            c.start()
            c.wait()

    row = _rows(tm, D_MODEL)
    return pl.pallas_call(
        body, name=name, grid=(nt,),
        in_specs=[row, row, _rows(tm, 2 * D_FF), _full((1, D_MODEL)), ANY],
        out_specs=[_rows(tm, 2 * D_FF), _full((1, D_MODEL)), ANY],
        out_shape=[jax.ShapeDtypeStruct((T, 2 * D_FF), BF16), jax.ShapeDtypeStruct((1, D_MODEL), F32),
                   jax.ShapeDtypeStruct(w_out.shape, F32)],
        scratch_shapes=[pltpu.VMEM(w_out.shape, BF16), pltpu.VMEM(w_out.shape, F32), pltpu.SemaphoreType.DMA((1,))],
        compiler_params=_params("arbitrary"),
    )(dout, f, z, g_post, w_out)


def _norm_matmul_bwd(dz, x, dres, g, w, *, name):
    T = x.shape[0]
    nb, _, bw = w.shape
    tm = ROW_TILE
    nt = T // tm

    def body(dz_ref, x_ref, dres_ref, g_ref, w_hbm, dx_ref, dg_ref, dw_hbm, w_v, dw_v, sems):
        i = pl.program_id(0)

        @pl.when(i == 0)
        def _():
            _load_once([(w_hbm, w_v)], sems)
            dw_v[...] = jnp.zeros_like(dw_v)
            dg_ref[...] = jnp.zeros_like(dg_ref)

        xv = x_ref[...]
        r = _rms(xv)
        xn = xv * r
        gv = g_ref[...]
        h = (xn * gv).astype(BF16)
        dh = jnp.zeros((tm, D_MODEL), F32)
        for j in range(nb):
            dzj = dz_ref[:, j * bw:(j + 1) * bw]
            dw_v[j] += _dot_tn(h, dzj)
            dh = dh + _dot_nt(dzj, w_v[j])
        dg_ref[...] += jnp.sum(dh * xn, axis=0, keepdims=True)
        dx_ref[...] = _rms_bwd(dh * gv, xn, r) + dres_ref[...]

        @pl.when(i == nt - 1)
        def _():
            c = pltpu.make_async_copy(dw_v, dw_hbm, sems.at[0])
            c.start()
            c.wait()

    row = _rows(tm, D_MODEL)
    return pl.pallas_call(
        body, name=name, grid=(nt,),
        in_specs=[_rows(tm, nb * bw), row, row, _full((1, D_MODEL)), ANY],
        out_specs=[row, _full((1, D_MODEL)), ANY],
        out_shape=[jax.ShapeDtypeStruct((T, D_MODEL), F32), jax.ShapeDtypeStruct((1, D_MODEL), F32),
                   jax.ShapeDtypeStruct(w.shape, F32)],
        scratch_shapes=[pltpu.VMEM(w.shape, BF16), pltpu.VMEM(w.shape, F32), pltpu.SemaphoreType.DMA((1,))],
        compiler_params=_params("arbitrary"),
    )(dz, x, dres, g, w)


def _mix_in_fwd(x, g, w):
    T = x.shape[0]
    tm = ROW_TILE

    def body(x_ref, g_ref, w_ref, q_ref, k_ref, v_ref, u_ref):
        xv = x_ref[...]
        h = (xv * _rms(xv) * g_ref[...]).astype(BF16)
        for j, o_ref in enumerate((q_ref, k_ref, v_ref, u_ref)):
            o_ref[...] = _dot(h, w_ref[j])

    col = _rows(tm, ATTN_W)
    return pl.pallas_call(
        body, name="mix_in_fwd", grid=(T // tm,),
        in_specs=[_rows(tm, D_MODEL), _full((1, D_MODEL)), _full(w.shape)],
        out_specs=[col] * 4, out_shape=[jax.ShapeDtypeStruct((T, ATTN_W), F32)] * 4,
        compiler_params=_params("parallel"),
    )(x, g, w)


def _mix_out_fwd(x, attn, ssm, w, g):
    T = x.shape[0]
    tm = ROW_TILE

    def body(x_ref, a_ref, s_ref, w_ref, g_ref, o_ref, m_ref):
        mixed = _dot(a_ref[...], w_ref[0]) + _dot(s_ref[...], w_ref[1])
        m_ref[...] = mixed
        o_ref[...] = x_ref[...] + mixed * _rms(mixed) * g_ref[...]

    row, col = _rows(tm, D_MODEL), _rows(tm, ATTN_W)
    return pl.pallas_call(
        body, name="mix_out_fwd", grid=(T // tm,),
        in_specs=[row, col, col, _full(w.shape), _full((1, D_MODEL))],
        out_specs=[row, row], out_shape=[jax.ShapeDtypeStruct((T, D_MODEL), F32)] * 2,
        compiler_params=_params("parallel"),
    )(x, attn, ssm, w, g)


def _mix_out_bwd(dout, mixed, attn, ssm, w, g):
    T = dout.shape[0]
    tm = ROW_TILE

    def body(dout_ref, m_ref, a_ref, s_ref, w_ref, g_ref, da_ref, ds_ref, dw_ref, dg_ref):
        @pl.when(pl.program_id(0) == 0)
        def _():
            dw_ref[...] = jnp.zeros_like(dw_ref)
            dg_ref[...] = jnp.zeros_like(dg_ref)

        dy = dout_ref[...]
        mixed = m_ref[...]
        r = _rms(mixed)
        mn = mixed * r
        dg_ref[...] += jnp.sum(dy * mn, axis=0, keepdims=True)
        dm = _rms_bwd(dy * g_ref[...], mn, r).astype(BF16)
        da_ref[...] = _dot_nt(dm, w_ref[0])
        ds_ref[...] = _dot_nt(dm, w_ref[1])
        dw_ref[0] += _dot_tn(a_ref[...], dm)
        dw_ref[1] += _dot_tn(s_ref[...], dm)

    row, col = _rows(tm, D_MODEL), _rows(tm, ATTN_W)
    return pl.pallas_call(
        body, name="mix_out_bwd", grid=(T // tm,),
        in_specs=[row, row, col, col, _full(w.shape), _full((1, D_MODEL))],
        out_specs=[col, col, _full(w.shape), _full((1, D_MODEL))],
        out_shape=[jax.ShapeDtypeStruct((T, ATTN_W), F32)] * 2
        + [jax.ShapeDtypeStruct(w.shape, F32), jax.ShapeDtypeStruct((1, D_MODEL), F32)],
        compiler_params=_params("arbitrary"),
    )(dout, mixed, attn, ssm, w, g)


ATTN_TILING = {1: (ATTN_W, 1), 4: (2 * HEAD_DIM, 4), 16: (2 * HEAD_DIM, 4)}


def _class_rows(d, r):
    return (pl.ds(r, QBLK, stride=d), slice(None)) if d > 1 else (slice(None), slice(None))


def _for_class_groups(d, group, fn):
    if d == group:
        fn(0)
    else:
        lax.fori_loop(0, d // group, lambda n, carry: (fn(n * group), carry)[1], 0)


def _block_slopes(lanes):
    heads = lanes // HEAD_DIM
    first = pl.program_id(1) * heads
    return [jnp.exp2(-jnp.full((1, 1), first + hh + 1, jnp.int32).astype(F32)) for hh in range(heads)]


def _attn_specs(d, nb, lanes):
    blk = (QBLK * d, lanes)
    cur = pl.BlockSpec(blk, lambda j, lb: (j, lb))
    prev = pl.BlockSpec(blk, lambda j, lb: (jnp.maximum(j - 1, 0), lb))
    nxt = pl.BlockSpec(blk, lambda j, lb: (jnp.minimum(j + 1, nb - 1), lb))
    return cur, prev, nxt


def _attn_branch_fwd(q, k, v, d):
    T = q.shape[0]
    nb = T // (d * QBLK)
    lanes, group = ATTN_TILING[d]
    scale = HEAD_DIM ** -0.5

    def body(q_ref, kc_ref, kp_ref, vc_ref, vp_ref, o_ref, l_ref, q_s, kk_s, vv_s, o_s, l_s):
        j = pl.program_id(0)
        qi = lax.broadcasted_iota(jnp.int32, (QBLK, 2 * QBLK), 0)
        ci = lax.broadcasted_iota(jnp.int32, (QBLK, 2 * QBLK), 1)
        steps = QBLK + qi - ci
        valid = (steps >= 0) & (steps <= N_BACK) & ((ci >= QBLK) | (j > 0))
        dist = (steps * d).astype(F32)
        slopes = _block_slopes(lanes)

        def classes(first):
            for n in range(group):
                rows = _class_rows(d, first + n)
                q_s[n] = q_ref[rows]
                kk_s[n, :QBLK], kk_s[n, QBLK:] = kp_ref[rows], kc_ref[rows]
                vv_s[n, :QBLK], vv_s[n, QBLK:] = vp_ref[rows], vc_ref[rows]
            for n in range(group):
                for hh in range(lanes // HEAD_DIM):
                    sl = slice(hh * HEAD_DIM, (hh + 1) * HEAD_DIM)
                    s = _dot_nt(q_s[n, :, sl], kk_s[n, :, sl]) * scale - slopes[hh] * dist
                    s = jnp.where(valid, s, NEG)
                    m = jnp.max(s, axis=-1, keepdims=True)
                    p = jnp.exp(s - m)
                    den = jnp.sum(p, axis=-1, keepdims=True)
                    o_s[n, :, sl] = _dot(p, vv_s[n, :, sl]) / den
                    l_s[n, :, sl] = jnp.broadcast_to(m + jnp.log(den), (QBLK, HEAD_DIM))
            for n in range(group):
                rows = _class_rows(d, first + n)
                o_ref[rows] = o_s[n]
                l_ref[rows] = l_s[n]

        _for_class_groups(d, group, classes)

    cur, prev, _ = _attn_specs(d, nb, lanes)
    shape = jax.ShapeDtypeStruct((T, ATTN_W), F32)
    one, two = pltpu.VMEM((group, QBLK, lanes), F32), pltpu.VMEM((group, 2 * QBLK, lanes), F32)
    return pl.pallas_call(
        body, name=f"attn_fwd_d{d}", grid=(nb, ATTN_W // lanes),
        in_specs=[cur, cur, prev, cur, prev], out_specs=[cur, cur], out_shape=[shape, shape],
        scratch_shapes=[one, two, two, one, one], compiler_params=_params("parallel", "parallel"),
    )(q, k, k, v, v)


def _attn_merge(outs, lses):
    T = outs[0].shape[0]
    tm = 512

    def body(o1, o2, o3, l1, l2, l3, a_ref, lse_ref):
        ls = [l1[...], l2[...], l3[...]]
        m = jnp.maximum(jnp.maximum(ls[0], ls[1]), ls[2])
        lse = m + jnp.log(jnp.exp(ls[0] - m) + jnp.exp(ls[1] - m) + jnp.exp(ls[2] - m))
        lse_ref[...] = lse
        a_ref[...] = jnp.exp(ls[0] - lse) * o1[...] + jnp.exp(ls[1] - lse) * o2[...] + jnp.exp(ls[2] - lse) * o3[...]

    col = _rows(tm, ATTN_W)
    return pl.pallas_call(
        body, name="attn_merge", grid=(T // tm,), in_specs=[col] * 6, out_specs=[col, col],
        out_shape=[jax.ShapeDtypeStruct((T, ATTN_W), F32)] * 2, compiler_params=_params("parallel"),
    )(*outs, *lses)


def _attn_branch_bwd(q, k, v, o, lse, do, d):
    T = q.shape[0]
    nb = T // (d * QBLK)
    lanes, group = ATTN_TILING[d]
    scale = HEAD_DIM ** -0.5

    def body(qc_ref, qn_ref, kc_ref, kp_ref, vc_ref, vp_ref, oc_ref, on_ref, lc_ref, ln_ref, doc_ref, don_ref,
             dq_ref, dk_ref, dv_ref, qq_s, kk_s, vv_s, oo_s, ll_s, doo_s, dq_s, dk_s, dv_s):
        j = pl.program_id(0)
        qi = lax.broadcasted_iota(jnp.int32, (QBLK, 2 * QBLK), 0)
        ci = lax.broadcasted_iota(jnp.int32, (QBLK, 2 * QBLK), 1)
        steps_q = QBLK + qi - ci
        valid_q = (steps_q >= 0) & (steps_q <= N_BACK) & ((ci >= QBLK) | (j > 0))
        dist_q = (steps_q * d).astype(F32)
        ri = lax.broadcasted_iota(jnp.int32, (2 * QBLK, QBLK), 0)
        ki = lax.broadcasted_iota(jnp.int32, (2 * QBLK, QBLK), 1)
        steps_k = ri - ki
        valid_k = (steps_k >= 0) & (steps_k <= N_BACK) & ((ri < QBLK) | (j < nb - 1))
        dist_k = (steps_k * d).astype(F32)
        lo, hi = slice(0, QBLK), slice(QBLK, 2 * QBLK)
        slopes = _block_slopes(lanes)

        def classes(first):
            for n in range(group):
                rows = _class_rows(d, first + n)
                qq_s[n, lo], qq_s[n, hi] = qc_ref[rows], qn_ref[rows]
                oo_s[n, lo], oo_s[n, hi] = oc_ref[rows], on_ref[rows]
                ll_s[n, lo], ll_s[n, hi] = lc_ref[rows], ln_ref[rows]
                doo_s[n, lo], doo_s[n, hi] = doc_ref[rows], don_ref[rows]
                kk_s[n, lo], kk_s[n, hi] = kp_ref[rows], kc_ref[rows]
                vv_s[n, lo], vv_s[n, hi] = vp_ref[rows], vc_ref[rows]
            for n in range(group):
                for hh in range(lanes // HEAD_DIM):
                    sl = slice(hh * HEAD_DIM, (hh + 1) * HEAD_DIM)
                    slope = slopes[hh]
                    qq, doo, kk, vv = qq_s[n, :, sl], doo_s[n, :, sl], kk_s[n, :, sl], vv_s[n, :, sl]
                    lse2 = ll_s[n, :, hh * HEAD_DIM:hh * HEAD_DIM + 1]
                    delta2 = jnp.sum(doo * oo_s[n, :, sl], axis=-1, keepdims=True)

                    s = jnp.where(valid_q, _dot_nt(qq[lo], kk) * scale - slope * dist_q, NEG)
                    p = jnp.exp(s - lse2[lo])
                    ds = p * (_dot_nt(doo[lo], vv) - delta2[lo])
                    dq_s[n, :, sl] = _dot(ds, kk) * scale

                    s2 = jnp.where(valid_k, _dot_nt(qq, kk[hi]) * scale - slope * dist_k, NEG)
                    p2 = jnp.exp(s2 - lse2)
                    dv_s[n, :, sl] = _dot_tn(p2, doo)
                    ds2 = p2 * (_dot_nt(doo, vv[hi]) - delta2)
                    dk_s[n, :, sl] = _dot_tn(ds2, qq) * scale
            for n in range(group):
                rows = _class_rows(d, first + n)
                dq_ref[rows] = dq_s[n]
                dk_ref[rows] = dk_s[n]
                dv_ref[rows] = dv_s[n]

        _for_class_groups(d, group, classes)

    cur, prev, nxt = _attn_specs(d, nb, lanes)
    shape = jax.ShapeDtypeStruct((T, ATTN_W), F32)
    one, two = pltpu.VMEM((group, QBLK, lanes), F32), pltpu.VMEM((group, 2 * QBLK, lanes), F32)
    return pl.pallas_call(
        body, name=f"attn_bwd_d{d}", grid=(nb, ATTN_W // lanes),
        in_specs=[cur, nxt, cur, prev, cur, prev, cur, nxt, cur, nxt, cur, nxt],
        out_specs=[cur] * 3, out_shape=[shape] * 3, scratch_shapes=[two] * 6 + [one] * 3,
        compiler_params=_params("parallel", "parallel"),
    )(q, q, k, k, v, v, o, o, lse, lse, do, do)


def _dproj_merge(dqs, dks, dvs, du):
    T = du.shape[0]
    tm = 512

    def body(*refs):
        o_ref = refs[-1]
        for part in range(3):
            a, b, c = refs[3 * part:3 * part + 3]
            o_ref[:, part * ATTN_W:(part + 1) * ATTN_W] = (a[...] + b[...] + c[...]).astype(BF16)
        o_ref[:, 3 * ATTN_W:] = refs[9][...].astype(BF16)

    col = _rows(tm, ATTN_W)
    return pl.pallas_call(
        body, name="dproj_merge", grid=(T // tm,), in_specs=[col] * 10, out_specs=_rows(tm, 4 * ATTN_W),
        out_shape=jax.ShapeDtypeStruct((T, 4 * ATTN_W), BF16), compiler_params=_params("parallel"),
    )(*dqs, *dks, *dvs, du)


def _attention_fwd(q, k, v):
    res = [_attn_branch_fwd(q, k, v, d) for d in DILATIONS]
    return _attn_merge([r[0] for r in res], [r[1] for r in res])


def _attention_bwd(q, k, v, attn, lse, dattn):
    res = [_attn_branch_bwd(q, k, v, attn, lse, dattn, d) for d in DILATIONS]
    return [r[0] for r in res], [r[1] for r in res], [r[2] for r in res]


SCAN_ROWS = 8
SCAN_LANES = 512
SSM_CHUNK = 256


def _cmul(ar, ai, br, bi):
    return ar * br - ai * bi, ar * bi + ai * br


def _ssm_discretize(a_re, a_im, log_dt, b_re, b_im):
    def body(ar_ref, ai_ref, ldt_ref, br_ref, bi_ref, abr_ref, abi_ref, er_ref, ei_ref, bbr_ref, bbi_ref, pr_ref, pi_ref):
        ar, ai = ar_ref[...], ai_ref[...]
        dt = jnp.exp(ldt_ref[...])
        n = lax.broadcasted_iota(jnp.int32, (1, SCAN_ROWS), 1).astype(F32) + 1.0
        mag, ang = jnp.exp(dt * ar), dt * ai
        abr, abi = mag * jnp.cos(ang), mag * jnp.sin(ang)
        abr_ref[...], abi_ref[...] = abr, abi
        pr_ref[...] = jnp.exp(dt * ar * n) * jnp.cos(ang * n)
        pi_ref[...] = jnp.exp(dt * ar * n) * jnp.sin(ang * n)
        den = ar * ar + ai * ai
        er = ((abr - 1.0) * ar + abi * ai) / den
        ei = (abi * ar - (abr - 1.0) * ai) / den
        er_ref[...], ei_ref[...] = er, ei
        bbr_ref[...], bbi_ref[...] = _cmul(er, ei, br_ref[...], bi_ref[...])

    col = jax.ShapeDtypeStruct((N_STATE, 1), F32)
    mat = jax.ShapeDtypeStruct((N_STATE, GROUP_CH), F32)
    pw = jax.ShapeDtypeStruct((N_STATE, SCAN_ROWS), F32)
    return pl.pallas_call(body, name="ssm_discretize", out_shape=[col] * 4 + [mat] * 2 + [pw] * 2)(
        a_re, a_im, log_dt, b_re, b_im)


def _ssm_discretize_bwd(a_re, a_im, log_dt, b_re, b_im, ab_re, ab_im, e_re, e_im, dab_re, dab_im, dbb_re, dbb_im):
    def body(ar_ref, ai_ref, ldt_ref, br_ref, bi_ref, abr_ref, abi_ref, er_ref, ei_ref, dabr_ref, dabi_ref,
             dbbr_ref, dbbi_ref, dar_ref, dai_ref, ddt_ref, dbr_ref, dbi_ref):
        ar, ai, dt = ar_ref[...], ai_ref[...], jnp.exp(ldt_ref[...])
        er, ei = er_ref[...], ei_ref[...]
        gbr, gbi = dbbr_ref[...], dbbi_ref[...]
        dbr_ref[...], dbi_ref[...] = _cmul(er, -ei, gbr, gbi)
        br, bi = br_ref[...], bi_ref[...]
        der = jnp.sum(br * gbr + bi * gbi, axis=-1, keepdims=True)
        dei = jnp.sum(br * gbi - bi * gbr, axis=-1, keepdims=True)
        den = ar * ar + ai * ai
        inv_r, inv_i = ar / den, -ai / den
        t_r, t_i = _cmul(der, dei, inv_r, -inv_i)
        gab_r, gab_i = dabr_ref[...] + t_r, dabi_ref[...] + t_i
        q_r, q_i = _cmul(er, ei, inv_r, inv_i)
        dl_r, dl_i = _cmul(der, dei, q_r, -q_i)
        dl_r, dl_i = -dl_r, -dl_i
        gw_r, gw_i = _cmul(gab_r, gab_i, abr_ref[...], -abi_ref[...])
        dar_ref[...] = dl_r + dt * gw_r
        dai_ref[...] = dl_i + dt * gw_i
        ddt_ref[...] = (gw_r * ar + gw_i * ai) * dt

    col = jax.ShapeDtypeStruct((N_STATE, 1), F32)
    mat = jax.ShapeDtypeStruct((N_STATE, GROUP_CH), F32)
    return pl.pallas_call(body, name="ssm_discretize_bwd", out_shape=[col] * 3 + [mat] * 2)(
        a_re, a_im, log_dt, b_re, b_im, ab_re, ab_im, e_re, e_im, dab_re, dab_im, dbb_re, dbb_im)


def _scan_tables(p_re, p_im, reverse):
    pr, pi = p_re.T, p_im.T
    if reverse:
        pi = -pi
    bc = lambda t, n: jnp.broadcast_to(t[n - 1], (SCAN_ROWS, N_STATE))
    carry = (pr[::-1], pi[::-1]) if reverse else (pr, pi)
    return jnp.stack([bc(pr, 1), bc(pi, 1), bc(pr, 2), bc(pi, 2), bc(pr, 4), bc(pi, 4), carry[0], carry[1]])


def _scan_group(xr, xi, tab_ref, ls, carry_r, carry_i, reverse):
    row = lax.broadcasted_iota(jnp.int32, (SCAN_ROWS, SCAN_LANES), 0)
    for n, s in enumerate((1, 2, 4)):
        if reverse:
            keep = row < SCAN_ROWS - s
            shr, shi = pltpu.roll(xr, SCAN_ROWS - s, 0), pltpu.roll(xi, SCAN_ROWS - s, 0)
        else:
            keep = row >= s
            shr, shi = pltpu.roll(xr, s, 0), pltpu.roll(xi, s, 0)
        shr, shi = jnp.where(keep, shr, 0.0), jnp.where(keep, shi, 0.0)
        mr, mi = _cmul(tab_ref[2 * n, :, ls], tab_ref[2 * n + 1, :, ls], shr, shi)
        xr, xi = xr + mr, xi + mi
    mr, mi = _cmul(tab_ref[6, :, ls], tab_ref[7, :, ls], carry_r, carry_i)
    return xr + mr, xi + mi


def _gelu(y):
    t = jnp.tanh(GELU_C * (y + 0.044715 * y * y * y))
    return 0.5 * y * (1.0 + t), t


def _ssm_fwd(u, tab, bd_re, bd_im, cd_re, cd_im, d_skip, w_glu, b_glu):
    T = u.shape[0]
    tc = SSM_CHUNK

    def body(u_ref, tab_ref, bdr_ref, bdi_ref, cdr_ref, cdi_ref, dsk_ref, wg_ref, bg_ref,
             sr_ref, si_ref, yp_ref, o_ref, car_r, car_i):
        @pl.when(pl.program_id(0) == 0)
        def _():
            car_r[...] = jnp.zeros_like(car_r)
            car_i[...] = jnp.zeros_like(car_i)

        uv = u_ref[...]
        sr_ref[...] = _dot(uv, bdr_ref[...])
        si_ref[...] = _dot(uv, bdi_ref[...])
        for lb in range(N_STATE // SCAN_LANES):
            ls = pl.ds(lb * SCAN_LANES, SCAN_LANES)

            def step(g, carry):
                rows = pl.ds(pl.multiple_of(g * SCAN_ROWS, SCAN_ROWS), SCAN_ROWS)
                xr, xi = _scan_group(sr_ref[rows, ls], si_ref[rows, ls], tab_ref, ls, carry[0], carry[1], False)
                sr_ref[rows, ls] = xr
                si_ref[rows, ls] = xi
                last = slice(SCAN_ROWS - 1, SCAN_ROWS)
                return (jnp.broadcast_to(xr[last], xr.shape), jnp.broadcast_to(xi[last], xi.shape))

            cr, ci = lax.fori_loop(0, tc // SCAN_ROWS, step, (car_r[:, ls], car_i[:, ls]))
            car_r[:, ls] = cr
            car_i[:, ls] = ci
        y = _dot(sr_ref[...], cdr_ref[...]) - _dot(si_ref[...], cdi_ref[...]) + dsk_ref[...] * uv
        yp_ref[...] = y
        gy, _ = _gelu(y)
        o_ref[...] = gy * jax.nn.sigmoid(_dot(gy, wg_ref[...]) + bg_ref[...])

    col, st = _rows(tc, SSM_W), _rows(tc, N_STATE)
    vec = _full((1, SSM_W))
    return pl.pallas_call(
        body, name="ssm_fwd", grid=(T // tc,),
        in_specs=[col, _full(tab.shape), _full(bd_re.shape), _full(bd_im.shape), _full(cd_re.shape), _full(cd_im.shape),
                  vec, _full(w_glu.shape), vec],
        out_specs=[st, st, col, col],
        out_shape=[jax.ShapeDtypeStruct((T, N_STATE), F32)] * 2 + [jax.ShapeDtypeStruct((T, SSM_W), F32)] * 2,
        scratch_shapes=[pltpu.VMEM((SCAN_ROWS, N_STATE), F32)] * 2,
        compiler_params=_params("arbitrary"),
    )(u, tab, bd_re, bd_im, cd_re, cd_im, d_skip, w_glu, b_glu)


def _ssm_bwd(dout, u, yp, s_re, s_im, tab, bd_re, bd_im, cd_re, cd_im, d_skip, w_glu, b_glu):
    T = u.shape[0]
    tc = SSM_CHUNK
    nt = T // tc
    rows_per_chunk = tc // SCAN_ROWS

    def body(do_ref, u_ref, yp_ref, sr_ref, si_ref, pr_ref, pi_ref, tab_ref, dsk_ref, wg_ref, bg_ref,
             bdr_hbm, bdi_hbm, cdr_hbm, cdi_hbm,
             du_ref, dsk_out, dbg_out, dwg_out, da_out, dbdr_hbm, dbdi_hbm, dcdr_hbm, dcdi_hbm,
             bdr_v, bdi_v, cdr_v, cdi_v, dbdr_v, dbdi_v, dcdr_v, dcdi_v, gr_v, gi_v, car_r, car_i, sems):
        i = pl.program_id(0)

        @pl.when(i == 0)
        def _():
            _load_once([(bdr_hbm, bdr_v), (bdi_hbm, bdi_v), (cdr_hbm, cdr_v), (cdi_hbm, cdi_v)], sems)
            for ref in (dbdr_v, dbdi_v, dcdr_v, dcdi_v, car_r, car_i, dsk_out, dbg_out, dwg_out, da_out):
                ref[...] = jnp.zeros_like(ref)

        uv, y, dout_v = u_ref[...], yp_ref[...], do_ref[...]
        gy, t = _gelu(y)
        sg = jax.nn.sigmoid(_dot(gy, wg_ref[...]) + bg_ref[...])
        dzg = dout_v * gy * sg * (1.0 - sg)
        dgy = dout_v * sg + _dot_nt(dzg, wg_ref[...])
        dwg_out[...] += _dot_tn(gy, dzg)
        dbg_out[...] += jnp.sum(dzg, axis=0, keepdims=True)
        dy = dgy * (0.5 * (1.0 + t) + 0.5 * y * (1.0 - t * t) * GELU_C * (1.0 + 3 * 0.044715 * y * y))
        dsk_out[...] += jnp.sum(dy * uv, axis=0, keepdims=True)

        gr_v[...] = _dot_nt(dy, cdr_v[...])
        gi_v[...] = -_dot_nt(dy, cdi_v[...])
        dcdr_v[...] += _dot_tn(sr_ref[...], dy)
        dcdi_v[...] -= _dot_tn(si_ref[...], dy)

        row = lax.broadcasted_iota(jnp.int32, (SCAN_ROWS, SCAN_LANES), 0)
        first_chunk = i == nt - 1
        for lb in range(N_STATE // SCAN_LANES):
            ls = pl.ds(lb * SCAN_LANES, SCAN_LANES)

            def step(n, carry):
                g = rows_per_chunk - 1 - n
                rows = pl.ds(pl.multiple_of(g * SCAN_ROWS, SCAN_ROWS), SCAN_ROWS)
                before = pl.ds(pl.multiple_of(jnp.maximum(g - 1, 0) * SCAN_ROWS, SCAN_ROWS), SCAN_ROWS)
                xr, xi = _scan_group(gr_v[rows, ls], gi_v[rows, ls], tab_ref, ls, carry[0], carry[1], True)
                gr_v[rows, ls] = xr
                gi_v[rows, ls] = xi
                last = slice(SCAN_ROWS - 1, SCAN_ROWS)
                edge_r = jnp.where(g > 0, sr_ref[before, ls][last], jnp.where(first_chunk, 0.0, pr_ref[:, ls][last]))
                edge_i = jnp.where(g > 0, si_ref[before, ls][last], jnp.where(first_chunk, 0.0, pi_ref[:, ls][last]))
                spr = jnp.where(row >= 1, pltpu.roll(sr_ref[rows, ls], 1, 0), edge_r)
                spi = jnp.where(row >= 1, pltpu.roll(si_ref[rows, ls], 1, 0), edge_i)
                first = slice(0, 1)
                return (jnp.broadcast_to(xr[first], xr.shape), jnp.broadcast_to(xi[first], xi.shape),
                        carry[2] + xr * spr + xi * spi, carry[3] + xi * spr - xr * spi)

            zero = jnp.zeros((SCAN_ROWS, SCAN_LANES), F32)
            cr, ci, dar, dai = lax.fori_loop(0, rows_per_chunk, step, (car_r[:, ls], car_i[:, ls], zero, zero))
            car_r[:, ls] = cr
            car_i[:, ls] = ci
            da_out[0, :, ls] += dar
            da_out[1, :, ls] += dai

        du_ref[...] = dsk_ref[...] * dy + _dot_nt(gr_v[...], bdr_v[...]) + _dot_nt(gi_v[...], bdi_v[...])
        dbdr_v[...] += _dot_tn(uv, gr_v[...])
        dbdi_v[...] += _dot_tn(uv, gi_v[...])

        @pl.when(i == nt - 1)
        def _():
            outs = [(dbdr_v, dbdr_hbm), (dbdi_v, dbdi_hbm), (dcdr_v, dcdr_hbm), (dcdi_v, dcdi_hbm)]
            copies = [pltpu.make_async_copy(src, dst, sems.at[k]) for k, (src, dst) in enumerate(outs)]
            for c in copies:
                c.start()
            for c in copies:
                c.wait()

    rev = lambda i: (nt - 1 - i, 0)
    col = pl.BlockSpec((tc, SSM_W), rev)
    st = pl.BlockSpec((tc, N_STATE), rev)
    st_before = pl.BlockSpec((SCAN_ROWS, N_STATE), lambda i: (jnp.maximum((nt - 1 - i) * rows_per_chunk - 1, 0), 0))
    vec = _full((1, SSM_W))
    bd = jax.ShapeDtypeStruct(bd_re.shape, F32)
    cd = jax.ShapeDtypeStruct(cd_re.shape, F32)
    return pl.pallas_call(
        body, name="ssm_bwd", grid=(nt,),
        in_specs=[col, col, col, st, st, st_before, st_before, _full(tab.shape), vec, _full(w_glu.shape), vec,
                  ANY, ANY, ANY, ANY],
        out_specs=[col, vec, vec, _full(w_glu.shape), _full((2, SCAN_ROWS, N_STATE)), ANY, ANY, ANY, ANY],
        out_shape=[jax.ShapeDtypeStruct((T, SSM_W), F32), jax.ShapeDtypeStruct((1, SSM_W), F32),
                   jax.ShapeDtypeStruct((1, SSM_W), F32), jax.ShapeDtypeStruct(w_glu.shape, F32),
                   jax.ShapeDtypeStruct((2, SCAN_ROWS, N_STATE), F32), bd, bd, cd, cd],
        scratch_shapes=[pltpu.VMEM(bd_re.shape, BF16)] * 2 + [pltpu.VMEM(cd_re.shape, BF16)] * 2
        + [pltpu.VMEM(bd_re.shape, F32)] * 2 + [pltpu.VMEM(cd_re.shape, F32)] * 2
        + [pltpu.VMEM((tc, N_STATE), F32)] * 2 + [pltpu.VMEM((SCAN_ROWS, N_STATE), F32)] * 2
        + [pltpu.SemaphoreType.DMA((4,))],
        compiler_params=_params("arbitrary"),
    )(dout, u, yp, s_re, s_im, s_re, s_im, tab, d_skip, w_glu, b_glu, bd_re, bd_im, cd_re, cd_im)


def _block_diag(t):
    g, a, b = t.shape
    eye = jnp.eye(N_GROUPS, dtype=t.dtype)
    return (t[:, :, None, :] * eye[:, None, :, None]).reshape(g * a, g * b)


def _diag_blocks(m, a, b):
    eye = jnp.eye(N_GROUPS, dtype=m.dtype)
    return jnp.sum(m.reshape(N_GROUPS, a, N_GROUPS, b) * eye[:, None, :, None], axis=2)


def _ssm_prepare(a_re, a_im, log_dt, b_re, b_im, c_re, c_im):
    col = lambda t: t.reshape(N_STATE, 1)
    ldt = jnp.broadcast_to(log_dt.reshape(N_GROUPS, 1), (N_GROUPS, STATE)).reshape(N_STATE, 1)
    b2r, b2i = b_re.reshape(N_STATE, GROUP_CH), b_im.reshape(N_STATE, GROUP_CH)
    ab_r, ab_i, e_r, e_i, bb_r, bb_i, p_r, p_i = _ssm_discretize(col(a_re), col(a_im), ldt, b2r, b2i)
    bd = [_block_diag(jnp.swapaxes(t.reshape(N_GROUPS, STATE, GROUP_CH), 1, 2)).astype(BF16) for t in (bb_r, bb_i)]
    cd = [_block_diag(jnp.swapaxes(t.reshape(N_GROUPS, GROUP_CH, STATE), 1, 2)).astype(BF16) for t in (c_re, c_im)]
    saved = dict(a_re=col(a_re), a_im=col(a_im), log_dt=ldt, b_re=b2r, b_im=b2i, ab_re=ab_r, ab_im=ab_i, e_re=e_r, e_im=e_i)
    return _scan_tables(p_r, p_i, False), _scan_tables(p_r, p_i, True), bd, cd, saved


BIG = ("ffn1_w_in", "ffn1_w_out", "w_mix_in", "w_glu", "w_mix_out", "ffn2_w_in", "ffn2_w_out")
SMALL = ("ffn1_pre_g", "ffn1_post_g", "mix_pre_g", "a_re", "a_im", "log_dt", "b_re", "b_im", "c_re", "c_im",
         "d_skip", "b_glu", "mix_post_g", "ffn2_pre_g", "ffn2_post_g")


def _local_step(x, target, p, w):
    vec = lambda t: t.reshape(1, -1)
    w1_in, w2_in = w["ffn1_w_in"], w["ffn2_w_in"]
    w1_out, w2_out = (w[n].reshape(2, FF_BLK, D_MODEL) for n in ("ffn1_w_out", "ffn2_w_out"))
    w_mi = w["w_mix_in"]
    w_glu = w["w_glu"].reshape(SSM_W, SSM_W)
    w_mo = w["w_mix_out"].reshape(2, ATTN_W, D_MODEL)

    x1, z1, f1 = _ffn_fwd(x, vec(p["ffn1_pre_g"]), w1_in, w1_out, vec(p["ffn1_post_g"]), None, name="ffn1_fwd")
    q, k, v, u = _mix_in_fwd(x1, vec(p["mix_pre_g"]), w_mi)
    attn, lse = _attention_fwd(q, k, v)
    tab_f, tab_b, bd, cd, sv = _ssm_prepare(p["a_re"], p["a_im"], p["log_dt"], p["b_re"], p["b_im"], p["c_re"], p["c_im"])
    ssm_args = (bd[0], bd[1], cd[0], cd[1], vec(p["d_skip"]), w_glu, vec(p["b_glu"]))
    s_re, s_im, yp, ssm = _ssm_fwd(u, tab_f, *ssm_args)
    x2, mixed = _mix_out_fwd(x1, attn, ssm, w_mo, vec(p["mix_post_g"]))
    dx3, loss_rows, z2, f2 = _ffn_fwd(x2, vec(p["ffn2_pre_g"]), w2_in, w2_out, vec(p["ffn2_post_g"]), target, name="ffn2_fwd")

    g = {}
    dz2, g["ffn2_post_g"], dw2_out = _ffn_bwd_out(dx3, f2, z2, w2_out, vec(p["ffn2_post_g"]), name="ffn2_bwd_out")
    dx2, g["ffn2_pre_g"], g["ffn2_w_in"] = _norm_matmul_bwd(dz2, x2, dx3, vec(p["ffn2_pre_g"]), w2_in, name="ffn2_bwd_in")
    dattn, dssm, dw_mo, g["mix_post_g"] = _mix_out_bwd(dx2, mixed, attn, ssm, w_mo, vec(p["mix_post_g"]))
    du, g["d_skip"], g["b_glu"], dw_glu, da, dbd_re, dbd_im, dcd_re, dcd_im = _ssm_bwd(dssm, u, yp, s_re, s_im, tab_b, *ssm_args)
    dqs, dks, dvs = _attention_bwd(q, k, v, attn, lse, dattn)
    dproj = _dproj_merge(dqs, dks, dvs, du)
    dx1, g["mix_pre_g"], g["w_mix_in"] = _norm_matmul_bwd(dproj, x1, dx2, vec(p["mix_pre_g"]), w_mi, name="mix_bwd_in")
    dz1, g["ffn1_post_g"], dw1_out = _ffn_bwd_out(dx1, f1, z1, w1_out, vec(p["ffn1_post_g"]), name="ffn1_bwd_out")
    grad_x, g["ffn1_pre_g"], g["ffn1_w_in"] = _norm_matmul_bwd(dz1, x, dx1, vec(p["ffn1_pre_g"]), w1_in, name="ffn1_bwd_in")

    g["c_re"], g["c_im"] = (jnp.swapaxes(_diag_blocks(m, STATE, GROUP_CH), 1, 2) for m in (dcd_re, dcd_im))
    dbb = [jnp.swapaxes(_diag_blocks(m, GROUP_CH, STATE), 1, 2).reshape(N_STATE, GROUP_CH) for m in (dbd_re, dbd_im)]
    dab = [jnp.sum(da[n], axis=0).reshape(N_STATE, 1) for n in range(2)]
    da_re, da_im, dldt, db_re, db_im = _ssm_discretize_bwd(
        sv["a_re"], sv["a_im"], sv["log_dt"], sv["b_re"], sv["b_im"], sv["ab_re"], sv["ab_im"], sv["e_re"], sv["e_im"],
        dab[0], dab[1], dbb[0], dbb[1])
    g["a_re"], g["a_im"] = da_re.reshape(N_GROUPS, STATE), da_im.reshape(N_GROUPS, STATE)
    g["log_dt"] = jnp.sum(dldt.reshape(N_GROUPS, STATE), axis=-1)
    g["b_re"], g["b_im"] = (t.reshape(N_GROUPS, STATE, GROUP_CH) for t in (db_re, db_im))

    g["ffn1_w_out"] = dw1_out.reshape(w["ffn1_w_out"].shape)
    g["ffn2_w_out"] = dw2_out.reshape(w["ffn2_w_out"].shape)
    g["w_glu"] = dw_glu.reshape(w["w_glu"].shape)
    g["w_mix_out"] = dw_mo.reshape(w["w_mix_out"].shape)
    return loss_rows, grad_x, g


MESH = pl.DeviceIdType.MESH
N_REL = 3


def _place():
    x, y, c = lax.axis_index("x"), lax.axis_index("y"), lax.axis_index("c")
    return x, y, c, [(1 - x, y), (x, 1 - y), (1 - x, 1 - y)]


def _remote(src, dst, send_sems, recv_sems, idx, to):
    return pltpu.make_async_remote_copy(src_ref=src, dst_ref=dst, send_sem=send_sems.at[idx], recv_sem=recv_sems.at[idx],
                                        device_id=to, device_id_type=MESH)


def _half(rows, who):
    return pl.ds(who * (rows // 2), rows // 2)


class _Copies:
    def __init__(self, send_sems, recv_sems, local_sems):
        self.send_sems, self.recv_sems, self.local_sems = send_sems, recv_sems, local_sems
        self.n_remote = self.n_local = 0

    def remote(self, src, dst, to):
        k, self.n_remote = self.n_remote, self.n_remote + 1
        return pltpu.make_async_remote_copy(src_ref=src, dst_ref=dst, send_sem=self.send_sems.at[k],
                                            recv_sem=self.recv_sems.at[k], device_id=to, device_id_type=MESH)

    def local(self, src, dst):
        k, self.n_local = self.n_local, self.n_local + 1
        return pltpu.make_async_copy(src, dst, self.local_sems.at[k])


def _exchange(name, plan, ins, out_shapes, n_remote, n_local=1):
    n_in = len(ins)

    def body(*refs):
        in_refs, out_refs = refs[:n_in], refs[n_in:-3]
        for phase in plan(in_refs, out_refs, _Copies(*refs[-3:])):
            for cp in phase:
                cp.start()
            for cp in phase:
                cp.wait()

    return pl.pallas_call(
        body, name=name, in_specs=[ANY] * n_in, out_specs=[ANY] * len(out_shapes), out_shape=out_shapes,
        scratch_shapes=[pltpu.SemaphoreType.DMA((n_remote,)), pltpu.SemaphoreType.DMA((n_remote,)),
                        pltpu.SemaphoreType.DMA((n_local,))],
    )(*ins)


def _gather_weights(shards, name):
    def plan(ins, outs, mk):
        x, y, c = lax.axis_index("x"), lax.axis_index("y"), lax.axis_index("c")
        me, sibling = 2 * x + y, (x, y, 1 - c)
        x_nb, y_nb, diag = (1 - x, y), (x, 1 - y), (1 - x, 1 - y)
        index = lambda chip: 2 * chip[0] + chip[1]
        first, second, third = [], [], []
        for i, shard in enumerate(shards):
            rows = shard.shape[0]
            mine = _half(rows, c)
            quarter = lambda which: pl.ds(c * (rows // 2) + which * (rows // 4), rows // 4)
            first.append(mk.remote(ins[i], outs[i].at[me], sibling))
            for nb in (x_nb, y_nb):
                first.append(mk.remote(ins[i].at[mine], outs[i].at[me, mine], (*nb, c)))
            for nb in (x_nb, y_nb):
                landed = outs[i].at[index(nb), mine]
                second.append(mk.remote(landed, landed, sibling))
            for nb, other, which in ((x_nb, y_nb, 0), (y_nb, x_nb, 1)):
                landed = outs[i].at[index(nb), quarter(which)]
                second.append(mk.remote(landed, landed, (*other, c)))
            landed = outs[i].at[index(diag), mine]
            third.append(mk.remote(landed, landed, sibling))
        return [first, second, third]

    n = len(shards)
    return _exchange(name, plan, shards, [jax.ShapeDtypeStruct((N_CHIPS,) + s.shape, s.dtype) for s in shards], 8 * n)


def _pair_exchange(grads):
    def plan(ins, outs, mk):
        x, y, c = lax.axis_index("x"), lax.axis_index("y"), lax.axis_index("c")
        return [[mk.remote(ins[i].at[:, _half(g.shape[1], 1 - c)], outs[i], (x, y, 1 - c)) for i, g in enumerate(grads)]]

    return _exchange("pair_exchange", plan, grads,
                     [jax.ShapeDtypeStruct((N_CHIPS, g.shape[1] // 2, g.shape[2]), g.dtype) for g in grads], len(grads))


def _reduce_first(pair):
    def plan(ins, outs, mk):
        x, y, c = lax.axis_index("x"), lax.axis_index("y"), lax.axis_index("c")
        phase = []
        for i, p in enumerate(pair):
            q = p.shape[1] // 2
            phase.append(mk.remote(ins[i].at[pl.ds(2 * (1 - x), 2), pl.ds(0, q)], outs[i].at[0], (1 - x, y, c)))
            for jx in range(2):
                phase.append(mk.remote(ins[i].at[2 * jx + 1 - y, pl.ds(q, q)], outs[i].at[1, jx], (x, 1 - y, c)))
        return [phase]

    return _exchange("reduce_first", plan, pair,
                     [jax.ShapeDtypeStruct((2, 2, p.shape[1] // 2, p.shape[2]), p.dtype) for p in pair], 3 * len(pair))


def _reduce_second(sums):
    def plan(ins, outs, mk):
        x, y, c = lax.axis_index("x"), lax.axis_index("y"), lax.axis_index("c")
        phase = []
        for i in range(len(sums)):
            phase.append(mk.remote(ins[i].at[0, 1 - y], outs[i].at[0], (x, 1 - y, c)))
            phase.append(mk.remote(ins[i].at[1, 1 - x], outs[i].at[1], (1 - x, y, c)))
        return [phase]

    return _exchange("reduce_second", plan, sums,
                     [jax.ShapeDtypeStruct((2,) + s.shape[2:], s.dtype) for s in sums], 2 * len(sums))


def _pair_swap(halves):
    def plan(ins, outs, mk):
        x, y, c = lax.axis_index("x"), lax.axis_index("y"), lax.axis_index("c")
        return [[mk.remote(ins[i], outs[i], (x, y, 1 - c)) for i in range(len(halves))]]

    return _exchange("pair_swap", plan, halves, [jax.ShapeDtypeStruct(h.shape, h.dtype) for h in halves], len(halves))


def _allreduce_small(packed):
    rows = packed.shape[0]
    n_dev = 2 * N_CHIPS

    def body(x_ref, o_ref, buf, send_sems, recv_sems):
        x, y, c, chips = _place()
        sibling = (x, y, 1 - c)

        def slot(px, py, pc):
            return buf.at[4 * px + 2 * py + pc]

        buf[4 * x + 2 * y + c] = x_ref[...]
        first = [_remote(x_ref, slot(x, y, c), send_sems, recv_sems, 0, sibling)]
        first += [_remote(x_ref, slot(x, y, c), send_sems, recv_sems, 1 + k, (*chip, c)) for k, chip in enumerate(chips)]
        for cp in first:
            cp.start()
        passed = []
        for k, chip in enumerate(chips):
            landed = slot(*chip, c)
            _remote(landed, landed, send_sems, recv_sems, 1 + k, (*chip, c)).wait_recv()
            passed.append(_remote(landed, landed, send_sems, recv_sems, 1 + N_REL + k, sibling))
            passed[-1].start()
        _remote(slot(*sibling), slot(*sibling), send_sems, recv_sems, 0, sibling).wait_recv()
        for k, chip in enumerate(chips):
            landed = slot(*chip, 1 - c)
            _remote(landed, landed, send_sems, recv_sems, 1 + N_REL + k, sibling).wait_recv()
        for cp in first + passed:
            cp.wait_send()
        total = buf[0]
        for d in range(1, n_dev):
            total = total + buf[d]
        o_ref[...] = total

    vm = pl.BlockSpec(memory_space=pltpu.VMEM)
    return pl.pallas_call(
        body, name="allreduce_small", in_specs=[vm], out_specs=vm, out_shape=jax.ShapeDtypeStruct(packed.shape, F32),
        scratch_shapes=[pltpu.VMEM((n_dev, rows, 128), F32), pltpu.SemaphoreType.DMA((1 + 2 * N_REL,)),
                        pltpu.SemaphoreType.DMA((1 + 2 * N_REL,))],
    )(packed)


def _row_tile(rows, cap=256):
    return max(t for t in range(8, cap + 1, 8) if rows % t == 0)


def _pair_sum(grad, got, c, name):
    _, half, cols = got.shape
    tr = _row_tile(half)
    nt = half // tr

    def body(c_ref, g_ref, r_ref, o_ref):
        o_ref[...] = g_ref[...] + r_ref[...]

    blk = (1, tr, cols)
    return pl.pallas_call(
        body, name=name,
        grid_spec=pltpu.PrefetchScalarGridSpec(
            num_scalar_prefetch=1, grid=(N_CHIPS, nt),
            in_specs=[pl.BlockSpec(blk, lambda j, t, c_ref: (j, c_ref[0] * nt + t, 0)),
                      pl.BlockSpec(blk, lambda j, t, c_ref: (j, t, 0))],
            out_specs=pl.BlockSpec(blk, lambda j, t, c_ref: (j, t, 0))),
        out_shape=jax.ShapeDtypeStruct(got.shape, F32), compiler_params=_params("parallel", "parallel"),
    )(c, grad, got)


def _reduce_sum_first(pair, got, sel, name):
    _, _, q, cols = got.shape
    tr = _row_tile(q)
    nt = q // tr

    def body(sel_ref, p_ref, r_ref, o_ref):
        o_ref[0, 0] = p_ref[0] + r_ref[0, 0]

    blk = (1, 1, tr, cols)
    return pl.pallas_call(
        body, name=name,
        grid_spec=pltpu.PrefetchScalarGridSpec(
            num_scalar_prefetch=1, grid=(2, 2, nt),
            in_specs=[pl.BlockSpec((1, tr, cols), lambda p, k, t, s: (s[2 * p] + s[2 * p + 1] * k, p * nt + t, 0)),
                      pl.BlockSpec(blk, lambda p, k, t, s: (p, k, t, 0))],
            out_specs=pl.BlockSpec(blk, lambda p, k, t, s: (p, k, t, 0))),
        out_shape=jax.ShapeDtypeStruct(got.shape, F32), compiler_params=_params("parallel", "parallel", "parallel"),
    )(sel, pair, got)


def _reduce_sum_second(sums, got, sel, name):
    _, q, cols = got.shape
    tr = _row_tile(q)

    def body(sel_ref, s_ref, r_ref, o_ref):
        o_ref[0] = s_ref[0, 0] + r_ref[0]

    blk = (1, tr, cols)
    return pl.pallas_call(
        body, name=name,
        grid_spec=pltpu.PrefetchScalarGridSpec(
            num_scalar_prefetch=1, grid=(2, q // tr),
            in_specs=[pl.BlockSpec((1, 1, tr, cols), lambda p, t, s: (p, s[p], t, 0)),
                      pl.BlockSpec(blk, lambda p, t, s: (p, t, 0))],
            out_specs=pl.BlockSpec(blk, lambda p, t, s: (p, t, 0))),
        out_shape=jax.ShapeDtypeStruct(got.shape, F32), compiler_params=_params("parallel", "parallel"),
    )(sel, sums, got)


def _adamw_update(w, g, m, v):
    m2 = ADAM_B1 * m + (1.0 - ADAM_B1) * g
    v2 = ADAM_B2 * v + (1.0 - ADAM_B2) * (g * g)
    m_hat = m2 / (1.0 - ADAM_B1 ** ADAM_STEP)
    v_hat = v2 / (1.0 - ADAM_B2 ** ADAM_STEP)
    return -ADAM_LR * (m_hat / (jnp.sqrt(v_hat) + ADAM_EPS) + ADAM_WD * w), m2, v2


def _adamw(w, g, m, v, name):
    rows, cols = w.shape
    tr = _row_tile(rows)

    def body(w_ref, g_ref, m_ref, v_ref, d_ref, mo_ref, vo_ref):
        d_ref[...], mo_ref[...], vo_ref[...] = _adamw_update(w_ref[...], g_ref[...], m_ref[...], v_ref[...])

    blk = _rows(tr, cols)
    return pl.pallas_call(
        body, name=name, grid=(rows // tr,), in_specs=[blk] * 4, out_specs=[blk] * 3,
        out_shape=[jax.ShapeDtypeStruct(w.shape, F32)] * 3, compiler_params=_params("parallel"),
    )(w, g, m, v)


def _adamw_halves(w, mine, theirs, m, v, core, name):
    rows, cols = w.shape
    tr = _row_tile(rows // 2)
    per_half = rows // 2 // tr

    def body(core_ref, w_ref, a_ref, b_ref, m_ref, v_ref, g_ref, d_ref, mo_ref, vo_ref):
        g = jnp.where(pl.program_id(0) // per_half == core_ref[0], a_ref[...], b_ref[...])
        g_ref[...] = g
        d_ref[...], mo_ref[...], vo_ref[...] = _adamw_update(w_ref[...], g, m_ref[...], v_ref[...])

    blk = pl.BlockSpec((tr, cols), lambda t, c: (t, 0))
    half = lambda own: pl.BlockSpec(
        (tr, cols), lambda t, c: (jnp.clip(t - (c[0] if own else 1 - c[0]) * per_half, 0, per_half - 1), 0))
    return pl.pallas_call(
        body, name=name,
        grid_spec=pltpu.PrefetchScalarGridSpec(
            num_scalar_prefetch=1, grid=(2 * per_half,), in_specs=[blk, half(True), half(False), blk, blk], out_specs=[blk] * 4),
        out_shape=[jax.ShapeDtypeStruct(w.shape, F32)] * 4, compiler_params=_params("arbitrary"),
    )(core, w, mine, theirs, m, v)


def _pack(parts):
    flat = []
    for t in parts:
        t = t.reshape(-1).astype(F32)
        flat.append(jnp.pad(t, (0, -t.shape[0] % 128)))
    flat = jnp.concatenate(flat)
    return jnp.pad(flat, (0, -flat.shape[0] % 1024)).reshape(-1, 128)


def _unpack(buf, shapes):
    flat, out, at = buf.reshape(-1), [], 0
    for s in shapes:
        size = math.prod(s)
        out.append(flat[at:at + size].reshape(s))
        at += size + (-size % 128)
    return out


def kernel(x, ffn1_pre_g, ffn1_w_in, ffn1_w_out, ffn1_post_g, mix_pre_g, w_mix_in, a_re, a_im, log_dt, b_re, b_im, c_re, c_im, d_skip, w_glu, b_glu, w_mix_out, mix_post_g, ffn2_pre_g, ffn2_w_in, ffn2_w_out, ffn2_post_g, loss_target, m_ffn1_pre_g, m_ffn1_w_in, m_ffn1_w_out, m_ffn1_post_g, m_mix_pre_g, m_w_mix_in, m_a_re, m_a_im, m_log_dt, m_b_re, m_b_im, m_c_re, m_c_im, m_d_skip, m_w_glu, m_b_glu, m_w_mix_out, m_mix_post_g, m_ffn2_pre_g, m_ffn2_w_in, m_ffn2_w_out, m_ffn2_post_g, v_ffn1_pre_g, v_ffn1_w_in, v_ffn1_w_out, v_ffn1_post_g, v_mix_pre_g, v_w_mix_in, v_a_re, v_a_im, v_log_dt, v_b_re, v_b_im, v_c_re, v_c_im, v_d_skip, v_w_glu, v_b_glu, v_w_mix_out, v_mix_post_g, v_ffn2_pre_g, v_ffn2_w_in, v_ffn2_w_out, v_ffn2_post_g):
    given = dict(locals())
    order = ("ffn1_pre_g", "ffn1_w_in", "ffn1_w_out", "ffn1_post_g", "mix_pre_g", "w_mix_in", "a_re", "a_im", "log_dt",
             "b_re", "b_im", "c_re", "c_im", "d_skip", "w_glu", "b_glu", "w_mix_out", "mix_post_g", "ffn2_pre_g",
             "ffn2_w_in", "ffn2_w_out", "ffn2_post_g")
    at_x, at_y, at_c = (lax.axis_index(a).astype(jnp.int32) for a in ("x", "y", "c"))
    core = at_c.reshape(1)
    sel_first = jnp.stack([2 * at_x, jnp.int32(1), at_y, jnp.int32(2)])
    sel_second = jnp.stack([at_y, at_x])

    shards = [given[n][0] for n in BIG]
    gathered = _gather_weights([s.astype(BF16) for s in shards], name="gather_weights")
    small = {n: given[n][0] for n in SMALL}
    loss_rows, grad_x, g = _local_step(x[0], loss_target[0], small, dict(zip(BIG, gathered)))

    local = [g[n] for n in BIG]
    pair = [_pair_sum(a, b, core, name=f"pair_sum_{n}") for n, a, b in zip(BIG, local, _pair_exchange(local))]
    sums = [_reduce_sum_first(a, b, sel_first, name=f"sum_first_{n}") for n, a, b in zip(BIG, pair, _reduce_first(pair))]
    halves = [_reduce_sum_second(a, b, sel_second, name=f"sum_second_{n}").reshape(a.shape[2] * 2, a.shape[3])
              for n, a, b in zip(BIG, sums, _reduce_second(sums))]
    from_sibling = _pair_swap(halves)

    total = _allreduce_small(_pack([g[n] for n in SMALL] + [loss_rows[0, :1]]))
    parts = _unpack(total, [small[n].shape for n in SMALL] + [(1,)])
    grads = dict(zip(SMALL, parts[:-1]))
    loss = parts[-1][0]

    delta, new_m, new_v = {}, {}, {}
    for n, w, mine, theirs in zip(BIG, shards, halves, from_sibling):
        grads[n], delta[n], new_m[n], new_v[n] = _adamw_halves(
            w, mine, theirs, given["m_" + n][0], given["v_" + n][0], core, name=f"adamw_{n}")
    packed = [_pack([given[pre + n] for n in SMALL]) for pre in ("", "m_", "v_")]
    outs = _adamw(packed[0], _pack([grads[n] for n in SMALL]), packed[1], packed[2], name="adamw_small")
    for store, buf in zip((delta, new_m, new_v), outs):
        store.update(zip(SMALL, _unpack(buf, [small[n].shape for n in SMALL])))

    lead = lambda d: [d[n][None] for n in order]
    return (loss, grad_x[None], *lead(grads), *lead(delta), *lead(new_m), *lead(new_v))
```

```python
import functools
import math

import jax
import jax.numpy as jnp
from jax import lax
from jax.experimental import pallas as pl
from jax.experimental.pallas import tpu as pltpu

F32, BF16 = jnp.float32, jnp.bfloat16

D_MODEL = 1024
D_FF = 2816
N_CHIPS = 4
FF_BLK = 2 * D_FF // N_CHIPS
ATTN_W = 512
SSM_W = 512
HEAD_DIM = 64
N_HEADS = ATTN_W // HEAD_DIM
DILATIONS = (1, 4, 16)
N_BACK = 128
QBLK = 128
N_GROUPS = 32
GROUP_CH = 16
STATE = 64
N_STATE = N_GROUPS * STATE
EPS = 1e-6
NEG = -1e30
GELU_C = math.sqrt(2.0 / math.pi)

ADAM_LR, ADAM_B1, ADAM_B2, ADAM_EPS, ADAM_WD, ADAM_STEP = 0.001, 0.9, 0.999, 1e-08, 0.01, 10

VMEM_LIMIT_V7X = 60 * 1024 * 1024
ROW_TILE = 256


def _params(*sem):
    return pltpu.CompilerParams(dimension_semantics=sem, vmem_limit_bytes=VMEM_LIMIT_V7X)


def _dot(a, b):
    return jnp.dot(a.astype(BF16), b.astype(BF16), preferred_element_type=F32)


def _dot_nt(a, b):
    return lax.dot_general(a.astype(BF16), b.astype(BF16), (((1,), (1,)), ((), ())), preferred_element_type=F32)


def _dot_tn(a, b):
    return lax.dot_general(a.astype(BF16), b.astype(BF16), (((0,), (0,)), ((), ())), preferred_element_type=F32)


def _full(shape):
    return pl.BlockSpec(shape, lambda *_: (0,) * len(shape))


def _rows(tm, width):
    return pl.BlockSpec((tm, width), lambda i: (i, 0))


ANY = pl.BlockSpec(memory_space=pl.ANY)


def _load_once(pairs, sems):
    copies = [pltpu.make_async_copy(src, dst, sems.at[k]) for k, (src, dst) in enumerate(pairs)]
    for c in copies:
        c.start()
    for c in copies:
        c.wait()


def _rms(x):
    return lax.rsqrt(jnp.mean(x * x, axis=-1, keepdims=True) + EPS)


def _rms_bwd(dy_g, xn, r):
    return r * (dy_g - xn * jnp.mean(dy_g * xn, axis=-1, keepdims=True))


def _ffn_fwd(x, g_pre, w_in, w_out, g_post, target, *, name, rider=None):
    T = x.shape[0]
    tm = ROW_TILE
    with_loss = target is not None

    def body(*refs):
        if with_loss:
            x_ref, gpre_ref, gpost_ref, tgt_ref, win_hbm, wout_hbm, o_ref, loss_ref, z_ref, f_ref, win_v, wout_v, sems = refs
        else:
            x_ref, gpre_ref, gpost_ref, win_hbm, wout_hbm, o_ref, z_ref, f_ref, win_v, wout_v, sems = refs

        @pl.when(pl.program_id(0) == 0)
        def _():
            _load_once([(win_hbm, win_v), (wout_hbm, wout_v)], sems)
            if with_loss:
                loss_ref[...] = jnp.zeros_like(loss_ref)

        xv = x_ref[...]
        h = (xv * _rms(xv) * gpre_ref[...]).astype(BF16)
        f = jnp.zeros((tm, D_MODEL), F32)
        for k in range(2):
            gate = _dot(h, win_v[k])
            up = _dot(h, win_v[k + 2])
            z_ref[:, k * FF_BLK:(k + 1) * FF_BLK] = gate.astype(BF16)
            z_ref[:, D_FF + k * FF_BLK:D_FF + (k + 1) * FF_BLK] = up.astype(BF16)
            f = f + _dot(gate * jax.nn.sigmoid(gate) * up, wout_v[k])
        f_ref[...] = f
        out = xv + 0.5 * (f * _rms(f) * gpost_ref[...])
        if with_loss:
            err = out - tgt_ref[...]
            o_ref[...] = err * (1.0 / D_MODEL)
            loss_ref[...] += jnp.sum(err * err) * (0.5 / D_MODEL)
        else:
            o_ref[...] = out

    row = _rows(tm, D_MODEL)
    vec = _full((1, D_MODEL))
    in_specs = [row, vec, vec] + ([row] if with_loss else []) + [ANY, ANY]
    out_shape = [jax.ShapeDtypeStruct((T, D_MODEL), F32)]
    out_specs = [row]
    if with_loss:
        out_shape.append(jax.ShapeDtypeStruct((8, 128), F32))
        out_specs.append(_full((8, 128)))
    out_shape += [jax.ShapeDtypeStruct((T, 2 * D_FF), BF16), jax.ShapeDtypeStruct((T, D_MODEL), F32)]
    out_specs += [_rows(tm, 2 * D_FF), row]
    args = (x, g_pre, g_post) + ((target,) if with_loss else ()) + (w_in, w_out)
    return _pallas(
        body, name=name, grid=(T // tm,), in_specs=in_specs, out_specs=out_specs, out_shape=out_shape,
        scratch_shapes=[pltpu.VMEM(w_in.shape, BF16), pltpu.VMEM(w_out.shape, BF16), pltpu.SemaphoreType.DMA((2,))],
        semantics=("arbitrary",), args=args, rider=rider)


def _ffn_bwd_out(dout, f, z, w_out, g_post, *, name):
    T = dout.shape[0]
    tm = ROW_TILE
    nt = T // tm

    def body(dout_ref, f_ref, z_ref, gpost_ref, wout_hbm, dz_ref, dgpost_ref, dwout_hbm, wout_v, dwout_v, sems):
        i = pl.program_id(0)

        @pl.when(i == 0)
        def _():
            _load_once([(wout_hbm, wout_v)], sems)
            dwout_v[...] = jnp.zeros_like(dwout_v)
            dgpost_ref[...] = jnp.zeros_like(dgpost_ref)

        dy = 0.5 * dout_ref[...]
        f = f_ref[...]
        r = _rms(f)
        fn = f * r
        dgpost_ref[...] += jnp.sum(dy * fn, axis=0, keepdims=True)
        df = _rms_bwd(dy * gpost_ref[...], fn, r).astype(BF16)
        for k in range(2):
            gate = z_ref[:, k * FF_BLK:(k + 1) * FF_BLK].astype(F32)
            up = z_ref[:, D_FF + k * FF_BLK:D_FF + (k + 1) * FF_BLK].astype(F32)
            sg = jax.nn.sigmoid(gate)
            silu = gate * sg
            dwout_v[k] += _dot_tn(silu * up, df)
            da = _dot_nt(df, wout_v[k])
            dz_ref[:, k * FF_BLK:(k + 1) * FF_BLK] = (da * up * (sg * (1.0 + gate * (1.0 - sg)))).astype(BF16)
            dz_ref[:, D_FF + k * FF_BLK:D_FF + (k + 1) * FF_BLK] = (da * silu).astype(BF16)

        @pl.when(i == nt - 1)
        def _():
            c = pltpu.make_async_copy(dwout_v, dwout_hbm, sems.at[0])
            c.start()
            c.wait()

    row = _rows(tm, D_MODEL)
    return pl.pallas_call(
        body, name=name, grid=(nt,),
        in_specs=[row, row, _rows(tm, 2 * D_FF), _full((1, D_MODEL)), ANY],
        out_specs=[_rows(tm, 2 * D_FF), _full((1, D_MODEL)), ANY],
        out_shape=[jax.ShapeDtypeStruct((T, 2 * D_FF), BF16), jax.ShapeDtypeStruct((1, D_MODEL), F32),
                   jax.ShapeDtypeStruct(w_out.shape, F32)],
        scratch_shapes=[pltpu.VMEM(w_out.shape, BF16), pltpu.VMEM(w_out.shape, F32), pltpu.SemaphoreType.DMA((1,))],
        compiler_params=_params("arbitrary"),
    )(dout, f, z, g_post, w_out)


def _norm_matmul_bwd(dz, x, dres, g, w, *, name, rider=None):
    T = x.shape[0]
    nb, _, bw = w.shape
    tm = ROW_TILE
    nt = T // tm

    def body(dz_ref, x_ref, dres_ref, g_ref, w_hbm, dx_ref, dg_ref, dw_hbm, w_v, dw_v, sems):
        i = pl.program_id(0)

        @pl.when(i == 0)
        def _():
            _load_once([(w_hbm, w_v)], sems)
            dw_v[...] = jnp.zeros_like(dw_v)
            dg_ref[...] = jnp.zeros_like(dg_ref)

        xv = x_ref[...]
        r = _rms(xv)
        xn = xv * r
        gv = g_ref[...]
        h = (xn * gv).astype(BF16)
        dh = jnp.zeros((tm, D_MODEL), F32)
        for j in range(nb):
            dzj = dz_ref[:, j * bw:(j + 1) * bw]
            dw_v[j] += _dot_tn(h, dzj)
            dh = dh + _dot_nt(dzj, w_v[j])
        dg_ref[...] += jnp.sum(dh * xn, axis=0, keepdims=True)
        dx_ref[...] = _rms_bwd(dh * gv, xn, r) + dres_ref[...]

        @pl.when(i == nt - 1)
        def _():
            c = pltpu.make_async_copy(dw_v, dw_hbm, sems.at[0])
            c.start()
            c.wait()

    row = _rows(tm, D_MODEL)
    return _pallas(
        body, name=name, grid=(nt,),
        in_specs=[_rows(tm, nb * bw), row, row, _full((1, D_MODEL)), ANY],
        out_specs=[row, _full((1, D_MODEL)), ANY],
        out_shape=[jax.ShapeDtypeStruct((T, D_MODEL), F32), jax.ShapeDtypeStruct((1, D_MODEL), F32),
                   jax.ShapeDtypeStruct(w.shape, F32)],
        scratch_shapes=[pltpu.VMEM(w.shape, BF16), pltpu.VMEM(w.shape, F32), pltpu.SemaphoreType.DMA((1,))],
        semantics=("arbitrary",), args=(dz, x, dres, g, w), rider=rider)


def _mix_in_fwd(x, g, w):
    T = x.shape[0]
    tm = ROW_TILE

    def body(x_ref, g_ref, w_ref, q_ref, k_ref, v_ref, u_ref):
        xv = x_ref[...]
        h = (xv * _rms(xv) * g_ref[...]).astype(BF16)
        for j, o_ref in enumerate((q_ref, k_ref, v_ref, u_ref)):
            o_ref[...] = _dot(h, w_ref[j])

    col = _rows(tm, ATTN_W)
    return pl.pallas_call(
        body, name="mix_in_fwd", grid=(T // tm,),
        in_specs=[_rows(tm, D_MODEL), _full((1, D_MODEL)), _full(w.shape)],
        out_specs=[col] * 4, out_shape=[jax.ShapeDtypeStruct((T, ATTN_W), F32)] * 4,
        compiler_params=_params("parallel"),
    )(x, g, w)


def _mix_out_fwd(x, attn, ssm, w, g):
    T = x.shape[0]
    tm = ROW_TILE

    def body(x_ref, a_ref, s_ref, w_ref, g_ref, o_ref, m_ref):
        mixed = _dot(a_ref[...], w_ref[0]) + _dot(s_ref[...], w_ref[1])
        m_ref[...] = mixed
        o_ref[...] = x_ref[...] + mixed * _rms(mixed) * g_ref[...]

    row, col = _rows(tm, D_MODEL), _rows(tm, ATTN_W)
    return pl.pallas_call(
        body, name="mix_out_fwd", grid=(T // tm,),
        in_specs=[row, col, col, _full(w.shape), _full((1, D_MODEL))],
        out_specs=[row, row], out_shape=[jax.ShapeDtypeStruct((T, D_MODEL), F32)] * 2,
        compiler_params=_params("parallel"),
    )(x, attn, ssm, w, g)


def _mix_out_bwd(dout, mixed, attn, ssm, w, g, rider=None):
    T = dout.shape[0]
    tm = ROW_TILE

    def body(dout_ref, m_ref, a_ref, s_ref, w_ref, g_ref, da_ref, ds_ref, dw_ref, dg_ref):
        @pl.when(pl.program_id(0) == 0)
        def _():
            dw_ref[...] = jnp.zeros_like(dw_ref)
            dg_ref[...] = jnp.zeros_like(dg_ref)

        dy = dout_ref[...]
        mixed = m_ref[...]
        r = _rms(mixed)
        mn = mixed * r
        dg_ref[...] += jnp.sum(dy * mn, axis=0, keepdims=True)
        dm = _rms_bwd(dy * g_ref[...], mn, r).astype(BF16)
        da_ref[...] = _dot_nt(dm, w_ref[0])
        ds_ref[...] = _dot_nt(dm, w_ref[1])
        dw_ref[0] += _dot_tn(a_ref[...], dm)
        dw_ref[1] += _dot_tn(s_ref[...], dm)

    row, col = _rows(tm, D_MODEL), _rows(tm, ATTN_W)
    return _pallas(
        body, name="mix_out_bwd", grid=(T // tm,),
        in_specs=[row, row, col, col, _full(w.shape), _full((1, D_MODEL))],
        out_specs=[col, col, _full(w.shape), _full((1, D_MODEL))],
        out_shape=[jax.ShapeDtypeStruct((T, ATTN_W), F32)] * 2
        + [jax.ShapeDtypeStruct(w.shape, F32), jax.ShapeDtypeStruct((1, D_MODEL), F32)],
        semantics=("arbitrary",), args=(dout, mixed, attn, ssm, w, g), rider=rider)


ATTN_TILING = {1: (ATTN_W, 1), 4: (2 * HEAD_DIM, 4), 16: (2 * HEAD_DIM, 4)}


def _class_rows(d, r):
    return (pl.ds(r, QBLK, stride=d), slice(None)) if d > 1 else (slice(None), slice(None))


def _for_class_groups(d, group, fn):
    if d == group:
        fn(0)
    else:
        lax.fori_loop(0, d // group, lambda n, carry: (fn(n * group), carry)[1], 0)


def _block_slopes(lanes):
    heads = lanes // HEAD_DIM
    first = pl.program_id(1) * heads
    return [jnp.exp2(-jnp.full((1, 1), first + hh + 1, jnp.int32).astype(F32)) for hh in range(heads)]


def _attn_specs(d, nb, lanes):
    blk = (QBLK * d, lanes)
    cur = pl.BlockSpec(blk, lambda j, lb: (j, lb))
    prev = pl.BlockSpec(blk, lambda j, lb: (jnp.maximum(j - 1, 0), lb))
    nxt = pl.BlockSpec(blk, lambda j, lb: (jnp.minimum(j + 1, nb - 1), lb))
    return cur, prev, nxt


def _attn_branch_fwd(q, k, v, d):
    T = q.shape[0]
    nb = T // (d * QBLK)
    lanes, group = ATTN_TILING[d]
    scale = HEAD_DIM ** -0.5

    def body(q_ref, kc_ref, kp_ref, vc_ref, vp_ref, o_ref, l_ref, q_s, kk_s, vv_s, o_s, l_s):
        j = pl.program_id(0)
        qi = lax.broadcasted_iota(jnp.int32, (QBLK, 2 * QBLK), 0)
        ci = lax.broadcasted_iota(jnp.int32, (QBLK, 2 * QBLK), 1)
        steps = QBLK + qi - ci
        valid = (steps >= 0) & (steps <= N_BACK) & ((ci >= QBLK) | (j > 0))
        dist = (steps * d).astype(F32)
        slopes = _block_slopes(lanes)

        def classes(first):
            for n in range(group):
                rows = _class_rows(d, first + n)
                q_s[n] = q_ref[rows]
                kk_s[n, :QBLK], kk_s[n, QBLK:] = kp_ref[rows], kc_ref[rows]
                vv_s[n, :QBLK], vv_s[n, QBLK:] = vp_ref[rows], vc_ref[rows]
            for n in range(group):
                for hh in range(lanes // HEAD_DIM):
                    sl = slice(hh * HEAD_DIM, (hh + 1) * HEAD_DIM)
                    s = _dot_nt(q_s[n, :, sl], kk_s[n, :, sl]) * scale - slopes[hh] * dist
                    s = jnp.where(valid, s, NEG)
                    m = jnp.max(s, axis=-1, keepdims=True)
                    p = jnp.exp(s - m)
                    den = jnp.sum(p, axis=-1, keepdims=True)
                    o_s[n, :, sl] = _dot(p, vv_s[n, :, sl]) / den
                    l_s[n, :, sl] = jnp.broadcast_to(m + jnp.log(den), (QBLK, HEAD_DIM))
            for n in range(group):
                rows = _class_rows(d, first + n)
                o_ref[rows] = o_s[n]
                l_ref[rows] = l_s[n]

        _for_class_groups(d, group, classes)

    cur, prev, _ = _attn_specs(d, nb, lanes)
    shape = jax.ShapeDtypeStruct((T, ATTN_W), F32)
    one, two = pltpu.VMEM((group, QBLK, lanes), F32), pltpu.VMEM((group, 2 * QBLK, lanes), F32)
    return pl.pallas_call(
        body, name=f"attn_fwd_d{d}", grid=(nb, ATTN_W // lanes),
        in_specs=[cur, cur, prev, cur, prev], out_specs=[cur, cur], out_shape=[shape, shape],
        scratch_shapes=[one, two, two, one, one], compiler_params=_params("parallel", "parallel"),
    )(q, k, k, v, v)


def _attn_merge(outs, lses):
    T = outs[0].shape[0]
    tm = 512

    def body(o1, o2, o3, l1, l2, l3, a_ref, lse_ref):
        ls = [l1[...], l2[...], l3[...]]
        m = jnp.maximum(jnp.maximum(ls[0], ls[1]), ls[2])
        lse = m + jnp.log(jnp.exp(ls[0] - m) + jnp.exp(ls[1] - m) + jnp.exp(ls[2] - m))
        lse_ref[...] = lse
        a_ref[...] = jnp.exp(ls[0] - lse) * o1[...] + jnp.exp(ls[1] - lse) * o2[...] + jnp.exp(ls[2] - lse) * o3[...]

    col = _rows(tm, ATTN_W)
    return pl.pallas_call(
        body, name="attn_merge", grid=(T // tm,), in_specs=[col] * 6, out_specs=[col, col],
        out_shape=[jax.ShapeDtypeStruct((T, ATTN_W), F32)] * 2, compiler_params=_params("parallel"),
    )(*outs, *lses)


def _attn_branch_bwd(q, k, v, o, lse, do, d, rider=None):
    T = q.shape[0]
    nb = T // (d * QBLK)
    lanes, group = ATTN_TILING[d]
    scale = HEAD_DIM ** -0.5

    def body(qc_ref, qn_ref, kc_ref, kp_ref, vc_ref, vp_ref, oc_ref, on_ref, lc_ref, ln_ref, doc_ref, don_ref,
             dq_ref, dk_ref, dv_ref, qq_s, kk_s, vv_s, oo_s, ll_s, doo_s, dq_s, dk_s, dv_s):
        j = pl.program_id(0)
        qi = lax.broadcasted_iota(jnp.int32, (QBLK, 2 * QBLK), 0)
        ci = lax.broadcasted_iota(jnp.int32, (QBLK, 2 * QBLK), 1)
        steps_q = QBLK + qi - ci
        valid_q = (steps_q >= 0) & (steps_q <= N_BACK) & ((ci >= QBLK) | (j > 0))
        dist_q = (steps_q * d).astype(F32)
        ri = lax.broadcasted_iota(jnp.int32, (2 * QBLK, QBLK), 0)
        ki = lax.broadcasted_iota(jnp.int32, (2 * QBLK, QBLK), 1)
        steps_k = ri - ki
        valid_k = (steps_k >= 0) & (steps_k <= N_BACK) & ((ri < QBLK) | (j < nb - 1))
        dist_k = (steps_k * d).astype(F32)
        lo, hi = slice(0, QBLK), slice(QBLK, 2 * QBLK)
        slopes = _block_slopes(lanes)

        def classes(first):
            for n in range(group):
                rows = _class_rows(d, first + n)
                qq_s[n, lo], qq_s[n, hi] = qc_ref[rows], qn_ref[rows]
                oo_s[n, lo], oo_s[n, hi] = oc_ref[rows], on_ref[rows]
                ll_s[n, lo], ll_s[n, hi] = lc_ref[rows], ln_ref[rows]
                doo_s[n, lo], doo_s[n, hi] = doc_ref[rows], don_ref[rows]
                kk_s[n, lo], kk_s[n, hi] = kp_ref[rows], kc_ref[rows]
                vv_s[n, lo], vv_s[n, hi] = vp_ref[rows], vc_ref[rows]
            for n in range(group):
                for hh in range(lanes // HEAD_DIM):
                    sl = slice(hh * HEAD_DIM, (hh + 1) * HEAD_DIM)
                    slope = slopes[hh]
                    qq, doo, kk, vv = qq_s[n, :, sl], doo_s[n, :, sl], kk_s[n, :, sl], vv_s[n, :, sl]
                    lse2 = ll_s[n, :, hh * HEAD_DIM:hh * HEAD_DIM + 1]
                    delta2 = jnp.sum(doo * oo_s[n, :, sl], axis=-1, keepdims=True)

                    s = jnp.where(valid_q, _dot_nt(qq[lo], kk) * scale - slope * dist_q, NEG)
                    p = jnp.exp(s - lse2[lo])
                    ds = p * (_dot_nt(doo[lo], vv) - delta2[lo])
                    dq_s[n, :, sl] = _dot(ds, kk) * scale

                    s2 = jnp.where(valid_k, _dot_nt(qq, kk[hi]) * scale - slope * dist_k, NEG)
                    p2 = jnp.exp(s2 - lse2)
                    dv_s[n, :, sl] = _dot_tn(p2, doo)
                    ds2 = p2 * (_dot_nt(doo, vv[hi]) - delta2)
                    dk_s[n, :, sl] = _dot_tn(ds2, qq) * scale
            for n in range(group):
                rows = _class_rows(d, first + n)
                dq_ref[rows] = dq_s[n]
                dk_ref[rows] = dk_s[n]
                dv_ref[rows] = dv_s[n]

        _for_class_groups(d, group, classes)

    cur, prev, nxt = _attn_specs(d, nb, lanes)
    shape = jax.ShapeDtypeStruct((T, ATTN_W), F32)
    one, two = pltpu.VMEM((group, QBLK, lanes), F32), pltpu.VMEM((group, 2 * QBLK, lanes), F32)
    return _pallas(
        body, name=f"attn_bwd_d{d}", grid=(nb, ATTN_W // lanes),
        in_specs=[cur, nxt, cur, prev, cur, prev, cur, nxt, cur, nxt, cur, nxt],
        out_specs=[cur] * 3, out_shape=[shape] * 3, scratch_shapes=[two] * 6 + [one] * 3,
        semantics=("parallel", "parallel"), args=(q, q, k, k, v, v, o, o, lse, lse, do, do), rider=rider)


def _dproj_merge(dqs, dks, dvs, du):
    T = du.shape[0]
    tm = 512

    def body(*refs):
        o_ref = refs[-1]
        for part in range(3):
            a, b, c = refs[3 * part:3 * part + 3]
            o_ref[:, part * ATTN_W:(part + 1) * ATTN_W] = (a[...] + b[...] + c[...]).astype(BF16)
        o_ref[:, 3 * ATTN_W:] = refs[9][...].astype(BF16)

    col = _rows(tm, ATTN_W)
    return pl.pallas_call(
        body, name="dproj_merge", grid=(T // tm,), in_specs=[col] * 10, out_specs=_rows(tm, 4 * ATTN_W),
        out_shape=jax.ShapeDtypeStruct((T, 4 * ATTN_W), BF16), compiler_params=_params("parallel"),
    )(*dqs, *dks, *dvs, du)


def _attention_fwd(q, k, v):
    res = [_attn_branch_fwd(q, k, v, d) for d in DILATIONS]
    return _attn_merge([r[0] for r in res], [r[1] for r in res])


def _attention_bwd(q, k, v, attn, lse, dattn):
    res = [_attn_branch_bwd(q, k, v, attn, lse, dattn, d) for d in DILATIONS]
    return [r[0] for r in res], [r[1] for r in res], [r[2] for r in res]


SCAN_ROWS = 8
SCAN_LANES = 512
SSM_CHUNK = 256


def _cmul(ar, ai, br, bi):
    return ar * br - ai * bi, ar * bi + ai * br


def _ssm_discretize(a_re, a_im, log_dt, b_re, b_im):
    def body(ar_ref, ai_ref, ldt_ref, br_ref, bi_ref, abr_ref, abi_ref, er_ref, ei_ref, bbr_ref, bbi_ref, pr_ref, pi_ref):
        ar, ai = ar_ref[...], ai_ref[...]
        dt = jnp.exp(ldt_ref[...])
        n = lax.broadcasted_iota(jnp.int32, (1, SCAN_ROWS), 1).astype(F32) + 1.0
        mag, ang = jnp.exp(dt * ar), dt * ai
        abr, abi = mag * jnp.cos(ang), mag * jnp.sin(ang)
        abr_ref[...], abi_ref[...] = abr, abi
        pr_ref[...] = jnp.exp(dt * ar * n) * jnp.cos(ang * n)
        pi_ref[...] = jnp.exp(dt * ar * n) * jnp.sin(ang * n)
        den = ar * ar + ai * ai
        er = ((abr - 1.0) * ar + abi * ai) / den
        ei = (abi * ar - (abr - 1.0) * ai) / den
        er_ref[...], ei_ref[...] = er, ei
        bbr_ref[...], bbi_ref[...] = _cmul(er, ei, br_ref[...], bi_ref[...])

    col = jax.ShapeDtypeStruct((N_STATE, 1), F32)
    mat = jax.ShapeDtypeStruct((N_STATE, GROUP_CH), F32)
    pw = jax.ShapeDtypeStruct((N_STATE, SCAN_ROWS), F32)
    return pl.pallas_call(body, name="ssm_discretize", out_shape=[col] * 4 + [mat] * 2 + [pw] * 2)(
        a_re, a_im, log_dt, b_re, b_im)


def _ssm_discretize_bwd(a_re, a_im, log_dt, b_re, b_im, ab_re, ab_im, e_re, e_im, dab_re, dab_im, dbb_re, dbb_im):
    def body(ar_ref, ai_ref, ldt_ref, br_ref, bi_ref, abr_ref, abi_ref, er_ref, ei_ref, dabr_ref, dabi_ref,
             dbbr_ref, dbbi_ref, dar_ref, dai_ref, ddt_ref, dbr_ref, dbi_ref):
        ar, ai, dt = ar_ref[...], ai_ref[...], jnp.exp(ldt_ref[...])
        er, ei = er_ref[...], ei_ref[...]
        gbr, gbi = dbbr_ref[...], dbbi_ref[...]
        dbr_ref[...], dbi_ref[...] = _cmul(er, -ei, gbr, gbi)
        br, bi = br_ref[...], bi_ref[...]
        der = jnp.sum(br * gbr + bi * gbi, axis=-1, keepdims=True)
        dei = jnp.sum(br * gbi - bi * gbr, axis=-1, keepdims=True)
        den = ar * ar + ai * ai
        inv_r, inv_i = ar / den, -ai / den
        t_r, t_i = _cmul(der, dei, inv_r, -inv_i)
        gab_r, gab_i = dabr_ref[...] + t_r, dabi_ref[...] + t_i
        q_r, q_i = _cmul(er, ei, inv_r, inv_i)
        dl_r, dl_i = _cmul(der, dei, q_r, -q_i)
        dl_r, dl_i = -dl_r, -dl_i
        gw_r, gw_i = _cmul(gab_r, gab_i, abr_ref[...], -abi_ref[...])
        dar_ref[...] = dl_r + dt * gw_r
        dai_ref[...] = dl_i + dt * gw_i
        ddt_ref[...] = (gw_r * ar + gw_i * ai) * dt

    col = jax.ShapeDtypeStruct((N_STATE, 1), F32)
    mat = jax.ShapeDtypeStruct((N_STATE, GROUP_CH), F32)
    return pl.pallas_call(body, name="ssm_discretize_bwd", out_shape=[col] * 3 + [mat] * 2)(
        a_re, a_im, log_dt, b_re, b_im, ab_re, ab_im, e_re, e_im, dab_re, dab_im, dbb_re, dbb_im)


def _scan_tables(p_re, p_im, reverse):
    pr, pi = p_re.T, p_im.T
    if reverse:
        pi = -pi
    bc = lambda t, n: jnp.broadcast_to(t[n - 1], (SCAN_ROWS, N_STATE))
    carry = (pr[::-1], pi[::-1]) if reverse else (pr, pi)
    return jnp.stack([bc(pr, 1), bc(pi, 1), bc(pr, 2), bc(pi, 2), bc(pr, 4), bc(pi, 4), carry[0], carry[1]])


def _scan_group(xr, xi, tab_ref, ls, carry_r, carry_i, reverse):
    row = lax.broadcasted_iota(jnp.int32, (SCAN_ROWS, SCAN_LANES), 0)
    for n, s in enumerate((1, 2, 4)):
        if reverse:
            keep = row < SCAN_ROWS - s
            shr, shi = pltpu.roll(xr, SCAN_ROWS - s, 0), pltpu.roll(xi, SCAN_ROWS - s, 0)
        else:
            keep = row >= s
            shr, shi = pltpu.roll(xr, s, 0), pltpu.roll(xi, s, 0)
        shr, shi = jnp.where(keep, shr, 0.0), jnp.where(keep, shi, 0.0)
        mr, mi = _cmul(tab_ref[2 * n, :, ls], tab_ref[2 * n + 1, :, ls], shr, shi)
        xr, xi = xr + mr, xi + mi
    mr, mi = _cmul(tab_ref[6, :, ls], tab_ref[7, :, ls], carry_r, carry_i)
    return xr + mr, xi + mi


def _gelu(y):
    t = jnp.tanh(GELU_C * (y + 0.044715 * y * y * y))
    return 0.5 * y * (1.0 + t), t


def _ssm_fwd(u, tab, bd_re, bd_im, cd_re, cd_im, d_skip, w_glu, b_glu):
    T = u.shape[0]
    tc = SSM_CHUNK

    def body(u_ref, tab_ref, bdr_ref, bdi_ref, cdr_ref, cdi_ref, dsk_ref, wg_ref, bg_ref,
             sr_ref, si_ref, yp_ref, o_ref, car_r, car_i):
        @pl.when(pl.program_id(0) == 0)
        def _():
            car_r[...] = jnp.zeros_like(car_r)
            car_i[...] = jnp.zeros_like(car_i)

        uv = u_ref[...]
        sr_ref[...] = _dot(uv, bdr_ref[...])
        si_ref[...] = _dot(uv, bdi_ref[...])
        for lb in range(N_STATE // SCAN_LANES):
            ls = pl.ds(lb * SCAN_LANES, SCAN_LANES)

            def step(g, carry):
                rows = pl.ds(pl.multiple_of(g * SCAN_ROWS, SCAN_ROWS), SCAN_ROWS)
                xr, xi = _scan_group(sr_ref[rows, ls], si_ref[rows, ls], tab_ref, ls, carry[0], carry[1], False)
                sr_ref[rows, ls] = xr
                si_ref[rows, ls] = xi
                last = slice(SCAN_ROWS - 1, SCAN_ROWS)
                return (jnp.broadcast_to(xr[last], xr.shape), jnp.broadcast_to(xi[last], xi.shape))

            cr, ci = lax.fori_loop(0, tc // SCAN_ROWS, step, (car_r[:, ls], car_i[:, ls]))
            car_r[:, ls] = cr
            car_i[:, ls] = ci
        y = _dot(sr_ref[...], cdr_ref[...]) - _dot(si_ref[...], cdi_ref[...]) + dsk_ref[...] * uv
        yp_ref[...] = y
        gy, _ = _gelu(y)
        o_ref[...] = gy * jax.nn.sigmoid(_dot(gy, wg_ref[...]) + bg_ref[...])

    col, st = _rows(tc, SSM_W), _rows(tc, N_STATE)
    vec = _full((1, SSM_W))
    return pl.pallas_call(
        body, name="ssm_fwd", grid=(T // tc,),
        in_specs=[col, _full(tab.shape), _full(bd_re.shape), _full(bd_im.shape), _full(cd_re.shape), _full(cd_im.shape),
                  vec, _full(w_glu.shape), vec],
        out_specs=[st, st, col, col],
        out_shape=[jax.ShapeDtypeStruct((T, N_STATE), F32)] * 2 + [jax.ShapeDtypeStruct((T, SSM_W), F32)] * 2,
        scratch_shapes=[pltpu.VMEM((SCAN_ROWS, N_STATE), F32)] * 2,
        compiler_params=_params("arbitrary"),
    )(u, tab, bd_re, bd_im, cd_re, cd_im, d_skip, w_glu, b_glu)


def _ssm_bwd(dout, u, yp, s_re, s_im, tab, bd_re, bd_im, cd_re, cd_im, d_skip, w_glu, b_glu, rider=None):
    T = u.shape[0]
    tc = SSM_CHUNK
    nt = T // tc
    rows_per_chunk = tc // SCAN_ROWS

    def body(do_ref, u_ref, yp_ref, sr_ref, si_ref, pr_ref, pi_ref, tab_ref, dsk_ref, wg_ref, bg_ref,
             bdr_hbm, bdi_hbm, cdr_hbm, cdi_hbm,
             du_ref, dsk_out, dbg_out, dwg_out, da_out, dbdr_hbm, dbdi_hbm, dcdr_hbm, dcdi_hbm,
             bdr_v, bdi_v, cdr_v, cdi_v, dbdr_v, dbdi_v, dcdr_v, dcdi_v, gr_v, gi_v, car_r, car_i, sems):
        i = pl.program_id(0)

        @pl.when(i == 0)
        def _():
            _load_once([(bdr_hbm, bdr_v), (bdi_hbm, bdi_v), (cdr_hbm, cdr_v), (cdi_hbm, cdi_v)], sems)
            for ref in (dbdr_v, dbdi_v, dcdr_v, dcdi_v, car_r, car_i, dsk_out, dbg_out, dwg_out, da_out):
                ref[...] = jnp.zeros_like(ref)

        uv, y, dout_v = u_ref[...], yp_ref[...], do_ref[...]
        gy, t = _gelu(y)
        sg = jax.nn.sigmoid(_dot(gy, wg_ref[...]) + bg_ref[...])
        dzg = dout_v * gy * sg * (1.0 - sg)
        dgy = dout_v * sg + _dot_nt(dzg, wg_ref[...])
        dwg_out[...] += _dot_tn(gy, dzg)
        dbg_out[...] += jnp.sum(dzg, axis=0, keepdims=True)
        dy = dgy * (0.5 * (1.0 + t) + 0.5 * y * (1.0 - t * t) * GELU_C * (1.0 + 3 * 0.044715 * y * y))
        dsk_out[...] += jnp.sum(dy * uv, axis=0, keepdims=True)

        gr_v[...] = _dot_nt(dy, cdr_v[...])
        gi_v[...] = -_dot_nt(dy, cdi_v[...])
        dcdr_v[...] += _dot_tn(sr_ref[...], dy)
        dcdi_v[...] -= _dot_tn(si_ref[...], dy)

        row = lax.broadcasted_iota(jnp.int32, (SCAN_ROWS, SCAN_LANES), 0)
        first_chunk = i == nt - 1
        for lb in range(N_STATE // SCAN_LANES):
            ls = pl.ds(lb * SCAN_LANES, SCAN_LANES)

            def step(n, carry):
                g = rows_per_chunk - 1 - n
                rows = pl.ds(pl.multiple_of(g * SCAN_ROWS, SCAN_ROWS), SCAN_ROWS)
                before = pl.ds(pl.multiple_of(jnp.maximum(g - 1, 0) * SCAN_ROWS, SCAN_ROWS), SCAN_ROWS)
                xr, xi = _scan_group(gr_v[rows, ls], gi_v[rows, ls], tab_ref, ls, carry[0], carry[1], True)
                gr_v[rows, ls] = xr
                gi_v[rows, ls] = xi
                last = slice(SCAN_ROWS - 1, SCAN_ROWS)
                edge_r = jnp.where(g > 0, sr_ref[before, ls][last], jnp.where(first_chunk, 0.0, pr_ref[:, ls][last]))
                edge_i = jnp.where(g > 0, si_ref[before, ls][last], jnp.where(first_chunk, 0.0, pi_ref[:, ls][last]))
                spr = jnp.where(row >= 1, pltpu.roll(sr_ref[rows, ls], 1, 0), edge_r)
                spi = jnp.where(row >= 1, pltpu.roll(si_ref[rows, ls], 1, 0), edge_i)
                first = slice(0, 1)
                return (jnp.broadcast_to(xr[first], xr.shape), jnp.broadcast_to(xi[first], xi.shape),
                        carry[2] + xr * spr + xi * spi, carry[3] + xi * spr - xr * spi)

            zero = jnp.zeros((SCAN_ROWS, SCAN_LANES), F32)
            cr, ci, dar, dai = lax.fori_loop(0, rows_per_chunk, step, (car_r[:, ls], car_i[:, ls], zero, zero))
            car_r[:, ls] = cr
            car_i[:, ls] = ci
            da_out[0, :, ls] += dar
            da_out[1, :, ls] += dai

        du_ref[...] = dsk_ref[...] * dy + _dot_nt(gr_v[...], bdr_v[...]) + _dot_nt(gi_v[...], bdi_v[...])
        dbdr_v[...] += _dot_tn(uv, gr_v[...])
        dbdi_v[...] += _dot_tn(uv, gi_v[...])

        @pl.when(i == nt - 1)
        def _():
            outs = [(dbdr_v, dbdr_hbm), (dbdi_v, dbdi_hbm), (dcdr_v, dcdr_hbm), (dcdi_v, dcdi_hbm)]
            copies = [pltpu.make_async_copy(src, dst, sems.at[k]) for k, (src, dst) in enumerate(outs)]
            for c in copies:
                c.start()
            for c in copies:
                c.wait()

    rev = lambda i: (nt - 1 - i, 0)
    col = pl.BlockSpec((tc, SSM_W), rev)
    st = pl.BlockSpec((tc, N_STATE), rev)
    st_before = pl.BlockSpec((SCAN_ROWS, N_STATE), lambda i: (jnp.maximum((nt - 1 - i) * rows_per_chunk - 1, 0), 0))
    vec = _full((1, SSM_W))
    bd = jax.ShapeDtypeStruct(bd_re.shape, F32)
    cd = jax.ShapeDtypeStruct(cd_re.shape, F32)
    return _pallas(
        body, name="ssm_bwd", grid=(nt,),
        in_specs=[col, col, col, st, st, st_before, st_before, _full(tab.shape), vec, _full(w_glu.shape), vec,
                  ANY, ANY, ANY, ANY],
        out_specs=[col, vec, vec, _full(w_glu.shape), _full((2, SCAN_ROWS, N_STATE)), ANY, ANY, ANY, ANY],
        out_shape=[jax.ShapeDtypeStruct((T, SSM_W), F32), jax.ShapeDtypeStruct((1, SSM_W), F32),
                   jax.ShapeDtypeStruct((1, SSM_W), F32), jax.ShapeDtypeStruct(w_glu.shape, F32),
                   jax.ShapeDtypeStruct((2, SCAN_ROWS, N_STATE), F32), bd, bd, cd, cd],
        scratch_shapes=[pltpu.VMEM(bd_re.shape, BF16)] * 2 + [pltpu.VMEM(cd_re.shape, BF16)] * 2
        + [pltpu.VMEM(bd_re.shape, F32)] * 2 + [pltpu.VMEM(cd_re.shape, F32)] * 2
        + [pltpu.VMEM((tc, N_STATE), F32)] * 2 + [pltpu.VMEM((SCAN_ROWS, N_STATE), F32)] * 2
        + [pltpu.SemaphoreType.DMA((4,))],
        semantics=("arbitrary",), rider=rider,
        args=(dout, u, yp, s_re, s_im, s_re, s_im, tab, d_skip, w_glu, b_glu, bd_re, bd_im, cd_re, cd_im))


def _block_diag(t):
    g, a, b = t.shape
    eye = jnp.eye(N_GROUPS, dtype=t.dtype)
    return (t[:, :, None, :] * eye[:, None, :, None]).reshape(g * a, g * b)


def _diag_blocks(m, a, b):
    eye = jnp.eye(N_GROUPS, dtype=m.dtype)
    return jnp.sum(m.reshape(N_GROUPS, a, N_GROUPS, b) * eye[:, None, :, None], axis=2)


def _ssm_prepare(a_re, a_im, log_dt, b_re, b_im, c_re, c_im):
    col = lambda t: t.reshape(N_STATE, 1)
    ldt = jnp.broadcast_to(log_dt.reshape(N_GROUPS, 1), (N_GROUPS, STATE)).reshape(N_STATE, 1)
    b2r, b2i = b_re.reshape(N_STATE, GROUP_CH), b_im.reshape(N_STATE, GROUP_CH)
    ab_r, ab_i, e_r, e_i, bb_r, bb_i, p_r, p_i = _ssm_discretize(col(a_re), col(a_im), ldt, b2r, b2i)
    bd = [_block_diag(jnp.swapaxes(t.reshape(N_GROUPS, STATE, GROUP_CH), 1, 2)).astype(BF16) for t in (bb_r, bb_i)]
    cd = [_block_diag(jnp.swapaxes(t.reshape(N_GROUPS, GROUP_CH, STATE), 1, 2)).astype(BF16) for t in (c_re, c_im)]
    saved = dict(a_re=col(a_re), a_im=col(a_im), log_dt=ldt, b_re=b2r, b_im=b2i, ab_re=ab_r, ab_im=ab_i, e_re=e_r, e_im=e_i)
    return _scan_tables(p_r, p_i, False), _scan_tables(p_r, p_i, True), bd, cd, saved


BIG = ("ffn1_w_in", "ffn1_w_out", "w_mix_in", "w_glu", "w_mix_out", "ffn2_w_in", "ffn2_w_out")
SMALL = ("ffn1_pre_g", "ffn1_post_g", "mix_pre_g", "a_re", "a_im", "log_dt", "b_re", "b_im", "c_re", "c_im",
         "d_skip", "b_glu", "mix_post_g", "ffn2_pre_g", "ffn2_post_g")


FIRST = ("ffn1_w_in", "ffn1_w_out")
REST = ("w_mix_in", "w_glu", "w_mix_out", "ffn2_w_in", "ffn2_w_out")
LATE = ("w_mix_in", "w_glu", "w_mix_out", "ffn1_w_out")
SHARD_SHAPE = {"ffn1_w_in": (D_MODEL, FF_BLK), "ffn2_w_in": (D_MODEL, FF_BLK), "ffn1_w_out": (D_FF // N_CHIPS, D_MODEL),
               "ffn2_w_out": (D_FF // N_CHIPS, D_MODEL), "w_mix_in": (D_MODEL, ATTN_W), "w_glu": (SSM_W // N_CHIPS, SSM_W),
               "w_mix_out": (2 * ATTN_W // N_CHIPS, D_MODEL)}


class _Reduction:
    def __init__(self, names, grads, place):
        self.names, self.local, self.place = list(names), list(grads), place

    def exchange(self):
        return _pair_exchange(self.local)

    def first(self, got):
        self.pair = [_pair_sum(a, b, self.place["core"], name=f"pair_sum_{n}") for n, a, b in zip(self.names, self.local, got)]
        return _reduce_first(self.pair)

    def second(self, got):
        self.sums = [_reduce_sum_first(a, b, self.place["sel_first"], name=f"sum_first_{n}")
                     for n, a, b in zip(self.names, self.pair, got)]
        return _reduce_second(self.sums)

    def swap(self, got):
        self.halves = [_reduce_sum_second(a, b, self.place["sel_second"], name=f"sum_second_{n}").reshape(2 * a.shape[2], a.shape[3])
                       for n, a, b in zip(self.names, self.sums, got)]
        return _pair_swap(self.halves)

    def done(self, got):
        return {n: (mine, theirs) for n, mine, theirs in zip(self.names, self.halves, got)}


def _local_step(x, target, p, w, place=None):
    vec = lambda t: t.reshape(1, -1)
    w1_in, w1_out = w["ffn1_w_in"], w["ffn1_w_out"].reshape(2, FF_BLK, D_MODEL)
    blocks = lambda n, t: t.reshape((N_CHIPS,) + SHARD_SHAPE[n])

    ffn1 = functools.partial(_ffn_fwd, x, vec(p["ffn1_pre_g"]), w1_in, w1_out, vec(p["ffn1_post_g"]), None, name="ffn1_fwd")
    if place is None:
        x1, z1, f1 = ffn1()
    else:
        (x1, z1, f1), rest = ffn1(rider=_gather_weights([w[n] for n in REST]))
        w = dict(w, **dict(zip(REST, rest)))
    w2_in, w2_out = w["ffn2_w_in"], w["ffn2_w_out"].reshape(2, FF_BLK, D_MODEL)
    w_mi, w_glu, w_mo = w["w_mix_in"], w["w_glu"].reshape(SSM_W, SSM_W), w["w_mix_out"].reshape(2, ATTN_W, D_MODEL)
    q, k, v, u = _mix_in_fwd(x1, vec(p["mix_pre_g"]), w_mi)
    attn, lse = _attention_fwd(q, k, v)
    tab_f, tab_b, bd, cd, sv = _ssm_prepare(p["a_re"], p["a_im"], p["log_dt"], p["b_re"], p["b_im"], p["c_re"], p["c_im"])
    ssm_args = (bd[0], bd[1], cd[0], cd[1], vec(p["d_skip"]), w_glu, vec(p["b_glu"]))
    s_re, s_im, yp, ssm = _ssm_fwd(u, tab_f, *ssm_args)
    x2, mixed = _mix_out_fwd(x1, attn, ssm, w_mo, vec(p["mix_post_g"]))
    dx3, loss_rows, z2, f2 = _ffn_fwd(x2, vec(p["ffn2_pre_g"]), w2_in, w2_out, vec(p["ffn2_post_g"]), target, name="ffn2_fwd")

    g = {}
    ride = (lambda call, exchange: call(rider=exchange)) if place else (lambda call, exchange: (call(), None))
    dz2, g["ffn2_post_g"], dw2_out = _ffn_bwd_out(dx3, f2, z2, w2_out, vec(p["ffn2_post_g"]), name="ffn2_bwd_out")
    dx2, g["ffn2_pre_g"], dw2_in = _norm_matmul_bwd(dz2, x2, dx3, vec(p["ffn2_pre_g"]), w2_in, name="ffn2_bwd_in")
    early = _Reduction(("ffn2_w_in", "ffn2_w_out"), [dw2_in, blocks("ffn2_w_out", dw2_out)], place) if place else None
    (dattn, dssm, dw_mo, g["mix_post_g"]), got = ride(
        functools.partial(_mix_out_bwd, dx2, mixed, attn, ssm, w_mo, vec(p["mix_post_g"])), early and early.exchange())
    (du, g["d_skip"], g["b_glu"], dw_glu, da, dbd_re, dbd_im, dcd_re, dcd_im), got = ride(
        functools.partial(_ssm_bwd, dssm, u, yp, s_re, s_im, tab_b, *ssm_args), early and early.first(got))
    branch = lambda d: functools.partial(_attn_branch_bwd, q, k, v, attn, lse, dattn, d)
    parts = [None] * 3
    parts[0], got = ride(branch(DILATIONS[0]), early and early.second(got))
    parts[1], early_theirs = ride(branch(DILATIONS[1]), early and early.swap(got))
    parts[2] = branch(DILATIONS[2])()
    dproj = _dproj_merge([r[0] for r in parts], [r[1] for r in parts], [r[2] for r in parts], du)
    dx1, g["mix_pre_g"], dw_mi = _norm_matmul_bwd(dproj, x1, dx2, vec(p["mix_pre_g"]), w_mi, name="mix_bwd_in")
    dz1, g["ffn1_post_g"], dw1_out = _ffn_bwd_out(dx1, f1, z1, w1_out, vec(p["ffn1_post_g"]), name="ffn1_bwd_out")
    big = {"ffn2_w_in": dw2_in, "ffn2_w_out": dw2_out, "w_mix_in": dw_mi, "w_glu": dw_glu, "w_mix_out": dw_mo, "ffn1_w_out": dw1_out}
    late = _Reduction(LATE, [blocks(n, big[n]) for n in LATE], place) if place else None
    (grad_x, g["ffn1_pre_g"], big["ffn1_w_in"]), late_got = ride(
        functools.partial(_norm_matmul_bwd, dz1, x, dx1, vec(p["ffn1_pre_g"]), w1_in, name="ffn1_bwd_in"), late and late.exchange())
    if place:
        last = _Reduction(("ffn1_w_in",), [big["ffn1_w_in"]], place)
        tail = _Reduction(LATE + ("ffn1_w_in",), late.local + last.local, place)
        got = tail.first(list(late_got) + list(last.exchange().run("pair_exchange"))).run("reduce_first")
        got = tail.second(got).run("reduce_second")
        got = tail.swap(got).run("pair_swap")
        g.update(early.done(early_theirs))
        g.update(tail.done(got))
    else:
        g.update({n: blocks(n, big[n]) for n in BIG})

    g["c_re"], g["c_im"] = (jnp.swapaxes(_diag_blocks(m, STATE, GROUP_CH), 1, 2) for m in (dcd_re, dcd_im))
    dbb = [jnp.swapaxes(_diag_blocks(m, GROUP_CH, STATE), 1, 2).reshape(N_STATE, GROUP_CH) for m in (dbd_re, dbd_im)]
    dab = [jnp.sum(da[n], axis=0).reshape(N_STATE, 1) for n in range(2)]
    da_re, da_im, dldt, db_re, db_im = _ssm_discretize_bwd(
        sv["a_re"], sv["a_im"], sv["log_dt"], sv["b_re"], sv["b_im"], sv["ab_re"], sv["ab_im"], sv["e_re"], sv["e_im"],
        dab[0], dab[1], dbb[0], dbb[1])
    g["a_re"], g["a_im"] = da_re.reshape(N_GROUPS, STATE), da_im.reshape(N_GROUPS, STATE)
    g["log_dt"] = jnp.sum(dldt.reshape(N_GROUPS, STATE), axis=-1)
    g["b_re"], g["b_im"] = (t.reshape(N_GROUPS, STATE, GROUP_CH) for t in (db_re, db_im))
    return loss_rows, grad_x, g


MESH = pl.DeviceIdType.MESH
N_REL = 3


def _place():
    x, y, c = lax.axis_index("x"), lax.axis_index("y"), lax.axis_index("c")
    return x, y, c, [(1 - x, y), (x, 1 - y), (1 - x, 1 - y)]


def _remote(src, dst, send_sems, recv_sems, idx, to):
    return pltpu.make_async_remote_copy(src_ref=src, dst_ref=dst, send_sem=send_sems.at[idx], recv_sem=recv_sems.at[idx],
                                        device_id=to, device_id_type=MESH)


def _half(rows, who):
    return pl.ds(who * (rows // 2), rows // 2)


class _Copies:
    def __init__(self, send_sems, recv_sems, local_sems):
        self.send_sems, self.recv_sems, self.local_sems = send_sems, recv_sems, local_sems
        self.n_remote = self.n_local = 0

    def remote(self, src, dst, to):
        k, self.n_remote = self.n_remote, self.n_remote + 1
        return pltpu.make_async_remote_copy(src_ref=src, dst_ref=dst, send_sem=self.send_sems.at[k],
                                            recv_sem=self.recv_sems.at[k], device_id=to, device_id_type=MESH)

    def local(self, src, dst):
        k, self.n_local = self.n_local, self.n_local + 1
        return pltpu.make_async_copy(src, dst, self.local_sems.at[k])


class _Exchange:
    def __init__(self, plan, ins, out_shapes, n_remote):
        self.plan, self.ins, self.out_shapes = plan, list(ins), list(out_shapes)
        self.sems = [pltpu.SemaphoreType.DMA((n_remote,)), pltpu.SemaphoreType.DMA((n_remote,)), pltpu.SemaphoreType.DMA((1,))]

    def run(self, name):
        n_in = len(self.ins)

        def body(*refs):
            for phase in self.plan(refs[:n_in], refs[n_in:-3], _Copies(*refs[-3:])):
                for cp in phase:
                    cp.start()
                for cp in phase:
                    cp.wait()

        return pl.pallas_call(body, name=name, in_specs=[ANY] * n_in, out_specs=[ANY] * len(self.out_shapes),
                              out_shape=self.out_shapes, scratch_shapes=self.sems)(*self.ins)


def _pallas(body, *, name, grid, in_specs, out_specs, out_shape, args, semantics, scratch_shapes=(), rider=None):
    if rider is None:
        return pl.pallas_call(body, name=name, grid=grid, in_specs=in_specs, out_specs=out_specs, out_shape=out_shape,
                              scratch_shapes=list(scratch_shapes), compiler_params=_params(*semantics))(*args)
    n_in, n_out, r_in, r_out = len(in_specs), len(out_specs), len(rider.ins), len(rider.out_shapes)
    n_steps = math.prod(grid)

    def carrier(*refs):
        ins, rider_ins = refs[:n_in], refs[n_in:n_in + r_in]
        outs = refs[n_in + r_in:n_in + r_in + n_out]
        rider_outs = refs[n_in + r_in + n_out:n_in + r_in + n_out + r_out]
        scratch, sems = refs[n_in + r_in + n_out + r_out:-3], refs[-3:]
        step = 0
        for axis, size in enumerate(grid):
            step = step * size + pl.program_id(axis)
        phases = rider.plan(rider_ins, rider_outs, _Copies(*sems))

        def start_phase(p):
            for cp in (phases[p - 1] if p else []):
                cp.wait()
            for cp in phases[p]:
                cp.start()

        for p in range(len(phases)):
            pl.when(step == p * n_steps // len(phases))(functools.partial(start_phase, p))
        body(*ins, *outs, *scratch)

        @pl.when(step == n_steps - 1)
        def _():
            for cp in phases[-1]:
                cp.wait()

    results = pl.pallas_call(
        carrier, name=name, grid=grid, in_specs=list(in_specs) + [ANY] * r_in, out_specs=list(out_specs) + [ANY] * r_out,
        out_shape=list(out_shape) + rider.out_shapes, scratch_shapes=list(scratch_shapes) + rider.sems,
        compiler_params=_params(*["arbitrary"] * len(grid)))(*args, *rider.ins)
    return results[:n_out], results[n_out:]


def _gather_weights(shards):
    def plan(ins, outs, mk):
        x, y, c = lax.axis_index("x"), lax.axis_index("y"), lax.axis_index("c")
        me, sibling = 2 * x + y, (x, y, 1 - c)
        x_nb, y_nb, diag = (1 - x, y), (x, 1 - y), (1 - x, 1 - y)
        index = lambda chip: 2 * chip[0] + chip[1]
        first, second, third = [], [], []
        for i, shard in enumerate(shards):
            rows = shard.shape[0]
            mine = _half(rows, c)
            quarter = lambda which: pl.ds(c * (rows // 2) + which * (rows // 4), rows // 4)
            first.append(mk.remote(ins[i], outs[i].at[me], sibling))
            for nb in (x_nb, y_nb):
                first.append(mk.remote(ins[i].at[mine], outs[i].at[me, mine], (*nb, c)))
            for nb in (x_nb, y_nb):
                landed = outs[i].at[index(nb), mine]
                second.append(mk.remote(landed, landed, sibling))
            for nb, other, which in ((x_nb, y_nb, 0), (y_nb, x_nb, 1)):
                landed = outs[i].at[index(nb), quarter(which)]
                second.append(mk.remote(landed, landed, (*other, c)))
            landed = outs[i].at[index(diag), mine]
            third.append(mk.remote(landed, landed, sibling))
        return [first, second, third]

    return _Exchange(plan, shards, [jax.ShapeDtypeStruct((N_CHIPS,) + s.shape, s.dtype) for s in shards], 8 * len(shards))


def _pair_exchange(grads):
    def plan(ins, outs, mk):
        x, y, c = lax.axis_index("x"), lax.axis_index("y"), lax.axis_index("c")
        return [[mk.remote(ins[i].at[:, _half(g.shape[1], 1 - c)], outs[i], (x, y, 1 - c)) for i, g in enumerate(grads)]]

    return _Exchange(plan, grads, [jax.ShapeDtypeStruct((N_CHIPS, g.shape[1] // 2, g.shape[2]), g.dtype) for g in grads], len(grads))


def _reduce_first(pair):
    def plan(ins, outs, mk):
        x, y, c = lax.axis_index("x"), lax.axis_index("y"), lax.axis_index("c")
        phase = []
        for i, p in enumerate(pair):
            q = p.shape[1] // 2
            phase.append(mk.remote(ins[i].at[pl.ds(2 * (1 - x), 2), pl.ds(0, q)], outs[i].at[0], (1 - x, y, c)))
            for jx in range(2):
                phase.append(mk.remote(ins[i].at[2 * jx + 1 - y, pl.ds(q, q)], outs[i].at[1, jx], (x, 1 - y, c)))
        return [phase]

    return _Exchange(plan, pair, [jax.ShapeDtypeStruct((2, 2, p.shape[1] // 2, p.shape[2]), p.dtype) for p in pair], 3 * len(pair))


def _reduce_second(sums):
    def plan(ins, outs, mk):
        x, y, c = lax.axis_index("x"), lax.axis_index("y"), lax.axis_index("c")
        phase = []
        for i in range(len(sums)):
            phase.append(mk.remote(ins[i].at[0, 1 - y], outs[i].at[0], (x, 1 - y, c)))
            phase.append(mk.remote(ins[i].at[1, 1 - x], outs[i].at[1], (1 - x, y, c)))
        return [phase]

    return _Exchange(plan, sums, [jax.ShapeDtypeStruct((2,) + s.shape[2:], s.dtype) for s in sums], 2 * len(sums))


def _pair_swap(halves):
    def plan(ins, outs, mk):
        x, y, c = lax.axis_index("x"), lax.axis_index("y"), lax.axis_index("c")
        return [[mk.remote(ins[i], outs[i], (x, y, 1 - c)) for i in range(len(halves))]]

    return _Exchange(plan, halves, [jax.ShapeDtypeStruct(h.shape, h.dtype) for h in halves], len(halves))


def _allreduce_small(packed):
    rows = packed.shape[0]
    n_dev = 2 * N_CHIPS

    def body(x_ref, o_ref, buf, send_sems, recv_sems):
        x, y, c, chips = _place()
        sibling = (x, y, 1 - c)

        def slot(px, py, pc):
            return buf.at[4 * px + 2 * py + pc]

        buf[4 * x + 2 * y + c] = x_ref[...]
        first = [_remote(x_ref, slot(x, y, c), send_sems, recv_sems, 0, sibling)]
        first += [_remote(x_ref, slot(x, y, c), send_sems, recv_sems, 1 + k, (*chip, c)) for k, chip in enumerate(chips)]
        for cp in first:
            cp.start()
        passed = []
        for k, chip in enumerate(chips):
            landed = slot(*chip, c)
            _remote(landed, landed, send_sems, recv_sems, 1 + k, (*chip, c)).wait_recv()
            passed.append(_remote(landed, landed, send_sems, recv_sems, 1 + N_REL + k, sibling))
            passed[-1].start()
        _remote(slot(*sibling), slot(*sibling), send_sems, recv_sems, 0, sibling).wait_recv()
        for k, chip in enumerate(chips):
            landed = slot(*chip, 1 - c)
            _remote(landed, landed, send_sems, recv_sems, 1 + N_REL + k, sibling).wait_recv()
        for cp in first + passed:
            cp.wait_send()
        total = buf[0]
        for d in range(1, n_dev):
            total = total + buf[d]
        o_ref[...] = total

    vm = pl.BlockSpec(memory_space=pltpu.VMEM)
    return pl.pallas_call(
        body, name="allreduce_small", in_specs=[vm], out_specs=vm, out_shape=jax.ShapeDtypeStruct(packed.shape, F32),
        scratch_shapes=[pltpu.VMEM((n_dev, rows, 128), F32), pltpu.SemaphoreType.DMA((1 + 2 * N_REL,)),
                        pltpu.SemaphoreType.DMA((1 + 2 * N_REL,))],
    )(packed)


def _row_tile(rows, cap=256):
    return max(t for t in range(8, cap + 1, 8) if rows % t == 0)


def _pair_sum(grad, got, c, name):
    _, half, cols = got.shape
    tr = _row_tile(half)
    nt = half // tr

    def body(c_ref, g_ref, r_ref, o_ref):
        o_ref[...] = g_ref[...] + r_ref[...]

    blk = (1, tr, cols)
    return pl.pallas_call(
        body, name=name,
        grid_spec=pltpu.PrefetchScalarGridSpec(
            num_scalar_prefetch=1, grid=(N_CHIPS, nt),
            in_specs=[pl.BlockSpec(blk, lambda j, t, c_ref: (j, c_ref[0] * nt + t, 0)),
                      pl.BlockSpec(blk, lambda j, t, c_ref: (j, t, 0))],
            out_specs=pl.BlockSpec(blk, lambda j, t, c_ref: (j, t, 0))),
        out_shape=jax.ShapeDtypeStruct(got.shape, F32), compiler_params=_params("parallel", "parallel"),
    )(c, grad, got)


def _reduce_sum_first(pair, got, sel, name):
    _, _, q, cols = got.shape
    tr = _row_tile(q)
    nt = q // tr

    def body(sel_ref, p_ref, r_ref, o_ref):
        o_ref[0, 0] = p_ref[0] + r_ref[0, 0]

    blk = (1, 1, tr, cols)
    return pl.pallas_call(
        body, name=name,
        grid_spec=pltpu.PrefetchScalarGridSpec(
            num_scalar_prefetch=1, grid=(2, 2, nt),
            in_specs=[pl.BlockSpec((1, tr, cols), lambda p, k, t, s: (s[2 * p] + s[2 * p + 1] * k, p * nt + t, 0)),
                      pl.BlockSpec(blk, lambda p, k, t, s: (p, k, t, 0))],
            out_specs=pl.BlockSpec(blk, lambda p, k, t, s: (p, k, t, 0))),
        out_shape=jax.ShapeDtypeStruct(got.shape, F32), compiler_params=_params("parallel", "parallel", "parallel"),
    )(sel, pair, got)


def _reduce_sum_second(sums, got, sel, name):
    _, q, cols = got.shape
    tr = _row_tile(q)

    def body(sel_ref, s_ref, r_ref, o_ref):
        o_ref[0] = s_ref[0, 0] + r_ref[0]

    blk = (1, tr, cols)
    return pl.pallas_call(
        body, name=name,
        grid_spec=pltpu.PrefetchScalarGridSpec(
            num_scalar_prefetch=1, grid=(2, q // tr),
            in_specs=[pl.BlockSpec((1, 1, tr, cols), lambda p, t, s: (p, s[p], t, 0)),
                      pl.BlockSpec(blk, lambda p, t, s: (p, t, 0))],
            out_specs=pl.BlockSpec(blk, lambda p, t, s: (p, t, 0))),
        out_shape=jax.ShapeDtypeStruct(got.shape, F32), compiler_params=_params("parallel", "parallel"),
    )(sel, sums, got)


def _adamw_update(w, g, m, v):
    m2 = ADAM_B1 * m + (1.0 - ADAM_B1) * g
    v2 = ADAM_B2 * v + (1.0 - ADAM_B2) * (g * g)
    m_hat = m2 / (1.0 - ADAM_B1 ** ADAM_STEP)
    v_hat = v2 / (1.0 - ADAM_B2 ** ADAM_STEP)
    return -ADAM_LR * (m_hat / (jnp.sqrt(v_hat) + ADAM_EPS) + ADAM_WD * w), m2, v2


def _adamw(w, g, m, v, name):
    rows, cols = w.shape
    tr = _row_tile(rows)

    def body(w_ref, g_ref, m_ref, v_ref, d_ref, mo_ref, vo_ref):
        d_ref[...], mo_ref[...], vo_ref[...] = _adamw_update(w_ref[...], g_ref[...], m_ref[...], v_ref[...])

    blk = _rows(tr, cols)
    return pl.pallas_call(
        body, name=name, grid=(rows // tr,), in_specs=[blk] * 4, out_specs=[blk] * 3,
        out_shape=[jax.ShapeDtypeStruct(w.shape, F32)] * 3, compiler_params=_params("parallel"),
    )(w, g, m, v)


def _adamw_halves(w, mine, theirs, m, v, core, name):
    rows, cols = w.shape
    tr = _row_tile(rows // 2)
    per_half = rows // 2 // tr

    def body(core_ref, w_ref, a_ref, b_ref, m_ref, v_ref, g_ref, d_ref, mo_ref, vo_ref):
        g = jnp.where(pl.program_id(0) // per_half == core_ref[0], a_ref[...], b_ref[...])
        g_ref[...] = g
        d_ref[...], mo_ref[...], vo_ref[...] = _adamw_update(w_ref[...], g, m_ref[...], v_ref[...])

    blk = pl.BlockSpec((tr, cols), lambda t, c: (t, 0))
    half = lambda own: pl.BlockSpec(
        (tr, cols), lambda t, c: (jnp.clip(t - (c[0] if own else 1 - c[0]) * per_half, 0, per_half - 1), 0))
    return pl.pallas_call(
        body, name=name,
        grid_spec=pltpu.PrefetchScalarGridSpec(
            num_scalar_prefetch=1, grid=(2 * per_half,), in_specs=[blk, half(True), half(False), blk, blk], out_specs=[blk] * 4),
        out_shape=[jax.ShapeDtypeStruct(w.shape, F32)] * 4, compiler_params=_params("arbitrary"),
    )(core, w, mine, theirs, m, v)


def _pack(parts):
    flat = []
    for t in parts:
        t = t.reshape(-1).astype(F32)
        flat.append(jnp.pad(t, (0, -t.shape[0] % 128)))
    flat = jnp.concatenate(flat)
    return jnp.pad(flat, (0, -flat.shape[0] % 1024)).reshape(-1, 128)


def _unpack(buf, shapes):
    flat, out, at = buf.reshape(-1), [], 0
    for s in shapes:
        size = math.prod(s)
        out.append(flat[at:at + size].reshape(s))
        at += size + (-size % 128)
    return out


def kernel(x, ffn1_pre_g, ffn1_w_in, ffn1_w_out, ffn1_post_g, mix_pre_g, w_mix_in, a_re, a_im, log_dt, b_re, b_im, c_re, c_im, d_skip, w_glu, b_glu, w_mix_out, mix_post_g, ffn2_pre_g, ffn2_w_in, ffn2_w_out, ffn2_post_g, loss_target, m_ffn1_pre_g, m_ffn1_w_in, m_ffn1_w_out, m_ffn1_post_g, m_mix_pre_g, m_w_mix_in, m_a_re, m_a_im, m_log_dt, m_b_re, m_b_im, m_c_re, m_c_im, m_d_skip, m_w_glu, m_b_glu, m_w_mix_out, m_mix_post_g, m_ffn2_pre_g, m_ffn2_w_in, m_ffn2_w_out, m_ffn2_post_g, v_ffn1_pre_g, v_ffn1_w_in, v_ffn1_w_out, v_ffn1_post_g, v_mix_pre_g, v_w_mix_in, v_a_re, v_a_im, v_log_dt, v_b_re, v_b_im, v_c_re, v_c_im, v_d_skip, v_w_glu, v_b_glu, v_w_mix_out, v_mix_post_g, v_ffn2_pre_g, v_ffn2_w_in, v_ffn2_w_out, v_ffn2_post_g):
    given = dict(locals())
    order = ("ffn1_pre_g", "ffn1_w_in", "ffn1_w_out", "ffn1_post_g", "mix_pre_g", "w_mix_in", "a_re", "a_im", "log_dt",
             "b_re", "b_im", "c_re", "c_im", "d_skip", "w_glu", "b_glu", "w_mix_out", "mix_post_g", "ffn2_pre_g",
             "ffn2_w_in", "ffn2_w_out", "ffn2_post_g")
    at_x, at_y, at_c = (lax.axis_index(a).astype(jnp.int32) for a in ("x", "y", "c"))
    place = dict(core=at_c.reshape(1), sel_first=jnp.stack([2 * at_x, jnp.int32(1), at_y, jnp.int32(2)]),
                 sel_second=jnp.stack([at_y, at_x]))

    shards = {n: given[n][0] for n in BIG}
    w = {n: shards[n].astype(BF16) for n in REST}
    w.update(zip(FIRST, _gather_weights([shards[n].astype(BF16) for n in FIRST]).run("gather_first")))
    small = {n: given[n][0] for n in SMALL}
    loss_rows, grad_x, g = _local_step(x[0], loss_target[0], small, w, place)

    total = _allreduce_small(_pack([g[n] for n in SMALL] + [loss_rows[0, :1]]))
    parts = _unpack(total, [small[n].shape for n in SMALL] + [(1,)])
    grads = dict(zip(SMALL, parts[:-1]))
    loss = parts[-1][0]

    delta, new_m, new_v = {}, {}, {}
    for n in BIG:
        grads[n], delta[n], new_m[n], new_v[n] = _adamw_halves(
            shards[n], *g[n], given["m_" + n][0], given["v_" + n][0], place["core"], name=f"adamw_{n}")
    packed = [_pack([given[pre + n] for n in SMALL]) for pre in ("", "m_", "v_")]
    outs = _adamw(packed[0], _pack([grads[n] for n in SMALL]), packed[1], packed[2], name="adamw_small")
    for store, buf in zip((delta, new_m, new_v), outs):
        store.update(zip(SMALL, _unpack(buf, [small[n].shape for n in SMALL])))

    lead = lambda d: [d[n][None] for n in order]
    return (loss, grad_x[None], *lead(grads), *lead(delta), *lead(new_m), *lead(new_v))
```

```python
import functools
import math

import jax
import jax.numpy as jnp
from jax import lax
from jax.experimental import pallas as pl
from jax.experimental.pallas import tpu as pltpu

F32, BF16 = jnp.float32, jnp.bfloat16

D_MODEL = 1024
D_FF = 2816
N_CHIPS = 4
FF_BLK = 2 * D_FF // N_CHIPS
ATTN_W = 512
SSM_W = 512
HEAD_DIM = 64
N_HEADS = ATTN_W // HEAD_DIM
DILATIONS = (1, 4, 16)
N_BACK = 128
QBLK = 128
N_GROUPS = 32
GROUP_CH = 16
STATE = 64
N_STATE = N_GROUPS * STATE
EPS = 1e-6
NEG = -1e30
GELU_C = math.sqrt(2.0 / math.pi)

ADAM_LR, ADAM_B1, ADAM_B2, ADAM_EPS, ADAM_WD, ADAM_STEP = 0.001, 0.9, 0.999, 1e-08, 0.01, 10

VMEM_LIMIT_V7X = 60 * 1024 * 1024
ROW_TILE = 256
FFN_ROW_TILE = 512
DW_ROW_TILE = 1024


def _params(*sem):
    return pltpu.CompilerParams(dimension_semantics=sem, vmem_limit_bytes=VMEM_LIMIT_V7X)


def _dot(a, b):
    return jnp.dot(a.astype(BF16), b.astype(BF16), preferred_element_type=F32)


def _dot_nt(a, b):
    return lax.dot_general(a.astype(BF16), b.astype(BF16), (((1,), (1,)), ((), ())), preferred_element_type=F32)


def _dot_tn(a, b):
    return lax.dot_general(a.astype(BF16), b.astype(BF16), (((0,), (0,)), ((), ())), preferred_element_type=F32)


def _full(shape):
    return pl.BlockSpec(shape, lambda *_: (0,) * len(shape))


def _rows(tm, width):
    return pl.BlockSpec((tm, width), lambda i: (i, 0))


ANY = pl.BlockSpec(memory_space=pl.ANY)


def _load_once(pairs, sems):
    copies = [pltpu.make_async_copy(src, dst, sems.at[k]) for k, (src, dst) in enumerate(pairs)]
    for c in copies:
        c.start()
    for c in copies:
        c.wait()


def _rms(x):
    return lax.rsqrt(jnp.mean(x * x, axis=-1, keepdims=True) + EPS)


def _rms_bwd(dy_g, xn, r):
    return r * (dy_g - xn * jnp.mean(dy_g * xn, axis=-1, keepdims=True))


def _ffn_fwd(x, g_pre, w_in, w_out, g_post, target, *, name, rider=None):
    T = x.shape[0]
    tm = FFN_ROW_TILE
    with_loss = target is not None

    def body(*refs):
        if with_loss:
            x_ref, gpre_ref, gpost_ref, tgt_ref, win_hbm, wout_hbm, o_ref, loss_ref, z_ref, f_ref, win_v, wout_v, sems = refs
        else:
            x_ref, gpre_ref, gpost_ref, win_hbm, wout_hbm, o_ref, z_ref, f_ref, win_v, wout_v, sems = refs

        @pl.when(pl.program_id(0) == 0)
        def _():
            _load_once([(win_hbm, win_v), (wout_hbm, wout_v)], sems)
            if with_loss:
                loss_ref[...] = jnp.zeros_like(loss_ref)

        xv = x_ref[...]
        h = (xv * _rms(xv) * gpre_ref[...]).astype(BF16)
        f = jnp.zeros((tm, D_MODEL), F32)
        for k in range(2):
            gate = _dot(h, win_v[k])
            up = _dot(h, win_v[k + 2])
            z_ref[:, k * FF_BLK:(k + 1) * FF_BLK] = gate.astype(BF16)
            z_ref[:, D_FF + k * FF_BLK:D_FF + (k + 1) * FF_BLK] = up.astype(BF16)
            f = f + _dot(gate * jax.nn.sigmoid(gate) * up, wout_v[k])
        f_ref[...] = f
        out = xv + 0.5 * (f * _rms(f) * gpost_ref[...])
        if with_loss:
            err = out - tgt_ref[...]
            o_ref[...] = err * (1.0 / D_MODEL)
            loss_ref[...] += jnp.sum(err * err) * (0.5 / D_MODEL)
        else:
            o_ref[...] = out

    row = _rows(tm, D_MODEL)
    vec = _full((1, D_MODEL))
    in_specs = [row, vec, vec] + ([row] if with_loss else []) + [ANY, ANY]
    out_shape = [jax.ShapeDtypeStruct((T, D_MODEL), F32)]
    out_specs = [row]
    if with_loss:
        out_shape.append(jax.ShapeDtypeStruct((8, 128), F32))
        out_specs.append(_full((8, 128)))
    out_shape += [jax.ShapeDtypeStruct((T, 2 * D_FF), BF16), jax.ShapeDtypeStruct((T, D_MODEL), F32)]
    out_specs += [_rows(tm, 2 * D_FF), row]
    args = (x, g_pre, g_post) + ((target,) if with_loss else ()) + (w_in, w_out)
    return _pallas(
        body, name=name, grid=(T // tm,), in_specs=in_specs, out_specs=out_specs, out_shape=out_shape,
        scratch_shapes=[pltpu.VMEM(w_in.shape, BF16), pltpu.VMEM(w_out.shape, BF16), pltpu.SemaphoreType.DMA((2,))],
        semantics=("arbitrary",), args=args, rider=rider)


def _ffn_bwd_out(dout, f, z, w_out, g_post, *, name):
    T = dout.shape[0]
    tm = FFN_ROW_TILE
    nt = T // tm

    def body(dout_ref, f_ref, z_ref, gpost_ref, wout_hbm, dz_ref, dgpost_ref, dwout_hbm, wout_v, dwout_v, sems):
        i = pl.program_id(0)

        @pl.when(i == 0)
        def _():
            _load_once([(wout_hbm, wout_v)], sems)
            dwout_v[...] = jnp.zeros_like(dwout_v)
            dgpost_ref[...] = jnp.zeros_like(dgpost_ref)

        dy = 0.5 * dout_ref[...]
        f = f_ref[...]
        r = _rms(f)
        fn = f * r
        dgpost_ref[...] += jnp.sum(dy * fn, axis=0, keepdims=True)
        df = _rms_bwd(dy * gpost_ref[...], fn, r).astype(BF16)
        for k in range(2):
            gate = z_ref[:, k * FF_BLK:(k + 1) * FF_BLK].astype(F32)
            up = z_ref[:, D_FF + k * FF_BLK:D_FF + (k + 1) * FF_BLK].astype(F32)
            sg = jax.nn.sigmoid(gate)
            silu = gate * sg
            dwout_v[k] += _dot_tn(silu * up, df)
            da = _dot_nt(df, wout_v[k])
            dz_ref[:, k * FF_BLK:(k + 1) * FF_BLK] = (da * up * (sg * (1.0 + gate * (1.0 - sg)))).astype(BF16)
            dz_ref[:, D_FF + k * FF_BLK:D_FF + (k + 1) * FF_BLK] = (da * silu).astype(BF16)

        @pl.when(i == nt - 1)
        def _():
            c = pltpu.make_async_copy(dwout_v, dwout_hbm, sems.at[0])
            c.start()
            c.wait()

    row = _rows(tm, D_MODEL)
    return pl.pallas_call(
        body, name=name, grid=(nt,),
        in_specs=[row, row, _rows(tm, 2 * D_FF), _full((1, D_MODEL)), ANY],
        out_specs=[_rows(tm, 2 * D_FF), _full((1, D_MODEL)), ANY],
        out_shape=[jax.ShapeDtypeStruct((T, 2 * D_FF), BF16), jax.ShapeDtypeStruct((1, D_MODEL), F32),
                   jax.ShapeDtypeStruct(w_out.shape, F32)],
        scratch_shapes=[pltpu.VMEM(w_out.shape, BF16), pltpu.VMEM(w_out.shape, F32), pltpu.SemaphoreType.DMA((1,))],
        compiler_params=_params("arbitrary"),
    )(dout, f, z, g_post, w_out)


def _norm_matmul_dw(dz, x, g, n_blocks, *, name, rider=None):
    T = x.shape[0]
    bw = dz.shape[1] // n_blocks
    tm = DW_ROW_TILE

    def body(dz_ref, x_ref, g_ref, dw_ref):
        @pl.when(pl.program_id(1) == 0)
        def _():
            dw_ref[...] = jnp.zeros_like(dw_ref)

        xv = x_ref[...]
        dw_ref[0] += _dot_tn(xv * _rms(xv) * g_ref[...], dz_ref[...])

    res = _pallas(
        body, name=name, grid=(n_blocks, T // tm),
        in_specs=[pl.BlockSpec((tm, bw), lambda j, t: (t, j)), pl.BlockSpec((tm, D_MODEL), lambda j, t: (t, 0)), _full((1, D_MODEL))],
        out_specs=[pl.BlockSpec((1, D_MODEL, bw), lambda j, t: (j, 0, 0))],
        out_shape=[jax.ShapeDtypeStruct((n_blocks, D_MODEL, bw), F32)], semantics=("parallel", "arbitrary"),
        args=(dz, x, g), rider=rider)
    return res[0] if rider is None else (res[0][0], res[1])


def _norm_matmul_dx(dz, x, dres, g, w, *, name, rider=None):
    T = x.shape[0]
    nb, _, bw = w.shape
    tm = FFN_ROW_TILE

    def body(dz_ref, x_ref, dres_ref, g_ref, w_hbm, dx_ref, dg_ref, w_v, sems):
        @pl.when(pl.program_id(0) == 0)
        def _():
            _load_once([(w_hbm, w_v)], sems)
            dg_ref[...] = jnp.zeros_like(dg_ref)

        xv = x_ref[...]
        r = _rms(xv)
        xn = xv * r
        gv = g_ref[...]
        dh = jnp.zeros((tm, D_MODEL), F32)
        for j in range(nb):
            dh = dh + _dot_nt(dz_ref[:, j * bw:(j + 1) * bw], w_v[j])
        dg_ref[...] += jnp.sum(dh * xn, axis=0, keepdims=True)
        dx_ref[...] = _rms_bwd(dh * gv, xn, r) + dres_ref[...]

    row = _rows(tm, D_MODEL)
    return _pallas(
        body, name=name, grid=(T // tm,),
        in_specs=[_rows(tm, nb * bw), row, row, _full((1, D_MODEL)), ANY],
        out_specs=[row, _full((1, D_MODEL))],
        out_shape=[jax.ShapeDtypeStruct((T, D_MODEL), F32), jax.ShapeDtypeStruct((1, D_MODEL), F32)],
        scratch_shapes=[pltpu.VMEM(w.shape, BF16), pltpu.SemaphoreType.DMA((1,))],
        semantics=("arbitrary",), args=(dz, x, dres, g, w), rider=rider)


def _mix_in_fwd(x, g, w):
    T = x.shape[0]
    tm = ROW_TILE

    def body(x_ref, g_ref, w_ref, q_ref, k_ref, v_ref, u_ref):
        xv = x_ref[...]
        h = (xv * _rms(xv) * g_ref[...]).astype(BF16)
        for j, o_ref in enumerate((q_ref, k_ref, v_ref, u_ref)):
            o_ref[...] = _dot(h, w_ref[j])

    col = _rows(tm, ATTN_W)
    return pl.pallas_call(
        body, name="mix_in_fwd", grid=(T // tm,),
        in_specs=[_rows(tm, D_MODEL), _full((1, D_MODEL)), _full(w.shape)],
        out_specs=[col] * 4, out_shape=[jax.ShapeDtypeStruct((T, ATTN_W), F32)] * 4,
        compiler_params=_params("parallel"),
    )(x, g, w)


def _mix_out_fwd(x, attn, ssm, w, g):
    T = x.shape[0]
    tm = ROW_TILE

    def body(x_ref, a_ref, s_ref, w_ref, g_ref, o_ref, m_ref):
        mixed = _dot(a_ref[...], w_ref[0]) + _dot(s_ref[...], w_ref[1])
        m_ref[...] = mixed
        o_ref[...] = x_ref[...] + mixed * _rms(mixed) * g_ref[...]

    row, col = _rows(tm, D_MODEL), _rows(tm, ATTN_W)
    return pl.pallas_call(
        body, name="mix_out_fwd", grid=(T // tm,),
        in_specs=[row, col, col, _full(w.shape), _full((1, D_MODEL))],
        out_specs=[row, row], out_shape=[jax.ShapeDtypeStruct((T, D_MODEL), F32)] * 2,
        compiler_params=_params("parallel"),
    )(x, attn, ssm, w, g)


def _mix_out_bwd(dout, mixed, attn, ssm, w, g, rider=None):
    T = dout.shape[0]
    tm = ROW_TILE

    def body(dout_ref, m_ref, a_ref, s_ref, w_ref, g_ref, da_ref, ds_ref, dw_ref, dg_ref):
        @pl.when(pl.program_id(0) == 0)
        def _():
            dw_ref[...] = jnp.zeros_like(dw_ref)
            dg_ref[...] = jnp.zeros_like(dg_ref)

        dy = dout_ref[...]
        mixed = m_ref[...]
        r = _rms(mixed)
        mn = mixed * r
        dg_ref[...] += jnp.sum(dy * mn, axis=0, keepdims=True)
        dm = _rms_bwd(dy * g_ref[...], mn, r).astype(BF16)
        da_ref[...] = _dot_nt(dm, w_ref[0])
        ds_ref[...] = _dot_nt(dm, w_ref[1])
        dw_ref[0] += _dot_tn(a_ref[...], dm)
        dw_ref[1] += _dot_tn(s_ref[...], dm)

    row, col = _rows(tm, D_MODEL), _rows(tm, ATTN_W)
    return _pallas(
        body, name="mix_out_bwd", grid=(T // tm,),
        in_specs=[row, row, col, col, _full(w.shape), _full((1, D_MODEL))],
        out_specs=[col, col, _full(w.shape), _full((1, D_MODEL))],
        out_shape=[jax.ShapeDtypeStruct((T, ATTN_W), F32)] * 2
        + [jax.ShapeDtypeStruct(w.shape, F32), jax.ShapeDtypeStruct((1, D_MODEL), F32)],
        semantics=("arbitrary",), args=(dout, mixed, attn, ssm, w, g), rider=rider)


ATTN_TILING = {1: (ATTN_W, 1), 4: (2 * HEAD_DIM, 4), 16: (2 * HEAD_DIM, 4)}


def _class_rows(d, r):
    return (pl.ds(r, QBLK, stride=d), slice(None)) if d > 1 else (slice(None), slice(None))


def _for_class_groups(d, group, fn):
    if d == group:
        fn(0)
    else:
        lax.fori_loop(0, d // group, lambda n, carry: (fn(n * group), carry)[1], 0)


def _block_slopes(lanes):
    heads = lanes // HEAD_DIM
    first = pl.program_id(1) * heads
    return [jnp.exp2(-jnp.full((1, 1), first + hh + 1, jnp.int32).astype(F32)) for hh in range(heads)]


def _attn_specs(d, nb, lanes):
    blk = (QBLK * d, lanes)
    cur = pl.BlockSpec(blk, lambda j, lb: (j, lb))
    prev = pl.BlockSpec(blk, lambda j, lb: (jnp.maximum(j - 1, 0), lb))
    nxt = pl.BlockSpec(blk, lambda j, lb: (jnp.minimum(j + 1, nb - 1), lb))
    return cur, prev, nxt


def _attn_branch_fwd(q, k, v, d):
    T = q.shape[0]
    nb = T // (d * QBLK)
    lanes, group = ATTN_TILING[d]
    scale = HEAD_DIM ** -0.5

    def body(q_ref, kc_ref, kp_ref, vc_ref, vp_ref, o_ref, l_ref, q_s, kk_s, vv_s, o_s, l_s):
        j = pl.program_id(0)
        qi = lax.broadcasted_iota(jnp.int32, (QBLK, 2 * QBLK), 0)
        ci = lax.broadcasted_iota(jnp.int32, (QBLK, 2 * QBLK), 1)
        steps = QBLK + qi - ci
        valid = (steps >= 0) & (steps <= N_BACK) & ((ci >= QBLK) | (j > 0))
        dist = (steps * d).astype(F32)
        slopes = _block_slopes(lanes)

        def classes(first):
            for n in range(group):
                rows = _class_rows(d, first + n)
                q_s[n] = q_ref[rows]
                kk_s[n, :QBLK], kk_s[n, QBLK:] = kp_ref[rows], kc_ref[rows]
                vv_s[n, :QBLK], vv_s[n, QBLK:] = vp_ref[rows], vc_ref[rows]
            for n in range(group):
                for hh in range(lanes // HEAD_DIM):
                    sl = slice(hh * HEAD_DIM, (hh + 1) * HEAD_DIM)
                    s = _dot_nt(q_s[n, :, sl], kk_s[n, :, sl]) * scale - slopes[hh] * dist
                    s = jnp.where(valid, s, NEG)
                    m = jnp.max(s, axis=-1, keepdims=True)
                    p = jnp.exp(s - m)
                    den = jnp.sum(p, axis=-1, keepdims=True)
                    o_s[n, :, sl] = _dot(p, vv_s[n, :, sl]) / den
                    l_s[n, :, sl] = jnp.broadcast_to(m + jnp.log(den), (QBLK, HEAD_DIM))
            for n in range(group):
                rows = _class_rows(d, first + n)
                o_ref[rows] = o_s[n]
                l_ref[rows] = l_s[n]

        _for_class_groups(d, group, classes)

    cur, prev, _ = _attn_specs(d, nb, lanes)
    shape = jax.ShapeDtypeStruct((T, ATTN_W), F32)
    one, two = pltpu.VMEM((group, QBLK, lanes), F32), pltpu.VMEM((group, 2 * QBLK, lanes), F32)
    return pl.pallas_call(
        body, name=f"attn_fwd_d{d}", grid=(nb, ATTN_W // lanes),
        in_specs=[cur, cur, prev, cur, prev], out_specs=[cur, cur], out_shape=[shape, shape],
        scratch_shapes=[one, two, two, one, one], compiler_params=_params("parallel", "parallel"),
    )(q, k, k, v, v)


def _attn_merge(outs, lses):
    T = outs[0].shape[0]
    tm = 512

    def body(o1, o2, o3, l1, l2, l3, a_ref, lse_ref):
        ls = [l1[...], l2[...], l3[...]]
        m = jnp.maximum(jnp.maximum(ls[0], ls[1]), ls[2])
        lse = m + jnp.log(jnp.exp(ls[0] - m) + jnp.exp(ls[1] - m) + jnp.exp(ls[2] - m))
        lse_ref[...] = lse
        a_ref[...] = jnp.exp(ls[0] - lse) * o1[...] + jnp.exp(ls[1] - lse) * o2[...] + jnp.exp(ls[2] - lse) * o3[...]

    col = _rows(tm, ATTN_W)
    return pl.pallas_call(
        body, name="attn_merge", grid=(T // tm,), in_specs=[col] * 6, out_specs=[col, col],
        out_shape=[jax.ShapeDtypeStruct((T, ATTN_W), F32)] * 2, compiler_params=_params("parallel"),
    )(*outs, *lses)


def _attn_branch_bwd(q, k, v, o, lse, do, d, rider=None):
    T = q.shape[0]
    nb = T // (d * QBLK)
    lanes, group = ATTN_TILING[d]
    scale = HEAD_DIM ** -0.5

    def body(qc_ref, qn_ref, kc_ref, kp_ref, vc_ref, vp_ref, oc_ref, on_ref, lc_ref, ln_ref, doc_ref, don_ref,
             dq_ref, dk_ref, dv_ref, qq_s, kk_s, vv_s, oo_s, ll_s, doo_s, dq_s, dk_s, dv_s):
        j = pl.program_id(0)
        qi = lax.broadcasted_iota(jnp.int32, (QBLK, 2 * QBLK), 0)
        ci = lax.broadcasted_iota(jnp.int32, (QBLK, 2 * QBLK), 1)
        steps_q = QBLK + qi - ci
        valid_q = (steps_q >= 0) & (steps_q <= N_BACK) & ((ci >= QBLK) | (j > 0))
        dist_q = (steps_q * d).astype(F32)
        ri = lax.broadcasted_iota(jnp.int32, (2 * QBLK, QBLK), 0)
        ki = lax.broadcasted_iota(jnp.int32, (2 * QBLK, QBLK), 1)
        steps_k = ri - ki
        valid_k = (steps_k >= 0) & (steps_k <= N_BACK) & ((ri < QBLK) | (j < nb - 1))
        dist_k = (steps_k * d).astype(F32)
        lo, hi = slice(0, QBLK), slice(QBLK, 2 * QBLK)
        slopes = _block_slopes(lanes)

        def classes(first):
            for n in range(group):
                rows = _class_rows(d, first + n)
                qq_s[n, lo], qq_s[n, hi] = qc_ref[rows], qn_ref[rows]
                oo_s[n, lo], oo_s[n, hi] = oc_ref[rows], on_ref[rows]
                ll_s[n, lo], ll_s[n, hi] = lc_ref[rows], ln_ref[rows]
                doo_s[n, lo], doo_s[n, hi] = doc_ref[rows], don_ref[rows]
                kk_s[n, lo], kk_s[n, hi] = kp_ref[rows], kc_ref[rows]
                vv_s[n, lo], vv_s[n, hi] = vp_ref[rows], vc_ref[rows]
            for n in range(group):
                for hh in range(lanes // HEAD_DIM):
                    sl = slice(hh * HEAD_DIM, (hh + 1) * HEAD_DIM)
                    slope = slopes[hh]
                    qq, doo, kk, vv = qq_s[n, :, sl], doo_s[n, :, sl], kk_s[n, :, sl], vv_s[n, :, sl]
                    lse2 = ll_s[n, :, hh * HEAD_DIM:hh * HEAD_DIM + 1]
                    delta2 = jnp.sum(doo * oo_s[n, :, sl], axis=-1, keepdims=True)

                    s = jnp.where(valid_q, _dot_nt(qq[lo], kk) * scale - slope * dist_q, NEG)
                    p = jnp.exp(s - lse2[lo])
                    ds = p * (_dot_nt(doo[lo], vv) - delta2[lo])
                    dq_s[n, :, sl] = _dot(ds, kk) * scale

                    s2 = jnp.where(valid_k, _dot_nt(qq, kk[hi]) * scale - slope * dist_k, NEG)
                    p2 = jnp.exp(s2 - lse2)
                    dv_s[n, :, sl] = _dot_tn(p2, doo)
                    ds2 = p2 * (_dot_nt(doo, vv[hi]) - delta2)
                    dk_s[n, :, sl] = _dot_tn(ds2, qq) * scale
            for n in range(group):
                rows = _class_rows(d, first + n)
                dq_ref[rows] = dq_s[n]
                dk_ref[rows] = dk_s[n]
                dv_ref[rows] = dv_s[n]

        _for_class_groups(d, group, classes)

    cur, prev, nxt = _attn_specs(d, nb, lanes)
    shape = jax.ShapeDtypeStruct((T, ATTN_W), F32)
    one, two = pltpu.VMEM((group, QBLK, lanes), F32), pltpu.VMEM((group, 2 * QBLK, lanes), F32)
    return _pallas(
        body, name=f"attn_bwd_d{d}", grid=(nb, ATTN_W // lanes),
        in_specs=[cur, nxt, cur, prev, cur, prev, cur, nxt, cur, nxt, cur, nxt],
        out_specs=[cur] * 3, out_shape=[shape] * 3, scratch_shapes=[two] * 6 + [one] * 3,
        semantics=("parallel", "parallel"), args=(q, q, k, k, v, v, o, o, lse, lse, do, do), rider=rider)


def _dproj_merge(dqs, dks, dvs, du):
    T = du.shape[0]
    tm = 512

    def body(*refs):
        o_ref = refs[-1]
        for part in range(3):
            a, b, c = refs[3 * part:3 * part + 3]
            o_ref[:, part * ATTN_W:(part + 1) * ATTN_W] = (a[...] + b[...] + c[...]).astype(BF16)
        o_ref[:, 3 * ATTN_W:] = refs[9][...].astype(BF16)

    col = _rows(tm, ATTN_W)
    return pl.pallas_call(
        body, name="dproj_merge", grid=(T // tm,), in_specs=[col] * 10, out_specs=_rows(tm, 4 * ATTN_W),
        out_shape=jax.ShapeDtypeStruct((T, 4 * ATTN_W), BF16), compiler_params=_params("parallel"),
    )(*dqs, *dks, *dvs, du)


def _attention_fwd(q, k, v):
    res = [_attn_branch_fwd(q, k, v, d) for d in DILATIONS]
    return _attn_merge([r[0] for r in res], [r[1] for r in res])


def _attention_bwd(q, k, v, attn, lse, dattn):
    res = [_attn_branch_bwd(q, k, v, attn, lse, dattn, d) for d in DILATIONS]
    return [r[0] for r in res], [r[1] for r in res], [r[2] for r in res]


SCAN_ROWS = 8
SCAN_LANES = 512
SSM_CHUNK = 256
SSM_HALVES = tuple((slice(h * SSM_W // 2, (h + 1) * SSM_W // 2), slice(h * N_STATE // 2, (h + 1) * N_STATE // 2)) for h in range(2))


def _cmul(ar, ai, br, bi):
    return ar * br - ai * bi, ar * bi + ai * br


def _ssm_discretize(a_re, a_im, log_dt, b_re, b_im):
    def body(ar_ref, ai_ref, ldt_ref, br_ref, bi_ref, abr_ref, abi_ref, er_ref, ei_ref, bbr_ref, bbi_ref, pr_ref, pi_ref):
        ar, ai = ar_ref[...], ai_ref[...]
        dt = jnp.exp(ldt_ref[...])
        n = lax.broadcasted_iota(jnp.int32, (1, SCAN_ROWS), 1).astype(F32) + 1.0
        mag, ang = jnp.exp(dt * ar), dt * ai
        abr, abi = mag * jnp.cos(ang), mag * jnp.sin(ang)
        abr_ref[...], abi_ref[...] = abr, abi
        pr_ref[...] = jnp.exp(dt * ar * n) * jnp.cos(ang * n)
        pi_ref[...] = jnp.exp(dt * ar * n) * jnp.sin(ang * n)
        den = ar * ar + ai * ai
        er = ((abr - 1.0) * ar + abi * ai) / den
        ei = (abi * ar - (abr - 1.0) * ai) / den
        er_ref[...], ei_ref[...] = er, ei
        bbr_ref[...], bbi_ref[...] = _cmul(er, ei, br_ref[...], bi_ref[...])

    col = jax.ShapeDtypeStruct((N_STATE, 1), F32)
    mat = jax.ShapeDtypeStruct((N_STATE, GROUP_CH), F32)
    pw = jax.ShapeDtypeStruct((N_STATE, SCAN_ROWS), F32)
    return pl.pallas_call(body, name="ssm_discretize", out_shape=[col] * 4 + [mat] * 2 + [pw] * 2)(
        a_re, a_im, log_dt, b_re, b_im)


def _ssm_discretize_bwd(a_re, a_im, log_dt, b_re, b_im, ab_re, ab_im, e_re, e_im, dab_re, dab_im, dbb_re, dbb_im):
    def body(ar_ref, ai_ref, ldt_ref, br_ref, bi_ref, abr_ref, abi_ref, er_ref, ei_ref, dabr_ref, dabi_ref,
             dbbr_ref, dbbi_ref, dar_ref, dai_ref, ddt_ref, dbr_ref, dbi_ref):
        ar, ai, dt = ar_ref[...], ai_ref[...], jnp.exp(ldt_ref[...])
        er, ei = er_ref[...], ei_ref[...]
        gbr, gbi = dbbr_ref[...], dbbi_ref[...]
        dbr_ref[...], dbi_ref[...] = _cmul(er, -ei, gbr, gbi)
        br, bi = br_ref[...], bi_ref[...]
        der = jnp.sum(br * gbr + bi * gbi, axis=-1, keepdims=True)
        dei = jnp.sum(br * gbi - bi * gbr, axis=-1, keepdims=True)
        den = ar * ar + ai * ai
        inv_r, inv_i = ar / den, -ai / den
        t_r, t_i = _cmul(der, dei, inv_r, -inv_i)
        gab_r, gab_i = dabr_ref[...] + t_r, dabi_ref[...] + t_i
        q_r, q_i = _cmul(er, ei, inv_r, inv_i)
        dl_r, dl_i = _cmul(der, dei, q_r, -q_i)
        dl_r, dl_i = -dl_r, -dl_i
        gw_r, gw_i = _cmul(gab_r, gab_i, abr_ref[...], -abi_ref[...])
        dar_ref[...] = dl_r + dt * gw_r
        dai_ref[...] = dl_i + dt * gw_i
        ddt_ref[...] = (gw_r * ar + gw_i * ai) * dt

    col = jax.ShapeDtypeStruct((N_STATE, 1), F32)
    mat = jax.ShapeDtypeStruct((N_STATE, GROUP_CH), F32)
    return pl.pallas_call(body, name="ssm_discretize_bwd", out_shape=[col] * 3 + [mat] * 2)(
        a_re, a_im, log_dt, b_re, b_im, ab_re, ab_im, e_re, e_im, dab_re, dab_im, dbb_re, dbb_im)


def _scan_tables(p_re, p_im, reverse):
    pr, pi = p_re.T, p_im.T
    if reverse:
        pi = -pi
    bc = lambda t, n: jnp.broadcast_to(t[n - 1], (SCAN_ROWS, N_STATE))
    carry = (pr[::-1], pi[::-1]) if reverse else (pr, pi)
    return jnp.stack([bc(pr, 1), bc(pi, 1), bc(pr, 2), bc(pi, 2), bc(pr, 4), bc(pi, 4), carry[0], carry[1]])


def _scan_group(xr, xi, tab_ref, ls, carry_r, carry_i, reverse):
    row = lax.broadcasted_iota(jnp.int32, (SCAN_ROWS, SCAN_LANES), 0)
    for n, s in enumerate((1, 2, 4)):
        if reverse:
            keep = row < SCAN_ROWS - s
            shr, shi = pltpu.roll(xr, SCAN_ROWS - s, 0), pltpu.roll(xi, SCAN_ROWS - s, 0)
        else:
            keep = row >= s
            shr, shi = pltpu.roll(xr, s, 0), pltpu.roll(xi, s, 0)
        shr, shi = jnp.where(keep, shr, 0.0), jnp.where(keep, shi, 0.0)
        mr, mi = _cmul(tab_ref[2 * n, :, ls], tab_ref[2 * n + 1, :, ls], shr, shi)
        xr, xi = xr + mr, xi + mi
    mr, mi = _cmul(tab_ref[6, :, ls], tab_ref[7, :, ls], carry_r, carry_i)
    return xr + mr, xi + mi


def _gelu(y):
    t = jnp.tanh(GELU_C * (y + 0.044715 * y * y * y))
    return 0.5 * y * (1.0 + t), t


def _ssm_fwd(u, tab, bd_re, bd_im, cd_re, cd_im, d_skip, w_glu, b_glu):
    T = u.shape[0]
    tc = SSM_CHUNK

    def body(u_ref, tab_ref, bdr_ref, bdi_ref, cdr_ref, cdi_ref, dsk_ref, wg_ref, bg_ref,
             sr_ref, si_ref, yp_ref, o_ref, car_r, car_i):
        @pl.when(pl.program_id(0) == 0)
        def _():
            car_r[...] = jnp.zeros_like(car_r)
            car_i[...] = jnp.zeros_like(car_i)

        uv = u_ref[...]
        for cs, ss in SSM_HALVES:
            sr_ref[:, ss] = _dot(uv[:, cs], bdr_ref[cs, ss])
            si_ref[:, ss] = _dot(uv[:, cs], bdi_ref[cs, ss])
        for lb in range(N_STATE // SCAN_LANES):
            ls = pl.ds(lb * SCAN_LANES, SCAN_LANES)

            def step(g, carry):
                rows = pl.ds(pl.multiple_of(g * SCAN_ROWS, SCAN_ROWS), SCAN_ROWS)
                xr, xi = _scan_group(sr_ref[rows, ls], si_ref[rows, ls], tab_ref, ls, carry[0], carry[1], False)
                sr_ref[rows, ls] = xr
                si_ref[rows, ls] = xi
                last = slice(SCAN_ROWS - 1, SCAN_ROWS)
                return (jnp.broadcast_to(xr[last], xr.shape), jnp.broadcast_to(xi[last], xi.shape))

            cr, ci = lax.fori_loop(0, tc // SCAN_ROWS, step, (car_r[:, ls], car_i[:, ls]))
            car_r[:, ls] = cr
            car_i[:, ls] = ci
        y = jnp.concatenate([_dot(sr_ref[:, ss], cdr_ref[ss, cs]) - _dot(si_ref[:, ss], cdi_ref[ss, cs])
                             for cs, ss in SSM_HALVES], axis=1) + dsk_ref[...] * uv
        yp_ref[...] = y
        gy, _ = _gelu(y)
        o_ref[...] = gy * jax.nn.sigmoid(_dot(gy, wg_ref[...]) + bg_ref[...])

    col, st = _rows(tc, SSM_W), _rows(tc, N_STATE)
    vec = _full((1, SSM_W))
    return pl.pallas_call(
        body, name="ssm_fwd", grid=(T // tc,),
        in_specs=[col, _full(tab.shape), _full(bd_re.shape), _full(bd_im.shape), _full(cd_re.shape), _full(cd_im.shape),
                  vec, _full(w_glu.shape), vec],
        out_specs=[st, st, col, col],
        out_shape=[jax.ShapeDtypeStruct((T, N_STATE), F32)] * 2 + [jax.ShapeDtypeStruct((T, SSM_W), F32)] * 2,
        scratch_shapes=[pltpu.VMEM((SCAN_ROWS, N_STATE), F32)] * 2,
        compiler_params=_params("arbitrary"),
    )(u, tab, bd_re, bd_im, cd_re, cd_im, d_skip, w_glu, b_glu)


def _ssm_bwd(dout, u, yp, s_re, s_im, tab, bd_re, bd_im, cd_re, cd_im, d_skip, w_glu, b_glu, rider=None):
    T = u.shape[0]
    tc = SSM_CHUNK
    nt = T // tc
    rows_per_chunk = tc // SCAN_ROWS

    def body(do_ref, u_ref, yp_ref, sr_ref, si_ref, pr_ref, pi_ref, tab_ref, dsk_ref, wg_ref, bg_ref,
             bdr_hbm, bdi_hbm, cdr_hbm, cdi_hbm,
             du_ref, dsk_out, dbg_out, dwg_out, da_out, dbdr_hbm, dbdi_hbm, dcdr_hbm, dcdi_hbm,
             bdr_v, bdi_v, cdr_v, cdi_v, dbdr_v, dbdi_v, dcdr_v, dcdi_v, gr_v, gi_v, car_r, car_i, sems):
        i = pl.program_id(0)

        @pl.when(i == 0)
        def _():
            _load_once([(bdr_hbm, bdr_v), (bdi_hbm, bdi_v), (cdr_hbm, cdr_v), (cdi_hbm, cdi_v)], sems)
            for ref in (dbdr_v, dbdi_v, dcdr_v, dcdi_v, car_r, car_i, dsk_out, dbg_out, dwg_out, da_out):
                ref[...] = jnp.zeros_like(ref)

        uv, y, dout_v = u_ref[...], yp_ref[...], do_ref[...]
        gy, t = _gelu(y)
        sg = jax.nn.sigmoid(_dot(gy, wg_ref[...]) + bg_ref[...])
        dzg = dout_v * gy * sg * (1.0 - sg)
        dgy = dout_v * sg + _dot_nt(dzg, wg_ref[...])
        dwg_out[...] += _dot_tn(gy, dzg)
        dbg_out[...] += jnp.sum(dzg, axis=0, keepdims=True)
        dy = dgy * (0.5 * (1.0 + t) + 0.5 * y * (1.0 - t * t) * GELU_C * (1.0 + 3 * 0.044715 * y * y))
        dsk_out[...] += jnp.sum(dy * uv, axis=0, keepdims=True)

        for cs, ss in SSM_HALVES:
            gr_v[:, ss] = _dot_nt(dy[:, cs], cdr_v[ss, cs])
            gi_v[:, ss] = -_dot_nt(dy[:, cs], cdi_v[ss, cs])
            dcdr_v[ss, cs] += _dot_tn(sr_ref[:, ss], dy[:, cs])
            dcdi_v[ss, cs] -= _dot_tn(si_ref[:, ss], dy[:, cs])

        row = lax.broadcasted_iota(jnp.int32, (SCAN_ROWS, SCAN_LANES), 0)
        first_chunk = i == nt - 1
        for lb in range(N_STATE // SCAN_LANES):
            ls = pl.ds(lb * SCAN_LANES, SCAN_LANES)

            def step(n, carry):
                g = rows_per_chunk - 1 - n
                rows = pl.ds(pl.multiple_of(g * SCAN_ROWS, SCAN_ROWS), SCAN_ROWS)
                before = pl.ds(pl.multiple_of(jnp.maximum(g - 1, 0) * SCAN_ROWS, SCAN_ROWS), SCAN_ROWS)
                xr, xi = _scan_group(gr_v[rows, ls], gi_v[rows, ls], tab_ref, ls, carry[0], carry[1], True)
                gr_v[rows, ls] = xr
                gi_v[rows, ls] = xi
                last = slice(SCAN_ROWS - 1, SCAN_ROWS)
                edge_r = jnp.where(g > 0, sr_ref[before, ls][last], jnp.where(first_chunk, 0.0, pr_ref[:, ls][last]))
                edge_i = jnp.where(g > 0, si_ref[before, ls][last], jnp.where(first_chunk, 0.0, pi_ref[:, ls][last]))
                spr = jnp.where(row >= 1, pltpu.roll(sr_ref[rows, ls], 1, 0), edge_r)
                spi = jnp.where(row >= 1, pltpu.roll(si_ref[rows, ls], 1, 0), edge_i)
                first = slice(0, 1)
                return (jnp.broadcast_to(xr[first], xr.shape), jnp.broadcast_to(xi[first], xi.shape),
                        carry[2] + xr * spr + xi * spi, carry[3] + xi * spr - xr * spi)

            zero = jnp.zeros((SCAN_ROWS, SCAN_LANES), F32)
            cr, ci, dar, dai = lax.fori_loop(0, rows_per_chunk, step, (car_r[:, ls], car_i[:, ls], zero, zero))
            car_r[:, ls] = cr
            car_i[:, ls] = ci
            da_out[0, :, ls] += dar
            da_out[1, :, ls] += dai

        du_ref[...] = dsk_ref[...] * dy + jnp.concatenate(
            [_dot_nt(gr_v[:, ss], bdr_v[cs, ss]) + _dot_nt(gi_v[:, ss], bdi_v[cs, ss]) for cs, ss in SSM_HALVES], axis=1)
        for cs, ss in SSM_HALVES:
            dbdr_v[cs, ss] += _dot_tn(uv[:, cs], gr_v[:, ss])
            dbdi_v[cs, ss] += _dot_tn(uv[:, cs], gi_v[:, ss])

        @pl.when(i == nt - 1)
        def _():
            outs = [(dbdr_v, dbdr_hbm), (dbdi_v, dbdi_hbm), (dcdr_v, dcdr_hbm), (dcdi_v, dcdi_hbm)]
            copies = [pltpu.make_async_copy(src, dst, sems.at[k]) for k, (src, dst) in enumerate(outs)]
            for c in copies:
                c.start()
            for c in copies:
                c.wait()

    rev = lambda i: (nt - 1 - i, 0)
    col = pl.BlockSpec((tc, SSM_W), rev)
    st = pl.BlockSpec((tc, N_STATE), rev)
    st_before = pl.BlockSpec((SCAN_ROWS, N_STATE), lambda i: (jnp.maximum((nt - 1 - i) * rows_per_chunk - 1, 0), 0))
    vec = _full((1, SSM_W))
    bd = jax.ShapeDtypeStruct(bd_re.shape, F32)
    cd = jax.ShapeDtypeStruct(cd_re.shape, F32)
    return _pallas(
        body, name="ssm_bwd", grid=(nt,),
        in_specs=[col, col, col, st, st, st_before, st_before, _full(tab.shape), vec, _full(w_glu.shape), vec,
                  ANY, ANY, ANY, ANY],
        out_specs=[col, vec, vec, _full(w_glu.shape), _full((2, SCAN_ROWS, N_STATE)), ANY, ANY, ANY, ANY],
        out_shape=[jax.ShapeDtypeStruct((T, SSM_W), F32), jax.ShapeDtypeStruct((1, SSM_W), F32),
                   jax.ShapeDtypeStruct((1, SSM_W), F32), jax.ShapeDtypeStruct(w_glu.shape, F32),
                   jax.ShapeDtypeStruct((2, SCAN_ROWS, N_STATE), F32), bd, bd, cd, cd],
        scratch_shapes=[pltpu.VMEM(bd_re.shape, BF16)] * 2 + [pltpu.VMEM(cd_re.shape, BF16)] * 2
        + [pltpu.VMEM(bd_re.shape, F32)] * 2 + [pltpu.VMEM(cd_re.shape, F32)] * 2
        + [pltpu.VMEM((tc, N_STATE), F32)] * 2 + [pltpu.VMEM((SCAN_ROWS, N_STATE), F32)] * 2
        + [pltpu.SemaphoreType.DMA((4,))],
        semantics=("arbitrary",), rider=rider,
        args=(dout, u, yp, s_re, s_im, s_re, s_im, tab, d_skip, w_glu, b_glu, bd_re, bd_im, cd_re, cd_im))


def _block_diag(t):
    g, a, b = t.shape
    eye = jnp.eye(N_GROUPS, dtype=t.dtype)
    return (t[:, :, None, :] * eye[:, None, :, None]).reshape(g * a, g * b)


def _diag_blocks(m, a, b):
    eye = jnp.eye(N_GROUPS, dtype=m.dtype)
    return jnp.sum(m.reshape(N_GROUPS, a, N_GROUPS, b) * eye[:, None, :, None], axis=2)


def _ssm_prepare(a_re, a_im, log_dt, b_re, b_im, c_re, c_im):
    col = lambda t: t.reshape(N_STATE, 1)
    ldt = jnp.broadcast_to(log_dt.reshape(N_GROUPS, 1), (N_GROUPS, STATE)).reshape(N_STATE, 1)
    b2r, b2i = b_re.reshape(N_STATE, GROUP_CH), b_im.reshape(N_STATE, GROUP_CH)
    ab_r, ab_i, e_r, e_i, bb_r, bb_i, p_r, p_i = _ssm_discretize(col(a_re), col(a_im), ldt, b2r, b2i)
    bd = [_block_diag(jnp.swapaxes(t.reshape(N_GROUPS, STATE, GROUP_CH), 1, 2)).astype(BF16) for t in (bb_r, bb_i)]
    cd = [_block_diag(jnp.swapaxes(t.reshape(N_GROUPS, GROUP_CH, STATE), 1, 2)).astype(BF16) for t in (c_re, c_im)]
    saved = dict(a_re=col(a_re), a_im=col(a_im), log_dt=ldt, b_re=b2r, b_im=b2i, ab_re=ab_r, ab_im=ab_i, e_re=e_r, e_im=e_i)
    return _scan_tables(p_r, p_i, False), _scan_tables(p_r, p_i, True), bd, cd, saved


BIG = ("ffn1_w_in", "ffn1_w_out", "w_mix_in", "w_glu", "w_mix_out", "ffn2_w_in", "ffn2_w_out")
SMALL = ("ffn1_pre_g", "ffn1_post_g", "mix_pre_g", "a_re", "a_im", "log_dt", "b_re", "b_im", "c_re", "c_im",
         "d_skip", "b_glu", "mix_post_g", "ffn2_pre_g", "ffn2_post_g")


FIRST = ("ffn1_w_in", "ffn1_w_out")
REST = ("w_mix_in", "w_glu", "w_mix_out", "ffn2_w_in", "ffn2_w_out")
LATE = ("w_mix_in", "w_glu", "w_mix_out", "ffn1_w_out")
SHARD_SHAPE = {"ffn1_w_in": (D_MODEL, FF_BLK), "ffn2_w_in": (D_MODEL, FF_BLK), "ffn1_w_out": (D_FF // N_CHIPS, D_MODEL),
               "ffn2_w_out": (D_FF // N_CHIPS, D_MODEL), "w_mix_in": (D_MODEL, ATTN_W), "w_glu": (SSM_W // N_CHIPS, SSM_W),
               "w_mix_out": (2 * ATTN_W // N_CHIPS, D_MODEL)}


class _Reduction:
    def __init__(self, names, grads, place):
        self.names, self.local, self.place = list(names), list(grads), place

    def exchange(self):
        return _pair_exchange(self.local)

    def first(self, got):
        self.pair = [_pair_sum(a, b, self.place["core"], name=f"pair_sum_{n}") for n, a, b in zip(self.names, self.local, got)]
        return _reduce_first(self.pair)

    def second(self, got):
        self.sums = [_reduce_sum_first(a, b, self.place["sel_first"], name=f"sum_first_{n}")
                     for n, a, b in zip(self.names, self.pair, got)]
        return _reduce_second(self.sums)

    def swap(self, got):
        self.halves = [_reduce_sum_second(a, b, self.place["sel_second"], name=f"sum_second_{n}").reshape(2 * a.shape[2], a.shape[3])
                       for n, a, b in zip(self.names, self.sums, got)]
        return _pair_swap(self.halves)

    def done(self, got):
        return {n: (mine, theirs) for n, mine, theirs in zip(self.names, self.halves, got)}


def _local_step(x, target, p, w, place=None):
    vec = lambda t: t.reshape(1, -1)
    w1_in, w1_out = w["ffn1_w_in"], w["ffn1_w_out"].reshape(2, FF_BLK, D_MODEL)
    blocks = lambda n, t: t.reshape((N_CHIPS,) + SHARD_SHAPE[n])

    ffn1 = functools.partial(_ffn_fwd, x, vec(p["ffn1_pre_g"]), w1_in, w1_out, vec(p["ffn1_post_g"]), None, name="ffn1_fwd")
    if place is None:
        x1, z1, f1 = ffn1()
    else:
        (x1, z1, f1), rest = ffn1(rider=_gather_weights([w[n] for n in REST]))
        w = dict(w, **dict(zip(REST, rest)))
    w2_in, w2_out = w["ffn2_w_in"], w["ffn2_w_out"].reshape(2, FF_BLK, D_MODEL)
    w_mi, w_glu, w_mo = w["w_mix_in"], w["w_glu"].reshape(SSM_W, SSM_W), w["w_mix_out"].reshape(2, ATTN_W, D_MODEL)
    q, k, v, u = _mix_in_fwd(x1, vec(p["mix_pre_g"]), w_mi)
    attn, lse = _attention_fwd(q, k, v)
    tab_f, tab_b, bd, cd, sv = _ssm_prepare(p["a_re"], p["a_im"], p["log_dt"], p["b_re"], p["b_im"], p["c_re"], p["c_im"])
    ssm_args = (bd[0], bd[1], cd[0], cd[1], vec(p["d_skip"]), w_glu, vec(p["b_glu"]))
    s_re, s_im, yp, ssm = _ssm_fwd(u, tab_f, *ssm_args)
    x2, mixed = _mix_out_fwd(x1, attn, ssm, w_mo, vec(p["mix_post_g"]))
    dx3, loss_rows, z2, f2 = _ffn_fwd(x2, vec(p["ffn2_pre_g"]), w2_in, w2_out, vec(p["ffn2_post_g"]), target, name="ffn2_fwd")

    g = {}
    ride = (lambda call, exchange: call(rider=exchange)) if place else (lambda call, exchange: (call(), None))
    dz2, g["ffn2_post_g"], dw2_out = _ffn_bwd_out(dx3, f2, z2, w2_out, vec(p["ffn2_post_g"]), name="ffn2_bwd_out")
    dw2_in = _norm_matmul_dw(dz2, x2, vec(p["ffn2_pre_g"]), N_CHIPS, name="ffn2_bwd_dw")
    early = _Reduction(("ffn2_w_in", "ffn2_w_out"), [dw2_in, blocks("ffn2_w_out", dw2_out)], place) if place else None
    (dx2, g["ffn2_pre_g"]), got = ride(
        functools.partial(_norm_matmul_dx, dz2, x2, dx3, vec(p["ffn2_pre_g"]), w2_in, name="ffn2_bwd_dx"), early and early.exchange())
    dattn, dssm, dw_mo, g["mix_post_g"] = _mix_out_bwd(dx2, mixed, attn, ssm, w_mo, vec(p["mix_post_g"]))
    (du, g["d_skip"], g["b_glu"], dw_glu, da, dbd_re, dbd_im, dcd_re, dcd_im), got = ride(
        functools.partial(_ssm_bwd, dssm, u, yp, s_re, s_im, tab_b, *ssm_args), early and early.first(got))
    branch = lambda d: functools.partial(_attn_branch_bwd, q, k, v, attn, lse, dattn, d)
    parts = [None] * 3
    parts[0], got = ride(branch(DILATIONS[0]), early and early.second(got))
    parts[1], early_theirs = ride(branch(DILATIONS[1]), early and early.swap(got))
    parts[2] = branch(DILATIONS[2])()
    dproj = _dproj_merge([r[0] for r in parts], [r[1] for r in parts], [r[2] for r in parts], du)
    dw_mi = _norm_matmul_dw(dproj, x1, vec(p["mix_pre_g"]), N_CHIPS, name="mix_bwd_dw")
    dx1, g["mix_pre_g"] = _norm_matmul_dx(dproj, x1, dx2, vec(p["mix_pre_g"]), w_mi, name="mix_bwd_dx")
    dz1, g["ffn1_post_g"], dw1_out = _ffn_bwd_out(dx1, f1, z1, w1_out, vec(p["ffn1_post_g"]), name="ffn1_bwd_out")
    big = {"ffn2_w_in": dw2_in, "ffn2_w_out": dw2_out, "w_mix_in": dw_mi, "w_glu": dw_glu, "w_mix_out": dw_mo, "ffn1_w_out": dw1_out}
    late = _Reduction(LATE, [blocks(n, big[n]) for n in LATE], place) if place else None
    big["ffn1_w_in"], got = ride(
        functools.partial(_norm_matmul_dw, dz1, x, vec(p["ffn1_pre_g"]), N_CHIPS, name="ffn1_bwd_dw"), late and late.exchange())
    (grad_x, g["ffn1_pre_g"]), got = ride(
        functools.partial(_norm_matmul_dx, dz1, x, dx1, vec(p["ffn1_pre_g"]), w1_in, name="ffn1_bwd_dx"), late and late.first(got))
    if place:
        last = _Reduction(("ffn1_w_in",), [big["ffn1_w_in"]], place)
        n_late = len(LATE)
        got = _merged(late.second(got), last.exchange()).run("tail_exchange")
        got = _merged(late.swap(got[:n_late]), last.first(got[n_late:])).run("tail_first")
        g.update(late.done(got[:n_late]))
        got = last.swap(last.second(got[n_late:]).run("tail_second")).run("tail_swap")
        g.update(last.done(got))
        g.update(early.done(early_theirs))
    else:
        g.update({n: blocks(n, big[n]) for n in BIG})

    g["c_re"], g["c_im"] = (jnp.swapaxes(_diag_blocks(m, STATE, GROUP_CH), 1, 2) for m in (dcd_re, dcd_im))
    dbb = [jnp.swapaxes(_diag_blocks(m, GROUP_CH, STATE), 1, 2).reshape(N_STATE, GROUP_CH) for m in (dbd_re, dbd_im)]
    dab = [jnp.sum(da[n], axis=0).reshape(N_STATE, 1) for n in range(2)]
    da_re, da_im, dldt, db_re, db_im = _ssm_discretize_bwd(
        sv["a_re"], sv["a_im"], sv["log_dt"], sv["b_re"], sv["b_im"], sv["ab_re"], sv["ab_im"], sv["e_re"], sv["e_im"],
        dab[0], dab[1], dbb[0], dbb[1])
    g["a_re"], g["a_im"] = da_re.reshape(N_GROUPS, STATE), da_im.reshape(N_GROUPS, STATE)
    g["log_dt"] = jnp.sum(dldt.reshape(N_GROUPS, STATE), axis=-1)
    g["b_re"], g["b_im"] = (t.reshape(N_GROUPS, STATE, GROUP_CH) for t in (db_re, db_im))
    return loss_rows, grad_x, g


MESH = pl.DeviceIdType.MESH
N_REL = 3


def _place():
    x, y, c = lax.axis_index("x"), lax.axis_index("y"), lax.axis_index("c")
    return x, y, c, [(1 - x, y), (x, 1 - y), (1 - x, 1 - y)]


def _remote(src, dst, send_sems, recv_sems, idx, to):
    return pltpu.make_async_remote_copy(src_ref=src, dst_ref=dst, send_sem=send_sems.at[idx], recv_sem=recv_sems.at[idx],
                                        device_id=to, device_id_type=MESH)


def _half(rows, who):
    return pl.ds(who * (rows // 2), rows // 2)


class _Copies:
    def __init__(self, send_sems, recv_sems, local_sems):
        self.send_sems, self.recv_sems, self.local_sems = send_sems, recv_sems, local_sems
        self.n_remote = self.n_local = 0

    def remote(self, src, dst, to):
        k, self.n_remote = self.n_remote, self.n_remote + 1
        return pltpu.make_async_remote_copy(src_ref=src, dst_ref=dst, send_sem=self.send_sems.at[k],
                                            recv_sem=self.recv_sems.at[k], device_id=to, device_id_type=MESH)

    def local(self, src, dst):
        k, self.n_local = self.n_local, self.n_local + 1
        return pltpu.make_async_copy(src, dst, self.local_sems.at[k])


class _Exchange:
    def __init__(self, plan, ins, out_shapes, n_remote):
        self.plan, self.ins, self.out_shapes, self.n_remote = plan, list(ins), list(out_shapes), n_remote
        self.sems = [pltpu.SemaphoreType.DMA((n_remote,)), pltpu.SemaphoreType.DMA((n_remote,)), pltpu.SemaphoreType.DMA((1,))]

    def run(self, name):
        n_in = len(self.ins)

        def body(*refs):
            for phase in self.plan(refs[:n_in], refs[n_in:-3], _Copies(*refs[-3:])):
                for cp in phase:
                    cp.start()
                for cp in phase:
                    cp.wait()

        return pl.pallas_call(body, name=name, in_specs=[ANY] * n_in, out_specs=[ANY] * len(self.out_shapes),
                              out_shape=self.out_shapes, scratch_shapes=self.sems)(*self.ins)


def _merged(a, b):
    n_in, n_out = len(a.ins), len(a.out_shapes)

    def plan(ins, outs, mk):
        (phase_a,), (phase_b,) = a.plan(ins[:n_in], outs[:n_out], mk), b.plan(ins[n_in:], outs[n_out:], mk)
        return [phase_a + phase_b]

    return _Exchange(plan, a.ins + b.ins, a.out_shapes + b.out_shapes, a.n_remote + b.n_remote)


def _pallas(body, *, name, grid, in_specs, out_specs, out_shape, args, semantics, scratch_shapes=(), rider=None):
    if rider is None:
        return pl.pallas_call(body, name=name, grid=grid, in_specs=in_specs, out_specs=out_specs, out_shape=out_shape,
                              scratch_shapes=list(scratch_shapes), compiler_params=_params(*semantics))(*args)
    n_in, n_out, r_in, r_out = len(in_specs), len(out_specs), len(rider.ins), len(rider.out_shapes)
    n_steps = math.prod(grid)

    def carrier(*refs):
        ins, rider_ins = refs[:n_in], refs[n_in:n_in + r_in]
        outs = refs[n_in + r_in:n_in + r_in + n_out]
        rider_outs = refs[n_in + r_in + n_out:n_in + r_in + n_out + r_out]
        scratch, sems = refs[n_in + r_in + n_out + r_out:-3], refs[-3:]
        step = 0
        for axis, size in enumerate(grid):
            step = step * size + pl.program_id(axis)
        phases = rider.plan(rider_ins, rider_outs, _Copies(*sems))

        def start_phase(p):
            for cp in (phases[p - 1] if p else []):
                cp.wait()
            for cp in phases[p]:
                cp.start()

        for p in range(len(phases)):
            pl.when(step == p * n_steps // len(phases))(functools.partial(start_phase, p))
        body(*ins, *outs, *scratch)

        @pl.when(step == n_steps - 1)
        def _():
            for cp in phases[-1]:
                cp.wait()

    results = pl.pallas_call(
        carrier, name=name, grid=grid, in_specs=list(in_specs) + [ANY] * r_in, out_specs=list(out_specs) + [ANY] * r_out,
        out_shape=list(out_shape) + rider.out_shapes, scratch_shapes=list(scratch_shapes) + rider.sems,
        compiler_params=_params(*["arbitrary"] * len(grid)))(*args, *rider.ins)
    return results[:n_out], results[n_out:]


def _gather_weights(shards):
    def plan(ins, outs, mk):
        x, y, c = lax.axis_index("x"), lax.axis_index("y"), lax.axis_index("c")
        me, sibling = 2 * x + y, (x, y, 1 - c)
        x_nb, y_nb, diag = (1 - x, y), (x, 1 - y), (1 - x, 1 - y)
        index = lambda chip: 2 * chip[0] + chip[1]
        first, second, third = [], [], []
        for i, shard in enumerate(shards):
            rows = shard.shape[0]
            mine = _half(rows, c)
            quarter = lambda which: pl.ds(c * (rows // 2) + which * (rows // 4), rows // 4)
            first.append(mk.remote(ins[i], outs[i].at[me], sibling))
            for nb in (x_nb, y_nb):
                first.append(mk.remote(ins[i].at[mine], outs[i].at[me, mine], (*nb, c)))
            for nb in (x_nb, y_nb):
                landed = outs[i].at[index(nb), mine]
                second.append(mk.remote(landed, landed, sibling))
            for nb, other, which in ((x_nb, y_nb, 0), (y_nb, x_nb, 1)):
                landed = outs[i].at[index(nb), quarter(which)]
                second.append(mk.remote(landed, landed, (*other, c)))
            landed = outs[i].at[index(diag), mine]
            third.append(mk.remote(landed, landed, sibling))
        return [first, second, third]

    return _Exchange(plan, shards, [jax.ShapeDtypeStruct((N_CHIPS,) + s.shape, s.dtype) for s in shards], 8 * len(shards))


def _pair_exchange(grads):
    def plan(ins, outs, mk):
        x, y, c = lax.axis_index("x"), lax.axis_index("y"), lax.axis_index("c")
        return [[mk.remote(ins[i].at[:, _half(g.shape[1], 1 - c)], outs[i], (x, y, 1 - c)) for i, g in enumerate(grads)]]

    return _Exchange(plan, grads, [jax.ShapeDtypeStruct((N_CHIPS, g.shape[1] // 2, g.shape[2]), g.dtype) for g in grads], len(grads))


def _reduce_first(pair):
    def plan(ins, outs, mk):
        x, y, c = lax.axis_index("x"), lax.axis_index("y"), lax.axis_index("c")
        phase = []
        for i, p in enumerate(pair):
            q = p.shape[1] // 2
            phase.append(mk.remote(ins[i].at[pl.ds(2 * (1 - x), 2), pl.ds(0, q)], outs[i].at[0], (1 - x, y, c)))
            for jx in range(2):
                phase.append(mk.remote(ins[i].at[2 * jx + 1 - y, pl.ds(q, q)], outs[i].at[1, jx], (x, 1 - y, c)))
        return [phase]

    return _Exchange(plan, pair, [jax.ShapeDtypeStruct((2, 2, p.shape[1] // 2, p.shape[2]), p.dtype) for p in pair], 3 * len(pair))


def _reduce_second(sums):
    def plan(ins, outs, mk):
        x, y, c = lax.axis_index("x"), lax.axis_index("y"), lax.axis_index("c")
        phase = []
        for i in range(len(sums)):
            phase.append(mk.remote(ins[i].at[0, 1 - y], outs[i].at[0], (x, 1 - y, c)))
            phase.append(mk.remote(ins[i].at[1, 1 - x], outs[i].at[1], (1 - x, y, c)))
        return [phase]

    return _Exchange(plan, sums, [jax.ShapeDtypeStruct((2,) + s.shape[2:], s.dtype) for s in sums], 2 * len(sums))


def _pair_swap(halves):
    def plan(ins, outs, mk):
        x, y, c = lax.axis_index("x"), lax.axis_index("y"), lax.axis_index("c")
        return [[mk.remote(ins[i], outs[i], (x, y, 1 - c)) for i in range(len(halves))]]

    return _Exchange(plan, halves, [jax.ShapeDtypeStruct(h.shape, h.dtype) for h in halves], len(halves))


def _allreduce_small(packed):
    rows = packed.shape[0]
    n_dev = 2 * N_CHIPS

    def body(x_ref, o_ref, buf, send_sems, recv_sems):
        x, y, c, chips = _place()
        sibling = (x, y, 1 - c)

        def slot(px, py, pc):
            return buf.at[4 * px + 2 * py + pc]

        buf[4 * x + 2 * y + c] = x_ref[...]
        first = [_remote(x_ref, slot(x, y, c), send_sems, recv_sems, 0, sibling)]
        first += [_remote(x_ref, slot(x, y, c), send_sems, recv_sems, 1 + k, (*chip, c)) for k, chip in enumerate(chips)]
        for cp in first:
            cp.start()
        passed = []
        for k, chip in enumerate(chips):
            landed = slot(*chip, c)
            _remote(landed, landed, send_sems, recv_sems, 1 + k, (*chip, c)).wait_recv()
            passed.append(_remote(landed, landed, send_sems, recv_sems, 1 + N_REL + k, sibling))
            passed[-1].start()
        _remote(slot(*sibling), slot(*sibling), send_sems, recv_sems, 0, sibling).wait_recv()
        for k, chip in enumerate(chips):
            landed = slot(*chip, 1 - c)
            _remote(landed, landed, send_sems, recv_sems, 1 + N_REL + k, sibling).wait_recv()
        for cp in first + passed:
            cp.wait_send()
        total = buf[0]
        for d in range(1, n_dev):
            total = total + buf[d]
        o_ref[...] = total

    vm = pl.BlockSpec(memory_space=pltpu.VMEM)
    return pl.pallas_call(
        body, name="allreduce_small", in_specs=[vm], out_specs=vm, out_shape=jax.ShapeDtypeStruct(packed.shape, F32),
        scratch_shapes=[pltpu.VMEM((n_dev, rows, 128), F32), pltpu.SemaphoreType.DMA((1 + 2 * N_REL,)),
                        pltpu.SemaphoreType.DMA((1 + 2 * N_REL,))],
    )(packed)


def _row_tile(rows, cap=256):
    return max(t for t in range(8, cap + 1, 8) if rows % t == 0)


def _pair_sum(grad, got, c, name):
    _, half, cols = got.shape
    tr = _row_tile(half)
    nt = half // tr

    def body(c_ref, g_ref, r_ref, o_ref):
        o_ref[...] = g_ref[...] + r_ref[...]

    blk = (1, tr, cols)
    return pl.pallas_call(
        body, name=name,
        grid_spec=pltpu.PrefetchScalarGridSpec(
            num_scalar_prefetch=1, grid=(N_CHIPS, nt),
            in_specs=[pl.BlockSpec(blk, lambda j, t, c_ref: (j, c_ref[0] * nt + t, 0)),
                      pl.BlockSpec(blk, lambda j, t, c_ref: (j, t, 0))],
            out_specs=pl.BlockSpec(blk, lambda j, t, c_ref: (j, t, 0))),
        out_shape=jax.ShapeDtypeStruct(got.shape, F32), compiler_params=_params("parallel", "parallel"),
    )(c, grad, got)


def _reduce_sum_first(pair, got, sel, name):
    _, _, q, cols = got.shape
    tr = _row_tile(q)
    nt = q // tr

    def body(sel_ref, p_ref, r_ref, o_ref):
        o_ref[0, 0] = p_ref[0] + r_ref[0, 0]

    blk = (1, 1, tr, cols)
    return pl.pallas_call(
        body, name=name,
        grid_spec=pltpu.PrefetchScalarGridSpec(
            num_scalar_prefetch=1, grid=(2, 2, nt),
            in_specs=[pl.BlockSpec((1, tr, cols), lambda p, k, t, s: (s[2 * p] + s[2 * p + 1] * k, p * nt + t, 0)),
                      pl.BlockSpec(blk, lambda p, k, t, s: (p, k, t, 0))],
            out_specs=pl.BlockSpec(blk, lambda p, k, t, s: (p, k, t, 0))),
        out_shape=jax.ShapeDtypeStruct(got.shape, F32), compiler_params=_params("parallel", "parallel", "parallel"),
    )(sel, pair, got)


def _reduce_sum_second(sums, got, sel, name):
    _, q, cols = got.shape
    tr = _row_tile(q)

    def body(sel_ref, s_ref, r_ref, o_ref):
        o_ref[0] = s_ref[0, 0] + r_ref[0]

    blk = (1, tr, cols)
    return pl.pallas_call(
        body, name=name,
        grid_spec=pltpu.PrefetchScalarGridSpec(
            num_scalar_prefetch=1, grid=(2, q // tr),
            in_specs=[pl.BlockSpec((1, 1, tr, cols), lambda p, t, s: (p, s[p], t, 0)),
                      pl.BlockSpec(blk, lambda p, t, s: (p, t, 0))],
            out_specs=pl.BlockSpec(blk, lambda p, t, s: (p, t, 0))),
        out_shape=jax.ShapeDtypeStruct(got.shape, F32), compiler_params=_params("parallel", "parallel"),
    )(sel, sums, got)


def _adamw_update(w, g, m, v):
    m2 = ADAM_B1 * m + (1.0 - ADAM_B1) * g
    v2 = ADAM_B2 * v + (1.0 - ADAM_B2) * (g * g)
    m_hat = m2 / (1.0 - ADAM_B1 ** ADAM_STEP)
    v_hat = v2 / (1.0 - ADAM_B2 ** ADAM_STEP)
    return -ADAM_LR * (m_hat / (jnp.sqrt(v_hat) + ADAM_EPS) + ADAM_WD * w), m2, v2


def _adamw(w, g, m, v, name):
    rows, cols = w.shape
    tr = _row_tile(rows)

    def body(w_ref, g_ref, m_ref, v_ref, d_ref, mo_ref, vo_ref):
        d_ref[...], mo_ref[...], vo_ref[...] = _adamw_update(w_ref[...], g_ref[...], m_ref[...], v_ref[...])

    blk = _rows(tr, cols)
    return pl.pallas_call(
        body, name=name, grid=(rows // tr,), in_specs=[blk] * 4, out_specs=[blk] * 3,
        out_shape=[jax.ShapeDtypeStruct(w.shape, F32)] * 3, compiler_params=_params("parallel"),
    )(w, g, m, v)


def _adamw_halves(w, mine, theirs, m, v, core, name):
    rows, cols = w.shape
    tr = _row_tile(rows // 2)
    per_half = rows // 2 // tr

    def body(core_ref, w_ref, a_ref, b_ref, m_ref, v_ref, g_ref, d_ref, mo_ref, vo_ref):
        g = jnp.where(pl.program_id(0) // per_half == core_ref[0], a_ref[...], b_ref[...])
        g_ref[...] = g
        d_ref[...], mo_ref[...], vo_ref[...] = _adamw_update(w_ref[...], g, m_ref[...], v_ref[...])

    blk = pl.BlockSpec((tr, cols), lambda t, c: (t, 0))
    half = lambda own: pl.BlockSpec(
        (tr, cols), lambda t, c: (jnp.clip(t - (c[0] if own else 1 - c[0]) * per_half, 0, per_half - 1), 0))
    return pl.pallas_call(
        body, name=name,
        grid_spec=pltpu.PrefetchScalarGridSpec(
            num_scalar_prefetch=1, grid=(2 * per_half,), in_specs=[blk, half(True), half(False), blk, blk], out_specs=[blk] * 4),
        out_shape=[jax.ShapeDtypeStruct(w.shape, F32)] * 4, compiler_params=_params("arbitrary"),
    )(core, w, mine, theirs, m, v)


def _pack(parts):
    flat = []
    for t in parts:
        t = t.reshape(-1).astype(F32)
        flat.append(jnp.pad(t, (0, -t.shape[0] % 128)))
    flat = jnp.concatenate(flat)
    return jnp.pad(flat, (0, -flat.shape[0] % 1024)).reshape(-1, 128)


def _unpack(buf, shapes):
    flat, out, at = buf.reshape(-1), [], 0
    for s in shapes:
        size = math.prod(s)
        out.append(flat[at:at + size].reshape(s))
        at += size + (-size % 128)
    return out


def kernel(x, ffn1_pre_g, ffn1_w_in, ffn1_w_out, ffn1_post_g, mix_pre_g, w_mix_in, a_re, a_im, log_dt, b_re, b_im, c_re, c_im, d_skip, w_glu, b_glu, w_mix_out, mix_post_g, ffn2_pre_g, ffn2_w_in, ffn2_w_out, ffn2_post_g, loss_target, m_ffn1_pre_g, m_ffn1_w_in, m_ffn1_w_out, m_ffn1_post_g, m_mix_pre_g, m_w_mix_in, m_a_re, m_a_im, m_log_dt, m_b_re, m_b_im, m_c_re, m_c_im, m_d_skip, m_w_glu, m_b_glu, m_w_mix_out, m_mix_post_g, m_ffn2_pre_g, m_ffn2_w_in, m_ffn2_w_out, m_ffn2_post_g, v_ffn1_pre_g, v_ffn1_w_in, v_ffn1_w_out, v_ffn1_post_g, v_mix_pre_g, v_w_mix_in, v_a_re, v_a_im, v_log_dt, v_b_re, v_b_im, v_c_re, v_c_im, v_d_skip, v_w_glu, v_b_glu, v_w_mix_out, v_mix_post_g, v_ffn2_pre_g, v_ffn2_w_in, v_ffn2_w_out, v_ffn2_post_g):
    given = dict(locals())
    order = ("ffn1_pre_g", "ffn1_w_in", "ffn1_w_out", "ffn1_post_g", "mix_pre_g", "w_mix_in", "a_re", "a_im", "log_dt",
             "b_re", "b_im", "c_re", "c_im", "d_skip", "w_glu", "b_glu", "w_mix_out", "mix_post_g", "ffn2_pre_g",
             "ffn2_w_in", "ffn2_w_out", "ffn2_post_g")
    at_x, at_y, at_c = (lax.axis_index(a).astype(jnp.int32) for a in ("x", "y", "c"))
    place = dict(core=at_c.reshape(1), sel_first=jnp.stack([2 * at_x, jnp.int32(1), at_y, jnp.int32(2)]),
                 sel_second=jnp.stack([at_y, at_x]))

    shards = {n: given[n][0] for n in BIG}
    w = {n: shards[n].astype(BF16) for n in REST}
    w.update(zip(FIRST, _gather_weights([shards[n].astype(BF16) for n in FIRST]).run("gather_first")))
    small = {n: given[n][0] for n in SMALL}
    loss_rows, grad_x, g = _local_step(x[0], loss_target[0], small, w, place)

    total = _allreduce_small(_pack([g[n] for n in SMALL] + [loss_rows[0, :1]]))
    parts = _unpack(total, [small[n].shape for n in SMALL] + [(1,)])
    grads = dict(zip(SMALL, parts[:-1]))
    loss = parts[-1][0]

    delta, new_m, new_v = {}, {}, {}
    for n in BIG:
        grads[n], delta[n], new_m[n], new_v[n] = _adamw_halves(
            shards[n], *g[n], given["m_" + n][0], given["v_" + n][0], place["core"], name=f"adamw_{n}")
    packed = [_pack([given[pre + n] for n in SMALL]) for pre in ("", "m_", "v_")]
    outs = _adamw(packed[0], _pack([grads[n] for n in SMALL]), packed[1], packed[2], name="adamw_small")
    for store, buf in zip((delta, new_m, new_v), outs):
        store.update(zip(SMALL, _unpack(buf, [small[n].shape for n in SMALL])))

    lead = lambda d: [d[n][None] for n in order]
    return (loss, grad_x[None], *lead(grads), *lead(delta), *lead(new_m), *lead(new_v))
```

```python
import functools
import math

import jax
import jax.numpy as jnp
from jax import lax
from jax.experimental import pallas as pl
from jax.experimental.pallas import tpu as pltpu

F32, BF16 = jnp.float32, jnp.bfloat16

D_MODEL = 1024
D_FF = 2816
N_CHIPS = 4
FF_BLK = 2 * D_FF // N_CHIPS
ATTN_W = 512
SSM_W = 512
HEAD_DIM = 64
N_HEADS = ATTN_W // HEAD_DIM
DILATIONS = (1, 4, 16)
N_BACK = 128
QBLK = 128
N_GROUPS = 32
GROUP_CH = 16
STATE = 64
N_STATE = N_GROUPS * STATE
EPS = 1e-6
NEG = -1e30
GELU_C = math.sqrt(2.0 / math.pi)

ADAM_LR, ADAM_B1, ADAM_B2, ADAM_EPS, ADAM_WD, ADAM_STEP = 0.001, 0.9, 0.999, 1e-08, 0.01, 10

VMEM_LIMIT_V7X = 60 * 1024 * 1024
ROW_TILE = 256
FFN_ROW_TILE = 512
DW_ROW_TILE = 1024


def _params(*sem):
    return pltpu.CompilerParams(dimension_semantics=sem, vmem_limit_bytes=VMEM_LIMIT_V7X)


def _dot(a, b):
    return jnp.dot(a.astype(BF16), b.astype(BF16), preferred_element_type=F32)


def _dot_nt(a, b):
    return lax.dot_general(a.astype(BF16), b.astype(BF16), (((1,), (1,)), ((), ())), preferred_element_type=F32)


def _dot_tn(a, b):
    return lax.dot_general(a.astype(BF16), b.astype(BF16), (((0,), (0,)), ((), ())), preferred_element_type=F32)


def _full(shape):
    return pl.BlockSpec(shape, lambda *_: (0,) * len(shape))


def _rows(tm, width):
    return pl.BlockSpec((tm, width), lambda i: (i, 0))


ANY = pl.BlockSpec(memory_space=pl.ANY)


def _load_once(pairs, sems):
    copies = [pltpu.make_async_copy(src, dst, sems.at[k]) for k, (src, dst) in enumerate(pairs)]
    for c in copies:
        c.start()
    for c in copies:
        c.wait()


def _rms(x):
    return lax.rsqrt(jnp.mean(x * x, axis=-1, keepdims=True) + EPS)


def _rms_bwd(dy_g, xn, r):
    return r * (dy_g - xn * jnp.mean(dy_g * xn, axis=-1, keepdims=True))


def _ffn_fwd(x, g_pre, w_in, w_out, g_post, target, *, name, rider=None):
    T = x.shape[0]
    tm = FFN_ROW_TILE
    with_loss = target is not None

    def body(*refs):
        if with_loss:
            x_ref, gpre_ref, gpost_ref, tgt_ref, win_hbm, wout_hbm, o_ref, loss_ref, z_ref, f_ref, win_v, wout_v, sems = refs
        else:
            x_ref, gpre_ref, gpost_ref, win_hbm, wout_hbm, o_ref, z_ref, f_ref, win_v, wout_v, sems = refs

        @pl.when(pl.program_id(0) == 0)
        def _():
            _load_once([(win_hbm, win_v), (wout_hbm, wout_v)], sems)
            if with_loss:
                loss_ref[...] = jnp.zeros_like(loss_ref)

        xv = x_ref[...]
        h = (xv * _rms(xv) * gpre_ref[...]).astype(BF16)
        f = jnp.zeros((tm, D_MODEL), F32)
        for k in range(2):
            gate = _dot(h, win_v[k])
            up = _dot(h, win_v[k + 2])
            z_ref[:, k * FF_BLK:(k + 1) * FF_BLK] = gate.astype(BF16)
            z_ref[:, D_FF + k * FF_BLK:D_FF + (k + 1) * FF_BLK] = up.astype(BF16)
            f = f + _dot(gate * jax.nn.sigmoid(gate) * up, wout_v[k])
        f_ref[...] = f
        out = xv + 0.5 * (f * _rms(f) * gpost_ref[...])
        if with_loss:
            err = out - tgt_ref[...]
            o_ref[...] = err * (1.0 / D_MODEL)
            loss_ref[...] += jnp.sum(err * err) * (0.5 / D_MODEL)
        else:
            o_ref[...] = out

    row = _rows(tm, D_MODEL)
    vec = _full((1, D_MODEL))
    in_specs = [row, vec, vec] + ([row] if with_loss else []) + [ANY, ANY]
    out_shape = [jax.ShapeDtypeStruct((T, D_MODEL), F32)]
    out_specs = [row]
    if with_loss:
        out_shape.append(jax.ShapeDtypeStruct((8, 128), F32))
        out_specs.append(_full((8, 128)))
    out_shape += [jax.ShapeDtypeStruct((T, 2 * D_FF), BF16), jax.ShapeDtypeStruct((T, D_MODEL), F32)]
    out_specs += [_rows(tm, 2 * D_FF), row]
    args = (x, g_pre, g_post) + ((target,) if with_loss else ()) + (w_in, w_out)
    return _pallas(
        body, name=name, grid=(T // tm,), in_specs=in_specs, out_specs=out_specs, out_shape=out_shape,
        scratch_shapes=[pltpu.VMEM(w_in.shape, BF16), pltpu.VMEM(w_out.shape, BF16), pltpu.SemaphoreType.DMA((2,))],
        semantics=("arbitrary",), args=args, rider=rider)


def _ffn_bwd_out(dout, f, z, w_out, g_post, *, name):
    T = dout.shape[0]
    tm = FFN_ROW_TILE
    nt = T // tm

    def body(dout_ref, f_ref, z_ref, gpost_ref, wout_hbm, dz_ref, dgpost_ref, dwout_hbm, wout_v, dwout_v, sems):
        i = pl.program_id(0)

        @pl.when(i == 0)
        def _():
            _load_once([(wout_hbm, wout_v)], sems)
            dwout_v[...] = jnp.zeros_like(dwout_v)
            dgpost_ref[...] = jnp.zeros_like(dgpost_ref)

        dy = 0.5 * dout_ref[...]
        f = f_ref[...]
        r = _rms(f)
        fn = f * r
        dgpost_ref[...] += jnp.sum(dy * fn, axis=0, keepdims=True)
        df = _rms_bwd(dy * gpost_ref[...], fn, r).astype(BF16)
        for k in range(2):
            gate = z_ref[:, k * FF_BLK:(k + 1) * FF_BLK].astype(F32)
            up = z_ref[:, D_FF + k * FF_BLK:D_FF + (k + 1) * FF_BLK].astype(F32)
            sg = jax.nn.sigmoid(gate)
            silu = gate * sg
            dwout_v[k] += _dot_tn(silu * up, df)
            da = _dot_nt(df, wout_v[k])
            dz_ref[:, k * FF_BLK:(k + 1) * FF_BLK] = (da * up * (sg * (1.0 + gate * (1.0 - sg)))).astype(BF16)
            dz_ref[:, D_FF + k * FF_BLK:D_FF + (k + 1) * FF_BLK] = (da * silu).astype(BF16)

        @pl.when(i == nt - 1)
        def _():
            c = pltpu.make_async_copy(dwout_v, dwout_hbm, sems.at[0])
            c.start()
            c.wait()

    row = _rows(tm, D_MODEL)
    return pl.pallas_call(
        body, name=name, grid=(nt,),
        in_specs=[row, row, _rows(tm, 2 * D_FF), _full((1, D_MODEL)), ANY],
        out_specs=[_rows(tm, 2 * D_FF), _full((1, D_MODEL)), ANY],
        out_shape=[jax.ShapeDtypeStruct((T, 2 * D_FF), BF16), jax.ShapeDtypeStruct((1, D_MODEL), F32),
                   jax.ShapeDtypeStruct(w_out.shape, F32)],
        scratch_shapes=[pltpu.VMEM(w_out.shape, BF16), pltpu.VMEM(w_out.shape, F32), pltpu.SemaphoreType.DMA((1,))],
        compiler_params=_params("arbitrary"),
    )(dout, f, z, g_post, w_out)


def _norm_matmul_dw(dz, x, g, n_blocks, *, name, rider=None):
    T = x.shape[0]
    bw = dz.shape[1] // n_blocks
    tm = DW_ROW_TILE

    def body(dz_ref, x_ref, g_ref, dw_ref):
        @pl.when(pl.program_id(1) == 0)
        def _():
            dw_ref[...] = jnp.zeros_like(dw_ref)

        xv = x_ref[...]
        dw_ref[0] += _dot_tn(xv * _rms(xv) * g_ref[...], dz_ref[...])

    res = _pallas(
        body, name=name, grid=(n_blocks, T // tm),
        in_specs=[pl.BlockSpec((tm, bw), lambda j, t: (t, j)), pl.BlockSpec((tm, D_MODEL), lambda j, t: (t, 0)), _full((1, D_MODEL))],
        out_specs=[pl.BlockSpec((1, D_MODEL, bw), lambda j, t: (j, 0, 0))],
        out_shape=[jax.ShapeDtypeStruct((n_blocks, D_MODEL, bw), F32)], semantics=("parallel", "arbitrary"),
        args=(dz, x, g), rider=rider)
    return res[0] if rider is None else (res[0][0], res[1])


def _norm_matmul_dx(dz, x, dres, g, w, *, name, rider=None, part=(0, 1), into=None):
    T = x.shape[0]
    nb, _, bw = w.shape
    tm = FFN_ROW_TILE
    steps = T // part[1] // tm
    first = part[0] * steps

    def body(dz_ref, x_ref, dres_ref, g_ref, w_hbm, *rest):
        dx_ref, dg_ref, w_v, sems = rest[-4:]

        @pl.when(pl.program_id(0) == 0)
        def _():
            _load_once([(w_hbm, w_v)], sems)
            dg_ref[...] = jnp.zeros_like(dg_ref)

        xv = x_ref[...]
        r = _rms(xv)
        xn = xv * r
        gv = g_ref[...]
        dh = jnp.zeros((tm, D_MODEL), F32)
        for j in range(nb):
            dh = dh + _dot_nt(dz_ref[:, j * bw:(j + 1) * bw], w_v[j])
        dg_ref[...] += jnp.sum(dh * xn, axis=0, keepdims=True)
        dx_ref[...] = _rms_bwd(dh * gv, xn, r) + dres_ref[...]

    rows = lambda width: pl.BlockSpec((tm, width), lambda i: (first + i, 0))
    row = rows(D_MODEL)
    return _pallas(
        body, name=name, grid=(steps,),
        in_specs=[rows(nb * bw), row, row, _full((1, D_MODEL)), ANY] + ([] if into is None else [ANY]),
        out_specs=[row, _full((1, D_MODEL))],
        out_shape=[jax.ShapeDtypeStruct((T, D_MODEL), F32), jax.ShapeDtypeStruct((1, D_MODEL), F32)],
        scratch_shapes=[pltpu.VMEM(w.shape, BF16), pltpu.SemaphoreType.DMA((1,))],
        semantics=("arbitrary",), args=(dz, x, dres, g, w) + (() if into is None else (into,)), rider=rider,
        aliases={} if into is None else {5: 0})


def _mix_in_fwd(x, g, w):
    T = x.shape[0]
    tm = ROW_TILE

    def body(x_ref, g_ref, w_ref, q_ref, k_ref, v_ref, u_ref):
        xv = x_ref[...]
        h = (xv * _rms(xv) * g_ref[...]).astype(BF16)
        for j, o_ref in enumerate((q_ref, k_ref, v_ref, u_ref)):
            o_ref[...] = _dot(h, w_ref[j])

    col = _rows(tm, ATTN_W)
    return pl.pallas_call(
        body, name="mix_in_fwd", grid=(T // tm,),
        in_specs=[_rows(tm, D_MODEL), _full((1, D_MODEL)), _full(w.shape)],
        out_specs=[col] * 4, out_shape=[jax.ShapeDtypeStruct((T, ATTN_W), F32)] * 4,
        compiler_params=_params("parallel"),
    )(x, g, w)


def _mix_out_fwd(x, attn, ssm, w, g):
    T = x.shape[0]
    tm = ROW_TILE

    def body(x_ref, a_ref, s_ref, w_ref, g_ref, o_ref, m_ref):
        mixed = _dot(a_ref[...], w_ref[0]) + _dot(s_ref[...], w_ref[1])
        m_ref[...] = mixed
        o_ref[...] = x_ref[...] + mixed * _rms(mixed) * g_ref[...]

    row, col = _rows(tm, D_MODEL), _rows(tm, ATTN_W)
    return pl.pallas_call(
        body, name="mix_out_fwd", grid=(T // tm,),
        in_specs=[row, col, col, _full(w.shape), _full((1, D_MODEL))],
        out_specs=[row, row], out_shape=[jax.ShapeDtypeStruct((T, D_MODEL), F32)] * 2,
        compiler_params=_params("parallel"),
    )(x, attn, ssm, w, g)


def _mix_out_bwd(dout, mixed, attn, ssm, w, g, rider=None):
    T = dout.shape[0]
    tm = ROW_TILE

    def body(dout_ref, m_ref, a_ref, s_ref, w_ref, g_ref, da_ref, ds_ref, dw_ref, dg_ref):
        @pl.when(pl.program_id(0) == 0)
        def _():
            dw_ref[...] = jnp.zeros_like(dw_ref)
            dg_ref[...] = jnp.zeros_like(dg_ref)

        dy = dout_ref[...]
        mixed = m_ref[...]
        r = _rms(mixed)
        mn = mixed * r
        dg_ref[...] += jnp.sum(dy * mn, axis=0, keepdims=True)
        dm = _rms_bwd(dy * g_ref[...], mn, r).astype(BF16)
        da_ref[...] = _dot_nt(dm, w_ref[0])
        ds_ref[...] = _dot_nt(dm, w_ref[1])
        dw_ref[0] += _dot_tn(a_ref[...], dm)
        dw_ref[1] += _dot_tn(s_ref[...], dm)

    row, col = _rows(tm, D_MODEL), _rows(tm, ATTN_W)
    return _pallas(
        body, name="mix_out_bwd", grid=(T // tm,),
        in_specs=[row, row, col, col, _full(w.shape), _full((1, D_MODEL))],
        out_specs=[col, col, _full(w.shape), _full((1, D_MODEL))],
        out_shape=[jax.ShapeDtypeStruct((T, ATTN_W), F32)] * 2
        + [jax.ShapeDtypeStruct(w.shape, F32), jax.ShapeDtypeStruct((1, D_MODEL), F32)],
        semantics=("arbitrary",), args=(dout, mixed, attn, ssm, w, g), rider=rider)


ATTN_TILING = {1: (ATTN_W, 1), 4: (2 * HEAD_DIM, 4), 16: (2 * HEAD_DIM, 4)}


def _class_rows(d, r):
    return (pl.ds(r, QBLK, stride=d), slice(None)) if d > 1 else (slice(None), slice(None))


def _for_class_groups(d, group, fn):
    if d == group:
        fn(0)
    else:
        lax.fori_loop(0, d // group, lambda n, carry: (fn(n * group), carry)[1], 0)


def _block_slopes(lanes, lane_block):
    heads = lanes // HEAD_DIM
    first = lane_block * heads
    return [jnp.exp2(-jnp.full((1, 1), first + hh + 1, jnp.int32).astype(F32)) for hh in range(heads)]


def _attn_specs(d, nb, lanes):
    blk = (QBLK * d, lanes)
    cur = pl.BlockSpec(blk, lambda j, lb: (j, lb))
    prev = pl.BlockSpec(blk, lambda j, lb: (jnp.maximum(j - 1, 0), lb))
    nxt = pl.BlockSpec(blk, lambda j, lb: (jnp.minimum(j + 1, nb - 1), lb))
    return cur, prev, nxt


def _attn_branch_fwd(q, k, v, d):
    T = q.shape[0]
    nb = T // (d * QBLK)
    lanes, group = ATTN_TILING[d]
    scale = HEAD_DIM ** -0.5

    def body(q_ref, kc_ref, kp_ref, vc_ref, vp_ref, o_ref, l_ref, q_s, kk_s, vv_s, o_s, l_s):
        j = pl.program_id(0)
        qi = lax.broadcasted_iota(jnp.int32, (QBLK, 2 * QBLK), 0)
        ci = lax.broadcasted_iota(jnp.int32, (QBLK, 2 * QBLK), 1)
        steps = QBLK + qi - ci
        valid = (steps >= 0) & (steps <= N_BACK) & ((ci >= QBLK) | (j > 0))
        dist = (steps * d).astype(F32)
        slopes = _block_slopes(lanes, pl.program_id(1))

        def classes(first):
            for n in range(group):
                rows = _class_rows(d, first + n)
                q_s[n] = q_ref[rows]
                kk_s[n, :QBLK], kk_s[n, QBLK:] = kp_ref[rows], kc_ref[rows]
                vv_s[n, :QBLK], vv_s[n, QBLK:] = vp_ref[rows], vc_ref[rows]
            for n in range(group):
                for hh in range(lanes // HEAD_DIM):
                    sl = slice(hh * HEAD_DIM, (hh + 1) * HEAD_DIM)
                    s = _dot_nt(q_s[n, :, sl], kk_s[n, :, sl]) * scale - slopes[hh] * dist
                    s = jnp.where(valid, s, NEG)
                    m = jnp.max(s, axis=-1, keepdims=True)
                    p = jnp.exp(s - m)
                    den = jnp.sum(p, axis=-1, keepdims=True)
                    o_s[n, :, sl] = _dot(p, vv_s[n, :, sl]) / den
                    l_s[n, :, sl] = jnp.broadcast_to(m + jnp.log(den), (QBLK, HEAD_DIM))
            for n in range(group):
                rows = _class_rows(d, first + n)
                o_ref[rows] = o_s[n]
                l_ref[rows] = l_s[n]

        _for_class_groups(d, group, classes)

    cur, prev, _ = _attn_specs(d, nb, lanes)
    shape = jax.ShapeDtypeStruct((T, ATTN_W), F32)
    one, two = pltpu.VMEM((group, QBLK, lanes), F32), pltpu.VMEM((group, 2 * QBLK, lanes), F32)
    return pl.pallas_call(
        body, name=f"attn_fwd_d{d}", grid=(nb, ATTN_W // lanes),
        in_specs=[cur, cur, prev, cur, prev], out_specs=[cur, cur], out_shape=[shape, shape],
        scratch_shapes=[one, two, two, one, one], compiler_params=_params("parallel", "parallel"),
    )(q, k, k, v, v)


def _attn_merge(outs, lses):
    T = outs[0].shape[0]
    tm = 512

    def body(o1, o2, o3, l1, l2, l3, a_ref, lse_ref):
        ls = [l1[...], l2[...], l3[...]]
        m = jnp.maximum(jnp.maximum(ls[0], ls[1]), ls[2])
        lse = m + jnp.log(jnp.exp(ls[0] - m) + jnp.exp(ls[1] - m) + jnp.exp(ls[2] - m))
        lse_ref[...] = lse
        a_ref[...] = jnp.exp(ls[0] - lse) * o1[...] + jnp.exp(ls[1] - lse) * o2[...] + jnp.exp(ls[2] - lse) * o3[...]

    col = _rows(tm, ATTN_W)
    return pl.pallas_call(
        body, name="attn_merge", grid=(T // tm,), in_specs=[col] * 6, out_specs=[col, col],
        out_shape=[jax.ShapeDtypeStruct((T, ATTN_W), F32)] * 2, compiler_params=_params("parallel"),
    )(*outs, *lses)


def _attn_branch_bwd(q, k, v, o, lse, do, d, rider=None):
    T = q.shape[0]
    nb = T // (d * QBLK)
    lanes, group = ATTN_TILING[d]
    scale = HEAD_DIM ** -0.5

    def body(q_ref, kc_ref, kp_ref, vc_ref, vp_ref, o_ref, l_ref, do_ref, dq_ref, dk_ref, dv_ref,
             q_s, o_s, l_s, do_s, kk_s, vv_s, dq_s, dk_s, dv_s, ck_s, cv_s):
        j = pl.program_id(1)

        @pl.when(j == 0)
        def _():
            ck_s[...] = jnp.zeros_like(ck_s)
            cv_s[...] = jnp.zeros_like(cv_s)

        qi = lax.broadcasted_iota(jnp.int32, (QBLK, 2 * QBLK), 0)
        ci = lax.broadcasted_iota(jnp.int32, (QBLK, 2 * QBLK), 1)
        steps = QBLK + qi - ci
        valid = (steps >= 0) & (steps <= N_BACK) & ((ci >= QBLK) | (j > 0))
        dist = (steps * d).astype(F32)
        lo, hi = slice(0, QBLK), slice(QBLK, 2 * QBLK)
        slopes = _block_slopes(lanes, pl.program_id(0))

        def classes(first):
            for n in range(group):
                rows = _class_rows(d, first + n)
                q_s[n], o_s[n], l_s[n], do_s[n] = q_ref[rows], o_ref[rows], l_ref[rows], do_ref[rows]
                kk_s[n, lo], kk_s[n, hi] = kp_ref[rows], kc_ref[rows]
                vv_s[n, lo], vv_s[n, hi] = vp_ref[rows], vc_ref[rows]
            for n in range(group):
                for hh in range(lanes // HEAD_DIM):
                    sl = slice(hh * HEAD_DIM, (hh + 1) * HEAD_DIM)
                    qh, doh, kk, vv = q_s[n, :, sl], do_s[n, :, sl], kk_s[n, :, sl], vv_s[n, :, sl]
                    delta = jnp.sum(doh * o_s[n, :, sl], axis=-1, keepdims=True)
                    s = jnp.where(valid, _dot_nt(qh, kk) * scale - slopes[hh] * dist, NEG)
                    p = jnp.exp(s - l_s[n, :, hh * HEAD_DIM:hh * HEAD_DIM + 1])
                    ds = p * (_dot_nt(doh, vv) - delta)
                    dq_s[n, :, sl] = _dot(ds, kk) * scale
                    dkk = _dot_tn(ds, qh) * scale
                    dvv = _dot_tn(p, doh)
                    dk_s[n, :, sl] = ck_s[first + n, :, sl] + dkk[lo]
                    dv_s[n, :, sl] = cv_s[first + n, :, sl] + dvv[lo]
                    ck_s[first + n, :, sl] = dkk[hi]
                    cv_s[first + n, :, sl] = dvv[hi]
            for n in range(group):
                rows = _class_rows(d, first + n)
                dq_ref[rows] = dq_s[n]
                dk_ref[rows] = dk_s[n]
                dv_ref[rows] = dv_s[n]

        @pl.when(j < nb)
        def _():
            _for_class_groups(d, group, classes)

        @pl.when(j == nb)
        def _():
            for r in range(d):
                dk_ref[_class_rows(d, r)] = ck_s[r]
                dv_ref[_class_rows(d, r)] = cv_s[r]

    blk = (QBLK * d, lanes)
    here = lambda j: jnp.minimum(j, nb - 1)
    cur = pl.BlockSpec(blk, lambda lb, j: (here(j), lb))
    prev = pl.BlockSpec(blk, lambda lb, j: (jnp.maximum(here(j) - 1, 0), lb))
    behind = pl.BlockSpec(blk, lambda lb, j: (jnp.maximum(j - 1, 0), lb))
    shape = jax.ShapeDtypeStruct((T, ATTN_W), F32)
    one, two = pltpu.VMEM((group, QBLK, lanes), F32), pltpu.VMEM((group, 2 * QBLK, lanes), F32)
    carry = pltpu.VMEM((d, QBLK, lanes), F32)
    return _pallas(
        body, name=f"attn_bwd_d{d}", grid=(ATTN_W // lanes, nb + 1),
        in_specs=[cur, cur, prev, cur, prev, cur, cur, cur], out_specs=[cur, behind, behind], out_shape=[shape] * 3,
        scratch_shapes=[one] * 4 + [two] * 2 + [one] * 3 + [carry] * 2,
        semantics=("parallel", "arbitrary"), args=(q, k, k, v, v, o, lse, do), rider=rider)


def _dproj_merge(dqs, dks, dvs, du):
    T = du.shape[0]
    tm = 512

    def body(*refs):
        o_ref = refs[-1]
        for part in range(3):
            a, b, c = refs[3 * part:3 * part + 3]
            o_ref[:, part * ATTN_W:(part + 1) * ATTN_W] = (a[...] + b[...] + c[...]).astype(BF16)
        o_ref[:, 3 * ATTN_W:] = refs[9][...].astype(BF16)

    col = _rows(tm, ATTN_W)
    return pl.pallas_call(
        body, name="dproj_merge", grid=(T // tm,), in_specs=[col] * 10, out_specs=_rows(tm, 4 * ATTN_W),
        out_shape=jax.ShapeDtypeStruct((T, 4 * ATTN_W), BF16), compiler_params=_params("parallel"),
    )(*dqs, *dks, *dvs, du)


def _attention_fwd(q, k, v):
    res = [_attn_branch_fwd(q, k, v, d) for d in DILATIONS]
    return _attn_merge([r[0] for r in res], [r[1] for r in res])


SCAN_ROWS = 8
SCAN_LANES = 512
SSM_CHUNK = 256
SSM_HALVES = tuple((slice(h * SSM_W // 2, (h + 1) * SSM_W // 2), slice(h * N_STATE // 2, (h + 1) * N_STATE // 2)) for h in range(2))


def _cmul(ar, ai, br, bi):
    return ar * br - ai * bi, ar * bi + ai * br


def _ssm_discretize(a_re, a_im, log_dt, b_re, b_im):
    def body(ar_ref, ai_ref, ldt_ref, br_ref, bi_ref, abr_ref, abi_ref, er_ref, ei_ref, bbr_ref, bbi_ref, pr_ref, pi_ref):
        ar, ai = ar_ref[...], ai_ref[...]
        dt = jnp.exp(ldt_ref[...])
        n = lax.broadcasted_iota(jnp.int32, (1, SCAN_ROWS), 1).astype(F32) + 1.0
        mag, ang = jnp.exp(dt * ar), dt * ai
        abr, abi = mag * jnp.cos(ang), mag * jnp.sin(ang)
        abr_ref[...], abi_ref[...] = abr, abi
        pr_ref[...] = jnp.exp(dt * ar * n) * jnp.cos(ang * n)
        pi_ref[...] = jnp.exp(dt * ar * n) * jnp.sin(ang * n)
        den = ar * ar + ai * ai
        er = ((abr - 1.0) * ar + abi * ai) / den
        ei = (abi * ar - (abr - 1.0) * ai) / den
        er_ref[...], ei_ref[...] = er, ei
        bbr_ref[...], bbi_ref[...] = _cmul(er, ei, br_ref[...], bi_ref[...])

    col = jax.ShapeDtypeStruct((N_STATE, 1), F32)
    mat = jax.ShapeDtypeStruct((N_STATE, GROUP_CH), F32)
    pw = jax.ShapeDtypeStruct((N_STATE, SCAN_ROWS), F32)
    return pl.pallas_call(body, name="ssm_discretize", out_shape=[col] * 4 + [mat] * 2 + [pw] * 2)(
        a_re, a_im, log_dt, b_re, b_im)


def _ssm_discretize_bwd(a_re, a_im, log_dt, b_re, b_im, ab_re, ab_im, e_re, e_im, dab_re, dab_im, dbb_re, dbb_im):
    def body(ar_ref, ai_ref, ldt_ref, br_ref, bi_ref, abr_ref, abi_ref, er_ref, ei_ref, dabr_ref, dabi_ref,
             dbbr_ref, dbbi_ref, dar_ref, dai_ref, ddt_ref, dbr_ref, dbi_ref):
        ar, ai, dt = ar_ref[...], ai_ref[...], jnp.exp(ldt_ref[...])
        er, ei = er_ref[...], ei_ref[...]
        gbr, gbi = dbbr_ref[...], dbbi_ref[...]
        dbr_ref[...], dbi_ref[...] = _cmul(er, -ei, gbr, gbi)
        br, bi = br_ref[...], bi_ref[...]
        der = jnp.sum(br * gbr + bi * gbi, axis=-1, keepdims=True)
        dei = jnp.sum(br * gbi - bi * gbr, axis=-1, keepdims=True)
        den = ar * ar + ai * ai
        inv_r, inv_i = ar / den, -ai / den
        t_r, t_i = _cmul(der, dei, inv_r, -inv_i)
        gab_r, gab_i = dabr_ref[...] + t_r, dabi_ref[...] + t_i
        q_r, q_i = _cmul(er, ei, inv_r, inv_i)
        dl_r, dl_i = _cmul(der, dei, q_r, -q_i)
        dl_r, dl_i = -dl_r, -dl_i
        gw_r, gw_i = _cmul(gab_r, gab_i, abr_ref[...], -abi_ref[...])
        dar_ref[...] = dl_r + dt * gw_r
        dai_ref[...] = dl_i + dt * gw_i
        ddt_ref[...] = (gw_r * ar + gw_i * ai) * dt

    col = jax.ShapeDtypeStruct((N_STATE, 1), F32)
    mat = jax.ShapeDtypeStruct((N_STATE, GROUP_CH), F32)
    return pl.pallas_call(body, name="ssm_discretize_bwd", out_shape=[col] * 3 + [mat] * 2)(
        a_re, a_im, log_dt, b_re, b_im, ab_re, ab_im, e_re, e_im, dab_re, dab_im, dbb_re, dbb_im)


def _scan_tables(p_re, p_im, reverse):
    pr, pi = p_re.T, p_im.T
    if reverse:
        pi = -pi
    bc = lambda t, n: jnp.broadcast_to(t[n - 1], (SCAN_ROWS, N_STATE))
    carry = (pr[::-1], pi[::-1]) if reverse else (pr, pi)
    return jnp.stack([bc(pr, 1), bc(pi, 1), bc(pr, 2), bc(pi, 2), bc(pr, 4), bc(pi, 4), carry[0], carry[1]])


def _scan_group(xr, xi, tab_ref, ls, carry_r, carry_i, reverse):
    row = lax.broadcasted_iota(jnp.int32, (SCAN_ROWS, SCAN_LANES), 0)
    for n, s in enumerate((1, 2, 4)):
        if reverse:
            keep = row < SCAN_ROWS - s
            shr, shi = pltpu.roll(xr, SCAN_ROWS - s, 0), pltpu.roll(xi, SCAN_ROWS - s, 0)
        else:
            keep = row >= s
            shr, shi = pltpu.roll(xr, s, 0), pltpu.roll(xi, s, 0)
        shr, shi = jnp.where(keep, shr, 0.0), jnp.where(keep, shi, 0.0)
        mr, mi = _cmul(tab_ref[2 * n, :, ls], tab_ref[2 * n + 1, :, ls], shr, shi)
        xr, xi = xr + mr, xi + mi
    mr, mi = _cmul(tab_ref[6, :, ls], tab_ref[7, :, ls], carry_r, carry_i)
    return xr + mr, xi + mi


def _gelu(y):
    t = jnp.tanh(GELU_C * (y + 0.044715 * y * y * y))
    return 0.5 * y * (1.0 + t), t


def _ssm_fwd(u, tab, bd_re, bd_im, cd_re, cd_im, d_skip, w_glu, b_glu):
    T = u.shape[0]
    tc = SSM_CHUNK

    def body(u_ref, tab_ref, bdr_ref, bdi_ref, cdr_ref, cdi_ref, dsk_ref, wg_ref, bg_ref,
             sr_ref, si_ref, yp_ref, o_ref, car_r, car_i):
        @pl.when(pl.program_id(0) == 0)
        def _():
            car_r[...] = jnp.zeros_like(car_r)
            car_i[...] = jnp.zeros_like(car_i)

        uv = u_ref[...]
        for cs, ss in SSM_HALVES:
            sr_ref[:, ss] = _dot(uv[:, cs], bdr_ref[cs, ss])
            si_ref[:, ss] = _dot(uv[:, cs], bdi_ref[cs, ss])
        for lb in range(N_STATE // SCAN_LANES):
            ls = pl.ds(lb * SCAN_LANES, SCAN_LANES)

            def step(g, carry):
                rows = pl.ds(pl.multiple_of(g * SCAN_ROWS, SCAN_ROWS), SCAN_ROWS)
                xr, xi = _scan_group(sr_ref[rows, ls], si_ref[rows, ls], tab_ref, ls, carry[0], carry[1], False)
                sr_ref[rows, ls] = xr
                si_ref[rows, ls] = xi
                last = slice(SCAN_ROWS - 1, SCAN_ROWS)
                return (jnp.broadcast_to(xr[last], xr.shape), jnp.broadcast_to(xi[last], xi.shape))

            cr, ci = lax.fori_loop(0, tc // SCAN_ROWS, step, (car_r[:, ls], car_i[:, ls]))
            car_r[:, ls] = cr
            car_i[:, ls] = ci
        y = jnp.concatenate([_dot(sr_ref[:, ss], cdr_ref[ss, cs]) - _dot(si_ref[:, ss], cdi_ref[ss, cs])
                             for cs, ss in SSM_HALVES], axis=1) + dsk_ref[...] * uv
        yp_ref[...] = y
        gy, _ = _gelu(y)
        o_ref[...] = gy * jax.nn.sigmoid(_dot(gy, wg_ref[...]) + bg_ref[...])

    col, st = _rows(tc, SSM_W), _rows(tc, N_STATE)
    vec = _full((1, SSM_W))
    return pl.pallas_call(
        body, name="ssm_fwd", grid=(T // tc,),
        in_specs=[col, _full(tab.shape), _full(bd_re.shape), _full(bd_im.shape), _full(cd_re.shape), _full(cd_im.shape),
                  vec, _full(w_glu.shape), vec],
        out_specs=[st, st, col, col],
        out_shape=[jax.ShapeDtypeStruct((T, N_STATE), F32)] * 2 + [jax.ShapeDtypeStruct((T, SSM_W), F32)] * 2,
        scratch_shapes=[pltpu.VMEM((SCAN_ROWS, N_STATE), F32)] * 2,
        compiler_params=_params("arbitrary"),
    )(u, tab, bd_re, bd_im, cd_re, cd_im, d_skip, w_glu, b_glu)


def _ssm_bwd(dout, u, yp, s_re, s_im, tab, bd_re, bd_im, cd_re, cd_im, d_skip, w_glu, b_glu, rider=None):
    T = u.shape[0]
    tc = SSM_CHUNK
    nt = T // tc
    rows_per_chunk = tc // SCAN_ROWS

    def body(do_ref, u_ref, yp_ref, sr_ref, si_ref, pr_ref, pi_ref, tab_ref, dsk_ref, wg_ref, bg_ref,
             bdr_hbm, bdi_hbm, cdr_hbm, cdi_hbm,
             du_ref, dsk_out, dbg_out, dwg_out, da_out, dbdr_hbm, dbdi_hbm, dcdr_hbm, dcdi_hbm,
             bdr_v, bdi_v, cdr_v, cdi_v, dbdr_v, dbdi_v, dcdr_v, dcdi_v, gr_v, gi_v, car_r, car_i, sems):
        i = pl.program_id(0)

        @pl.when(i == 0)
        def _():
            _load_once([(bdr_hbm, bdr_v), (bdi_hbm, bdi_v), (cdr_hbm, cdr_v), (cdi_hbm, cdi_v)], sems)
            for ref in (dbdr_v, dbdi_v, dcdr_v, dcdi_v, car_r, car_i, dsk_out, dbg_out, dwg_out, da_out):
                ref[...] = jnp.zeros_like(ref)

        uv, y, dout_v = u_ref[...], yp_ref[...], do_ref[...]
        gy, t = _gelu(y)
        sg = jax.nn.sigmoid(_dot(gy, wg_ref[...]) + bg_ref[...])
        dzg = dout_v * gy * sg * (1.0 - sg)
        dgy = dout_v * sg + _dot_nt(dzg, wg_ref[...])
        dwg_out[...] += _dot_tn(gy, dzg)
        dbg_out[...] += jnp.sum(dzg, axis=0, keepdims=True)
        dy = dgy * (0.5 * (1.0 + t) + 0.5 * y * (1.0 - t * t) * GELU_C * (1.0 + 3 * 0.044715 * y * y))
        dsk_out[...] += jnp.sum(dy * uv, axis=0, keepdims=True)

        for cs, ss in SSM_HALVES:
            gr_v[:, ss] = _dot_nt(dy[:, cs], cdr_v[ss, cs])
            gi_v[:, ss] = -_dot_nt(dy[:, cs], cdi_v[ss, cs])
            dcdr_v[ss, cs] += _dot_tn(sr_ref[:, ss], dy[:, cs])
            dcdi_v[ss, cs] -= _dot_tn(si_ref[:, ss], dy[:, cs])

        row = lax.broadcasted_iota(jnp.int32, (SCAN_ROWS, SCAN_LANES), 0)
        first_chunk = i == nt - 1
        for lb in range(N_STATE // SCAN_LANES):
            ls = pl.ds(lb * SCAN_LANES, SCAN_LANES)

            def step(n, carry):
                g = rows_per_chunk - 1 - n
                rows = pl.ds(pl.multiple_of(g * SCAN_ROWS, SCAN_ROWS), SCAN_ROWS)
                before = pl.ds(pl.multiple_of(jnp.maximum(g - 1, 0) * SCAN_ROWS, SCAN_ROWS), SCAN_ROWS)
                xr, xi = _scan_group(gr_v[rows, ls], gi_v[rows, ls], tab_ref, ls, carry[0], carry[1], True)
                gr_v[rows, ls] = xr
                gi_v[rows, ls] = xi
                last = slice(SCAN_ROWS - 1, SCAN_ROWS)
                edge_r = jnp.where(g > 0, sr_ref[before, ls][last], jnp.where(first_chunk, 0.0, pr_ref[:, ls][last]))
                edge_i = jnp.where(g > 0, si_ref[before, ls][last], jnp.where(first_chunk, 0.0, pi_ref[:, ls][last]))
                spr = jnp.where(row >= 1, pltpu.roll(sr_ref[rows, ls], 1, 0), edge_r)
                spi = jnp.where(row >= 1, pltpu.roll(si_ref[rows, ls], 1, 0), edge_i)
                first = slice(0, 1)
                return (jnp.broadcast_to(xr[first], xr.shape), jnp.broadcast_to(xi[first], xi.shape),
                        carry[2] + xr * spr + xi * spi, carry[3] + xi * spr - xr * spi)

            zero = jnp.zeros((SCAN_ROWS, SCAN_LANES), F32)
            cr, ci, dar, dai = lax.fori_loop(0, rows_per_chunk, step, (car_r[:, ls], car_i[:, ls], zero, zero))
            car_r[:, ls] = cr
            car_i[:, ls] = ci
            da_out[0, :, ls] += dar
            da_out[1, :, ls] += dai

        du_ref[...] = dsk_ref[...] * dy + jnp.concatenate(
            [_dot_nt(gr_v[:, ss], bdr_v[cs, ss]) + _dot_nt(gi_v[:, ss], bdi_v[cs, ss]) for cs, ss in SSM_HALVES], axis=1)
        for cs, ss in SSM_HALVES:
            dbdr_v[cs, ss] += _dot_tn(uv[:, cs], gr_v[:, ss])
            dbdi_v[cs, ss] += _dot_tn(uv[:, cs], gi_v[:, ss])

        @pl.when(i == nt - 1)
        def _():
            outs = [(dbdr_v, dbdr_hbm), (dbdi_v, dbdi_hbm), (dcdr_v, dcdr_hbm), (dcdi_v, dcdi_hbm)]
            copies = [pltpu.make_async_copy(src, dst, sems.at[k]) for k, (src, dst) in enumerate(outs)]
            for c in copies:
                c.start()
            for c in copies:
                c.wait()

    rev = lambda i: (nt - 1 - i, 0)
    col = pl.BlockSpec((tc, SSM_W), rev)
    st = pl.BlockSpec((tc, N_STATE), rev)
    st_before = pl.BlockSpec((SCAN_ROWS, N_STATE), lambda i: (jnp.maximum((nt - 1 - i) * rows_per_chunk - 1, 0), 0))
    vec = _full((1, SSM_W))
    bd = jax.ShapeDtypeStruct(bd_re.shape, F32)
    cd = jax.ShapeDtypeStruct(cd_re.shape, F32)
    return _pallas(
        body, name="ssm_bwd", grid=(nt,),
        in_specs=[col, col, col, st, st, st_before, st_before, _full(tab.shape), vec, _full(w_glu.shape), vec,
                  ANY, ANY, ANY, ANY],
        out_specs=[col, vec, vec, _full(w_glu.shape), _full((2, SCAN_ROWS, N_STATE)), ANY, ANY, ANY, ANY],
        out_shape=[jax.ShapeDtypeStruct((T, SSM_W), F32), jax.ShapeDtypeStruct((1, SSM_W), F32),
                   jax.ShapeDtypeStruct((1, SSM_W), F32), jax.ShapeDtypeStruct(w_glu.shape, F32),
                   jax.ShapeDtypeStruct((2, SCAN_ROWS, N_STATE), F32), bd, bd, cd, cd],
        scratch_shapes=[pltpu.VMEM(bd_re.shape, BF16)] * 2 + [pltpu.VMEM(cd_re.shape, BF16)] * 2
        + [pltpu.VMEM(bd_re.shape, F32)] * 2 + [pltpu.VMEM(cd_re.shape, F32)] * 2
        + [pltpu.VMEM((tc, N_STATE), F32)] * 2 + [pltpu.VMEM((SCAN_ROWS, N_STATE), F32)] * 2
        + [pltpu.SemaphoreType.DMA((4,))],
        semantics=("arbitrary",), rider=rider,
        args=(dout, u, yp, s_re, s_im, s_re, s_im, tab, d_skip, w_glu, b_glu, bd_re, bd_im, cd_re, cd_im))


def _block_diag(t):
    g, a, b = t.shape
    eye = jnp.eye(N_GROUPS, dtype=t.dtype)
    return (t[:, :, None, :] * eye[:, None, :, None]).reshape(g * a, g * b)


def _diag_blocks(m, a, b):
    eye = jnp.eye(N_GROUPS, dtype=m.dtype)
    return jnp.sum(m.reshape(N_GROUPS, a, N_GROUPS, b) * eye[:, None, :, None], axis=2)


def _ssm_prepare(a_re, a_im, log_dt, b_re, b_im, c_re, c_im):
    col = lambda t: t.reshape(N_STATE, 1)
    ldt = jnp.broadcast_to(log_dt.reshape(N_GROUPS, 1), (N_GROUPS, STATE)).reshape(N_STATE, 1)
    b2r, b2i = b_re.reshape(N_STATE, GROUP_CH), b_im.reshape(N_STATE, GROUP_CH)
    ab_r, ab_i, e_r, e_i, bb_r, bb_i, p_r, p_i = _ssm_discretize(col(a_re), col(a_im), ldt, b2r, b2i)
    bd = [_block_diag(jnp.swapaxes(t.reshape(N_GROUPS, STATE, GROUP_CH), 1, 2)).astype(BF16) for t in (bb_r, bb_i)]
    cd = [_block_diag(jnp.swapaxes(t.reshape(N_GROUPS, GROUP_CH, STATE), 1, 2)).astype(BF16) for t in (c_re, c_im)]
    saved = dict(a_re=col(a_re), a_im=col(a_im), log_dt=ldt, b_re=b2r, b_im=b2i, ab_re=ab_r, ab_im=ab_i, e_re=e_r, e_im=e_i)
    return _scan_tables(p_r, p_i, False), _scan_tables(p_r, p_i, True), bd, cd, saved


BIG = ("ffn1_w_in", "ffn1_w_out", "w_mix_in", "w_glu", "w_mix_out", "ffn2_w_in", "ffn2_w_out")
SMALL = ("ffn1_pre_g", "ffn1_post_g", "mix_pre_g", "a_re", "a_im", "log_dt", "b_re", "b_im", "c_re", "c_im",
         "d_skip", "b_glu", "mix_post_g", "ffn2_pre_g", "ffn2_post_g")


FIRST = ("ffn1_w_in", "ffn1_w_out")
REST = ("w_mix_in", "w_glu", "w_mix_out", "ffn2_w_in", "ffn2_w_out")
LATE = ("w_mix_in", "w_glu", "w_mix_out", "ffn1_w_out")
SHARD_SHAPE = {"ffn1_w_in": (D_MODEL, FF_BLK), "ffn2_w_in": (D_MODEL, FF_BLK), "ffn1_w_out": (D_FF // N_CHIPS, D_MODEL),
               "ffn2_w_out": (D_FF // N_CHIPS, D_MODEL), "w_mix_in": (D_MODEL, ATTN_W), "w_glu": (SSM_W // N_CHIPS, SSM_W),
               "w_mix_out": (2 * ATTN_W // N_CHIPS, D_MODEL)}


class _Reduction:
    def __init__(self, names, grads, place):
        self.names, self.local, self.place = list(names), list(grads), place

    def exchange(self):
        return _pair_exchange(self.local)

    def first(self, got):
        self.pair = [_pair_sum(a, b, self.place["core"], name=f"pair_sum_{n}") for n, a, b in zip(self.names, self.local, got)]
        return _reduce_first(self.pair)

    def second(self, got):
        self.sums = [_reduce_sum_first(a, b, self.place["sel_first"], name=f"sum_first_{n}")
                     for n, a, b in zip(self.names, self.pair, got)]
        return _reduce_second(self.sums)

    def swap(self, got):
        self.halves = [_reduce_sum_second(a, b, self.place["sel_second"], name=f"sum_second_{n}").reshape(2 * a.shape[2], a.shape[3])
                       for n, a, b in zip(self.names, self.sums, got)]
        return _pair_swap(self.halves)

    def done(self, got):
        return {n: (mine, theirs) for n, mine, theirs in zip(self.names, self.halves, got)}


def _local_step(x, target, p, w, place=None):
    vec = lambda t: t.reshape(1, -1)
    w1_in, w1_out = w["ffn1_w_in"], w["ffn1_w_out"].reshape(2, FF_BLK, D_MODEL)
    blocks = lambda n, t: t.reshape((N_CHIPS,) + SHARD_SHAPE[n])

    ffn1 = functools.partial(_ffn_fwd, x, vec(p["ffn1_pre_g"]), w1_in, w1_out, vec(p["ffn1_post_g"]), None, name="ffn1_fwd")
    if place is None:
        x1, z1, f1 = ffn1()
    else:
        (x1, z1, f1), rest = ffn1(rider=_gather_weights([w[n] for n in REST]))
        w = dict(w, **dict(zip(REST, rest)))
    w2_in, w2_out = w["ffn2_w_in"], w["ffn2_w_out"].reshape(2, FF_BLK, D_MODEL)
    w_mi, w_glu, w_mo = w["w_mix_in"], w["w_glu"].reshape(SSM_W, SSM_W), w["w_mix_out"].reshape(2, ATTN_W, D_MODEL)
    q, k, v, u = _mix_in_fwd(x1, vec(p["mix_pre_g"]), w_mi)
    attn, lse = _attention_fwd(q, k, v)
    tab_f, tab_b, bd, cd, sv = _ssm_prepare(p["a_re"], p["a_im"], p["log_dt"], p["b_re"], p["b_im"], p["c_re"], p["c_im"])
    ssm_args = (bd[0], bd[1], cd[0], cd[1], vec(p["d_skip"]), w_glu, vec(p["b_glu"]))
    s_re, s_im, yp, ssm = _ssm_fwd(u, tab_f, *ssm_args)
    x2, mixed = _mix_out_fwd(x1, attn, ssm, w_mo, vec(p["mix_post_g"]))
    dx3, loss_rows, z2, f2 = _ffn_fwd(x2, vec(p["ffn2_pre_g"]), w2_in, w2_out, vec(p["ffn2_post_g"]), target, name="ffn2_fwd")

    g = {}
    ride = (lambda call, exchange: call(rider=exchange)) if place else (lambda call, exchange: (call(), None))
    dz2, g["ffn2_post_g"], dw2_out = _ffn_bwd_out(dx3, f2, z2, w2_out, vec(p["ffn2_post_g"]), name="ffn2_bwd_out")
    dw2_in = _norm_matmul_dw(dz2, x2, vec(p["ffn2_pre_g"]), N_CHIPS, name="ffn2_bwd_dw")
    early = _Reduction(("ffn2_w_in", "ffn2_w_out"), [dw2_in, blocks("ffn2_w_out", dw2_out)], place) if place else None
    (dx2, g["ffn2_pre_g"]), got = ride(
        functools.partial(_norm_matmul_dx, dz2, x2, dx3, vec(p["ffn2_pre_g"]), w2_in, name="ffn2_bwd_dx"), early and early.exchange())
    dattn, dssm, dw_mo, g["mix_post_g"] = _mix_out_bwd(dx2, mixed, attn, ssm, w_mo, vec(p["mix_post_g"]))
    (du, g["d_skip"], g["b_glu"], dw_glu, da, dbd_re, dbd_im, dcd_re, dcd_im), got = ride(
        functools.partial(_ssm_bwd, dssm, u, yp, s_re, s_im, tab_b, *ssm_args), early and early.first(got))
    branch = lambda d: functools.partial(_attn_branch_bwd, q, k, v, attn, lse, dattn, d)
    parts = [None] * 3
    parts[0], got = ride(branch(DILATIONS[0]), early and early.second(got))
    parts[1], early_theirs = ride(branch(DILATIONS[1]), early and early.swap(got))
    parts[2] = branch(DILATIONS[2])()
    dproj = _dproj_merge([r[0] for r in parts], [r[1] for r in parts], [r[2] for r in parts], du)
    dw_mi = _norm_matmul_dw(dproj, x1, vec(p["mix_pre_g"]), N_CHIPS, name="mix_bwd_dw")
    dx1, g["mix_pre_g"] = _norm_matmul_dx(dproj, x1, dx2, vec(p["mix_pre_g"]), w_mi, name="mix_bwd_dx")
    dz1, g["ffn1_post_g"], dw1_out = _ffn_bwd_out(dx1, f1, z1, w1_out, vec(p["ffn1_post_g"]), name="ffn1_bwd_out")
    big = {"ffn2_w_in": dw2_in, "ffn2_w_out": dw2_out, "w_mix_in": dw_mi, "w_glu": dw_glu, "w_mix_out": dw_mo, "ffn1_w_out": dw1_out}
    late = _Reduction(LATE, [blocks(n, big[n]) for n in LATE], place) if place else None
    big["ffn1_w_in"], got = ride(
        functools.partial(_norm_matmul_dw, dz1, x, vec(p["ffn1_pre_g"]), N_CHIPS, name="ffn1_bwd_dw"), late and late.exchange())
    dx_call = functools.partial(_norm_matmul_dx, dz1, x, dx1, vec(p["ffn1_pre_g"]), w1_in)
    if place:
        last = _Reduction(("ffn1_w_in",), [big["ffn1_w_in"]], place)
        n_late = len(LATE)
        (dx_half, dg_a), got = dx_call(name="ffn1_bwd_dx_a", part=(0, 2), rider=_merged(late.first(got), last.exchange()))
        (grad_x, dg_b), got = dx_call(name="ffn1_bwd_dx_b", part=(1, 2), into=dx_half,
                                      rider=_merged(late.second(got[:n_late]), last.first(got[n_late:])))
        g["ffn1_pre_g"] = dg_a + dg_b
        got = _merged(late.swap(got[:n_late]), last.second(got[n_late:])).run("tail_second")
        g.update(late.done(got[:n_late]))
        g.update(last.done(last.swap(got[n_late:]).run("tail_swap")))
        g.update(early.done(early_theirs))
    else:
        grad_x, g["ffn1_pre_g"] = dx_call(name="ffn1_bwd_dx")
        g.update({n: blocks(n, big[n]) for n in BIG})

    g["c_re"], g["c_im"] = (jnp.swapaxes(_diag_blocks(m, STATE, GROUP_CH), 1, 2) for m in (dcd_re, dcd_im))
    dbb = [jnp.swapaxes(_diag_blocks(m, GROUP_CH, STATE), 1, 2).reshape(N_STATE, GROUP_CH) for m in (dbd_re, dbd_im)]
    dab = [jnp.sum(da[n], axis=0).reshape(N_STATE, 1) for n in range(2)]
    da_re, da_im, dldt, db_re, db_im = _ssm_discretize_bwd(
        sv["a_re"], sv["a_im"], sv["log_dt"], sv["b_re"], sv["b_im"], sv["ab_re"], sv["ab_im"], sv["e_re"], sv["e_im"],
        dab[0], dab[1], dbb[0], dbb[1])
    g["a_re"], g["a_im"] = da_re.reshape(N_GROUPS, STATE), da_im.reshape(N_GROUPS, STATE)
    g["log_dt"] = jnp.sum(dldt.reshape(N_GROUPS, STATE), axis=-1)
    g["b_re"], g["b_im"] = (t.reshape(N_GROUPS, STATE, GROUP_CH) for t in (db_re, db_im))
    return loss_rows, grad_x, g


MESH = pl.DeviceIdType.MESH
N_REL = 3


def _place():
    x, y, c = lax.axis_index("x"), lax.axis_index("y"), lax.axis_index("c")
    return x, y, c, [(1 - x, y), (x, 1 - y), (1 - x, 1 - y)]


def _remote(src, dst, send_sems, recv_sems, idx, to):
    return pltpu.make_async_remote_copy(src_ref=src, dst_ref=dst, send_sem=send_sems.at[idx], recv_sem=recv_sems.at[idx],
                                        device_id=to, device_id_type=MESH)


def _half(rows, who):
    return pl.ds(who * (rows // 2), rows // 2)


class _Copies:
    def __init__(self, send_sems, recv_sems, local_sems):
        self.send_sems, self.recv_sems, self.local_sems = send_sems, recv_sems, local_sems
        self.n_remote = self.n_local = 0

    def remote(self, src, dst, to):
        k, self.n_remote = self.n_remote, self.n_remote + 1
        return pltpu.make_async_remote_copy(src_ref=src, dst_ref=dst, send_sem=self.send_sems.at[k],
                                            recv_sem=self.recv_sems.at[k], device_id=to, device_id_type=MESH)

    def local(self, src, dst):
        k, self.n_local = self.n_local, self.n_local + 1
        return pltpu.make_async_copy(src, dst, self.local_sems.at[k])


class _Exchange:
    def __init__(self, plan, ins, out_shapes, n_remote):
        self.plan, self.ins, self.out_shapes, self.n_remote = plan, list(ins), list(out_shapes), n_remote
        self.sems = [pltpu.SemaphoreType.DMA((n_remote,)), pltpu.SemaphoreType.DMA((n_remote,)), pltpu.SemaphoreType.DMA((1,))]

    def run(self, name):
        n_in = len(self.ins)

        def body(*refs):
            for phase in self.plan(refs[:n_in], refs[n_in:-3], _Copies(*refs[-3:])):
                for cp in phase:
                    cp.start()
                for cp in phase:
                    cp.wait()

        return pl.pallas_call(body, name=name, in_specs=[ANY] * n_in, out_specs=[ANY] * len(self.out_shapes),
                              out_shape=self.out_shapes, scratch_shapes=self.sems)(*self.ins)


def _merged(a, b):
    n_in, n_out = len(a.ins), len(a.out_shapes)

    def plan(ins, outs, mk):
        (phase_a,), (phase_b,) = a.plan(ins[:n_in], outs[:n_out], mk), b.plan(ins[n_in:], outs[n_out:], mk)
        return [phase_a + phase_b]

    return _Exchange(plan, a.ins + b.ins, a.out_shapes + b.out_shapes, a.n_remote + b.n_remote)


def _pallas(body, *, name, grid, in_specs, out_specs, out_shape, args, semantics, scratch_shapes=(), rider=None, aliases=None):
    aliases = aliases or {}
    if rider is None:
        return pl.pallas_call(body, name=name, grid=grid, in_specs=in_specs, out_specs=out_specs, out_shape=out_shape,
                              scratch_shapes=list(scratch_shapes), input_output_aliases=aliases,
                              compiler_params=_params(*semantics))(*args)
    n_in, n_out, r_in, r_out = len(in_specs), len(out_specs), len(rider.ins), len(rider.out_shapes)
    n_steps = math.prod(grid)

    def carrier(*refs):
        ins, rider_ins = refs[:n_in], refs[n_in:n_in + r_in]
        outs = refs[n_in + r_in:n_in + r_in + n_out]
        rider_outs = refs[n_in + r_in + n_out:n_in + r_in + n_out + r_out]
        scratch, sems = refs[n_in + r_in + n_out + r_out:-3], refs[-3:]
        step = 0
        for axis, size in enumerate(grid):
            step = step * size + pl.program_id(axis)
        phases = rider.plan(rider_ins, rider_outs, _Copies(*sems))

        def start_phase(p):
            for cp in (phases[p - 1] if p else []):
                cp.wait()
            for cp in phases[p]:
                cp.start()

        for p in range(len(phases)):
            pl.when(step == p * n_steps // len(phases))(functools.partial(start_phase, p))
        body(*ins, *outs, *scratch)

        @pl.when(step == n_steps - 1)
        def _():
            for cp in phases[-1]:
                cp.wait()

    results = pl.pallas_call(
        carrier, name=name, grid=grid, in_specs=list(in_specs) + [ANY] * r_in, out_specs=list(out_specs) + [ANY] * r_out,
        out_shape=list(out_shape) + rider.out_shapes, scratch_shapes=list(scratch_shapes) + rider.sems,
        input_output_aliases=aliases, compiler_params=_params(*["arbitrary"] * len(grid)))(*args, *rider.ins)
    return results[:n_out], results[n_out:]


def _gather_weights(shards):
    def plan(ins, outs, mk):
        x, y, c = lax.axis_index("x"), lax.axis_index("y"), lax.axis_index("c")
        me, sibling = 2 * x + y, (x, y, 1 - c)
        x_nb, y_nb, diag = (1 - x, y), (x, 1 - y), (1 - x, 1 - y)
        index = lambda chip: 2 * chip[0] + chip[1]
        first, second, third = [], [], []
        for i, shard in enumerate(shards):
            rows = shard.shape[0]
            mine = _half(rows, c)
            quarter = lambda which: pl.ds(c * (rows // 2) + which * (rows // 4), rows // 4)
            first.append(mk.remote(ins[i], outs[i].at[me], sibling))
            for nb in (x_nb, y_nb):
                first.append(mk.remote(ins[i].at[mine], outs[i].at[me, mine], (*nb, c)))
            for nb in (x_nb, y_nb):
                landed = outs[i].at[index(nb), mine]
                second.append(mk.remote(landed, landed, sibling))
            for nb, other, which in ((x_nb, y_nb, 0), (y_nb, x_nb, 1)):
                landed = outs[i].at[index(nb), quarter(which)]
                second.append(mk.remote(landed, landed, (*other, c)))
            landed = outs[i].at[index(diag), mine]
            third.append(mk.remote(landed, landed, sibling))
        return [first, second, third]

    return _Exchange(plan, shards, [jax.ShapeDtypeStruct((N_CHIPS,) + s.shape, s.dtype) for s in shards], 8 * len(shards))


def _pair_exchange(grads):
    def plan(ins, outs, mk):
        x, y, c = lax.axis_index("x"), lax.axis_index("y"), lax.axis_index("c")
        return [[mk.remote(ins[i].at[:, _half(g.shape[1], 1 - c)], outs[i], (x, y, 1 - c)) for i, g in enumerate(grads)]]

    return _Exchange(plan, grads, [jax.ShapeDtypeStruct((N_CHIPS, g.shape[1] // 2, g.shape[2]), g.dtype) for g in grads], len(grads))


def _reduce_first(pair):
    def plan(ins, outs, mk):
        x, y, c = lax.axis_index("x"), lax.axis_index("y"), lax.axis_index("c")
        phase = []
        for i, p in enumerate(pair):
            q = p.shape[1] // 2
            phase.append(mk.remote(ins[i].at[pl.ds(2 * (1 - x), 2), pl.ds(0, q)], outs[i].at[0], (1 - x, y, c)))
            for jx in range(2):
                phase.append(mk.remote(ins[i].at[2 * jx + 1 - y, pl.ds(q, q)], outs[i].at[1, jx], (x, 1 - y, c)))
        return [phase]

    return _Exchange(plan, pair, [jax.ShapeDtypeStruct((2, 2, p.shape[1] // 2, p.shape[2]), p.dtype) for p in pair], 3 * len(pair))


def _reduce_second(sums):
    def plan(ins, outs, mk):
        x, y, c = lax.axis_index("x"), lax.axis_index("y"), lax.axis_index("c")
        phase = []
        for i in range(len(sums)):
            phase.append(mk.remote(ins[i].at[0, 1 - y], outs[i].at[0], (x, 1 - y, c)))
            phase.append(mk.remote(ins[i].at[1, 1 - x], outs[i].at[1], (1 - x, y, c)))
        return [phase]

    return _Exchange(plan, sums, [jax.ShapeDtypeStruct((2,) + s.shape[2:], s.dtype) for s in sums], 2 * len(sums))


def _pair_swap(halves):
    def plan(ins, outs, mk):
        x, y, c = lax.axis_index("x"), lax.axis_index("y"), lax.axis_index("c")
        return [[mk.remote(ins[i], outs[i], (x, y, 1 - c)) for i in range(len(halves))]]

    return _Exchange(plan, halves, [jax.ShapeDtypeStruct(h.shape, h.dtype) for h in halves], len(halves))


def _allreduce_small(packed):
    rows = packed.shape[0]
    n_dev = 2 * N_CHIPS

    def body(x_ref, o_ref, buf, send_sems, recv_sems):
        x, y, c, chips = _place()
        sibling = (x, y, 1 - c)

        def slot(px, py, pc):
            return buf.at[4 * px + 2 * py + pc]

        buf[4 * x + 2 * y + c] = x_ref[...]
        first = [_remote(x_ref, slot(x, y, c), send_sems, recv_sems, 0, sibling)]
        first += [_remote(x_ref, slot(x, y, c), send_sems, recv_sems, 1 + k, (*chip, c)) for k, chip in enumerate(chips)]
        for cp in first:
            cp.start()
        passed = []
        for k, chip in enumerate(chips):
            landed = slot(*chip, c)
            _remote(landed, landed, send_sems, recv_sems, 1 + k, (*chip, c)).wait_recv()
            passed.append(_remote(landed, landed, send_sems, recv_sems, 1 + N_REL + k, sibling))
            passed[-1].start()
        _remote(slot(*sibling), slot(*sibling), send_sems, recv_sems, 0, sibling).wait_recv()
        for k, chip in enumerate(chips):
            landed = slot(*chip, 1 - c)
            _remote(landed, landed, send_sems, recv_sems, 1 + N_REL + k, sibling).wait_recv()
        for cp in first + passed:
            cp.wait_send()
        total = buf[0]
        for d in range(1, n_dev):
            total = total + buf[d]
        o_ref[...] = total

    vm = pl.BlockSpec(memory_space=pltpu.VMEM)
    return pl.pallas_call(
        body, name="allreduce_small", in_specs=[vm], out_specs=vm, out_shape=jax.ShapeDtypeStruct(packed.shape, F32),
        scratch_shapes=[pltpu.VMEM((n_dev, rows, 128), F32), pltpu.SemaphoreType.DMA((1 + 2 * N_REL,)),
                        pltpu.SemaphoreType.DMA((1 + 2 * N_REL,))],
    )(packed)


def _row_tile(rows, cap=256):
    return max(t for t in range(8, cap + 1, 8) if rows % t == 0)


def _pair_sum(grad, got, c, name):
    _, half, cols = got.shape
    tr = _row_tile(half)
    nt = half // tr

    def body(c_ref, g_ref, r_ref, o_ref):
        o_ref[...] = g_ref[...] + r_ref[...]

    blk = (1, tr, cols)
    return pl.pallas_call(
        body, name=name,
        grid_spec=pltpu.PrefetchScalarGridSpec(
            num_scalar_prefetch=1, grid=(N_CHIPS, nt),
            in_specs=[pl.BlockSpec(blk, lambda j, t, c_ref: (j, c_ref[0] * nt + t, 0)),
                      pl.BlockSpec(blk, lambda j, t, c_ref: (j, t, 0))],
            out_specs=pl.BlockSpec(blk, lambda j, t, c_ref: (j, t, 0))),
        out_shape=jax.ShapeDtypeStruct(got.shape, F32), compiler_params=_params("parallel", "parallel"),
    )(c, grad, got)


def _reduce_sum_first(pair, got, sel, name):
    _, _, q, cols = got.shape
    tr = _row_tile(q)
    nt = q // tr

    def body(sel_ref, p_ref, r_ref, o_ref):
        o_ref[0, 0] = p_ref[0] + r_ref[0, 0]

    blk = (1, 1, tr, cols)
    return pl.pallas_call(
        body, name=name,
        grid_spec=pltpu.PrefetchScalarGridSpec(
            num_scalar_prefetch=1, grid=(2, 2, nt),
            in_specs=[pl.BlockSpec((1, tr, cols), lambda p, k, t, s: (s[2 * p] + s[2 * p + 1] * k, p * nt + t, 0)),
                      pl.BlockSpec(blk, lambda p, k, t, s: (p, k, t, 0))],
            out_specs=pl.BlockSpec(blk, lambda p, k, t, s: (p, k, t, 0))),
        out_shape=jax.ShapeDtypeStruct(got.shape, F32), compiler_params=_params("parallel", "parallel", "parallel"),
    )(sel, pair, got)


def _reduce_sum_second(sums, got, sel, name):
    _, q, cols = got.shape
    tr = _row_tile(q)

    def body(sel_ref, s_ref, r_ref, o_ref):
        o_ref[0] = s_ref[0, 0] + r_ref[0]

    blk = (1, tr, cols)
    return pl.pallas_call(
        body, name=name,
        grid_spec=pltpu.PrefetchScalarGridSpec(
            num_scalar_prefetch=1, grid=(2, q // tr),
            in_specs=[pl.BlockSpec((1, 1, tr, cols), lambda p, t, s: (p, s[p], t, 0)),
                      pl.BlockSpec(blk, lambda p, t, s: (p, t, 0))],
            out_specs=pl.BlockSpec(blk, lambda p, t, s: (p, t, 0))),
        out_shape=jax.ShapeDtypeStruct(got.shape, F32), compiler_params=_params("parallel", "parallel"),
    )(sel, sums, got)


def _adamw_update(w, g, m, v):
    m2 = ADAM_B1 * m + (1.0 - ADAM_B1) * g
    v2 = ADAM_B2 * v + (1.0 - ADAM_B2) * (g * g)
    m_hat = m2 / (1.0 - ADAM_B1 ** ADAM_STEP)
    v_hat = v2 / (1.0 - ADAM_B2 ** ADAM_STEP)
    return -ADAM_LR * (m_hat / (jnp.sqrt(v_hat) + ADAM_EPS) + ADAM_WD * w), m2, v2


def _adamw(w, g, m, v, name):
    rows, cols = w.shape
    tr = _row_tile(rows)

    def body(w_ref, g_ref, m_ref, v_ref, d_ref, mo_ref, vo_ref):
        d_ref[...], mo_ref[...], vo_ref[...] = _adamw_update(w_ref[...], g_ref[...], m_ref[...], v_ref[...])

    blk = _rows(tr, cols)
    return pl.pallas_call(
        body, name=name, grid=(rows // tr,), in_specs=[blk] * 4, out_specs=[blk] * 3,
        out_shape=[jax.ShapeDtypeStruct(w.shape, F32)] * 3, compiler_params=_params("parallel"),
    )(w, g, m, v)


def _adamw_halves(w, mine, theirs, m, v, core, name):
    rows, cols = w.shape
    tr = _row_tile(rows // 2)
    per_half = rows // 2 // tr

    def body(core_ref, w_ref, a_ref, b_ref, m_ref, v_ref, g_ref, d_ref, mo_ref, vo_ref):
        g = jnp.where(pl.program_id(0) // per_half == core_ref[0], a_ref[...], b_ref[...])
        g_ref[...] = g
        d_ref[...], mo_ref[...], vo_ref[...] = _adamw_update(w_ref[...], g, m_ref[...], v_ref[...])

    blk = pl.BlockSpec((tr, cols), lambda t, c: (t, 0))
    half = lambda own: pl.BlockSpec(
        (tr, cols), lambda t, c: (jnp.clip(t - (c[0] if own else 1 - c[0]) * per_half, 0, per_half - 1), 0))
    return pl.pallas_call(
        body, name=name,
        grid_spec=pltpu.PrefetchScalarGridSpec(
            num_scalar_prefetch=1, grid=(2 * per_half,), in_specs=[blk, half(True), half(False), blk, blk], out_specs=[blk] * 4),
        out_shape=[jax.ShapeDtypeStruct(w.shape, F32)] * 4, compiler_params=_params("arbitrary"),
    )(core, w, mine, theirs, m, v)


def _pack(parts):
    flat = []
    for t in parts:
        t = t.reshape(-1).astype(F32)
        flat.append(jnp.pad(t, (0, -t.shape[0] % 128)))
    flat = jnp.concatenate(flat)
    return jnp.pad(flat, (0, -flat.shape[0] % 1024)).reshape(-1, 128)


def _unpack(buf, shapes):
    flat, out, at = buf.reshape(-1), [], 0
    for s in shapes:
        size = math.prod(s)
        out.append(flat[at:at + size].reshape(s))
        at += size + (-size % 128)
    return out


def kernel(x, ffn1_pre_g, ffn1_w_in, ffn1_w_out, ffn1_post_g, mix_pre_g, w_mix_in, a_re, a_im, log_dt, b_re, b_im, c_re, c_im, d_skip, w_glu, b_glu, w_mix_out, mix_post_g, ffn2_pre_g, ffn2_w_in, ffn2_w_out, ffn2_post_g, loss_target, m_ffn1_pre_g, m_ffn1_w_in, m_ffn1_w_out, m_ffn1_post_g, m_mix_pre_g, m_w_mix_in, m_a_re, m_a_im, m_log_dt, m_b_re, m_b_im, m_c_re, m_c_im, m_d_skip, m_w_glu, m_b_glu, m_w_mix_out, m_mix_post_g, m_ffn2_pre_g, m_ffn2_w_in, m_ffn2_w_out, m_ffn2_post_g, v_ffn1_pre_g, v_ffn1_w_in, v_ffn1_w_out, v_ffn1_post_g, v_mix_pre_g, v_w_mix_in, v_a_re, v_a_im, v_log_dt, v_b_re, v_b_im, v_c_re, v_c_im, v_d_skip, v_w_glu, v_b_glu, v_w_mix_out, v_mix_post_g, v_ffn2_pre_g, v_ffn2_w_in, v_ffn2_w_out, v_ffn2_post_g):
    given = dict(locals())
    order = ("ffn1_pre_g", "ffn1_w_in", "ffn1_w_out", "ffn1_post_g", "mix_pre_g", "w_mix_in", "a_re", "a_im", "log_dt",
             "b_re", "b_im", "c_re", "c_im", "d_skip", "w_glu", "b_glu", "w_mix_out", "mix_post_g", "ffn2_pre_g",
             "ffn2_w_in", "ffn2_w_out", "ffn2_post_g")
    at_x, at_y, at_c = (lax.axis_index(a).astype(jnp.int32) for a in ("x", "y", "c"))
    place = dict(core=at_c.reshape(1), sel_first=jnp.stack([2 * at_x, jnp.int32(1), at_y, jnp.int32(2)]),
                 sel_second=jnp.stack([at_y, at_x]))

    shards = {n: given[n][0] for n in BIG}
    w = {n: shards[n].astype(BF16) for n in REST}
    w.update(zip(FIRST, _gather_weights([shards[n].astype(BF16) for n in FIRST]).run("gather_first")))
    small = {n: given[n][0] for n in SMALL}
    loss_rows, grad_x, g = _local_step(x[0], loss_target[0], small, w, place)

    total = _allreduce_small(_pack([g[n] for n in SMALL] + [loss_rows[0, :1]]))
    parts = _unpack(total, [small[n].shape for n in SMALL] + [(1,)])
    grads = dict(zip(SMALL, parts[:-1]))
    loss = parts[-1][0]

    delta, new_m, new_v = {}, {}, {}
    for n in BIG:
        grads[n], delta[n], new_m[n], new_v[n] = _adamw_halves(
            shards[n], *g[n], given["m_" + n][0], given["v_" + n][0], place["core"], name=f"adamw_{n}")
    packed = [_pack([given[pre + n] for n in SMALL]) for pre in ("", "m_", "v_")]
    outs = _adamw(packed[0], _pack([grads[n] for n in SMALL]), packed[1], packed[2], name="adamw_small")
    for store, buf in zip((delta, new_m, new_v), outs):
        store.update(zip(SMALL, _unpack(buf, [small[n].shape for n in SMALL])))

    lead = lambda d: [d[n][None] for n in order]
    return (loss, grad_x[None], *lead(grads), *lead(delta), *lead(new_m), *lead(new_v))
```

```python
import functools
import math

import jax
import jax.numpy as jnp
from jax import lax
from jax.experimental import pallas as pl
from jax.experimental.pallas import tpu as pltpu

F32, BF16 = jnp.float32, jnp.bfloat16

D_MODEL = 1024
D_FF = 2816
N_CHIPS = 4
FF_BLK = 2 * D_FF // N_CHIPS
ATTN_W = 512
SSM_W = 512
HEAD_DIM = 64
N_HEADS = ATTN_W // HEAD_DIM
DILATIONS = (1, 4, 16)
N_BACK = 128
QBLK = 128
N_GROUPS = 32
GROUP_CH = 16
STATE = 64
N_STATE = N_GROUPS * STATE
EPS = 1e-6
NEG = -1e30
GELU_C = math.sqrt(2.0 / math.pi)

ADAM_LR, ADAM_B1, ADAM_B2, ADAM_EPS, ADAM_WD, ADAM_STEP = 0.001, 0.9, 0.999, 1e-08, 0.01, 10

VMEM_LIMIT_V7X = 60 * 1024 * 1024
ROW_TILE = 512
FFN_ROW_TILE = 512
DW_ROW_TILE = 1024


def _params(*sem):
    return pltpu.CompilerParams(dimension_semantics=sem, vmem_limit_bytes=VMEM_LIMIT_V7X)


def _dot(a, b):
    return jnp.dot(a.astype(BF16), b.astype(BF16), preferred_element_type=F32)


def _dot_nt(a, b):
    return lax.dot_general(a.astype(BF16), b.astype(BF16), (((1,), (1,)), ((), ())), preferred_element_type=F32)


def _dot_tn(a, b):
    return lax.dot_general(a.astype(BF16), b.astype(BF16), (((0,), (0,)), ((), ())), preferred_element_type=F32)


def _full(shape):
    return pl.BlockSpec(shape, lambda *_: (0,) * len(shape))


def _rows(tm, width):
    return pl.BlockSpec((tm, width), lambda i: (i, 0))


ANY = pl.BlockSpec(memory_space=pl.ANY)


def _load_once(pairs, sems):
    copies = [pltpu.make_async_copy(src, dst, sems.at[k]) for k, (src, dst) in enumerate(pairs)]
    for c in copies:
        c.start()
    for c in copies:
        c.wait()


def _rms(x):
    return lax.rsqrt(jnp.mean(x * x, axis=-1, keepdims=True) + EPS)


def _rms_bwd(dy_g, xn, r):
    return r * (dy_g - xn * jnp.mean(dy_g * xn, axis=-1, keepdims=True))


def _ffn_fwd(x, g_pre, w_in, w_out, g_post, target, *, name, rider=None):
    T = x.shape[0]
    tm = FFN_ROW_TILE
    with_loss = target is not None

    def body(*refs):
        if with_loss:
            x_ref, gpre_ref, gpost_ref, tgt_ref, win_hbm, wout_hbm, o_ref, loss_ref, z_ref, f_ref, win_v, wout_v, sems = refs
        else:
            x_ref, gpre_ref, gpost_ref, win_hbm, wout_hbm, o_ref, z_ref, f_ref, win_v, wout_v, sems = refs

        @pl.when(pl.program_id(0) == 0)
        def _():
            _load_once([(win_hbm, win_v), (wout_hbm, wout_v)], sems)
            if with_loss:
                loss_ref[...] = jnp.zeros_like(loss_ref)

        xv = x_ref[...]
        h = (xv * _rms(xv) * gpre_ref[...]).astype(BF16)
        f = jnp.zeros((tm, D_MODEL), F32)
        for k in range(2):
            gate = _dot(h, win_v[k])
            up = _dot(h, win_v[k + 2])
            z_ref[:, k * FF_BLK:(k + 1) * FF_BLK] = gate.astype(BF16)
            z_ref[:, D_FF + k * FF_BLK:D_FF + (k + 1) * FF_BLK] = up.astype(BF16)
            f = f + _dot(gate * jax.nn.sigmoid(gate) * up, wout_v[k])
        f_ref[...] = f
        out = xv + 0.5 * (f * _rms(f) * gpost_ref[...])
        if with_loss:
            err = out - tgt_ref[...]
            o_ref[...] = err * (1.0 / D_MODEL)
            loss_ref[...] += jnp.sum(err * err) * (0.5 / D_MODEL)
        else:
            o_ref[...] = out

    row = _rows(tm, D_MODEL)
    vec = _full((1, D_MODEL))
    in_specs = [row, vec, vec] + ([row] if with_loss else []) + [ANY, ANY]
    out_shape = [jax.ShapeDtypeStruct((T, D_MODEL), F32)]
    out_specs = [row]
    if with_loss:
        out_shape.append(jax.ShapeDtypeStruct((8, 128), F32))
        out_specs.append(_full((8, 128)))
    out_shape += [jax.ShapeDtypeStruct((T, 2 * D_FF), BF16), jax.ShapeDtypeStruct((T, D_MODEL), F32)]
    out_specs += [_rows(tm, 2 * D_FF), row]
    args = (x, g_pre, g_post) + ((target,) if with_loss else ()) + (w_in, w_out)
    return _pallas(
        body, name=name, grid=(T // tm,), in_specs=in_specs, out_specs=out_specs, out_shape=out_shape,
        scratch_shapes=[pltpu.VMEM(w_in.shape, BF16), pltpu.VMEM(w_out.shape, BF16), pltpu.SemaphoreType.DMA((2,))],
        semantics=("arbitrary",), args=args, rider=rider)


def _ffn_bwd_out(dout, f, z, w_out, g_post, *, name):
    T = dout.shape[0]
    tm = FFN_ROW_TILE
    nt = T // tm

    def body(dout_ref, f_ref, z_ref, gpost_ref, wout_hbm, dz_ref, dgpost_ref, dwout_hbm, wout_v, dwout_v, sems):
        i = pl.program_id(0)

        @pl.when(i == 0)
        def _():
            _load_once([(wout_hbm, wout_v)], sems)
            dwout_v[...] = jnp.zeros_like(dwout_v)
            dgpost_ref[...] = jnp.zeros_like(dgpost_ref)

        dy = 0.5 * dout_ref[...]
        f = f_ref[...]
        r = _rms(f)
        fn = f * r
        dgpost_ref[...] += jnp.sum(dy * fn, axis=0, keepdims=True)
        df = _rms_bwd(dy * gpost_ref[...], fn, r).astype(BF16)
        for k in range(2):
            gate = z_ref[:, k * FF_BLK:(k + 1) * FF_BLK].astype(F32)
            up = z_ref[:, D_FF + k * FF_BLK:D_FF + (k + 1) * FF_BLK].astype(F32)
            sg = jax.nn.sigmoid(gate)
            silu = gate * sg
            dwout_v[k] += _dot_tn(silu * up, df)
            da = _dot_nt(df, wout_v[k])
            dz_ref[:, k * FF_BLK:(k + 1) * FF_BLK] = (da * up * (sg * (1.0 + gate * (1.0 - sg)))).astype(BF16)
            dz_ref[:, D_FF + k * FF_BLK:D_FF + (k + 1) * FF_BLK] = (da * silu).astype(BF16)

        @pl.when(i == nt - 1)
        def _():
            c = pltpu.make_async_copy(dwout_v, dwout_hbm, sems.at[0])
            c.start()
            c.wait()

    row = _rows(tm, D_MODEL)
    return pl.pallas_call(
        body, name=name, grid=(nt,),
        in_specs=[row, row, _rows(tm, 2 * D_FF), _full((1, D_MODEL)), ANY],
        out_specs=[_rows(tm, 2 * D_FF), _full((1, D_MODEL)), ANY],
        out_shape=[jax.ShapeDtypeStruct((T, 2 * D_FF), BF16), jax.ShapeDtypeStruct((1, D_MODEL), F32),
                   jax.ShapeDtypeStruct(w_out.shape, F32)],
        scratch_shapes=[pltpu.VMEM(w_out.shape, BF16), pltpu.VMEM(w_out.shape, F32), pltpu.SemaphoreType.DMA((1,))],
        compiler_params=_params("arbitrary"),
    )(dout, f, z, g_post, w_out)


def _norm_matmul_dw(dz, x, g, n_blocks, *, name, rider=None):
    T = x.shape[0]
    bw = dz.shape[1] // n_blocks
    tm = DW_ROW_TILE

    def body(dz_ref, x_ref, g_ref, dw_ref):
        @pl.when(pl.program_id(1) == 0)
        def _():
            dw_ref[...] = jnp.zeros_like(dw_ref)

        xv = x_ref[...]
        dw_ref[0] += _dot_tn(xv * _rms(xv) * g_ref[...], dz_ref[...])

    res = _pallas(
        body, name=name, grid=(n_blocks, T // tm),
        in_specs=[pl.BlockSpec((tm, bw), lambda j, t: (t, j)), pl.BlockSpec((tm, D_MODEL), lambda j, t: (t, 0)), _full((1, D_MODEL))],
        out_specs=[pl.BlockSpec((1, D_MODEL, bw), lambda j, t: (j, 0, 0))],
        out_shape=[jax.ShapeDtypeStruct((n_blocks, D_MODEL, bw), F32)], semantics=("parallel", "arbitrary"),
        args=(dz, x, g), rider=rider)
    return res[0] if rider is None else (res[0][0], res[1])


def _norm_matmul_dx(dz, x, dres, g, w, *, name, rider=None, part=(0, 1), into=None):
    T = x.shape[0]
    nb, _, bw = w.shape
    tm = FFN_ROW_TILE
    steps = T // part[1] // tm
    first = part[0] * steps

    def body(dz_ref, x_ref, dres_ref, g_ref, w_hbm, *rest):
        dx_ref, dg_ref, w_v, sems = rest[-4:]

        @pl.when(pl.program_id(0) == 0)
        def _():
            _load_once([(w_hbm, w_v)], sems)
            dg_ref[...] = jnp.zeros_like(dg_ref)

        xv = x_ref[...]
        r = _rms(xv)
        xn = xv * r
        gv = g_ref[...]
        dh = jnp.zeros((tm, D_MODEL), F32)
        for j in range(nb):
            dh = dh + _dot_nt(dz_ref[:, j * bw:(j + 1) * bw], w_v[j])
        dg_ref[...] += jnp.sum(dh * xn, axis=0, keepdims=True)
        dx_ref[...] = _rms_bwd(dh * gv, xn, r) + dres_ref[...]

    rows = lambda width: pl.BlockSpec((tm, width), lambda i: (first + i, 0))
    row = rows(D_MODEL)
    return _pallas(
        body, name=name, grid=(steps,),
        in_specs=[rows(nb * bw), row, row, _full((1, D_MODEL)), ANY] + ([] if into is None else [ANY]),
        out_specs=[row, _full((1, D_MODEL))],
        out_shape=[jax.ShapeDtypeStruct((T, D_MODEL), F32), jax.ShapeDtypeStruct((1, D_MODEL), F32)],
        scratch_shapes=[pltpu.VMEM(w.shape, BF16), pltpu.SemaphoreType.DMA((1,))],
        semantics=("arbitrary",), args=(dz, x, dres, g, w) + (() if into is None else (into,)), rider=rider,
        aliases={} if into is None else {5: 0})


def _mix_in_fwd(x, g, w):
    T = x.shape[0]
    tm = ROW_TILE

    def body(x_ref, g_ref, w_ref, q_ref, k_ref, v_ref, u_ref):
        xv = x_ref[...]
        h = (xv * _rms(xv) * g_ref[...]).astype(BF16)
        for j, o_ref in enumerate((q_ref, k_ref, v_ref, u_ref)):
            o_ref[...] = _dot(h, w_ref[j])

    col = _rows(tm, ATTN_W)
    return pl.pallas_call(
        body, name="mix_in_fwd", grid=(T // tm,),
        in_specs=[_rows(tm, D_MODEL), _full((1, D_MODEL)), _full(w.shape)],
        out_specs=[col] * 4, out_shape=[jax.ShapeDtypeStruct((T, ATTN_W), F32)] * 4,
        compiler_params=_params("parallel"),
    )(x, g, w)


def _mix_out_fwd(x, attn, ssm, w, g):
    T = x.shape[0]
    tm = ROW_TILE

    def body(x_ref, a_ref, s_ref, w_ref, g_ref, o_ref, m_ref):
        mixed = _dot(a_ref[...], w_ref[0]) + _dot(s_ref[...], w_ref[1])
        m_ref[...] = mixed
        o_ref[...] = x_ref[...] + mixed * _rms(mixed) * g_ref[...]

    row, col = _rows(tm, D_MODEL), _rows(tm, ATTN_W)
    return pl.pallas_call(
        body, name="mix_out_fwd", grid=(T // tm,),
        in_specs=[row, col, col, _full(w.shape), _full((1, D_MODEL))],
        out_specs=[row, row], out_shape=[jax.ShapeDtypeStruct((T, D_MODEL), F32)] * 2,
        compiler_params=_params("parallel"),
    )(x, attn, ssm, w, g)


def _mix_out_bwd(dout, mixed, attn, ssm, w, g, rider=None):
    T = dout.shape[0]
    tm = ROW_TILE

    def body(dout_ref, m_ref, a_ref, s_ref, w_ref, g_ref, da_ref, ds_ref, dw_ref, dg_ref):
        @pl.when(pl.program_id(0) == 0)
        def _():
            dw_ref[...] = jnp.zeros_like(dw_ref)
            dg_ref[...] = jnp.zeros_like(dg_ref)

        dy = dout_ref[...]
        mixed = m_ref[...]
        r = _rms(mixed)
        mn = mixed * r
        dg_ref[...] += jnp.sum(dy * mn, axis=0, keepdims=True)
        dm = _rms_bwd(dy * g_ref[...], mn, r).astype(BF16)
        da_ref[...] = _dot_nt(dm, w_ref[0])
        ds_ref[...] = _dot_nt(dm, w_ref[1])
        dw_ref[0] += _dot_tn(a_ref[...], dm)
        dw_ref[1] += _dot_tn(s_ref[...], dm)

    row, col = _rows(tm, D_MODEL), _rows(tm, ATTN_W)
    return _pallas(
        body, name="mix_out_bwd", grid=(T // tm,),
        in_specs=[row, row, col, col, _full(w.shape), _full((1, D_MODEL))],
        out_specs=[col, col, _full(w.shape), _full((1, D_MODEL))],
        out_shape=[jax.ShapeDtypeStruct((T, ATTN_W), F32)] * 2
        + [jax.ShapeDtypeStruct(w.shape, F32), jax.ShapeDtypeStruct((1, D_MODEL), F32)],
        semantics=("arbitrary",), args=(dout, mixed, attn, ssm, w, g), rider=rider)


ATTN_TILING = {1: (ATTN_W, 1), 4: (2 * HEAD_DIM, 4), 16: (2 * HEAD_DIM, 4)}


def _class_rows(d, r):
    return (pl.ds(r, QBLK, stride=d), slice(None)) if d > 1 else (slice(None), slice(None))


def _for_class_groups(d, group, fn):
    if d == group:
        fn(0)
    else:
        lax.fori_loop(0, d // group, lambda n, carry: (fn(n * group), carry)[1], 0)


def _block_slopes(lanes, lane_block):
    heads = lanes // HEAD_DIM
    first = lane_block * heads
    return [jnp.exp2(-jnp.full((1, 1), first + hh + 1, jnp.int32).astype(F32)) for hh in range(heads)]


def _attn_specs(d, nb, lanes):
    blk = (QBLK * d, lanes)
    cur = pl.BlockSpec(blk, lambda j, lb: (j, lb))
    prev = pl.BlockSpec(blk, lambda j, lb: (jnp.maximum(j - 1, 0), lb))
    nxt = pl.BlockSpec(blk, lambda j, lb: (jnp.minimum(j + 1, nb - 1), lb))
    return cur, prev, nxt


def _attn_branch_fwd(q, k, v, d):
    T = q.shape[0]
    nb = T // (d * QBLK)
    lanes, group = ATTN_TILING[d]
    scale = HEAD_DIM ** -0.5

    def body(q_ref, kc_ref, kp_ref, vc_ref, vp_ref, o_ref, l_ref, q_s, kk_s, vv_s, o_s, l_s):
        j = pl.program_id(0)
        qi = lax.broadcasted_iota(jnp.int32, (QBLK, 2 * QBLK), 0)
        ci = lax.broadcasted_iota(jnp.int32, (QBLK, 2 * QBLK), 1)
        steps = QBLK + qi - ci
        valid = (steps >= 0) & (steps <= N_BACK) & ((ci >= QBLK) | (j > 0))
        dist = (steps * d).astype(F32)
        slopes = _block_slopes(lanes, pl.program_id(1))

        def classes(first):
            for n in range(group):
                rows = _class_rows(d, first + n)
                q_s[n] = q_ref[rows]
                kk_s[n, :QBLK], kk_s[n, QBLK:] = kp_ref[rows], kc_ref[rows]
                vv_s[n, :QBLK], vv_s[n, QBLK:] = vp_ref[rows], vc_ref[rows]
            for n in range(group):
                for hh in range(lanes // HEAD_DIM):
                    sl = slice(hh * HEAD_DIM, (hh + 1) * HEAD_DIM)
                    s = _dot_nt(q_s[n, :, sl], kk_s[n, :, sl]) * scale - slopes[hh] * dist
                    s = jnp.where(valid, s, NEG)
                    m = jnp.max(s, axis=-1, keepdims=True)
                    p = jnp.exp(s - m)
                    den = jnp.sum(p, axis=-1, keepdims=True)
                    o_s[n, :, sl] = _dot(p, vv_s[n, :, sl]) / den
                    l_s[n, :, sl] = jnp.broadcast_to(m + jnp.log(den), (QBLK, HEAD_DIM))
            for n in range(group):
                rows = _class_rows(d, first + n)
                o_ref[rows] = o_s[n]
                l_ref[rows] = l_s[n]

        _for_class_groups(d, group, classes)

    cur, prev, _ = _attn_specs(d, nb, lanes)
    shape = jax.ShapeDtypeStruct((T, ATTN_W), F32)
    one, two = pltpu.VMEM((group, QBLK, lanes), F32), pltpu.VMEM((group, 2 * QBLK, lanes), F32)
    return pl.pallas_call(
        body, name=f"attn_fwd_d{d}", grid=(nb, ATTN_W // lanes),
        in_specs=[cur, cur, prev, cur, prev], out_specs=[cur, cur], out_shape=[shape, shape],
        scratch_shapes=[one, two, two, one, one], compiler_params=_params("parallel", "parallel"),
    )(q, k, k, v, v)


def _attn_merge(outs, lses):
    T = outs[0].shape[0]
    tm = 512

    def body(o1, o2, o3, l1, l2, l3, a_ref, lse_ref):
        ls = [l1[...], l2[...], l3[...]]
        m = jnp.maximum(jnp.maximum(ls[0], ls[1]), ls[2])
        lse = m + jnp.log(jnp.exp(ls[0] - m) + jnp.exp(ls[1] - m) + jnp.exp(ls[2] - m))
        lse_ref[...] = lse
        a_ref[...] = jnp.exp(ls[0] - lse) * o1[...] + jnp.exp(ls[1] - lse) * o2[...] + jnp.exp(ls[2] - lse) * o3[...]

    col = _rows(tm, ATTN_W)
    return pl.pallas_call(
        body, name="attn_merge", grid=(T // tm,), in_specs=[col] * 6, out_specs=[col, col],
        out_shape=[jax.ShapeDtypeStruct((T, ATTN_W), F32)] * 2, compiler_params=_params("parallel"),
    )(*outs, *lses)


def _attn_branch_bwd(q, k, v, o, lse, do, d, rider=None):
    T = q.shape[0]
    nb = T // (d * QBLK)
    lanes, group = ATTN_TILING[d]
    scale = HEAD_DIM ** -0.5

    def body(q_ref, kc_ref, kp_ref, vc_ref, vp_ref, o_ref, l_ref, do_ref, dq_ref, dk_ref, dv_ref,
             q_s, o_s, l_s, do_s, kk_s, vv_s, dq_s, dk_s, dv_s, ck_s, cv_s):
        j = pl.program_id(1)

        @pl.when(j == 0)
        def _():
            ck_s[...] = jnp.zeros_like(ck_s)
            cv_s[...] = jnp.zeros_like(cv_s)

        qi = lax.broadcasted_iota(jnp.int32, (QBLK, 2 * QBLK), 0)
        ci = lax.broadcasted_iota(jnp.int32, (QBLK, 2 * QBLK), 1)
        steps = QBLK + qi - ci
        valid = (steps >= 0) & (steps <= N_BACK) & ((ci >= QBLK) | (j > 0))
        dist = (steps * d).astype(F32)
        lo, hi = slice(0, QBLK), slice(QBLK, 2 * QBLK)
        slopes = _block_slopes(lanes, pl.program_id(0))

        def classes(first):
            for n in range(group):
                rows = _class_rows(d, first + n)
                q_s[n], o_s[n], l_s[n], do_s[n] = q_ref[rows], o_ref[rows], l_ref[rows], do_ref[rows]
                kk_s[n, lo], kk_s[n, hi] = kp_ref[rows], kc_ref[rows]
                vv_s[n, lo], vv_s[n, hi] = vp_ref[rows], vc_ref[rows]
            for n in range(group):
                for hh in range(lanes // HEAD_DIM):
                    sl = slice(hh * HEAD_DIM, (hh + 1) * HEAD_DIM)
                    qh, doh, kk, vv = q_s[n, :, sl], do_s[n, :, sl], kk_s[n, :, sl], vv_s[n, :, sl]
                    delta = jnp.sum(doh * o_s[n, :, sl], axis=-1, keepdims=True)
                    s = jnp.where(valid, _dot_nt(qh, kk) * scale - slopes[hh] * dist, NEG)
                    p = jnp.exp(s - l_s[n, :, hh * HEAD_DIM:hh * HEAD_DIM + 1])
                    ds = p * (_dot_nt(doh, vv) - delta)
                    dq_s[n, :, sl] = _dot(ds, kk) * scale
                    dkk = _dot_tn(ds, qh) * scale
                    dvv = _dot_tn(p, doh)
                    dk_s[n, :, sl] = ck_s[first + n, :, sl] + dkk[lo]
                    dv_s[n, :, sl] = cv_s[first + n, :, sl] + dvv[lo]
                    ck_s[first + n, :, sl] = dkk[hi]
                    cv_s[first + n, :, sl] = dvv[hi]
            for n in range(group):
                rows = _class_rows(d, first + n)
                dq_ref[rows] = dq_s[n]
                dk_ref[rows] = dk_s[n]
                dv_ref[rows] = dv_s[n]

        @pl.when(j < nb)
        def _():
            _for_class_groups(d, group, classes)

        @pl.when(j == nb)
        def _():
            for r in range(d):
                dk_ref[_class_rows(d, r)] = ck_s[r]
                dv_ref[_class_rows(d, r)] = cv_s[r]

    blk = (QBLK * d, lanes)
    here = lambda j: jnp.minimum(j, nb - 1)
    cur = pl.BlockSpec(blk, lambda lb, j: (here(j), lb))
    prev = pl.BlockSpec(blk, lambda lb, j: (jnp.maximum(here(j) - 1, 0), lb))
    behind = pl.BlockSpec(blk, lambda lb, j: (jnp.maximum(j - 1, 0), lb))
    shape = jax.ShapeDtypeStruct((T, ATTN_W), F32)
    one, two = pltpu.VMEM((group, QBLK, lanes), F32), pltpu.VMEM((group, 2 * QBLK, lanes), F32)
    carry = pltpu.VMEM((d, QBLK, lanes), F32)
    return _pallas(
        body, name=f"attn_bwd_d{d}", grid=(ATTN_W // lanes, nb + 1),
        in_specs=[cur, cur, prev, cur, prev, cur, cur, cur], out_specs=[cur, behind, behind], out_shape=[shape] * 3,
        scratch_shapes=[one] * 4 + [two] * 2 + [one] * 3 + [carry] * 2,
        semantics=("parallel", "arbitrary"), args=(q, k, k, v, v, o, lse, do), rider=rider)


def _dproj_merge(dqs, dks, dvs, du):
    T = du.shape[0]
    tm = 512

    def body(*refs):
        o_ref = refs[-1]
        for part in range(3):
            a, b, c = refs[3 * part:3 * part + 3]
            o_ref[:, part * ATTN_W:(part + 1) * ATTN_W] = (a[...] + b[...] + c[...]).astype(BF16)
        o_ref[:, 3 * ATTN_W:] = refs[9][...].astype(BF16)

    col = _rows(tm, ATTN_W)
    return pl.pallas_call(
        body, name="dproj_merge", grid=(T // tm,), in_specs=[col] * 10, out_specs=_rows(tm, 4 * ATTN_W),
        out_shape=jax.ShapeDtypeStruct((T, 4 * ATTN_W), BF16), compiler_params=_params("parallel"),
    )(*dqs, *dks, *dvs, du)


def _attention_fwd(q, k, v):
    res = [_attn_branch_fwd(q, k, v, d) for d in DILATIONS]
    return _attn_merge([r[0] for r in res], [r[1] for r in res])


SCAN_ROWS = 8
SCAN_LANES = 512
SSM_CHUNK = 512
SSM_CHUNK_BWD = 256
SSM_HALVES = tuple((slice(h * SSM_W // 2, (h + 1) * SSM_W // 2), slice(h * N_STATE // 2, (h + 1) * N_STATE // 2)) for h in range(2))


def _cmul(ar, ai, br, bi):
    return ar * br - ai * bi, ar * bi + ai * br


def _ssm_discretize(a_re, a_im, log_dt, b_re, b_im):
    def body(ar_ref, ai_ref, ldt_ref, br_ref, bi_ref, abr_ref, abi_ref, er_ref, ei_ref, bbr_ref, bbi_ref, pr_ref, pi_ref):
        ar, ai = ar_ref[...], ai_ref[...]
        dt = jnp.exp(ldt_ref[...])
        n = lax.broadcasted_iota(jnp.int32, (1, SCAN_ROWS), 1).astype(F32) + 1.0
        mag, ang = jnp.exp(dt * ar), dt * ai
        abr, abi = mag * jnp.cos(ang), mag * jnp.sin(ang)
        abr_ref[...], abi_ref[...] = abr, abi
        pr_ref[...] = jnp.exp(dt * ar * n) * jnp.cos(ang * n)
        pi_ref[...] = jnp.exp(dt * ar * n) * jnp.sin(ang * n)
        den = ar * ar + ai * ai
        er = ((abr - 1.0) * ar + abi * ai) / den
        ei = (abi * ar - (abr - 1.0) * ai) / den
        er_ref[...], ei_ref[...] = er, ei
        bbr_ref[...], bbi_ref[...] = _cmul(er, ei, br_ref[...], bi_ref[...])

    col = jax.ShapeDtypeStruct((N_STATE, 1), F32)
    mat = jax.ShapeDtypeStruct((N_STATE, GROUP_CH), F32)
    pw = jax.ShapeDtypeStruct((N_STATE, SCAN_ROWS), F32)
    return pl.pallas_call(body, name="ssm_discretize", out_shape=[col] * 4 + [mat] * 2 + [pw] * 2)(
        a_re, a_im, log_dt, b_re, b_im)


def _ssm_discretize_bwd(a_re, a_im, log_dt, b_re, b_im, ab_re, ab_im, e_re, e_im, dab_re, dab_im, dbb_re, dbb_im):
    def body(ar_ref, ai_ref, ldt_ref, br_ref, bi_ref, abr_ref, abi_ref, er_ref, ei_ref, dabr_ref, dabi_ref,
             dbbr_ref, dbbi_ref, dar_ref, dai_ref, ddt_ref, dbr_ref, dbi_ref):
        ar, ai, dt = ar_ref[...], ai_ref[...], jnp.exp(ldt_ref[...])
        er, ei = er_ref[...], ei_ref[...]
        gbr, gbi = dbbr_ref[...], dbbi_ref[...]
        dbr_ref[...], dbi_ref[...] = _cmul(er, -ei, gbr, gbi)
        br, bi = br_ref[...], bi_ref[...]
        der = jnp.sum(br * gbr + bi * gbi, axis=-1, keepdims=True)
        dei = jnp.sum(br * gbi - bi * gbr, axis=-1, keepdims=True)
        den = ar * ar + ai * ai
        inv_r, inv_i = ar / den, -ai / den
        t_r, t_i = _cmul(der, dei, inv_r, -inv_i)
        gab_r, gab_i = dabr_ref[...] + t_r, dabi_ref[...] + t_i
        q_r, q_i = _cmul(er, ei, inv_r, inv_i)
        dl_r, dl_i = _cmul(der, dei, q_r, -q_i)
        dl_r, dl_i = -dl_r, -dl_i
        gw_r, gw_i = _cmul(gab_r, gab_i, abr_ref[...], -abi_ref[...])
        dar_ref[...] = dl_r + dt * gw_r
        dai_ref[...] = dl_i + dt * gw_i
        ddt_ref[...] = (gw_r * ar + gw_i * ai) * dt

    col = jax.ShapeDtypeStruct((N_STATE, 1), F32)
    mat = jax.ShapeDtypeStruct((N_STATE, GROUP_CH), F32)
    return pl.pallas_call(body, name="ssm_discretize_bwd", out_shape=[col] * 3 + [mat] * 2)(
        a_re, a_im, log_dt, b_re, b_im, ab_re, ab_im, e_re, e_im, dab_re, dab_im, dbb_re, dbb_im)


def _scan_tables(p_re, p_im, reverse):
    pr, pi = p_re.T, p_im.T
    if reverse:
        pi = -pi
    row = jnp.arange(SCAN_ROWS)[:, None]
    level = lambda t, s: jnp.where((row < SCAN_ROWS - s) if reverse else (row >= s), t[s - 1][None, :], 0.0)
    carry = (pr[::-1], pi[::-1]) if reverse else (pr, pi)
    return jnp.stack([level(pr, 1), level(pi, 1), level(pr, 2), level(pi, 2), level(pr, 4), level(pi, 4), carry[0], carry[1]])


def _scan_group(xr, xi, tab_ref, ls, carry_r, carry_i, reverse):
    for n, s in enumerate((1, 2, 4)):
        shift = SCAN_ROWS - s if reverse else s
        mr, mi = _cmul(tab_ref[2 * n, :, ls], tab_ref[2 * n + 1, :, ls], pltpu.roll(xr, shift, 0), pltpu.roll(xi, shift, 0))
        xr, xi = xr + mr, xi + mi
    mr, mi = _cmul(tab_ref[6, :, ls], tab_ref[7, :, ls], carry_r, carry_i)
    return xr + mr, xi + mi


def _gelu(y):
    t = jnp.tanh(GELU_C * (y + 0.044715 * y * y * y))
    return 0.5 * y * (1.0 + t), t


def _ssm_fwd(u, tab, bd_re, bd_im, cd_re, cd_im, d_skip, w_glu, b_glu):
    T = u.shape[0]
    tc = SSM_CHUNK

    def body(u_ref, tab_ref, bdr_ref, bdi_ref, cdr_ref, cdi_ref, dsk_ref, wg_ref, bg_ref,
             sr_ref, si_ref, yp_ref, o_ref, car_r, car_i):
        @pl.when(pl.program_id(0) == 0)
        def _():
            car_r[...] = jnp.zeros_like(car_r)
            car_i[...] = jnp.zeros_like(car_i)

        uv = u_ref[...]
        for cs, ss in SSM_HALVES:
            sr_ref[:, ss] = _dot(uv[:, cs], bdr_ref[cs, ss])
            si_ref[:, ss] = _dot(uv[:, cs], bdi_ref[cs, ss])
        for lb in range(N_STATE // SCAN_LANES):
            ls = pl.ds(lb * SCAN_LANES, SCAN_LANES)

            def step(g, carry):
                rows = pl.ds(pl.multiple_of(g * SCAN_ROWS, SCAN_ROWS), SCAN_ROWS)
                xr, xi = _scan_group(sr_ref[rows, ls], si_ref[rows, ls], tab_ref, ls, carry[0], carry[1], False)
                sr_ref[rows, ls] = xr
                si_ref[rows, ls] = xi
                last = slice(SCAN_ROWS - 1, SCAN_ROWS)
                return (jnp.broadcast_to(xr[last], xr.shape), jnp.broadcast_to(xi[last], xi.shape))

            cr, ci = lax.fori_loop(0, tc // SCAN_ROWS, step, (car_r[:, ls], car_i[:, ls]))
            car_r[:, ls] = cr
            car_i[:, ls] = ci
        y = jnp.concatenate([_dot(sr_ref[:, ss], cdr_ref[ss, cs]) - _dot(si_ref[:, ss], cdi_ref[ss, cs])
                             for cs, ss in SSM_HALVES], axis=1) + dsk_ref[...] * uv
        yp_ref[...] = y
        gy, _ = _gelu(y)
        o_ref[...] = gy * jax.nn.sigmoid(_dot(gy, wg_ref[...]) + bg_ref[...])

    col, st = _rows(tc, SSM_W), _rows(tc, N_STATE)
    vec = _full((1, SSM_W))
    return pl.pallas_call(
        body, name="ssm_fwd", grid=(T // tc,),
        in_specs=[col, _full(tab.shape), _full(bd_re.shape), _full(bd_im.shape), _full(cd_re.shape), _full(cd_im.shape),
                  vec, _full(w_glu.shape), vec],
        out_specs=[st, st, col, col],
        out_shape=[jax.ShapeDtypeStruct((T, N_STATE), F32)] * 2 + [jax.ShapeDtypeStruct((T, SSM_W), F32)] * 2,
        scratch_shapes=[pltpu.VMEM((SCAN_ROWS, N_STATE), F32)] * 2,
        compiler_params=_params("arbitrary"),
    )(u, tab, bd_re, bd_im, cd_re, cd_im, d_skip, w_glu, b_glu)


def _ssm_bwd(dout, u, yp, s_re, s_im, tab, bd_re, bd_im, cd_re, cd_im, d_skip, w_glu, b_glu, rider=None):
    T = u.shape[0]
    tc = SSM_CHUNK_BWD
    nt = T // tc
    rows_per_chunk = tc // SCAN_ROWS

    def body(do_ref, u_ref, yp_ref, sr_ref, si_ref, pr_ref, pi_ref, tab_ref, dsk_ref, wg_ref, bg_ref,
             bdr_hbm, bdi_hbm, cdr_hbm, cdi_hbm,
             du_ref, dsk_out, dbg_out, dwg_out, da_out, dbdr_hbm, dbdi_hbm, dcdr_hbm, dcdi_hbm,
             bdr_v, bdi_v, cdr_v, cdi_v, dbdr_v, dbdi_v, dcdr_v, dcdi_v, gr_v, gi_v, car_r, car_i, sems):
        i = pl.program_id(0)

        @pl.when(i == 0)
        def _():
            _load_once([(bdr_hbm, bdr_v), (bdi_hbm, bdi_v), (cdr_hbm, cdr_v), (cdi_hbm, cdi_v)], sems)
            for ref in (dbdr_v, dbdi_v, dcdr_v, dcdi_v, car_r, car_i, dsk_out, dbg_out, dwg_out, da_out):
                ref[...] = jnp.zeros_like(ref)

        uv, y, dout_v = u_ref[...], yp_ref[...], do_ref[...]
        gy, t = _gelu(y)
        sg = jax.nn.sigmoid(_dot(gy, wg_ref[...]) + bg_ref[...])
        dzg = dout_v * gy * sg * (1.0 - sg)
        dgy = dout_v * sg + _dot_nt(dzg, wg_ref[...])
        dwg_out[...] += _dot_tn(gy, dzg)
        dbg_out[...] += jnp.sum(dzg, axis=0, keepdims=True)
        dy = dgy * (0.5 * (1.0 + t) + 0.5 * y * (1.0 - t * t) * GELU_C * (1.0 + 3 * 0.044715 * y * y))
        dsk_out[...] += jnp.sum(dy * uv, axis=0, keepdims=True)

        for cs, ss in SSM_HALVES:
            gr_v[:, ss] = _dot_nt(dy[:, cs], cdr_v[ss, cs])
            gi_v[:, ss] = -_dot_nt(dy[:, cs], cdi_v[ss, cs])
            dcdr_v[ss, cs] += _dot_tn(sr_ref[:, ss], dy[:, cs])
            dcdi_v[ss, cs] -= _dot_tn(si_ref[:, ss], dy[:, cs])

        row = lax.broadcasted_iota(jnp.int32, (SCAN_ROWS, SCAN_LANES), 0)
        first_chunk = i == nt - 1
        for lb in range(N_STATE // SCAN_LANES):
            ls = pl.ds(lb * SCAN_LANES, SCAN_LANES)

            def step(n, carry):
                g = rows_per_chunk - 1 - n
                rows = pl.ds(pl.multiple_of(g * SCAN_ROWS, SCAN_ROWS), SCAN_ROWS)
                before = pl.ds(pl.multiple_of(jnp.maximum(g - 1, 0) * SCAN_ROWS, SCAN_ROWS), SCAN_ROWS)
                xr, xi = _scan_group(gr_v[rows, ls], gi_v[rows, ls], tab_ref, ls, carry[0], carry[1], True)
                gr_v[rows, ls] = xr
                gi_v[rows, ls] = xi
                last = slice(SCAN_ROWS - 1, SCAN_ROWS)
                edge_r = jnp.where(g > 0, sr_ref[before, ls][last], jnp.where(first_chunk, 0.0, pr_ref[:, ls][last]))
                edge_i = jnp.where(g > 0, si_ref[before, ls][last], jnp.where(first_chunk, 0.0, pi_ref[:, ls][last]))
                spr = jnp.where(row >= 1, pltpu.roll(sr_ref[rows, ls], 1, 0), edge_r)
                spi = jnp.where(row >= 1, pltpu.roll(si_ref[rows, ls], 1, 0), edge_i)
                first = slice(0, 1)
                return (jnp.broadcast_to(xr[first], xr.shape), jnp.broadcast_to(xi[first], xi.shape),
                        carry[2] + xr * spr + xi * spi, carry[3] + xi * spr - xr * spi)

            zero = jnp.zeros((SCAN_ROWS, SCAN_LANES), F32)
            cr, ci, dar, dai = lax.fori_loop(0, rows_per_chunk, step, (car_r[:, ls], car_i[:, ls], zero, zero))
            car_r[:, ls] = cr
            car_i[:, ls] = ci
            da_out[0, :, ls] += dar
            da_out[1, :, ls] += dai

        du_ref[...] = dsk_ref[...] * dy + jnp.concatenate(
            [_dot_nt(gr_v[:, ss], bdr_v[cs, ss]) + _dot_nt(gi_v[:, ss], bdi_v[cs, ss]) for cs, ss in SSM_HALVES], axis=1)
        for cs, ss in SSM_HALVES:
            dbdr_v[cs, ss] += _dot_tn(uv[:, cs], gr_v[:, ss])
            dbdi_v[cs, ss] += _dot_tn(uv[:, cs], gi_v[:, ss])

        @pl.when(i == nt - 1)
        def _():
            outs = [(dbdr_v, dbdr_hbm), (dbdi_v, dbdi_hbm), (dcdr_v, dcdr_hbm), (dcdi_v, dcdi_hbm)]
            copies = [pltpu.make_async_copy(src, dst, sems.at[k]) for k, (src, dst) in enumerate(outs)]
            for c in copies:
                c.start()
            for c in copies:
                c.wait()

    rev = lambda i: (nt - 1 - i, 0)
    col = pl.BlockSpec((tc, SSM_W), rev)
    st = pl.BlockSpec((tc, N_STATE), rev)
    st_before = pl.BlockSpec((SCAN_ROWS, N_STATE), lambda i: (jnp.maximum((nt - 1 - i) * rows_per_chunk - 1, 0), 0))
    vec = _full((1, SSM_W))
    bd = jax.ShapeDtypeStruct(bd_re.shape, F32)
    cd = jax.ShapeDtypeStruct(cd_re.shape, F32)
    return _pallas(
        body, name="ssm_bwd", grid=(nt,),
        in_specs=[col, col, col, st, st, st_before, st_before, _full(tab.shape), vec, _full(w_glu.shape), vec,
                  ANY, ANY, ANY, ANY],
        out_specs=[col, vec, vec, _full(w_glu.shape), _full((2, SCAN_ROWS, N_STATE)), ANY, ANY, ANY, ANY],
        out_shape=[jax.ShapeDtypeStruct((T, SSM_W), F32), jax.ShapeDtypeStruct((1, SSM_W), F32),
                   jax.ShapeDtypeStruct((1, SSM_W), F32), jax.ShapeDtypeStruct(w_glu.shape, F32),
                   jax.ShapeDtypeStruct((2, SCAN_ROWS, N_STATE), F32), bd, bd, cd, cd],
        scratch_shapes=[pltpu.VMEM(bd_re.shape, BF16)] * 2 + [pltpu.VMEM(cd_re.shape, BF16)] * 2
        + [pltpu.VMEM(bd_re.shape, F32)] * 2 + [pltpu.VMEM(cd_re.shape, F32)] * 2
        + [pltpu.VMEM((tc, N_STATE), F32)] * 2 + [pltpu.VMEM((SCAN_ROWS, N_STATE), F32)] * 2
        + [pltpu.SemaphoreType.DMA((4,))],
        semantics=("arbitrary",), rider=rider,
        args=(dout, u, yp, s_re, s_im, s_re, s_im, tab, d_skip, w_glu, b_glu, bd_re, bd_im, cd_re, cd_im))


def _block_diag(t):
    g, a, b = t.shape
    eye = jnp.eye(N_GROUPS, dtype=t.dtype)
    return (t[:, :, None, :] * eye[:, None, :, None]).reshape(g * a, g * b)


def _diag_blocks(m, a, b):
    eye = jnp.eye(N_GROUPS, dtype=m.dtype)
    return jnp.sum(m.reshape(N_GROUPS, a, N_GROUPS, b) * eye[:, None, :, None], axis=2)


def _ssm_prepare(a_re, a_im, log_dt, b_re, b_im, c_re, c_im):
    col = lambda t: t.reshape(N_STATE, 1)
    ldt = jnp.broadcast_to(log_dt.reshape(N_GROUPS, 1), (N_GROUPS, STATE)).reshape(N_STATE, 1)
    b2r, b2i = b_re.reshape(N_STATE, GROUP_CH), b_im.reshape(N_STATE, GROUP_CH)
    ab_r, ab_i, e_r, e_i, bb_r, bb_i, p_r, p_i = _ssm_discretize(col(a_re), col(a_im), ldt, b2r, b2i)
    bd = [_block_diag(jnp.swapaxes(t.reshape(N_GROUPS, STATE, GROUP_CH), 1, 2)).astype(BF16) for t in (bb_r, bb_i)]
    cd = [_block_diag(jnp.swapaxes(t.reshape(N_GROUPS, GROUP_CH, STATE), 1, 2)).astype(BF16) for t in (c_re, c_im)]
    saved = dict(a_re=col(a_re), a_im=col(a_im), log_dt=ldt, b_re=b2r, b_im=b2i, ab_re=ab_r, ab_im=ab_i, e_re=e_r, e_im=e_i)
    return _scan_tables(p_r, p_i, False), _scan_tables(p_r, p_i, True), bd, cd, saved


BIG = ("ffn1_w_in", "ffn1_w_out", "w_mix_in", "w_glu", "w_mix_out", "ffn2_w_in", "ffn2_w_out")
SMALL = ("ffn1_pre_g", "ffn1_post_g", "mix_pre_g", "a_re", "a_im", "log_dt", "b_re", "b_im", "c_re", "c_im",
         "d_skip", "b_glu", "mix_post_g", "ffn2_pre_g", "ffn2_post_g")


FIRST = ("ffn1_w_in", "ffn1_w_out")
REST = ("w_mix_in", "w_glu", "w_mix_out", "ffn2_w_in", "ffn2_w_out")
LATE = ("w_mix_in", "w_glu", "w_mix_out", "ffn1_w_out")
SHARD_SHAPE = {"ffn1_w_in": (D_MODEL, FF_BLK), "ffn2_w_in": (D_MODEL, FF_BLK), "ffn1_w_out": (D_FF // N_CHIPS, D_MODEL),
               "ffn2_w_out": (D_FF // N_CHIPS, D_MODEL), "w_mix_in": (D_MODEL, ATTN_W), "w_glu": (SSM_W // N_CHIPS, SSM_W),
               "w_mix_out": (2 * ATTN_W // N_CHIPS, D_MODEL)}


class _Reduction:
    def __init__(self, names, grads, place):
        self.names, self.local, self.place = list(names), list(grads), place

    def exchange(self):
        return _pair_exchange(self.local)

    def first(self, got):
        self.pair = [_pair_sum(a, b, self.place["core"], name=f"pair_sum_{n}") for n, a, b in zip(self.names, self.local, got)]
        return _reduce_first(self.pair)

    def second(self, got):
        self.sums = [_reduce_sum_first(a, b, self.place["sel_first"], name=f"sum_first_{n}")
                     for n, a, b in zip(self.names, self.pair, got)]
        return _reduce_second(self.sums)

    def swap(self, got):
        self.halves = [_reduce_sum_second(a, b, self.place["sel_second"], name=f"sum_second_{n}").reshape(2 * a.shape[2], a.shape[3])
                       for n, a, b in zip(self.names, self.sums, got)]
        return _pair_swap(self.halves)

    def done(self, got):
        return {n: (mine, theirs) for n, mine, theirs in zip(self.names, self.halves, got)}


def _local_step(x, target, p, w, place=None):
    vec = lambda t: t.reshape(1, -1)
    w1_in, w1_out = w["ffn1_w_in"], w["ffn1_w_out"].reshape(2, FF_BLK, D_MODEL)
    blocks = lambda n, t: t.reshape((N_CHIPS,) + SHARD_SHAPE[n])

    ffn1 = functools.partial(_ffn_fwd, x, vec(p["ffn1_pre_g"]), w1_in, w1_out, vec(p["ffn1_post_g"]), None, name="ffn1_fwd")
    if place is None:
        x1, z1, f1 = ffn1()
    else:
        (x1, z1, f1), rest = ffn1(rider=_gather_weights([w[n] for n in REST]))
        w = dict(w, **dict(zip(REST, rest)))
    w2_in, w2_out = w["ffn2_w_in"], w["ffn2_w_out"].reshape(2, FF_BLK, D_MODEL)
    w_mi, w_glu, w_mo = w["w_mix_in"], w["w_glu"].reshape(SSM_W, SSM_W), w["w_mix_out"].reshape(2, ATTN_W, D_MODEL)
    q, k, v, u = _mix_in_fwd(x1, vec(p["mix_pre_g"]), w_mi)
    attn, lse = _attention_fwd(q, k, v)
    tab_f, tab_b, bd, cd, sv = _ssm_prepare(p["a_re"], p["a_im"], p["log_dt"], p["b_re"], p["b_im"], p["c_re"], p["c_im"])
    ssm_args = (bd[0], bd[1], cd[0], cd[1], vec(p["d_skip"]), w_glu, vec(p["b_glu"]))
    s_re, s_im, yp, ssm = _ssm_fwd(u, tab_f, *ssm_args)
    x2, mixed = _mix_out_fwd(x1, attn, ssm, w_mo, vec(p["mix_post_g"]))
    dx3, loss_rows, z2, f2 = _ffn_fwd(x2, vec(p["ffn2_pre_g"]), w2_in, w2_out, vec(p["ffn2_post_g"]), target, name="ffn2_fwd")

    g = {}
    ride = (lambda call, exchange: call(rider=exchange)) if place else (lambda call, exchange: (call(), None))
    dz2, g["ffn2_post_g"], dw2_out = _ffn_bwd_out(dx3, f2, z2, w2_out, vec(p["ffn2_post_g"]), name="ffn2_bwd_out")
    dw2_in = _norm_matmul_dw(dz2, x2, vec(p["ffn2_pre_g"]), N_CHIPS, name="ffn2_bwd_dw")
    early = _Reduction(("ffn2_w_in", "ffn2_w_out"), [dw2_in, blocks("ffn2_w_out", dw2_out)], place) if place else None
    (dx2, g["ffn2_pre_g"]), got = ride(
        functools.partial(_norm_matmul_dx, dz2, x2, dx3, vec(p["ffn2_pre_g"]), w2_in, name="ffn2_bwd_dx"), early and early.exchange())
    dattn, dssm, dw_mo, g["mix_post_g"] = _mix_out_bwd(dx2, mixed, attn, ssm, w_mo, vec(p["mix_post_g"]))
    (du, g["d_skip"], g["b_glu"], dw_glu, da, dbd_re, dbd_im, dcd_re, dcd_im), got = ride(
        functools.partial(_ssm_bwd, dssm, u, yp, s_re, s_im, tab_b, *ssm_args), early and early.first(got))
    branch = lambda d: functools.partial(_attn_branch_bwd, q, k, v, attn, lse, dattn, d)
    parts = [None] * 3
    parts[0], got = ride(branch(DILATIONS[0]), early and early.second(got))
    parts[1], early_theirs = ride(branch(DILATIONS[1]), early and early.swap(got))
    parts[2] = branch(DILATIONS[2])()
    dproj = _dproj_merge([r[0] for r in parts], [r[1] for r in parts], [r[2] for r in parts], du)
    dw_mi = _norm_matmul_dw(dproj, x1, vec(p["mix_pre_g"]), N_CHIPS, name="mix_bwd_dw")
    dx1, g["mix_pre_g"] = _norm_matmul_dx(dproj, x1, dx2, vec(p["mix_pre_g"]), w_mi, name="mix_bwd_dx")
    dz1, g["ffn1_post_g"], dw1_out = _ffn_bwd_out(dx1, f1, z1, w1_out, vec(p["ffn1_post_g"]), name="ffn1_bwd_out")
    big = {"ffn2_w_in": dw2_in, "ffn2_w_out": dw2_out, "w_mix_in": dw_mi, "w_glu": dw_glu, "w_mix_out": dw_mo, "ffn1_w_out": dw1_out}
    late = _Reduction(LATE, [blocks(n, big[n]) for n in LATE], place) if place else None
    big["ffn1_w_in"], got = ride(
        functools.partial(_norm_matmul_dw, dz1, x, vec(p["ffn1_pre_g"]), N_CHIPS, name="ffn1_bwd_dw"), late and late.exchange())
    dx_call = functools.partial(_norm_matmul_dx, dz1, x, dx1, vec(p["ffn1_pre_g"]), w1_in)
    if place:
        last = _Reduction(("ffn1_w_in",), [big["ffn1_w_in"]], place)
        n_late = len(LATE)
        (dx_half, dg_a), got = dx_call(name="ffn1_bwd_dx_a", part=(0, 2), rider=_merged(late.first(got), last.exchange()))
        (grad_x, dg_b), got = dx_call(name="ffn1_bwd_dx_b", part=(1, 2), into=dx_half,
                                      rider=_merged(late.second(got[:n_late]), last.first(got[n_late:])))
        g["ffn1_pre_g"] = dg_a + dg_b
        got = _merged(late.swap(got[:n_late]), last.second(got[n_late:])).run("tail_second")
        g.update(late.done(got[:n_late]))
        g.update(last.done(last.swap(got[n_late:]).run("tail_swap")))
        g.update(early.done(early_theirs))
    else:
        grad_x, g["ffn1_pre_g"] = dx_call(name="ffn1_bwd_dx")
        g.update({n: blocks(n, big[n]) for n in BIG})

    g["c_re"], g["c_im"] = (jnp.swapaxes(_diag_blocks(m, STATE, GROUP_CH), 1, 2) for m in (dcd_re, dcd_im))
    dbb = [jnp.swapaxes(_diag_blocks(m, GROUP_CH, STATE), 1, 2).reshape(N_STATE, GROUP_CH) for m in (dbd_re, dbd_im)]
    dab = [jnp.sum(da[n], axis=0).reshape(N_STATE, 1) for n in range(2)]
    da_re, da_im, dldt, db_re, db_im = _ssm_discretize_bwd(
        sv["a_re"], sv["a_im"], sv["log_dt"], sv["b_re"], sv["b_im"], sv["ab_re"], sv["ab_im"], sv["e_re"], sv["e_im"],
        dab[0], dab[1], dbb[0], dbb[1])
    g["a_re"], g["a_im"] = da_re.reshape(N_GROUPS, STATE), da_im.reshape(N_GROUPS, STATE)
    g["log_dt"] = jnp.sum(dldt.reshape(N_GROUPS, STATE), axis=-1)
    g["b_re"], g["b_im"] = (t.reshape(N_GROUPS, STATE, GROUP_CH) for t in (db_re, db_im))
    return loss_rows, grad_x, g


MESH = pl.DeviceIdType.MESH
N_REL = 3


def _place():
    x, y, c = lax.axis_index("x"), lax.axis_index("y"), lax.axis_index("c")
    return x, y, c, [(1 - x, y), (x, 1 - y), (1 - x, 1 - y)]


def _remote(src, dst, send_sems, recv_sems, idx, to):
    return pltpu.make_async_remote_copy(src_ref=src, dst_ref=dst, send_sem=send_sems.at[idx], recv_sem=recv_sems.at[idx],
                                        device_id=to, device_id_type=MESH)


def _half(rows, who):
    return pl.ds(who * (rows // 2), rows // 2)


class _Copies:
    def __init__(self, send_sems, recv_sems, local_sems):
        self.send_sems, self.recv_sems, self.local_sems = send_sems, recv_sems, local_sems
        self.n_remote = self.n_local = 0

    def remote(self, src, dst, to):
        k, self.n_remote = self.n_remote, self.n_remote + 1
        return pltpu.make_async_remote_copy(src_ref=src, dst_ref=dst, send_sem=self.send_sems.at[k],
                                            recv_sem=self.recv_sems.at[k], device_id=to, device_id_type=MESH)

    def local(self, src, dst):
        k, self.n_local = self.n_local, self.n_local + 1
        return pltpu.make_async_copy(src, dst, self.local_sems.at[k])


class _Exchange:
    def __init__(self, plan, ins, out_shapes, n_remote):
        self.plan, self.ins, self.out_shapes, self.n_remote = plan, list(ins), list(out_shapes), n_remote
        self.sems = [pltpu.SemaphoreType.DMA((n_remote,)), pltpu.SemaphoreType.DMA((n_remote,)), pltpu.SemaphoreType.DMA((1,))]

    def run(self, name):
        n_in = len(self.ins)

        def body(*refs):
            for phase in self.plan(refs[:n_in], refs[n_in:-3], _Copies(*refs[-3:])):
                for cp in phase:
                    cp.start()
                for cp in phase:
                    cp.wait()

        return pl.pallas_call(body, name=name, in_specs=[ANY] * n_in, out_specs=[ANY] * len(self.out_shapes),
                              out_shape=self.out_shapes, scratch_shapes=self.sems)(*self.ins)


def _merged(a, b):
    n_in, n_out = len(a.ins), len(a.out_shapes)

    def plan(ins, outs, mk):
        (phase_a,), (phase_b,) = a.plan(ins[:n_in], outs[:n_out], mk), b.plan(ins[n_in:], outs[n_out:], mk)
        return [phase_a + phase_b]

    return _Exchange(plan, a.ins + b.ins, a.out_shapes + b.out_shapes, a.n_remote + b.n_remote)


def _pallas(body, *, name, grid, in_specs, out_specs, out_shape, args, semantics, scratch_shapes=(), rider=None, aliases=None):
    aliases = aliases or {}
    if rider is None:
        return pl.pallas_call(body, name=name, grid=grid, in_specs=in_specs, out_specs=out_specs, out_shape=out_shape,
                              scratch_shapes=list(scratch_shapes), input_output_aliases=aliases,
                              compiler_params=_params(*semantics))(*args)
    n_in, n_out, r_in, r_out = len(in_specs), len(out_specs), len(rider.ins), len(rider.out_shapes)
    n_steps = math.prod(grid)

    def carrier(*refs):
        ins, rider_ins = refs[:n_in], refs[n_in:n_in + r_in]
        outs = refs[n_in + r_in:n_in + r_in + n_out]
        rider_outs = refs[n_in + r_in + n_out:n_in + r_in + n_out + r_out]
        scratch, sems = refs[n_in + r_in + n_out + r_out:-3], refs[-3:]
        step = 0
        for axis, size in enumerate(grid):
            step = step * size + pl.program_id(axis)
        phases = rider.plan(rider_ins, rider_outs, _Copies(*sems))

        def start_phase(p):
            for cp in (phases[p - 1] if p else []):
                cp.wait()
            for cp in phases[p]:
                cp.start()

        for p in range(len(phases)):
            pl.when(step == p * n_steps // len(phases))(functools.partial(start_phase, p))
        body(*ins, *outs, *scratch)

        @pl.when(step == n_steps - 1)
        def _():
            for cp in phases[-1]:
                cp.wait()

    results = pl.pallas_call(
        carrier, name=name, grid=grid, in_specs=list(in_specs) + [ANY] * r_in, out_specs=list(out_specs) + [ANY] * r_out,
        out_shape=list(out_shape) + rider.out_shapes, scratch_shapes=list(scratch_shapes) + rider.sems,
        input_output_aliases=aliases, compiler_params=_params(*["arbitrary"] * len(grid)))(*args, *rider.ins)
    return results[:n_out], results[n_out:]


def _gather_weights(shards):
    def plan(ins, outs, mk):
        x, y, c = lax.axis_index("x"), lax.axis_index("y"), lax.axis_index("c")
        me, sibling = 2 * x + y, (x, y, 1 - c)
        x_nb, y_nb, diag = (1 - x, y), (x, 1 - y), (1 - x, 1 - y)
        index = lambda chip: 2 * chip[0] + chip[1]
        first, second, third = [], [], []
        for i, shard in enumerate(shards):
            rows = shard.shape[0]
            mine = _half(rows, c)
            quarter = lambda which: pl.ds(c * (rows // 2) + which * (rows // 4), rows // 4)
            first.append(mk.remote(ins[i], outs[i].at[me], sibling))
            for nb in (x_nb, y_nb):
                first.append(mk.remote(ins[i].at[mine], outs[i].at[me, mine], (*nb, c)))
            for nb in (x_nb, y_nb):
                landed = outs[i].at[index(nb), mine]
                second.append(mk.remote(landed, landed, sibling))
            for nb, other, which in ((x_nb, y_nb, 0), (y_nb, x_nb, 1)):
                landed = outs[i].at[index(nb), quarter(which)]
                second.append(mk.remote(landed, landed, (*other, c)))
            landed = outs[i].at[index(diag), mine]
            third.append(mk.remote(landed, landed, sibling))
        return [first, second, third]

    return _Exchange(plan, shards, [jax.ShapeDtypeStruct((N_CHIPS,) + s.shape, s.dtype) for s in shards], 8 * len(shards))


def _pair_exchange(grads):
    def plan(ins, outs, mk):
        x, y, c = lax.axis_index("x"), lax.axis_index("y"), lax.axis_index("c")
        return [[mk.remote(ins[i].at[:, _half(g.shape[1], 1 - c)], outs[i], (x, y, 1 - c)) for i, g in enumerate(grads)]]

    return _Exchange(plan, grads, [jax.ShapeDtypeStruct((N_CHIPS, g.shape[1] // 2, g.shape[2]), g.dtype) for g in grads], len(grads))


def _reduce_first(pair):
    def plan(ins, outs, mk):
        x, y, c = lax.axis_index("x"), lax.axis_index("y"), lax.axis_index("c")
        phase = []
        for i, p in enumerate(pair):
            q = p.shape[1] // 2
            phase.append(mk.remote(ins[i].at[pl.ds(2 * (1 - x), 2), pl.ds(0, q)], outs[i].at[0], (1 - x, y, c)))
            for jx in range(2):
                phase.append(mk.remote(ins[i].at[2 * jx + 1 - y, pl.ds(q, q)], outs[i].at[1, jx], (x, 1 - y, c)))
        return [phase]

    return _Exchange(plan, pair, [jax.ShapeDtypeStruct((2, 2, p.shape[1] // 2, p.shape[2]), p.dtype) for p in pair], 3 * len(pair))


def _reduce_second(sums):
    def plan(ins, outs, mk):
        x, y, c = lax.axis_index("x"), lax.axis_index("y"), lax.axis_index("c")
        phase = []
        for i in range(len(sums)):
            phase.append(mk.remote(ins[i].at[0, 1 - y], outs[i].at[0], (x, 1 - y, c)))
            phase.append(mk.remote(ins[i].at[1, 1 - x], outs[i].at[1], (1 - x, y, c)))
        return [phase]

    return _Exchange(plan, sums, [jax.ShapeDtypeStruct((2,) + s.shape[2:], s.dtype) for s in sums], 2 * len(sums))


def _pair_swap(halves):
    def plan(ins, outs, mk):
        x, y, c = lax.axis_index("x"), lax.axis_index("y"), lax.axis_index("c")
        return [[mk.remote(ins[i], outs[i], (x, y, 1 - c)) for i in range(len(halves))]]

    return _Exchange(plan, halves, [jax.ShapeDtypeStruct(h.shape, h.dtype) for h in halves], len(halves))


def _allreduce_small(packed):
    rows = packed.shape[0]
    n_dev = 2 * N_CHIPS

    def body(x_ref, o_ref, buf, send_sems, recv_sems):
        x, y, c, chips = _place()
        sibling = (x, y, 1 - c)

        def slot(px, py, pc):
            return buf.at[4 * px + 2 * py + pc]

        buf[4 * x + 2 * y + c] = x_ref[...]
        first = [_remote(x_ref, slot(x, y, c), send_sems, recv_sems, 0, sibling)]
        first += [_remote(x_ref, slot(x, y, c), send_sems, recv_sems, 1 + k, (*chip, c)) for k, chip in enumerate(chips)]
        for cp in first:
            cp.start()
        passed = []
        for k, chip in enumerate(chips):
            landed = slot(*chip, c)
            _remote(landed, landed, send_sems, recv_sems, 1 + k, (*chip, c)).wait_recv()
            passed.append(_remote(landed, landed, send_sems, recv_sems, 1 + N_REL + k, sibling))
            passed[-1].start()
        _remote(slot(*sibling), slot(*sibling), send_sems, recv_sems, 0, sibling).wait_recv()
        for k, chip in enumerate(chips):
            landed = slot(*chip, 1 - c)
            _remote(landed, landed, send_sems, recv_sems, 1 + N_REL + k, sibling).wait_recv()
        for cp in first + passed:
            cp.wait_send()
        total = buf[0]
        for d in range(1, n_dev):
            total = total + buf[d]
        o_ref[...] = total

    vm = pl.BlockSpec(memory_space=pltpu.VMEM)
    return pl.pallas_call(
        body, name="allreduce_small", in_specs=[vm], out_specs=vm, out_shape=jax.ShapeDtypeStruct(packed.shape, F32),
        scratch_shapes=[pltpu.VMEM((n_dev, rows, 128), F32), pltpu.SemaphoreType.DMA((1 + 2 * N_REL,)),
                        pltpu.SemaphoreType.DMA((1 + 2 * N_REL,))],
    )(packed)


def _row_tile(rows, cap=256):
    return max(t for t in range(8, cap + 1, 8) if rows % t == 0)


def _pair_sum(grad, got, c, name):
    _, half, cols = got.shape
    tr = _row_tile(half)
    nt = half // tr

    def body(c_ref, g_ref, r_ref, o_ref):
        o_ref[...] = g_ref[...] + r_ref[...]

    blk = (1, tr, cols)
    return pl.pallas_call(
        body, name=name,
        grid_spec=pltpu.PrefetchScalarGridSpec(
            num_scalar_prefetch=1, grid=(N_CHIPS, nt),
            in_specs=[pl.BlockSpec(blk, lambda j, t, c_ref: (j, c_ref[0] * nt + t, 0)),
                      pl.BlockSpec(blk, lambda j, t, c_ref: (j, t, 0))],
            out_specs=pl.BlockSpec(blk, lambda j, t, c_ref: (j, t, 0))),
        out_shape=jax.ShapeDtypeStruct(got.shape, F32), compiler_params=_params("parallel", "parallel"),
    )(c, grad, got)


def _reduce_sum_first(pair, got, sel, name):
    _, _, q, cols = got.shape
    tr = _row_tile(q)
    nt = q // tr

    def body(sel_ref, p_ref, r_ref, o_ref):
        o_ref[0, 0] = p_ref[0] + r_ref[0, 0]

    blk = (1, 1, tr, cols)
    return pl.pallas_call(
        body, name=name,
        grid_spec=pltpu.PrefetchScalarGridSpec(
            num_scalar_prefetch=1, grid=(2, 2, nt),
            in_specs=[pl.BlockSpec((1, tr, cols), lambda p, k, t, s: (s[2 * p] + s[2 * p + 1] * k, p * nt + t, 0)),
                      pl.BlockSpec(blk, lambda p, k, t, s: (p, k, t, 0))],
            out_specs=pl.BlockSpec(blk, lambda p, k, t, s: (p, k, t, 0))),
        out_shape=jax.ShapeDtypeStruct(got.shape, F32), compiler_params=_params("parallel", "parallel", "parallel"),
    )(sel, pair, got)


def _reduce_sum_second(sums, got, sel, name):
    _, q, cols = got.shape
    tr = _row_tile(q)

    def body(sel_ref, s_ref, r_ref, o_ref):
        o_ref[0] = s_ref[0, 0] + r_ref[0]

    blk = (1, tr, cols)
    return pl.pallas_call(
        body, name=name,
        grid_spec=pltpu.PrefetchScalarGridSpec(
            num_scalar_prefetch=1, grid=(2, q // tr),
            in_specs=[pl.BlockSpec((1, 1, tr, cols), lambda p, t, s: (p, s[p], t, 0)),
                      pl.BlockSpec(blk, lambda p, t, s: (p, t, 0))],
            out_specs=pl.BlockSpec(blk, lambda p, t, s: (p, t, 0))),
        out_shape=jax.ShapeDtypeStruct(got.shape, F32), compiler_params=_params("parallel", "parallel"),
    )(sel, sums, got)


def _adamw_update(w, g, m, v):
    m2 = ADAM_B1 * m + (1.0 - ADAM_B1) * g
    v2 = ADAM_B2 * v + (1.0 - ADAM_B2) * (g * g)
    m_hat = m2 / (1.0 - ADAM_B1 ** ADAM_STEP)
    v_hat = v2 / (1.0 - ADAM_B2 ** ADAM_STEP)
    return -ADAM_LR * (m_hat / (jnp.sqrt(v_hat) + ADAM_EPS) + ADAM_WD * w), m2, v2


def _adamw(w, g, m, v, name):
    rows, cols = w.shape
    tr = _row_tile(rows)

    def body(w_ref, g_ref, m_ref, v_ref, d_ref, mo_ref, vo_ref):
        d_ref[...], mo_ref[...], vo_ref[...] = _adamw_update(w_ref[...], g_ref[...], m_ref[...], v_ref[...])

    blk = _rows(tr, cols)
    return pl.pallas_call(
        body, name=name, grid=(rows // tr,), in_specs=[blk] * 4, out_specs=[blk] * 3,
        out_shape=[jax.ShapeDtypeStruct(w.shape, F32)] * 3, compiler_params=_params("parallel"),
    )(w, g, m, v)


def _adamw_halves(w, mine, theirs, m, v, core, name):
    rows, cols = w.shape
    tr = _row_tile(rows // 2)
    per_half = rows // 2 // tr

    def body(core_ref, w_ref, a_ref, b_ref, m_ref, v_ref, g_ref, d_ref, mo_ref, vo_ref):
        g = jnp.where(pl.program_id(0) // per_half == core_ref[0], a_ref[...], b_ref[...])
        g_ref[...] = g
        d_ref[...], mo_ref[...], vo_ref[...] = _adamw_update(w_ref[...], g, m_ref[...], v_ref[...])

    blk = pl.BlockSpec((tr, cols), lambda t, c: (t, 0))
    half = lambda own: pl.BlockSpec(
        (tr, cols), lambda t, c: (jnp.clip(t - (c[0] if own else 1 - c[0]) * per_half, 0, per_half - 1), 0))
    return pl.pallas_call(
        body, name=name,
        grid_spec=pltpu.PrefetchScalarGridSpec(
            num_scalar_prefetch=1, grid=(2 * per_half,), in_specs=[blk, half(True), half(False), blk, blk], out_specs=[blk] * 4),
        out_shape=[jax.ShapeDtypeStruct(w.shape, F32)] * 4, compiler_params=_params("arbitrary"),
    )(core, w, mine, theirs, m, v)


def _pack(parts):
    flat = []
    for t in parts:
        t = t.reshape(-1).astype(F32)
        flat.append(jnp.pad(t, (0, -t.shape[0] % 128)))
    flat = jnp.concatenate(flat)
    return jnp.pad(flat, (0, -flat.shape[0] % 1024)).reshape(-1, 128)


def _unpack(buf, shapes):
    flat, out, at = buf.reshape(-1), [], 0
    for s in shapes:
        size = math.prod(s)
        out.append(flat[at:at + size].reshape(s))
        at += size + (-size % 128)
    return out


def kernel(x, ffn1_pre_g, ffn1_w_in, ffn1_w_out, ffn1_post_g, mix_pre_g, w_mix_in, a_re, a_im, log_dt, b_re, b_im, c_re, c_im, d_skip, w_glu, b_glu, w_mix_out, mix_post_g, ffn2_pre_g, ffn2_w_in, ffn2_w_out, ffn2_post_g, loss_target, m_ffn1_pre_g, m_ffn1_w_in, m_ffn1_w_out, m_ffn1_post_g, m_mix_pre_g, m_w_mix_in, m_a_re, m_a_im, m_log_dt, m_b_re, m_b_im, m_c_re, m_c_im, m_d_skip, m_w_glu, m_b_glu, m_w_mix_out, m_mix_post_g, m_ffn2_pre_g, m_ffn2_w_in, m_ffn2_w_out, m_ffn2_post_g, v_ffn1_pre_g, v_ffn1_w_in, v_ffn1_w_out, v_ffn1_post_g, v_mix_pre_g, v_w_mix_in, v_a_re, v_a_im, v_log_dt, v_b_re, v_b_im, v_c_re, v_c_im, v_d_skip, v_w_glu, v_b_glu, v_w_mix_out, v_mix_post_g, v_ffn2_pre_g, v_ffn2_w_in, v_ffn2_w_out, v_ffn2_post_g):
    given = dict(locals())
    order = ("ffn1_pre_g", "ffn1_w_in", "ffn1_w_out", "ffn1_post_g", "mix_pre_g", "w_mix_in", "a_re", "a_im", "log_dt",
             "b_re", "b_im", "c_re", "c_im", "d_skip", "w_glu", "b_glu", "w_mix_out", "mix_post_g", "ffn2_pre_g",
             "ffn2_w_in", "ffn2_w_out", "ffn2_post_g")
    at_x, at_y, at_c = (lax.axis_index(a).astype(jnp.int32) for a in ("x", "y", "c"))
    place = dict(core=at_c.reshape(1), sel_first=jnp.stack([2 * at_x, jnp.int32(1), at_y, jnp.int32(2)]),
                 sel_second=jnp.stack([at_y, at_x]))

    shards = {n: given[n][0] for n in BIG}
    w = {n: shards[n].astype(BF16) for n in REST}
    w.update(zip(FIRST, _gather_weights([shards[n].astype(BF16) for n in FIRST]).run("gather_first")))
    small = {n: given[n][0] for n in SMALL}
    loss_rows, grad_x, g = _local_step(x[0], loss_target[0], small, w, place)

    total = _allreduce_small(_pack([g[n] for n in SMALL] + [loss_rows[0, :1]]))
    parts = _unpack(total, [small[n].shape for n in SMALL] + [(1,)])
    grads = dict(zip(SMALL, parts[:-1]))
    loss = parts[-1][0]

    delta, new_m, new_v = {}, {}, {}
    for n in BIG:
        grads[n], delta[n], new_m[n], new_v[n] = _adamw_halves(
            shards[n], *g[n], given["m_" + n][0], given["v_" + n][0], place["core"], name=f"adamw_{n}")
    packed = [_pack([given[pre + n] for n in SMALL]) for pre in ("", "m_", "v_")]
    outs = _adamw(packed[0], _pack([grads[n] for n in SMALL]), packed[1], packed[2], name="adamw_small")
    for store, buf in zip((delta, new_m, new_v), outs):
        store.update(zip(SMALL, _unpack(buf, [small[n].shape for n in SMALL])))

    lead = lambda d: [d[n][None] for n in order]
    return (loss, grad_x[None], *lead(grads), *lead(delta), *lead(new_m), *lead(new_v))
```

```python
import functools
import math

import jax
import jax.numpy as jnp
from jax import lax
from jax.experimental import pallas as pl
from jax.experimental.pallas import tpu as pltpu

F32, BF16 = jnp.float32, jnp.bfloat16

D_MODEL = 1024
D_FF = 2816
N_CHIPS = 4
FF_BLK = 2 * D_FF // N_CHIPS
ATTN_W = 512
SSM_W = 512
HEAD_DIM = 64
N_HEADS = ATTN_W // HEAD_DIM
DILATIONS = (1, 4, 16)
N_BACK = 128
QBLK = 128
N_GROUPS = 32
GROUP_CH = 16
STATE = 64
N_STATE = N_GROUPS * STATE
EPS = 1e-6
NEG = -1e30
GELU_C = math.sqrt(2.0 / math.pi)

ADAM_LR, ADAM_B1, ADAM_B2, ADAM_EPS, ADAM_WD, ADAM_STEP = 0.001, 0.9, 0.999, 1e-08, 0.01, 10

VMEM_LIMIT_V7X = 60 * 1024 * 1024
ROW_TILE = 512
FFN_ROW_TILE = 512
DW_ROW_TILE = 1024


def _params(*sem):
    return pltpu.CompilerParams(dimension_semantics=sem, vmem_limit_bytes=VMEM_LIMIT_V7X)


def _dot(a, b):
    return jnp.dot(a.astype(BF16), b.astype(BF16), preferred_element_type=F32)


def _dot_nt(a, b):
    return lax.dot_general(a.astype(BF16), b.astype(BF16), (((1,), (1,)), ((), ())), preferred_element_type=F32)


def _dot_tn(a, b):
    return lax.dot_general(a.astype(BF16), b.astype(BF16), (((0,), (0,)), ((), ())), preferred_element_type=F32)


def _full(shape):
    return pl.BlockSpec(shape, lambda *_: (0,) * len(shape))


def _rows(tm, width):
    return pl.BlockSpec((tm, width), lambda i: (i, 0))


ANY = pl.BlockSpec(memory_space=pl.ANY)


def _load_once(pairs, sems):
    copies = [pltpu.make_async_copy(src, dst, sems.at[k]) for k, (src, dst) in enumerate(pairs)]
    for c in copies:
        c.start()
    for c in copies:
        c.wait()


def _rms(x):
    return lax.rsqrt(jnp.mean(x * x, axis=-1, keepdims=True) + EPS)


def _rms_bwd(dy_g, xn, r):
    return r * (dy_g - xn * jnp.mean(dy_g * xn, axis=-1, keepdims=True))


def _ffn_fwd(x, g_pre, w_in, w_out, g_post, target, *, name, rider=None):
    T = x.shape[0]
    tm = FFN_ROW_TILE
    with_loss = target is not None

    def body(*refs):
        if with_loss:
            x_ref, gpre_ref, gpost_ref, tgt_ref, win_hbm, wout_hbm, o_ref, loss_ref, z_ref, f_ref, win_v, wout_v, sems = refs
        else:
            x_ref, gpre_ref, gpost_ref, win_hbm, wout_hbm, o_ref, z_ref, f_ref, win_v, wout_v, sems = refs

        @pl.when(pl.program_id(0) == 0)
        def _():
            _load_once([(win_hbm, win_v), (wout_hbm, wout_v)], sems)
            if with_loss:
                loss_ref[...] = jnp.zeros_like(loss_ref)

        xv = x_ref[...]
        h = (xv * _rms(xv) * gpre_ref[...]).astype(BF16)
        f = jnp.zeros((tm, D_MODEL), F32)
        for k in range(2):
            gate = _dot(h, win_v[k])
            up = _dot(h, win_v[k + 2])
            z_ref[:, k * FF_BLK:(k + 1) * FF_BLK] = gate.astype(BF16)
            z_ref[:, D_FF + k * FF_BLK:D_FF + (k + 1) * FF_BLK] = up.astype(BF16)
            f = f + _dot(gate * jax.nn.sigmoid(gate) * up, wout_v[k])
        f_ref[...] = f
        out = xv + 0.5 * (f * _rms(f) * gpost_ref[...])
        if with_loss:
            err = out - tgt_ref[...]
            o_ref[...] = err * (1.0 / D_MODEL)
            loss_ref[...] += jnp.sum(err * err) * (0.5 / D_MODEL)
        else:
            o_ref[...] = out

    row = _rows(tm, D_MODEL)
    vec = _full((1, D_MODEL))
    in_specs = [row, vec, vec] + ([row] if with_loss else []) + [ANY, ANY]
    out_shape = [jax.ShapeDtypeStruct((T, D_MODEL), F32)]
    out_specs = [row]
    if with_loss:
        out_shape.append(jax.ShapeDtypeStruct((8, 128), F32))
        out_specs.append(_full((8, 128)))
    out_shape += [jax.ShapeDtypeStruct((T, 2 * D_FF), BF16), jax.ShapeDtypeStruct((T, D_MODEL), F32)]
    out_specs += [_rows(tm, 2 * D_FF), row]
    args = (x, g_pre, g_post) + ((target,) if with_loss else ()) + (w_in, w_out)
    return _pallas(
        body, name=name, grid=(T // tm,), in_specs=in_specs, out_specs=out_specs, out_shape=out_shape,
        scratch_shapes=[pltpu.VMEM(w_in.shape, BF16), pltpu.VMEM(w_out.shape, BF16), pltpu.SemaphoreType.DMA((2,))],
        semantics=("arbitrary",), args=args, rider=rider)


def _ffn_bwd_out(dout, f, z, w_out, g_post, *, name):
    T = dout.shape[0]
    tm = FFN_ROW_TILE
    nt = T // tm

    def body(dout_ref, f_ref, z_ref, gpost_ref, wout_hbm, dz_ref, dgpost_ref, dwout_hbm, wout_v, dwout_v, sems):
        i = pl.program_id(0)

        @pl.when(i == 0)
        def _():
            _load_once([(wout_hbm, wout_v)], sems)
            dwout_v[...] = jnp.zeros_like(dwout_v)
            dgpost_ref[...] = jnp.zeros_like(dgpost_ref)

        dy = 0.5 * dout_ref[...]
        f = f_ref[...]
        r = _rms(f)
        fn = f * r
        dgpost_ref[...] += jnp.sum(dy * fn, axis=0, keepdims=True)
        df = _rms_bwd(dy * gpost_ref[...], fn, r).astype(BF16)
        for k in range(2):
            gate = z_ref[:, k * FF_BLK:(k + 1) * FF_BLK].astype(F32)
            up = z_ref[:, D_FF + k * FF_BLK:D_FF + (k + 1) * FF_BLK].astype(F32)
            sg = jax.nn.sigmoid(gate)
            silu = gate * sg
            dwout_v[k] += _dot_tn(silu * up, df)
            da = _dot_nt(df, wout_v[k])
            dz_ref[:, k * FF_BLK:(k + 1) * FF_BLK] = (da * up * (sg * (1.0 + gate * (1.0 - sg)))).astype(BF16)
            dz_ref[:, D_FF + k * FF_BLK:D_FF + (k + 1) * FF_BLK] = (da * silu).astype(BF16)

        @pl.when(i == nt - 1)
        def _():
            c = pltpu.make_async_copy(dwout_v, dwout_hbm, sems.at[0])
            c.start()
            c.wait()

    row = _rows(tm, D_MODEL)
    return pl.pallas_call(
        body, name=name, grid=(nt,),
        in_specs=[row, row, _rows(tm, 2 * D_FF), _full((1, D_MODEL)), ANY],
        out_specs=[_rows(tm, 2 * D_FF), _full((1, D_MODEL)), ANY],
        out_shape=[jax.ShapeDtypeStruct((T, 2 * D_FF), BF16), jax.ShapeDtypeStruct((1, D_MODEL), F32),
                   jax.ShapeDtypeStruct(w_out.shape, F32)],
        scratch_shapes=[pltpu.VMEM(w_out.shape, BF16), pltpu.VMEM(w_out.shape, F32), pltpu.SemaphoreType.DMA((1,))],
        compiler_params=_params("arbitrary"),
    )(dout, f, z, g_post, w_out)


def _norm_matmul_dw(dz, x, g, n_blocks, *, name, rider=None):
    T = x.shape[0]
    bw = dz.shape[1] // n_blocks
    tm = DW_ROW_TILE

    def body(dz_ref, x_ref, g_ref, dw_ref):
        @pl.when(pl.program_id(1) == 0)
        def _():
            dw_ref[...] = jnp.zeros_like(dw_ref)

        xv = x_ref[...]
        dw_ref[0] += _dot_tn(xv * _rms(xv) * g_ref[...], dz_ref[...])

    res = _pallas(
        body, name=name, grid=(n_blocks, T // tm),
        in_specs=[pl.BlockSpec((tm, bw), lambda j, t: (t, j)), pl.BlockSpec((tm, D_MODEL), lambda j, t: (t, 0)), _full((1, D_MODEL))],
        out_specs=[pl.BlockSpec((1, D_MODEL, bw), lambda j, t: (j, 0, 0))],
        out_shape=[jax.ShapeDtypeStruct((n_blocks, D_MODEL, bw), F32)], semantics=("parallel", "arbitrary"),
        args=(dz, x, g), rider=rider)
    return res[0] if rider is None else (res[0][0], res[1])


def _norm_matmul_dx(dz, x, dres, g, w, *, name, rider=None, part=(0, 1), into=None):
    T = x.shape[0]
    nb, _, bw = w.shape
    tm = FFN_ROW_TILE
    first, steps = round(part[0] * T) // tm, round((part[1] - part[0]) * T) // tm

    def body(dz_ref, x_ref, dres_ref, g_ref, w_hbm, *rest):
        dx_ref, dg_ref, w_v, sems = rest[-4:]

        @pl.when(pl.program_id(0) == 0)
        def _():
            _load_once([(w_hbm, w_v)], sems)
            dg_ref[...] = jnp.zeros_like(dg_ref)

        xv = x_ref[...]
        r = _rms(xv)
        xn = xv * r
        gv = g_ref[...]
        dh = jnp.zeros((tm, D_MODEL), F32)
        for j in range(nb):
            dh = dh + _dot_nt(dz_ref[:, j * bw:(j + 1) * bw], w_v[j])
        dg_ref[...] += jnp.sum(dh * xn, axis=0, keepdims=True)
        dx_ref[...] = _rms_bwd(dh * gv, xn, r) + dres_ref[...]

    rows = lambda width: pl.BlockSpec((tm, width), lambda i: (first + i, 0))
    row = rows(D_MODEL)
    return _pallas(
        body, name=name, grid=(steps,),
        in_specs=[rows(nb * bw), row, row, _full((1, D_MODEL)), ANY] + ([] if into is None else [ANY]),
        out_specs=[row, _full((1, D_MODEL))],
        out_shape=[jax.ShapeDtypeStruct((T, D_MODEL), F32), jax.ShapeDtypeStruct((1, D_MODEL), F32)],
        scratch_shapes=[pltpu.VMEM(w.shape, BF16), pltpu.SemaphoreType.DMA((1,))],
        semantics=("arbitrary",), args=(dz, x, dres, g, w) + (() if into is None else (into,)), rider=rider,
        aliases={} if into is None else {5: 0})


def _mix_in_fwd(x, g, w):
    T = x.shape[0]
    tm = ROW_TILE

    def body(x_ref, g_ref, w_ref, q_ref, k_ref, v_ref, u_ref):
        xv = x_ref[...]
        h = (xv * _rms(xv) * g_ref[...]).astype(BF16)
        for j, o_ref in enumerate((q_ref, k_ref, v_ref, u_ref)):
            o_ref[...] = _dot(h, w_ref[j])

    col = _rows(tm, ATTN_W)
    return pl.pallas_call(
        body, name="mix_in_fwd", grid=(T // tm,),
        in_specs=[_rows(tm, D_MODEL), _full((1, D_MODEL)), _full(w.shape)],
        out_specs=[col] * 4, out_shape=[jax.ShapeDtypeStruct((T, ATTN_W), F32)] * 4,
        compiler_params=_params("parallel"),
    )(x, g, w)


def _mix_out_fwd(x, attn, ssm, w, g):
    T = x.shape[0]
    tm = ROW_TILE

    def body(x_ref, a_ref, s_ref, w_ref, g_ref, o_ref, m_ref):
        mixed = _dot(a_ref[...], w_ref[0]) + _dot(s_ref[...], w_ref[1])
        m_ref[...] = mixed
        o_ref[...] = x_ref[...] + mixed * _rms(mixed) * g_ref[...]

    row, col = _rows(tm, D_MODEL), _rows(tm, ATTN_W)
    return pl.pallas_call(
        body, name="mix_out_fwd", grid=(T // tm,),
        in_specs=[row, col, col, _full(w.shape), _full((1, D_MODEL))],
        out_specs=[row, row], out_shape=[jax.ShapeDtypeStruct((T, D_MODEL), F32)] * 2,
        compiler_params=_params("parallel"),
    )(x, attn, ssm, w, g)


def _mix_out_bwd(dout, mixed, attn, ssm, w, g, rider=None):
    T = dout.shape[0]
    tm = ROW_TILE

    def body(dout_ref, m_ref, a_ref, s_ref, w_ref, g_ref, da_ref, ds_ref, dw_ref, dg_ref):
        @pl.when(pl.program_id(0) == 0)
        def _():
            dw_ref[...] = jnp.zeros_like(dw_ref)
            dg_ref[...] = jnp.zeros_like(dg_ref)

        dy = dout_ref[...]
        mixed = m_ref[...]
        r = _rms(mixed)
        mn = mixed * r
        dg_ref[...] += jnp.sum(dy * mn, axis=0, keepdims=True)
        dm = _rms_bwd(dy * g_ref[...], mn, r).astype(BF16)
        da_ref[...] = _dot_nt(dm, w_ref[0])
        ds_ref[...] = _dot_nt(dm, w_ref[1])
        dw_ref[0] += _dot_tn(a_ref[...], dm)
        dw_ref[1] += _dot_tn(s_ref[...], dm)

    row, col = _rows(tm, D_MODEL), _rows(tm, ATTN_W)
    return _pallas(
        body, name="mix_out_bwd", grid=(T // tm,),
        in_specs=[row, row, col, col, _full(w.shape), _full((1, D_MODEL))],
        out_specs=[col, col, _full(w.shape), _full((1, D_MODEL))],
        out_shape=[jax.ShapeDtypeStruct((T, ATTN_W), F32)] * 2
        + [jax.ShapeDtypeStruct(w.shape, F32), jax.ShapeDtypeStruct((1, D_MODEL), F32)],
        semantics=("arbitrary",), args=(dout, mixed, attn, ssm, w, g), rider=rider)


ATTN_TILING = {1: (ATTN_W, 1), 4: (2 * HEAD_DIM, 4), 16: (2 * HEAD_DIM, 4)}


def _class_rows(d, r):
    return (pl.ds(r, QBLK, stride=d), slice(None)) if d > 1 else (slice(None), slice(None))


def _for_class_groups(d, group, fn):
    if d == group:
        fn(0)
    else:
        lax.fori_loop(0, d // group, lambda n, carry: (fn(n * group), carry)[1], 0)


def _block_slopes(lanes, lane_block):
    heads = lanes // HEAD_DIM
    first = lane_block * heads
    return [jnp.exp2(-jnp.full((1, 1), first + hh + 1, jnp.int32).astype(F32)) for hh in range(heads)]


def _attn_specs(d, nb, lanes):
    blk = (QBLK * d, lanes)
    cur = pl.BlockSpec(blk, lambda j, lb: (j, lb))
    prev = pl.BlockSpec(blk, lambda j, lb: (jnp.maximum(j - 1, 0), lb))
    nxt = pl.BlockSpec(blk, lambda j, lb: (jnp.minimum(j + 1, nb - 1), lb))
    return cur, prev, nxt


def _attn_branch_fwd(q, k, v, d):
    T = q.shape[0]
    nb = T // (d * QBLK)
    lanes, group = ATTN_TILING[d]
    scale = HEAD_DIM ** -0.5

    def body(q_ref, kc_ref, kp_ref, vc_ref, vp_ref, o_ref, l_ref, q_s, kk_s, vv_s, o_s, l_s):
        j = pl.program_id(0)
        qi = lax.broadcasted_iota(jnp.int32, (QBLK, 2 * QBLK), 0)
        ci = lax.broadcasted_iota(jnp.int32, (QBLK, 2 * QBLK), 1)
        steps = QBLK + qi - ci
        valid = (steps >= 0) & (steps <= N_BACK) & ((ci >= QBLK) | (j > 0))
        dist = (steps * d).astype(F32)
        slopes = _block_slopes(lanes, pl.program_id(1))

        def classes(first):
            for n in range(group):
                rows = _class_rows(d, first + n)
                q_s[n] = q_ref[rows]
                kk_s[n, :QBLK], kk_s[n, QBLK:] = kp_ref[rows], kc_ref[rows]
                vv_s[n, :QBLK], vv_s[n, QBLK:] = vp_ref[rows], vc_ref[rows]
            for n in range(group):
                for hh in range(lanes // HEAD_DIM):
                    sl = slice(hh * HEAD_DIM, (hh + 1) * HEAD_DIM)
                    s = _dot_nt(q_s[n, :, sl], kk_s[n, :, sl]) * scale - slopes[hh] * dist
                    s = jnp.where(valid, s, NEG)
                    m = jnp.max(s, axis=-1, keepdims=True)
                    p = jnp.exp(s - m)
                    den = jnp.sum(p, axis=-1, keepdims=True)
                    o_s[n, :, sl] = _dot(p, vv_s[n, :, sl]) / den
                    l_s[n, :, sl] = jnp.broadcast_to(m + jnp.log(den), (QBLK, HEAD_DIM))
            for n in range(group):
                rows = _class_rows(d, first + n)
                o_ref[rows] = o_s[n]
                l_ref[rows] = l_s[n]

        _for_class_groups(d, group, classes)

    cur, prev, _ = _attn_specs(d, nb, lanes)
    shape = jax.ShapeDtypeStruct((T, ATTN_W), F32)
    one, two = pltpu.VMEM((group, QBLK, lanes), F32), pltpu.VMEM((group, 2 * QBLK, lanes), F32)
    return pl.pallas_call(
        body, name=f"attn_fwd_d{d}", grid=(nb, ATTN_W // lanes),
        in_specs=[cur, cur, prev, cur, prev], out_specs=[cur, cur], out_shape=[shape, shape],
        scratch_shapes=[one, two, two, one, one], compiler_params=_params("parallel", "parallel"),
    )(q, k, k, v, v)


def _attn_merge(outs, lses):
    T = outs[0].shape[0]
    tm = 512

    def body(o1, o2, o3, l1, l2, l3, a_ref, lse_ref):
        ls = [l1[...], l2[...], l3[...]]
        m = jnp.maximum(jnp.maximum(ls[0], ls[1]), ls[2])
        lse = m + jnp.log(jnp.exp(ls[0] - m) + jnp.exp(ls[1] - m) + jnp.exp(ls[2] - m))
        lse_ref[...] = lse
        a_ref[...] = jnp.exp(ls[0] - lse) * o1[...] + jnp.exp(ls[1] - lse) * o2[...] + jnp.exp(ls[2] - lse) * o3[...]

    col = _rows(tm, ATTN_W)
    return pl.pallas_call(
        body, name="attn_merge", grid=(T // tm,), in_specs=[col] * 6, out_specs=[col, col],
        out_shape=[jax.ShapeDtypeStruct((T, ATTN_W), F32)] * 2, compiler_params=_params("parallel"),
    )(*outs, *lses)


def _attn_branch_bwd(q, k, v, o, lse, do, d, rider=None):
    T = q.shape[0]
    nb = T // (d * QBLK)
    lanes, group = ATTN_TILING[d]
    scale = HEAD_DIM ** -0.5

    def body(q_ref, kc_ref, kp_ref, vc_ref, vp_ref, o_ref, l_ref, do_ref, dq_ref, dk_ref, dv_ref,
             q_s, o_s, l_s, do_s, kk_s, vv_s, dq_s, dk_s, dv_s, ck_s, cv_s):
        j = pl.program_id(1)

        @pl.when(j == 0)
        def _():
            ck_s[...] = jnp.zeros_like(ck_s)
            cv_s[...] = jnp.zeros_like(cv_s)

        qi = lax.broadcasted_iota(jnp.int32, (QBLK, 2 * QBLK), 0)
        ci = lax.broadcasted_iota(jnp.int32, (QBLK, 2 * QBLK), 1)
        steps = QBLK + qi - ci
        valid = (steps >= 0) & (steps <= N_BACK) & ((ci >= QBLK) | (j > 0))
        dist = (steps * d).astype(F32)
        lo, hi = slice(0, QBLK), slice(QBLK, 2 * QBLK)
        slopes = _block_slopes(lanes, pl.program_id(0))

        def classes(first):
            for n in range(group):
                rows = _class_rows(d, first + n)
                q_s[n], o_s[n], l_s[n], do_s[n] = q_ref[rows], o_ref[rows], l_ref[rows], do_ref[rows]
                kk_s[n, lo], kk_s[n, hi] = kp_ref[rows], kc_ref[rows]
                vv_s[n, lo], vv_s[n, hi] = vp_ref[rows], vc_ref[rows]
            for n in range(group):
                for hh in range(lanes // HEAD_DIM):
                    sl = slice(hh * HEAD_DIM, (hh + 1) * HEAD_DIM)
                    qh, doh, kk, vv = q_s[n, :, sl], do_s[n, :, sl], kk_s[n, :, sl], vv_s[n, :, sl]
                    delta = jnp.sum(doh * o_s[n, :, sl], axis=-1, keepdims=True)
                    s = jnp.where(valid, _dot_nt(qh, kk) * scale - slopes[hh] * dist, NEG)
                    p = jnp.exp(s - l_s[n, :, hh * HEAD_DIM:hh * HEAD_DIM + 1])
                    ds = p * (_dot_nt(doh, vv) - delta)
                    dq_s[n, :, sl] = _dot(ds, kk) * scale
                    dkk = _dot_tn(ds, qh) * scale
                    dvv = _dot_tn(p, doh)
                    dk_s[n, :, sl] = ck_s[first + n, :, sl] + dkk[lo]
                    dv_s[n, :, sl] = cv_s[first + n, :, sl] + dvv[lo]
                    ck_s[first + n, :, sl] = dkk[hi]
                    cv_s[first + n, :, sl] = dvv[hi]
            for n in range(group):
                rows = _class_rows(d, first + n)
                dq_ref[rows] = dq_s[n]
                dk_ref[rows] = dk_s[n]
                dv_ref[rows] = dv_s[n]

        @pl.when(j < nb)
        def _():
            _for_class_groups(d, group, classes)

        @pl.when(j == nb)
        def _():
            for r in range(d):
                dk_ref[_class_rows(d, r)] = ck_s[r]
                dv_ref[_class_rows(d, r)] = cv_s[r]

    blk = (QBLK * d, lanes)
    here = lambda j: jnp.minimum(j, nb - 1)
    cur = pl.BlockSpec(blk, lambda lb, j: (here(j), lb))
    prev = pl.BlockSpec(blk, lambda lb, j: (jnp.maximum(here(j) - 1, 0), lb))
    behind = pl.BlockSpec(blk, lambda lb, j: (jnp.maximum(j - 1, 0), lb))
    shape = jax.ShapeDtypeStruct((T, ATTN_W), F32)
    one, two = pltpu.VMEM((group, QBLK, lanes), F32), pltpu.VMEM((group, 2 * QBLK, lanes), F32)
    carry = pltpu.VMEM((d, QBLK, lanes), F32)
    return _pallas(
        body, name=f"attn_bwd_d{d}", grid=(ATTN_W // lanes, nb + 1),
        in_specs=[cur, cur, prev, cur, prev, cur, cur, cur], out_specs=[cur, behind, behind], out_shape=[shape] * 3,
        scratch_shapes=[one] * 4 + [two] * 2 + [one] * 3 + [carry] * 2,
        semantics=("parallel", "arbitrary"), args=(q, k, k, v, v, o, lse, do), rider=rider)


def _dproj_merge(dqs, dks, dvs, du):
    T = du.shape[0]
    tm = 512

    def body(*refs):
        o_ref = refs[-1]
        for part in range(3):
            a, b, c = refs[3 * part:3 * part + 3]
            o_ref[:, part * ATTN_W:(part + 1) * ATTN_W] = (a[...] + b[...] + c[...]).astype(BF16)
        o_ref[:, 3 * ATTN_W:] = refs[9][...].astype(BF16)

    col = _rows(tm, ATTN_W)
    return pl.pallas_call(
        body, name="dproj_merge", grid=(T // tm,), in_specs=[col] * 10, out_specs=_rows(tm, 4 * ATTN_W),
        out_shape=jax.ShapeDtypeStruct((T, 4 * ATTN_W), BF16), compiler_params=_params("parallel"),
    )(*dqs, *dks, *dvs, du)


def _attention_fwd(q, k, v):
    res = [_attn_branch_fwd(q, k, v, d) for d in DILATIONS]
    return _attn_merge([r[0] for r in res], [r[1] for r in res])


SCAN_ROWS = 8
SCAN_LANES = 512
SSM_CHUNK = 512
SSM_CHUNK_BWD = 256
SSM_HALVES = tuple((slice(h * SSM_W // 2, (h + 1) * SSM_W // 2), slice(h * N_STATE // 2, (h + 1) * N_STATE // 2)) for h in range(2))


def _cmul(ar, ai, br, bi):
    return ar * br - ai * bi, ar * bi + ai * br


def _ssm_discretize(a_re, a_im, log_dt, b_re, b_im):
    def body(ar_ref, ai_ref, ldt_ref, br_ref, bi_ref, abr_ref, abi_ref, er_ref, ei_ref, bbr_ref, bbi_ref, pr_ref, pi_ref):
        ar, ai = ar_ref[...], ai_ref[...]
        dt = jnp.exp(ldt_ref[...])
        n = lax.broadcasted_iota(jnp.int32, (1, SCAN_ROWS), 1).astype(F32) + 1.0
        mag, ang = jnp.exp(dt * ar), dt * ai
        abr, abi = mag * jnp.cos(ang), mag * jnp.sin(ang)
        abr_ref[...], abi_ref[...] = abr, abi
        pr_ref[...] = jnp.exp(dt * ar * n) * jnp.cos(ang * n)
        pi_ref[...] = jnp.exp(dt * ar * n) * jnp.sin(ang * n)
        den = ar * ar + ai * ai
        er = ((abr - 1.0) * ar + abi * ai) / den
        ei = (abi * ar - (abr - 1.0) * ai) / den
        er_ref[...], ei_ref[...] = er, ei
        bbr_ref[...], bbi_ref[...] = _cmul(er, ei, br_ref[...], bi_ref[...])

    col = jax.ShapeDtypeStruct((N_STATE, 1), F32)
    mat = jax.ShapeDtypeStruct((N_STATE, GROUP_CH), F32)
    pw = jax.ShapeDtypeStruct((N_STATE, SCAN_ROWS), F32)
    return pl.pallas_call(body, name="ssm_discretize", out_shape=[col] * 4 + [mat] * 2 + [pw] * 2)(
        a_re, a_im, log_dt, b_re, b_im)


def _ssm_discretize_bwd(a_re, a_im, log_dt, b_re, b_im, ab_re, ab_im, e_re, e_im, dab_re, dab_im, dbb_re, dbb_im):
    def body(ar_ref, ai_ref, ldt_ref, br_ref, bi_ref, abr_ref, abi_ref, er_ref, ei_ref, dabr_ref, dabi_ref,
             dbbr_ref, dbbi_ref, dar_ref, dai_ref, ddt_ref, dbr_ref, dbi_ref):
        ar, ai, dt = ar_ref[...], ai_ref[...], jnp.exp(ldt_ref[...])
        er, ei = er_ref[...], ei_ref[...]
        gbr, gbi = dbbr_ref[...], dbbi_ref[...]
        dbr_ref[...], dbi_ref[...] = _cmul(er, -ei, gbr, gbi)
        br, bi = br_ref[...], bi_ref[...]
        der = jnp.sum(br * gbr + bi * gbi, axis=-1, keepdims=True)
        dei = jnp.sum(br * gbi - bi * gbr, axis=-1, keepdims=True)
        den = ar * ar + ai * ai
        inv_r, inv_i = ar / den, -ai / den
        t_r, t_i = _cmul(der, dei, inv_r, -inv_i)
        gab_r, gab_i = dabr_ref[...] + t_r, dabi_ref[...] + t_i
        q_r, q_i = _cmul(er, ei, inv_r, inv_i)
        dl_r, dl_i = _cmul(der, dei, q_r, -q_i)
        dl_r, dl_i = -dl_r, -dl_i
        gw_r, gw_i = _cmul(gab_r, gab_i, abr_ref[...], -abi_ref[...])
        dar_ref[...] = dl_r + dt * gw_r
        dai_ref[...] = dl_i + dt * gw_i
        ddt_ref[...] = (gw_r * ar + gw_i * ai) * dt

    col = jax.ShapeDtypeStruct((N_STATE, 1), F32)
    mat = jax.ShapeDtypeStruct((N_STATE, GROUP_CH), F32)
    return pl.pallas_call(body, name="ssm_discretize_bwd", out_shape=[col] * 3 + [mat] * 2)(
        a_re, a_im, log_dt, b_re, b_im, ab_re, ab_im, e_re, e_im, dab_re, dab_im, dbb_re, dbb_im)


def _scan_tables(p_re, p_im, reverse):
    pr, pi = p_re.T, p_im.T
    if reverse:
        pi = -pi
    row = jnp.arange(SCAN_ROWS)[:, None]
    level = lambda t, s: jnp.where((row < SCAN_ROWS - s) if reverse else (row >= s), t[s - 1][None, :], 0.0)
    carry = (pr[::-1], pi[::-1]) if reverse else (pr, pi)
    return jnp.stack([level(pr, 1), level(pi, 1), level(pr, 2), level(pi, 2), level(pr, 4), level(pi, 4), carry[0], carry[1]])


def _scan_group(xr, xi, tab_ref, ls, carry_r, carry_i, reverse):
    for n, s in enumerate((1, 2, 4)):
        shift = SCAN_ROWS - s if reverse else s
        mr, mi = _cmul(tab_ref[2 * n, :, ls], tab_ref[2 * n + 1, :, ls], pltpu.roll(xr, shift, 0), pltpu.roll(xi, shift, 0))
        xr, xi = xr + mr, xi + mi
    mr, mi = _cmul(tab_ref[6, :, ls], tab_ref[7, :, ls], carry_r, carry_i)
    return xr + mr, xi + mi


def _gelu(y):
    t = jnp.tanh(GELU_C * (y + 0.044715 * y * y * y))
    return 0.5 * y * (1.0 + t), t


def _ssm_fwd(u, tab, bd_re, bd_im, cd_re, cd_im, d_skip, w_glu, b_glu):
    T = u.shape[0]
    tc = SSM_CHUNK

    def body(u_ref, tab_ref, bdr_ref, bdi_ref, cdr_ref, cdi_ref, dsk_ref, wg_ref, bg_ref,
             sr_ref, si_ref, yp_ref, o_ref, car_r, car_i):
        @pl.when(pl.program_id(0) == 0)
        def _():
            car_r[...] = jnp.zeros_like(car_r)
            car_i[...] = jnp.zeros_like(car_i)

        uv = u_ref[...]
        for cs, ss in SSM_HALVES:
            sr_ref[:, ss] = _dot(uv[:, cs], bdr_ref[cs, ss])
            si_ref[:, ss] = _dot(uv[:, cs], bdi_ref[cs, ss])
        for lb in range(N_STATE // SCAN_LANES):
            ls = pl.ds(lb * SCAN_LANES, SCAN_LANES)

            def step(g, carry):
                rows = pl.ds(pl.multiple_of(g * SCAN_ROWS, SCAN_ROWS), SCAN_ROWS)
                xr, xi = _scan_group(sr_ref[rows, ls], si_ref[rows, ls], tab_ref, ls, carry[0], carry[1], False)
                sr_ref[rows, ls] = xr
                si_ref[rows, ls] = xi
                last = slice(SCAN_ROWS - 1, SCAN_ROWS)
                return (jnp.broadcast_to(xr[last], xr.shape), jnp.broadcast_to(xi[last], xi.shape))

            cr, ci = lax.fori_loop(0, tc // SCAN_ROWS, step, (car_r[:, ls], car_i[:, ls]))
            car_r[:, ls] = cr
            car_i[:, ls] = ci
        y = jnp.concatenate([_dot(sr_ref[:, ss], cdr_ref[ss, cs]) - _dot(si_ref[:, ss], cdi_ref[ss, cs])
                             for cs, ss in SSM_HALVES], axis=1) + dsk_ref[...] * uv
        yp_ref[...] = y
        gy, _ = _gelu(y)
        o_ref[...] = gy * jax.nn.sigmoid(_dot(gy, wg_ref[...]) + bg_ref[...])

    col, st = _rows(tc, SSM_W), _rows(tc, N_STATE)
    vec = _full((1, SSM_W))
    return pl.pallas_call(
        body, name="ssm_fwd", grid=(T // tc,),
        in_specs=[col, _full(tab.shape), _full(bd_re.shape), _full(bd_im.shape), _full(cd_re.shape), _full(cd_im.shape),
                  vec, _full(w_glu.shape), vec],
        out_specs=[st, st, col, col],
        out_shape=[jax.ShapeDtypeStruct((T, N_STATE), F32)] * 2 + [jax.ShapeDtypeStruct((T, SSM_W), F32)] * 2,
        scratch_shapes=[pltpu.VMEM((SCAN_ROWS, N_STATE), F32)] * 2,
        compiler_params=_params("arbitrary"),
    )(u, tab, bd_re, bd_im, cd_re, cd_im, d_skip, w_glu, b_glu)


def _ssm_bwd(dout, u, yp, s_re, s_im, tab, bd_re, bd_im, cd_re, cd_im, d_skip, w_glu, b_glu, rider=None):
    T = u.shape[0]
    tc = SSM_CHUNK_BWD
    nt = T // tc
    rows_per_chunk = tc // SCAN_ROWS

    def body(do_ref, u_ref, yp_ref, sr_ref, si_ref, pr_ref, pi_ref, tab_ref, dsk_ref, wg_ref, bg_ref,
             bdr_hbm, bdi_hbm, cdr_hbm, cdi_hbm,
             du_ref, dsk_out, dbg_out, dwg_out, da_out, dbdr_hbm, dbdi_hbm, dcdr_hbm, dcdi_hbm,
             bdr_v, bdi_v, cdr_v, cdi_v, dbdr_v, dbdi_v, dcdr_v, dcdi_v, gr_v, gi_v, car_r, car_i, sems):
        i = pl.program_id(0)

        @pl.when(i == 0)
        def _():
            _load_once([(bdr_hbm, bdr_v), (bdi_hbm, bdi_v), (cdr_hbm, cdr_v), (cdi_hbm, cdi_v)], sems)
            for ref in (dbdr_v, dbdi_v, dcdr_v, dcdi_v, car_r, car_i, dsk_out, dbg_out, dwg_out, da_out):
                ref[...] = jnp.zeros_like(ref)

        uv, y, dout_v = u_ref[...], yp_ref[...], do_ref[...]
        gy, t = _gelu(y)
        sg = jax.nn.sigmoid(_dot(gy, wg_ref[...]) + bg_ref[...])
        dzg = dout_v * gy * sg * (1.0 - sg)
        dgy = dout_v * sg + _dot_nt(dzg, wg_ref[...])
        dwg_out[...] += _dot_tn(gy, dzg)
        dbg_out[...] += jnp.sum(dzg, axis=0, keepdims=True)
        dy = dgy * (0.5 * (1.0 + t) + 0.5 * y * (1.0 - t * t) * GELU_C * (1.0 + 3 * 0.044715 * y * y))
        dsk_out[...] += jnp.sum(dy * uv, axis=0, keepdims=True)

        for cs, ss in SSM_HALVES:
            gr_v[:, ss] = _dot_nt(dy[:, cs], cdr_v[ss, cs])
            gi_v[:, ss] = -_dot_nt(dy[:, cs], cdi_v[ss, cs])
            dcdr_v[ss, cs] += _dot_tn(sr_ref[:, ss], dy[:, cs])
            dcdi_v[ss, cs] -= _dot_tn(si_ref[:, ss], dy[:, cs])

        row = lax.broadcasted_iota(jnp.int32, (SCAN_ROWS, SCAN_LANES), 0)
        first_chunk = i == nt - 1
        for lb in range(N_STATE // SCAN_LANES):
            ls = pl.ds(lb * SCAN_LANES, SCAN_LANES)

            def step(n, carry):
                g = rows_per_chunk - 1 - n
                rows = pl.ds(pl.multiple_of(g * SCAN_ROWS, SCAN_ROWS), SCAN_ROWS)
                before = pl.ds(pl.multiple_of(jnp.maximum(g - 1, 0) * SCAN_ROWS, SCAN_ROWS), SCAN_ROWS)
                xr, xi = _scan_group(gr_v[rows, ls], gi_v[rows, ls], tab_ref, ls, carry[0], carry[1], True)
                gr_v[rows, ls] = xr
                gi_v[rows, ls] = xi
                last = slice(SCAN_ROWS - 1, SCAN_ROWS)
                edge_r = jnp.where(g > 0, sr_ref[before, ls][last], jnp.where(first_chunk, 0.0, pr_ref[:, ls][last]))
                edge_i = jnp.where(g > 0, si_ref[before, ls][last], jnp.where(first_chunk, 0.0, pi_ref[:, ls][last]))
                spr = jnp.where(row >= 1, pltpu.roll(sr_ref[rows, ls], 1, 0), edge_r)
                spi = jnp.where(row >= 1, pltpu.roll(si_ref[rows, ls], 1, 0), edge_i)
                first = slice(0, 1)
                return (jnp.broadcast_to(xr[first], xr.shape), jnp.broadcast_to(xi[first], xi.shape),
                        carry[2] + xr * spr + xi * spi, carry[3] + xi * spr - xr * spi)

            zero = jnp.zeros((SCAN_ROWS, SCAN_LANES), F32)
            cr, ci, dar, dai = lax.fori_loop(0, rows_per_chunk, step, (car_r[:, ls], car_i[:, ls], zero, zero))
            car_r[:, ls] = cr
            car_i[:, ls] = ci
            da_out[0, :, ls] += dar
            da_out[1, :, ls] += dai

        du_ref[...] = dsk_ref[...] * dy + jnp.concatenate(
            [_dot_nt(gr_v[:, ss], bdr_v[cs, ss]) + _dot_nt(gi_v[:, ss], bdi_v[cs, ss]) for cs, ss in SSM_HALVES], axis=1)
        for cs, ss in SSM_HALVES:
            dbdr_v[cs, ss] += _dot_tn(uv[:, cs], gr_v[:, ss])
            dbdi_v[cs, ss] += _dot_tn(uv[:, cs], gi_v[:, ss])

        @pl.when(i == nt - 1)
        def _():
            outs = [(dbdr_v, dbdr_hbm), (dbdi_v, dbdi_hbm), (dcdr_v, dcdr_hbm), (dcdi_v, dcdi_hbm)]
            copies = [pltpu.make_async_copy(src, dst, sems.at[k]) for k, (src, dst) in enumerate(outs)]
            for c in copies:
                c.start()
            for c in copies:
                c.wait()

    rev = lambda i: (nt - 1 - i, 0)
    col = pl.BlockSpec((tc, SSM_W), rev)
    st = pl.BlockSpec((tc, N_STATE), rev)
    st_before = pl.BlockSpec((SCAN_ROWS, N_STATE), lambda i: (jnp.maximum((nt - 1 - i) * rows_per_chunk - 1, 0), 0))
    vec = _full((1, SSM_W))
    bd = jax.ShapeDtypeStruct(bd_re.shape, F32)
    cd = jax.ShapeDtypeStruct(cd_re.shape, F32)
    return _pallas(
        body, name="ssm_bwd", grid=(nt,),
        in_specs=[col, col, col, st, st, st_before, st_before, _full(tab.shape), vec, _full(w_glu.shape), vec,
                  ANY, ANY, ANY, ANY],
        out_specs=[col, vec, vec, _full(w_glu.shape), _full((2, SCAN_ROWS, N_STATE)), ANY, ANY, ANY, ANY],
        out_shape=[jax.ShapeDtypeStruct((T, SSM_W), F32), jax.ShapeDtypeStruct((1, SSM_W), F32),
                   jax.ShapeDtypeStruct((1, SSM_W), F32), jax.ShapeDtypeStruct(w_glu.shape, F32),
                   jax.ShapeDtypeStruct((2, SCAN_ROWS, N_STATE), F32), bd, bd, cd, cd],
        scratch_shapes=[pltpu.VMEM(bd_re.shape, BF16)] * 2 + [pltpu.VMEM(cd_re.shape, BF16)] * 2
        + [pltpu.VMEM(bd_re.shape, F32)] * 2 + [pltpu.VMEM(cd_re.shape, F32)] * 2
        + [pltpu.VMEM((tc, N_STATE), F32)] * 2 + [pltpu.VMEM((SCAN_ROWS, N_STATE), F32)] * 2
        + [pltpu.SemaphoreType.DMA((4,))],
        semantics=("arbitrary",), rider=rider,
        args=(dout, u, yp, s_re, s_im, s_re, s_im, tab, d_skip, w_glu, b_glu, bd_re, bd_im, cd_re, cd_im))


def _block_diag(t):
    g, a, b = t.shape
    eye = jnp.eye(N_GROUPS, dtype=t.dtype)
    return (t[:, :, None, :] * eye[:, None, :, None]).reshape(g * a, g * b)


def _diag_blocks(m, a, b):
    eye = jnp.eye(N_GROUPS, dtype=m.dtype)
    return jnp.sum(m.reshape(N_GROUPS, a, N_GROUPS, b) * eye[:, None, :, None], axis=2)


def _ssm_prepare(a_re, a_im, log_dt, b_re, b_im, c_re, c_im):
    col = lambda t: t.reshape(N_STATE, 1)
    ldt = jnp.broadcast_to(log_dt.reshape(N_GROUPS, 1), (N_GROUPS, STATE)).reshape(N_STATE, 1)
    b2r, b2i = b_re.reshape(N_STATE, GROUP_CH), b_im.reshape(N_STATE, GROUP_CH)
    ab_r, ab_i, e_r, e_i, bb_r, bb_i, p_r, p_i = _ssm_discretize(col(a_re), col(a_im), ldt, b2r, b2i)
    bd = [_block_diag(jnp.swapaxes(t.reshape(N_GROUPS, STATE, GROUP_CH), 1, 2)).astype(BF16) for t in (bb_r, bb_i)]
    cd = [_block_diag(jnp.swapaxes(t.reshape(N_GROUPS, GROUP_CH, STATE), 1, 2)).astype(BF16) for t in (c_re, c_im)]
    saved = dict(a_re=col(a_re), a_im=col(a_im), log_dt=ldt, b_re=b2r, b_im=b2i, ab_re=ab_r, ab_im=ab_i, e_re=e_r, e_im=e_i)
    return _scan_tables(p_r, p_i, False), _scan_tables(p_r, p_i, True), bd, cd, saved


BIG = ("ffn1_w_in", "ffn1_w_out", "w_mix_in", "w_glu", "w_mix_out", "ffn2_w_in", "ffn2_w_out")
SMALL = ("ffn1_pre_g", "ffn1_post_g", "mix_pre_g", "a_re", "a_im", "log_dt", "b_re", "b_im", "c_re", "c_im",
         "d_skip", "b_glu", "mix_post_g", "ffn2_pre_g", "ffn2_post_g")


FIRST = ("ffn1_w_in", "ffn1_w_out")
REST = ("w_mix_in", "w_glu", "w_mix_out", "ffn2_w_in", "ffn2_w_out")
LATE = ("w_mix_in", "w_glu", "w_mix_out", "ffn1_w_out")
SHARD_SHAPE = {"ffn1_w_in": (D_MODEL, FF_BLK), "ffn2_w_in": (D_MODEL, FF_BLK), "ffn1_w_out": (D_FF // N_CHIPS, D_MODEL),
               "ffn2_w_out": (D_FF // N_CHIPS, D_MODEL), "w_mix_in": (D_MODEL, ATTN_W), "w_glu": (SSM_W // N_CHIPS, SSM_W),
               "w_mix_out": (2 * ATTN_W // N_CHIPS, D_MODEL)}


class _Reduction:
    def __init__(self, names, grads, place):
        self.names, self.local, self.place = list(names), list(grads), place

    def exchange(self):
        return _pair_exchange(self.local)

    def first(self, got):
        both = [_pair_sum(a, b, self.place["core"], name=f"pair_sum_{n}") for n, a, b in zip(self.names, self.local, got)]
        self.pair = [f32 for f32, _ in both]
        return _reduce_first([wire for _, wire in both])

    def second(self, got):
        both = [_reduce_sum_first(a, b, self.place["sel_first"], name=f"sum_first_{n}")
                for n, a, b in zip(self.names, self.pair, got)]
        self.sums = [f32 for f32, _ in both]
        return _reduce_second([wire for _, wire in both])

    def swap(self, got):
        self.halves = [_reduce_sum_second(a, b, self.place["sel_second"], name=f"sum_second_{n}").reshape(2 * a.shape[2], a.shape[3])
                       for n, a, b in zip(self.names, self.sums, got)]
        return _pair_swap(self.halves)

    def done(self, got):
        return {n: (mine, theirs) for n, mine, theirs in zip(self.names, self.halves, got)}


def _local_step(x, target, p, w, place=None):
    vec = lambda t: t.reshape(1, -1)
    w1_in, w1_out = w["ffn1_w_in"], w["ffn1_w_out"].reshape(2, FF_BLK, D_MODEL)
    blocks = lambda n, t: t.reshape((N_CHIPS,) + SHARD_SHAPE[n])

    ffn1 = functools.partial(_ffn_fwd, x, vec(p["ffn1_pre_g"]), w1_in, w1_out, vec(p["ffn1_post_g"]), None, name="ffn1_fwd")
    if place is None:
        x1, z1, f1 = ffn1()
    else:
        (x1, z1, f1), rest = ffn1(rider=_gather_weights([w[n] for n in REST]))
        w = dict(w, **dict(zip(REST, rest)))
    w2_in, w2_out = w["ffn2_w_in"], w["ffn2_w_out"].reshape(2, FF_BLK, D_MODEL)
    w_mi, w_glu, w_mo = w["w_mix_in"], w["w_glu"].reshape(SSM_W, SSM_W), w["w_mix_out"].reshape(2, ATTN_W, D_MODEL)
    q, k, v, u = _mix_in_fwd(x1, vec(p["mix_pre_g"]), w_mi)
    attn, lse = _attention_fwd(q, k, v)
    tab_f, tab_b, bd, cd, sv = _ssm_prepare(p["a_re"], p["a_im"], p["log_dt"], p["b_re"], p["b_im"], p["c_re"], p["c_im"])
    ssm_args = (bd[0], bd[1], cd[0], cd[1], vec(p["d_skip"]), w_glu, vec(p["b_glu"]))
    s_re, s_im, yp, ssm = _ssm_fwd(u, tab_f, *ssm_args)
    x2, mixed = _mix_out_fwd(x1, attn, ssm, w_mo, vec(p["mix_post_g"]))
    dx3, loss_rows, z2, f2 = _ffn_fwd(x2, vec(p["ffn2_pre_g"]), w2_in, w2_out, vec(p["ffn2_post_g"]), target, name="ffn2_fwd")

    g = {}
    ride = (lambda call, exchange: call(rider=exchange)) if place else (lambda call, exchange: (call(), None))
    dz2, g["ffn2_post_g"], dw2_out = _ffn_bwd_out(dx3, f2, z2, w2_out, vec(p["ffn2_post_g"]), name="ffn2_bwd_out")
    dw2_in = _norm_matmul_dw(dz2, x2, vec(p["ffn2_pre_g"]), N_CHIPS, name="ffn2_bwd_dw")
    early = _Reduction(("ffn2_w_in", "ffn2_w_out"), [dw2_in, blocks("ffn2_w_out", dw2_out)], place) if place else None
    (dx2, g["ffn2_pre_g"]), got = ride(
        functools.partial(_norm_matmul_dx, dz2, x2, dx3, vec(p["ffn2_pre_g"]), w2_in, name="ffn2_bwd_dx"), early and early.exchange())
    dattn, dssm, dw_mo, g["mix_post_g"] = _mix_out_bwd(dx2, mixed, attn, ssm, w_mo, vec(p["mix_post_g"]))
    (du, g["d_skip"], g["b_glu"], dw_glu, da, dbd_re, dbd_im, dcd_re, dcd_im), got = ride(
        functools.partial(_ssm_bwd, dssm, u, yp, s_re, s_im, tab_b, *ssm_args), early and early.first(got))
    branch = lambda d: functools.partial(_attn_branch_bwd, q, k, v, attn, lse, dattn, d)
    parts = [None] * 3
    parts[0], got = ride(branch(DILATIONS[0]), early and early.second(got))
    parts[1], early_theirs = ride(branch(DILATIONS[1]), early and early.swap(got))
    parts[2] = branch(DILATIONS[2])()
    dproj = _dproj_merge([r[0] for r in parts], [r[1] for r in parts], [r[2] for r in parts], du)
    dw_mi = _norm_matmul_dw(dproj, x1, vec(p["mix_pre_g"]), N_CHIPS, name="mix_bwd_dw")
    dx1, g["mix_pre_g"] = _norm_matmul_dx(dproj, x1, dx2, vec(p["mix_pre_g"]), w_mi, name="mix_bwd_dx")
    dz1, g["ffn1_post_g"], dw1_out = _ffn_bwd_out(dx1, f1, z1, w1_out, vec(p["ffn1_post_g"]), name="ffn1_bwd_out")
    big = {"ffn2_w_in": dw2_in, "ffn2_w_out": dw2_out, "w_mix_in": dw_mi, "w_glu": dw_glu, "w_mix_out": dw_mo, "ffn1_w_out": dw1_out}
    late = _Reduction(LATE, [blocks(n, big[n]) for n in LATE], place) if place else None
    big["ffn1_w_in"], got = ride(
        functools.partial(_norm_matmul_dw, dz1, x, vec(p["ffn1_pre_g"]), N_CHIPS, name="ffn1_bwd_dw"), late and late.exchange())
    dx_call = functools.partial(_norm_matmul_dx, dz1, x, dx1, vec(p["ffn1_pre_g"]), w1_in)
    if place:
        last = _Reduction(("ffn1_w_in",), [big["ffn1_w_in"]], place)
        n_late = len(LATE)
        (dx_part, dg_a), got = dx_call(name="ffn1_bwd_dx_a", part=(0, 0.375), rider=_merged(late.first(got), last.exchange()))
        (dx_part, dg_b), got = dx_call(name="ffn1_bwd_dx_b", part=(0.375, 0.75), into=dx_part,
                                       rider=_merged(late.second(got[:n_late]), last.first(got[n_late:])))
        grad_x, dg_c = dx_call(name="ffn1_bwd_dx_c", part=(0.75, 1), into=dx_part)
        g["ffn1_pre_g"] = dg_a + dg_b + dg_c
        got = _merged(late.swap(got[:n_late]), last.second(got[n_late:])).run("tail_second")
        g.update(late.done(got[:n_late]))
        g.update(last.done(last.swap(got[n_late:]).run("tail_swap")))
        g.update(early.done(early_theirs))
    else:
        grad_x, g["ffn1_pre_g"] = dx_call(name="ffn1_bwd_dx")
        g.update({n: blocks(n, big[n]) for n in BIG})

    g["c_re"], g["c_im"] = (jnp.swapaxes(_diag_blocks(m, STATE, GROUP_CH), 1, 2) for m in (dcd_re, dcd_im))
    dbb = [jnp.swapaxes(_diag_blocks(m, GROUP_CH, STATE), 1, 2).reshape(N_STATE, GROUP_CH) for m in (dbd_re, dbd_im)]
    dab = [jnp.sum(da[n], axis=0).reshape(N_STATE, 1) for n in range(2)]
    da_re, da_im, dldt, db_re, db_im = _ssm_discretize_bwd(
        sv["a_re"], sv["a_im"], sv["log_dt"], sv["b_re"], sv["b_im"], sv["ab_re"], sv["ab_im"], sv["e_re"], sv["e_im"],
        dab[0], dab[1], dbb[0], dbb[1])
    g["a_re"], g["a_im"] = da_re.reshape(N_GROUPS, STATE), da_im.reshape(N_GROUPS, STATE)
    g["log_dt"] = jnp.sum(dldt.reshape(N_GROUPS, STATE), axis=-1)
    g["b_re"], g["b_im"] = (t.reshape(N_GROUPS, STATE, GROUP_CH) for t in (db_re, db_im))
    return loss_rows, grad_x, g


MESH = pl.DeviceIdType.MESH
N_REL = 3


def _place():
    x, y, c = lax.axis_index("x"), lax.axis_index("y"), lax.axis_index("c")
    return x, y, c, [(1 - x, y), (x, 1 - y), (1 - x, 1 - y)]


def _remote(src, dst, send_sems, recv_sems, idx, to):
    return pltpu.make_async_remote_copy(src_ref=src, dst_ref=dst, send_sem=send_sems.at[idx], recv_sem=recv_sems.at[idx],
                                        device_id=to, device_id_type=MESH)


def _half(rows, who):
    return pl.ds(who * (rows // 2), rows // 2)


class _Copies:
    def __init__(self, send_sems, recv_sems, local_sems):
        self.send_sems, self.recv_sems, self.local_sems = send_sems, recv_sems, local_sems
        self.n_remote = self.n_local = 0

    def remote(self, src, dst, to):
        k, self.n_remote = self.n_remote, self.n_remote + 1
        return pltpu.make_async_remote_copy(src_ref=src, dst_ref=dst, send_sem=self.send_sems.at[k],
                                            recv_sem=self.recv_sems.at[k], device_id=to, device_id_type=MESH)

    def local(self, src, dst):
        k, self.n_local = self.n_local, self.n_local + 1
        return pltpu.make_async_copy(src, dst, self.local_sems.at[k])


class _Exchange:
    def __init__(self, plan, ins, out_shapes, n_remote):
        self.plan, self.ins, self.out_shapes, self.n_remote = plan, list(ins), list(out_shapes), n_remote
        self.sems = [pltpu.SemaphoreType.DMA((n_remote,)), pltpu.SemaphoreType.DMA((n_remote,)), pltpu.SemaphoreType.DMA((1,))]

    def run(self, name):
        n_in = len(self.ins)

        def body(*refs):
            for phase in self.plan(refs[:n_in], refs[n_in:-3], _Copies(*refs[-3:])):
                for cp in phase:
                    cp.start()
                for cp in phase:
                    cp.wait()

        return pl.pallas_call(body, name=name, in_specs=[ANY] * n_in, out_specs=[ANY] * len(self.out_shapes),
                              out_shape=self.out_shapes, scratch_shapes=self.sems)(*self.ins)


def _merged(a, b):
    n_in, n_out = len(a.ins), len(a.out_shapes)

    def plan(ins, outs, mk):
        (phase_a,), (phase_b,) = a.plan(ins[:n_in], outs[:n_out], mk), b.plan(ins[n_in:], outs[n_out:], mk)
        return [phase_a + phase_b]

    return _Exchange(plan, a.ins + b.ins, a.out_shapes + b.out_shapes, a.n_remote + b.n_remote)


def _pallas(body, *, name, grid, in_specs, out_specs, out_shape, args, semantics, scratch_shapes=(), rider=None, aliases=None):
    aliases = aliases or {}
    if rider is None:
        return pl.pallas_call(body, name=name, grid=grid, in_specs=in_specs, out_specs=out_specs, out_shape=out_shape,
                              scratch_shapes=list(scratch_shapes), input_output_aliases=aliases,
                              compiler_params=_params(*semantics))(*args)
    n_in, n_out, r_in, r_out = len(in_specs), len(out_specs), len(rider.ins), len(rider.out_shapes)
    n_steps = math.prod(grid)

    def carrier(*refs):
        ins, rider_ins = refs[:n_in], refs[n_in:n_in + r_in]
        outs = refs[n_in + r_in:n_in + r_in + n_out]
        rider_outs = refs[n_in + r_in + n_out:n_in + r_in + n_out + r_out]
        scratch, sems = refs[n_in + r_in + n_out + r_out:-3], refs[-3:]
        step = 0
        for axis, size in enumerate(grid):
            step = step * size + pl.program_id(axis)
        phases = rider.plan(rider_ins, rider_outs, _Copies(*sems))

        def start_phase(p):
            for cp in (phases[p - 1] if p else []):
                cp.wait()
            for cp in phases[p]:
                cp.start()

        for p in range(len(phases)):
            pl.when(step == p * n_steps // len(phases))(functools.partial(start_phase, p))
        body(*ins, *outs, *scratch)

        @pl.when(step == n_steps - 1)
        def _():
            for cp in phases[-1]:
                cp.wait()

    results = pl.pallas_call(
        carrier, name=name, grid=grid, in_specs=list(in_specs) + [ANY] * r_in, out_specs=list(out_specs) + [ANY] * r_out,
        out_shape=list(out_shape) + rider.out_shapes, scratch_shapes=list(scratch_shapes) + rider.sems,
        input_output_aliases=aliases, compiler_params=_params(*["arbitrary"] * len(grid)))(*args, *rider.ins)
    return results[:n_out], results[n_out:]


def _gather_weights(shards):
    def plan(ins, outs, mk):
        x, y, c = lax.axis_index("x"), lax.axis_index("y"), lax.axis_index("c")
        me, sibling = 2 * x + y, (x, y, 1 - c)
        x_nb, y_nb, diag = (1 - x, y), (x, 1 - y), (1 - x, 1 - y)
        index = lambda chip: 2 * chip[0] + chip[1]
        first, second, third = [], [], []
        for i, shard in enumerate(shards):
            rows = shard.shape[0]
            mine = _half(rows, c)
            quarter = lambda which: pl.ds(c * (rows // 2) + which * (rows // 4), rows // 4)
            first.append(mk.remote(ins[i], outs[i].at[me], sibling))
            for nb in (x_nb, y_nb):
                first.append(mk.remote(ins[i].at[mine], outs[i].at[me, mine], (*nb, c)))
            for nb in (x_nb, y_nb):
                landed = outs[i].at[index(nb), mine]
                second.append(mk.remote(landed, landed, sibling))
            for nb, other, which in ((x_nb, y_nb, 0), (y_nb, x_nb, 1)):
                landed = outs[i].at[index(nb), quarter(which)]
                second.append(mk.remote(landed, landed, (*other, c)))
            landed = outs[i].at[index(diag), mine]
            third.append(mk.remote(landed, landed, sibling))
        return [first, second, third]

    return _Exchange(plan, shards, [jax.ShapeDtypeStruct((N_CHIPS,) + s.shape, s.dtype) for s in shards], 8 * len(shards))


def _pair_exchange(grads):
    def plan(ins, outs, mk):
        x, y, c = lax.axis_index("x"), lax.axis_index("y"), lax.axis_index("c")
        return [[mk.remote(ins[i].at[:, _half(g.shape[1], 1 - c)], outs[i], (x, y, 1 - c)) for i, g in enumerate(grads)]]

    return _Exchange(plan, grads, [jax.ShapeDtypeStruct((N_CHIPS, g.shape[1] // 2, g.shape[2]), g.dtype) for g in grads], len(grads))


def _reduce_first(pair):
    def plan(ins, outs, mk):
        x, y, c = lax.axis_index("x"), lax.axis_index("y"), lax.axis_index("c")
        phase = []
        for i, p in enumerate(pair):
            q = p.shape[1] // 2
            phase.append(mk.remote(ins[i].at[pl.ds(2 * (1 - x), 2), pl.ds(0, q)], outs[i].at[0], (1 - x, y, c)))
            for jx in range(2):
                phase.append(mk.remote(ins[i].at[2 * jx + 1 - y, pl.ds(q, q)], outs[i].at[1, jx], (x, 1 - y, c)))
        return [phase]

    return _Exchange(plan, pair, [jax.ShapeDtypeStruct((2, 2, p.shape[1] // 2, p.shape[2]), p.dtype) for p in pair], 3 * len(pair))


def _reduce_second(sums):
    def plan(ins, outs, mk):
        x, y, c = lax.axis_index("x"), lax.axis_index("y"), lax.axis_index("c")
        phase = []
        for i in range(len(sums)):
            phase.append(mk.remote(ins[i].at[0, 1 - y], outs[i].at[0], (x, 1 - y, c)))
            phase.append(mk.remote(ins[i].at[1, 1 - x], outs[i].at[1], (1 - x, y, c)))
        return [phase]

    return _Exchange(plan, sums, [jax.ShapeDtypeStruct((2,) + s.shape[2:], s.dtype) for s in sums], 2 * len(sums))


def _pair_swap(halves):
    def plan(ins, outs, mk):
        x, y, c = lax.axis_index("x"), lax.axis_index("y"), lax.axis_index("c")
        return [[mk.remote(ins[i], outs[i], (x, y, 1 - c)) for i in range(len(halves))]]

    return _Exchange(plan, halves, [jax.ShapeDtypeStruct(h.shape, h.dtype) for h in halves], len(halves))


def _allreduce_small(packed):
    rows = packed.shape[0]
    n_dev = 2 * N_CHIPS

    def body(x_ref, o_ref, buf, send_sems, recv_sems):
        x, y, c, chips = _place()
        sibling = (x, y, 1 - c)

        def slot(px, py, pc):
            return buf.at[4 * px + 2 * py + pc]

        buf[4 * x + 2 * y + c] = x_ref[...]
        first = [_remote(x_ref, slot(x, y, c), send_sems, recv_sems, 0, sibling)]
        first += [_remote(x_ref, slot(x, y, c), send_sems, recv_sems, 1 + k, (*chip, c)) for k, chip in enumerate(chips)]
        for cp in first:
            cp.start()
        passed = []
        for k, chip in enumerate(chips):
            landed = slot(*chip, c)
            _remote(landed, landed, send_sems, recv_sems, 1 + k, (*chip, c)).wait_recv()
            passed.append(_remote(landed, landed, send_sems, recv_sems, 1 + N_REL + k, sibling))
            passed[-1].start()
        _remote(slot(*sibling), slot(*sibling), send_sems, recv_sems, 0, sibling).wait_recv()
        for k, chip in enumerate(chips):
            landed = slot(*chip, 1 - c)
            _remote(landed, landed, send_sems, recv_sems, 1 + N_REL + k, sibling).wait_recv()
        for cp in first + passed:
            cp.wait_send()
        total = buf[0]
        for d in range(1, n_dev):
            total = total + buf[d]
        o_ref[...] = total

    vm = pl.BlockSpec(memory_space=pltpu.VMEM)
    return pl.pallas_call(
        body, name="allreduce_small", in_specs=[vm], out_specs=vm, out_shape=jax.ShapeDtypeStruct(packed.shape, F32),
        scratch_shapes=[pltpu.VMEM((n_dev, rows, 128), F32), pltpu.SemaphoreType.DMA((1 + 2 * N_REL,)),
                        pltpu.SemaphoreType.DMA((1 + 2 * N_REL,))],
    )(packed)


def _row_tile(rows, cap=256):
    return max(t for t in range(8, cap + 1, 8) if rows % t == 0)


def _pair_sum(grad, got, c, name):
    _, half, cols = got.shape
    tr = _row_tile(half)
    nt = half // tr

    def body(c_ref, g_ref, r_ref, o_ref, wire_ref):
        total = g_ref[...] + r_ref[...]
        o_ref[...] = total
        wire_ref[...] = total.astype(BF16)

    blk = (1, tr, cols)
    out = pl.BlockSpec(blk, lambda j, t, c_ref: (j, t, 0))
    return pl.pallas_call(
        body, name=name,
        grid_spec=pltpu.PrefetchScalarGridSpec(
            num_scalar_prefetch=1, grid=(N_CHIPS, nt),
            in_specs=[pl.BlockSpec(blk, lambda j, t, c_ref: (j, c_ref[0] * nt + t, 0)), out], out_specs=[out, out]),
        out_shape=[jax.ShapeDtypeStruct(got.shape, F32), jax.ShapeDtypeStruct(got.shape, BF16)],
        compiler_params=_params("parallel", "parallel"),
    )(c, grad, got)


def _reduce_sum_first(pair, got, sel, name):
    _, _, q, cols = got.shape
    tr = _row_tile(q)
    nt = q // tr

    def body(sel_ref, p_ref, r_ref, o_ref, wire_ref):
        total = p_ref[0] + r_ref[0, 0].astype(F32)
        o_ref[0, 0] = total
        wire_ref[0, 0] = total.astype(BF16)

    blk = pl.BlockSpec((1, 1, tr, cols), lambda p, k, t, s: (p, k, t, 0))
    return pl.pallas_call(
        body, name=name,
        grid_spec=pltpu.PrefetchScalarGridSpec(
            num_scalar_prefetch=1, grid=(2, 2, nt),
            in_specs=[pl.BlockSpec((1, tr, cols), lambda p, k, t, s: (s[2 * p] + s[2 * p + 1] * k, p * nt + t, 0)), blk],
            out_specs=[blk, blk]),
        out_shape=[jax.ShapeDtypeStruct(got.shape, F32), jax.ShapeDtypeStruct(got.shape, BF16)],
        compiler_params=_params("parallel", "parallel", "parallel"),
    )(sel, pair, got)


def _reduce_sum_second(sums, got, sel, name):
    _, q, cols = got.shape
    tr = _row_tile(q)

    def body(sel_ref, s_ref, r_ref, o_ref):
        o_ref[0] = s_ref[0, 0] + r_ref[0].astype(F32)

    blk = (1, tr, cols)
    return pl.pallas_call(
        body, name=name,
        grid_spec=pltpu.PrefetchScalarGridSpec(
            num_scalar_prefetch=1, grid=(2, q // tr),
            in_specs=[pl.BlockSpec((1, 1, tr, cols), lambda p, t, s: (p, s[p], t, 0)),
                      pl.BlockSpec(blk, lambda p, t, s: (p, t, 0))],
            out_specs=pl.BlockSpec(blk, lambda p, t, s: (p, t, 0))),
        out_shape=jax.ShapeDtypeStruct(got.shape, F32), compiler_params=_params("parallel", "parallel"),
    )(sel, sums, got)


def _adamw_update(w, g, m, v):
    m2 = ADAM_B1 * m + (1.0 - ADAM_B1) * g
    v2 = ADAM_B2 * v + (1.0 - ADAM_B2) * (g * g)
    m_hat = m2 / (1.0 - ADAM_B1 ** ADAM_STEP)
    v_hat = v2 / (1.0 - ADAM_B2 ** ADAM_STEP)
    return -ADAM_LR * (m_hat / (jnp.sqrt(v_hat) + ADAM_EPS) + ADAM_WD * w), m2, v2


def _adamw_small(ws, gs, ms, vs):
    n = len(ws)

    def body(*refs):
        w, g, m, v, d, mo, vo = (refs[k * n:(k + 1) * n] for k in range(7))
        for i in range(n):
            d[i][...], mo[i][...], vo[i][...] = _adamw_update(w[i][...], g[i][...], m[i][...], v[i][...])

    shapes = [jax.ShapeDtypeStruct(t.shape, F32) for t in ws]
    outs = pl.pallas_call(body, name="adamw_small", out_shape=shapes * 3,
                          compiler_params=pltpu.CompilerParams(vmem_limit_bytes=VMEM_LIMIT_V7X))(*ws, *gs, *ms, *vs)
    return outs[:n], outs[n:2 * n], outs[2 * n:]


def _adamw_halves(w, mine, theirs, m, v, core, name):
    rows, cols = w.shape
    tr = _row_tile(rows // 2)
    per_half = rows // 2 // tr

    def body(core_ref, w_ref, a_ref, b_ref, m_ref, v_ref, g_ref, d_ref, mo_ref, vo_ref):
        g = jnp.where(pl.program_id(0) // per_half == core_ref[0], a_ref[...], b_ref[...])
        g_ref[...] = g
        d_ref[...], mo_ref[...], vo_ref[...] = _adamw_update(w_ref[...], g, m_ref[...], v_ref[...])

    blk = pl.BlockSpec((tr, cols), lambda t, c: (t, 0))
    half = lambda own: pl.BlockSpec(
        (tr, cols), lambda t, c: (jnp.clip(t - (c[0] if own else 1 - c[0]) * per_half, 0, per_half - 1), 0))
    return pl.pallas_call(
        body, name=name,
        grid_spec=pltpu.PrefetchScalarGridSpec(
            num_scalar_prefetch=1, grid=(2 * per_half,), in_specs=[blk, half(True), half(False), blk, blk], out_specs=[blk] * 4),
        out_shape=[jax.ShapeDtypeStruct(w.shape, F32)] * 4, compiler_params=_params("arbitrary"),
    )(core, w, mine, theirs, m, v)


def _pack(parts):
    flat = []
    for t in parts:
        t = t.reshape(-1).astype(F32)
        flat.append(jnp.pad(t, (0, -t.shape[0] % 128)))
    flat = jnp.concatenate(flat)
    return jnp.pad(flat, (0, -flat.shape[0] % 1024)).reshape(-1, 128)


def _unpack(buf, shapes):
    flat, out, at = buf.reshape(-1), [], 0
    for s in shapes:
        size = math.prod(s)
        out.append(flat[at:at + size].reshape(s))
        at += size + (-size % 128)
    return out


def kernel(x, ffn1_pre_g, ffn1_w_in, ffn1_w_out, ffn1_post_g, mix_pre_g, w_mix_in, a_re, a_im, log_dt, b_re, b_im, c_re, c_im, d_skip, w_glu, b_glu, w_mix_out, mix_post_g, ffn2_pre_g, ffn2_w_in, ffn2_w_out, ffn2_post_g, loss_target, m_ffn1_pre_g, m_ffn1_w_in, m_ffn1_w_out, m_ffn1_post_g, m_mix_pre_g, m_w_mix_in, m_a_re, m_a_im, m_log_dt, m_b_re, m_b_im, m_c_re, m_c_im, m_d_skip, m_w_glu, m_b_glu, m_w_mix_out, m_mix_post_g, m_ffn2_pre_g, m_ffn2_w_in, m_ffn2_w_out, m_ffn2_post_g, v_ffn1_pre_g, v_ffn1_w_in, v_ffn1_w_out, v_ffn1_post_g, v_mix_pre_g, v_w_mix_in, v_a_re, v_a_im, v_log_dt, v_b_re, v_b_im, v_c_re, v_c_im, v_d_skip, v_w_glu, v_b_glu, v_w_mix_out, v_mix_post_g, v_ffn2_pre_g, v_ffn2_w_in, v_ffn2_w_out, v_ffn2_post_g):
    given = dict(locals())
    order = ("ffn1_pre_g", "ffn1_w_in", "ffn1_w_out", "ffn1_post_g", "mix_pre_g", "w_mix_in", "a_re", "a_im", "log_dt",
             "b_re", "b_im", "c_re", "c_im", "d_skip", "w_glu", "b_glu", "w_mix_out", "mix_post_g", "ffn2_pre_g",
             "ffn2_w_in", "ffn2_w_out", "ffn2_post_g")
    at_x, at_y, at_c = (lax.axis_index(a).astype(jnp.int32) for a in ("x", "y", "c"))
    place = dict(core=at_c.reshape(1), sel_first=jnp.stack([2 * at_x, jnp.int32(1), at_y, jnp.int32(2)]),
                 sel_second=jnp.stack([at_y, at_x]))

    shards = {n: given[n][0] for n in BIG}
    w = {n: shards[n].astype(BF16) for n in REST}
    w.update(zip(FIRST, _gather_weights([shards[n].astype(BF16) for n in FIRST]).run("gather_first")))
    small = {n: given[n][0] for n in SMALL}
    loss_rows, grad_x, g = _local_step(x[0], loss_target[0], small, w, place)

    total = _allreduce_small(_pack([g[n] for n in SMALL] + [loss_rows[0, :1]]))
    parts = _unpack(total, [small[n].shape for n in SMALL] + [(1,)])
    grads = dict(zip(SMALL, parts[:-1]))
    loss = parts[-1][0]

    delta, new_m, new_v = {}, {}, {}
    for n in BIG:
        grads[n], delta[n], new_m[n], new_v[n] = _adamw_halves(
            shards[n], *g[n], given["m_" + n][0], given["v_" + n][0], place["core"], name=f"adamw_{n}")
    take = lambda pre: [given[pre + n] for n in SMALL]
    outs = _adamw_small(take(""), [grads[n][None] for n in SMALL], take("m_"), take("v_"))
    for store, arrays in zip((delta, new_m, new_v), outs):
        store.update({n: t[0] for n, t in zip(SMALL, arrays)})

    lead = lambda d: [d[n][None] for n in order]
    return (loss, grad_x[None], *lead(grads), *lead(delta), *lead(new_m), *lead(new_v))
```

```python
import functools
import math

import jax
import jax.numpy as jnp
from jax import lax
from jax.experimental import pallas as pl
from jax.experimental.pallas import tpu as pltpu

F32, BF16 = jnp.float32, jnp.bfloat16

D_MODEL = 1024
D_FF = 2816
N_CHIPS = 4
FF_BLK = 2 * D_FF // N_CHIPS
ATTN_W = 512
SSM_W = 512
HEAD_DIM = 64
N_HEADS = ATTN_W // HEAD_DIM
DILATIONS = (1, 4, 16)
N_BACK = 128
QBLK = 128
N_GROUPS = 32
GROUP_CH = 16
STATE = 64
N_STATE = N_GROUPS * STATE
EPS = 1e-6
NEG = -1e30
GELU_C = math.sqrt(2.0 / math.pi)

ADAM_LR, ADAM_B1, ADAM_B2, ADAM_EPS, ADAM_WD, ADAM_STEP = 0.001, 0.9, 0.999, 1e-08, 0.01, 10

VMEM_LIMIT_V7X = 60 * 1024 * 1024
ROW_TILE = 512
FFN_ROW_TILE = 512
DW_ROW_TILE = 1024


def _params(*sem):
    return pltpu.CompilerParams(dimension_semantics=sem, vmem_limit_bytes=VMEM_LIMIT_V7X)


def _dot(a, b):
    return jnp.dot(a.astype(BF16), b.astype(BF16), preferred_element_type=F32)


def _dot_nt(a, b):
    return lax.dot_general(a.astype(BF16), b.astype(BF16), (((1,), (1,)), ((), ())), preferred_element_type=F32)


def _dot_tn(a, b):
    return lax.dot_general(a.astype(BF16), b.astype(BF16), (((0,), (0,)), ((), ())), preferred_element_type=F32)


def _full(shape):
    return pl.BlockSpec(shape, lambda *_: (0,) * len(shape))


def _rows(tm, width):
    return pl.BlockSpec((tm, width), lambda i: (i, 0))


ANY = pl.BlockSpec(memory_space=pl.ANY)


def _load_once(pairs, sems):
    copies = [pltpu.make_async_copy(src, dst, sems.at[k]) for k, (src, dst) in enumerate(pairs)]
    for c in copies:
        c.start()
    for c in copies:
        c.wait()


def _rms(x):
    return lax.rsqrt(jnp.mean(x * x, axis=-1, keepdims=True) + EPS)


def _rms_bwd(dy_g, xn, r):
    return r * (dy_g - xn * jnp.mean(dy_g * xn, axis=-1, keepdims=True))


def _ffn_fwd(x, g_pre, w_in, w_out, g_post, target, *, name, rider=None):
    T = x.shape[0]
    tm = FFN_ROW_TILE
    with_loss = target is not None

    def body(*refs):
        if with_loss:
            x_ref, gpre_ref, gpost_ref, tgt_ref, win_hbm, wout_hbm, o_ref, loss_ref, z_ref, f_ref, win_v, wout_v, sems = refs
        else:
            x_ref, gpre_ref, gpost_ref, win_hbm, wout_hbm, o_ref, z_ref, f_ref, win_v, wout_v, sems = refs

        @pl.when(pl.program_id(0) == 0)
        def _():
            _load_once([(win_hbm, win_v), (wout_hbm, wout_v)], sems)
            if with_loss:
                loss_ref[...] = jnp.zeros_like(loss_ref)

        xv = x_ref[...]
        h = (xv * _rms(xv) * gpre_ref[...]).astype(BF16)
        f = jnp.zeros((tm, D_MODEL), F32)
        for k in range(2):
            gate = _dot(h, win_v[k])
            up = _dot(h, win_v[k + 2])
            z_ref[:, k * FF_BLK:(k + 1) * FF_BLK] = gate.astype(BF16)
            z_ref[:, D_FF + k * FF_BLK:D_FF + (k + 1) * FF_BLK] = up.astype(BF16)
            f = f + _dot(gate * jax.nn.sigmoid(gate) * up, wout_v[k])
        f_ref[...] = f
        out = xv + 0.5 * (f * _rms(f) * gpost_ref[...])
        if with_loss:
            err = out - tgt_ref[...]
            o_ref[...] = err * (1.0 / D_MODEL)
            loss_ref[...] += jnp.sum(err * err) * (0.5 / D_MODEL)
        else:
            o_ref[...] = out

    row = _rows(tm, D_MODEL)
    vec = _full((1, D_MODEL))
    in_specs = [row, vec, vec] + ([row] if with_loss else []) + [ANY, ANY]
    out_shape = [jax.ShapeDtypeStruct((T, D_MODEL), F32)]
    out_specs = [row]
    if with_loss:
        out_shape.append(jax.ShapeDtypeStruct((8, 128), F32))
        out_specs.append(_full((8, 128)))
    out_shape += [jax.ShapeDtypeStruct((T, 2 * D_FF), BF16), jax.ShapeDtypeStruct((T, D_MODEL), F32)]
    out_specs += [_rows(tm, 2 * D_FF), row]
    args = (x, g_pre, g_post) + ((target,) if with_loss else ()) + (w_in, w_out)
    return _pallas(
        body, name=name, grid=(T // tm,), in_specs=in_specs, out_specs=out_specs, out_shape=out_shape,
        scratch_shapes=[pltpu.VMEM(w_in.shape, BF16), pltpu.VMEM(w_out.shape, BF16), pltpu.SemaphoreType.DMA((2,))],
        semantics=("arbitrary",), args=args, rider=rider)


def _ffn_bwd_out(dout, f, z, w_out, g_post, *, name):
    T = dout.shape[0]
    tm = FFN_ROW_TILE

    def body(dout_ref, f_ref, z_ref, gpost_ref, wout_hbm, dz_ref, df_ref, dgpost_ref, wout_v, sems):
        @pl.when(pl.program_id(0) == 0)
        def _():
            _load_once([(wout_hbm, wout_v)], sems)
            dgpost_ref[...] = jnp.zeros_like(dgpost_ref)

        dy = 0.5 * dout_ref[...]
        f = f_ref[...]
        r = _rms(f)
        fn = f * r
        dgpost_ref[...] += jnp.sum(dy * fn, axis=0, keepdims=True)
        df = _rms_bwd(dy * gpost_ref[...], fn, r).astype(BF16)
        df_ref[...] = df
        for k in range(2):
            gate = z_ref[:, k * FF_BLK:(k + 1) * FF_BLK].astype(F32)
            up = z_ref[:, D_FF + k * FF_BLK:D_FF + (k + 1) * FF_BLK].astype(F32)
            sg = jax.nn.sigmoid(gate)
            da = _dot_nt(df, wout_v[k])
            dz_ref[:, k * FF_BLK:(k + 1) * FF_BLK] = (da * up * (sg * (1.0 + gate * (1.0 - sg)))).astype(BF16)
            dz_ref[:, D_FF + k * FF_BLK:D_FF + (k + 1) * FF_BLK] = (da * (gate * sg)).astype(BF16)

    row = _rows(tm, D_MODEL)
    return pl.pallas_call(
        body, name=name, grid=(T // tm,),
        in_specs=[row, row, _rows(tm, 2 * D_FF), _full((1, D_MODEL)), ANY],
        out_specs=[_rows(tm, 2 * D_FF), row, _full((1, D_MODEL))],
        out_shape=[jax.ShapeDtypeStruct((T, 2 * D_FF), BF16), jax.ShapeDtypeStruct((T, D_MODEL), BF16),
                   jax.ShapeDtypeStruct((1, D_MODEL), F32)],
        scratch_shapes=[pltpu.VMEM(w_out.shape, BF16), pltpu.SemaphoreType.DMA((1,))],
        compiler_params=_params("arbitrary"),
    )(dout, f, z, g_post, w_out)


def _ffn_dw_out(z, df, *, name):
    T = z.shape[0]
    tm = FFN_ROW_TILE

    def body(gate_ref, up_ref, df_ref, dw_ref):
        @pl.when(pl.program_id(1) == 0)
        def _():
            dw_ref[...] = jnp.zeros_like(dw_ref)

        gate = gate_ref[...].astype(F32)
        dw_ref[0] += _dot_tn(gate * jax.nn.sigmoid(gate) * up_ref[...].astype(F32), df_ref[...])

    return pl.pallas_call(
        body, name=name, grid=(2, T // tm),
        in_specs=[pl.BlockSpec((tm, FF_BLK), lambda k, t: (t, k)), pl.BlockSpec((tm, FF_BLK), lambda k, t: (t, 2 + k)),
                  pl.BlockSpec((tm, D_MODEL), lambda k, t: (t, 0))],
        out_specs=pl.BlockSpec((1, FF_BLK, D_MODEL), lambda k, t: (k, 0, 0)),
        out_shape=jax.ShapeDtypeStruct((2, FF_BLK, D_MODEL), F32), compiler_params=_params("parallel", "arbitrary"),
    )(z, z, df)


def _norm_matmul_dw(dz, x, g, n_blocks, *, name, rider=None):
    T = x.shape[0]
    bw = dz.shape[1] // n_blocks
    tm = DW_ROW_TILE

    def body(dz_ref, x_ref, g_ref, dw_ref):
        @pl.when(pl.program_id(1) == 0)
        def _():
            dw_ref[...] = jnp.zeros_like(dw_ref)

        xv = x_ref[...]
        dw_ref[0] += _dot_tn(xv * _rms(xv) * g_ref[...], dz_ref[...])

    res = _pallas(
        body, name=name, grid=(n_blocks, T // tm),
        in_specs=[pl.BlockSpec((tm, bw), lambda j, t: (t, j)), pl.BlockSpec((tm, D_MODEL), lambda j, t: (t, 0)), _full((1, D_MODEL))],
        out_specs=[pl.BlockSpec((1, D_MODEL, bw), lambda j, t: (j, 0, 0))],
        out_shape=[jax.ShapeDtypeStruct((n_blocks, D_MODEL, bw), F32)], semantics=("parallel", "arbitrary"),
        args=(dz, x, g), rider=rider)
    return res[0] if rider is None else (res[0][0], res[1])


def _norm_matmul_dx(dz, x, dres, g, w, *, name, rider=None, part=(0, 1), into=None):
    T = x.shape[0]
    nb, _, bw = w.shape
    tm = FFN_ROW_TILE
    first, steps = round(part[0] * T) // tm, round((part[1] - part[0]) * T) // tm

    def body(dz_ref, x_ref, dres_ref, g_ref, w_hbm, *rest):
        dx_ref, dg_ref, w_v, sems = rest[-4:]

        @pl.when(pl.program_id(0) == 0)
        def _():
            _load_once([(w_hbm, w_v)], sems)
            dg_ref[...] = jnp.zeros_like(dg_ref)

        xv = x_ref[...]
        r = _rms(xv)
        xn = xv * r
        gv = g_ref[...]
        dh = jnp.zeros((tm, D_MODEL), F32)
        for j in range(nb):
            dh = dh + _dot_nt(dz_ref[:, j * bw:(j + 1) * bw], w_v[j])
        dg_ref[...] += jnp.sum(dh * xn, axis=0, keepdims=True)
        dx_ref[...] = _rms_bwd(dh * gv, xn, r) + dres_ref[...]

    rows = lambda width: pl.BlockSpec((tm, width), lambda i: (first + i, 0))
    row = rows(D_MODEL)
    return _pallas(
        body, name=name, grid=(steps,),
        in_specs=[rows(nb * bw), row, row, _full((1, D_MODEL)), ANY] + ([] if into is None else [ANY]),
        out_specs=[row, _full((1, D_MODEL))],
        out_shape=[jax.ShapeDtypeStruct((T, D_MODEL), F32), jax.ShapeDtypeStruct((1, D_MODEL), F32)],
        scratch_shapes=[pltpu.VMEM(w.shape, BF16), pltpu.SemaphoreType.DMA((1,))],
        semantics=("arbitrary",), args=(dz, x, dres, g, w) + (() if into is None else (into,)), rider=rider,
        aliases={} if into is None else {5: 0})


def _mix_in_fwd(x, g, w):
    T = x.shape[0]
    tm = ROW_TILE

    def body(x_ref, g_ref, w_ref, q_ref, k_ref, v_ref, u_ref):
        xv = x_ref[...]
        h = (xv * _rms(xv) * g_ref[...]).astype(BF16)
        for j, o_ref in enumerate((q_ref, k_ref, v_ref, u_ref)):
            o_ref[...] = _dot(h, w_ref[j])

    col = _rows(tm, ATTN_W)
    return pl.pallas_call(
        body, name="mix_in_fwd", grid=(T // tm,),
        in_specs=[_rows(tm, D_MODEL), _full((1, D_MODEL)), _full(w.shape)],
        out_specs=[col] * 4, out_shape=[jax.ShapeDtypeStruct((T, ATTN_W), F32)] * 4,
        compiler_params=_params("parallel"),
    )(x, g, w)


def _mix_out_fwd(x, attn, ssm, w, g):
    T = x.shape[0]
    tm = ROW_TILE

    def body(x_ref, a_ref, s_ref, w_ref, g_ref, o_ref, m_ref):
        mixed = _dot(a_ref[...], w_ref[0]) + _dot(s_ref[...], w_ref[1])
        m_ref[...] = mixed
        o_ref[...] = x_ref[...] + mixed * _rms(mixed) * g_ref[...]

    row, col = _rows(tm, D_MODEL), _rows(tm, ATTN_W)
    return pl.pallas_call(
        body, name="mix_out_fwd", grid=(T // tm,),
        in_specs=[row, col, col, _full(w.shape), _full((1, D_MODEL))],
        out_specs=[row, row], out_shape=[jax.ShapeDtypeStruct((T, D_MODEL), F32)] * 2,
        compiler_params=_params("parallel"),
    )(x, attn, ssm, w, g)


def _mix_out_bwd(dout, mixed, attn, ssm, w, g, rider=None):
    T = dout.shape[0]
    tm = ROW_TILE

    def body(dout_ref, m_ref, a_ref, s_ref, w_ref, g_ref, da_ref, ds_ref, dw_ref, dg_ref):
        @pl.when(pl.program_id(0) == 0)
        def _():
            dw_ref[...] = jnp.zeros_like(dw_ref)
            dg_ref[...] = jnp.zeros_like(dg_ref)

        dy = dout_ref[...]
        mixed = m_ref[...]
        r = _rms(mixed)
        mn = mixed * r
        dg_ref[...] += jnp.sum(dy * mn, axis=0, keepdims=True)
        dm = _rms_bwd(dy * g_ref[...], mn, r).astype(BF16)
        da_ref[...] = _dot_nt(dm, w_ref[0])
        ds_ref[...] = _dot_nt(dm, w_ref[1])
        dw_ref[0] += _dot_tn(a_ref[...], dm)
        dw_ref[1] += _dot_tn(s_ref[...], dm)

    row, col = _rows(tm, D_MODEL), _rows(tm, ATTN_W)
    return _pallas(
        body, name="mix_out_bwd", grid=(T // tm,),
        in_specs=[row, row, col, col, _full(w.shape), _full((1, D_MODEL))],
        out_specs=[col, col, _full(w.shape), _full((1, D_MODEL))],
        out_shape=[jax.ShapeDtypeStruct((T, ATTN_W), F32)] * 2
        + [jax.ShapeDtypeStruct(w.shape, F32), jax.ShapeDtypeStruct((1, D_MODEL), F32)],
        semantics=("arbitrary",), args=(dout, mixed, attn, ssm, w, g), rider=rider)


ATTN_TILING = {1: (ATTN_W, 1), 4: (2 * HEAD_DIM, 4), 16: (2 * HEAD_DIM, 4)}


def _class_rows(d, r):
    return (pl.ds(r, QBLK, stride=d), slice(None)) if d > 1 else (slice(None), slice(None))


def _for_class_groups(d, group, fn):
    if d == group:
        fn(0)
    else:
        lax.fori_loop(0, d // group, lambda n, carry: (fn(n * group), carry)[1], 0)


def _block_slopes(lanes, lane_block):
    heads = lanes // HEAD_DIM
    first = lane_block * heads
    return [jnp.exp2(-jnp.full((1, 1), first + hh + 1, jnp.int32).astype(F32)) for hh in range(heads)]


def _attn_specs(d, nb, lanes):
    blk = (QBLK * d, lanes)
    cur = pl.BlockSpec(blk, lambda j, lb: (j, lb))
    prev = pl.BlockSpec(blk, lambda j, lb: (jnp.maximum(j - 1, 0), lb))
    nxt = pl.BlockSpec(blk, lambda j, lb: (jnp.minimum(j + 1, nb - 1), lb))
    return cur, prev, nxt


def _attn_branch_fwd(q, k, v, d):
    T = q.shape[0]
    nb = T // (d * QBLK)
    lanes, group = ATTN_TILING[d]
    scale = HEAD_DIM ** -0.5

    def body(q_ref, kc_ref, kp_ref, vc_ref, vp_ref, o_ref, l_ref, q_s, kk_s, vv_s, o_s, l_s):
        j = pl.program_id(0)
        qi = lax.broadcasted_iota(jnp.int32, (QBLK, 2 * QBLK), 0)
        ci = lax.broadcasted_iota(jnp.int32, (QBLK, 2 * QBLK), 1)
        steps = QBLK + qi - ci
        valid = (steps >= 0) & (steps <= N_BACK) & ((ci >= QBLK) | (j > 0))
        dist = (steps * d).astype(F32)
        slopes = _block_slopes(lanes, pl.program_id(1))

        def classes(first):
            for n in range(group):
                rows = _class_rows(d, first + n)
                q_s[n] = q_ref[rows]
                kk_s[n, :QBLK], kk_s[n, QBLK:] = kp_ref[rows], kc_ref[rows]
                vv_s[n, :QBLK], vv_s[n, QBLK:] = vp_ref[rows], vc_ref[rows]
            for n in range(group):
                for hh in range(lanes // HEAD_DIM):
                    sl = slice(hh * HEAD_DIM, (hh + 1) * HEAD_DIM)
                    s = _dot_nt(q_s[n, :, sl], kk_s[n, :, sl]) * scale - slopes[hh] * dist
                    s = jnp.where(valid, s, NEG)
                    m = jnp.max(s, axis=-1, keepdims=True)
                    p = jnp.exp(s - m)
                    den = jnp.sum(p, axis=-1, keepdims=True)
                    o_s[n, :, sl] = _dot(p, vv_s[n, :, sl]) / den
                    l_s[n, :, sl] = jnp.broadcast_to(m + jnp.log(den), (QBLK, HEAD_DIM))
            for n in range(group):
                rows = _class_rows(d, first + n)
                o_ref[rows] = o_s[n]
                l_ref[rows] = l_s[n]

        _for_class_groups(d, group, classes)

    cur, prev, _ = _attn_specs(d, nb, lanes)
    shape = jax.ShapeDtypeStruct((T, ATTN_W), F32)
    one, two = pltpu.VMEM((group, QBLK, lanes), F32), pltpu.VMEM((group, 2 * QBLK, lanes), F32)
    return pl.pallas_call(
        body, name=f"attn_fwd_d{d}", grid=(nb, ATTN_W // lanes),
        in_specs=[cur, cur, prev, cur, prev], out_specs=[cur, cur], out_shape=[shape, shape],
        scratch_shapes=[one, two, two, one, one], compiler_params=_params("parallel", "parallel"),
    )(q, k, k, v, v)


def _attn_merge(outs, lses):
    T = outs[0].shape[0]
    tm = 512

    def body(o1, o2, o3, l1, l2, l3, a_ref, lse_ref):
        ls = [l1[...], l2[...], l3[...]]
        m = jnp.maximum(jnp.maximum(ls[0], ls[1]), ls[2])
        lse = m + jnp.log(jnp.exp(ls[0] - m) + jnp.exp(ls[1] - m) + jnp.exp(ls[2] - m))
        lse_ref[...] = lse
        a_ref[...] = jnp.exp(ls[0] - lse) * o1[...] + jnp.exp(ls[1] - lse) * o2[...] + jnp.exp(ls[2] - lse) * o3[...]

    col = _rows(tm, ATTN_W)
    return pl.pallas_call(
        body, name="attn_merge", grid=(T // tm,), in_specs=[col] * 6, out_specs=[col, col],
        out_shape=[jax.ShapeDtypeStruct((T, ATTN_W), F32)] * 2, compiler_params=_params("parallel"),
    )(*outs, *lses)


def _attn_branch_bwd(q, k, v, o, lse, do, d, rider=None):
    T = q.shape[0]
    nb = T // (d * QBLK)
    lanes, group = ATTN_TILING[d]
    scale = HEAD_DIM ** -0.5

    def body(q_ref, kc_ref, kp_ref, vc_ref, vp_ref, o_ref, l_ref, do_ref, dq_ref, dk_ref, dv_ref,
             q_s, o_s, l_s, do_s, kk_s, vv_s, dq_s, dk_s, dv_s, ck_s, cv_s):
        j = pl.program_id(1)

        @pl.when(j == 0)
        def _():
            ck_s[...] = jnp.zeros_like(ck_s)
            cv_s[...] = jnp.zeros_like(cv_s)

        qi = lax.broadcasted_iota(jnp.int32, (QBLK, 2 * QBLK), 0)
        ci = lax.broadcasted_iota(jnp.int32, (QBLK, 2 * QBLK), 1)
        steps = QBLK + qi - ci
        valid = (steps >= 0) & (steps <= N_BACK) & ((ci >= QBLK) | (j > 0))
        dist = (steps * d).astype(F32)
        lo, hi = slice(0, QBLK), slice(QBLK, 2 * QBLK)
        slopes = _block_slopes(lanes, pl.program_id(0))

        def classes(first):
            for n in range(group):
                rows = _class_rows(d, first + n)
                q_s[n], o_s[n], l_s[n], do_s[n] = q_ref[rows], o_ref[rows], l_ref[rows], do_ref[rows]
                kk_s[n, lo], kk_s[n, hi] = kp_ref[rows], kc_ref[rows]
                vv_s[n, lo], vv_s[n, hi] = vp_ref[rows], vc_ref[rows]
            for n in range(group):
                for hh in range(lanes // HEAD_DIM):
                    sl = slice(hh * HEAD_DIM, (hh + 1) * HEAD_DIM)
                    qh, doh, kk, vv = q_s[n, :, sl], do_s[n, :, sl], kk_s[n, :, sl], vv_s[n, :, sl]
                    delta = jnp.sum(doh * o_s[n, :, sl], axis=-1, keepdims=True)
                    s = jnp.where(valid, _dot_nt(qh, kk) * scale - slopes[hh] * dist, NEG)
                    p = jnp.exp(s - l_s[n, :, hh * HEAD_DIM:hh * HEAD_DIM + 1])
                    ds = p * (_dot_nt(doh, vv) - delta)
                    dq_s[n, :, sl] = _dot(ds, kk) * scale
                    dkk = _dot_tn(ds, qh) * scale
                    dvv = _dot_tn(p, doh)
                    dk_s[n, :, sl] = ck_s[first + n, :, sl] + dkk[lo]
                    dv_s[n, :, sl] = cv_s[first + n, :, sl] + dvv[lo]
                    ck_s[first + n, :, sl] = dkk[hi]
                    cv_s[first + n, :, sl] = dvv[hi]
            for n in range(group):
                rows = _class_rows(d, first + n)
                dq_ref[rows] = dq_s[n]
                dk_ref[rows] = dk_s[n]
                dv_ref[rows] = dv_s[n]

        @pl.when(j < nb)
        def _():
            _for_class_groups(d, group, classes)

        @pl.when(j == nb)
        def _():
            for r in range(d):
                dk_ref[_class_rows(d, r)] = ck_s[r]
                dv_ref[_class_rows(d, r)] = cv_s[r]

    blk = (QBLK * d, lanes)
    here = lambda j: jnp.minimum(j, nb - 1)
    cur = pl.BlockSpec(blk, lambda lb, j: (here(j), lb))
    prev = pl.BlockSpec(blk, lambda lb, j: (jnp.maximum(here(j) - 1, 0), lb))
    behind = pl.BlockSpec(blk, lambda lb, j: (jnp.maximum(j - 1, 0), lb))
    shape = jax.ShapeDtypeStruct((T, ATTN_W), F32)
    one, two = pltpu.VMEM((group, QBLK, lanes), F32), pltpu.VMEM((group, 2 * QBLK, lanes), F32)
    carry = pltpu.VMEM((d, QBLK, lanes), F32)
    return _pallas(
        body, name=f"attn_bwd_d{d}", grid=(ATTN_W // lanes, nb + 1),
        in_specs=[cur, cur, prev, cur, prev, cur, cur, cur], out_specs=[cur, behind, behind], out_shape=[shape] * 3,
        scratch_shapes=[one] * 4 + [two] * 2 + [one] * 3 + [carry] * 2,
        semantics=("parallel", "arbitrary"), args=(q, k, k, v, v, o, lse, do), rider=rider)


def _dproj_merge(dqs, dks, dvs, du):
    T = du.shape[0]
    tm = 512

    def body(*refs):
        o_ref = refs[-1]
        for part in range(3):
            a, b, c = refs[3 * part:3 * part + 3]
            o_ref[:, part * ATTN_W:(part + 1) * ATTN_W] = (a[...] + b[...] + c[...]).astype(BF16)
        o_ref[:, 3 * ATTN_W:] = refs[9][...].astype(BF16)

    col = _rows(tm, ATTN_W)
    return pl.pallas_call(
        body, name="dproj_merge", grid=(T // tm,), in_specs=[col] * 10, out_specs=_rows(tm, 4 * ATTN_W),
        out_shape=jax.ShapeDtypeStruct((T, 4 * ATTN_W), BF16), compiler_params=_params("parallel"),
    )(*dqs, *dks, *dvs, du)


def _attention_fwd(q, k, v):
    res = [_attn_branch_fwd(q, k, v, d) for d in DILATIONS]
    return _attn_merge([r[0] for r in res], [r[1] for r in res])


SCAN_ROWS = 8
SCAN_LANES = 512
SSM_CHUNK = 512
SSM_CHUNK_BWD = 256
SSM_HALVES = tuple((slice(h * SSM_W // 2, (h + 1) * SSM_W // 2), slice(h * N_STATE // 2, (h + 1) * N_STATE // 2)) for h in range(2))


def _cmul(ar, ai, br, bi):
    return ar * br - ai * bi, ar * bi + ai * br


def _ssm_discretize(a_re, a_im, log_dt, b_re, b_im):
    def body(ar_ref, ai_ref, ldt_ref, br_ref, bi_ref, abr_ref, abi_ref, er_ref, ei_ref, bbr_ref, bbi_ref, pr_ref, pi_ref):
        ar, ai = ar_ref[...], ai_ref[...]
        dt = jnp.exp(ldt_ref[...])
        n = lax.broadcasted_iota(jnp.int32, (1, SCAN_ROWS), 1).astype(F32) + 1.0
        mag, ang = jnp.exp(dt * ar), dt * ai
        abr, abi = mag * jnp.cos(ang), mag * jnp.sin(ang)
        abr_ref[...], abi_ref[...] = abr, abi
        pr_ref[...] = jnp.exp(dt * ar * n) * jnp.cos(ang * n)
        pi_ref[...] = jnp.exp(dt * ar * n) * jnp.sin(ang * n)
        den = ar * ar + ai * ai
        er = ((abr - 1.0) * ar + abi * ai) / den
        ei = (abi * ar - (abr - 1.0) * ai) / den
        er_ref[...], ei_ref[...] = er, ei
        bbr_ref[...], bbi_ref[...] = _cmul(er, ei, br_ref[...], bi_ref[...])

    col = jax.ShapeDtypeStruct((N_STATE, 1), F32)
    mat = jax.ShapeDtypeStruct((N_STATE, GROUP_CH), F32)
    pw = jax.ShapeDtypeStruct((N_STATE, SCAN_ROWS), F32)
    return pl.pallas_call(body, name="ssm_discretize", out_shape=[col] * 4 + [mat] * 2 + [pw] * 2)(
        a_re, a_im, log_dt, b_re, b_im)


def _ssm_discretize_bwd(a_re, a_im, log_dt, b_re, b_im, ab_re, ab_im, e_re, e_im, dab_re, dab_im, dbb_re, dbb_im):
    def body(ar_ref, ai_ref, ldt_ref, br_ref, bi_ref, abr_ref, abi_ref, er_ref, ei_ref, dabr_ref, dabi_ref,
             dbbr_ref, dbbi_ref, dar_ref, dai_ref, ddt_ref, dbr_ref, dbi_ref):
        ar, ai, dt = ar_ref[...], ai_ref[...], jnp.exp(ldt_ref[...])
        er, ei = er_ref[...], ei_ref[...]
        gbr, gbi = dbbr_ref[...], dbbi_ref[...]
        dbr_ref[...], dbi_ref[...] = _cmul(er, -ei, gbr, gbi)
        br, bi = br_ref[...], bi_ref[...]
        der = jnp.sum(br * gbr + bi * gbi, axis=-1, keepdims=True)
        dei = jnp.sum(br * gbi - bi * gbr, axis=-1, keepdims=True)
        den = ar * ar + ai * ai
        inv_r, inv_i = ar / den, -ai / den
        t_r, t_i = _cmul(der, dei, inv_r, -inv_i)
        gab_r, gab_i = dabr_ref[...] + t_r, dabi_ref[...] + t_i
        q_r, q_i = _cmul(er, ei, inv_r, inv_i)
        dl_r, dl_i = _cmul(der, dei, q_r, -q_i)
        dl_r, dl_i = -dl_r, -dl_i
        gw_r, gw_i = _cmul(gab_r, gab_i, abr_ref[...], -abi_ref[...])
        dar_ref[...] = dl_r + dt * gw_r
        dai_ref[...] = dl_i + dt * gw_i
        ddt_ref[...] = (gw_r * ar + gw_i * ai) * dt

    col = jax.ShapeDtypeStruct((N_STATE, 1), F32)
    mat = jax.ShapeDtypeStruct((N_STATE, GROUP_CH), F32)
    return pl.pallas_call(body, name="ssm_discretize_bwd", out_shape=[col] * 3 + [mat] * 2)(
        a_re, a_im, log_dt, b_re, b_im, ab_re, ab_im, e_re, e_im, dab_re, dab_im, dbb_re, dbb_im)


def _scan_tables(p_re, p_im, reverse):
    pr, pi = p_re.T, p_im.T
    if reverse:
        pi = -pi
    row = jnp.arange(SCAN_ROWS)[:, None]
    level = lambda t, s: jnp.where((row < SCAN_ROWS - s) if reverse else (row >= s), t[s - 1][None, :], 0.0)
    carry = (pr[::-1], pi[::-1]) if reverse else (pr, pi)
    return jnp.stack([level(pr, 1), level(pi, 1), level(pr, 2), level(pi, 2), level(pr, 4), level(pi, 4), carry[0], carry[1]])


def _scan_group(xr, xi, tab_ref, ls, carry_r, carry_i, reverse):
    for n, s in enumerate((1, 2, 4)):
        shift = SCAN_ROWS - s if reverse else s
        mr, mi = _cmul(tab_ref[2 * n, :, ls], tab_ref[2 * n + 1, :, ls], pltpu.roll(xr, shift, 0), pltpu.roll(xi, shift, 0))
        xr, xi = xr + mr, xi + mi
    mr, mi = _cmul(tab_ref[6, :, ls], tab_ref[7, :, ls], carry_r, carry_i)
    return xr + mr, xi + mi


def _gelu(y):
    t = jnp.tanh(GELU_C * (y + 0.044715 * y * y * y))
    return 0.5 * y * (1.0 + t), t


def _ssm_fwd(u, tab, bd_re, bd_im, cd_re, cd_im, d_skip, w_glu, b_glu):
    T = u.shape[0]
    tc = SSM_CHUNK

    def body(u_ref, tab_ref, bdr_ref, bdi_ref, cdr_ref, cdi_ref, dsk_ref, wg_ref, bg_ref,
             sr_ref, si_ref, yp_ref, o_ref, car_r, car_i):
        @pl.when(pl.program_id(0) == 0)
        def _():
            car_r[...] = jnp.zeros_like(car_r)
            car_i[...] = jnp.zeros_like(car_i)

        uv = u_ref[...]
        for cs, ss in SSM_HALVES:
            sr_ref[:, ss] = _dot(uv[:, cs], bdr_ref[cs, ss])
            si_ref[:, ss] = _dot(uv[:, cs], bdi_ref[cs, ss])
        for lb in range(N_STATE // SCAN_LANES):
            ls = pl.ds(lb * SCAN_LANES, SCAN_LANES)

            def step(g, carry):
                rows = pl.ds(pl.multiple_of(g * SCAN_ROWS, SCAN_ROWS), SCAN_ROWS)
                xr, xi = _scan_group(sr_ref[rows, ls], si_ref[rows, ls], tab_ref, ls, carry[0], carry[1], False)
                sr_ref[rows, ls] = xr
                si_ref[rows, ls] = xi
                last = slice(SCAN_ROWS - 1, SCAN_ROWS)
                return (jnp.broadcast_to(xr[last], xr.shape), jnp.broadcast_to(xi[last], xi.shape))

            cr, ci = lax.fori_loop(0, tc // SCAN_ROWS, step, (car_r[:, ls], car_i[:, ls]))
            car_r[:, ls] = cr
            car_i[:, ls] = ci
        y = jnp.concatenate([_dot(sr_ref[:, ss], cdr_ref[ss, cs]) - _dot(si_ref[:, ss], cdi_ref[ss, cs])
                             for cs, ss in SSM_HALVES], axis=1) + dsk_ref[...] * uv
        yp_ref[...] = y
        gy, _ = _gelu(y)
        o_ref[...] = gy * jax.nn.sigmoid(_dot(gy, wg_ref[...]) + bg_ref[...])

    col, st = _rows(tc, SSM_W), _rows(tc, N_STATE)
    vec = _full((1, SSM_W))
    return pl.pallas_call(
        body, name="ssm_fwd", grid=(T // tc,),
        in_specs=[col, _full(tab.shape), _full(bd_re.shape), _full(bd_im.shape), _full(cd_re.shape), _full(cd_im.shape),
                  vec, _full(w_glu.shape), vec],
        out_specs=[st, st, col, col],
        out_shape=[jax.ShapeDtypeStruct((T, N_STATE), F32)] * 2 + [jax.ShapeDtypeStruct((T, SSM_W), F32)] * 2,
        scratch_shapes=[pltpu.VMEM((SCAN_ROWS, N_STATE), F32)] * 2,
        compiler_params=_params("arbitrary"),
    )(u, tab, bd_re, bd_im, cd_re, cd_im, d_skip, w_glu, b_glu)


def _ssm_bwd(dout, u, yp, s_re, s_im, tab, bd_re, bd_im, cd_re, cd_im, d_skip, w_glu, b_glu, rider=None):
    T = u.shape[0]
    tc = SSM_CHUNK_BWD
    nt = T // tc
    rows_per_chunk = tc // SCAN_ROWS

    def body(do_ref, u_ref, yp_ref, sr_ref, si_ref, pr_ref, pi_ref, tab_ref, dsk_ref, wg_ref, bg_ref,
             bdr_hbm, bdi_hbm, cdr_hbm, cdi_hbm,
             du_ref, dsk_out, dbg_out, dwg_out, da_out, dbdr_out, dbdi_out, dcdr_out, dcdi_out,
             bdr_v, bdi_v, cdr_v, cdi_v, dbdr_v, dbdi_v, dcdr_v, dcdi_v, gr_v, gi_v, car_r, car_i, sems):
        i = pl.program_id(0)

        @pl.when(i == 0)
        def _():
            _load_once([(bdr_hbm, bdr_v), (bdi_hbm, bdi_v), (cdr_hbm, cdr_v), (cdi_hbm, cdi_v)], sems)
            for ref in (dbdr_v, dbdi_v, dcdr_v, dcdi_v, car_r, car_i, dsk_out, dbg_out, dwg_out, da_out):
                ref[...] = jnp.zeros_like(ref)

        uv, y, dout_v = u_ref[...], yp_ref[...], do_ref[...]
        gy, t = _gelu(y)
        sg = jax.nn.sigmoid(_dot(gy, wg_ref[...]) + bg_ref[...])
        dzg = dout_v * gy * sg * (1.0 - sg)
        dgy = dout_v * sg + _dot_nt(dzg, wg_ref[...])
        dwg_out[...] += _dot_tn(gy, dzg)
        dbg_out[...] += jnp.sum(dzg, axis=0, keepdims=True)
        dy = dgy * (0.5 * (1.0 + t) + 0.5 * y * (1.0 - t * t) * GELU_C * (1.0 + 3 * 0.044715 * y * y))
        dsk_out[...] += jnp.sum(dy * uv, axis=0, keepdims=True)

        for cs, ss in SSM_HALVES:
            gr_v[:, ss] = _dot_nt(dy[:, cs], cdr_v[ss, cs])
            gi_v[:, ss] = -_dot_nt(dy[:, cs], cdi_v[ss, cs])
            dcdr_v[ss, cs] += _dot_tn(sr_ref[:, ss], dy[:, cs])
            dcdi_v[ss, cs] -= _dot_tn(si_ref[:, ss], dy[:, cs])

        row = lax.broadcasted_iota(jnp.int32, (SCAN_ROWS, SCAN_LANES), 0)
        first_chunk = i == nt - 1
        for lb in range(N_STATE // SCAN_LANES):
            ls = pl.ds(lb * SCAN_LANES, SCAN_LANES)

            def step(n, carry):
                g = rows_per_chunk - 1 - n
                rows = pl.ds(pl.multiple_of(g * SCAN_ROWS, SCAN_ROWS), SCAN_ROWS)
                before = pl.ds(pl.multiple_of(jnp.maximum(g - 1, 0) * SCAN_ROWS, SCAN_ROWS), SCAN_ROWS)
                xr, xi = _scan_group(gr_v[rows, ls], gi_v[rows, ls], tab_ref, ls, carry[0], carry[1], True)
                gr_v[rows, ls] = xr
                gi_v[rows, ls] = xi
                last = slice(SCAN_ROWS - 1, SCAN_ROWS)
                edge_r = jnp.where(g > 0, sr_ref[before, ls][last], jnp.where(first_chunk, 0.0, pr_ref[:, ls][last]))
                edge_i = jnp.where(g > 0, si_ref[before, ls][last], jnp.where(first_chunk, 0.0, pi_ref[:, ls][last]))
                spr = jnp.where(row >= 1, pltpu.roll(sr_ref[rows, ls], 1, 0), edge_r)
                spi = jnp.where(row >= 1, pltpu.roll(si_ref[rows, ls], 1, 0), edge_i)
                first = slice(0, 1)
                return (jnp.broadcast_to(xr[first], xr.shape), jnp.broadcast_to(xi[first], xi.shape),
                        carry[2] + xr * spr + xi * spi, carry[3] + xi * spr - xr * spi)

            zero = jnp.zeros((SCAN_ROWS, SCAN_LANES), F32)
            cr, ci, dar, dai = lax.fori_loop(0, rows_per_chunk, step, (car_r[:, ls], car_i[:, ls], zero, zero))
            car_r[:, ls] = cr
            car_i[:, ls] = ci
            da_out[0, :, ls] += dar
            da_out[1, :, ls] += dai

        du_ref[...] = dsk_ref[...] * dy + jnp.concatenate(
            [_dot_nt(gr_v[:, ss], bdr_v[cs, ss]) + _dot_nt(gi_v[:, ss], bdi_v[cs, ss]) for cs, ss in SSM_HALVES], axis=1)
        for cs, ss in SSM_HALVES:
            dbdr_v[cs, ss] += _dot_tn(uv[:, cs], gr_v[:, ss])
            dbdi_v[cs, ss] += _dot_tn(uv[:, cs], gi_v[:, ss])

        @pl.when(i == nt - 1)
        def _():
            for grp in range(N_GROUPS):
                ch, stt = pl.ds(grp * GROUP_CH, GROUP_CH), pl.ds(grp * STATE, STATE)
                dbdr_out[grp], dbdi_out[grp] = dbdr_v[ch, stt], dbdi_v[ch, stt]
                dcdr_out[grp], dcdi_out[grp] = dcdr_v[stt, ch], dcdi_v[stt, ch]

    rev = lambda i: (nt - 1 - i, 0)
    col = pl.BlockSpec((tc, SSM_W), rev)
    st = pl.BlockSpec((tc, N_STATE), rev)
    st_before = pl.BlockSpec((SCAN_ROWS, N_STATE), lambda i: (jnp.maximum((nt - 1 - i) * rows_per_chunk - 1, 0), 0))
    vec = _full((1, SSM_W))
    bd_blocks, cd_blocks = (N_GROUPS, GROUP_CH, STATE), (N_GROUPS, STATE, GROUP_CH)
    bd, cd = jax.ShapeDtypeStruct(bd_blocks, F32), jax.ShapeDtypeStruct(cd_blocks, F32)
    return _pallas(
        body, name="ssm_bwd", grid=(nt,),
        in_specs=[col, col, col, st, st, st_before, st_before, _full(tab.shape), vec, _full(w_glu.shape), vec,
                  ANY, ANY, ANY, ANY],
        out_specs=[col, vec, vec, _full(w_glu.shape), _full((2, SCAN_ROWS, N_STATE)),
                   _full(bd_blocks), _full(bd_blocks), _full(cd_blocks), _full(cd_blocks)],
        out_shape=[jax.ShapeDtypeStruct((T, SSM_W), F32), jax.ShapeDtypeStruct((1, SSM_W), F32),
                   jax.ShapeDtypeStruct((1, SSM_W), F32), jax.ShapeDtypeStruct(w_glu.shape, F32),
                   jax.ShapeDtypeStruct((2, SCAN_ROWS, N_STATE), F32), bd, bd, cd, cd],
        scratch_shapes=[pltpu.VMEM(bd_re.shape, BF16)] * 2 + [pltpu.VMEM(cd_re.shape, BF16)] * 2
        + [pltpu.VMEM(bd_re.shape, F32)] * 2 + [pltpu.VMEM(cd_re.shape, F32)] * 2
        + [pltpu.VMEM((tc, N_STATE), F32)] * 2 + [pltpu.VMEM((SCAN_ROWS, N_STATE), F32)] * 2
        + [pltpu.SemaphoreType.DMA((4,))],
        semantics=("arbitrary",), rider=rider,
        args=(dout, u, yp, s_re, s_im, s_re, s_im, tab, d_skip, w_glu, b_glu, bd_re, bd_im, cd_re, cd_im))


def _block_diagonals(parts):
    def body(*refs):
        for t_ref, o_ref in zip(refs[:len(parts)], refs[len(parts):]):
            _, a, b = t_ref.shape
            o_ref[...] = jnp.zeros_like(o_ref)
            for grp in range(N_GROUPS):
                o_ref[grp * a:(grp + 1) * a, grp * b:(grp + 1) * b] = t_ref[grp].astype(BF16)

    return pl.pallas_call(body, name="block_diagonals",
                          out_shape=[jax.ShapeDtypeStruct((N_GROUPS * t.shape[1], N_GROUPS * t.shape[2]), BF16) for t in parts])(*parts)


def _ssm_prepare(a_re, a_im, log_dt, b_re, b_im, c_re, c_im):
    col = lambda t: t.reshape(N_STATE, 1)
    ldt = jnp.broadcast_to(log_dt.reshape(N_GROUPS, 1), (N_GROUPS, STATE)).reshape(N_STATE, 1)
    b2r, b2i = b_re.reshape(N_STATE, GROUP_CH), b_im.reshape(N_STATE, GROUP_CH)
    ab_r, ab_i, e_r, e_i, bb_r, bb_i, p_r, p_i = _ssm_discretize(col(a_re), col(a_im), ldt, b2r, b2i)
    diag = _block_diagonals([jnp.swapaxes(t.reshape(N_GROUPS, STATE, GROUP_CH), 1, 2) for t in (bb_r, bb_i)]
                            + [jnp.swapaxes(t, 1, 2) for t in (c_re, c_im)])
    bd, cd = diag[:2], diag[2:]
    saved = dict(a_re=col(a_re), a_im=col(a_im), log_dt=ldt, b_re=b2r, b_im=b2i, ab_re=ab_r, ab_im=ab_i, e_re=e_r, e_im=e_i)
    return _scan_tables(p_r, p_i, False), _scan_tables(p_r, p_i, True), bd, cd, saved


BIG = ("ffn1_w_in", "ffn1_w_out", "w_mix_in", "w_glu", "w_mix_out", "ffn2_w_in", "ffn2_w_out")
SMALL = ("ffn1_pre_g", "ffn1_post_g", "mix_pre_g", "a_re", "a_im", "log_dt", "b_re", "b_im", "c_re", "c_im",
         "d_skip", "b_glu", "mix_post_g", "ffn2_pre_g", "ffn2_post_g")


FIRST = ("ffn1_w_in", "ffn1_w_out")
REST = ("w_mix_in", "w_glu", "w_mix_out", "ffn2_w_in", "ffn2_w_out")
LATE = ("w_mix_in", "w_glu", "w_mix_out", "ffn1_w_out")
SHARD_SHAPE = {"ffn1_w_in": (D_MODEL, FF_BLK), "ffn2_w_in": (D_MODEL, FF_BLK), "ffn1_w_out": (D_FF // N_CHIPS, D_MODEL),
               "ffn2_w_out": (D_FF // N_CHIPS, D_MODEL), "w_mix_in": (D_MODEL, ATTN_W), "w_glu": (SSM_W // N_CHIPS, SSM_W),
               "w_mix_out": (2 * ATTN_W // N_CHIPS, D_MODEL)}


class _Reduction:
    def __init__(self, names, grads, place):
        self.names, self.local, self.place = list(names), list(grads), place

    def exchange(self):
        return _pair_exchange(self.local)

    def first(self, got):
        both = [_pair_sum(a, b, self.place["core"], name=f"pair_sum_{n}") for n, a, b in zip(self.names, self.local, got)]
        self.pair = [f32 for f32, _ in both]
        return _reduce_first([wire for _, wire in both])

    def second(self, got):
        both = [_reduce_sum_first(a, b, self.place["sel_first"], name=f"sum_first_{n}")
                for n, a, b in zip(self.names, self.pair, got)]
        self.sums = [f32 for f32, _ in both]
        return _reduce_second([wire for _, wire in both])

    def swap(self, got):
        self.halves = [_reduce_sum_second(a, b, self.place["sel_second"], name=f"sum_second_{n}").reshape(2 * a.shape[2], a.shape[3])
                       for n, a, b in zip(self.names, self.sums, got)]
        return _pair_swap(self.halves)

    def done(self, got):
        return {n: (mine, theirs) for n, mine, theirs in zip(self.names, self.halves, got)}


def _local_step(x, target, p, w, place=None):
    vec = lambda t: t.reshape(1, -1)
    w1_in, w1_out = w["ffn1_w_in"], w["ffn1_w_out"].reshape(2, FF_BLK, D_MODEL)
    blocks = lambda n, t: t.reshape((N_CHIPS,) + SHARD_SHAPE[n])

    ffn1 = functools.partial(_ffn_fwd, x, vec(p["ffn1_pre_g"]), w1_in, w1_out, vec(p["ffn1_post_g"]), None, name="ffn1_fwd")
    if place is None:
        x1, z1, f1 = ffn1()
    else:
        (x1, z1, f1), rest = ffn1(rider=_gather_weights([w[n] for n in REST]))
        w = dict(w, **dict(zip(REST, rest)))
    w2_in, w2_out = w["ffn2_w_in"], w["ffn2_w_out"].reshape(2, FF_BLK, D_MODEL)
    w_mi, w_glu, w_mo = w["w_mix_in"], w["w_glu"].reshape(SSM_W, SSM_W), w["w_mix_out"].reshape(2, ATTN_W, D_MODEL)
    q, k, v, u = _mix_in_fwd(x1, vec(p["mix_pre_g"]), w_mi)
    attn, lse = _attention_fwd(q, k, v)
    tab_f, tab_b, bd, cd, sv = _ssm_prepare(p["a_re"], p["a_im"], p["log_dt"], p["b_re"], p["b_im"], p["c_re"], p["c_im"])
    ssm_args = (bd[0], bd[1], cd[0], cd[1], vec(p["d_skip"]), w_glu, vec(p["b_glu"]))
    s_re, s_im, yp, ssm = _ssm_fwd(u, tab_f, *ssm_args)
    x2, mixed = _mix_out_fwd(x1, attn, ssm, w_mo, vec(p["mix_post_g"]))
    dx3, loss_rows, z2, f2 = _ffn_fwd(x2, vec(p["ffn2_pre_g"]), w2_in, w2_out, vec(p["ffn2_post_g"]), target, name="ffn2_fwd")

    g = {}
    ride = (lambda call, exchange: call(rider=exchange)) if place else (lambda call, exchange: (call(), None))
    dz2, df2, g["ffn2_post_g"] = _ffn_bwd_out(dx3, f2, z2, w2_out, vec(p["ffn2_post_g"]), name="ffn2_bwd_out")
    dw2_out = _ffn_dw_out(z2, df2, name="ffn2_bwd_dw_out")
    dw2_in = _norm_matmul_dw(dz2, x2, vec(p["ffn2_pre_g"]), N_CHIPS, name="ffn2_bwd_dw")
    early = _Reduction(("ffn2_w_in", "ffn2_w_out"), [dw2_in, blocks("ffn2_w_out", dw2_out)], place) if place else None
    (dx2, g["ffn2_pre_g"]), got = ride(
        functools.partial(_norm_matmul_dx, dz2, x2, dx3, vec(p["ffn2_pre_g"]), w2_in, name="ffn2_bwd_dx"), early and early.exchange())
    dattn, dssm, dw_mo, g["mix_post_g"] = _mix_out_bwd(dx2, mixed, attn, ssm, w_mo, vec(p["mix_post_g"]))
    (du, g["d_skip"], g["b_glu"], dw_glu, da, dbd_re, dbd_im, dcd_re, dcd_im), got = ride(
        functools.partial(_ssm_bwd, dssm, u, yp, s_re, s_im, tab_b, *ssm_args), early and early.first(got))
    branch = lambda d: functools.partial(_attn_branch_bwd, q, k, v, attn, lse, dattn, d)
    parts = [None] * 3
    parts[0], got = ride(branch(DILATIONS[0]), early and early.second(got))
    parts[1], early_theirs = ride(branch(DILATIONS[1]), early and early.swap(got))
    parts[2] = branch(DILATIONS[2])()
    dproj = _dproj_merge([r[0] for r in parts], [r[1] for r in parts], [r[2] for r in parts], du)
    dw_mi = _norm_matmul_dw(dproj, x1, vec(p["mix_pre_g"]), N_CHIPS, name="mix_bwd_dw")
    dx1, g["mix_pre_g"] = _norm_matmul_dx(dproj, x1, dx2, vec(p["mix_pre_g"]), w_mi, name="mix_bwd_dx")
    dz1, df1, g["ffn1_post_g"] = _ffn_bwd_out(dx1, f1, z1, w1_out, vec(p["ffn1_post_g"]), name="ffn1_bwd_out")
    dw1_out = _ffn_dw_out(z1, df1, name="ffn1_bwd_dw_out")
    big = {"ffn2_w_in": dw2_in, "ffn2_w_out": dw2_out, "w_mix_in": dw_mi, "w_glu": dw_glu, "w_mix_out": dw_mo, "ffn1_w_out": dw1_out}
    late = _Reduction(LATE, [blocks(n, big[n]) for n in LATE], place) if place else None
    big["ffn1_w_in"], got = ride(
        functools.partial(_norm_matmul_dw, dz1, x, vec(p["ffn1_pre_g"]), N_CHIPS, name="ffn1_bwd_dw"), late and late.exchange())
    dx_call = functools.partial(_norm_matmul_dx, dz1, x, dx1, vec(p["ffn1_pre_g"]), w1_in)
    if place:
        last = _Reduction(("ffn1_w_in",), [big["ffn1_w_in"]], place)
        n_late = len(LATE)
        (dx_part, dg_a), got = dx_call(name="ffn1_bwd_dx_a", part=(0, 0.375), rider=_merged(late.first(got), last.exchange()))
        (dx_part, dg_b), got = dx_call(name="ffn1_bwd_dx_b", part=(0.375, 0.75), into=dx_part,
                                       rider=_merged(late.second(got[:n_late]), last.first(got[n_late:])))
        grad_x, dg_c = dx_call(name="ffn1_bwd_dx_c", part=(0.75, 1), into=dx_part)
        g["ffn1_pre_g"] = dg_a + dg_b + dg_c
        got = _merged(late.swap(got[:n_late]), last.second(got[n_late:])).run("tail_second")
        g.update(late.done(got[:n_late]))
        g.update(last.done(last.swap(got[n_late:]).run("tail_swap")))
        g.update(early.done(early_theirs))
    else:
        grad_x, g["ffn1_pre_g"] = dx_call(name="ffn1_bwd_dx")
        g.update({n: blocks(n, big[n]) for n in BIG})

    g["c_re"], g["c_im"] = (jnp.swapaxes(m, 1, 2) for m in (dcd_re, dcd_im))
    dbb = [jnp.swapaxes(m, 1, 2).reshape(N_STATE, GROUP_CH) for m in (dbd_re, dbd_im)]
    dab = [jnp.sum(da[n], axis=0).reshape(N_STATE, 1) for n in range(2)]
    da_re, da_im, dldt, db_re, db_im = _ssm_discretize_bwd(
        sv["a_re"], sv["a_im"], sv["log_dt"], sv["b_re"], sv["b_im"], sv["ab_re"], sv["ab_im"], sv["e_re"], sv["e_im"],
        dab[0], dab[1], dbb[0], dbb[1])
    g["a_re"], g["a_im"] = da_re.reshape(N_GROUPS, STATE), da_im.reshape(N_GROUPS, STATE)
    g["log_dt"] = jnp.sum(dldt.reshape(N_GROUPS, STATE), axis=-1)
    g["b_re"], g["b_im"] = (t.reshape(N_GROUPS, STATE, GROUP_CH) for t in (db_re, db_im))
    return loss_rows, grad_x, g


MESH = pl.DeviceIdType.MESH
N_REL = 3


def _place():
    x, y, c = lax.axis_index("x"), lax.axis_index("y"), lax.axis_index("c")
    return x, y, c, [(1 - x, y), (x, 1 - y), (1 - x, 1 - y)]


def _remote(src, dst, send_sems, recv_sems, idx, to):
    return pltpu.make_async_remote_copy(src_ref=src, dst_ref=dst, send_sem=send_sems.at[idx], recv_sem=recv_sems.at[idx],
                                        device_id=to, device_id_type=MESH)


def _half(rows, who):
    return pl.ds(who * (rows // 2), rows // 2)


class _Copies:
    def __init__(self, send_sems, recv_sems, local_sems):
        self.send_sems, self.recv_sems, self.local_sems = send_sems, recv_sems, local_sems
        self.n_remote = self.n_local = 0

    def remote(self, src, dst, to):
        k, self.n_remote = self.n_remote, self.n_remote + 1
        return pltpu.make_async_remote_copy(src_ref=src, dst_ref=dst, send_sem=self.send_sems.at[k],
                                            recv_sem=self.recv_sems.at[k], device_id=to, device_id_type=MESH)

    def local(self, src, dst):
        k, self.n_local = self.n_local, self.n_local + 1
        return pltpu.make_async_copy(src, dst, self.local_sems.at[k])


class _Exchange:
    def __init__(self, plan, ins, out_shapes, n_remote):
        self.plan, self.ins, self.out_shapes, self.n_remote = plan, list(ins), list(out_shapes), n_remote
        self.sems = [pltpu.SemaphoreType.DMA((n_remote,)), pltpu.SemaphoreType.DMA((n_remote,)), pltpu.SemaphoreType.DMA((1,))]

    def run(self, name):
        n_in = len(self.ins)

        def body(*refs):
            for phase in self.plan(refs[:n_in], refs[n_in:-3], _Copies(*refs[-3:])):
                for cp in phase:
                    cp.start()
                for cp in phase:
                    cp.wait()

        return pl.pallas_call(body, name=name, in_specs=[ANY] * n_in, out_specs=[ANY] * len(self.out_shapes),
                              out_shape=self.out_shapes, scratch_shapes=self.sems)(*self.ins)


def _merged(a, b):
    n_in, n_out = len(a.ins), len(a.out_shapes)

    def plan(ins, outs, mk):
        (phase_a,), (phase_b,) = a.plan(ins[:n_in], outs[:n_out], mk), b.plan(ins[n_in:], outs[n_out:], mk)
        return [phase_a + phase_b]

    return _Exchange(plan, a.ins + b.ins, a.out_shapes + b.out_shapes, a.n_remote + b.n_remote)


def _pallas(body, *, name, grid, in_specs, out_specs, out_shape, args, semantics, scratch_shapes=(), rider=None, aliases=None):
    aliases = aliases or {}
    if rider is None:
        return pl.pallas_call(body, name=name, grid=grid, in_specs=in_specs, out_specs=out_specs, out_shape=out_shape,
                              scratch_shapes=list(scratch_shapes), input_output_aliases=aliases,
                              compiler_params=_params(*semantics))(*args)
    n_in, n_out, r_in, r_out = len(in_specs), len(out_specs), len(rider.ins), len(rider.out_shapes)
    n_steps = math.prod(grid)

    def carrier(*refs):
        ins, rider_ins = refs[:n_in], refs[n_in:n_in + r_in]
        outs = refs[n_in + r_in:n_in + r_in + n_out]
        rider_outs = refs[n_in + r_in + n_out:n_in + r_in + n_out + r_out]
        scratch, sems = refs[n_in + r_in + n_out + r_out:-3], refs[-3:]
        step = 0
        for axis, size in enumerate(grid):
            step = step * size + pl.program_id(axis)
        phases = rider.plan(rider_ins, rider_outs, _Copies(*sems))

        def start_phase(p):
            for cp in (phases[p - 1] if p else []):
                cp.wait()
            for cp in phases[p]:
                cp.start()

        for p in range(len(phases)):
            pl.when(step == p * n_steps // len(phases))(functools.partial(start_phase, p))
        body(*ins, *outs, *scratch)

        @pl.when(step == n_steps - 1)
        def _():
            for cp in phases[-1]:
                cp.wait()

    results = pl.pallas_call(
        carrier, name=name, grid=grid, in_specs=list(in_specs) + [ANY] * r_in, out_specs=list(out_specs) + [ANY] * r_out,
        out_shape=list(out_shape) + rider.out_shapes, scratch_shapes=list(scratch_shapes) + rider.sems,
        input_output_aliases=aliases, compiler_params=_params(*["arbitrary"] * len(grid)))(*args, *rider.ins)
    return results[:n_out], results[n_out:]


def _gather_weights(shards):
    def plan(ins, outs, mk):
        x, y, c = lax.axis_index("x"), lax.axis_index("y"), lax.axis_index("c")
        me, sibling = 2 * x + y, (x, y, 1 - c)
        x_nb, y_nb, diag = (1 - x, y), (x, 1 - y), (1 - x, 1 - y)
        index = lambda chip: 2 * chip[0] + chip[1]
        first, second, third = [], [], []
        for i, shard in enumerate(shards):
            rows = shard.shape[0]
            mine = _half(rows, c)
            quarter = lambda which: pl.ds(c * (rows // 2) + which * (rows // 4), rows // 4)
            first.append(mk.remote(ins[i], outs[i].at[me], sibling))
            for nb in (x_nb, y_nb):
                first.append(mk.remote(ins[i].at[mine], outs[i].at[me, mine], (*nb, c)))
            for nb in (x_nb, y_nb):
                landed = outs[i].at[index(nb), mine]
                second.append(mk.remote(landed, landed, sibling))
            for nb, other, which in ((x_nb, y_nb, 0), (y_nb, x_nb, 1)):
                landed = outs[i].at[index(nb), quarter(which)]
                second.append(mk.remote(landed, landed, (*other, c)))
            landed = outs[i].at[index(diag), mine]
            third.append(mk.remote(landed, landed, sibling))
        return [first, second, third]

    return _Exchange(plan, shards, [jax.ShapeDtypeStruct((N_CHIPS,) + s.shape, s.dtype) for s in shards], 8 * len(shards))


def _pair_exchange(grads):
    def plan(ins, outs, mk):
        x, y, c = lax.axis_index("x"), lax.axis_index("y"), lax.axis_index("c")
        return [[mk.remote(ins[i].at[:, _half(g.shape[1], 1 - c)], outs[i], (x, y, 1 - c)) for i, g in enumerate(grads)]]

    return _Exchange(plan, grads, [jax.ShapeDtypeStruct((N_CHIPS, g.shape[1] // 2, g.shape[2]), g.dtype) for g in grads], len(grads))


def _reduce_first(pair):
    def plan(ins, outs, mk):
        x, y, c = lax.axis_index("x"), lax.axis_index("y"), lax.axis_index("c")
        phase = []
        for i, p in enumerate(pair):
            q = p.shape[1] // 2
            phase.append(mk.remote(ins[i].at[pl.ds(2 * (1 - x), 2), pl.ds(0, q)], outs[i].at[0], (1 - x, y, c)))
            for jx in range(2):
                phase.append(mk.remote(ins[i].at[2 * jx + 1 - y, pl.ds(q, q)], outs[i].at[1, jx], (x, 1 - y, c)))
        return [phase]

    return _Exchange(plan, pair, [jax.ShapeDtypeStruct((2, 2, p.shape[1] // 2, p.shape[2]), p.dtype) for p in pair], 3 * len(pair))


def _reduce_second(sums):
    def plan(ins, outs, mk):
        x, y, c = lax.axis_index("x"), lax.axis_index("y"), lax.axis_index("c")
        phase = []
        for i in range(len(sums)):
            phase.append(mk.remote(ins[i].at[0, 1 - y], outs[i].at[0], (x, 1 - y, c)))
            phase.append(mk.remote(ins[i].at[1, 1 - x], outs[i].at[1], (1 - x, y, c)))
        return [phase]

    return _Exchange(plan, sums, [jax.ShapeDtypeStruct((2,) + s.shape[2:], s.dtype) for s in sums], 2 * len(sums))


def _pair_swap(halves):
    def plan(ins, outs, mk):
        x, y, c = lax.axis_index("x"), lax.axis_index("y"), lax.axis_index("c")
        return [[mk.remote(ins[i], outs[i], (x, y, 1 - c)) for i in range(len(halves))]]

    return _Exchange(plan, halves, [jax.ShapeDtypeStruct(h.shape, h.dtype) for h in halves], len(halves))


def _allreduce_small(packed):
    rows = packed.shape[0]
    n_dev = 2 * N_CHIPS

    def body(x_ref, o_ref, buf, send_sems, recv_sems):
        x, y, c, chips = _place()
        sibling = (x, y, 1 - c)

        def slot(px, py, pc):
            return buf.at[4 * px + 2 * py + pc]

        buf[4 * x + 2 * y + c] = x_ref[...]
        first = [_remote(x_ref, slot(x, y, c), send_sems, recv_sems, 0, sibling)]
        first += [_remote(x_ref, slot(x, y, c), send_sems, recv_sems, 1 + k, (*chip, c)) for k, chip in enumerate(chips)]
        for cp in first:
            cp.start()
        passed = []
        for k, chip in enumerate(chips):
            landed = slot(*chip, c)
            _remote(landed, landed, send_sems, recv_sems, 1 + k, (*chip, c)).wait_recv()
            passed.append(_remote(landed, landed, send_sems, recv_sems, 1 + N_REL + k, sibling))
            passed[-1].start()
        _remote(slot(*sibling), slot(*sibling), send_sems, recv_sems, 0, sibling).wait_recv()
        for k, chip in enumerate(chips):
            landed = slot(*chip, 1 - c)
            _remote(landed, landed, send_sems, recv_sems, 1 + N_REL + k, sibling).wait_recv()
        for cp in first + passed:
            cp.wait_send()
        total = buf[0]
        for d in range(1, n_dev):
            total = total + buf[d]
        o_ref[...] = total

    vm = pl.BlockSpec(memory_space=pltpu.VMEM)
    return pl.pallas_call(
        body, name="allreduce_small", in_specs=[vm], out_specs=vm, out_shape=jax.ShapeDtypeStruct(packed.shape, F32),
        scratch_shapes=[pltpu.VMEM((n_dev, rows, 128), F32), pltpu.SemaphoreType.DMA((1 + 2 * N_REL,)),
                        pltpu.SemaphoreType.DMA((1 + 2 * N_REL,))],
    )(packed)


def _row_tile(rows, cap=256):
    return max(t for t in range(8, cap + 1, 8) if rows % t == 0)


def _pair_sum(grad, got, c, name):
    _, half, cols = got.shape
    tr = _row_tile(half)
    nt = half // tr

    def body(c_ref, g_ref, r_ref, o_ref, wire_ref):
        total = g_ref[...] + r_ref[...]
        o_ref[...] = total
        wire_ref[...] = total.astype(BF16)

    blk = (1, tr, cols)
    out = pl.BlockSpec(blk, lambda j, t, c_ref: (j, t, 0))
    return pl.pallas_call(
        body, name=name,
        grid_spec=pltpu.PrefetchScalarGridSpec(
            num_scalar_prefetch=1, grid=(N_CHIPS, nt),
            in_specs=[pl.BlockSpec(blk, lambda j, t, c_ref: (j, c_ref[0] * nt + t, 0)), out], out_specs=[out, out]),
        out_shape=[jax.ShapeDtypeStruct(got.shape, F32), jax.ShapeDtypeStruct(got.shape, BF16)],
        compiler_params=_params("parallel", "parallel"),
    )(c, grad, got)


def _reduce_sum_first(pair, got, sel, name):
    _, _, q, cols = got.shape
    tr = _row_tile(q)
    nt = q // tr

    def body(sel_ref, p_ref, r_ref, o_ref, wire_ref):
        total = p_ref[0] + r_ref[0, 0].astype(F32)
        o_ref[0, 0] = total
        wire_ref[0, 0] = total.astype(BF16)

    blk = pl.BlockSpec((1, 1, tr, cols), lambda p, k, t, s: (p, k, t, 0))
    return pl.pallas_call(
        body, name=name,
        grid_spec=pltpu.PrefetchScalarGridSpec(
            num_scalar_prefetch=1, grid=(2, 2, nt),
            in_specs=[pl.BlockSpec((1, tr, cols), lambda p, k, t, s: (s[2 * p] + s[2 * p + 1] * k, p * nt + t, 0)), blk],
            out_specs=[blk, blk]),
        out_shape=[jax.ShapeDtypeStruct(got.shape, F32), jax.ShapeDtypeStruct(got.shape, BF16)],
        compiler_params=_params("parallel", "parallel", "parallel"),
    )(sel, pair, got)


def _reduce_sum_second(sums, got, sel, name):
    _, q, cols = got.shape
    tr = _row_tile(q)

    def body(sel_ref, s_ref, r_ref, o_ref):
        o_ref[0] = s_ref[0, 0] + r_ref[0].astype(F32)

    blk = (1, tr, cols)
    return pl.pallas_call(
        body, name=name,
        grid_spec=pltpu.PrefetchScalarGridSpec(
            num_scalar_prefetch=1, grid=(2, q // tr),
            in_specs=[pl.BlockSpec((1, 1, tr, cols), lambda p, t, s: (p, s[p], t, 0)),
                      pl.BlockSpec(blk, lambda p, t, s: (p, t, 0))],
            out_specs=pl.BlockSpec(blk, lambda p, t, s: (p, t, 0))),
        out_shape=jax.ShapeDtypeStruct(got.shape, F32), compiler_params=_params("parallel", "parallel"),
    )(sel, sums, got)


def _adamw_update(w, g, m, v):
    m2 = ADAM_B1 * m + (1.0 - ADAM_B1) * g
    v2 = ADAM_B2 * v + (1.0 - ADAM_B2) * (g * g)
    m_hat = m2 / (1.0 - ADAM_B1 ** ADAM_STEP)
    v_hat = v2 / (1.0 - ADAM_B2 ** ADAM_STEP)
    return -ADAM_LR * (m_hat / (jnp.sqrt(v_hat) + ADAM_EPS) + ADAM_WD * w), m2, v2


def _adamw_small(ws, gs, ms, vs):
    n = len(ws)

    def body(*refs):
        w, g, m, v, d, mo, vo = (refs[k * n:(k + 1) * n] for k in range(7))
        for i in range(n):
            d[i][...], mo[i][...], vo[i][...] = _adamw_update(w[i][...], g[i][...], m[i][...], v[i][...])

    shapes = [jax.ShapeDtypeStruct(t.shape, F32) for t in ws]
    outs = pl.pallas_call(body, name="adamw_small", out_shape=shapes * 3,
                          compiler_params=pltpu.CompilerParams(vmem_limit_bytes=VMEM_LIMIT_V7X))(*ws, *gs, *ms, *vs)
    return outs[:n], outs[n:2 * n], outs[2 * n:]


def _adamw_halves(w, mine, theirs, m, v, core, name):
    rows, cols = w.shape
    tr = _row_tile(rows // 2)
    per_half = rows // 2 // tr

    def body(core_ref, w_ref, a_ref, b_ref, m_ref, v_ref, g_ref, d_ref, mo_ref, vo_ref):
        g = jnp.where(pl.program_id(0) // per_half == core_ref[0], a_ref[...], b_ref[...])
        g_ref[...] = g
        d_ref[...], mo_ref[...], vo_ref[...] = _adamw_update(w_ref[...], g, m_ref[...], v_ref[...])

    blk = pl.BlockSpec((tr, cols), lambda t, c: (t, 0))
    half = lambda own: pl.BlockSpec(
        (tr, cols), lambda t, c: (jnp.clip(t - (c[0] if own else 1 - c[0]) * per_half, 0, per_half - 1), 0))
    return pl.pallas_call(
        body, name=name,
        grid_spec=pltpu.PrefetchScalarGridSpec(
            num_scalar_prefetch=1, grid=(2 * per_half,), in_specs=[blk, half(True), half(False), blk, blk], out_specs=[blk] * 4),
        out_shape=[jax.ShapeDtypeStruct(w.shape, F32)] * 4, compiler_params=_params("arbitrary"),
    )(core, w, mine, theirs, m, v)


def _pack(parts):
    flat = []
    for t in parts:
        t = t.reshape(-1).astype(F32)
        flat.append(jnp.pad(t, (0, -t.shape[0] % 128)))
    flat = jnp.concatenate(flat)
    return jnp.pad(flat, (0, -flat.shape[0] % 1024)).reshape(-1, 128)


def _unpack(buf, shapes):
    flat, out, at = buf.reshape(-1), [], 0
    for s in shapes:
        size = math.prod(s)
        out.append(flat[at:at + size].reshape(s))
        at += size + (-size % 128)
    return out


def kernel(x, ffn1_pre_g, ffn1_w_in, ffn1_w_out, ffn1_post_g, mix_pre_g, w_mix_in, a_re, a_im, log_dt, b_re, b_im, c_re, c_im, d_skip, w_glu, b_glu, w_mix_out, mix_post_g, ffn2_pre_g, ffn2_w_in, ffn2_w_out, ffn2_post_g, loss_target, m_ffn1_pre_g, m_ffn1_w_in, m_ffn1_w_out, m_ffn1_post_g, m_mix_pre_g, m_w_mix_in, m_a_re, m_a_im, m_log_dt, m_b_re, m_b_im, m_c_re, m_c_im, m_d_skip, m_w_glu, m_b_glu, m_w_mix_out, m_mix_post_g, m_ffn2_pre_g, m_ffn2_w_in, m_ffn2_w_out, m_ffn2_post_g, v_ffn1_pre_g, v_ffn1_w_in, v_ffn1_w_out, v_ffn1_post_g, v_mix_pre_g, v_w_mix_in, v_a_re, v_a_im, v_log_dt, v_b_re, v_b_im, v_c_re, v_c_im, v_d_skip, v_w_glu, v_b_glu, v_w_mix_out, v_mix_post_g, v_ffn2_pre_g, v_ffn2_w_in, v_ffn2_w_out, v_ffn2_post_g):
    given = dict(locals())
    order = ("ffn1_pre_g", "ffn1_w_in", "ffn1_w_out", "ffn1_post_g", "mix_pre_g", "w_mix_in", "a_re", "a_im", "log_dt",
             "b_re", "b_im", "c_re", "c_im", "d_skip", "w_glu", "b_glu", "w_mix_out", "mix_post_g", "ffn2_pre_g",
             "ffn2_w_in", "ffn2_w_out", "ffn2_post_g")
    at_x, at_y, at_c = (lax.axis_index(a).astype(jnp.int32) for a in ("x", "y", "c"))
    place = dict(core=at_c.reshape(1), sel_first=jnp.stack([2 * at_x, jnp.int32(1), at_y, jnp.int32(2)]),
                 sel_second=jnp.stack([at_y, at_x]))

    shards = {n: given[n][0] for n in BIG}
    w = {n: shards[n].astype(BF16) for n in REST}
    w.update(zip(FIRST, _gather_weights([shards[n].astype(BF16) for n in FIRST]).run("gather_first")))
    small = {n: given[n][0] for n in SMALL}
    loss_rows, grad_x, g = _local_step(x[0], loss_target[0], small, w, place)

    total = _allreduce_small(_pack([g[n] for n in SMALL] + [loss_rows[0, :1]]))
    parts = _unpack(total, [small[n].shape for n in SMALL] + [(1,)])
    grads = dict(zip(SMALL, parts[:-1]))
    loss = parts[-1][0]

    delta, new_m, new_v = {}, {}, {}
    for n in BIG:
        grads[n], delta[n], new_m[n], new_v[n] = _adamw_halves(
            shards[n], *g[n], given["m_" + n][0], given["v_" + n][0], place["core"], name=f"adamw_{n}")
    take = lambda pre: [given[pre + n] for n in SMALL]
    outs = _adamw_small(take(""), [grads[n][None] for n in SMALL], take("m_"), take("v_"))
    for store, arrays in zip((delta, new_m, new_v), outs):
        store.update({n: t[0] for n, t in zip(SMALL, arrays)})

    lead = lambda d: [d[n][None] for n in order]
    return (loss, grad_x[None], *lead(grads), *lead(delta), *lead(new_m), *lead(new_v))
```

```python
import functools
import math

import jax
import jax.numpy as jnp
from jax import lax
from jax.experimental import pallas as pl
from jax.experimental.pallas import tpu as pltpu

F32, BF16 = jnp.float32, jnp.bfloat16

D_MODEL = 1024
D_FF = 2816
N_CHIPS = 4
FF_BLK = 2 * D_FF // N_CHIPS
ATTN_W = 512
SSM_W = 512
HEAD_DIM = 64
N_HEADS = ATTN_W // HEAD_DIM
DILATIONS = (1, 4, 16)
N_BACK = 128
QBLK = 128
N_GROUPS = 32
GROUP_CH = 16
STATE = 64
N_STATE = N_GROUPS * STATE
EPS = 1e-6
NEG = -1e30
GELU_C = math.sqrt(2.0 / math.pi)

ADAM_LR, ADAM_B1, ADAM_B2, ADAM_EPS, ADAM_WD, ADAM_STEP = 0.001, 0.9, 0.999, 1e-08, 0.01, 10

VMEM_LIMIT_V7X = 60 * 1024 * 1024
ROW_TILE = 512
FFN_ROW_TILE = 512
DW_ROW_TILE = 1024


def _params(*sem):
    return pltpu.CompilerParams(dimension_semantics=sem, vmem_limit_bytes=VMEM_LIMIT_V7X)


def _dot(a, b):
    return jnp.dot(a.astype(BF16), b.astype(BF16), preferred_element_type=F32)


def _dot_nt(a, b):
    return lax.dot_general(a.astype(BF16), b.astype(BF16), (((1,), (1,)), ((), ())), preferred_element_type=F32)


def _dot_tn(a, b):
    return lax.dot_general(a.astype(BF16), b.astype(BF16), (((0,), (0,)), ((), ())), preferred_element_type=F32)


def _full(shape):
    return pl.BlockSpec(shape, lambda *_: (0,) * len(shape))


def _rows(tm, width):
    return pl.BlockSpec((tm, width), lambda i: (i, 0))


ANY = pl.BlockSpec(memory_space=pl.ANY)


def _load_once(pairs, sems):
    copies = [pltpu.make_async_copy(src, dst, sems.at[k]) for k, (src, dst) in enumerate(pairs)]
    for c in copies:
        c.start()
    for c in copies:
        c.wait()


def _rms(x):
    return lax.rsqrt(jnp.mean(x * x, axis=-1, keepdims=True) + EPS)


def _rms_bwd(dy_g, xn, r):
    return r * (dy_g - xn * jnp.mean(dy_g * xn, axis=-1, keepdims=True))


def _ffn_fwd(x, g_pre, w_in, w_out, g_post, target, *, name, rider=None):
    T = x.shape[0]
    tm = FFN_ROW_TILE
    with_loss = target is not None

    def body(*refs):
        if with_loss:
            x_ref, gpre_ref, gpost_ref, tgt_ref, win_hbm, wout_hbm, o_ref, loss_ref, z_ref, f_ref, win_v, wout_v, sems = refs
        else:
            x_ref, gpre_ref, gpost_ref, win_hbm, wout_hbm, o_ref, z_ref, f_ref, win_v, wout_v, sems = refs

        @pl.when(pl.program_id(0) == 0)
        def _():
            _load_once([(win_hbm, win_v), (wout_hbm, wout_v)], sems)
            if with_loss:
                loss_ref[...] = jnp.zeros_like(loss_ref)

        xv = x_ref[...]
        h = (xv * _rms(xv) * gpre_ref[...]).astype(BF16)
        f = jnp.zeros((tm, D_MODEL), F32)
        for k in range(2):
            gate = _dot(h, win_v[k])
            up = _dot(h, win_v[k + 2])
            z_ref[:, k * FF_BLK:(k + 1) * FF_BLK] = gate.astype(BF16)
            z_ref[:, D_FF + k * FF_BLK:D_FF + (k + 1) * FF_BLK] = up.astype(BF16)
            f = f + _dot(gate * jax.nn.sigmoid(gate) * up, wout_v[k])
        f_ref[...] = f
        out = xv + 0.5 * (f * _rms(f) * gpost_ref[...])
        if with_loss:
            err = out - tgt_ref[...]
            o_ref[...] = err * (1.0 / D_MODEL)
            loss_ref[...] += jnp.sum(err * err) * (0.5 / D_MODEL)
        else:
            o_ref[...] = out

    row = _rows(tm, D_MODEL)
    vec = _full((1, D_MODEL))
    in_specs = [row, vec, vec] + ([row] if with_loss else []) + [ANY, ANY]
    out_shape = [jax.ShapeDtypeStruct((T, D_MODEL), F32)]
    out_specs = [row]
    if with_loss:
        out_shape.append(jax.ShapeDtypeStruct((8, 128), F32))
        out_specs.append(_full((8, 128)))
    out_shape += [jax.ShapeDtypeStruct((T, 2 * D_FF), BF16), jax.ShapeDtypeStruct((T, D_MODEL), F32)]
    out_specs += [_rows(tm, 2 * D_FF), row]
    args = (x, g_pre, g_post) + ((target,) if with_loss else ()) + (w_in, w_out)
    return _pallas(
        body, name=name, grid=(T // tm,), in_specs=in_specs, out_specs=out_specs, out_shape=out_shape,
        scratch_shapes=[pltpu.VMEM(w_in.shape, BF16), pltpu.VMEM(w_out.shape, BF16), pltpu.SemaphoreType.DMA((2,))],
        semantics=("arbitrary",), args=args, rider=rider)


def _ffn_bwd_out(dout, f, z, w_out, g_post, *, name):
    T = dout.shape[0]
    tm = FFN_ROW_TILE
    nt = T // tm

    def body(dout_ref, f_ref, z_ref, gpost_ref, wout_hbm, dz_ref, dgpost_ref, dwout_hbm, wout_v, dwout_v, sems):
        i = pl.program_id(0)

        @pl.when(i == 0)
        def _():
            _load_once([(wout_hbm, wout_v)], sems)
            dwout_v[...] = jnp.zeros_like(dwout_v)
            dgpost_ref[...] = jnp.zeros_like(dgpost_ref)

        dy = 0.5 * dout_ref[...]
        f = f_ref[...]
        r = _rms(f)
        fn = f * r
        dgpost_ref[...] += jnp.sum(dy * fn, axis=0, keepdims=True)
        df = _rms_bwd(dy * gpost_ref[...], fn, r).astype(BF16)
        for k in range(2):
            gate = z_ref[:, k * FF_BLK:(k + 1) * FF_BLK].astype(F32)
            up = z_ref[:, D_FF + k * FF_BLK:D_FF + (k + 1) * FF_BLK].astype(F32)
            sg = jax.nn.sigmoid(gate)
            silu = gate * sg
            dwout_v[k] += _dot_tn(silu * up, df)
            da = _dot_nt(df, wout_v[k])
            dz_ref[:, k * FF_BLK:(k + 1) * FF_BLK] = (da * up * (sg * (1.0 + gate * (1.0 - sg)))).astype(BF16)
            dz_ref[:, D_FF + k * FF_BLK:D_FF + (k + 1) * FF_BLK] = (da * silu).astype(BF16)

        @pl.when(i == nt - 1)
        def _():
            c = pltpu.make_async_copy(dwout_v, dwout_hbm, sems.at[0])
            c.start()
            c.wait()

    row = _rows(tm, D_MODEL)
    return pl.pallas_call(
        body, name=name, grid=(nt,),
        in_specs=[row, row, _rows(tm, 2 * D_FF), _full((1, D_MODEL)), ANY],
        out_specs=[_rows(tm, 2 * D_FF), _full((1, D_MODEL)), ANY],
        out_shape=[jax.ShapeDtypeStruct((T, 2 * D_FF), BF16), jax.ShapeDtypeStruct((1, D_MODEL), F32),
                   jax.ShapeDtypeStruct(w_out.shape, F32)],
        scratch_shapes=[pltpu.VMEM(w_out.shape, BF16), pltpu.VMEM(w_out.shape, F32), pltpu.SemaphoreType.DMA((1,))],
        compiler_params=_params("arbitrary"),
    )(dout, f, z, g_post, w_out)


def _norm_matmul_dw(dz, x, g, n_blocks, *, name, rider=None, together=1):
    T = x.shape[0]
    bw = dz.shape[1] // n_blocks
    tm = DW_ROW_TILE

    def body(dz_ref, x_ref, g_ref, dw_ref):
        @pl.when(pl.program_id(1) == 0)
        def _():
            dw_ref[...] = jnp.zeros_like(dw_ref)

        xv = x_ref[...]
        h_t = (xv * _rms(xv) * g_ref[...]).T.astype(BF16)
        for b in range(together):
            dw_ref[b] += jnp.dot(h_t, dz_ref[:, b * bw:(b + 1) * bw], preferred_element_type=F32)

    res = _pallas(
        body, name=name, grid=(n_blocks // together, T // tm),
        in_specs=[pl.BlockSpec((tm, together * bw), lambda j, t: (t, j)), pl.BlockSpec((tm, D_MODEL), lambda j, t: (t, 0)),
                  _full((1, D_MODEL))],
        out_specs=[pl.BlockSpec((together, D_MODEL, bw), lambda j, t: (j, 0, 0))],
        out_shape=[jax.ShapeDtypeStruct((n_blocks, D_MODEL, bw), F32)], semantics=("parallel", "arbitrary"),
        args=(dz, x, g), rider=rider)
    return res[0] if rider is None else (res[0][0], res[1])


def _norm_matmul_dx(dz, x, dres, g, w, *, name, rider=None, part=(0, 1), into=None):
    T = x.shape[0]
    nb, _, bw = w.shape
    tm = FFN_ROW_TILE
    first, steps = round(part[0] * T) // tm, round((part[1] - part[0]) * T) // tm

    def body(dz_ref, x_ref, dres_ref, g_ref, w_hbm, *rest):
        dx_ref, dg_ref, w_v, sems = rest[-4:]

        @pl.when(pl.program_id(0) == 0)
        def _():
            _load_once([(w_hbm, w_v)], sems)
            dg_ref[...] = jnp.zeros_like(dg_ref)

        xv = x_ref[...]
        r = _rms(xv)
        xn = xv * r
        gv = g_ref[...]
        dh = jnp.zeros((tm, D_MODEL), F32)
        for j in range(nb):
            dh = dh + _dot_nt(dz_ref[:, j * bw:(j + 1) * bw], w_v[j])
        dg_ref[...] += jnp.sum(dh * xn, axis=0, keepdims=True)
        dx_ref[...] = _rms_bwd(dh * gv, xn, r) + dres_ref[...]

    rows = lambda width: pl.BlockSpec((tm, width), lambda i: (first + i, 0))
    row = rows(D_MODEL)
    return _pallas(
        body, name=name, grid=(steps,),
        in_specs=[rows(nb * bw), row, row, _full((1, D_MODEL)), ANY] + ([] if into is None else [ANY]),
        out_specs=[row, _full((1, D_MODEL))],
        out_shape=[jax.ShapeDtypeStruct((T, D_MODEL), F32), jax.ShapeDtypeStruct((1, D_MODEL), F32)],
        scratch_shapes=[pltpu.VMEM(w.shape, BF16), pltpu.SemaphoreType.DMA((1,))],
        semantics=("arbitrary",), args=(dz, x, dres, g, w) + (() if into is None else (into,)), rider=rider,
        aliases={} if into is None else {5: 0})


def _mix_in_fwd(x, g, w):
    T = x.shape[0]
    tm = ROW_TILE

    def body(x_ref, g_ref, w_ref, q_ref, k_ref, v_ref, u_ref):
        xv = x_ref[...]
        h = (xv * _rms(xv) * g_ref[...]).astype(BF16)
        for j, o_ref in enumerate((q_ref, k_ref, v_ref, u_ref)):
            o_ref[...] = _dot(h, w_ref[j])

    col = _rows(tm, ATTN_W)
    return pl.pallas_call(
        body, name="mix_in_fwd", grid=(T // tm,),
        in_specs=[_rows(tm, D_MODEL), _full((1, D_MODEL)), _full(w.shape)],
        out_specs=[col] * 4, out_shape=[jax.ShapeDtypeStruct((T, ATTN_W), F32)] * 4,
        compiler_params=_params("parallel"),
    )(x, g, w)


def _mix_out_fwd(x, attn, ssm, w, g):
    T = x.shape[0]
    tm = ROW_TILE

    def body(x_ref, a_ref, s_ref, w_ref, g_ref, o_ref, m_ref):
        mixed = _dot(a_ref[...], w_ref[0]) + _dot(s_ref[...], w_ref[1])
        m_ref[...] = mixed
        o_ref[...] = x_ref[...] + mixed * _rms(mixed) * g_ref[...]

    row, col = _rows(tm, D_MODEL), _rows(tm, ATTN_W)
    return pl.pallas_call(
        body, name="mix_out_fwd", grid=(T // tm,),
        in_specs=[row, col, col, _full(w.shape), _full((1, D_MODEL))],
        out_specs=[row, row], out_shape=[jax.ShapeDtypeStruct((T, D_MODEL), F32)] * 2,
        compiler_params=_params("parallel"),
    )(x, attn, ssm, w, g)


def _mix_out_bwd(dout, mixed, attn, ssm, w, g, rider=None):
    T = dout.shape[0]
    tm = ROW_TILE

    def body(dout_ref, m_ref, a_ref, s_ref, w_ref, g_ref, da_ref, ds_ref, dw_ref, dg_ref):
        @pl.when(pl.program_id(0) == 0)
        def _():
            dw_ref[...] = jnp.zeros_like(dw_ref)
            dg_ref[...] = jnp.zeros_like(dg_ref)

        dy = dout_ref[...]
        mixed = m_ref[...]
        r = _rms(mixed)
        mn = mixed * r
        dg_ref[...] += jnp.sum(dy * mn, axis=0, keepdims=True)
        dm = _rms_bwd(dy * g_ref[...], mn, r).astype(BF16)
        da_ref[...] = _dot_nt(dm, w_ref[0])
        ds_ref[...] = _dot_nt(dm, w_ref[1])
        dw_ref[0] += _dot_tn(a_ref[...], dm)
        dw_ref[1] += _dot_tn(s_ref[...], dm)

    row, col = _rows(tm, D_MODEL), _rows(tm, ATTN_W)
    return _pallas(
        body, name="mix_out_bwd", grid=(T // tm,),
        in_specs=[row, row, col, col, _full(w.shape), _full((1, D_MODEL))],
        out_specs=[col, col, _full(w.shape), _full((1, D_MODEL))],
        out_shape=[jax.ShapeDtypeStruct((T, ATTN_W), F32)] * 2
        + [jax.ShapeDtypeStruct(w.shape, F32), jax.ShapeDtypeStruct((1, D_MODEL), F32)],
        semantics=("arbitrary",), args=(dout, mixed, attn, ssm, w, g), rider=rider)


ATTN_TILING = {1: (ATTN_W, 1), 4: (2 * HEAD_DIM, 4), 16: (2 * HEAD_DIM, 4)}


def _class_rows(d, r):
    return (pl.ds(r, QBLK, stride=d), slice(None)) if d > 1 else (slice(None), slice(None))


def _for_class_groups(d, group, fn):
    if d == group:
        fn(0)
    else:
        lax.fori_loop(0, d // group, lambda n, carry: (fn(n * group), carry)[1], 0)


def _block_slopes(lanes, lane_block):
    heads = lanes // HEAD_DIM
    first = lane_block * heads
    return [jnp.exp2(-jnp.full((1, 1), first + hh + 1, jnp.int32).astype(F32)) for hh in range(heads)]


def _attn_specs(d, nb, lanes):
    blk = (QBLK * d, lanes)
    cur = pl.BlockSpec(blk, lambda j, lb: (j, lb))
    prev = pl.BlockSpec(blk, lambda j, lb: (jnp.maximum(j - 1, 0), lb))
    nxt = pl.BlockSpec(blk, lambda j, lb: (jnp.minimum(j + 1, nb - 1), lb))
    return cur, prev, nxt


def _attn_branch_fwd(q, k, v, d):
    T = q.shape[0]
    nb = T // (d * QBLK)
    lanes, group = ATTN_TILING[d]
    scale = HEAD_DIM ** -0.5

    def body(q_ref, kc_ref, kp_ref, vc_ref, vp_ref, o_ref, l_ref, q_s, kk_s, vv_s, o_s, l_s):
        j = pl.program_id(0)
        qi = lax.broadcasted_iota(jnp.int32, (QBLK, 2 * QBLK), 0)
        ci = lax.broadcasted_iota(jnp.int32, (QBLK, 2 * QBLK), 1)
        steps = QBLK + qi - ci
        valid = (steps >= 0) & (steps <= N_BACK) & ((ci >= QBLK) | (j > 0))
        dist = (steps * d).astype(F32)
        slopes = _block_slopes(lanes, pl.program_id(1))

        def classes(first):
            for n in range(group):
                rows = _class_rows(d, first + n)
                q_s[n] = q_ref[rows]
                kk_s[n, :QBLK], kk_s[n, QBLK:] = kp_ref[rows], kc_ref[rows]
                vv_s[n, :QBLK], vv_s[n, QBLK:] = vp_ref[rows], vc_ref[rows]
            for n in range(group):
                for hh in range(lanes // HEAD_DIM):
                    sl = slice(hh * HEAD_DIM, (hh + 1) * HEAD_DIM)
                    s = _dot_nt(q_s[n, :, sl], kk_s[n, :, sl]) * scale - slopes[hh] * dist
                    s = jnp.where(valid, s, NEG)
                    m = jnp.max(s, axis=-1, keepdims=True)
                    p = jnp.exp(s - m)
                    den = jnp.sum(p, axis=-1, keepdims=True)
                    o_s[n, :, sl] = _dot(p, vv_s[n, :, sl]) / den
                    l_s[n, :, sl] = jnp.broadcast_to(m + jnp.log(den), (QBLK, HEAD_DIM))
            for n in range(group):
                rows = _class_rows(d, first + n)
                o_ref[rows] = o_s[n]
                l_ref[rows] = l_s[n]

        _for_class_groups(d, group, classes)

    cur, prev, _ = _attn_specs(d, nb, lanes)
    shape = jax.ShapeDtypeStruct((T, ATTN_W), F32)
    one, two = pltpu.VMEM((group, QBLK, lanes), F32), pltpu.VMEM((group, 2 * QBLK, lanes), F32)
    return pl.pallas_call(
        body, name=f"attn_fwd_d{d}", grid=(nb, ATTN_W // lanes),
        in_specs=[cur, cur, prev, cur, prev], out_specs=[cur, cur], out_shape=[shape, shape],
        scratch_shapes=[one, two, two, one, one], compiler_params=_params("parallel", "parallel"),
    )(q, k, k, v, v)


def _attn_merge(outs, lses):
    T = outs[0].shape[0]
    tm = 512

    def body(o1, o2, o3, l1, l2, l3, a_ref, lse_ref):
        ls = [l1[...], l2[...], l3[...]]
        m = jnp.maximum(jnp.maximum(ls[0], ls[1]), ls[2])
        lse = m + jnp.log(jnp.exp(ls[0] - m) + jnp.exp(ls[1] - m) + jnp.exp(ls[2] - m))
        lse_ref[...] = lse
        a_ref[...] = jnp.exp(ls[0] - lse) * o1[...] + jnp.exp(ls[1] - lse) * o2[...] + jnp.exp(ls[2] - lse) * o3[...]

    col = _rows(tm, ATTN_W)
    return pl.pallas_call(
        body, name="attn_merge", grid=(T // tm,), in_specs=[col] * 6, out_specs=[col, col],
        out_shape=[jax.ShapeDtypeStruct((T, ATTN_W), F32)] * 2, compiler_params=_params("parallel"),
    )(*outs, *lses)


def _attn_branch_bwd(q, k, v, o, lse, do, d, rider=None):
    T = q.shape[0]
    nb = T // (d * QBLK)
    lanes, group = ATTN_TILING[d]
    scale = HEAD_DIM ** -0.5

    def body(q_ref, kc_ref, kp_ref, vc_ref, vp_ref, o_ref, l_ref, do_ref, dq_ref, dk_ref, dv_ref,
             q_s, o_s, l_s, do_s, kk_s, vv_s, dq_s, dk_s, dv_s, ck_s, cv_s):
        j = pl.program_id(1)

        @pl.when(j == 0)
        def _():
            ck_s[...] = jnp.zeros_like(ck_s)
            cv_s[...] = jnp.zeros_like(cv_s)

        qi = lax.broadcasted_iota(jnp.int32, (QBLK, 2 * QBLK), 0)
        ci = lax.broadcasted_iota(jnp.int32, (QBLK, 2 * QBLK), 1)
        steps = QBLK + qi - ci
        valid = (steps >= 0) & (steps <= N_BACK) & ((ci >= QBLK) | (j > 0))
        dist = (steps * d).astype(F32)
        lo, hi = slice(0, QBLK), slice(QBLK, 2 * QBLK)
        slopes = _block_slopes(lanes, pl.program_id(0))

        def classes(first):
            for n in range(group):
                rows = _class_rows(d, first + n)
                q_s[n], o_s[n], l_s[n], do_s[n] = q_ref[rows], o_ref[rows], l_ref[rows], do_ref[rows]
                kk_s[n, lo], kk_s[n, hi] = kp_ref[rows], kc_ref[rows]
                vv_s[n, lo], vv_s[n, hi] = vp_ref[rows], vc_ref[rows]
            for n in range(group):
                for hh in range(lanes // HEAD_DIM):
                    sl = slice(hh * HEAD_DIM, (hh + 1) * HEAD_DIM)
                    qh, doh, kk, vv = q_s[n, :, sl], do_s[n, :, sl], kk_s[n, :, sl], vv_s[n, :, sl]
                    delta = jnp.sum(doh * o_s[n, :, sl], axis=-1, keepdims=True)
                    s = jnp.where(valid, _dot_nt(qh, kk) * scale - slopes[hh] * dist, NEG)
                    p = jnp.exp(s - l_s[n, :, hh * HEAD_DIM:hh * HEAD_DIM + 1])
                    ds = p * (_dot_nt(doh, vv) - delta)
                    dq_s[n, :, sl] = _dot(ds, kk) * scale
                    dkk = _dot_tn(ds, qh) * scale
                    dvv = _dot_tn(p, doh)
                    dk_s[n, :, sl] = ck_s[first + n, :, sl] + dkk[lo]
                    dv_s[n, :, sl] = cv_s[first + n, :, sl] + dvv[lo]
                    ck_s[first + n, :, sl] = dkk[hi]
                    cv_s[first + n, :, sl] = dvv[hi]
            for n in range(group):
                rows = _class_rows(d, first + n)
                dq_ref[rows] = dq_s[n]
                dk_ref[rows] = dk_s[n]
                dv_ref[rows] = dv_s[n]

        @pl.when(j < nb)
        def _():
            _for_class_groups(d, group, classes)

        @pl.when(j == nb)
        def _():
            for r in range(d):
                dk_ref[_class_rows(d, r)] = ck_s[r]
                dv_ref[_class_rows(d, r)] = cv_s[r]

    blk = (QBLK * d, lanes)
    here = lambda j: jnp.minimum(j, nb - 1)
    cur = pl.BlockSpec(blk, lambda lb, j: (here(j), lb))
    prev = pl.BlockSpec(blk, lambda lb, j: (jnp.maximum(here(j) - 1, 0), lb))
    behind = pl.BlockSpec(blk, lambda lb, j: (jnp.maximum(j - 1, 0), lb))
    shape = jax.ShapeDtypeStruct((T, ATTN_W), F32)
    one, two = pltpu.VMEM((group, QBLK, lanes), F32), pltpu.VMEM((group, 2 * QBLK, lanes), F32)
    carry = pltpu.VMEM((d, QBLK, lanes), F32)
    return _pallas(
        body, name=f"attn_bwd_d{d}", grid=(ATTN_W // lanes, nb + 1),
        in_specs=[cur, cur, prev, cur, prev, cur, cur, cur], out_specs=[cur, behind, behind], out_shape=[shape] * 3,
        scratch_shapes=[one] * 4 + [two] * 2 + [one] * 3 + [carry] * 2,
        semantics=("parallel", "arbitrary"), args=(q, k, k, v, v, o, lse, do), rider=rider)


def _dproj_merge(dqs, dks, dvs, du):
    T = du.shape[0]
    tm = 512

    def body(*refs):
        o_ref = refs[-1]
        for part in range(3):
            a, b, c = refs[3 * part:3 * part + 3]
            o_ref[:, part * ATTN_W:(part + 1) * ATTN_W] = (a[...] + b[...] + c[...]).astype(BF16)
        o_ref[:, 3 * ATTN_W:] = refs[9][...].astype(BF16)

    col = _rows(tm, ATTN_W)
    return pl.pallas_call(
        body, name="dproj_merge", grid=(T // tm,), in_specs=[col] * 10, out_specs=_rows(tm, 4 * ATTN_W),
        out_shape=jax.ShapeDtypeStruct((T, 4 * ATTN_W), BF16), compiler_params=_params("parallel"),
    )(*dqs, *dks, *dvs, du)


def _attention_fwd(q, k, v):
    res = [_attn_branch_fwd(q, k, v, d) for d in DILATIONS]
    return _attn_merge([r[0] for r in res], [r[1] for r in res])


SCAN_ROWS = 8
SCAN_LANES = 512
SSM_CHUNK = 512
SSM_CHUNK_BWD = 256
SSM_HALVES = tuple((slice(h * SSM_W // 2, (h + 1) * SSM_W // 2), slice(h * N_STATE // 2, (h + 1) * N_STATE // 2)) for h in range(2))


def _cmul(ar, ai, br, bi):
    return ar * br - ai * bi, ar * bi + ai * br


def _ssm_discretize(a_re, a_im, log_dt, b_re, b_im):
    def body(ar_ref, ai_ref, ldt_ref, br_ref, bi_ref, abr_ref, abi_ref, er_ref, ei_ref, bbr_ref, bbi_ref, pr_ref, pi_ref):
        ar, ai = ar_ref[...], ai_ref[...]
        dt = jnp.exp(ldt_ref[...])
        n = lax.broadcasted_iota(jnp.int32, (1, SCAN_ROWS), 1).astype(F32) + 1.0
        mag, ang = jnp.exp(dt * ar), dt * ai
        abr, abi = mag * jnp.cos(ang), mag * jnp.sin(ang)
        abr_ref[...], abi_ref[...] = abr, abi
        pr_ref[...] = jnp.exp(dt * ar * n) * jnp.cos(ang * n)
        pi_ref[...] = jnp.exp(dt * ar * n) * jnp.sin(ang * n)
        den = ar * ar + ai * ai
        er = ((abr - 1.0) * ar + abi * ai) / den
        ei = (abi * ar - (abr - 1.0) * ai) / den
        er_ref[...], ei_ref[...] = er, ei
        bbr_ref[...], bbi_ref[...] = _cmul(er, ei, br_ref[...], bi_ref[...])

    col = jax.ShapeDtypeStruct((N_STATE, 1), F32)
    mat = jax.ShapeDtypeStruct((N_STATE, GROUP_CH), F32)
    pw = jax.ShapeDtypeStruct((N_STATE, SCAN_ROWS), F32)
    return pl.pallas_call(body, name="ssm_discretize", out_shape=[col] * 4 + [mat] * 2 + [pw] * 2)(
        a_re, a_im, log_dt, b_re, b_im)


def _ssm_discretize_bwd(a_re, a_im, log_dt, b_re, b_im, ab_re, ab_im, e_re, e_im, dab_re, dab_im, dbb_re, dbb_im):
    def body(ar_ref, ai_ref, ldt_ref, br_ref, bi_ref, abr_ref, abi_ref, er_ref, ei_ref, dabr_ref, dabi_ref,
             dbbr_ref, dbbi_ref, dar_ref, dai_ref, ddt_ref, dbr_ref, dbi_ref):
        ar, ai, dt = ar_ref[...], ai_ref[...], jnp.exp(ldt_ref[...])
        er, ei = er_ref[...], ei_ref[...]
        gbr, gbi = dbbr_ref[...], dbbi_ref[...]
        dbr_ref[...], dbi_ref[...] = _cmul(er, -ei, gbr, gbi)
        br, bi = br_ref[...], bi_ref[...]
        der = jnp.sum(br * gbr + bi * gbi, axis=-1, keepdims=True)
        dei = jnp.sum(br * gbi - bi * gbr, axis=-1, keepdims=True)
        den = ar * ar + ai * ai
        inv_r, inv_i = ar / den, -ai / den
        t_r, t_i = _cmul(der, dei, inv_r, -inv_i)
        gab_r, gab_i = dabr_ref[...] + t_r, dabi_ref[...] + t_i
        q_r, q_i = _cmul(er, ei, inv_r, inv_i)
        dl_r, dl_i = _cmul(der, dei, q_r, -q_i)
        dl_r, dl_i = -dl_r, -dl_i
        gw_r, gw_i = _cmul(gab_r, gab_i, abr_ref[...], -abi_ref[...])
        dar_ref[...] = dl_r + dt * gw_r
        dai_ref[...] = dl_i + dt * gw_i
        ddt_ref[...] = (gw_r * ar + gw_i * ai) * dt

    col = jax.ShapeDtypeStruct((N_STATE, 1), F32)
    mat = jax.ShapeDtypeStruct((N_STATE, GROUP_CH), F32)
    return pl.pallas_call(body, name="ssm_discretize_bwd", out_shape=[col] * 3 + [mat] * 2)(
        a_re, a_im, log_dt, b_re, b_im, ab_re, ab_im, e_re, e_im, dab_re, dab_im, dbb_re, dbb_im)


def _scan_tables(p_re, p_im, reverse):
    pr, pi = p_re.T, p_im.T
    if reverse:
        pi = -pi
    row = jnp.arange(SCAN_ROWS)[:, None]
    level = lambda t, s: jnp.where((row < SCAN_ROWS - s) if reverse else (row >= s), t[s - 1][None, :], 0.0)
    carry = (pr[::-1], pi[::-1]) if reverse else (pr, pi)
    return jnp.stack([level(pr, 1), level(pi, 1), level(pr, 2), level(pi, 2), level(pr, 4), level(pi, 4), carry[0], carry[1]])


def _scan_group(xr, xi, tab_ref, ls, carry_r, carry_i, reverse):
    for n, s in enumerate((1, 2, 4)):
        shift = SCAN_ROWS - s if reverse else s
        mr, mi = _cmul(tab_ref[2 * n, :, ls], tab_ref[2 * n + 1, :, ls], pltpu.roll(xr, shift, 0), pltpu.roll(xi, shift, 0))
        xr, xi = xr + mr, xi + mi
    mr, mi = _cmul(tab_ref[6, :, ls], tab_ref[7, :, ls], carry_r, carry_i)
    return xr + mr, xi + mi


def _gelu(y):
    t = jnp.tanh(GELU_C * (y + 0.044715 * y * y * y))
    return 0.5 * y * (1.0 + t), t


def _ssm_fwd(u, tab, bd_re, bd_im, cd_re, cd_im, d_skip, w_glu, b_glu):
    T = u.shape[0]
    tc = SSM_CHUNK

    def body(u_ref, tab_ref, bdr_ref, bdi_ref, cdr_ref, cdi_ref, dsk_ref, wg_ref, bg_ref,
             sr_ref, si_ref, yp_ref, o_ref, car_r, car_i):
        @pl.when(pl.program_id(0) == 0)
        def _():
            car_r[...] = jnp.zeros_like(car_r)
            car_i[...] = jnp.zeros_like(car_i)

        uv = u_ref[...]
        for cs, ss in SSM_HALVES:
            sr_ref[:, ss] = _dot(uv[:, cs], bdr_ref[cs, ss])
            si_ref[:, ss] = _dot(uv[:, cs], bdi_ref[cs, ss])
        for lb in range(N_STATE // SCAN_LANES):
            ls = pl.ds(lb * SCAN_LANES, SCAN_LANES)

            def step(g, carry):
                rows = pl.ds(pl.multiple_of(g * SCAN_ROWS, SCAN_ROWS), SCAN_ROWS)
                xr, xi = _scan_group(sr_ref[rows, ls], si_ref[rows, ls], tab_ref, ls, carry[0], carry[1], False)
                sr_ref[rows, ls] = xr
                si_ref[rows, ls] = xi
                last = slice(SCAN_ROWS - 1, SCAN_ROWS)
                return (jnp.broadcast_to(xr[last], xr.shape), jnp.broadcast_to(xi[last], xi.shape))

            cr, ci = lax.fori_loop(0, tc // SCAN_ROWS, step, (car_r[:, ls], car_i[:, ls]))
            car_r[:, ls] = cr
            car_i[:, ls] = ci
        y = jnp.concatenate([_dot(sr_ref[:, ss], cdr_ref[ss, cs]) - _dot(si_ref[:, ss], cdi_ref[ss, cs])
                             for cs, ss in SSM_HALVES], axis=1) + dsk_ref[...] * uv
        yp_ref[...] = y
        gy, _ = _gelu(y)
        o_ref[...] = gy * jax.nn.sigmoid(_dot(gy, wg_ref[...]) + bg_ref[...])

    col, st = _rows(tc, SSM_W), _rows(tc, N_STATE)
    vec = _full((1, SSM_W))
    return pl.pallas_call(
        body, name="ssm_fwd", grid=(T // tc,),
        in_specs=[col, _full(tab.shape), _full(bd_re.shape), _full(bd_im.shape), _full(cd_re.shape), _full(cd_im.shape),
                  vec, _full(w_glu.shape), vec],
        out_specs=[st, st, col, col],
        out_shape=[jax.ShapeDtypeStruct((T, N_STATE), F32)] * 2 + [jax.ShapeDtypeStruct((T, SSM_W), F32)] * 2,
        scratch_shapes=[pltpu.VMEM((SCAN_ROWS, N_STATE), F32)] * 2,
        compiler_params=_params("arbitrary"),
    )(u, tab, bd_re, bd_im, cd_re, cd_im, d_skip, w_glu, b_glu)


def _ssm_bwd(dout, u, yp, s_re, s_im, tab, bd_re, bd_im, cd_re, cd_im, d_skip, w_glu, b_glu, rider=None):
    T = u.shape[0]
    tc = SSM_CHUNK_BWD
    nt = T // tc
    rows_per_chunk = tc // SCAN_ROWS

    def body(do_ref, u_ref, yp_ref, sr_ref, si_ref, pr_ref, pi_ref, tab_ref, dsk_ref, wg_ref, bg_ref,
             bdr_hbm, bdi_hbm, cdr_hbm, cdi_hbm,
             du_ref, dsk_out, dbg_out, dwg_out, da_out, dbdr_out, dbdi_out, dcdr_out, dcdi_out,
             bdr_v, bdi_v, cdr_v, cdi_v, dbdr_v, dbdi_v, dcdr_v, dcdi_v, gr_v, gi_v, car_r, car_i, sems):
        i = pl.program_id(0)

        @pl.when(i == 0)
        def _():
            _load_once([(bdr_hbm, bdr_v), (bdi_hbm, bdi_v), (cdr_hbm, cdr_v), (cdi_hbm, cdi_v)], sems)
            for ref in (dbdr_v, dbdi_v, dcdr_v, dcdi_v, car_r, car_i, dsk_out, dbg_out, dwg_out, da_out):
                ref[...] = jnp.zeros_like(ref)

        uv, y, dout_v = u_ref[...], yp_ref[...], do_ref[...]
        gy, t = _gelu(y)
        sg = jax.nn.sigmoid(_dot(gy, wg_ref[...]) + bg_ref[...])
        dzg = dout_v * gy * sg * (1.0 - sg)
        dgy = dout_v * sg + _dot_nt(dzg, wg_ref[...])
        dwg_out[...] += _dot_tn(gy, dzg)
        dbg_out[...] += jnp.sum(dzg, axis=0, keepdims=True)
        dy = dgy * (0.5 * (1.0 + t) + 0.5 * y * (1.0 - t * t) * GELU_C * (1.0 + 3 * 0.044715 * y * y))
        dsk_out[...] += jnp.sum(dy * uv, axis=0, keepdims=True)

        for cs, ss in SSM_HALVES:
            gr_v[:, ss] = _dot_nt(dy[:, cs], cdr_v[ss, cs])
            gi_v[:, ss] = -_dot_nt(dy[:, cs], cdi_v[ss, cs])
            dcdr_v[ss, cs] += _dot_tn(sr_ref[:, ss], dy[:, cs])
            dcdi_v[ss, cs] -= _dot_tn(si_ref[:, ss], dy[:, cs])

        row = lax.broadcasted_iota(jnp.int32, (SCAN_ROWS, SCAN_LANES), 0)
        first_chunk = i == nt - 1
        for lb in range(N_STATE // SCAN_LANES):
            ls = pl.ds(lb * SCAN_LANES, SCAN_LANES)

            def step(n, carry):
                g = rows_per_chunk - 1 - n
                rows = pl.ds(pl.multiple_of(g * SCAN_ROWS, SCAN_ROWS), SCAN_ROWS)
                before = pl.ds(pl.multiple_of(jnp.maximum(g - 1, 0) * SCAN_ROWS, SCAN_ROWS), SCAN_ROWS)
                xr, xi = _scan_group(gr_v[rows, ls], gi_v[rows, ls], tab_ref, ls, carry[0], carry[1], True)
                gr_v[rows, ls] = xr
                gi_v[rows, ls] = xi
                last = slice(SCAN_ROWS - 1, SCAN_ROWS)
                edge_r = jnp.where(g > 0, sr_ref[before, ls][last], jnp.where(first_chunk, 0.0, pr_ref[:, ls][last]))
                edge_i = jnp.where(g > 0, si_ref[before, ls][last], jnp.where(first_chunk, 0.0, pi_ref[:, ls][last]))
                spr = jnp.where(row >= 1, pltpu.roll(sr_ref[rows, ls], 1, 0), edge_r)
                spi = jnp.where(row >= 1, pltpu.roll(si_ref[rows, ls], 1, 0), edge_i)
                first = slice(0, 1)
                return (jnp.broadcast_to(xr[first], xr.shape), jnp.broadcast_to(xi[first], xi.shape),
                        carry[2] + xr * spr + xi * spi, carry[3] + xi * spr - xr * spi)

            zero = jnp.zeros((SCAN_ROWS, SCAN_LANES), F32)
            cr, ci, dar, dai = lax.fori_loop(0, rows_per_chunk, step, (car_r[:, ls], car_i[:, ls], zero, zero))
            car_r[:, ls] = cr
            car_i[:, ls] = ci
            da_out[0, :, ls] += dar
            da_out[1, :, ls] += dai

        du_ref[...] = dsk_ref[...] * dy + jnp.concatenate(
            [_dot_nt(gr_v[:, ss], bdr_v[cs, ss]) + _dot_nt(gi_v[:, ss], bdi_v[cs, ss]) for cs, ss in SSM_HALVES], axis=1)
        for cs, ss in SSM_HALVES:
            dbdr_v[cs, ss] += _dot_tn(uv[:, cs], gr_v[:, ss])
            dbdi_v[cs, ss] += _dot_tn(uv[:, cs], gi_v[:, ss])

        @pl.when(i == nt - 1)
        def _():
            for grp in range(N_GROUPS):
                ch, stt = pl.ds(grp * GROUP_CH, GROUP_CH), pl.ds(grp * STATE, STATE)
                dbdr_out[grp], dbdi_out[grp] = dbdr_v[ch, stt], dbdi_v[ch, stt]
                dcdr_out[grp], dcdi_out[grp] = dcdr_v[stt, ch], dcdi_v[stt, ch]

    rev = lambda i: (nt - 1 - i, 0)
    col = pl.BlockSpec((tc, SSM_W), rev)
    st = pl.BlockSpec((tc, N_STATE), rev)
    st_before = pl.BlockSpec((SCAN_ROWS, N_STATE), lambda i: (jnp.maximum((nt - 1 - i) * rows_per_chunk - 1, 0), 0))
    vec = _full((1, SSM_W))
    bd_blocks, cd_blocks = (N_GROUPS, GROUP_CH, STATE), (N_GROUPS, STATE, GROUP_CH)
    bd, cd = jax.ShapeDtypeStruct(bd_blocks, F32), jax.ShapeDtypeStruct(cd_blocks, F32)
    return _pallas(
        body, name="ssm_bwd", grid=(nt,),
        in_specs=[col, col, col, st, st, st_before, st_before, _full(tab.shape), vec, _full(w_glu.shape), vec,
                  ANY, ANY, ANY, ANY],
        out_specs=[col, vec, vec, _full(w_glu.shape), _full((2, SCAN_ROWS, N_STATE)),
                   _full(bd_blocks), _full(bd_blocks), _full(cd_blocks), _full(cd_blocks)],
        out_shape=[jax.ShapeDtypeStruct((T, SSM_W), F32), jax.ShapeDtypeStruct((1, SSM_W), F32),
                   jax.ShapeDtypeStruct((1, SSM_W), F32), jax.ShapeDtypeStruct(w_glu.shape, F32),
                   jax.ShapeDtypeStruct((2, SCAN_ROWS, N_STATE), F32), bd, bd, cd, cd],
        scratch_shapes=[pltpu.VMEM(bd_re.shape, BF16)] * 2 + [pltpu.VMEM(cd_re.shape, BF16)] * 2
        + [pltpu.VMEM(bd_re.shape, F32)] * 2 + [pltpu.VMEM(cd_re.shape, F32)] * 2
        + [pltpu.VMEM((tc, N_STATE), F32)] * 2 + [pltpu.VMEM((SCAN_ROWS, N_STATE), F32)] * 2
        + [pltpu.SemaphoreType.DMA((4,))],
        semantics=("arbitrary",), rider=rider,
        args=(dout, u, yp, s_re, s_im, s_re, s_im, tab, d_skip, w_glu, b_glu, bd_re, bd_im, cd_re, cd_im))


def _block_diagonals(parts):
    def body(*refs):
        for t_ref, o_ref in zip(refs[:len(parts)], refs[len(parts):]):
            _, a, b = t_ref.shape
            o_ref[...] = jnp.zeros_like(o_ref)
            for grp in range(N_GROUPS):
                o_ref[grp * a:(grp + 1) * a, grp * b:(grp + 1) * b] = t_ref[grp].astype(BF16)

    return pl.pallas_call(body, name="block_diagonals",
                          out_shape=[jax.ShapeDtypeStruct((N_GROUPS * t.shape[1], N_GROUPS * t.shape[2]), BF16) for t in parts])(*parts)


def _ssm_prepare(a_re, a_im, log_dt, b_re, b_im, c_re, c_im):
    col = lambda t: t.reshape(N_STATE, 1)
    ldt = jnp.broadcast_to(log_dt.reshape(N_GROUPS, 1), (N_GROUPS, STATE)).reshape(N_STATE, 1)
    b2r, b2i = b_re.reshape(N_STATE, GROUP_CH), b_im.reshape(N_STATE, GROUP_CH)
    ab_r, ab_i, e_r, e_i, bb_r, bb_i, p_r, p_i = _ssm_discretize(col(a_re), col(a_im), ldt, b2r, b2i)
    diag = _block_diagonals([jnp.swapaxes(t.reshape(N_GROUPS, STATE, GROUP_CH), 1, 2) for t in (bb_r, bb_i)]
                            + [jnp.swapaxes(t, 1, 2) for t in (c_re, c_im)])
    bd, cd = diag[:2], diag[2:]
    saved = dict(a_re=col(a_re), a_im=col(a_im), log_dt=ldt, b_re=b2r, b_im=b2i, ab_re=ab_r, ab_im=ab_i, e_re=e_r, e_im=e_i)
    return _scan_tables(p_r, p_i, False), _scan_tables(p_r, p_i, True), bd, cd, saved


BIG = ("ffn1_w_in", "ffn1_w_out", "w_mix_in", "w_glu", "w_mix_out", "ffn2_w_in", "ffn2_w_out")
SMALL = ("ffn1_pre_g", "ffn1_post_g", "mix_pre_g", "a_re", "a_im", "log_dt", "b_re", "b_im", "c_re", "c_im",
         "d_skip", "b_glu", "mix_post_g", "ffn2_pre_g", "ffn2_post_g")


FIRST = ("ffn1_w_in", "ffn1_w_out")
REST = ("w_mix_in", "w_glu", "w_mix_out", "ffn2_w_in", "ffn2_w_out")
LATE = ("w_mix_in", "w_glu", "w_mix_out", "ffn1_w_out")
SHARD_SHAPE = {"ffn1_w_in": (D_MODEL, FF_BLK), "ffn2_w_in": (D_MODEL, FF_BLK), "ffn1_w_out": (D_FF // N_CHIPS, D_MODEL),
               "ffn2_w_out": (D_FF // N_CHIPS, D_MODEL), "w_mix_in": (D_MODEL, ATTN_W), "w_glu": (SSM_W // N_CHIPS, SSM_W),
               "w_mix_out": (2 * ATTN_W // N_CHIPS, D_MODEL)}


class _Reduction:
    def __init__(self, names, grads, place):
        self.names, self.local, self.place = list(names), list(grads), place

    def exchange(self):
        return _pair_exchange(self.local)

    def first(self, got):
        both = [_pair_sum(a, b, self.place["core"], name=f"pair_sum_{n}") for n, a, b in zip(self.names, self.local, got)]
        self.pair = [f32 for f32, _ in both]
        return _reduce_first([wire for _, wire in both])

    def second(self, got):
        both = [_reduce_sum_first(a, b, self.place["sel_first"], name=f"sum_first_{n}")
                for n, a, b in zip(self.names, self.pair, got)]
        self.sums = [f32 for f32, _ in both]
        return _reduce_second([wire for _, wire in both])

    def swap(self, got):
        self.halves = [_reduce_sum_second(a, b, self.place["sel_second"], name=f"sum_second_{n}").reshape(2 * a.shape[2], a.shape[3])
                       for n, a, b in zip(self.names, self.sums, got)]
        return _pair_swap(self.halves)

    def done(self, got):
        return {n: (mine, theirs) for n, mine, theirs in zip(self.names, self.halves, got)}


def _local_step(x, target, p, w, place=None):
    vec = lambda t: t.reshape(1, -1)
    w1_in, w1_out = w["ffn1_w_in"], w["ffn1_w_out"].reshape(2, FF_BLK, D_MODEL)
    blocks = lambda n, t: t.reshape((N_CHIPS,) + SHARD_SHAPE[n])

    ffn1 = functools.partial(_ffn_fwd, x, vec(p["ffn1_pre_g"]), w1_in, w1_out, vec(p["ffn1_post_g"]), None, name="ffn1_fwd")
    if place is None:
        x1, z1, f1 = ffn1()
    else:
        (x1, z1, f1), rest = ffn1(rider=_gather_weights([w[n] for n in REST]))
        w = dict(w, **dict(zip(REST, rest)))
    w2_in, w2_out = w["ffn2_w_in"], w["ffn2_w_out"].reshape(2, FF_BLK, D_MODEL)
    w_mi, w_glu, w_mo = w["w_mix_in"], w["w_glu"].reshape(SSM_W, SSM_W), w["w_mix_out"].reshape(2, ATTN_W, D_MODEL)
    q, k, v, u = _mix_in_fwd(x1, vec(p["mix_pre_g"]), w_mi)
    attn, lse = _attention_fwd(q, k, v)
    tab_f, tab_b, bd, cd, sv = _ssm_prepare(p["a_re"], p["a_im"], p["log_dt"], p["b_re"], p["b_im"], p["c_re"], p["c_im"])
    ssm_args = (bd[0], bd[1], cd[0], cd[1], vec(p["d_skip"]), w_glu, vec(p["b_glu"]))
    s_re, s_im, yp, ssm = _ssm_fwd(u, tab_f, *ssm_args)
    x2, mixed = _mix_out_fwd(x1, attn, ssm, w_mo, vec(p["mix_post_g"]))
    dx3, loss_rows, z2, f2 = _ffn_fwd(x2, vec(p["ffn2_pre_g"]), w2_in, w2_out, vec(p["ffn2_post_g"]), target, name="ffn2_fwd")

    g = {}
    ride = (lambda call, exchange: call(rider=exchange)) if place else (lambda call, exchange: (call(), None))
    dz2, g["ffn2_post_g"], dw2_out = _ffn_bwd_out(dx3, f2, z2, w2_out, vec(p["ffn2_post_g"]), name="ffn2_bwd_out")
    dw2_in = _norm_matmul_dw(dz2, x2, vec(p["ffn2_pre_g"]), N_CHIPS, name="ffn2_bwd_dw", together=2)
    early = _Reduction(("ffn2_w_in", "ffn2_w_out"), [dw2_in, blocks("ffn2_w_out", dw2_out)], place) if place else None
    (dx2, g["ffn2_pre_g"]), got = ride(
        functools.partial(_norm_matmul_dx, dz2, x2, dx3, vec(p["ffn2_pre_g"]), w2_in, name="ffn2_bwd_dx"), early and early.exchange())
    dattn, dssm, dw_mo, g["mix_post_g"] = _mix_out_bwd(dx2, mixed, attn, ssm, w_mo, vec(p["mix_post_g"]))
    (du, g["d_skip"], g["b_glu"], dw_glu, da, dbd_re, dbd_im, dcd_re, dcd_im), got = ride(
        functools.partial(_ssm_bwd, dssm, u, yp, s_re, s_im, tab_b, *ssm_args), early and early.first(got))
    branch = lambda d: functools.partial(_attn_branch_bwd, q, k, v, attn, lse, dattn, d)
    parts = [None] * 3
    parts[0], got = ride(branch(DILATIONS[0]), early and early.second(got))
    parts[1], early_theirs = ride(branch(DILATIONS[1]), early and early.swap(got))
    parts[2] = branch(DILATIONS[2])()
    dproj = _dproj_merge([r[0] for r in parts], [r[1] for r in parts], [r[2] for r in parts], du)
    dw_mi = _norm_matmul_dw(dproj, x1, vec(p["mix_pre_g"]), N_CHIPS, name="mix_bwd_dw", together=N_CHIPS)
    dx1, g["mix_pre_g"] = _norm_matmul_dx(dproj, x1, dx2, vec(p["mix_pre_g"]), w_mi, name="mix_bwd_dx")
    dz1, g["ffn1_post_g"], dw1_out = _ffn_bwd_out(dx1, f1, z1, w1_out, vec(p["ffn1_post_g"]), name="ffn1_bwd_out")
    big = {"ffn2_w_in": dw2_in, "ffn2_w_out": dw2_out, "w_mix_in": dw_mi, "w_glu": dw_glu, "w_mix_out": dw_mo, "ffn1_w_out": dw1_out}
    late = _Reduction(LATE, [blocks(n, big[n]) for n in LATE], place) if place else None
    big["ffn1_w_in"], got = ride(
        functools.partial(_norm_matmul_dw, dz1, x, vec(p["ffn1_pre_g"]), N_CHIPS, name="ffn1_bwd_dw", together=2),
        late and late.exchange())
    dx_call = functools.partial(_norm_matmul_dx, dz1, x, dx1, vec(p["ffn1_pre_g"]), w1_in)
    if place:
        last = _Reduction(("ffn1_w_in",), [big["ffn1_w_in"]], place)
        n_late = len(LATE)
        (dx_part, dg_a), got = dx_call(name="ffn1_bwd_dx_a", part=(0, 0.375), rider=_merged(late.first(got), last.exchange()))
        (dx_part, dg_b), got = dx_call(name="ffn1_bwd_dx_b", part=(0.375, 0.75), into=dx_part,
                                       rider=_merged(late.second(got[:n_late]), last.first(got[n_late:])))
        grad_x, dg_c = dx_call(name="ffn1_bwd_dx_c", part=(0.75, 1), into=dx_part)
        g["ffn1_pre_g"] = dg_a + dg_b + dg_c
        got = _merged(late.swap(got[:n_late]), last.second(got[n_late:])).run("tail_second")
        g.update(late.done(got[:n_late]))
        g.update(last.done(last.swap(got[n_late:]).run("tail_swap")))
        g.update(early.done(early_theirs))
    else:
        grad_x, g["ffn1_pre_g"] = dx_call(name="ffn1_bwd_dx")
        g.update({n: blocks(n, big[n]) for n in BIG})

    g["c_re"], g["c_im"] = (jnp.swapaxes(m, 1, 2) for m in (dcd_re, dcd_im))
    dbb = [jnp.swapaxes(m, 1, 2).reshape(N_STATE, GROUP_CH) for m in (dbd_re, dbd_im)]
    dab = [jnp.sum(da[n], axis=0).reshape(N_STATE, 1) for n in range(2)]
    da_re, da_im, dldt, db_re, db_im = _ssm_discretize_bwd(
        sv["a_re"], sv["a_im"], sv["log_dt"], sv["b_re"], sv["b_im"], sv["ab_re"], sv["ab_im"], sv["e_re"], sv["e_im"],
        dab[0], dab[1], dbb[0], dbb[1])
    g["a_re"], g["a_im"] = da_re.reshape(N_GROUPS, STATE), da_im.reshape(N_GROUPS, STATE)
    g["log_dt"] = jnp.sum(dldt.reshape(N_GROUPS, STATE), axis=-1)
    g["b_re"], g["b_im"] = (t.reshape(N_GROUPS, STATE, GROUP_CH) for t in (db_re, db_im))
    return loss_rows, grad_x, g


MESH = pl.DeviceIdType.MESH
N_REL = 3


def _place():
    x, y, c = lax.axis_index("x"), lax.axis_index("y"), lax.axis_index("c")
    return x, y, c, [(1 - x, y), (x, 1 - y), (1 - x, 1 - y)]


def _remote(src, dst, send_sems, recv_sems, idx, to):
    return pltpu.make_async_remote_copy(src_ref=src, dst_ref=dst, send_sem=send_sems.at[idx], recv_sem=recv_sems.at[idx],
                                        device_id=to, device_id_type=MESH)


def _half(rows, who):
    return pl.ds(who * (rows // 2), rows // 2)


class _Copies:
    def __init__(self, send_sems, recv_sems, local_sems):
        self.send_sems, self.recv_sems, self.local_sems = send_sems, recv_sems, local_sems
        self.n_remote = self.n_local = 0

    def remote(self, src, dst, to):
        k, self.n_remote = self.n_remote, self.n_remote + 1
        return pltpu.make_async_remote_copy(src_ref=src, dst_ref=dst, send_sem=self.send_sems.at[k],
                                            recv_sem=self.recv_sems.at[k], device_id=to, device_id_type=MESH)

    def local(self, src, dst):
        k, self.n_local = self.n_local, self.n_local + 1
        return pltpu.make_async_copy(src, dst, self.local_sems.at[k])


class _Exchange:
    def __init__(self, plan, ins, out_shapes, n_remote):
        self.plan, self.ins, self.out_shapes, self.n_remote = plan, list(ins), list(out_shapes), n_remote
        self.sems = [pltpu.SemaphoreType.DMA((n_remote,)), pltpu.SemaphoreType.DMA((n_remote,)), pltpu.SemaphoreType.DMA((1,))]

    def run(self, name):
        n_in = len(self.ins)

        def body(*refs):
            for phase in self.plan(refs[:n_in], refs[n_in:-3], _Copies(*refs[-3:])):
                for cp in phase:
                    cp.start()
                for cp in phase:
                    cp.wait()

        return pl.pallas_call(body, name=name, in_specs=[ANY] * n_in, out_specs=[ANY] * len(self.out_shapes),
                              out_shape=self.out_shapes, scratch_shapes=self.sems)(*self.ins)


def _merged(a, b):
    n_in, n_out = len(a.ins), len(a.out_shapes)

    def plan(ins, outs, mk):
        (phase_a,), (phase_b,) = a.plan(ins[:n_in], outs[:n_out], mk), b.plan(ins[n_in:], outs[n_out:], mk)
        return [phase_a + phase_b]

    return _Exchange(plan, a.ins + b.ins, a.out_shapes + b.out_shapes, a.n_remote + b.n_remote)


def _pallas(body, *, name, grid, in_specs, out_specs, out_shape, args, semantics, scratch_shapes=(), rider=None, aliases=None):
    aliases = aliases or {}
    if rider is None:
        return pl.pallas_call(body, name=name, grid=grid, in_specs=in_specs, out_specs=out_specs, out_shape=out_shape,
                              scratch_shapes=list(scratch_shapes), input_output_aliases=aliases,
                              compiler_params=_params(*semantics))(*args)
    n_in, n_out, r_in, r_out = len(in_specs), len(out_specs), len(rider.ins), len(rider.out_shapes)
    n_steps = math.prod(grid)

    def carrier(*refs):
        ins, rider_ins = refs[:n_in], refs[n_in:n_in + r_in]
        outs = refs[n_in + r_in:n_in + r_in + n_out]
        rider_outs = refs[n_in + r_in + n_out:n_in + r_in + n_out + r_out]
        scratch, sems = refs[n_in + r_in + n_out + r_out:-3], refs[-3:]
        step = 0
        for axis, size in enumerate(grid):
            step = step * size + pl.program_id(axis)
        phases = rider.plan(rider_ins, rider_outs, _Copies(*sems))

        def start_phase(p):
            for cp in (phases[p - 1] if p else []):
                cp.wait()
            for cp in phases[p]:
                cp.start()

        for p in range(len(phases)):
            pl.when(step == p * n_steps // len(phases))(functools.partial(start_phase, p))
        body(*ins, *outs, *scratch)

        @pl.when(step == n_steps - 1)
        def _():
            for cp in phases[-1]:
                cp.wait()

    results = pl.pallas_call(
        carrier, name=name, grid=grid, in_specs=list(in_specs) + [ANY] * r_in, out_specs=list(out_specs) + [ANY] * r_out,
        out_shape=list(out_shape) + rider.out_shapes, scratch_shapes=list(scratch_shapes) + rider.sems,
        input_output_aliases=aliases, compiler_params=_params(*["arbitrary"] * len(grid)))(*args, *rider.ins)
    return results[:n_out], results[n_out:]


def _gather_weights(shards):
    def plan(ins, outs, mk):
        x, y, c = lax.axis_index("x"), lax.axis_index("y"), lax.axis_index("c")
        me, sibling = 2 * x + y, (x, y, 1 - c)
        x_nb, y_nb, diag = (1 - x, y), (x, 1 - y), (1 - x, 1 - y)
        index = lambda chip: 2 * chip[0] + chip[1]
        first, second, third = [], [], []
        for i, shard in enumerate(shards):
            rows = shard.shape[0]
            mine = _half(rows, c)
            quarter = lambda which: pl.ds(c * (rows // 2) + which * (rows // 4), rows // 4)
            first.append(mk.remote(ins[i], outs[i].at[me], sibling))
            for nb in (x_nb, y_nb):
                first.append(mk.remote(ins[i].at[mine], outs[i].at[me, mine], (*nb, c)))
            for nb in (x_nb, y_nb):
                landed = outs[i].at[index(nb), mine]
                second.append(mk.remote(landed, landed, sibling))
            for nb, other, which in ((x_nb, y_nb, 0), (y_nb, x_nb, 1)):
                landed = outs[i].at[index(nb), quarter(which)]
                second.append(mk.remote(landed, landed, (*other, c)))
            landed = outs[i].at[index(diag), mine]
            third.append(mk.remote(landed, landed, sibling))
        return [first, second, third]

    return _Exchange(plan, shards, [jax.ShapeDtypeStruct((N_CHIPS,) + s.shape, s.dtype) for s in shards], 8 * len(shards))


def _pair_exchange(grads):
    def plan(ins, outs, mk):
        x, y, c = lax.axis_index("x"), lax.axis_index("y"), lax.axis_index("c")
        return [[mk.remote(ins[i].at[:, _half(g.shape[1], 1 - c)], outs[i], (x, y, 1 - c)) for i, g in enumerate(grads)]]

    return _Exchange(plan, grads, [jax.ShapeDtypeStruct((N_CHIPS, g.shape[1] // 2, g.shape[2]), g.dtype) for g in grads], len(grads))


def _reduce_first(pair):
    def plan(ins, outs, mk):
        x, y, c = lax.axis_index("x"), lax.axis_index("y"), lax.axis_index("c")
        phase = []
        for i, p in enumerate(pair):
            q = p.shape[1] // 2
            phase.append(mk.remote(ins[i].at[pl.ds(2 * (1 - x), 2), pl.ds(0, q)], outs[i].at[0], (1 - x, y, c)))
            for jx in range(2):
                phase.append(mk.remote(ins[i].at[2 * jx + 1 - y, pl.ds(q, q)], outs[i].at[1, jx], (x, 1 - y, c)))
        return [phase]

    return _Exchange(plan, pair, [jax.ShapeDtypeStruct((2, 2, p.shape[1] // 2, p.shape[2]), p.dtype) for p in pair], 3 * len(pair))


def _reduce_second(sums):
    def plan(ins, outs, mk):
        x, y, c = lax.axis_index("x"), lax.axis_index("y"), lax.axis_index("c")
        phase = []
        for i in range(len(sums)):
            phase.append(mk.remote(ins[i].at[0, 1 - y], outs[i].at[0], (x, 1 - y, c)))
            phase.append(mk.remote(ins[i].at[1, 1 - x], outs[i].at[1], (1 - x, y, c)))
        return [phase]

    return _Exchange(plan, sums, [jax.ShapeDtypeStruct((2,) + s.shape[2:], s.dtype) for s in sums], 2 * len(sums))


def _pair_swap(halves):
    def plan(ins, outs, mk):
        x, y, c = lax.axis_index("x"), lax.axis_index("y"), lax.axis_index("c")
        return [[mk.remote(ins[i], outs[i], (x, y, 1 - c)) for i in range(len(halves))]]

    return _Exchange(plan, halves, [jax.ShapeDtypeStruct(h.shape, h.dtype) for h in halves], len(halves))


def _allreduce_small(packed):
    rows = packed.shape[0]
    n_dev = 2 * N_CHIPS

    def body(x_ref, o_ref, buf, send_sems, recv_sems):
        x, y, c, chips = _place()
        sibling = (x, y, 1 - c)

        def slot(px, py, pc):
            return buf.at[4 * px + 2 * py + pc]

        buf[4 * x + 2 * y + c] = x_ref[...]
        first = [_remote(x_ref, slot(x, y, c), send_sems, recv_sems, 0, sibling)]
        first += [_remote(x_ref, slot(x, y, c), send_sems, recv_sems, 1 + k, (*chip, c)) for k, chip in enumerate(chips)]
        for cp in first:
            cp.start()
        passed = []
        for k, chip in enumerate(chips):
            landed = slot(*chip, c)
            _remote(landed, landed, send_sems, recv_sems, 1 + k, (*chip, c)).wait_recv()
            passed.append(_remote(landed, landed, send_sems, recv_sems, 1 + N_REL + k, sibling))
            passed[-1].start()
        _remote(slot(*sibling), slot(*sibling), send_sems, recv_sems, 0, sibling).wait_recv()
        for k, chip in enumerate(chips):
            landed = slot(*chip, 1 - c)
            _remote(landed, landed, send_sems, recv_sems, 1 + N_REL + k, sibling).wait_recv()
        for cp in first + passed:
            cp.wait_send()
        total = buf[0]
        for d in range(1, n_dev):
            total = total + buf[d]
        o_ref[...] = total

    vm = pl.BlockSpec(memory_space=pltpu.VMEM)
    return pl.pallas_call(
        body, name="allreduce_small", in_specs=[vm], out_specs=vm, out_shape=jax.ShapeDtypeStruct(packed.shape, F32),
        scratch_shapes=[pltpu.VMEM((n_dev, rows, 128), F32), pltpu.SemaphoreType.DMA((1 + 2 * N_REL,)),
                        pltpu.SemaphoreType.DMA((1 + 2 * N_REL,))],
    )(packed)


def _row_tile(rows, cap=256):
    return max(t for t in range(8, cap + 1, 8) if rows % t == 0)


def _pair_sum(grad, got, c, name):
    _, half, cols = got.shape
    tr = _row_tile(half)
    nt = half // tr

    def body(c_ref, g_ref, r_ref, o_ref, wire_ref):
        total = g_ref[...] + r_ref[...]
        o_ref[...] = total
        wire_ref[...] = total.astype(BF16)

    blk = (1, tr, cols)
    out = pl.BlockSpec(blk, lambda j, t, c_ref: (j, t, 0))
    return pl.pallas_call(
        body, name=name,
        grid_spec=pltpu.PrefetchScalarGridSpec(
            num_scalar_prefetch=1, grid=(N_CHIPS, nt),
            in_specs=[pl.BlockSpec(blk, lambda j, t, c_ref: (j, c_ref[0] * nt + t, 0)), out], out_specs=[out, out]),
        out_shape=[jax.ShapeDtypeStruct(got.shape, F32), jax.ShapeDtypeStruct(got.shape, BF16)],
        compiler_params=_params("parallel", "parallel"),
    )(c, grad, got)


def _reduce_sum_first(pair, got, sel, name):
    _, _, q, cols = got.shape
    tr = _row_tile(q)
    nt = q // tr

    def body(sel_ref, p_ref, r_ref, o_ref, wire_ref):
        total = p_ref[0] + r_ref[0, 0].astype(F32)
        o_ref[0, 0] = total
        wire_ref[0, 0] = total.astype(BF16)

    blk = pl.BlockSpec((1, 1, tr, cols), lambda p, k, t, s: (p, k, t, 0))
    return pl.pallas_call(
        body, name=name,
        grid_spec=pltpu.PrefetchScalarGridSpec(
            num_scalar_prefetch=1, grid=(2, 2, nt),
            in_specs=[pl.BlockSpec((1, tr, cols), lambda p, k, t, s: (s[2 * p] + s[2 * p + 1] * k, p * nt + t, 0)), blk],
            out_specs=[blk, blk]),
        out_shape=[jax.ShapeDtypeStruct(got.shape, F32), jax.ShapeDtypeStruct(got.shape, BF16)],
        compiler_params=_params("parallel", "parallel", "parallel"),
    )(sel, pair, got)


def _reduce_sum_second(sums, got, sel, name):
    _, q, cols = got.shape
    tr = _row_tile(q)

    def body(sel_ref, s_ref, r_ref, o_ref):
        o_ref[0] = s_ref[0, 0] + r_ref[0].astype(F32)

    blk = (1, tr, cols)
    return pl.pallas_call(
        body, name=name,
        grid_spec=pltpu.PrefetchScalarGridSpec(
            num_scalar_prefetch=1, grid=(2, q // tr),
            in_specs=[pl.BlockSpec((1, 1, tr, cols), lambda p, t, s: (p, s[p], t, 0)),
                      pl.BlockSpec(blk, lambda p, t, s: (p, t, 0))],
            out_specs=pl.BlockSpec(blk, lambda p, t, s: (p, t, 0))),
        out_shape=jax.ShapeDtypeStruct(got.shape, F32), compiler_params=_params("parallel", "parallel"),
    )(sel, sums, got)


def _adamw_update(w, g, m, v):
    m2 = ADAM_B1 * m + (1.0 - ADAM_B1) * g
    v2 = ADAM_B2 * v + (1.0 - ADAM_B2) * (g * g)
    m_hat = m2 / (1.0 - ADAM_B1 ** ADAM_STEP)
    v_hat = v2 / (1.0 - ADAM_B2 ** ADAM_STEP)
    return -ADAM_LR * (m_hat / (jnp.sqrt(v_hat) + ADAM_EPS) + ADAM_WD * w), m2, v2


def _adamw_small(ws, gs, ms, vs):
    n = len(ws)

    def body(*refs):
        w, g, m, v, d, mo, vo = (refs[k * n:(k + 1) * n] for k in range(7))
        for i in range(n):
            d[i][...], mo[i][...], vo[i][...] = _adamw_update(w[i][...], g[i][...], m[i][...], v[i][...])

    shapes = [jax.ShapeDtypeStruct(t.shape, F32) for t in ws]
    outs = pl.pallas_call(body, name="adamw_small", out_shape=shapes * 3,
                          compiler_params=pltpu.CompilerParams(vmem_limit_bytes=VMEM_LIMIT_V7X))(*ws, *gs, *ms, *vs)
    return outs[:n], outs[n:2 * n], outs[2 * n:]


def _adamw_halves(w, mine, theirs, m, v, core, name):
    rows, cols = w.shape
    tr = _row_tile(rows // 2)
    per_half = rows // 2 // tr

    def body(core_ref, w_ref, a_ref, b_ref, m_ref, v_ref, g_ref, d_ref, mo_ref, vo_ref):
        g = jnp.where(pl.program_id(0) // per_half == core_ref[0], a_ref[...], b_ref[...])
        g_ref[...] = g
        d_ref[...], mo_ref[...], vo_ref[...] = _adamw_update(w_ref[...], g, m_ref[...], v_ref[...])

    blk = pl.BlockSpec((tr, cols), lambda t, c: (t, 0))
    half = lambda own: pl.BlockSpec(
        (tr, cols), lambda t, c: (jnp.clip(t - (c[0] if own else 1 - c[0]) * per_half, 0, per_half - 1), 0))
    return pl.pallas_call(
        body, name=name,
        grid_spec=pltpu.PrefetchScalarGridSpec(
            num_scalar_prefetch=1, grid=(2 * per_half,), in_specs=[blk, half(True), half(False), blk, blk], out_specs=[blk] * 4),
        out_shape=[jax.ShapeDtypeStruct(w.shape, F32)] * 4, compiler_params=_params("arbitrary"),
    )(core, w, mine, theirs, m, v)


def _pack(parts):
    flat = []
    for t in parts:
        t = t.reshape(-1).astype(F32)
        flat.append(jnp.pad(t, (0, -t.shape[0] % 128)))
    flat = jnp.concatenate(flat)
    return jnp.pad(flat, (0, -flat.shape[0] % 1024)).reshape(-1, 128)


def _unpack(buf, shapes):
    flat, out, at = buf.reshape(-1), [], 0
    for s in shapes:
        size = math.prod(s)
        out.append(flat[at:at + size].reshape(s))
        at += size + (-size % 128)
    return out


def kernel(x, ffn1_pre_g, ffn1_w_in, ffn1_w_out, ffn1_post_g, mix_pre_g, w_mix_in, a_re, a_im, log_dt, b_re, b_im, c_re, c_im, d_skip, w_glu, b_glu, w_mix_out, mix_post_g, ffn2_pre_g, ffn2_w_in, ffn2_w_out, ffn2_post_g, loss_target, m_ffn1_pre_g, m_ffn1_w_in, m_ffn1_w_out, m_ffn1_post_g, m_mix_pre_g, m_w_mix_in, m_a_re, m_a_im, m_log_dt, m_b_re, m_b_im, m_c_re, m_c_im, m_d_skip, m_w_glu, m_b_glu, m_w_mix_out, m_mix_post_g, m_ffn2_pre_g, m_ffn2_w_in, m_ffn2_w_out, m_ffn2_post_g, v_ffn1_pre_g, v_ffn1_w_in, v_ffn1_w_out, v_ffn1_post_g, v_mix_pre_g, v_w_mix_in, v_a_re, v_a_im, v_log_dt, v_b_re, v_b_im, v_c_re, v_c_im, v_d_skip, v_w_glu, v_b_glu, v_w_mix_out, v_mix_post_g, v_ffn2_pre_g, v_ffn2_w_in, v_ffn2_w_out, v_ffn2_post_g):
    given = dict(locals())
    order = ("ffn1_pre_g", "ffn1_w_in", "ffn1_w_out", "ffn1_post_g", "mix_pre_g", "w_mix_in", "a_re", "a_im", "log_dt",
             "b_re", "b_im", "c_re", "c_im", "d_skip", "w_glu", "b_glu", "w_mix_out", "mix_post_g", "ffn2_pre_g",
             "ffn2_w_in", "ffn2_w_out", "ffn2_post_g")
    at_x, at_y, at_c = (lax.axis_index(a).astype(jnp.int32) for a in ("x", "y", "c"))
    place = dict(core=at_c.reshape(1), sel_first=jnp.stack([2 * at_x, jnp.int32(1), at_y, jnp.int32(2)]),
                 sel_second=jnp.stack([at_y, at_x]))

    shards = {n: given[n][0] for n in BIG}
    w = {n: shards[n].astype(BF16) for n in REST}
    w.update(zip(FIRST, _gather_weights([shards[n].astype(BF16) for n in FIRST]).run("gather_first")))
    small = {n: given[n][0] for n in SMALL}
    loss_rows, grad_x, g = _local_step(x[0], loss_target[0], small, w, place)

    total = _allreduce_small(_pack([g[n] for n in SMALL] + [loss_rows[0, :1]]))
    parts = _unpack(total, [small[n].shape for n in SMALL] + [(1,)])
    grads = dict(zip(SMALL, parts[:-1]))
    loss = parts[-1][0]

    delta, new_m, new_v = {}, {}, {}
    for n in BIG:
        grads[n], delta[n], new_m[n], new_v[n] = _adamw_halves(
            shards[n], *g[n], given["m_" + n][0], given["v_" + n][0], place["core"], name=f"adamw_{n}")
    take = lambda pre: [given[pre + n] for n in SMALL]
    outs = _adamw_small(take(""), [grads[n][None] for n in SMALL], take("m_"), take("v_"))
    for store, arrays in zip((delta, new_m, new_v), outs):
        store.update({n: t[0] for n, t in zip(SMALL, arrays)})

    lead = lambda d: [d[n][None] for n in order]
    return (loss, grad_x[None], *lead(grads), *lead(delta), *lead(new_m), *lead(new_v))
```

```python
import functools
import math

import jax
import jax.numpy as jnp
from jax import lax
from jax.experimental import pallas as pl
from jax.experimental.pallas import tpu as pltpu

F32, BF16 = jnp.float32, jnp.bfloat16

D_MODEL = 1024
D_FF = 2816
N_CHIPS = 4
FF_BLK = 2 * D_FF // N_CHIPS
ATTN_W = 512
SSM_W = 512
HEAD_DIM = 64
N_HEADS = ATTN_W // HEAD_DIM
DILATIONS = (1, 4, 16)
N_BACK = 128
QBLK = 128
N_GROUPS = 32
GROUP_CH = 16
STATE = 64
N_STATE = N_GROUPS * STATE
EPS = 1e-6
NEG = -1e30
GELU_C = math.sqrt(2.0 / math.pi)

ADAM_LR, ADAM_B1, ADAM_B2, ADAM_EPS, ADAM_WD, ADAM_STEP = 0.001, 0.9, 0.999, 1e-08, 0.01, 10

VMEM_LIMIT_V7X = 60 * 1024 * 1024
ROW_TILE = 512
FFN_ROW_TILE = 512
DW_ROW_TILE = 1024


def _params(*sem):
    return pltpu.CompilerParams(dimension_semantics=sem, vmem_limit_bytes=VMEM_LIMIT_V7X)


def _dot(a, b):
    return jnp.dot(a.astype(BF16), b.astype(BF16), preferred_element_type=F32)


def _dot_nt(a, b):
    return lax.dot_general(a.astype(BF16), b.astype(BF16), (((1,), (1,)), ((), ())), preferred_element_type=F32)


def _dot_tn(a, b):
    return lax.dot_general(a.astype(BF16), b.astype(BF16), (((0,), (0,)), ((), ())), preferred_element_type=F32)


def _full(shape):
    return pl.BlockSpec(shape, lambda *_: (0,) * len(shape))


def _rows(tm, width):
    return pl.BlockSpec((tm, width), lambda i: (i, 0))


ANY = pl.BlockSpec(memory_space=pl.ANY)


def _load_once(pairs, sems):
    copies = [pltpu.make_async_copy(src, dst, sems.at[k]) for k, (src, dst) in enumerate(pairs)]
    for c in copies:
        c.start()
    for c in copies:
        c.wait()


def _rms(x):
    return lax.rsqrt(jnp.mean(x * x, axis=-1, keepdims=True) + EPS)


def _rms_bwd(dy_g, xn, r):
    return r * (dy_g - xn * jnp.mean(dy_g * xn, axis=-1, keepdims=True))


def _ffn_fwd(x, g_pre, w_in, w_out, g_post, target, *, name, rider=None):
    T = x.shape[0]
    tm = FFN_ROW_TILE
    with_loss = target is not None

    def body(*refs):
        if with_loss:
            x_ref, gpre_ref, gpost_ref, tgt_ref, win_hbm, wout_hbm, o_ref, loss_ref, z_ref, f_ref, win_v, wout_v, sems = refs
        else:
            x_ref, gpre_ref, gpost_ref, win_hbm, wout_hbm, o_ref, z_ref, f_ref, win_v, wout_v, sems = refs

        @pl.when(pl.program_id(0) == 0)
        def _():
            _load_once([(win_hbm, win_v), (wout_hbm, wout_v)], sems)
            if with_loss:
                loss_ref[...] = jnp.zeros_like(loss_ref)

        xv = x_ref[...]
        h = (xv * _rms(xv) * gpre_ref[...]).astype(BF16)
        f = jnp.zeros((tm, D_MODEL), F32)
        for k in range(2):
            gate = _dot(h, win_v[k])
            up = _dot(h, win_v[k + 2])
            z_ref[:, k * FF_BLK:(k + 1) * FF_BLK] = gate.astype(BF16)
            z_ref[:, D_FF + k * FF_BLK:D_FF + (k + 1) * FF_BLK] = up.astype(BF16)
            f = f + _dot(gate * jax.nn.sigmoid(gate) * up, wout_v[k])
        f_ref[...] = f
        out = xv + 0.5 * (f * _rms(f) * gpost_ref[...])
        if with_loss:
            err = out - tgt_ref[...]
            o_ref[...] = err * (1.0 / D_MODEL)
            loss_ref[...] += jnp.sum(err * err) * (0.5 / D_MODEL)
        else:
            o_ref[...] = out

    row = _rows(tm, D_MODEL)
    vec = _full((1, D_MODEL))
    in_specs = [row, vec, vec] + ([row] if with_loss else []) + [ANY, ANY]
    out_shape = [jax.ShapeDtypeStruct((T, D_MODEL), F32)]
    out_specs = [row]
    if with_loss:
        out_shape.append(jax.ShapeDtypeStruct((8, 128), F32))
        out_specs.append(_full((8, 128)))
    out_shape += [jax.ShapeDtypeStruct((T, 2 * D_FF), BF16), jax.ShapeDtypeStruct((T, D_MODEL), F32)]
    out_specs += [_rows(tm, 2 * D_FF), row]
    args = (x, g_pre, g_post) + ((target,) if with_loss else ()) + (w_in, w_out)
    return _pallas(
        body, name=name, grid=(T // tm,), in_specs=in_specs, out_specs=out_specs, out_shape=out_shape,
        scratch_shapes=[pltpu.VMEM(w_in.shape, BF16), pltpu.VMEM(w_out.shape, BF16), pltpu.SemaphoreType.DMA((2,))],
        semantics=("arbitrary",), args=args, rider=rider)


def _ffn_bwd_out(dout, f, z, w_out, g_post, *, name):
    T = dout.shape[0]
    tm = FFN_ROW_TILE
    nt = T // tm

    def body(dout_ref, f_ref, z_ref, gpost_ref, wout_hbm, dz_ref, dgpost_ref, dwout_hbm, wout_v, dwout_v, sems):
        i = pl.program_id(0)

        @pl.when(i == 0)
        def _():
            _load_once([(wout_hbm, wout_v)], sems)
            dwout_v[...] = jnp.zeros_like(dwout_v)
            dgpost_ref[...] = jnp.zeros_like(dgpost_ref)

        dy = 0.5 * dout_ref[...]
        f = f_ref[...]
        r = _rms(f)
        fn = f * r
        dgpost_ref[...] += jnp.sum(dy * fn, axis=0, keepdims=True)
        df = _rms_bwd(dy * gpost_ref[...], fn, r).astype(BF16)
        for k in range(2):
            gate = z_ref[:, k * FF_BLK:(k + 1) * FF_BLK].astype(F32)
            up = z_ref[:, D_FF + k * FF_BLK:D_FF + (k + 1) * FF_BLK].astype(F32)
            sg = jax.nn.sigmoid(gate)
            silu = gate * sg
            dwout_v[k] += _dot_tn(silu * up, df)
            da = _dot_nt(df, wout_v[k])
            dz_ref[:, k * FF_BLK:(k + 1) * FF_BLK] = (da * up * (sg * (1.0 + gate * (1.0 - sg)))).astype(BF16)
            dz_ref[:, D_FF + k * FF_BLK:D_FF + (k + 1) * FF_BLK] = (da * silu).astype(BF16)

        @pl.when(i == nt - 1)
        def _():
            c = pltpu.make_async_copy(dwout_v, dwout_hbm, sems.at[0])
            c.start()
            c.wait()

    row = _rows(tm, D_MODEL)
    return pl.pallas_call(
        body, name=name, grid=(nt,),
        in_specs=[row, row, _rows(tm, 2 * D_FF), _full((1, D_MODEL)), ANY],
        out_specs=[_rows(tm, 2 * D_FF), _full((1, D_MODEL)), ANY],
        out_shape=[jax.ShapeDtypeStruct((T, 2 * D_FF), BF16), jax.ShapeDtypeStruct((1, D_MODEL), F32),
                   jax.ShapeDtypeStruct(w_out.shape, F32)],
        scratch_shapes=[pltpu.VMEM(w_out.shape, BF16), pltpu.VMEM(w_out.shape, F32), pltpu.SemaphoreType.DMA((1,))],
        compiler_params=_params("arbitrary"),
    )(dout, f, z, g_post, w_out)


def _norm_matmul_dw(dz, x, g, n_blocks, *, name, rider=None, together=1):
    T = x.shape[0]
    bw = dz.shape[1] // n_blocks
    tm = DW_ROW_TILE

    def body(dz_ref, x_ref, g_ref, dw_ref):
        @pl.when(pl.program_id(1) == 0)
        def _():
            dw_ref[...] = jnp.zeros_like(dw_ref)

        xv = x_ref[...]
        h_t = (xv * _rms(xv) * g_ref[...]).T.astype(BF16)
        for b in range(together):
            dw_ref[b] += jnp.dot(h_t, dz_ref[:, b * bw:(b + 1) * bw], preferred_element_type=F32)

    res = _pallas(
        body, name=name, grid=(n_blocks // together, T // tm),
        in_specs=[pl.BlockSpec((tm, together * bw), lambda j, t: (t, j)), pl.BlockSpec((tm, D_MODEL), lambda j, t: (t, 0)),
                  _full((1, D_MODEL))],
        out_specs=[pl.BlockSpec((together, D_MODEL, bw), lambda j, t: (j, 0, 0))],
        out_shape=[jax.ShapeDtypeStruct((n_blocks, D_MODEL, bw), F32)], semantics=("parallel", "arbitrary"),
        args=(dz, x, g), rider=rider)
    return res[0] if rider is None else (res[0][0], res[1])


def _norm_matmul_dx(dz, x, dres, g, w, *, name, rider=None, part=(0, 1), into=None):
    T = x.shape[0]
    nb, _, bw = w.shape
    tm = FFN_ROW_TILE
    first, steps = round(part[0] * T) // tm, round((part[1] - part[0]) * T) // tm

    def body(dz_ref, x_ref, dres_ref, g_ref, w_hbm, *rest):
        dx_ref, dg_ref, w_v, sems = rest[-4:]

        @pl.when(pl.program_id(0) == 0)
        def _():
            _load_once([(w_hbm, w_v)], sems)
            dg_ref[...] = jnp.zeros_like(dg_ref)

        xv = x_ref[...]
        r = _rms(xv)
        xn = xv * r
        gv = g_ref[...]
        dh = jnp.zeros((tm, D_MODEL), F32)
        for j in range(nb):
            dh = dh + _dot_nt(dz_ref[:, j * bw:(j + 1) * bw], w_v[j])
        dg_ref[...] += jnp.sum(dh * xn, axis=0, keepdims=True)
        dx_ref[...] = _rms_bwd(dh * gv, xn, r) + dres_ref[...]

    rows = lambda width: pl.BlockSpec((tm, width), lambda i: (first + i, 0))
    row = rows(D_MODEL)
    return _pallas(
        body, name=name, grid=(steps,),
        in_specs=[rows(nb * bw), row, row, _full((1, D_MODEL)), ANY] + ([] if into is None else [ANY]),
        out_specs=[row, _full((1, D_MODEL))],
        out_shape=[jax.ShapeDtypeStruct((T, D_MODEL), F32), jax.ShapeDtypeStruct((1, D_MODEL), F32)],
        scratch_shapes=[pltpu.VMEM(w.shape, BF16), pltpu.SemaphoreType.DMA((1,))],
        semantics=("arbitrary",), args=(dz, x, dres, g, w) + (() if into is None else (into,)), rider=rider,
        aliases={} if into is None else {5: 0})


def _mix_in_fwd(x, g, w):
    T = x.shape[0]
    tm = ROW_TILE

    def body(x_ref, g_ref, w_ref, q_ref, k_ref, v_ref, u_ref):
        xv = x_ref[...]
        h = (xv * _rms(xv) * g_ref[...]).astype(BF16)
        for j, o_ref in enumerate((q_ref, k_ref, v_ref, u_ref)):
            o_ref[...] = _dot(h, w_ref[j])

    col = _rows(tm, ATTN_W)
    return pl.pallas_call(
        body, name="mix_in_fwd", grid=(T // tm,),
        in_specs=[_rows(tm, D_MODEL), _full((1, D_MODEL)), _full(w.shape)],
        out_specs=[col] * 4, out_shape=[jax.ShapeDtypeStruct((T, ATTN_W), F32)] * 4,
        compiler_params=_params("parallel"),
    )(x, g, w)


def _mix_out_fwd(x, attn, ssm, w, g):
    T = x.shape[0]
    tm = ROW_TILE

    def body(x_ref, a_ref, s_ref, w_ref, g_ref, o_ref, m_ref):
        mixed = _dot(a_ref[...], w_ref[0]) + _dot(s_ref[...], w_ref[1])
        m_ref[...] = mixed
        o_ref[...] = x_ref[...] + mixed * _rms(mixed) * g_ref[...]

    row, col = _rows(tm, D_MODEL), _rows(tm, ATTN_W)
    return pl.pallas_call(
        body, name="mix_out_fwd", grid=(T // tm,),
        in_specs=[row, col, col, _full(w.shape), _full((1, D_MODEL))],
        out_specs=[row, row], out_shape=[jax.ShapeDtypeStruct((T, D_MODEL), F32)] * 2,
        compiler_params=_params("parallel"),
    )(x, attn, ssm, w, g)


def _mix_out_bwd(dout, mixed, attn, ssm, w, g, rider=None):
    T = dout.shape[0]
    tm = ROW_TILE

    def body(dout_ref, m_ref, a_ref, s_ref, w_ref, g_ref, da_ref, ds_ref, dw_ref, dg_ref):
        @pl.when(pl.program_id(0) == 0)
        def _():
            dw_ref[...] = jnp.zeros_like(dw_ref)
            dg_ref[...] = jnp.zeros_like(dg_ref)

        dy = dout_ref[...]
        mixed = m_ref[...]
        r = _rms(mixed)
        mn = mixed * r
        dg_ref[...] += jnp.sum(dy * mn, axis=0, keepdims=True)
        dm = _rms_bwd(dy * g_ref[...], mn, r).astype(BF16)
        da_ref[...] = _dot_nt(dm, w_ref[0])
        ds_ref[...] = _dot_nt(dm, w_ref[1])
        dw_ref[0] += _dot_tn(a_ref[...], dm)
        dw_ref[1] += _dot_tn(s_ref[...], dm)

    row, col = _rows(tm, D_MODEL), _rows(tm, ATTN_W)
    return _pallas(
        body, name="mix_out_bwd", grid=(T // tm,),
        in_specs=[row, row, col, col, _full(w.shape), _full((1, D_MODEL))],
        out_specs=[col, col, _full(w.shape), _full((1, D_MODEL))],
        out_shape=[jax.ShapeDtypeStruct((T, ATTN_W), F32)] * 2
        + [jax.ShapeDtypeStruct(w.shape, F32), jax.ShapeDtypeStruct((1, D_MODEL), F32)],
        semantics=("arbitrary",), args=(dout, mixed, attn, ssm, w, g), rider=rider)


ATTN_TILING = {1: (ATTN_W, 1), 4: (2 * HEAD_DIM, 4), 16: (2 * HEAD_DIM, 4)}


def _class_rows(d, r):
    return (pl.ds(r, QBLK, stride=d), slice(None)) if d > 1 else (slice(None), slice(None))


def _for_class_groups(d, group, fn):
    if d == group:
        fn(0)
    else:
        lax.fori_loop(0, d // group, lambda n, carry: (fn(n * group), carry)[1], 0)


def _block_slopes(lanes, lane_block):
    heads = lanes // HEAD_DIM
    first = lane_block * heads
    return [jnp.exp2(-jnp.full((1, 1), first + hh + 1, jnp.int32).astype(F32)) for hh in range(heads)]


def _attn_specs(d, nb, lanes):
    blk = (QBLK * d, lanes)
    cur = pl.BlockSpec(blk, lambda j, lb: (j, lb))
    prev = pl.BlockSpec(blk, lambda j, lb: (jnp.maximum(j - 1, 0), lb))
    nxt = pl.BlockSpec(blk, lambda j, lb: (jnp.minimum(j + 1, nb - 1), lb))
    return cur, prev, nxt


def _attn_branch_fwd(q, k, v, d):
    T = q.shape[0]
    nb = T // (d * QBLK)
    lanes, group = ATTN_TILING[d]
    scale = HEAD_DIM ** -0.5

    def body(q_ref, kc_ref, kp_ref, vc_ref, vp_ref, o_ref, l_ref, q_s, kk_s, vv_s, o_s, l_s):
        j = pl.program_id(0)
        qi = lax.broadcasted_iota(jnp.int32, (QBLK, 2 * QBLK), 0)
        ci = lax.broadcasted_iota(jnp.int32, (QBLK, 2 * QBLK), 1)
        steps = QBLK + qi - ci
        valid = (steps >= 0) & (steps <= N_BACK) & ((ci >= QBLK) | (j > 0))
        dist = (steps * d).astype(F32)
        slopes = _block_slopes(lanes, pl.program_id(1))

        def classes(first):
            for n in range(group):
                rows = _class_rows(d, first + n)
                q_s[n] = q_ref[rows]
                kk_s[n, :QBLK], kk_s[n, QBLK:] = kp_ref[rows], kc_ref[rows]
                vv_s[n, :QBLK], vv_s[n, QBLK:] = vp_ref[rows], vc_ref[rows]
            for n in range(group):
                for hh in range(lanes // HEAD_DIM):
                    sl = slice(hh * HEAD_DIM, (hh + 1) * HEAD_DIM)
                    s = _dot_nt(q_s[n, :, sl], kk_s[n, :, sl]) * scale - slopes[hh] * dist
                    s = jnp.where(valid, s, NEG)
                    m = jnp.max(s, axis=-1, keepdims=True)
                    p = jnp.exp(s - m)
                    den = jnp.sum(p, axis=-1, keepdims=True)
                    o_s[n, :, sl] = _dot(p, vv_s[n, :, sl]) / den
                    l_s[n, :, sl] = jnp.broadcast_to(m + jnp.log(den), (QBLK, HEAD_DIM))
            for n in range(group):
                rows = _class_rows(d, first + n)
                o_ref[rows] = o_s[n]
                l_ref[rows] = l_s[n]

        _for_class_groups(d, group, classes)

    cur, prev, _ = _attn_specs(d, nb, lanes)
    shape = jax.ShapeDtypeStruct((T, ATTN_W), F32)
    one, two = pltpu.VMEM((group, QBLK, lanes), F32), pltpu.VMEM((group, 2 * QBLK, lanes), F32)
    return pl.pallas_call(
        body, name=f"attn_fwd_d{d}", grid=(nb, ATTN_W // lanes),
        in_specs=[cur, cur, prev, cur, prev], out_specs=[cur, cur], out_shape=[shape, shape],
        scratch_shapes=[one, two, two, one, one], compiler_params=_params("parallel", "parallel"),
    )(q, k, k, v, v)


def _attn_merge(outs, lses):
    T = outs[0].shape[0]
    tm = 512

    def body(o1, o2, o3, l1, l2, l3, a_ref, lse_ref):
        ls = [l1[...], l2[...], l3[...]]
        m = jnp.maximum(jnp.maximum(ls[0], ls[1]), ls[2])
        lse = m + jnp.log(jnp.exp(ls[0] - m) + jnp.exp(ls[1] - m) + jnp.exp(ls[2] - m))
        lse_ref[...] = lse
        a_ref[...] = jnp.exp(ls[0] - lse) * o1[...] + jnp.exp(ls[1] - lse) * o2[...] + jnp.exp(ls[2] - lse) * o3[...]

    col = _rows(tm, ATTN_W)
    return pl.pallas_call(
        body, name="attn_merge", grid=(T // tm,), in_specs=[col] * 6, out_specs=[col, col],
        out_shape=[jax.ShapeDtypeStruct((T, ATTN_W), F32)] * 2, compiler_params=_params("parallel"),
    )(*outs, *lses)


def _attn_branch_bwd(q, k, v, o, lse, do, d, rider=None):
    T = q.shape[0]
    nb = T // (d * QBLK)
    lanes, group = ATTN_TILING[d]
    scale = HEAD_DIM ** -0.5

    def body(q_ref, kc_ref, kp_ref, vc_ref, vp_ref, o_ref, l_ref, do_ref, dq_ref, dk_ref, dv_ref,
             q_s, o_s, l_s, do_s, kk_s, vv_s, dq_s, dk_s, dv_s, ck_s, cv_s):
        j = pl.program_id(1)

        @pl.when(j == 0)
        def _():
            ck_s[...] = jnp.zeros_like(ck_s)
            cv_s[...] = jnp.zeros_like(cv_s)

        qi = lax.broadcasted_iota(jnp.int32, (QBLK, 2 * QBLK), 0)
        ci = lax.broadcasted_iota(jnp.int32, (QBLK, 2 * QBLK), 1)
        steps = QBLK + qi - ci
        valid = (steps >= 0) & (steps <= N_BACK) & ((ci >= QBLK) | (j > 0))
        dist = (steps * d).astype(F32)
        lo, hi = slice(0, QBLK), slice(QBLK, 2 * QBLK)
        slopes = _block_slopes(lanes, pl.program_id(0))

        def classes(first):
            for n in range(group):
                rows = _class_rows(d, first + n)
                q_s[n], o_s[n], l_s[n], do_s[n] = q_ref[rows], o_ref[rows], l_ref[rows], do_ref[rows]
                kk_s[n, lo], kk_s[n, hi] = kp_ref[rows], kc_ref[rows]
                vv_s[n, lo], vv_s[n, hi] = vp_ref[rows], vc_ref[rows]
            for n in range(group):
                for hh in range(lanes // HEAD_DIM):
                    sl = slice(hh * HEAD_DIM, (hh + 1) * HEAD_DIM)
                    qh, doh, kk, vv = q_s[n, :, sl], do_s[n, :, sl], kk_s[n, :, sl], vv_s[n, :, sl]
                    delta = jnp.sum(doh * o_s[n, :, sl], axis=-1, keepdims=True)
                    s = jnp.where(valid, _dot_nt(qh, kk) * scale - slopes[hh] * dist, NEG)
                    p = jnp.exp(s - l_s[n, :, hh * HEAD_DIM:hh * HEAD_DIM + 1])
                    ds = p * (_dot_nt(doh, vv) - delta)
                    dq_s[n, :, sl] = _dot(ds, kk) * scale
                    dkk = _dot_tn(ds, qh) * scale
                    dvv = _dot_tn(p, doh)
                    dk_s[n, :, sl] = ck_s[first + n, :, sl] + dkk[lo]
                    dv_s[n, :, sl] = cv_s[first + n, :, sl] + dvv[lo]
                    ck_s[first + n, :, sl] = dkk[hi]
                    cv_s[first + n, :, sl] = dvv[hi]
            for n in range(group):
                rows = _class_rows(d, first + n)
                dq_ref[rows] = dq_s[n]
                dk_ref[rows] = dk_s[n]
                dv_ref[rows] = dv_s[n]

        @pl.when(j < nb)
        def _():
            _for_class_groups(d, group, classes)

        @pl.when(j == nb)
        def _():
            for r in range(d):
                dk_ref[_class_rows(d, r)] = ck_s[r]
                dv_ref[_class_rows(d, r)] = cv_s[r]

    blk = (QBLK * d, lanes)
    here = lambda j: jnp.minimum(j, nb - 1)
    cur = pl.BlockSpec(blk, lambda lb, j: (here(j), lb))
    prev = pl.BlockSpec(blk, lambda lb, j: (jnp.maximum(here(j) - 1, 0), lb))
    behind = pl.BlockSpec(blk, lambda lb, j: (jnp.maximum(j - 1, 0), lb))
    shape = jax.ShapeDtypeStruct((T, ATTN_W), F32)
    one, two = pltpu.VMEM((group, QBLK, lanes), F32), pltpu.VMEM((group, 2 * QBLK, lanes), F32)
    carry = pltpu.VMEM((d, QBLK, lanes), F32)
    return _pallas(
        body, name=f"attn_bwd_d{d}", grid=(ATTN_W // lanes, nb + 1),
        in_specs=[cur, cur, prev, cur, prev, cur, cur, cur], out_specs=[cur, behind, behind], out_shape=[shape] * 3,
        scratch_shapes=[one] * 4 + [two] * 2 + [one] * 3 + [carry] * 2,
        semantics=("parallel", "arbitrary"), args=(q, k, k, v, v, o, lse, do), rider=rider)


def _dproj_merge(dqs, dks, dvs, du):
    T = du.shape[0]
    tm = 512

    def body(*refs):
        o_ref = refs[-1]
        for part in range(3):
            a, b, c = refs[3 * part:3 * part + 3]
            o_ref[:, part * ATTN_W:(part + 1) * ATTN_W] = (a[...] + b[...] + c[...]).astype(BF16)
        o_ref[:, 3 * ATTN_W:] = refs[9][...].astype(BF16)

    col = _rows(tm, ATTN_W)
    return pl.pallas_call(
        body, name="dproj_merge", grid=(T // tm,), in_specs=[col] * 10, out_specs=_rows(tm, 4 * ATTN_W),
        out_shape=jax.ShapeDtypeStruct((T, 4 * ATTN_W), BF16), compiler_params=_params("parallel"),
    )(*dqs, *dks, *dvs, du)


def _attention_fwd(q, k, v):
    res = [_attn_branch_fwd(q, k, v, d) for d in DILATIONS]
    return _attn_merge([r[0] for r in res], [r[1] for r in res])


SCAN_ROWS = 8
SCAN_LANES = 512
SSM_CHUNK = 512
SSM_CHUNK_BWD = 512
SSM_HALVES = tuple((slice(h * SSM_W // 2, (h + 1) * SSM_W // 2), slice(h * N_STATE // 2, (h + 1) * N_STATE // 2)) for h in range(2))


def _cmul(ar, ai, br, bi):
    return ar * br - ai * bi, ar * bi + ai * br


def _ssm_discretize(a_re, a_im, log_dt, b_re, b_im):
    def body(ar_ref, ai_ref, ldt_ref, br_ref, bi_ref, abr_ref, abi_ref, er_ref, ei_ref, bbr_ref, bbi_ref, pr_ref, pi_ref):
        ar, ai = ar_ref[...], ai_ref[...]
        dt = jnp.exp(ldt_ref[...])
        n = lax.broadcasted_iota(jnp.int32, (1, SCAN_ROWS), 1).astype(F32) + 1.0
        mag, ang = jnp.exp(dt * ar), dt * ai
        abr, abi = mag * jnp.cos(ang), mag * jnp.sin(ang)
        abr_ref[...], abi_ref[...] = abr, abi
        pr_ref[...] = jnp.exp(dt * ar * n) * jnp.cos(ang * n)
        pi_ref[...] = jnp.exp(dt * ar * n) * jnp.sin(ang * n)
        den = ar * ar + ai * ai
        er = ((abr - 1.0) * ar + abi * ai) / den
        ei = (abi * ar - (abr - 1.0) * ai) / den
        er_ref[...], ei_ref[...] = er, ei
        bbr_ref[...], bbi_ref[...] = _cmul(er, ei, br_ref[...], bi_ref[...])

    col = jax.ShapeDtypeStruct((N_STATE, 1), F32)
    mat = jax.ShapeDtypeStruct((N_STATE, GROUP_CH), F32)
    pw = jax.ShapeDtypeStruct((N_STATE, SCAN_ROWS), F32)
    return pl.pallas_call(body, name="ssm_discretize", out_shape=[col] * 4 + [mat] * 2 + [pw] * 2)(
        a_re, a_im, log_dt, b_re, b_im)


def _ssm_discretize_bwd(a_re, a_im, log_dt, b_re, b_im, ab_re, ab_im, e_re, e_im, dab_re, dab_im, dbb_re, dbb_im):
    def body(ar_ref, ai_ref, ldt_ref, br_ref, bi_ref, abr_ref, abi_ref, er_ref, ei_ref, dabr_ref, dabi_ref,
             dbbr_ref, dbbi_ref, dar_ref, dai_ref, ddt_ref, dbr_ref, dbi_ref):
        ar, ai, dt = ar_ref[...], ai_ref[...], jnp.exp(ldt_ref[...])
        er, ei = er_ref[...], ei_ref[...]
        gbr, gbi = dbbr_ref[...], dbbi_ref[...]
        dbr_ref[...], dbi_ref[...] = _cmul(er, -ei, gbr, gbi)
        br, bi = br_ref[...], bi_ref[...]
        der = jnp.sum(br * gbr + bi * gbi, axis=-1, keepdims=True)
        dei = jnp.sum(br * gbi - bi * gbr, axis=-1, keepdims=True)
        den = ar * ar + ai * ai
        inv_r, inv_i = ar / den, -ai / den
        t_r, t_i = _cmul(der, dei, inv_r, -inv_i)
        gab_r, gab_i = dabr_ref[...] + t_r, dabi_ref[...] + t_i
        q_r, q_i = _cmul(er, ei, inv_r, inv_i)
        dl_r, dl_i = _cmul(der, dei, q_r, -q_i)
        dl_r, dl_i = -dl_r, -dl_i
        gw_r, gw_i = _cmul(gab_r, gab_i, abr_ref[...], -abi_ref[...])
        dar_ref[...] = dl_r + dt * gw_r
        dai_ref[...] = dl_i + dt * gw_i
        ddt_ref[...] = (gw_r * ar + gw_i * ai) * dt

    col = jax.ShapeDtypeStruct((N_STATE, 1), F32)
    mat = jax.ShapeDtypeStruct((N_STATE, GROUP_CH), F32)
    return pl.pallas_call(body, name="ssm_discretize_bwd", out_shape=[col] * 3 + [mat] * 2)(
        a_re, a_im, log_dt, b_re, b_im, ab_re, ab_im, e_re, e_im, dab_re, dab_im, dbb_re, dbb_im)


def _scan_tables(p_re, p_im, reverse):
    pr, pi = p_re.T, p_im.T
    if reverse:
        pi = -pi
    row = jnp.arange(SCAN_ROWS)[:, None]
    level = lambda t, s: jnp.where((row < SCAN_ROWS - s) if reverse else (row >= s), t[s - 1][None, :], 0.0)
    carry = (pr[::-1], pi[::-1]) if reverse else (pr, pi)
    return jnp.stack([level(pr, 1), level(pi, 1), level(pr, 2), level(pi, 2), level(pr, 4), level(pi, 4), carry[0], carry[1]])


def _scan_group(xr, xi, tab_ref, ls, carry_r, carry_i, reverse):
    for n, s in enumerate((1, 2, 4)):
        shift = SCAN_ROWS - s if reverse else s
        mr, mi = _cmul(tab_ref[2 * n, :, ls], tab_ref[2 * n + 1, :, ls], pltpu.roll(xr, shift, 0), pltpu.roll(xi, shift, 0))
        xr, xi = xr + mr, xi + mi
    mr, mi = _cmul(tab_ref[6, :, ls], tab_ref[7, :, ls], carry_r, carry_i)
    return xr + mr, xi + mi


def _gelu(y):
    t = jnp.tanh(GELU_C * (y + 0.044715 * y * y * y))
    return 0.5 * y * (1.0 + t), t


def _ssm_fwd(u, tab, bd_re, bd_im, cd_re, cd_im, d_skip, w_glu, b_glu):
    T = u.shape[0]
    tc = SSM_CHUNK

    def body(u_ref, tab_ref, bdr_ref, bdi_ref, cdr_ref, cdi_ref, dsk_ref, wg_ref, bg_ref,
             sr_ref, si_ref, yp_ref, o_ref, car_r, car_i):
        @pl.when(pl.program_id(0) == 0)
        def _():
            car_r[...] = jnp.zeros_like(car_r)
            car_i[...] = jnp.zeros_like(car_i)

        uv = u_ref[...]
        for cs, ss in SSM_HALVES:
            sr_ref[:, ss] = _dot(uv[:, cs], bdr_ref[cs, ss])
            si_ref[:, ss] = _dot(uv[:, cs], bdi_ref[cs, ss])
        for lb in range(N_STATE // SCAN_LANES):
            ls = pl.ds(lb * SCAN_LANES, SCAN_LANES)

            def step(g, carry):
                rows = pl.ds(pl.multiple_of(g * SCAN_ROWS, SCAN_ROWS), SCAN_ROWS)
                xr, xi = _scan_group(sr_ref[rows, ls], si_ref[rows, ls], tab_ref, ls, carry[0], carry[1], False)
                sr_ref[rows, ls] = xr
                si_ref[rows, ls] = xi
                last = slice(SCAN_ROWS - 1, SCAN_ROWS)
                return (jnp.broadcast_to(xr[last], xr.shape), jnp.broadcast_to(xi[last], xi.shape))

            cr, ci = lax.fori_loop(0, tc // SCAN_ROWS, step, (car_r[:, ls], car_i[:, ls]))
            car_r[:, ls] = cr
            car_i[:, ls] = ci
        y = jnp.concatenate([_dot(sr_ref[:, ss], cdr_ref[ss, cs]) - _dot(si_ref[:, ss], cdi_ref[ss, cs])
                             for cs, ss in SSM_HALVES], axis=1) + dsk_ref[...] * uv
        yp_ref[...] = y
        gy, _ = _gelu(y)
        o_ref[...] = gy * jax.nn.sigmoid(_dot(gy, wg_ref[...]) + bg_ref[...])

    col, st = _rows(tc, SSM_W), _rows(tc, N_STATE)
    vec = _full((1, SSM_W))
    return pl.pallas_call(
        body, name="ssm_fwd", grid=(T // tc,),
        in_specs=[col, _full(tab.shape), _full(bd_re.shape), _full(bd_im.shape), _full(cd_re.shape), _full(cd_im.shape),
                  vec, _full(w_glu.shape), vec],
        out_specs=[st, st, col, col],
        out_shape=[jax.ShapeDtypeStruct((T, N_STATE), F32)] * 2 + [jax.ShapeDtypeStruct((T, SSM_W), F32)] * 2,
        scratch_shapes=[pltpu.VMEM((SCAN_ROWS, N_STATE), F32)] * 2,
        compiler_params=_params("arbitrary"),
    )(u, tab, bd_re, bd_im, cd_re, cd_im, d_skip, w_glu, b_glu)


def _ssm_bwd(dout, u, yp, s_re, s_im, tab, bd_re, bd_im, cd_re, cd_im, d_skip, w_glu, b_glu, rider=None):
    T = u.shape[0]
    tc = SSM_CHUNK_BWD
    nt = T // tc
    rows_per_chunk = tc // SCAN_ROWS

    def body(do_ref, u_ref, yp_ref, sr_ref, si_ref, pr_ref, pi_ref, tab_ref, dsk_ref, wg_ref, bg_ref,
             bdr_hbm, bdi_hbm, cdr_hbm, cdi_hbm,
             du_ref, dsk_out, dbg_out, dwg_out, da_out, dbdr_out, dbdi_out, dcdr_out, dcdi_out,
             bdr_v, bdi_v, cdr_v, cdi_v, dbdr_v, dbdi_v, dcdr_v, dcdi_v, gr_v, gi_v, car_r, car_i, sems):
        i = pl.program_id(0)

        @pl.when(i == 0)
        def _():
            _load_once([(hbm.at[(ss, cs) if by_state else (cs, ss)], vmem.at[h])
                        for hbm, vmem, by_state in ((bdr_hbm, bdr_v, False), (bdi_hbm, bdi_v, False),
                                                    (cdr_hbm, cdr_v, True), (cdi_hbm, cdi_v, True))
                        for h, (cs, ss) in enumerate(SSM_HALVES)], sems)
            for ref in (dbdr_v, dbdi_v, dcdr_v, dcdi_v, car_r, car_i, dsk_out, dbg_out, dwg_out, da_out):
                ref[...] = jnp.zeros_like(ref)

        uv, y, dout_v = u_ref[...], yp_ref[...], do_ref[...]
        gy, t = _gelu(y)
        sg = jax.nn.sigmoid(_dot(gy, wg_ref[...]) + bg_ref[...])
        dzg = dout_v * gy * sg * (1.0 - sg)
        dgy = dout_v * sg + _dot_nt(dzg, wg_ref[...])
        dwg_out[...] += _dot_tn(gy, dzg)
        dbg_out[...] += jnp.sum(dzg, axis=0, keepdims=True)
        dy = dgy * (0.5 * (1.0 + t) + 0.5 * y * (1.0 - t * t) * GELU_C * (1.0 + 3 * 0.044715 * y * y))
        dsk_out[...] += jnp.sum(dy * uv, axis=0, keepdims=True)

        for h, (cs, ss) in enumerate(SSM_HALVES):
            gr_v[:, ss] = _dot_nt(dy[:, cs], cdr_v[h])
            gi_v[:, ss] = -_dot_nt(dy[:, cs], cdi_v[h])
            dcdr_v[h] += _dot_tn(sr_ref[:, ss], dy[:, cs])
            dcdi_v[h] -= _dot_tn(si_ref[:, ss], dy[:, cs])

        row = lax.broadcasted_iota(jnp.int32, (SCAN_ROWS, SCAN_LANES), 0)
        first_chunk = i == nt - 1
        for lb in range(N_STATE // SCAN_LANES):
            ls = pl.ds(lb * SCAN_LANES, SCAN_LANES)

            def step(n, carry):
                g = rows_per_chunk - 1 - n
                rows = pl.ds(pl.multiple_of(g * SCAN_ROWS, SCAN_ROWS), SCAN_ROWS)
                before = pl.ds(pl.multiple_of(jnp.maximum(g - 1, 0) * SCAN_ROWS, SCAN_ROWS), SCAN_ROWS)
                xr, xi = _scan_group(gr_v[rows, ls], gi_v[rows, ls], tab_ref, ls, carry[0], carry[1], True)
                gr_v[rows, ls] = xr
                gi_v[rows, ls] = xi
                last = slice(SCAN_ROWS - 1, SCAN_ROWS)
                edge_r = jnp.where(g > 0, sr_ref[before, ls][last], jnp.where(first_chunk, 0.0, pr_ref[:, ls][last]))
                edge_i = jnp.where(g > 0, si_ref[before, ls][last], jnp.where(first_chunk, 0.0, pi_ref[:, ls][last]))
                spr = jnp.where(row >= 1, pltpu.roll(sr_ref[rows, ls], 1, 0), edge_r)
                spi = jnp.where(row >= 1, pltpu.roll(si_ref[rows, ls], 1, 0), edge_i)
                first = slice(0, 1)
                return (jnp.broadcast_to(xr[first], xr.shape), jnp.broadcast_to(xi[first], xi.shape),
                        carry[2] + xr * spr + xi * spi, carry[3] + xi * spr - xr * spi)

            zero = jnp.zeros((SCAN_ROWS, SCAN_LANES), F32)
            cr, ci, dar, dai = lax.fori_loop(0, rows_per_chunk, step, (car_r[:, ls], car_i[:, ls], zero, zero))
            car_r[:, ls] = cr
            car_i[:, ls] = ci
            da_out[0, :, ls] += dar
            da_out[1, :, ls] += dai

        du_ref[...] = dsk_ref[...] * dy + jnp.concatenate(
            [_dot_nt(gr_v[:, ss], bdr_v[h]) + _dot_nt(gi_v[:, ss], bdi_v[h]) for h, (_, ss) in enumerate(SSM_HALVES)], axis=1)
        for h, (cs, ss) in enumerate(SSM_HALVES):
            dbdr_v[h] += _dot_tn(uv[:, cs], gr_v[:, ss])
            dbdi_v[h] += _dot_tn(uv[:, cs], gi_v[:, ss])

        @pl.when(i == nt - 1)
        def _():
            per_half = N_GROUPS // len(SSM_HALVES)
            for grp in range(N_GROUPS):
                h, at = divmod(grp, per_half)
                ch, stt = pl.ds(at * GROUP_CH, GROUP_CH), pl.ds(at * STATE, STATE)
                dbdr_out[grp], dbdi_out[grp] = dbdr_v[h, ch, stt], dbdi_v[h, ch, stt]
                dcdr_out[grp], dcdi_out[grp] = dcdr_v[h, stt, ch], dcdi_v[h, stt, ch]

    rev = lambda i: (nt - 1 - i, 0)
    col = pl.BlockSpec((tc, SSM_W), rev)
    st = pl.BlockSpec((tc, N_STATE), rev)
    st_before = pl.BlockSpec((SCAN_ROWS, N_STATE), lambda i: (jnp.maximum((nt - 1 - i) * rows_per_chunk - 1, 0), 0))
    vec = _full((1, SSM_W))
    bd_blocks, cd_blocks = (N_GROUPS, GROUP_CH, STATE), (N_GROUPS, STATE, GROUP_CH)
    bd, cd = jax.ShapeDtypeStruct(bd_blocks, F32), jax.ShapeDtypeStruct(cd_blocks, F32)
    n_half = len(SSM_HALVES)
    bd_half, cd_half = (n_half, SSM_W // n_half, N_STATE // n_half), (n_half, N_STATE // n_half, SSM_W // n_half)
    return _pallas(
        body, name="ssm_bwd", grid=(nt,),
        in_specs=[col, col, col, st, st, st_before, st_before, _full(tab.shape), vec, _full(w_glu.shape), vec,
                  ANY, ANY, ANY, ANY],
        out_specs=[col, vec, vec, _full(w_glu.shape), _full((2, SCAN_ROWS, N_STATE)),
                   _full(bd_blocks), _full(bd_blocks), _full(cd_blocks), _full(cd_blocks)],
        out_shape=[jax.ShapeDtypeStruct((T, SSM_W), F32), jax.ShapeDtypeStruct((1, SSM_W), F32),
                   jax.ShapeDtypeStruct((1, SSM_W), F32), jax.ShapeDtypeStruct(w_glu.shape, F32),
                   jax.ShapeDtypeStruct((2, SCAN_ROWS, N_STATE), F32), bd, bd, cd, cd],
        scratch_shapes=[pltpu.VMEM(bd_half, BF16)] * 2 + [pltpu.VMEM(cd_half, BF16)] * 2
        + [pltpu.VMEM(bd_half, F32)] * 2 + [pltpu.VMEM(cd_half, F32)] * 2
        + [pltpu.VMEM((tc, N_STATE), F32)] * 2 + [pltpu.VMEM((SCAN_ROWS, N_STATE), F32)] * 2
        + [pltpu.SemaphoreType.DMA((4 * len(SSM_HALVES),))],
        semantics=("arbitrary",), rider=rider,
        args=(dout, u, yp, s_re, s_im, s_re, s_im, tab, d_skip, w_glu, b_glu, bd_re, bd_im, cd_re, cd_im))


def _block_diagonals(parts):
    def body(*refs):
        for t_ref, o_ref in zip(refs[:len(parts)], refs[len(parts):]):
            _, a, b = t_ref.shape
            o_ref[...] = jnp.zeros_like(o_ref)
            for grp in range(N_GROUPS):
                o_ref[grp * a:(grp + 1) * a, grp * b:(grp + 1) * b] = t_ref[grp].astype(BF16)

    return pl.pallas_call(body, name="block_diagonals",
                          out_shape=[jax.ShapeDtypeStruct((N_GROUPS * t.shape[1], N_GROUPS * t.shape[2]), BF16) for t in parts])(*parts)


def _ssm_prepare(a_re, a_im, log_dt, b_re, b_im, c_re, c_im):
    col = lambda t: t.reshape(N_STATE, 1)
    ldt = jnp.broadcast_to(log_dt.reshape(N_GROUPS, 1), (N_GROUPS, STATE)).reshape(N_STATE, 1)
    b2r, b2i = b_re.reshape(N_STATE, GROUP_CH), b_im.reshape(N_STATE, GROUP_CH)
    ab_r, ab_i, e_r, e_i, bb_r, bb_i, p_r, p_i = _ssm_discretize(col(a_re), col(a_im), ldt, b2r, b2i)
    diag = _block_diagonals([jnp.swapaxes(t.reshape(N_GROUPS, STATE, GROUP_CH), 1, 2) for t in (bb_r, bb_i)]
                            + [jnp.swapaxes(t, 1, 2) for t in (c_re, c_im)])
    bd, cd = diag[:2], diag[2:]
    saved = dict(a_re=col(a_re), a_im=col(a_im), log_dt=ldt, b_re=b2r, b_im=b2i, ab_re=ab_r, ab_im=ab_i, e_re=e_r, e_im=e_i)
    return _scan_tables(p_r, p_i, False), _scan_tables(p_r, p_i, True), bd, cd, saved


BIG = ("ffn1_w_in", "ffn1_w_out", "w_mix_in", "w_glu", "w_mix_out", "ffn2_w_in", "ffn2_w_out")
SMALL = ("ffn1_pre_g", "ffn1_post_g", "mix_pre_g", "a_re", "a_im", "log_dt", "b_re", "b_im", "c_re", "c_im",
         "d_skip", "b_glu", "mix_post_g", "ffn2_pre_g", "ffn2_post_g")


FIRST = ("ffn1_w_in", "ffn1_w_out")
REST = ("w_mix_in", "w_glu", "w_mix_out", "ffn2_w_in", "ffn2_w_out")
LATE = ("w_mix_in", "w_glu", "w_mix_out", "ffn1_w_out")
SHARD_SHAPE = {"ffn1_w_in": (D_MODEL, FF_BLK), "ffn2_w_in": (D_MODEL, FF_BLK), "ffn1_w_out": (D_FF // N_CHIPS, D_MODEL),
               "ffn2_w_out": (D_FF // N_CHIPS, D_MODEL), "w_mix_in": (D_MODEL, ATTN_W), "w_glu": (SSM_W // N_CHIPS, SSM_W),
               "w_mix_out": (2 * ATTN_W // N_CHIPS, D_MODEL)}


class _Reduction:
    def __init__(self, names, grads, place):
        self.names, self.local, self.place = list(names), list(grads), place

    def exchange(self):
        return _pair_exchange(self.local)

    def first(self, got):
        both = [_pair_sum(a, b, self.place["core"], name=f"pair_sum_{n}") for n, a, b in zip(self.names, self.local, got)]
        self.pair = [f32 for f32, _ in both]
        return _reduce_first([wire for _, wire in both])

    def second(self, got):
        both = [_reduce_sum_first(a, b, self.place["sel_first"], name=f"sum_first_{n}")
                for n, a, b in zip(self.names, self.pair, got)]
        self.sums = [f32 for f32, _ in both]
        return _reduce_second([wire for _, wire in both])

    def swap(self, got):
        self.halves = [_reduce_sum_second(a, b, self.place["sel_second"], name=f"sum_second_{n}").reshape(2 * a.shape[2], a.shape[3])
                       for n, a, b in zip(self.names, self.sums, got)]
        return _pair_swap(self.halves)

    def done(self, got):
        return {n: (mine, theirs) for n, mine, theirs in zip(self.names, self.halves, got)}


def _local_step(x, target, p, w, place=None):
    vec = lambda t: t.reshape(1, -1)
    w1_in, w1_out = w["ffn1_w_in"], w["ffn1_w_out"].reshape(2, FF_BLK, D_MODEL)
    blocks = lambda n, t: t.reshape((N_CHIPS,) + SHARD_SHAPE[n])

    ffn1 = functools.partial(_ffn_fwd, x, vec(p["ffn1_pre_g"]), w1_in, w1_out, vec(p["ffn1_post_g"]), None, name="ffn1_fwd")
    if place is None:
        x1, z1, f1 = ffn1()
    else:
        (x1, z1, f1), rest = ffn1(rider=_gather_weights([w[n] for n in REST]))
        w = dict(w, **dict(zip(REST, rest)))
    w2_in, w2_out = w["ffn2_w_in"], w["ffn2_w_out"].reshape(2, FF_BLK, D_MODEL)
    w_mi, w_glu, w_mo = w["w_mix_in"], w["w_glu"].reshape(SSM_W, SSM_W), w["w_mix_out"].reshape(2, ATTN_W, D_MODEL)
    q, k, v, u = _mix_in_fwd(x1, vec(p["mix_pre_g"]), w_mi)
    attn, lse = _attention_fwd(q, k, v)
    tab_f, tab_b, bd, cd, sv = _ssm_prepare(p["a_re"], p["a_im"], p["log_dt"], p["b_re"], p["b_im"], p["c_re"], p["c_im"])
    ssm_args = (bd[0], bd[1], cd[0], cd[1], vec(p["d_skip"]), w_glu, vec(p["b_glu"]))
    s_re, s_im, yp, ssm = _ssm_fwd(u, tab_f, *ssm_args)
    x2, mixed = _mix_out_fwd(x1, attn, ssm, w_mo, vec(p["mix_post_g"]))
    dx3, loss_rows, z2, f2 = _ffn_fwd(x2, vec(p["ffn2_pre_g"]), w2_in, w2_out, vec(p["ffn2_post_g"]), target, name="ffn2_fwd")

    g = {}
    ride = (lambda call, exchange: call(rider=exchange)) if place else (lambda call, exchange: (call(), None))
    dz2, g["ffn2_post_g"], dw2_out = _ffn_bwd_out(dx3, f2, z2, w2_out, vec(p["ffn2_post_g"]), name="ffn2_bwd_out")
    dw2_in = _norm_matmul_dw(dz2, x2, vec(p["ffn2_pre_g"]), N_CHIPS, name="ffn2_bwd_dw", together=2)
    early = _Reduction(("ffn2_w_in", "ffn2_w_out"), [dw2_in, blocks("ffn2_w_out", dw2_out)], place) if place else None
    (dx2, g["ffn2_pre_g"]), got = ride(
        functools.partial(_norm_matmul_dx, dz2, x2, dx3, vec(p["ffn2_pre_g"]), w2_in, name="ffn2_bwd_dx"), early and early.exchange())
    dattn, dssm, dw_mo, g["mix_post_g"] = _mix_out_bwd(dx2, mixed, attn, ssm, w_mo, vec(p["mix_post_g"]))
    (du, g["d_skip"], g["b_glu"], dw_glu, da, dbd_re, dbd_im, dcd_re, dcd_im), got = ride(
        functools.partial(_ssm_bwd, dssm, u, yp, s_re, s_im, tab_b, *ssm_args), early and early.first(got))
    branch = lambda d: functools.partial(_attn_branch_bwd, q, k, v, attn, lse, dattn, d)
    parts = [None] * 3
    parts[0], got = ride(branch(DILATIONS[0]), early and early.second(got))
    parts[1], early_theirs = ride(branch(DILATIONS[1]), early and early.swap(got))
    parts[2] = branch(DILATIONS[2])()
    dproj = _dproj_merge([r[0] for r in parts], [r[1] for r in parts], [r[2] for r in parts], du)
    dw_mi = _norm_matmul_dw(dproj, x1, vec(p["mix_pre_g"]), N_CHIPS, name="mix_bwd_dw", together=N_CHIPS)
    dx1, g["mix_pre_g"] = _norm_matmul_dx(dproj, x1, dx2, vec(p["mix_pre_g"]), w_mi, name="mix_bwd_dx")
    dz1, g["ffn1_post_g"], dw1_out = _ffn_bwd_out(dx1, f1, z1, w1_out, vec(p["ffn1_post_g"]), name="ffn1_bwd_out")
    big = {"ffn2_w_in": dw2_in, "ffn2_w_out": dw2_out, "w_mix_in": dw_mi, "w_glu": dw_glu, "w_mix_out": dw_mo, "ffn1_w_out": dw1_out}
    late = _Reduction(LATE, [blocks(n, big[n]) for n in LATE], place) if place else None
    big["ffn1_w_in"], got = ride(
        functools.partial(_norm_matmul_dw, dz1, x, vec(p["ffn1_pre_g"]), N_CHIPS, name="ffn1_bwd_dw", together=2),
        late and late.exchange())
    dx_call = functools.partial(_norm_matmul_dx, dz1, x, dx1, vec(p["ffn1_pre_g"]), w1_in)
    if place:
        last = _Reduction(("ffn1_w_in",), [big["ffn1_w_in"]], place)
        n_late = len(LATE)
        (dx_part, dg_a), got = dx_call(name="ffn1_bwd_dx_a", part=(0, 0.375), rider=_merged(late.first(got), last.exchange()))
        (dx_part, dg_b), got = dx_call(name="ffn1_bwd_dx_b", part=(0.375, 0.75), into=dx_part,
                                       rider=_merged(late.second(got[:n_late]), last.first(got[n_late:])))
        grad_x, dg_c = dx_call(name="ffn1_bwd_dx_c", part=(0.75, 1), into=dx_part)
        g["ffn1_pre_g"] = dg_a + dg_b + dg_c
        got = _merged(late.swap(got[:n_late]), last.second(got[n_late:])).run("tail_second")
        g.update(late.done(got[:n_late]))
        g.update(last.done(last.swap(got[n_late:]).run("tail_swap")))
        g.update(early.done(early_theirs))
    else:
        grad_x, g["ffn1_pre_g"] = dx_call(name="ffn1_bwd_dx")
        g.update({n: blocks(n, big[n]) for n in BIG})

    g["c_re"], g["c_im"] = (jnp.swapaxes(m, 1, 2) for m in (dcd_re, dcd_im))
    dbb = [jnp.swapaxes(m, 1, 2).reshape(N_STATE, GROUP_CH) for m in (dbd_re, dbd_im)]
    dab = [jnp.sum(da[n], axis=0).reshape(N_STATE, 1) for n in range(2)]
    da_re, da_im, dldt, db_re, db_im = _ssm_discretize_bwd(
        sv["a_re"], sv["a_im"], sv["log_dt"], sv["b_re"], sv["b_im"], sv["ab_re"], sv["ab_im"], sv["e_re"], sv["e_im"],
        dab[0], dab[1], dbb[0], dbb[1])
    g["a_re"], g["a_im"] = da_re.reshape(N_GROUPS, STATE), da_im.reshape(N_GROUPS, STATE)
    g["log_dt"] = jnp.sum(dldt.reshape(N_GROUPS, STATE), axis=-1)
    g["b_re"], g["b_im"] = (t.reshape(N_GROUPS, STATE, GROUP_CH) for t in (db_re, db_im))
    return loss_rows, grad_x, g


MESH = pl.DeviceIdType.MESH
N_REL = 3


def _place():
    x, y, c = lax.axis_index("x"), lax.axis_index("y"), lax.axis_index("c")
    return x, y, c, [(1 - x, y), (x, 1 - y), (1 - x, 1 - y)]


def _remote(src, dst, send_sems, recv_sems, idx, to):
    return pltpu.make_async_remote_copy(src_ref=src, dst_ref=dst, send_sem=send_sems.at[idx], recv_sem=recv_sems.at[idx],
                                        device_id=to, device_id_type=MESH)


def _half(rows, who):
    return pl.ds(who * (rows // 2), rows // 2)


class _Copies:
    def __init__(self, send_sems, recv_sems, local_sems):
        self.send_sems, self.recv_sems, self.local_sems = send_sems, recv_sems, local_sems
        self.n_remote = self.n_local = 0

    def remote(self, src, dst, to):
        k, self.n_remote = self.n_remote, self.n_remote + 1
        return pltpu.make_async_remote_copy(src_ref=src, dst_ref=dst, send_sem=self.send_sems.at[k],
                                            recv_sem=self.recv_sems.at[k], device_id=to, device_id_type=MESH)

    def local(self, src, dst):
        k, self.n_local = self.n_local, self.n_local + 1
        return pltpu.make_async_copy(src, dst, self.local_sems.at[k])


class _Exchange:
    def __init__(self, plan, ins, out_shapes, n_remote):
        self.plan, self.ins, self.out_shapes, self.n_remote = plan, list(ins), list(out_shapes), n_remote
        self.sems = [pltpu.SemaphoreType.DMA((n_remote,)), pltpu.SemaphoreType.DMA((n_remote,)), pltpu.SemaphoreType.DMA((1,))]

    def run(self, name):
        n_in = len(self.ins)

        def body(*refs):
            for phase in self.plan(refs[:n_in], refs[n_in:-3], _Copies(*refs[-3:])):
                for cp in phase:
                    cp.start()
                for cp in phase:
                    cp.wait()

        return pl.pallas_call(body, name=name, in_specs=[ANY] * n_in, out_specs=[ANY] * len(self.out_shapes),
                              out_shape=self.out_shapes, scratch_shapes=self.sems)(*self.ins)


def _merged(a, b):
    n_in, n_out = len(a.ins), len(a.out_shapes)

    def plan(ins, outs, mk):
        (phase_a,), (phase_b,) = a.plan(ins[:n_in], outs[:n_out], mk), b.plan(ins[n_in:], outs[n_out:], mk)
        return [phase_a + phase_b]

    return _Exchange(plan, a.ins + b.ins, a.out_shapes + b.out_shapes, a.n_remote + b.n_remote)


def _pallas(body, *, name, grid, in_specs, out_specs, out_shape, args, semantics, scratch_shapes=(), rider=None, aliases=None):
    aliases = aliases or {}
    if rider is None:
        return pl.pallas_call(body, name=name, grid=grid, in_specs=in_specs, out_specs=out_specs, out_shape=out_shape,
                              scratch_shapes=list(scratch_shapes), input_output_aliases=aliases,
                              compiler_params=_params(*semantics))(*args)
    n_in, n_out, r_in, r_out = len(in_specs), len(out_specs), len(rider.ins), len(rider.out_shapes)
    n_steps = math.prod(grid)

    def carrier(*refs):
        ins, rider_ins = refs[:n_in], refs[n_in:n_in + r_in]
        outs = refs[n_in + r_in:n_in + r_in + n_out]
        rider_outs = refs[n_in + r_in + n_out:n_in + r_in + n_out + r_out]
        scratch, sems = refs[n_in + r_in + n_out + r_out:-3], refs[-3:]
        step = 0
        for axis, size in enumerate(grid):
            step = step * size + pl.program_id(axis)
        phases = rider.plan(rider_ins, rider_outs, _Copies(*sems))

        def start_phase(p):
            for cp in (phases[p - 1] if p else []):
                cp.wait()
            for cp in phases[p]:
                cp.start()

        for p in range(len(phases)):
            pl.when(step == p * n_steps // len(phases))(functools.partial(start_phase, p))
        body(*ins, *outs, *scratch)

        @pl.when(step == n_steps - 1)
        def _():
            for cp in phases[-1]:
                cp.wait()

    results = pl.pallas_call(
        carrier, name=name, grid=grid, in_specs=list(in_specs) + [ANY] * r_in, out_specs=list(out_specs) + [ANY] * r_out,
        out_shape=list(out_shape) + rider.out_shapes, scratch_shapes=list(scratch_shapes) + rider.sems,
        input_output_aliases=aliases, compiler_params=_params(*["arbitrary"] * len(grid)))(*args, *rider.ins)
    return results[:n_out], results[n_out:]


def _gather_weights(shards):
    def plan(ins, outs, mk):
        x, y, c = lax.axis_index("x"), lax.axis_index("y"), lax.axis_index("c")
        me, sibling = 2 * x + y, (x, y, 1 - c)
        x_nb, y_nb, diag = (1 - x, y), (x, 1 - y), (1 - x, 1 - y)
        index = lambda chip: 2 * chip[0] + chip[1]
        first, second, third = [], [], []
        for i, shard in enumerate(shards):
            rows = shard.shape[0]
            mine = _half(rows, c)
            quarter = lambda which: pl.ds(c * (rows // 2) + which * (rows // 4), rows // 4)
            first.append(mk.remote(ins[i], outs[i].at[me], sibling))
            for nb in (x_nb, y_nb):
                first.append(mk.remote(ins[i].at[mine], outs[i].at[me, mine], (*nb, c)))
            for nb in (x_nb, y_nb):
                landed = outs[i].at[index(nb), mine]
                second.append(mk.remote(landed, landed, sibling))
            for nb, other, which in ((x_nb, y_nb, 0), (y_nb, x_nb, 1)):
                landed = outs[i].at[index(nb), quarter(which)]
                second.append(mk.remote(landed, landed, (*other, c)))
            landed = outs[i].at[index(diag), mine]
            third.append(mk.remote(landed, landed, sibling))
        return [first, second, third]

    return _Exchange(plan, shards, [jax.ShapeDtypeStruct((N_CHIPS,) + s.shape, s.dtype) for s in shards], 8 * len(shards))


def _pair_exchange(grads):
    def plan(ins, outs, mk):
        x, y, c = lax.axis_index("x"), lax.axis_index("y"), lax.axis_index("c")
        return [[mk.remote(ins[i].at[:, _half(g.shape[1], 1 - c)], outs[i], (x, y, 1 - c)) for i, g in enumerate(grads)]]

    return _Exchange(plan, grads, [jax.ShapeDtypeStruct((N_CHIPS, g.shape[1] // 2, g.shape[2]), g.dtype) for g in grads], len(grads))


def _reduce_first(pair):
    def plan(ins, outs, mk):
        x, y, c = lax.axis_index("x"), lax.axis_index("y"), lax.axis_index("c")
        phase = []
        for i, p in enumerate(pair):
            q = p.shape[1] // 2
            phase.append(mk.remote(ins[i].at[pl.ds(2 * (1 - x), 2), pl.ds(0, q)], outs[i].at[0], (1 - x, y, c)))
            for jx in range(2):
                phase.append(mk.remote(ins[i].at[2 * jx + 1 - y, pl.ds(q, q)], outs[i].at[1, jx], (x, 1 - y, c)))
        return [phase]

    return _Exchange(plan, pair, [jax.ShapeDtypeStruct((2, 2, p.shape[1] // 2, p.shape[2]), p.dtype) for p in pair], 3 * len(pair))


def _reduce_second(sums):
    def plan(ins, outs, mk):
        x, y, c = lax.axis_index("x"), lax.axis_index("y"), lax.axis_index("c")
        phase = []
        for i in range(len(sums)):
            phase.append(mk.remote(ins[i].at[0, 1 - y], outs[i].at[0], (x, 1 - y, c)))
            phase.append(mk.remote(ins[i].at[1, 1 - x], outs[i].at[1], (1 - x, y, c)))
        return [phase]

    return _Exchange(plan, sums, [jax.ShapeDtypeStruct((2,) + s.shape[2:], s.dtype) for s in sums], 2 * len(sums))


def _pair_swap(halves):
    def plan(ins, outs, mk):
        x, y, c = lax.axis_index("x"), lax.axis_index("y"), lax.axis_index("c")
        return [[mk.remote(ins[i], outs[i], (x, y, 1 - c)) for i in range(len(halves))]]

    return _Exchange(plan, halves, [jax.ShapeDtypeStruct(h.shape, h.dtype) for h in halves], len(halves))


def _allreduce_small(packed):
    rows = packed.shape[0]
    n_dev = 2 * N_CHIPS

    def body(x_ref, o_ref, buf, send_sems, recv_sems):
        x, y, c, chips = _place()
        sibling = (x, y, 1 - c)

        def slot(px, py, pc):
            return buf.at[4 * px + 2 * py + pc]

        buf[4 * x + 2 * y + c] = x_ref[...]
        first = [_remote(x_ref, slot(x, y, c), send_sems, recv_sems, 0, sibling)]
        first += [_remote(x_ref, slot(x, y, c), send_sems, recv_sems, 1 + k, (*chip, c)) for k, chip in enumerate(chips)]
        for cp in first:
            cp.start()
        passed = []
        for k, chip in enumerate(chips):
            landed = slot(*chip, c)
            _remote(landed, landed, send_sems, recv_sems, 1 + k, (*chip, c)).wait_recv()
            passed.append(_remote(landed, landed, send_sems, recv_sems, 1 + N_REL + k, sibling))
            passed[-1].start()
        _remote(slot(*sibling), slot(*sibling), send_sems, recv_sems, 0, sibling).wait_recv()
        for k, chip in enumerate(chips):
            landed = slot(*chip, 1 - c)
            _remote(landed, landed, send_sems, recv_sems, 1 + N_REL + k, sibling).wait_recv()
        for cp in first + passed:
            cp.wait_send()
        total = buf[0]
        for d in range(1, n_dev):
            total = total + buf[d]
        o_ref[...] = total

    vm = pl.BlockSpec(memory_space=pltpu.VMEM)
    return pl.pallas_call(
        body, name="allreduce_small", in_specs=[vm], out_specs=vm, out_shape=jax.ShapeDtypeStruct(packed.shape, F32),
        scratch_shapes=[pltpu.VMEM((n_dev, rows, 128), F32), pltpu.SemaphoreType.DMA((1 + 2 * N_REL,)),
                        pltpu.SemaphoreType.DMA((1 + 2 * N_REL,))],
    )(packed)


def _row_tile(rows, cap=256):
    return max(t for t in range(8, cap + 1, 8) if rows % t == 0)


def _pair_sum(grad, got, c, name):
    _, half, cols = got.shape
    tr = _row_tile(half)
    nt = half // tr

    def body(c_ref, g_ref, r_ref, o_ref, wire_ref):
        total = g_ref[...] + r_ref[...]
        o_ref[...] = total
        wire_ref[...] = total.astype(BF16)

    blk = (1, tr, cols)
    out = pl.BlockSpec(blk, lambda j, t, c_ref: (j, t, 0))
    return pl.pallas_call(
        body, name=name,
        grid_spec=pltpu.PrefetchScalarGridSpec(
            num_scalar_prefetch=1, grid=(N_CHIPS, nt),
            in_specs=[pl.BlockSpec(blk, lambda j, t, c_ref: (j, c_ref[0] * nt + t, 0)), out], out_specs=[out, out]),
        out_shape=[jax.ShapeDtypeStruct(got.shape, F32), jax.ShapeDtypeStruct(got.shape, BF16)],
        compiler_params=_params("parallel", "parallel"),
    )(c, grad, got)


def _reduce_sum_first(pair, got, sel, name):
    _, _, q, cols = got.shape
    tr = _row_tile(q)
    nt = q // tr

    def body(sel_ref, p_ref, r_ref, o_ref, wire_ref):
        total = p_ref[0] + r_ref[0, 0].astype(F32)
        o_ref[0, 0] = total
        wire_ref[0, 0] = total.astype(BF16)

    blk = pl.BlockSpec((1, 1, tr, cols), lambda p, k, t, s: (p, k, t, 0))
    return pl.pallas_call(
        body, name=name,
        grid_spec=pltpu.PrefetchScalarGridSpec(
            num_scalar_prefetch=1, grid=(2, 2, nt),
            in_specs=[pl.BlockSpec((1, tr, cols), lambda p, k, t, s: (s[2 * p] + s[2 * p + 1] * k, p * nt + t, 0)), blk],
            out_specs=[blk, blk]),
        out_shape=[jax.ShapeDtypeStruct(got.shape, F32), jax.ShapeDtypeStruct(got.shape, BF16)],
        compiler_params=_params("parallel", "parallel", "parallel"),
    )(sel, pair, got)


def _reduce_sum_second(sums, got, sel, name):
    _, q, cols = got.shape
    tr = _row_tile(q)

    def body(sel_ref, s_ref, r_ref, o_ref):
        o_ref[0] = s_ref[0, 0] + r_ref[0].astype(F32)

    blk = (1, tr, cols)
    return pl.pallas_call(
        body, name=name,
        grid_spec=pltpu.PrefetchScalarGridSpec(
            num_scalar_prefetch=1, grid=(2, q // tr),
            in_specs=[pl.BlockSpec((1, 1, tr, cols), lambda p, t, s: (p, s[p], t, 0)),
                      pl.BlockSpec(blk, lambda p, t, s: (p, t, 0))],
            out_specs=pl.BlockSpec(blk, lambda p, t, s: (p, t, 0))),
        out_shape=jax.ShapeDtypeStruct(got.shape, F32), compiler_params=_params("parallel", "parallel"),
    )(sel, sums, got)


def _adamw_update(w, g, m, v):
    m2 = ADAM_B1 * m + (1.0 - ADAM_B1) * g
    v2 = ADAM_B2 * v + (1.0 - ADAM_B2) * (g * g)
    m_hat = m2 / (1.0 - ADAM_B1 ** ADAM_STEP)
    v_hat = v2 / (1.0 - ADAM_B2 ** ADAM_STEP)
    return -ADAM_LR * (m_hat / (jnp.sqrt(v_hat) + ADAM_EPS) + ADAM_WD * w), m2, v2


def _adamw_small(ws, gs, ms, vs):
    n = len(ws)

    def body(*refs):
        w, g, m, v, d, mo, vo = (refs[k * n:(k + 1) * n] for k in range(7))
        for i in range(n):
            d[i][...], mo[i][...], vo[i][...] = _adamw_update(w[i][...], g[i][...], m[i][...], v[i][...])

    shapes = [jax.ShapeDtypeStruct(t.shape, F32) for t in ws]
    outs = pl.pallas_call(body, name="adamw_small", out_shape=shapes * 3,
                          compiler_params=pltpu.CompilerParams(vmem_limit_bytes=VMEM_LIMIT_V7X))(*ws, *gs, *ms, *vs)
    return outs[:n], outs[n:2 * n], outs[2 * n:]


def _adamw_halves(w, mine, theirs, m, v, core, name):
    rows, cols = w.shape
    tr = _row_tile(rows // 2)
    per_half = rows // 2 // tr

    def body(core_ref, w_ref, a_ref, b_ref, m_ref, v_ref, g_ref, d_ref, mo_ref, vo_ref):
        g = jnp.where(pl.program_id(0) // per_half == core_ref[0], a_ref[...], b_ref[...])
        g_ref[...] = g
        d_ref[...], mo_ref[...], vo_ref[...] = _adamw_update(w_ref[...], g, m_ref[...], v_ref[...])

    blk = pl.BlockSpec((tr, cols), lambda t, c: (t, 0))
    half = lambda own: pl.BlockSpec(
        (tr, cols), lambda t, c: (jnp.clip(t - (c[0] if own else 1 - c[0]) * per_half, 0, per_half - 1), 0))
    return pl.pallas_call(
        body, name=name,
        grid_spec=pltpu.PrefetchScalarGridSpec(
            num_scalar_prefetch=1, grid=(2 * per_half,), in_specs=[blk, half(True), half(False), blk, blk], out_specs=[blk] * 4),
        out_shape=[jax.ShapeDtypeStruct(w.shape, F32)] * 4, compiler_params=_params("arbitrary"),
    )(core, w, mine, theirs, m, v)


def _pack(parts):
    flat = []
    for t in parts:
        t = t.reshape(-1).astype(F32)
        flat.append(jnp.pad(t, (0, -t.shape[0] % 128)))
    flat = jnp.concatenate(flat)
    return jnp.pad(flat, (0, -flat.shape[0] % 1024)).reshape(-1, 128)


def _unpack(buf, shapes):
    flat, out, at = buf.reshape(-1), [], 0
    for s in shapes:
        size = math.prod(s)
        out.append(flat[at:at + size].reshape(s))
        at += size + (-size % 128)
    return out


def kernel(x, ffn1_pre_g, ffn1_w_in, ffn1_w_out, ffn1_post_g, mix_pre_g, w_mix_in, a_re, a_im, log_dt, b_re, b_im, c_re, c_im, d_skip, w_glu, b_glu, w_mix_out, mix_post_g, ffn2_pre_g, ffn2_w_in, ffn2_w_out, ffn2_post_g, loss_target, m_ffn1_pre_g, m_ffn1_w_in, m_ffn1_w_out, m_ffn1_post_g, m_mix_pre_g, m_w_mix_in, m_a_re, m_a_im, m_log_dt, m_b_re, m_b_im, m_c_re, m_c_im, m_d_skip, m_w_glu, m_b_glu, m_w_mix_out, m_mix_post_g, m_ffn2_pre_g, m_ffn2_w_in, m_ffn2_w_out, m_ffn2_post_g, v_ffn1_pre_g, v_ffn1_w_in, v_ffn1_w_out, v_ffn1_post_g, v_mix_pre_g, v_w_mix_in, v_a_re, v_a_im, v_log_dt, v_b_re, v_b_im, v_c_re, v_c_im, v_d_skip, v_w_glu, v_b_glu, v_w_mix_out, v_mix_post_g, v_ffn2_pre_g, v_ffn2_w_in, v_ffn2_w_out, v_ffn2_post_g):
    given = dict(locals())
    order = ("ffn1_pre_g", "ffn1_w_in", "ffn1_w_out", "ffn1_post_g", "mix_pre_g", "w_mix_in", "a_re", "a_im", "log_dt",
             "b_re", "b_im", "c_re", "c_im", "d_skip", "w_glu", "b_glu", "w_mix_out", "mix_post_g", "ffn2_pre_g",
             "ffn2_w_in", "ffn2_w_out", "ffn2_post_g")
    at_x, at_y, at_c = (lax.axis_index(a).astype(jnp.int32) for a in ("x", "y", "c"))
    place = dict(core=at_c.reshape(1), sel_first=jnp.stack([2 * at_x, jnp.int32(1), at_y, jnp.int32(2)]),
                 sel_second=jnp.stack([at_y, at_x]))

    shards = {n: given[n][0] for n in BIG}
    w = {n: shards[n].astype(BF16) for n in REST}
    w.update(zip(FIRST, _gather_weights([shards[n].astype(BF16) for n in FIRST]).run("gather_first")))
    small = {n: given[n][0] for n in SMALL}
    loss_rows, grad_x, g = _local_step(x[0], loss_target[0], small, w, place)

    total = _allreduce_small(_pack([g[n] for n in SMALL] + [loss_rows[0, :1]]))
    parts = _unpack(total, [small[n].shape for n in SMALL] + [(1,)])
    grads = dict(zip(SMALL, parts[:-1]))
    loss = parts[-1][0]

    delta, new_m, new_v = {}, {}, {}
    for n in BIG:
        grads[n], delta[n], new_m[n], new_v[n] = _adamw_halves(
            shards[n], *g[n], given["m_" + n][0], given["v_" + n][0], place["core"], name=f"adamw_{n}")
    take = lambda pre: [given[pre + n] for n in SMALL]
    outs = _adamw_small(take(""), [grads[n][None] for n in SMALL], take("m_"), take("v_"))
    for store, arrays in zip((delta, new_m, new_v), outs):
        store.update({n: t[0] for n, t in zip(SMALL, arrays)})

    lead = lambda d: [d[n][None] for n in order]
    return (loss, grad_x[None], *lead(grads), *lead(delta), *lead(new_m), *lead(new_v))
```

```python
import functools
import math

import jax
import jax.numpy as jnp
from jax import lax
from jax.experimental import pallas as pl
from jax.experimental.pallas import tpu as pltpu

F32, BF16 = jnp.float32, jnp.bfloat16

D_MODEL = 1024
D_FF = 2816
N_CHIPS = 4
FF_BLK = 2 * D_FF // N_CHIPS
ATTN_W = 512
SSM_W = 512
HEAD_DIM = 64
N_HEADS = ATTN_W // HEAD_DIM
DILATIONS = (1, 4, 16)
N_BACK = 128
QBLK = 128
N_GROUPS = 32
GROUP_CH = 16
STATE = 64
N_STATE = N_GROUPS * STATE
EPS = 1e-6
NEG = -1e30
GELU_C = math.sqrt(2.0 / math.pi)

ADAM_LR, ADAM_B1, ADAM_B2, ADAM_EPS, ADAM_WD, ADAM_STEP = 0.001, 0.9, 0.999, 1e-08, 0.01, 10

VMEM_LIMIT_V7X = 60 * 1024 * 1024
ROW_TILE = 512
FFN_ROW_TILE = 512
DW_ROW_TILE = 1024


def _params(*sem):
    return pltpu.CompilerParams(dimension_semantics=sem, vmem_limit_bytes=VMEM_LIMIT_V7X)


def _dot(a, b):
    return jnp.dot(a.astype(BF16), b.astype(BF16), preferred_element_type=F32)


def _dot_nt(a, b):
    return lax.dot_general(a.astype(BF16), b.astype(BF16), (((1,), (1,)), ((), ())), preferred_element_type=F32)


def _dot_tn(a, b):
    return lax.dot_general(a.astype(BF16), b.astype(BF16), (((0,), (0,)), ((), ())), preferred_element_type=F32)


def _full(shape):
    return pl.BlockSpec(shape, lambda *_: (0,) * len(shape))


def _rows(tm, width):
    return pl.BlockSpec((tm, width), lambda i: (i, 0))


ANY = pl.BlockSpec(memory_space=pl.ANY)


def _load_once(pairs, sems):
    copies = [pltpu.make_async_copy(src, dst, sems.at[k]) for k, (src, dst) in enumerate(pairs)]
    for c in copies:
        c.start()
    for c in copies:
        c.wait()


def _rms(x):
    return lax.rsqrt(jnp.mean(x * x, axis=-1, keepdims=True) + EPS)


def _rms_bwd(dy_g, xn, r):
    return r * (dy_g - xn * jnp.mean(dy_g * xn, axis=-1, keepdims=True))


def _ffn_fwd(x, g_pre, w_in, w_out, g_post, target, *, name, rider=None):
    T = x.shape[0]
    tm = FFN_ROW_TILE
    with_loss = target is not None

    def body(*refs):
        if with_loss:
            x_ref, gpre_ref, gpost_ref, tgt_ref, win_hbm, wout_hbm, o_ref, loss_ref, z_ref, f_ref, win_v, wout_v, sems = refs
        else:
            x_ref, gpre_ref, gpost_ref, win_hbm, wout_hbm, o_ref, z_ref, f_ref, win_v, wout_v, sems = refs

        @pl.when(pl.program_id(0) == 0)
        def _():
            _load_once([(win_hbm, win_v), (wout_hbm, wout_v)], sems)
            if with_loss:
                loss_ref[...] = jnp.zeros_like(loss_ref)

        xv = x_ref[...]
        h = (xv * _rms(xv) * gpre_ref[...]).astype(BF16)
        f = jnp.zeros((tm, D_MODEL), F32)
        for k in range(2):
            gate = _dot(h, win_v[k])
            up = _dot(h, win_v[k + 2])
            z_ref[:, k * FF_BLK:(k + 1) * FF_BLK] = gate.astype(BF16)
            z_ref[:, D_FF + k * FF_BLK:D_FF + (k + 1) * FF_BLK] = up.astype(BF16)
            f = f + _dot(gate * jax.nn.sigmoid(gate) * up, wout_v[k])
        f_ref[...] = f
        out = xv + 0.5 * (f * _rms(f) * gpost_ref[...])
        if with_loss:
            err = out - tgt_ref[...]
            o_ref[...] = err * (1.0 / D_MODEL)
            loss_ref[...] += jnp.sum(err * err) * (0.5 / D_MODEL)
        else:
            o_ref[...] = out

    row = _rows(tm, D_MODEL)
    vec = _full((1, D_MODEL))
    in_specs = [row, vec, vec] + ([row] if with_loss else []) + [ANY, ANY]
    out_shape = [jax.ShapeDtypeStruct((T, D_MODEL), F32)]
    out_specs = [row]
    if with_loss:
        out_shape.append(jax.ShapeDtypeStruct((8, 128), F32))
        out_specs.append(_full((8, 128)))
    out_shape += [jax.ShapeDtypeStruct((T, 2 * D_FF), BF16), jax.ShapeDtypeStruct((T, D_MODEL), F32)]
    out_specs += [_rows(tm, 2 * D_FF), row]
    args = (x, g_pre, g_post) + ((target,) if with_loss else ()) + (w_in, w_out)
    return _pallas(
        body, name=name, grid=(T // tm,), in_specs=in_specs, out_specs=out_specs, out_shape=out_shape,
        scratch_shapes=[pltpu.VMEM(w_in.shape, BF16), pltpu.VMEM(w_out.shape, BF16), pltpu.SemaphoreType.DMA((2,))],
        semantics=("arbitrary",), args=args, rider=rider)


def _ffn_bwd_out(dout, f, z, w_out, g_post, *, name):
    T = dout.shape[0]
    tm = FFN_ROW_TILE
    nt = T // tm

    def body(dout_ref, f_ref, z_ref, gpost_ref, wout_hbm, dz_ref, dgpost_ref, dwout_hbm, wout_v, dwout_v, sems):
        i = pl.program_id(0)

        @pl.when(i == 0)
        def _():
            _load_once([(wout_hbm, wout_v)], sems)
            dwout_v[...] = jnp.zeros_like(dwout_v)
            dgpost_ref[...] = jnp.zeros_like(dgpost_ref)

        dy = 0.5 * dout_ref[...]
        f = f_ref[...]
        r = _rms(f)
        fn = f * r
        dgpost_ref[...] += jnp.sum(dy * fn, axis=0, keepdims=True)
        df = _rms_bwd(dy * gpost_ref[...], fn, r).astype(BF16)
        for k in range(2):
            gate = z_ref[:, k * FF_BLK:(k + 1) * FF_BLK].astype(F32)
            up = z_ref[:, D_FF + k * FF_BLK:D_FF + (k + 1) * FF_BLK].astype(F32)
            sg = jax.nn.sigmoid(gate)
            silu = gate * sg
            dwout_v[k] += _dot_tn(silu * up, df)
            da = _dot_nt(df, wout_v[k])
            dz_ref[:, k * FF_BLK:(k + 1) * FF_BLK] = (da * up * (sg * (1.0 + gate * (1.0 - sg)))).astype(BF16)
            dz_ref[:, D_FF + k * FF_BLK:D_FF + (k + 1) * FF_BLK] = (da * silu).astype(BF16)

        @pl.when(i == nt - 1)
        def _():
            c = pltpu.make_async_copy(dwout_v, dwout_hbm, sems.at[0])
            c.start()
            c.wait()

    row = _rows(tm, D_MODEL)
    return pl.pallas_call(
        body, name=name, grid=(nt,),
        in_specs=[row, row, _rows(tm, 2 * D_FF), _full((1, D_MODEL)), ANY],
        out_specs=[_rows(tm, 2 * D_FF), _full((1, D_MODEL)), ANY],
        out_shape=[jax.ShapeDtypeStruct((T, 2 * D_FF), BF16), jax.ShapeDtypeStruct((1, D_MODEL), F32),
                   jax.ShapeDtypeStruct(w_out.shape, F32)],
        scratch_shapes=[pltpu.VMEM(w_out.shape, BF16), pltpu.VMEM(w_out.shape, F32), pltpu.SemaphoreType.DMA((1,))],
        compiler_params=_params("arbitrary"),
    )(dout, f, z, g_post, w_out)


def _norm_matmul_dw(dz, x, g, n_blocks, *, name, rider=None, together=1):
    T = x.shape[0]
    bw = dz.shape[1] // n_blocks
    tm = DW_ROW_TILE

    def body(dz_ref, x_ref, g_ref, dw_ref):
        @pl.when(pl.program_id(1) == 0)
        def _():
            dw_ref[...] = jnp.zeros_like(dw_ref)

        xv = x_ref[...]
        h_t = (xv * _rms(xv) * g_ref[...]).T.astype(BF16)
        for b in range(together):
            dw_ref[b] += jnp.dot(h_t, dz_ref[:, b * bw:(b + 1) * bw], preferred_element_type=F32)

    res = _pallas(
        body, name=name, grid=(n_blocks // together, T // tm),
        in_specs=[pl.BlockSpec((tm, together * bw), lambda j, t: (t, j)), pl.BlockSpec((tm, D_MODEL), lambda j, t: (t, 0)),
                  _full((1, D_MODEL))],
        out_specs=[pl.BlockSpec((together, D_MODEL, bw), lambda j, t: (j, 0, 0))],
        out_shape=[jax.ShapeDtypeStruct((n_blocks, D_MODEL, bw), F32)], semantics=("parallel", "arbitrary"),
        args=(dz, x, g), rider=rider)
    return res[0] if rider is None else (res[0][0], res[1])


def _norm_matmul_dx(dz, x, dres, g, w, *, name, rider=None, part=(0, 1), into=None):
    T = x.shape[0]
    nb, _, bw = w.shape
    tm = FFN_ROW_TILE
    first, steps = round(part[0] * T) // tm, round((part[1] - part[0]) * T) // tm

    def body(dz_ref, x_ref, dres_ref, g_ref, w_hbm, *rest):
        dx_ref, dg_ref, w_v, sems = rest[-4:]

        @pl.when(pl.program_id(0) == 0)
        def _():
            _load_once([(w_hbm, w_v)], sems)
            dg_ref[...] = jnp.zeros_like(dg_ref)

        xv = x_ref[...]
        r = _rms(xv)
        xn = xv * r
        gv = g_ref[...]
        dh = jnp.zeros((tm, D_MODEL), F32)
        for j in range(nb):
            dh = dh + _dot_nt(dz_ref[:, j * bw:(j + 1) * bw], w_v[j])
        dg_ref[...] += jnp.sum(dh * xn, axis=0, keepdims=True)
        dx_ref[...] = _rms_bwd(dh * gv, xn, r) + dres_ref[...]

    rows = lambda width: pl.BlockSpec((tm, width), lambda i: (first + i, 0))
    row = rows(D_MODEL)
    return _pallas(
        body, name=name, grid=(steps,),
        in_specs=[rows(nb * bw), row, row, _full((1, D_MODEL)), ANY] + ([] if into is None else [ANY]),
        out_specs=[row, _full((1, D_MODEL))],
        out_shape=[jax.ShapeDtypeStruct((T, D_MODEL), F32), jax.ShapeDtypeStruct((1, D_MODEL), F32)],
        scratch_shapes=[pltpu.VMEM(w.shape, BF16), pltpu.SemaphoreType.DMA((1,))],
        semantics=("arbitrary",), args=(dz, x, dres, g, w) + (() if into is None else (into,)), rider=rider,
        aliases={} if into is None else {5: 0})


def _mix_in_fwd(x, g, w):
    T = x.shape[0]
    tm = ROW_TILE

    def body(x_ref, g_ref, w_ref, q_ref, k_ref, v_ref, u_ref):
        xv = x_ref[...]
        h = (xv * _rms(xv) * g_ref[...]).astype(BF16)
        for j, o_ref in enumerate((q_ref, k_ref, v_ref, u_ref)):
            o_ref[...] = _dot(h, w_ref[j])

    col = _rows(tm, ATTN_W)
    return pl.pallas_call(
        body, name="mix_in_fwd", grid=(T // tm,),
        in_specs=[_rows(tm, D_MODEL), _full((1, D_MODEL)), _full(w.shape)],
        out_specs=[col] * 4, out_shape=[jax.ShapeDtypeStruct((T, ATTN_W), F32)] * 4,
        compiler_params=_params("parallel"),
    )(x, g, w)


def _mix_out_fwd(x, outs, lses, ssm, w, g):
    T = x.shape[0]
    tm = ROW_TILE

    def body(x_ref, o1, o2, o3, l1, l2, l3, s_ref, w_ref, g_ref, o_ref, m_ref, a_ref, lse_ref):
        ls = [l1[...], l2[...], l3[...]]
        top = jnp.maximum(jnp.maximum(ls[0], ls[1]), ls[2])
        lse = top + jnp.log(jnp.exp(ls[0] - top) + jnp.exp(ls[1] - top) + jnp.exp(ls[2] - top))
        attn = jnp.exp(ls[0] - lse) * o1[...] + jnp.exp(ls[1] - lse) * o2[...] + jnp.exp(ls[2] - lse) * o3[...]
        lse_ref[...] = lse
        a_ref[...] = attn
        mixed = _dot(attn, w_ref[0]) + _dot(s_ref[...], w_ref[1])
        m_ref[...] = mixed
        o_ref[...] = x_ref[...] + mixed * _rms(mixed) * g_ref[...]

    row, col = _rows(tm, D_MODEL), _rows(tm, ATTN_W)
    return pl.pallas_call(
        body, name="mix_out_fwd", grid=(T // tm,),
        in_specs=[row] + [col] * 7 + [_full(w.shape), _full((1, D_MODEL))],
        out_specs=[row, row, col, col],
        out_shape=[jax.ShapeDtypeStruct((T, D_MODEL), F32)] * 2 + [jax.ShapeDtypeStruct((T, ATTN_W), F32)] * 2,
        compiler_params=_params("parallel"),
    )(x, *outs, *lses, ssm, w, g)


def _mix_out_bwd(dout, mixed, attn, ssm, w, g, rider=None):
    T = dout.shape[0]
    tm = ROW_TILE

    def body(dout_ref, m_ref, a_ref, s_ref, w_ref, g_ref, da_ref, ds_ref, dw_ref, dg_ref):
        @pl.when(pl.program_id(0) == 0)
        def _():
            dw_ref[...] = jnp.zeros_like(dw_ref)
            dg_ref[...] = jnp.zeros_like(dg_ref)

        dy = dout_ref[...]
        mixed = m_ref[...]
        r = _rms(mixed)
        mn = mixed * r
        dg_ref[...] += jnp.sum(dy * mn, axis=0, keepdims=True)
        dm = _rms_bwd(dy * g_ref[...], mn, r).astype(BF16)
        da_ref[...] = _dot_nt(dm, w_ref[0])
        ds_ref[...] = _dot_nt(dm, w_ref[1])
        dw_ref[0] += _dot_tn(a_ref[...], dm)
        dw_ref[1] += _dot_tn(s_ref[...], dm)

    row, col = _rows(tm, D_MODEL), _rows(tm, ATTN_W)
    return _pallas(
        body, name="mix_out_bwd", grid=(T // tm,),
        in_specs=[row, row, col, col, _full(w.shape), _full((1, D_MODEL))],
        out_specs=[col, col, _full(w.shape), _full((1, D_MODEL))],
        out_shape=[jax.ShapeDtypeStruct((T, ATTN_W), F32)] * 2
        + [jax.ShapeDtypeStruct(w.shape, F32), jax.ShapeDtypeStruct((1, D_MODEL), F32)],
        semantics=("arbitrary",), args=(dout, mixed, attn, ssm, w, g), rider=rider)


ATTN_TILING = {1: (ATTN_W, 1), 4: (2 * HEAD_DIM, 4), 16: (2 * HEAD_DIM, 4)}


def _class_rows(d, r):
    return (pl.ds(r, QBLK, stride=d), slice(None)) if d > 1 else (slice(None), slice(None))


def _for_class_groups(d, group, fn):
    if d == group:
        fn(0)
    else:
        lax.fori_loop(0, d // group, lambda n, carry: (fn(n * group), carry)[1], 0)


def _block_slopes(lanes, lane_block):
    heads = lanes // HEAD_DIM
    first = lane_block * heads
    return [jnp.exp2(-jnp.full((1, 1), first + hh + 1, jnp.int32).astype(F32)) for hh in range(heads)]


def _attn_specs(d, nb, lanes):
    blk = (QBLK * d, lanes)
    cur = pl.BlockSpec(blk, lambda j, lb: (j, lb))
    prev = pl.BlockSpec(blk, lambda j, lb: (jnp.maximum(j - 1, 0), lb))
    nxt = pl.BlockSpec(blk, lambda j, lb: (jnp.minimum(j + 1, nb - 1), lb))
    return cur, prev, nxt


def _attn_branch_fwd(q, k, v, d):
    T = q.shape[0]
    nb = T // (d * QBLK)
    lanes, group = ATTN_TILING[d]
    scale = HEAD_DIM ** -0.5

    def body(q_ref, kc_ref, kp_ref, vc_ref, vp_ref, o_ref, l_ref, q_s, kk_s, vv_s, o_s, l_s):
        j = pl.program_id(0)
        qi = lax.broadcasted_iota(jnp.int32, (QBLK, 2 * QBLK), 0)
        ci = lax.broadcasted_iota(jnp.int32, (QBLK, 2 * QBLK), 1)
        steps = QBLK + qi - ci
        valid = (steps >= 0) & (steps <= N_BACK) & ((ci >= QBLK) | (j > 0))
        dist = (steps * d).astype(F32)
        slopes = _block_slopes(lanes, pl.program_id(1))

        def classes(first):
            for n in range(group):
                rows = _class_rows(d, first + n)
                q_s[n] = q_ref[rows]
                kk_s[n, :QBLK], kk_s[n, QBLK:] = kp_ref[rows], kc_ref[rows]
                vv_s[n, :QBLK], vv_s[n, QBLK:] = vp_ref[rows], vc_ref[rows]
            for n in range(group):
                for hh in range(lanes // HEAD_DIM):
                    sl = slice(hh * HEAD_DIM, (hh + 1) * HEAD_DIM)
                    s = _dot_nt(q_s[n, :, sl], kk_s[n, :, sl]) * scale - slopes[hh] * dist
                    s = jnp.where(valid, s, NEG)
                    m = jnp.max(s, axis=-1, keepdims=True)
                    p = jnp.exp(s - m)
                    den = jnp.sum(p, axis=-1, keepdims=True)
                    o_s[n, :, sl] = _dot(p, vv_s[n, :, sl]) / den
                    l_s[n, :, sl] = jnp.broadcast_to(m + jnp.log(den), (QBLK, HEAD_DIM))
            for n in range(group):
                rows = _class_rows(d, first + n)
                o_ref[rows] = o_s[n]
                l_ref[rows] = l_s[n]

        _for_class_groups(d, group, classes)

    cur, prev, _ = _attn_specs(d, nb, lanes)
    shape = jax.ShapeDtypeStruct((T, ATTN_W), F32)
    one, two = pltpu.VMEM((group, QBLK, lanes), F32), pltpu.VMEM((group, 2 * QBLK, lanes), F32)
    return pl.pallas_call(
        body, name=f"attn_fwd_d{d}", grid=(nb, ATTN_W // lanes),
        in_specs=[cur, cur, prev, cur, prev], out_specs=[cur, cur], out_shape=[shape, shape],
        scratch_shapes=[one, two, two, one, one], compiler_params=_params("parallel", "parallel"),
    )(q, k, k, v, v)


def _attn_branch_bwd(q, k, v, o, lse, do, d, rider=None):
    T = q.shape[0]
    nb = T // (d * QBLK)
    lanes, group = ATTN_TILING[d]
    scale = HEAD_DIM ** -0.5

    def body(q_ref, kc_ref, kp_ref, vc_ref, vp_ref, o_ref, l_ref, do_ref, dq_ref, dk_ref, dv_ref,
             q_s, o_s, l_s, do_s, kk_s, vv_s, dq_s, dk_s, dv_s, ck_s, cv_s):
        j = pl.program_id(1)

        @pl.when(j == 0)
        def _():
            ck_s[...] = jnp.zeros_like(ck_s)
            cv_s[...] = jnp.zeros_like(cv_s)

        qi = lax.broadcasted_iota(jnp.int32, (QBLK, 2 * QBLK), 0)
        ci = lax.broadcasted_iota(jnp.int32, (QBLK, 2 * QBLK), 1)
        steps = QBLK + qi - ci
        valid = (steps >= 0) & (steps <= N_BACK) & ((ci >= QBLK) | (j > 0))
        dist = (steps * d).astype(F32)
        lo, hi = slice(0, QBLK), slice(QBLK, 2 * QBLK)
        slopes = _block_slopes(lanes, pl.program_id(0))

        def classes(first):
            for n in range(group):
                rows = _class_rows(d, first + n)
                q_s[n], o_s[n], l_s[n], do_s[n] = q_ref[rows], o_ref[rows], l_ref[rows], do_ref[rows]
                kk_s[n, lo], kk_s[n, hi] = kp_ref[rows], kc_ref[rows]
                vv_s[n, lo], vv_s[n, hi] = vp_ref[rows], vc_ref[rows]
            for n in range(group):
                for hh in range(lanes // HEAD_DIM):
                    sl = slice(hh * HEAD_DIM, (hh + 1) * HEAD_DIM)
                    qh, doh, kk, vv = q_s[n, :, sl], do_s[n, :, sl], kk_s[n, :, sl], vv_s[n, :, sl]
                    delta = jnp.sum(doh * o_s[n, :, sl], axis=-1, keepdims=True)
                    s = jnp.where(valid, _dot_nt(qh, kk) * scale - slopes[hh] * dist, NEG)
                    p = jnp.exp(s - l_s[n, :, hh * HEAD_DIM:hh * HEAD_DIM + 1])
                    ds = p * (_dot_nt(doh, vv) - delta)
                    dq_s[n, :, sl] = _dot(ds, kk) * scale
                    dkk = _dot_tn(ds, qh) * scale
                    dvv = _dot_tn(p, doh)
                    dk_s[n, :, sl] = ck_s[first + n, :, sl] + dkk[lo]
                    dv_s[n, :, sl] = cv_s[first + n, :, sl] + dvv[lo]
                    ck_s[first + n, :, sl] = dkk[hi]
                    cv_s[first + n, :, sl] = dvv[hi]
            for n in range(group):
                rows = _class_rows(d, first + n)
                dq_ref[rows] = dq_s[n]
                dk_ref[rows] = dk_s[n]
                dv_ref[rows] = dv_s[n]

        @pl.when(j < nb)
        def _():
            _for_class_groups(d, group, classes)

        @pl.when(j == nb)
        def _():
            for r in range(d):
                dk_ref[_class_rows(d, r)] = ck_s[r]
                dv_ref[_class_rows(d, r)] = cv_s[r]

    blk = (QBLK * d, lanes)
    here = lambda j: jnp.minimum(j, nb - 1)
    cur = pl.BlockSpec(blk, lambda lb, j: (here(j), lb))
    prev = pl.BlockSpec(blk, lambda lb, j: (jnp.maximum(here(j) - 1, 0), lb))
    behind = pl.BlockSpec(blk, lambda lb, j: (jnp.maximum(j - 1, 0), lb))
    shape = jax.ShapeDtypeStruct((T, ATTN_W), F32)
    one, two = pltpu.VMEM((group, QBLK, lanes), F32), pltpu.VMEM((group, 2 * QBLK, lanes), F32)
    carry = pltpu.VMEM((d, QBLK, lanes), F32)
    return _pallas(
        body, name=f"attn_bwd_d{d}", grid=(ATTN_W // lanes, nb + 1),
        in_specs=[cur, cur, prev, cur, prev, cur, cur, cur], out_specs=[cur, behind, behind], out_shape=[shape] * 3,
        scratch_shapes=[one] * 4 + [two] * 2 + [one] * 3 + [carry] * 2,
        semantics=("parallel", "arbitrary"), args=(q, k, k, v, v, o, lse, do), rider=rider)


def _mix_in_dw(dqs, dks, dvs, du, x, g):
    T = x.shape[0]
    tm = ROW_TILE

    def body(*refs):
        x_ref, g_ref, dz_ref, dw_ref = refs[10:]

        @pl.when(pl.program_id(0) == 0)
        def _():
            dw_ref[...] = jnp.zeros_like(dw_ref)

        xv = x_ref[...]
        h_t = (xv * _rms(xv) * g_ref[...]).T.astype(BF16)
        for part in range(4):
            a, b, c = refs[3 * part:3 * part + 3] if part < 3 else (refs[9], None, None)
            dz = (a[...] if b is None else a[...] + b[...] + c[...]).astype(BF16)
            dz_ref[:, part * ATTN_W:(part + 1) * ATTN_W] = dz
            dw_ref[part] += jnp.dot(h_t, dz, preferred_element_type=F32)

    col = _rows(tm, ATTN_W)
    return pl.pallas_call(
        body, name="mix_bwd_dw", grid=(T // tm,),
        in_specs=[col] * 10 + [_rows(tm, D_MODEL), _full((1, D_MODEL))],
        out_specs=[_rows(tm, 4 * ATTN_W), _full((4, D_MODEL, ATTN_W))],
        out_shape=[jax.ShapeDtypeStruct((T, 4 * ATTN_W), BF16), jax.ShapeDtypeStruct((4, D_MODEL, ATTN_W), F32)],
        compiler_params=_params("arbitrary"),
    )(*dqs, *dks, *dvs, du, x, g)


SCAN_ROWS = 8
SCAN_LANES = 512
SSM_CHUNK = 512
SSM_CHUNK_BWD = 512
SSM_HALVES = tuple((slice(h * SSM_W // 2, (h + 1) * SSM_W // 2), slice(h * N_STATE // 2, (h + 1) * N_STATE // 2)) for h in range(2))


def _cmul(ar, ai, br, bi):
    return ar * br - ai * bi, ar * bi + ai * br


def _ssm_discretize(a_re, a_im, log_dt, b_re, b_im):
    def body(ar_ref, ai_ref, ldt_ref, br_ref, bi_ref, abr_ref, abi_ref, er_ref, ei_ref, bbr_ref, bbi_ref, pr_ref, pi_ref):
        ar, ai = ar_ref[...], ai_ref[...]
        dt = jnp.exp(ldt_ref[...])
        n = lax.broadcasted_iota(jnp.int32, (1, SCAN_ROWS), 1).astype(F32) + 1.0
        mag, ang = jnp.exp(dt * ar), dt * ai
        abr, abi = mag * jnp.cos(ang), mag * jnp.sin(ang)
        abr_ref[...], abi_ref[...] = abr, abi
        pr_ref[...] = jnp.exp(dt * ar * n) * jnp.cos(ang * n)
        pi_ref[...] = jnp.exp(dt * ar * n) * jnp.sin(ang * n)
        den = ar * ar + ai * ai
        er = ((abr - 1.0) * ar + abi * ai) / den
        ei = (abi * ar - (abr - 1.0) * ai) / den
        er_ref[...], ei_ref[...] = er, ei
        bbr_ref[...], bbi_ref[...] = _cmul(er, ei, br_ref[...], bi_ref[...])

    col = jax.ShapeDtypeStruct((N_STATE, 1), F32)
    mat = jax.ShapeDtypeStruct((N_STATE, GROUP_CH), F32)
    pw = jax.ShapeDtypeStruct((N_STATE, SCAN_ROWS), F32)
    return pl.pallas_call(body, name="ssm_discretize", out_shape=[col] * 4 + [mat] * 2 + [pw] * 2)(
        a_re, a_im, log_dt, b_re, b_im)


def _ssm_discretize_bwd(a_re, a_im, log_dt, b_re, b_im, ab_re, ab_im, e_re, e_im, dab_re, dab_im, dbb_re, dbb_im):
    def body(ar_ref, ai_ref, ldt_ref, br_ref, bi_ref, abr_ref, abi_ref, er_ref, ei_ref, dabr_ref, dabi_ref,
             dbbr_ref, dbbi_ref, dar_ref, dai_ref, ddt_ref, dbr_ref, dbi_ref):
        ar, ai, dt = ar_ref[...], ai_ref[...], jnp.exp(ldt_ref[...])
        er, ei = er_ref[...], ei_ref[...]
        gbr, gbi = dbbr_ref[...], dbbi_ref[...]
        dbr_ref[...], dbi_ref[...] = _cmul(er, -ei, gbr, gbi)
        br, bi = br_ref[...], bi_ref[...]
        der = jnp.sum(br * gbr + bi * gbi, axis=-1, keepdims=True)
        dei = jnp.sum(br * gbi - bi * gbr, axis=-1, keepdims=True)
        den = ar * ar + ai * ai
        inv_r, inv_i = ar / den, -ai / den
        t_r, t_i = _cmul(der, dei, inv_r, -inv_i)
        gab_r, gab_i = dabr_ref[...] + t_r, dabi_ref[...] + t_i
        q_r, q_i = _cmul(er, ei, inv_r, inv_i)
        dl_r, dl_i = _cmul(der, dei, q_r, -q_i)
        dl_r, dl_i = -dl_r, -dl_i
        gw_r, gw_i = _cmul(gab_r, gab_i, abr_ref[...], -abi_ref[...])
        dar_ref[...] = dl_r + dt * gw_r
        dai_ref[...] = dl_i + dt * gw_i
        ddt_ref[...] = (gw_r * ar + gw_i * ai) * dt

    col = jax.ShapeDtypeStruct((N_STATE, 1), F32)
    mat = jax.ShapeDtypeStruct((N_STATE, GROUP_CH), F32)
    return pl.pallas_call(body, name="ssm_discretize_bwd", out_shape=[col] * 3 + [mat] * 2)(
        a_re, a_im, log_dt, b_re, b_im, ab_re, ab_im, e_re, e_im, dab_re, dab_im, dbb_re, dbb_im)


def _scan_tables(p_re, p_im, reverse):
    pr, pi = p_re.T, p_im.T
    if reverse:
        pi = -pi
    row = jnp.arange(SCAN_ROWS)[:, None]
    level = lambda t, s: jnp.where((row < SCAN_ROWS - s) if reverse else (row >= s), t[s - 1][None, :], 0.0)
    carry = (pr[::-1], pi[::-1]) if reverse else (pr, pi)
    return jnp.stack([level(pr, 1), level(pi, 1), level(pr, 2), level(pi, 2), level(pr, 4), level(pi, 4), carry[0], carry[1]])


def _scan_group(xr, xi, tab_ref, ls, carry_r, carry_i, reverse):
    for n, s in enumerate((1, 2, 4)):
        shift = SCAN_ROWS - s if reverse else s
        mr, mi = _cmul(tab_ref[2 * n, :, ls], tab_ref[2 * n + 1, :, ls], pltpu.roll(xr, shift, 0), pltpu.roll(xi, shift, 0))
        xr, xi = xr + mr, xi + mi
    mr, mi = _cmul(tab_ref[6, :, ls], tab_ref[7, :, ls], carry_r, carry_i)
    return xr + mr, xi + mi


def _gelu(y):
    t = jnp.tanh(GELU_C * (y + 0.044715 * y * y * y))
    return 0.5 * y * (1.0 + t), t


def _ssm_fwd(u, tab, bd_re, bd_im, cd_re, cd_im, d_skip, w_glu, b_glu):
    T = u.shape[0]
    tc = SSM_CHUNK

    def body(u_ref, tab_ref, bdr_ref, bdi_ref, cdr_ref, cdi_ref, dsk_ref, wg_ref, bg_ref,
             sr_ref, si_ref, yp_ref, o_ref, car_r, car_i):
        @pl.when(pl.program_id(0) == 0)
        def _():
            car_r[...] = jnp.zeros_like(car_r)
            car_i[...] = jnp.zeros_like(car_i)

        uv = u_ref[...]
        for cs, ss in SSM_HALVES:
            sr_ref[:, ss] = _dot(uv[:, cs], bdr_ref[cs, ss])
            si_ref[:, ss] = _dot(uv[:, cs], bdi_ref[cs, ss])
        for lb in range(N_STATE // SCAN_LANES):
            ls = pl.ds(lb * SCAN_LANES, SCAN_LANES)

            def step(g, carry):
                rows = pl.ds(pl.multiple_of(g * SCAN_ROWS, SCAN_ROWS), SCAN_ROWS)
                xr, xi = _scan_group(sr_ref[rows, ls], si_ref[rows, ls], tab_ref, ls, carry[0], carry[1], False)
                sr_ref[rows, ls] = xr
                si_ref[rows, ls] = xi
                last = slice(SCAN_ROWS - 1, SCAN_ROWS)
                return (jnp.broadcast_to(xr[last], xr.shape), jnp.broadcast_to(xi[last], xi.shape))

            cr, ci = lax.fori_loop(0, tc // SCAN_ROWS, step, (car_r[:, ls], car_i[:, ls]))
            car_r[:, ls] = cr
            car_i[:, ls] = ci
        y = jnp.concatenate([_dot(sr_ref[:, ss], cdr_ref[ss, cs]) - _dot(si_ref[:, ss], cdi_ref[ss, cs])
                             for cs, ss in SSM_HALVES], axis=1) + dsk_ref[...] * uv
        yp_ref[...] = y
        gy, _ = _gelu(y)
        o_ref[...] = gy * jax.nn.sigmoid(_dot(gy, wg_ref[...]) + bg_ref[...])

    col, st = _rows(tc, SSM_W), _rows(tc, N_STATE)
    vec = _full((1, SSM_W))
    return pl.pallas_call(
        body, name="ssm_fwd", grid=(T // tc,),
        in_specs=[col, _full(tab.shape), _full(bd_re.shape), _full(bd_im.shape), _full(cd_re.shape), _full(cd_im.shape),
                  vec, _full(w_glu.shape), vec],
        out_specs=[st, st, col, col],
        out_shape=[jax.ShapeDtypeStruct((T, N_STATE), F32)] * 2 + [jax.ShapeDtypeStruct((T, SSM_W), F32)] * 2,
        scratch_shapes=[pltpu.VMEM((SCAN_ROWS, N_STATE), F32)] * 2,
        compiler_params=_params("arbitrary"),
    )(u, tab, bd_re, bd_im, cd_re, cd_im, d_skip, w_glu, b_glu)


def _ssm_bwd(dout, u, yp, s_re, s_im, tab, bd_re, bd_im, cd_re, cd_im, d_skip, w_glu, b_glu, rider=None):
    T = u.shape[0]
    tc = SSM_CHUNK_BWD
    nt = T // tc
    rows_per_chunk = tc // SCAN_ROWS

    def body(do_ref, u_ref, yp_ref, sr_ref, si_ref, pr_ref, pi_ref, tab_ref, dsk_ref, wg_ref, bg_ref,
             bdr_hbm, bdi_hbm, cdr_hbm, cdi_hbm,
             du_ref, dsk_out, dbg_out, dwg_out, da_out, dbdr_out, dbdi_out, dcdr_out, dcdi_out,
             bdr_v, bdi_v, cdr_v, cdi_v, dbdr_v, dbdi_v, dcdr_v, dcdi_v, gr_v, gi_v, car_r, car_i, sems):
        i = pl.program_id(0)

        @pl.when(i == 0)
        def _():
            _load_once([(hbm.at[(ss, cs) if by_state else (cs, ss)], vmem.at[h])
                        for hbm, vmem, by_state in ((bdr_hbm, bdr_v, False), (bdi_hbm, bdi_v, False),
                                                    (cdr_hbm, cdr_v, True), (cdi_hbm, cdi_v, True))
                        for h, (cs, ss) in enumerate(SSM_HALVES)], sems)
            for ref in (dbdr_v, dbdi_v, dcdr_v, dcdi_v, car_r, car_i, dsk_out, dbg_out, dwg_out, da_out):
                ref[...] = jnp.zeros_like(ref)

        uv, y, dout_v = u_ref[...], yp_ref[...], do_ref[...]
        gy, t = _gelu(y)
        sg = jax.nn.sigmoid(_dot(gy, wg_ref[...]) + bg_ref[...])
        dzg = dout_v * gy * sg * (1.0 - sg)
        dgy = dout_v * sg + _dot_nt(dzg, wg_ref[...])
        dwg_out[...] += _dot_tn(gy, dzg)
        dbg_out[...] += jnp.sum(dzg, axis=0, keepdims=True)
        dy = dgy * (0.5 * (1.0 + t) + 0.5 * y * (1.0 - t * t) * GELU_C * (1.0 + 3 * 0.044715 * y * y))
        dsk_out[...] += jnp.sum(dy * uv, axis=0, keepdims=True)

        for h, (cs, ss) in enumerate(SSM_HALVES):
            gr_v[:, ss] = _dot_nt(dy[:, cs], cdr_v[h])
            gi_v[:, ss] = -_dot_nt(dy[:, cs], cdi_v[h])
            dcdr_v[h] += _dot_tn(sr_ref[:, ss], dy[:, cs])
            dcdi_v[h] -= _dot_tn(si_ref[:, ss], dy[:, cs])

        row = lax.broadcasted_iota(jnp.int32, (SCAN_ROWS, SCAN_LANES), 0)
        first_chunk = i == nt - 1
        for lb in range(N_STATE // SCAN_LANES):
            ls = pl.ds(lb * SCAN_LANES, SCAN_LANES)

            def step(n, carry):
                g = rows_per_chunk - 1 - n
                rows = pl.ds(pl.multiple_of(g * SCAN_ROWS, SCAN_ROWS), SCAN_ROWS)
                before = pl.ds(pl.multiple_of(jnp.maximum(g - 1, 0) * SCAN_ROWS, SCAN_ROWS), SCAN_ROWS)
                xr, xi = _scan_group(gr_v[rows, ls], gi_v[rows, ls], tab_ref, ls, carry[0], carry[1], True)
                gr_v[rows, ls] = xr
                gi_v[rows, ls] = xi
                last = slice(SCAN_ROWS - 1, SCAN_ROWS)
                edge_r = jnp.where(g > 0, sr_ref[before, ls][last], jnp.where(first_chunk, 0.0, pr_ref[:, ls][last]))
                edge_i = jnp.where(g > 0, si_ref[before, ls][last], jnp.where(first_chunk, 0.0, pi_ref[:, ls][last]))
                spr = jnp.where(row >= 1, pltpu.roll(sr_ref[rows, ls], 1, 0), edge_r)
                spi = jnp.where(row >= 1, pltpu.roll(si_ref[rows, ls], 1, 0), edge_i)
                first = slice(0, 1)
                return (jnp.broadcast_to(xr[first], xr.shape), jnp.broadcast_to(xi[first], xi.shape),
                        carry[2] + xr * spr + xi * spi, carry[3] + xi * spr - xr * spi)

            zero = jnp.zeros((SCAN_ROWS, SCAN_LANES), F32)
            cr, ci, dar, dai = lax.fori_loop(0, rows_per_chunk, step, (car_r[:, ls], car_i[:, ls], zero, zero))
            car_r[:, ls] = cr
            car_i[:, ls] = ci
            da_out[0, :, ls] += dar
            da_out[1, :, ls] += dai

        du_ref[...] = dsk_ref[...] * dy + jnp.concatenate(
            [_dot_nt(gr_v[:, ss], bdr_v[h]) + _dot_nt(gi_v[:, ss], bdi_v[h]) for h, (_, ss) in enumerate(SSM_HALVES)], axis=1)
        for h, (cs, ss) in enumerate(SSM_HALVES):
            dbdr_v[h] += _dot_tn(uv[:, cs], gr_v[:, ss])
            dbdi_v[h] += _dot_tn(uv[:, cs], gi_v[:, ss])

        @pl.when(i == nt - 1)
        def _():
            per_half = N_GROUPS // len(SSM_HALVES)
            for grp in range(N_GROUPS):
                h, at = divmod(grp, per_half)
                ch, stt = pl.ds(at * GROUP_CH, GROUP_CH), pl.ds(at * STATE, STATE)
                dbdr_out[grp], dbdi_out[grp] = dbdr_v[h, ch, stt], dbdi_v[h, ch, stt]
                dcdr_out[grp], dcdi_out[grp] = dcdr_v[h, stt, ch], dcdi_v[h, stt, ch]

    rev = lambda i: (nt - 1 - i, 0)
    col = pl.BlockSpec((tc, SSM_W), rev)
    st = pl.BlockSpec((tc, N_STATE), rev)
    st_before = pl.BlockSpec((SCAN_ROWS, N_STATE), lambda i: (jnp.maximum((nt - 1 - i) * rows_per_chunk - 1, 0), 0))
    vec = _full((1, SSM_W))
    bd_blocks, cd_blocks = (N_GROUPS, GROUP_CH, STATE), (N_GROUPS, STATE, GROUP_CH)
    bd, cd = jax.ShapeDtypeStruct(bd_blocks, F32), jax.ShapeDtypeStruct(cd_blocks, F32)
    n_half = len(SSM_HALVES)
    bd_half, cd_half = (n_half, SSM_W // n_half, N_STATE // n_half), (n_half, N_STATE // n_half, SSM_W // n_half)
    return _pallas(
        body, name="ssm_bwd", grid=(nt,),
        in_specs=[col, col, col, st, st, st_before, st_before, _full(tab.shape), vec, _full(w_glu.shape), vec,
                  ANY, ANY, ANY, ANY],
        out_specs=[col, vec, vec, _full(w_glu.shape), _full((2, SCAN_ROWS, N_STATE)),
                   _full(bd_blocks), _full(bd_blocks), _full(cd_blocks), _full(cd_blocks)],
        out_shape=[jax.ShapeDtypeStruct((T, SSM_W), F32), jax.ShapeDtypeStruct((1, SSM_W), F32),
                   jax.ShapeDtypeStruct((1, SSM_W), F32), jax.ShapeDtypeStruct(w_glu.shape, F32),
                   jax.ShapeDtypeStruct((2, SCAN_ROWS, N_STATE), F32), bd, bd, cd, cd],
        scratch_shapes=[pltpu.VMEM(bd_half, BF16)] * 2 + [pltpu.VMEM(cd_half, BF16)] * 2
        + [pltpu.VMEM(bd_half, F32)] * 2 + [pltpu.VMEM(cd_half, F32)] * 2
        + [pltpu.VMEM((tc, N_STATE), F32)] * 2 + [pltpu.VMEM((SCAN_ROWS, N_STATE), F32)] * 2
        + [pltpu.SemaphoreType.DMA((4 * len(SSM_HALVES),))],
        semantics=("arbitrary",), rider=rider,
        args=(dout, u, yp, s_re, s_im, s_re, s_im, tab, d_skip, w_glu, b_glu, bd_re, bd_im, cd_re, cd_im))


def _block_diagonals(parts):
    def body(*refs):
        for t_ref, o_ref in zip(refs[:len(parts)], refs[len(parts):]):
            _, a, b = t_ref.shape
            o_ref[...] = jnp.zeros_like(o_ref)
            for grp in range(N_GROUPS):
                o_ref[grp * a:(grp + 1) * a, grp * b:(grp + 1) * b] = t_ref[grp].astype(BF16)

    return pl.pallas_call(body, name="block_diagonals",
                          out_shape=[jax.ShapeDtypeStruct((N_GROUPS * t.shape[1], N_GROUPS * t.shape[2]), BF16) for t in parts])(*parts)


def _ssm_prepare(a_re, a_im, log_dt, b_re, b_im, c_re, c_im):
    col = lambda t: t.reshape(N_STATE, 1)
    ldt = jnp.broadcast_to(log_dt.reshape(N_GROUPS, 1), (N_GROUPS, STATE)).reshape(N_STATE, 1)
    b2r, b2i = b_re.reshape(N_STATE, GROUP_CH), b_im.reshape(N_STATE, GROUP_CH)
    ab_r, ab_i, e_r, e_i, bb_r, bb_i, p_r, p_i = _ssm_discretize(col(a_re), col(a_im), ldt, b2r, b2i)
    diag = _block_diagonals([jnp.swapaxes(t.reshape(N_GROUPS, STATE, GROUP_CH), 1, 2) for t in (bb_r, bb_i)]
                            + [jnp.swapaxes(t, 1, 2) for t in (c_re, c_im)])
    bd, cd = diag[:2], diag[2:]
    saved = dict(a_re=col(a_re), a_im=col(a_im), log_dt=ldt, b_re=b2r, b_im=b2i, ab_re=ab_r, ab_im=ab_i, e_re=e_r, e_im=e_i)
    return _scan_tables(p_r, p_i, False), _scan_tables(p_r, p_i, True), bd, cd, saved


BIG = ("ffn1_w_in", "ffn1_w_out", "w_mix_in", "w_glu", "w_mix_out", "ffn2_w_in", "ffn2_w_out")
SMALL = ("ffn1_pre_g", "ffn1_post_g", "mix_pre_g", "a_re", "a_im", "log_dt", "b_re", "b_im", "c_re", "c_im",
         "d_skip", "b_glu", "mix_post_g", "ffn2_pre_g", "ffn2_post_g")


FIRST = ("ffn1_w_in", "ffn1_w_out")
REST = ("w_mix_in", "w_glu", "w_mix_out", "ffn2_w_in", "ffn2_w_out")
LATE = ("w_mix_in", "w_glu", "w_mix_out", "ffn1_w_out")
SHARD_SHAPE = {"ffn1_w_in": (D_MODEL, FF_BLK), "ffn2_w_in": (D_MODEL, FF_BLK), "ffn1_w_out": (D_FF // N_CHIPS, D_MODEL),
               "ffn2_w_out": (D_FF // N_CHIPS, D_MODEL), "w_mix_in": (D_MODEL, ATTN_W), "w_glu": (SSM_W // N_CHIPS, SSM_W),
               "w_mix_out": (2 * ATTN_W // N_CHIPS, D_MODEL)}


class _Reduction:
    def __init__(self, names, grads, place):
        self.names, self.local, self.place = list(names), list(grads), place

    def exchange(self):
        return _pair_exchange(self.local)

    def first(self, got):
        both = [_pair_sum(a, b, self.place["core"], name=f"pair_sum_{n}") for n, a, b in zip(self.names, self.local, got)]
        self.pair = [f32 for f32, _ in both]
        return _reduce_first([wire for _, wire in both])

    def second(self, got):
        both = [_reduce_sum_first(a, b, self.place["sel_first"], name=f"sum_first_{n}")
                for n, a, b in zip(self.names, self.pair, got)]
        self.sums = [f32 for f32, _ in both]
        return _reduce_second([wire for _, wire in both])

    def swap(self, got):
        self.halves = [_reduce_sum_second(a, b, self.place["sel_second"], name=f"sum_second_{n}").reshape(2 * a.shape[2], a.shape[3])
                       for n, a, b in zip(self.names, self.sums, got)]
        return _pair_swap(self.halves)

    def done(self, got):
        return {n: (mine, theirs) for n, mine, theirs in zip(self.names, self.halves, got)}


def _local_step(x, target, p, w, place=None):
    vec = lambda t: t.reshape(1, -1)
    w1_in, w1_out = w["ffn1_w_in"], w["ffn1_w_out"].reshape(2, FF_BLK, D_MODEL)
    blocks = lambda n, t: t.reshape((N_CHIPS,) + SHARD_SHAPE[n])

    ffn1 = functools.partial(_ffn_fwd, x, vec(p["ffn1_pre_g"]), w1_in, w1_out, vec(p["ffn1_post_g"]), None, name="ffn1_fwd")
    if place is None:
        x1, z1, f1 = ffn1()
    else:
        (x1, z1, f1), rest = ffn1(rider=_gather_weights([w[n] for n in REST]))
        w = dict(w, **dict(zip(REST, rest)))
    w2_in, w2_out = w["ffn2_w_in"], w["ffn2_w_out"].reshape(2, FF_BLK, D_MODEL)
    w_mi, w_glu, w_mo = w["w_mix_in"], w["w_glu"].reshape(SSM_W, SSM_W), w["w_mix_out"].reshape(2, ATTN_W, D_MODEL)
    q, k, v, u = _mix_in_fwd(x1, vec(p["mix_pre_g"]), w_mi)
    branches = [_attn_branch_fwd(q, k, v, d) for d in DILATIONS]
    tab_f, tab_b, bd, cd, sv = _ssm_prepare(p["a_re"], p["a_im"], p["log_dt"], p["b_re"], p["b_im"], p["c_re"], p["c_im"])
    ssm_args = (bd[0], bd[1], cd[0], cd[1], vec(p["d_skip"]), w_glu, vec(p["b_glu"]))
    s_re, s_im, yp, ssm = _ssm_fwd(u, tab_f, *ssm_args)
    x2, mixed, attn, lse = _mix_out_fwd(x1, [o for o, _ in branches], [l for _, l in branches], ssm, w_mo, vec(p["mix_post_g"]))
    dx3, loss_rows, z2, f2 = _ffn_fwd(x2, vec(p["ffn2_pre_g"]), w2_in, w2_out, vec(p["ffn2_post_g"]), target, name="ffn2_fwd")

    g = {}
    ride = (lambda call, exchange: call(rider=exchange)) if place else (lambda call, exchange: (call(), None))
    dz2, g["ffn2_post_g"], dw2_out = _ffn_bwd_out(dx3, f2, z2, w2_out, vec(p["ffn2_post_g"]), name="ffn2_bwd_out")
    dw2_in = _norm_matmul_dw(dz2, x2, vec(p["ffn2_pre_g"]), N_CHIPS, name="ffn2_bwd_dw", together=2)
    early = _Reduction(("ffn2_w_in", "ffn2_w_out"), [dw2_in, blocks("ffn2_w_out", dw2_out)], place) if place else None
    (dx2, g["ffn2_pre_g"]), got = ride(
        functools.partial(_norm_matmul_dx, dz2, x2, dx3, vec(p["ffn2_pre_g"]), w2_in, name="ffn2_bwd_dx"), early and early.exchange())
    dattn, dssm, dw_mo, g["mix_post_g"] = _mix_out_bwd(dx2, mixed, attn, ssm, w_mo, vec(p["mix_post_g"]))
    (du, g["d_skip"], g["b_glu"], dw_glu, da, dbd_re, dbd_im, dcd_re, dcd_im), got = ride(
        functools.partial(_ssm_bwd, dssm, u, yp, s_re, s_im, tab_b, *ssm_args), early and early.first(got))
    branch = lambda d: functools.partial(_attn_branch_bwd, q, k, v, attn, lse, dattn, d)
    parts = [None] * 3
    parts[0], got = ride(branch(DILATIONS[0]), early and early.second(got))
    parts[1], early_theirs = ride(branch(DILATIONS[1]), early and early.swap(got))
    parts[2] = branch(DILATIONS[2])()
    dproj, dw_mi = _mix_in_dw([r[0] for r in parts], [r[1] for r in parts], [r[2] for r in parts], du, x1, vec(p["mix_pre_g"]))
    dx1, g["mix_pre_g"] = _norm_matmul_dx(dproj, x1, dx2, vec(p["mix_pre_g"]), w_mi, name="mix_bwd_dx")
    dz1, g["ffn1_post_g"], dw1_out = _ffn_bwd_out(dx1, f1, z1, w1_out, vec(p["ffn1_post_g"]), name="ffn1_bwd_out")
    big = {"ffn2_w_in": dw2_in, "ffn2_w_out": dw2_out, "w_mix_in": dw_mi, "w_glu": dw_glu, "w_mix_out": dw_mo, "ffn1_w_out": dw1_out}
    late = _Reduction(LATE, [blocks(n, big[n]) for n in LATE], place) if place else None
    big["ffn1_w_in"], got = ride(
        functools.partial(_norm_matmul_dw, dz1, x, vec(p["ffn1_pre_g"]), N_CHIPS, name="ffn1_bwd_dw", together=2),
        late and late.exchange())
    dx_call = functools.partial(_norm_matmul_dx, dz1, x, dx1, vec(p["ffn1_pre_g"]), w1_in)
    if place:
        last = _Reduction(("ffn1_w_in",), [big["ffn1_w_in"]], place)
        n_late = len(LATE)
        (dx_part, dg_a), got = dx_call(name="ffn1_bwd_dx_a", part=(0, 0.375), rider=_merged(late.first(got), last.exchange()))
        (dx_part, dg_b), got = dx_call(name="ffn1_bwd_dx_b", part=(0.375, 0.75), into=dx_part,
                                       rider=_merged(late.second(got[:n_late]), last.first(got[n_late:])))
        grad_x, dg_c = dx_call(name="ffn1_bwd_dx_c", part=(0.75, 1), into=dx_part)
        g["ffn1_pre_g"] = dg_a + dg_b + dg_c
        got = _merged(late.swap(got[:n_late]), last.second(got[n_late:])).run("tail_second")
        g.update(late.done(got[:n_late]))
        g.update(last.done(last.swap(got[n_late:]).run("tail_swap")))
        g.update(early.done(early_theirs))
    else:
        grad_x, g["ffn1_pre_g"] = dx_call(name="ffn1_bwd_dx")
        g.update({n: blocks(n, big[n]) for n in BIG})

    g["c_re"], g["c_im"] = (jnp.swapaxes(m, 1, 2) for m in (dcd_re, dcd_im))
    dbb = [jnp.swapaxes(m, 1, 2).reshape(N_STATE, GROUP_CH) for m in (dbd_re, dbd_im)]
    dab = [jnp.sum(da[n], axis=0).reshape(N_STATE, 1) for n in range(2)]
    da_re, da_im, dldt, db_re, db_im = _ssm_discretize_bwd(
        sv["a_re"], sv["a_im"], sv["log_dt"], sv["b_re"], sv["b_im"], sv["ab_re"], sv["ab_im"], sv["e_re"], sv["e_im"],
        dab[0], dab[1], dbb[0], dbb[1])
    g["a_re"], g["a_im"] = da_re.reshape(N_GROUPS, STATE), da_im.reshape(N_GROUPS, STATE)
    g["log_dt"] = jnp.sum(dldt.reshape(N_GROUPS, STATE), axis=-1)
    g["b_re"], g["b_im"] = (t.reshape(N_GROUPS, STATE, GROUP_CH) for t in (db_re, db_im))
    return loss_rows, grad_x, g


MESH = pl.DeviceIdType.MESH
N_REL = 3


def _place():
    x, y, c = lax.axis_index("x"), lax.axis_index("y"), lax.axis_index("c")
    return x, y, c, [(1 - x, y), (x, 1 - y), (1 - x, 1 - y)]


def _remote(src, dst, send_sems, recv_sems, idx, to):
    return pltpu.make_async_remote_copy(src_ref=src, dst_ref=dst, send_sem=send_sems.at[idx], recv_sem=recv_sems.at[idx],
                                        device_id=to, device_id_type=MESH)


def _half(rows, who):
    return pl.ds(who * (rows // 2), rows // 2)


class _Copies:
    def __init__(self, send_sems, recv_sems, local_sems):
        self.send_sems, self.recv_sems, self.local_sems = send_sems, recv_sems, local_sems
        self.n_remote = self.n_local = 0

    def remote(self, src, dst, to):
        k, self.n_remote = self.n_remote, self.n_remote + 1
        return pltpu.make_async_remote_copy(src_ref=src, dst_ref=dst, send_sem=self.send_sems.at[k],
                                            recv_sem=self.recv_sems.at[k], device_id=to, device_id_type=MESH)

    def local(self, src, dst):
        k, self.n_local = self.n_local, self.n_local + 1
        return pltpu.make_async_copy(src, dst, self.local_sems.at[k])


class _Exchange:
    def __init__(self, plan, ins, out_shapes, n_remote):
        self.plan, self.ins, self.out_shapes, self.n_remote = plan, list(ins), list(out_shapes), n_remote
        self.sems = [pltpu.SemaphoreType.DMA((n_remote,)), pltpu.SemaphoreType.DMA((n_remote,)), pltpu.SemaphoreType.DMA((1,))]

    def run(self, name):
        n_in = len(self.ins)

        def body(*refs):
            for phase in self.plan(refs[:n_in], refs[n_in:-3], _Copies(*refs[-3:])):
                for cp in phase:
                    cp.start()
                for cp in phase:
                    cp.wait()

        return pl.pallas_call(body, name=name, in_specs=[ANY] * n_in, out_specs=[ANY] * len(self.out_shapes),
                              out_shape=self.out_shapes, scratch_shapes=self.sems)(*self.ins)


def _merged(a, b):
    n_in, n_out = len(a.ins), len(a.out_shapes)

    def plan(ins, outs, mk):
        (phase_a,), (phase_b,) = a.plan(ins[:n_in], outs[:n_out], mk), b.plan(ins[n_in:], outs[n_out:], mk)
        return [phase_a + phase_b]

    return _Exchange(plan, a.ins + b.ins, a.out_shapes + b.out_shapes, a.n_remote + b.n_remote)


def _pallas(body, *, name, grid, in_specs, out_specs, out_shape, args, semantics, scratch_shapes=(), rider=None, aliases=None):
    aliases = aliases or {}
    if rider is None:
        return pl.pallas_call(body, name=name, grid=grid, in_specs=in_specs, out_specs=out_specs, out_shape=out_shape,
                              scratch_shapes=list(scratch_shapes), input_output_aliases=aliases,
                              compiler_params=_params(*semantics))(*args)
    n_in, n_out, r_in, r_out = len(in_specs), len(out_specs), len(rider.ins), len(rider.out_shapes)
    n_steps = math.prod(grid)

    def carrier(*refs):
        ins, rider_ins = refs[:n_in], refs[n_in:n_in + r_in]
        outs = refs[n_in + r_in:n_in + r_in + n_out]
        rider_outs = refs[n_in + r_in + n_out:n_in + r_in + n_out + r_out]
        scratch, sems = refs[n_in + r_in + n_out + r_out:-3], refs[-3:]
        step = 0
        for axis, size in enumerate(grid):
            step = step * size + pl.program_id(axis)
        phases = rider.plan(rider_ins, rider_outs, _Copies(*sems))

        def start_phase(p):
            for cp in (phases[p - 1] if p else []):
                cp.wait()
            for cp in phases[p]:
                cp.start()

        for p in range(len(phases)):
            pl.when(step == p * n_steps // len(phases))(functools.partial(start_phase, p))
        body(*ins, *outs, *scratch)

        @pl.when(step == n_steps - 1)
        def _():
            for cp in phases[-1]:
                cp.wait()

    results = pl.pallas_call(
        carrier, name=name, grid=grid, in_specs=list(in_specs) + [ANY] * r_in, out_specs=list(out_specs) + [ANY] * r_out,
        out_shape=list(out_shape) + rider.out_shapes, scratch_shapes=list(scratch_shapes) + rider.sems,
        input_output_aliases=aliases, compiler_params=_params(*["arbitrary"] * len(grid)))(*args, *rider.ins)
    return results[:n_out], results[n_out:]


def _gather_weights(shards):
    def plan(ins, outs, mk):
        x, y, c = lax.axis_index("x"), lax.axis_index("y"), lax.axis_index("c")
        me, sibling = 2 * x + y, (x, y, 1 - c)
        x_nb, y_nb, diag = (1 - x, y), (x, 1 - y), (1 - x, 1 - y)
        index = lambda chip: 2 * chip[0] + chip[1]
        first, second, third = [], [], []
        for i, shard in enumerate(shards):
            rows = shard.shape[0]
            mine = _half(rows, c)
            quarter = lambda which: pl.ds(c * (rows // 2) + which * (rows // 4), rows // 4)
            first.append(mk.remote(ins[i], outs[i].at[me], sibling))
            for nb in (x_nb, y_nb):
                first.append(mk.remote(ins[i].at[mine], outs[i].at[me, mine], (*nb, c)))
            for nb in (x_nb, y_nb):
                landed = outs[i].at[index(nb), mine]
                second.append(mk.remote(landed, landed, sibling))
            for nb, other, which in ((x_nb, y_nb, 0), (y_nb, x_nb, 1)):
                landed = outs[i].at[index(nb), quarter(which)]
                second.append(mk.remote(landed, landed, (*other, c)))
            landed = outs[i].at[index(diag), mine]
            third.append(mk.remote(landed, landed, sibling))
        return [first, second, third]

    return _Exchange(plan, shards, [jax.ShapeDtypeStruct((N_CHIPS,) + s.shape, s.dtype) for s in shards], 8 * len(shards))


def _pair_exchange(grads):
    def plan(ins, outs, mk):
        x, y, c = lax.axis_index("x"), lax.axis_index("y"), lax.axis_index("c")
        return [[mk.remote(ins[i].at[:, _half(g.shape[1], 1 - c)], outs[i], (x, y, 1 - c)) for i, g in enumerate(grads)]]

    return _Exchange(plan, grads, [jax.ShapeDtypeStruct((N_CHIPS, g.shape[1] // 2, g.shape[2]), g.dtype) for g in grads], len(grads))


def _reduce_first(pair):
    def plan(ins, outs, mk):
        x, y, c = lax.axis_index("x"), lax.axis_index("y"), lax.axis_index("c")
        phase = []
        for i, p in enumerate(pair):
            q = p.shape[1] // 2
            phase.append(mk.remote(ins[i].at[pl.ds(2 * (1 - x), 2), pl.ds(0, q)], outs[i].at[0], (1 - x, y, c)))
            for jx in range(2):
                phase.append(mk.remote(ins[i].at[2 * jx + 1 - y, pl.ds(q, q)], outs[i].at[1, jx], (x, 1 - y, c)))
        return [phase]

    return _Exchange(plan, pair, [jax.ShapeDtypeStruct((2, 2, p.shape[1] // 2, p.shape[2]), p.dtype) for p in pair], 3 * len(pair))


def _reduce_second(sums):
    def plan(ins, outs, mk):
        x, y, c = lax.axis_index("x"), lax.axis_index("y"), lax.axis_index("c")
        phase = []
        for i in range(len(sums)):
            phase.append(mk.remote(ins[i].at[0, 1 - y], outs[i].at[0], (x, 1 - y, c)))
            phase.append(mk.remote(ins[i].at[1, 1 - x], outs[i].at[1], (1 - x, y, c)))
        return [phase]

    return _Exchange(plan, sums, [jax.ShapeDtypeStruct((2,) + s.shape[2:], s.dtype) for s in sums], 2 * len(sums))


def _pair_swap(halves):
    def plan(ins, outs, mk):
        x, y, c = lax.axis_index("x"), lax.axis_index("y"), lax.axis_index("c")
        return [[mk.remote(ins[i], outs[i], (x, y, 1 - c)) for i in range(len(halves))]]

    return _Exchange(plan, halves, [jax.ShapeDtypeStruct(h.shape, h.dtype) for h in halves], len(halves))


def _allreduce_small(packed):
    rows = packed.shape[0]
    n_dev = 2 * N_CHIPS

    def body(x_ref, o_ref, buf, send_sems, recv_sems):
        x, y, c, chips = _place()
        sibling = (x, y, 1 - c)

        def slot(px, py, pc):
            return buf.at[4 * px + 2 * py + pc]

        buf[4 * x + 2 * y + c] = x_ref[...]
        first = [_remote(x_ref, slot(x, y, c), send_sems, recv_sems, 0, sibling)]
        first += [_remote(x_ref, slot(x, y, c), send_sems, recv_sems, 1 + k, (*chip, c)) for k, chip in enumerate(chips)]
        for cp in first:
            cp.start()
        passed = []
        for k, chip in enumerate(chips):
            landed = slot(*chip, c)
            _remote(landed, landed, send_sems, recv_sems, 1 + k, (*chip, c)).wait_recv()
            passed.append(_remote(landed, landed, send_sems, recv_sems, 1 + N_REL + k, sibling))
            passed[-1].start()
        _remote(slot(*sibling), slot(*sibling), send_sems, recv_sems, 0, sibling).wait_recv()
        for k, chip in enumerate(chips):
            landed = slot(*chip, 1 - c)
            _remote(landed, landed, send_sems, recv_sems, 1 + N_REL + k, sibling).wait_recv()
        for cp in first + passed:
            cp.wait_send()
        total = buf[0]
        for d in range(1, n_dev):
            total = total + buf[d]
        o_ref[...] = total

    vm = pl.BlockSpec(memory_space=pltpu.VMEM)
    return pl.pallas_call(
        body, name="allreduce_small", in_specs=[vm], out_specs=vm, out_shape=jax.ShapeDtypeStruct(packed.shape, F32),
        scratch_shapes=[pltpu.VMEM((n_dev, rows, 128), F32), pltpu.SemaphoreType.DMA((1 + 2 * N_REL,)),
                        pltpu.SemaphoreType.DMA((1 + 2 * N_REL,))],
    )(packed)


def _row_tile(rows, cap=256):
    return max(t for t in range(8, cap + 1, 8) if rows % t == 0)


def _pair_sum(grad, got, c, name):
    _, half, cols = got.shape
    tr = _row_tile(half)
    nt = half // tr

    def body(c_ref, g_ref, r_ref, o_ref, wire_ref):
        total = g_ref[...] + r_ref[...]
        o_ref[...] = total
        wire_ref[...] = total.astype(BF16)

    blk = (1, tr, cols)
    out = pl.BlockSpec(blk, lambda j, t, c_ref: (j, t, 0))
    return pl.pallas_call(
        body, name=name,
        grid_spec=pltpu.PrefetchScalarGridSpec(
            num_scalar_prefetch=1, grid=(N_CHIPS, nt),
            in_specs=[pl.BlockSpec(blk, lambda j, t, c_ref: (j, c_ref[0] * nt + t, 0)), out], out_specs=[out, out]),
        out_shape=[jax.ShapeDtypeStruct(got.shape, F32), jax.ShapeDtypeStruct(got.shape, BF16)],
        compiler_params=_params("parallel", "parallel"),
    )(c, grad, got)


def _reduce_sum_first(pair, got, sel, name):
    _, _, q, cols = got.shape
    tr = _row_tile(q)
    nt = q // tr

    def body(sel_ref, p_ref, r_ref, o_ref, wire_ref):
        total = p_ref[0] + r_ref[0, 0].astype(F32)
        o_ref[0, 0] = total
        wire_ref[0, 0] = total.astype(BF16)

    blk = pl.BlockSpec((1, 1, tr, cols), lambda p, k, t, s: (p, k, t, 0))
    return pl.pallas_call(
        body, name=name,
        grid_spec=pltpu.PrefetchScalarGridSpec(
            num_scalar_prefetch=1, grid=(2, 2, nt),
            in_specs=[pl.BlockSpec((1, tr, cols), lambda p, k, t, s: (s[2 * p] + s[2 * p + 1] * k, p * nt + t, 0)), blk],
            out_specs=[blk, blk]),
        out_shape=[jax.ShapeDtypeStruct(got.shape, F32), jax.ShapeDtypeStruct(got.shape, BF16)],
        compiler_params=_params("parallel", "parallel", "parallel"),
    )(sel, pair, got)


def _reduce_sum_second(sums, got, sel, name):
    _, q, cols = got.shape
    tr = _row_tile(q)

    def body(sel_ref, s_ref, r_ref, o_ref):
        o_ref[0] = s_ref[0, 0] + r_ref[0].astype(F32)

    blk = (1, tr, cols)
    return pl.pallas_call(
        body, name=name,
        grid_spec=pltpu.PrefetchScalarGridSpec(
            num_scalar_prefetch=1, grid=(2, q // tr),
            in_specs=[pl.BlockSpec((1, 1, tr, cols), lambda p, t, s: (p, s[p], t, 0)),
                      pl.BlockSpec(blk, lambda p, t, s: (p, t, 0))],
            out_specs=pl.BlockSpec(blk, lambda p, t, s: (p, t, 0))),
        out_shape=jax.ShapeDtypeStruct(got.shape, F32), compiler_params=_params("parallel", "parallel"),
    )(sel, sums, got)


def _adamw_update(w, g, m, v):
    m2 = ADAM_B1 * m + (1.0 - ADAM_B1) * g
    v2 = ADAM_B2 * v + (1.0 - ADAM_B2) * (g * g)
    m_hat = m2 / (1.0 - ADAM_B1 ** ADAM_STEP)
    v_hat = v2 / (1.0 - ADAM_B2 ** ADAM_STEP)
    return -ADAM_LR * (m_hat / (jnp.sqrt(v_hat) + ADAM_EPS) + ADAM_WD * w), m2, v2


def _adamw_small(ws, gs, ms, vs):
    n = len(ws)

    def body(*refs):
        w, g, m, v, d, mo, vo = (refs[k * n:(k + 1) * n] for k in range(7))
        for i in range(n):
            d[i][...], mo[i][...], vo[i][...] = _adamw_update(w[i][...], g[i][...], m[i][...], v[i][...])

    shapes = [jax.ShapeDtypeStruct(t.shape, F32) for t in ws]
    outs = pl.pallas_call(body, name="adamw_small", out_shape=shapes * 3,
                          compiler_params=pltpu.CompilerParams(vmem_limit_bytes=VMEM_LIMIT_V7X))(*ws, *gs, *ms, *vs)
    return outs[:n], outs[n:2 * n], outs[2 * n:]


def _adamw_halves(w, mine, theirs, m, v, core, name):
    rows, cols = w.shape
    tr = _row_tile(rows // 2)
    per_half = rows // 2 // tr

    def body(core_ref, w_ref, a_ref, b_ref, m_ref, v_ref, g_ref, d_ref, mo_ref, vo_ref):
        g = jnp.where(pl.program_id(0) // per_half == core_ref[0], a_ref[...], b_ref[...])
        g_ref[...] = g
        d_ref[...], mo_ref[...], vo_ref[...] = _adamw_update(w_ref[...], g, m_ref[...], v_ref[...])

    blk = pl.BlockSpec((tr, cols), lambda t, c: (t, 0))
    half = lambda own: pl.BlockSpec(
        (tr, cols), lambda t, c: (jnp.clip(t - (c[0] if own else 1 - c[0]) * per_half, 0, per_half - 1), 0))
    return pl.pallas_call(
        body, name=name,
        grid_spec=pltpu.PrefetchScalarGridSpec(
            num_scalar_prefetch=1, grid=(2 * per_half,), in_specs=[blk, half(True), half(False), blk, blk], out_specs=[blk] * 4),
        out_shape=[jax.ShapeDtypeStruct(w.shape, F32)] * 4, compiler_params=_params("arbitrary"),
    )(core, w, mine, theirs, m, v)


def _pack(parts):
    flat = []
    for t in parts:
        t = t.reshape(-1).astype(F32)
        flat.append(jnp.pad(t, (0, -t.shape[0] % 128)))
    flat = jnp.concatenate(flat)
    return jnp.pad(flat, (0, -flat.shape[0] % 1024)).reshape(-1, 128)


def _unpack(buf, shapes):
    flat, out, at = buf.reshape(-1), [], 0
    for s in shapes:
        size = math.prod(s)
        out.append(flat[at:at + size].reshape(s))
        at += size + (-size % 128)
    return out


def kernel(x, ffn1_pre_g, ffn1_w_in, ffn1_w_out, ffn1_post_g, mix_pre_g, w_mix_in, a_re, a_im, log_dt, b_re, b_im, c_re, c_im, d_skip, w_glu, b_glu, w_mix_out, mix_post_g, ffn2_pre_g, ffn2_w_in, ffn2_w_out, ffn2_post_g, loss_target, m_ffn1_pre_g, m_ffn1_w_in, m_ffn1_w_out, m_ffn1_post_g, m_mix_pre_g, m_w_mix_in, m_a_re, m_a_im, m_log_dt, m_b_re, m_b_im, m_c_re, m_c_im, m_d_skip, m_w_glu, m_b_glu, m_w_mix_out, m_mix_post_g, m_ffn2_pre_g, m_ffn2_w_in, m_ffn2_w_out, m_ffn2_post_g, v_ffn1_pre_g, v_ffn1_w_in, v_ffn1_w_out, v_ffn1_post_g, v_mix_pre_g, v_w_mix_in, v_a_re, v_a_im, v_log_dt, v_b_re, v_b_im, v_c_re, v_c_im, v_d_skip, v_w_glu, v_b_glu, v_w_mix_out, v_mix_post_g, v_ffn2_pre_g, v_ffn2_w_in, v_ffn2_w_out, v_ffn2_post_g):
    given = dict(locals())
    order = ("ffn1_pre_g", "ffn1_w_in", "ffn1_w_out", "ffn1_post_g", "mix_pre_g", "w_mix_in", "a_re", "a_im", "log_dt",
             "b_re", "b_im", "c_re", "c_im", "d_skip", "w_glu", "b_glu", "w_mix_out", "mix_post_g", "ffn2_pre_g",
             "ffn2_w_in", "ffn2_w_out", "ffn2_post_g")
    at_x, at_y, at_c = (lax.axis_index(a).astype(jnp.int32) for a in ("x", "y", "c"))
    place = dict(core=at_c.reshape(1), sel_first=jnp.stack([2 * at_x, jnp.int32(1), at_y, jnp.int32(2)]),
                 sel_second=jnp.stack([at_y, at_x]))

    shards = {n: given[n][0] for n in BIG}
    w = {n: shards[n].astype(BF16) for n in REST}
    w.update(zip(FIRST, _gather_weights([shards[n].astype(BF16) for n in FIRST]).run("gather_first")))
    small = {n: given[n][0] for n in SMALL}
    loss_rows, grad_x, g = _local_step(x[0], loss_target[0], small, w, place)

    total = _allreduce_small(_pack([g[n] for n in SMALL] + [loss_rows[0, :1]]))
    parts = _unpack(total, [small[n].shape for n in SMALL] + [(1,)])
    grads = dict(zip(SMALL, parts[:-1]))
    loss = parts[-1][0]

    delta, new_m, new_v = {}, {}, {}
    for n in BIG:
        grads[n], delta[n], new_m[n], new_v[n] = _adamw_halves(
            shards[n], *g[n], given["m_" + n][0], given["v_" + n][0], place["core"], name=f"adamw_{n}")
    take = lambda pre: [given[pre + n] for n in SMALL]
    outs = _adamw_small(take(""), [grads[n][None] for n in SMALL], take("m_"), take("v_"))
    for store, arrays in zip((delta, new_m, new_v), outs):
        store.update({n: t[0] for n, t in zip(SMALL, arrays)})

    lead = lambda d: [d[n][None] for n in order]
    return (loss, grad_x[None], *lead(grads), *lead(delta), *lead(new_m), *lead(new_v))
```

```python
import functools
import math

import jax
import jax.numpy as jnp
from jax import lax
from jax.experimental import pallas as pl
from jax.experimental.pallas import tpu as pltpu

F32, BF16 = jnp.float32, jnp.bfloat16

D_MODEL = 1024
D_FF = 2816
N_CHIPS = 4
FF_BLK = 2 * D_FF // N_CHIPS
ATTN_W = 512
SSM_W = 512
HEAD_DIM = 64
N_HEADS = ATTN_W // HEAD_DIM
DILATIONS = (1, 4, 16)
N_BACK = 128
QBLK = 128
N_GROUPS = 32
GROUP_CH = 16
STATE = 64
N_STATE = N_GROUPS * STATE
EPS = 1e-6
NEG = -1e30
GELU_C = math.sqrt(2.0 / math.pi)

ADAM_LR, ADAM_B1, ADAM_B2, ADAM_EPS, ADAM_WD, ADAM_STEP = 0.001, 0.9, 0.999, 1e-08, 0.01, 10

VMEM_LIMIT_V7X = 60 * 1024 * 1024
ROW_TILE = 512
FFN_ROW_TILE = 512
DW_ROW_TILE = 1024


def _params(*sem):
    return pltpu.CompilerParams(dimension_semantics=sem, vmem_limit_bytes=VMEM_LIMIT_V7X)


def _dot(a, b):
    return jnp.dot(a.astype(BF16), b.astype(BF16), preferred_element_type=F32)


def _dot_nt(a, b):
    return lax.dot_general(a.astype(BF16), b.astype(BF16), (((1,), (1,)), ((), ())), preferred_element_type=F32)


def _dot_tn(a, b):
    return lax.dot_general(a.astype(BF16), b.astype(BF16), (((0,), (0,)), ((), ())), preferred_element_type=F32)


def _full(shape):
    return pl.BlockSpec(shape, lambda *_: (0,) * len(shape))


def _rows(tm, width):
    return pl.BlockSpec((tm, width), lambda i: (i, 0))


ANY = pl.BlockSpec(memory_space=pl.ANY)


def _load_once(pairs, sems):
    copies = [pltpu.make_async_copy(src, dst, sems.at[k]) for k, (src, dst) in enumerate(pairs)]
    for c in copies:
        c.start()
    for c in copies:
        c.wait()


def _rms(x):
    return lax.rsqrt(jnp.mean(x * x, axis=-1, keepdims=True) + EPS)


def _rms_bwd(dy_g, xn, r):
    return r * (dy_g - xn * jnp.mean(dy_g * xn, axis=-1, keepdims=True))


def _ffn_fwd(x, g_pre, w_in, w_out, g_post, target, *, name, rider=None):
    T = x.shape[0]
    tm = FFN_ROW_TILE
    with_loss = target is not None

    def body(*refs):
        if with_loss:
            x_ref, gpre_ref, gpost_ref, tgt_ref, win_hbm, wout_hbm, o_ref, loss_ref, z_ref, f_ref, win_v, wout_v, sems = refs
        else:
            x_ref, gpre_ref, gpost_ref, win_hbm, wout_hbm, o_ref, z_ref, f_ref, win_v, wout_v, sems = refs

        @pl.when(pl.program_id(0) == 0)
        def _():
            _load_once([(win_hbm, win_v), (wout_hbm, wout_v)], sems)
            if with_loss:
                loss_ref[...] = jnp.zeros_like(loss_ref)

        xv = x_ref[...]
        h = (xv * _rms(xv) * gpre_ref[...]).astype(BF16)
        f = jnp.zeros((tm, D_MODEL), F32)
        for k in range(2):
            gate = _dot(h, win_v[k])
            up = _dot(h, win_v[k + 2])
            z_ref[:, k * FF_BLK:(k + 1) * FF_BLK] = gate.astype(BF16)
            z_ref[:, D_FF + k * FF_BLK:D_FF + (k + 1) * FF_BLK] = up.astype(BF16)
            f = f + _dot(gate * jax.nn.sigmoid(gate) * up, wout_v[k])
        f_ref[...] = f
        out = xv + 0.5 * (f * _rms(f) * gpost_ref[...])
        if with_loss:
            err = out - tgt_ref[...]
            o_ref[...] = err * (1.0 / D_MODEL)
            loss_ref[...] += jnp.sum(err * err) * (0.5 / D_MODEL)
        else:
            o_ref[...] = out

    row = _rows(tm, D_MODEL)
    vec = _full((1, D_MODEL))
    in_specs = [row, vec, vec] + ([row] if with_loss else []) + [ANY, ANY]
    out_shape = [jax.ShapeDtypeStruct((T, D_MODEL), F32)]
    out_specs = [row]
    if with_loss:
        out_shape.append(jax.ShapeDtypeStruct((8, 128), F32))
        out_specs.append(_full((8, 128)))
    out_shape += [jax.ShapeDtypeStruct((T, 2 * D_FF), BF16), jax.ShapeDtypeStruct((T, D_MODEL), F32)]
    out_specs += [_rows(tm, 2 * D_FF), row]
    args = (x, g_pre, g_post) + ((target,) if with_loss else ()) + (w_in, w_out)
    return _pallas(
        body, name=name, grid=(T // tm,), in_specs=in_specs, out_specs=out_specs, out_shape=out_shape,
        scratch_shapes=[pltpu.VMEM(w_in.shape, BF16), pltpu.VMEM(w_out.shape, BF16), pltpu.SemaphoreType.DMA((2,))],
        semantics=("arbitrary",), args=args, rider=rider)


def _ffn_bwd_out(dout, f, z, w_out, g_post, *, name):
    T = dout.shape[0]
    tm = FFN_ROW_TILE
    nt = T // tm

    def body(dout_ref, f_ref, z_ref, gpost_ref, wout_hbm, dz_ref, dgpost_ref, dwout_hbm, wout_v, dwout_v, sems):
        i = pl.program_id(0)

        @pl.when(i == 0)
        def _():
            _load_once([(wout_hbm, wout_v)], sems)
            dwout_v[...] = jnp.zeros_like(dwout_v)
            dgpost_ref[...] = jnp.zeros_like(dgpost_ref)

        dy = 0.5 * dout_ref[...]
        f = f_ref[...]
        r = _rms(f)
        fn = f * r
        dgpost_ref[...] += jnp.sum(dy * fn, axis=0, keepdims=True)
        df = _rms_bwd(dy * gpost_ref[...], fn, r).astype(BF16)
        for k in range(2):
            gate = z_ref[:, k * FF_BLK:(k + 1) * FF_BLK].astype(F32)
            up = z_ref[:, D_FF + k * FF_BLK:D_FF + (k + 1) * FF_BLK].astype(F32)
            sg = jax.nn.sigmoid(gate)
            silu = gate * sg
            dwout_v[k] += _dot_tn(silu * up, df)
            da = _dot_nt(df, wout_v[k])
            dz_ref[:, k * FF_BLK:(k + 1) * FF_BLK] = (da * up * (sg * (1.0 + gate * (1.0 - sg)))).astype(BF16)
            dz_ref[:, D_FF + k * FF_BLK:D_FF + (k + 1) * FF_BLK] = (da * silu).astype(BF16)

        @pl.when(i == nt - 1)
        def _():
            c = pltpu.make_async_copy(dwout_v, dwout_hbm, sems.at[0])
            c.start()
            c.wait()

    row = _rows(tm, D_MODEL)
    return pl.pallas_call(
        body, name=name, grid=(nt,),
        in_specs=[row, row, _rows(tm, 2 * D_FF), _full((1, D_MODEL)), ANY],
        out_specs=[_rows(tm, 2 * D_FF), _full((1, D_MODEL)), ANY],
        out_shape=[jax.ShapeDtypeStruct((T, 2 * D_FF), BF16), jax.ShapeDtypeStruct((1, D_MODEL), F32),
                   jax.ShapeDtypeStruct(w_out.shape, F32)],
        scratch_shapes=[pltpu.VMEM(w_out.shape, BF16), pltpu.VMEM(w_out.shape, F32), pltpu.SemaphoreType.DMA((1,))],
        compiler_params=_params("arbitrary"),
    )(dout, f, z, g_post, w_out)


def _norm_matmul_dw(dz, x, g, n_blocks, *, name, rider=None, together=1):
    T = x.shape[0]
    bw = dz.shape[1] // n_blocks
    tm = DW_ROW_TILE

    def body(dz_ref, x_ref, g_ref, dw_ref):
        @pl.when(pl.program_id(1) == 0)
        def _():
            dw_ref[...] = jnp.zeros_like(dw_ref)

        xv = x_ref[...]
        h_t = (xv * _rms(xv) * g_ref[...]).T.astype(BF16)
        for b in range(together):
            dw_ref[b] += jnp.dot(h_t, dz_ref[:, b * bw:(b + 1) * bw], preferred_element_type=F32)

    res = _pallas(
        body, name=name, grid=(n_blocks // together, T // tm),
        in_specs=[pl.BlockSpec((tm, together * bw), lambda j, t: (t, j)), pl.BlockSpec((tm, D_MODEL), lambda j, t: (t, 0)),
                  _full((1, D_MODEL))],
        out_specs=[pl.BlockSpec((together, D_MODEL, bw), lambda j, t: (j, 0, 0))],
        out_shape=[jax.ShapeDtypeStruct((n_blocks, D_MODEL, bw), F32)], semantics=("parallel", "arbitrary"),
        args=(dz, x, g), rider=rider)
    return res[0] if rider is None else (res[0][0], res[1])


def _norm_matmul_dx(dz, x, dres, g, w, *, name, rider=None, part=(0, 1), into=None):
    T = x.shape[0]
    nb, _, bw = w.shape
    tm = FFN_ROW_TILE
    first, steps = round(part[0] * T) // tm, round((part[1] - part[0]) * T) // tm

    def body(dz_ref, x_ref, dres_ref, g_ref, w_hbm, *rest):
        dx_ref, dg_ref, w_v, sems = rest[-4:]

        @pl.when(pl.program_id(0) == 0)
        def _():
            _load_once([(w_hbm, w_v)], sems)
            dg_ref[...] = jnp.zeros_like(dg_ref)

        xv = x_ref[...]
        r = _rms(xv)
        xn = xv * r
        gv = g_ref[...]
        dh = jnp.zeros((tm, D_MODEL), F32)
        for j in range(nb):
            dh = dh + _dot_nt(dz_ref[:, j * bw:(j + 1) * bw], w_v[j])
        dg_ref[...] += jnp.sum(dh * xn, axis=0, keepdims=True)
        dx_ref[...] = _rms_bwd(dh * gv, xn, r) + dres_ref[...]

    rows = lambda width: pl.BlockSpec((tm, width), lambda i: (first + i, 0))
    row = rows(D_MODEL)
    return _pallas(
        body, name=name, grid=(steps,),
        in_specs=[rows(nb * bw), row, row, _full((1, D_MODEL)), ANY] + ([] if into is None else [ANY]),
        out_specs=[row, _full((1, D_MODEL))],
        out_shape=[jax.ShapeDtypeStruct((T, D_MODEL), F32), jax.ShapeDtypeStruct((1, D_MODEL), F32)],
        scratch_shapes=[pltpu.VMEM(w.shape, BF16), pltpu.SemaphoreType.DMA((1,))],
        semantics=("arbitrary",), args=(dz, x, dres, g, w) + (() if into is None else (into,)), rider=rider,
        aliases={} if into is None else {5: 0})


def _mix_in_fwd(x, g, w):
    T = x.shape[0]
    tm = ROW_TILE

    def body(x_ref, g_ref, w_ref, q_ref, k_ref, v_ref, u_ref):
        xv = x_ref[...]
        h = (xv * _rms(xv) * g_ref[...]).astype(BF16)
        for j, o_ref in enumerate((q_ref, k_ref, v_ref, u_ref)):
            o_ref[...] = _dot(h, w_ref[j])

    col = _rows(tm, ATTN_W)
    return pl.pallas_call(
        body, name="mix_in_fwd", grid=(T // tm,),
        in_specs=[_rows(tm, D_MODEL), _full((1, D_MODEL)), _full(w.shape)],
        out_specs=[col] * 4, out_shape=[jax.ShapeDtypeStruct((T, ATTN_W), F32)] * 4,
        compiler_params=_params("parallel"),
    )(x, g, w)


def _mix_out_fwd(x, outs, lses, ssm, w, g):
    T = x.shape[0]
    tm = ROW_TILE

    def body(x_ref, o1, o2, o3, l1, l2, l3, s_ref, w_ref, g_ref, o_ref, m_ref, a_ref, lse_ref):
        ls = [l1[...], l2[...], l3[...]]
        top = jnp.maximum(jnp.maximum(ls[0], ls[1]), ls[2])
        lse = top + jnp.log(jnp.exp(ls[0] - top) + jnp.exp(ls[1] - top) + jnp.exp(ls[2] - top))
        attn = jnp.exp(ls[0] - lse) * o1[...] + jnp.exp(ls[1] - lse) * o2[...] + jnp.exp(ls[2] - lse) * o3[...]
        lse_ref[...] = lse
        a_ref[...] = attn
        mixed = _dot(attn, w_ref[0]) + _dot(s_ref[...], w_ref[1])
        m_ref[...] = mixed
        o_ref[...] = x_ref[...] + mixed * _rms(mixed) * g_ref[...]

    row, col = _rows(tm, D_MODEL), _rows(tm, ATTN_W)
    return pl.pallas_call(
        body, name="mix_out_fwd", grid=(T // tm,),
        in_specs=[row] + [col] * 7 + [_full(w.shape), _full((1, D_MODEL))],
        out_specs=[row, row, col, col],
        out_shape=[jax.ShapeDtypeStruct((T, D_MODEL), F32)] * 2 + [jax.ShapeDtypeStruct((T, ATTN_W), F32)] * 2,
        compiler_params=_params("parallel"),
    )(x, *outs, *lses, ssm, w, g)


def _mix_out_bwd(dout, mixed, attn, ssm, w, g, rider=None):
    T = dout.shape[0]
    tm = ROW_TILE

    def body(dout_ref, m_ref, a_ref, s_ref, w_ref, g_ref, da_ref, ds_ref, dw_ref, dg_ref):
        @pl.when(pl.program_id(0) == 0)
        def _():
            dw_ref[...] = jnp.zeros_like(dw_ref)
            dg_ref[...] = jnp.zeros_like(dg_ref)

        dy = dout_ref[...]
        mixed = m_ref[...]
        r = _rms(mixed)
        mn = mixed * r
        dg_ref[...] += jnp.sum(dy * mn, axis=0, keepdims=True)
        dm = _rms_bwd(dy * g_ref[...], mn, r).astype(BF16)
        da_ref[...] = _dot_nt(dm, w_ref[0])
        ds_ref[...] = _dot_nt(dm, w_ref[1])
        dw_ref[0] += _dot_tn(a_ref[...], dm)
        dw_ref[1] += _dot_tn(s_ref[...], dm)

    row, col = _rows(tm, D_MODEL), _rows(tm, ATTN_W)
    return _pallas(
        body, name="mix_out_bwd", grid=(T // tm,),
        in_specs=[row, row, col, col, _full(w.shape), _full((1, D_MODEL))],
        out_specs=[col, col, _full(w.shape), _full((1, D_MODEL))],
        out_shape=[jax.ShapeDtypeStruct((T, ATTN_W), F32)] * 2
        + [jax.ShapeDtypeStruct(w.shape, F32), jax.ShapeDtypeStruct((1, D_MODEL), F32)],
        semantics=("arbitrary",), args=(dout, mixed, attn, ssm, w, g), rider=rider)


ATTN_TILING = {1: (ATTN_W, 1), 4: (2 * HEAD_DIM, 4), 16: (2 * HEAD_DIM, 4)}


def _class_rows(d, r):
    return (pl.ds(r, QBLK, stride=d), slice(None)) if d > 1 else (slice(None), slice(None))


def _for_class_groups(d, group, fn):
    if d == group:
        fn(0)
    else:
        lax.fori_loop(0, d // group, lambda n, carry: (fn(n * group), carry)[1], 0)


def _block_slopes(lanes, lane_block):
    heads = lanes // HEAD_DIM
    first = lane_block * heads
    return [jnp.exp2(-jnp.full((1, 1), first + hh + 1, jnp.int32).astype(F32)) for hh in range(heads)]


def _attn_specs(d, nb, lanes):
    blk = (QBLK * d, lanes)
    cur = pl.BlockSpec(blk, lambda j, lb: (j, lb))
    prev = pl.BlockSpec(blk, lambda j, lb: (jnp.maximum(j - 1, 0), lb))
    nxt = pl.BlockSpec(blk, lambda j, lb: (jnp.minimum(j + 1, nb - 1), lb))
    return cur, prev, nxt


def _attn_branch_fwd(q, k, v, d):
    T = q.shape[0]
    nb = T // (d * QBLK)
    lanes, group = ATTN_TILING[d]
    scale = HEAD_DIM ** -0.5

    def body(q_ref, kc_ref, kp_ref, vc_ref, vp_ref, o_ref, l_ref, q_s, kk_s, vv_s, o_s, l_s):
        j = pl.program_id(0)
        qi = lax.broadcasted_iota(jnp.int32, (QBLK, 2 * QBLK), 0)
        ci = lax.broadcasted_iota(jnp.int32, (QBLK, 2 * QBLK), 1)
        steps = QBLK + qi - ci
        valid = (steps >= 0) & (steps <= N_BACK) & ((ci >= QBLK) | (j > 0))
        dist = (steps * d).astype(F32)
        bias = [jnp.where(valid, -slope * dist, NEG) for slope in _block_slopes(lanes, pl.program_id(1))]

        def classes(first):
            for n in range(group):
                rows = _class_rows(d, first + n)
                q_s[n] = q_ref[rows]
                kk_s[n, :QBLK], kk_s[n, QBLK:] = kp_ref[rows], kc_ref[rows]
                vv_s[n, :QBLK], vv_s[n, QBLK:] = vp_ref[rows], vc_ref[rows]
            for n in range(group):
                for hh in range(lanes // HEAD_DIM):
                    sl = slice(hh * HEAD_DIM, (hh + 1) * HEAD_DIM)
                    s = _dot_nt(q_s[n, :, sl], kk_s[n, :, sl]) * scale + bias[hh]
                    m = jnp.max(s, axis=-1, keepdims=True)
                    p = jnp.exp(s - m)
                    den = jnp.sum(p, axis=-1, keepdims=True)
                    o_s[n, :, sl] = _dot(p, vv_s[n, :, sl]) / den
                    l_s[n, :, sl] = jnp.broadcast_to(m + jnp.log(den), (QBLK, HEAD_DIM))
            for n in range(group):
                rows = _class_rows(d, first + n)
                o_ref[rows] = o_s[n]
                l_ref[rows] = l_s[n]

        _for_class_groups(d, group, classes)

    cur, prev, _ = _attn_specs(d, nb, lanes)
    shape = jax.ShapeDtypeStruct((T, ATTN_W), F32)
    one, two = pltpu.VMEM((group, QBLK, lanes), F32), pltpu.VMEM((group, 2 * QBLK, lanes), F32)
    return pl.pallas_call(
        body, name=f"attn_fwd_d{d}", grid=(nb, ATTN_W // lanes),
        in_specs=[cur, cur, prev, cur, prev], out_specs=[cur, cur], out_shape=[shape, shape],
        scratch_shapes=[one, two, two, one, one], compiler_params=_params("parallel", "parallel"),
    )(q, k, k, v, v)


def _attn_branch_bwd(q, k, v, o, lse, do, d, rider=None):
    T = q.shape[0]
    nb = T // (d * QBLK)
    lanes, group = ATTN_TILING[d]
    scale = HEAD_DIM ** -0.5

    def body(q_ref, kc_ref, kp_ref, vc_ref, vp_ref, o_ref, l_ref, do_ref, dq_ref, dk_ref, dv_ref,
             q_s, o_s, l_s, do_s, kk_s, vv_s, dq_s, dk_s, dv_s, ck_s, cv_s):
        j = pl.program_id(1)

        @pl.when(j == 0)
        def _():
            ck_s[...] = jnp.zeros_like(ck_s)
            cv_s[...] = jnp.zeros_like(cv_s)

        qi = lax.broadcasted_iota(jnp.int32, (QBLK, 2 * QBLK), 0)
        ci = lax.broadcasted_iota(jnp.int32, (QBLK, 2 * QBLK), 1)
        steps = QBLK + qi - ci
        valid = (steps >= 0) & (steps <= N_BACK) & ((ci >= QBLK) | (j > 0))
        dist = (steps * d).astype(F32)
        lo, hi = slice(0, QBLK), slice(QBLK, 2 * QBLK)
        bias = [jnp.where(valid, -slope * dist, NEG) for slope in _block_slopes(lanes, pl.program_id(0))]

        def classes(first):
            for n in range(group):
                rows = _class_rows(d, first + n)
                q_s[n], o_s[n], l_s[n], do_s[n] = q_ref[rows], o_ref[rows], l_ref[rows], do_ref[rows]
                kk_s[n, lo], kk_s[n, hi] = kp_ref[rows], kc_ref[rows]
                vv_s[n, lo], vv_s[n, hi] = vp_ref[rows], vc_ref[rows]
            for n in range(group):
                for hh in range(lanes // HEAD_DIM):
                    sl = slice(hh * HEAD_DIM, (hh + 1) * HEAD_DIM)
                    qh, doh, kk, vv = q_s[n, :, sl], do_s[n, :, sl], kk_s[n, :, sl], vv_s[n, :, sl]
                    delta = jnp.sum(doh * o_s[n, :, sl], axis=-1, keepdims=True)
                    s = _dot_nt(qh, kk) * scale + bias[hh]
                    p = jnp.exp(s - l_s[n, :, hh * HEAD_DIM:hh * HEAD_DIM + 1])
                    ds = p * (_dot_nt(doh, vv) - delta)
                    dq_s[n, :, sl] = _dot(ds, kk) * scale
                    dkk = _dot_tn(ds, qh) * scale
                    dvv = _dot_tn(p, doh)
                    dk_s[n, :, sl] = ck_s[first + n, :, sl] + dkk[lo]
                    dv_s[n, :, sl] = cv_s[first + n, :, sl] + dvv[lo]
                    ck_s[first + n, :, sl] = dkk[hi]
                    cv_s[first + n, :, sl] = dvv[hi]
            for n in range(group):
                rows = _class_rows(d, first + n)
                dq_ref[rows] = dq_s[n]
                dk_ref[rows] = dk_s[n]
                dv_ref[rows] = dv_s[n]

        @pl.when(j < nb)
        def _():
            _for_class_groups(d, group, classes)

        @pl.when(j == nb)
        def _():
            for r in range(d):
                dk_ref[_class_rows(d, r)] = ck_s[r]
                dv_ref[_class_rows(d, r)] = cv_s[r]

    blk = (QBLK * d, lanes)
    here = lambda j: jnp.minimum(j, nb - 1)
    cur = pl.BlockSpec(blk, lambda lb, j: (here(j), lb))
    prev = pl.BlockSpec(blk, lambda lb, j: (jnp.maximum(here(j) - 1, 0), lb))
    behind = pl.BlockSpec(blk, lambda lb, j: (jnp.maximum(j - 1, 0), lb))
    shape = jax.ShapeDtypeStruct((T, ATTN_W), F32)
    one, two = pltpu.VMEM((group, QBLK, lanes), F32), pltpu.VMEM((group, 2 * QBLK, lanes), F32)
    carry = pltpu.VMEM((d, QBLK, lanes), F32)
    return _pallas(
        body, name=f"attn_bwd_d{d}", grid=(ATTN_W // lanes, nb + 1),
        in_specs=[cur, cur, prev, cur, prev, cur, cur, cur], out_specs=[cur, behind, behind], out_shape=[shape] * 3,
        scratch_shapes=[one] * 4 + [two] * 2 + [one] * 3 + [carry] * 2,
        semantics=("parallel", "arbitrary"), args=(q, k, k, v, v, o, lse, do), rider=rider)


def _mix_in_dw(dqs, dks, dvs, du, x, g):
    T = x.shape[0]
    tm = ROW_TILE

    def body(*refs):
        x_ref, g_ref, dz_ref, dw_ref = refs[10:]

        @pl.when(pl.program_id(0) == 0)
        def _():
            dw_ref[...] = jnp.zeros_like(dw_ref)

        xv = x_ref[...]
        h_t = (xv * _rms(xv) * g_ref[...]).T.astype(BF16)
        for part in range(4):
            a, b, c = refs[3 * part:3 * part + 3] if part < 3 else (refs[9], None, None)
            dz = (a[...] if b is None else a[...] + b[...] + c[...]).astype(BF16)
            dz_ref[:, part * ATTN_W:(part + 1) * ATTN_W] = dz
            dw_ref[part] += jnp.dot(h_t, dz, preferred_element_type=F32)

    col = _rows(tm, ATTN_W)
    return pl.pallas_call(
        body, name="mix_bwd_dw", grid=(T // tm,),
        in_specs=[col] * 10 + [_rows(tm, D_MODEL), _full((1, D_MODEL))],
        out_specs=[_rows(tm, 4 * ATTN_W), _full((4, D_MODEL, ATTN_W))],
        out_shape=[jax.ShapeDtypeStruct((T, 4 * ATTN_W), BF16), jax.ShapeDtypeStruct((4, D_MODEL, ATTN_W), F32)],
        compiler_params=_params("arbitrary"),
    )(*dqs, *dks, *dvs, du, x, g)


SCAN_ROWS = 8
SCAN_LANES = 512
SSM_CHUNK = 512
SSM_CHUNK_BWD = 512
SSM_HALVES = tuple((slice(h * SSM_W // 2, (h + 1) * SSM_W // 2), slice(h * N_STATE // 2, (h + 1) * N_STATE // 2)) for h in range(2))


def _cmul(ar, ai, br, bi):
    return ar * br - ai * bi, ar * bi + ai * br


def _ssm_discretize(a_re, a_im, log_dt, b_re, b_im):
    def body(ar_ref, ai_ref, ldt_ref, br_ref, bi_ref, abr_ref, abi_ref, er_ref, ei_ref, bbr_ref, bbi_ref, pr_ref, pi_ref):
        ar, ai = ar_ref[...], ai_ref[...]
        dt = jnp.exp(ldt_ref[...])
        n = lax.broadcasted_iota(jnp.int32, (1, SCAN_ROWS), 1).astype(F32) + 1.0
        mag, ang = jnp.exp(dt * ar), dt * ai
        abr, abi = mag * jnp.cos(ang), mag * jnp.sin(ang)
        abr_ref[...], abi_ref[...] = abr, abi
        pr_ref[...] = jnp.exp(dt * ar * n) * jnp.cos(ang * n)
        pi_ref[...] = jnp.exp(dt * ar * n) * jnp.sin(ang * n)
        den = ar * ar + ai * ai
        er = ((abr - 1.0) * ar + abi * ai) / den
        ei = (abi * ar - (abr - 1.0) * ai) / den
        er_ref[...], ei_ref[...] = er, ei
        bbr_ref[...], bbi_ref[...] = _cmul(er, ei, br_ref[...], bi_ref[...])

    col = jax.ShapeDtypeStruct((N_STATE, 1), F32)
    mat = jax.ShapeDtypeStruct((N_STATE, GROUP_CH), F32)
    pw = jax.ShapeDtypeStruct((N_STATE, SCAN_ROWS), F32)
    return pl.pallas_call(body, name="ssm_discretize", out_shape=[col] * 4 + [mat] * 2 + [pw] * 2)(
        a_re, a_im, log_dt, b_re, b_im)


def _ssm_discretize_bwd(a_re, a_im, log_dt, b_re, b_im, ab_re, ab_im, e_re, e_im, dab_re, dab_im, dbb_re, dbb_im):
    def body(ar_ref, ai_ref, ldt_ref, br_ref, bi_ref, abr_ref, abi_ref, er_ref, ei_ref, dabr_ref, dabi_ref,
             dbbr_ref, dbbi_ref, dar_ref, dai_ref, ddt_ref, dbr_ref, dbi_ref):
        ar, ai, dt = ar_ref[...], ai_ref[...], jnp.exp(ldt_ref[...])
        er, ei = er_ref[...], ei_ref[...]
        gbr, gbi = dbbr_ref[...], dbbi_ref[...]
        dbr_ref[...], dbi_ref[...] = _cmul(er, -ei, gbr, gbi)
        br, bi = br_ref[...], bi_ref[...]
        der = jnp.sum(br * gbr + bi * gbi, axis=-1, keepdims=True)
        dei = jnp.sum(br * gbi - bi * gbr, axis=-1, keepdims=True)
        den = ar * ar + ai * ai
        inv_r, inv_i = ar / den, -ai / den
        t_r, t_i = _cmul(der, dei, inv_r, -inv_i)
        gab_r, gab_i = dabr_ref[...] + t_r, dabi_ref[...] + t_i
        q_r, q_i = _cmul(er, ei, inv_r, inv_i)
        dl_r, dl_i = _cmul(der, dei, q_r, -q_i)
        dl_r, dl_i = -dl_r, -dl_i
        gw_r, gw_i = _cmul(gab_r, gab_i, abr_ref[...], -abi_ref[...])
        dar_ref[...] = dl_r + dt * gw_r
        dai_ref[...] = dl_i + dt * gw_i
        ddt_ref[...] = (gw_r * ar + gw_i * ai) * dt

    col = jax.ShapeDtypeStruct((N_STATE, 1), F32)
    mat = jax.ShapeDtypeStruct((N_STATE, GROUP_CH), F32)
    return pl.pallas_call(body, name="ssm_discretize_bwd", out_shape=[col] * 3 + [mat] * 2)(
        a_re, a_im, log_dt, b_re, b_im, ab_re, ab_im, e_re, e_im, dab_re, dab_im, dbb_re, dbb_im)


def _scan_tables(p_re, p_im, reverse):
    pr, pi = p_re.T, p_im.T
    if reverse:
        pi = -pi
    row = jnp.arange(SCAN_ROWS)[:, None]
    level = lambda t, s: jnp.where((row < SCAN_ROWS - s) if reverse else (row >= s), t[s - 1][None, :], 0.0)
    carry = (pr[::-1], pi[::-1]) if reverse else (pr, pi)
    return jnp.stack([level(pr, 1), level(pi, 1), level(pr, 2), level(pi, 2), level(pr, 4), level(pi, 4), carry[0], carry[1]])


def _scan_group(xr, xi, tab_ref, ls, carry_r, carry_i, reverse):
    for n, s in enumerate((1, 2, 4)):
        shift = SCAN_ROWS - s if reverse else s
        mr, mi = _cmul(tab_ref[2 * n, :, ls], tab_ref[2 * n + 1, :, ls], pltpu.roll(xr, shift, 0), pltpu.roll(xi, shift, 0))
        xr, xi = xr + mr, xi + mi
    mr, mi = _cmul(tab_ref[6, :, ls], tab_ref[7, :, ls], carry_r, carry_i)
    return xr + mr, xi + mi


def _gelu(y):
    t = jnp.tanh(GELU_C * (y + 0.044715 * y * y * y))
    return 0.5 * y * (1.0 + t), t


def _ssm_fwd(u, tab, bd_re, bd_im, cd_re, cd_im, d_skip, w_glu, b_glu):
    T = u.shape[0]
    tc = SSM_CHUNK

    def body(u_ref, tab_ref, bdr_ref, bdi_ref, cdr_ref, cdi_ref, dsk_ref, wg_ref, bg_ref,
             sr_ref, si_ref, yp_ref, o_ref, car_r, car_i):
        @pl.when(pl.program_id(0) == 0)
        def _():
            car_r[...] = jnp.zeros_like(car_r)
            car_i[...] = jnp.zeros_like(car_i)

        uv = u_ref[...]
        for cs, ss in SSM_HALVES:
            sr_ref[:, ss] = _dot(uv[:, cs], bdr_ref[cs, ss])
            si_ref[:, ss] = _dot(uv[:, cs], bdi_ref[cs, ss])
        for lb in range(N_STATE // SCAN_LANES):
            ls = pl.ds(lb * SCAN_LANES, SCAN_LANES)

            def step(g, carry):
                rows = pl.ds(pl.multiple_of(g * SCAN_ROWS, SCAN_ROWS), SCAN_ROWS)
                xr, xi = _scan_group(sr_ref[rows, ls], si_ref[rows, ls], tab_ref, ls, carry[0], carry[1], False)
                sr_ref[rows, ls] = xr
                si_ref[rows, ls] = xi
                last = slice(SCAN_ROWS - 1, SCAN_ROWS)
                return (jnp.broadcast_to(xr[last], xr.shape), jnp.broadcast_to(xi[last], xi.shape))

            cr, ci = lax.fori_loop(0, tc // SCAN_ROWS, step, (car_r[:, ls], car_i[:, ls]))
            car_r[:, ls] = cr
            car_i[:, ls] = ci
        y = jnp.concatenate([_dot(sr_ref[:, ss], cdr_ref[ss, cs]) - _dot(si_ref[:, ss], cdi_ref[ss, cs])
                             for cs, ss in SSM_HALVES], axis=1) + dsk_ref[...] * uv
        yp_ref[...] = y
        gy, _ = _gelu(y)
        o_ref[...] = gy * jax.nn.sigmoid(_dot(gy, wg_ref[...]) + bg_ref[...])

    col, st = _rows(tc, SSM_W), _rows(tc, N_STATE)
    vec = _full((1, SSM_W))
    return pl.pallas_call(
        body, name="ssm_fwd", grid=(T // tc,),
        in_specs=[col, _full(tab.shape), _full(bd_re.shape), _full(bd_im.shape), _full(cd_re.shape), _full(cd_im.shape),
                  vec, _full(w_glu.shape), vec],
        out_specs=[st, st, col, col],
        out_shape=[jax.ShapeDtypeStruct((T, N_STATE), F32)] * 2 + [jax.ShapeDtypeStruct((T, SSM_W), F32)] * 2,
        scratch_shapes=[pltpu.VMEM((SCAN_ROWS, N_STATE), F32)] * 2,
        compiler_params=_params("arbitrary"),
    )(u, tab, bd_re, bd_im, cd_re, cd_im, d_skip, w_glu, b_glu)


def _ssm_bwd(dout, u, yp, s_re, s_im, tab, bd_re, bd_im, cd_re, cd_im, d_skip, w_glu, b_glu, rider=None):
    T = u.shape[0]
    tc = SSM_CHUNK_BWD
    nt = T // tc
    rows_per_chunk = tc // SCAN_ROWS

    def body(do_ref, u_ref, yp_ref, sr_ref, si_ref, pr_ref, pi_ref, tab_ref, dsk_ref, wg_ref, bg_ref,
             bdr_hbm, bdi_hbm, cdr_hbm, cdi_hbm,
             du_ref, dsk_out, dbg_out, dwg_out, da_out, dbdr_out, dbdi_out, dcdr_out, dcdi_out,
             bdr_v, bdi_v, cdr_v, cdi_v, dbdr_v, dbdi_v, dcdr_v, dcdi_v, gr_v, gi_v, car_r, car_i, sems):
        i = pl.program_id(0)

        @pl.when(i == 0)
        def _():
            _load_once([(hbm.at[(ss, cs) if by_state else (cs, ss)], vmem.at[h])
                        for hbm, vmem, by_state in ((bdr_hbm, bdr_v, False), (bdi_hbm, bdi_v, False),
                                                    (cdr_hbm, cdr_v, True), (cdi_hbm, cdi_v, True))
                        for h, (cs, ss) in enumerate(SSM_HALVES)], sems)
            for ref in (dbdr_v, dbdi_v, dcdr_v, dcdi_v, car_r, car_i, dsk_out, dbg_out, dwg_out, da_out):
                ref[...] = jnp.zeros_like(ref)

        uv, y, dout_v = u_ref[...], yp_ref[...], do_ref[...]
        gy, t = _gelu(y)
        sg = jax.nn.sigmoid(_dot(gy, wg_ref[...]) + bg_ref[...])
        dzg = dout_v * gy * sg * (1.0 - sg)
        dgy = dout_v * sg + _dot_nt(dzg, wg_ref[...])
        dwg_out[...] += _dot_tn(gy, dzg)
        dbg_out[...] += jnp.sum(dzg, axis=0, keepdims=True)
        dy = dgy * (0.5 * (1.0 + t) + 0.5 * y * (1.0 - t * t) * GELU_C * (1.0 + 3 * 0.044715 * y * y))
        dsk_out[...] += jnp.sum(dy * uv, axis=0, keepdims=True)

        for h, (cs, ss) in enumerate(SSM_HALVES):
            gr_v[:, ss] = _dot_nt(dy[:, cs], cdr_v[h])
            gi_v[:, ss] = -_dot_nt(dy[:, cs], cdi_v[h])
            dcdr_v[h] += _dot_tn(sr_ref[:, ss], dy[:, cs])
            dcdi_v[h] -= _dot_tn(si_ref[:, ss], dy[:, cs])

        row = lax.broadcasted_iota(jnp.int32, (SCAN_ROWS, SCAN_LANES), 0)
        first_chunk = i == nt - 1
        for lb in range(N_STATE // SCAN_LANES):
            ls = pl.ds(lb * SCAN_LANES, SCAN_LANES)

            def step(n, carry):
                g = rows_per_chunk - 1 - n
                rows = pl.ds(pl.multiple_of(g * SCAN_ROWS, SCAN_ROWS), SCAN_ROWS)
                before = pl.ds(pl.multiple_of(jnp.maximum(g - 1, 0) * SCAN_ROWS, SCAN_ROWS), SCAN_ROWS)
                xr, xi = _scan_group(gr_v[rows, ls], gi_v[rows, ls], tab_ref, ls, carry[0], carry[1], True)
                gr_v[rows, ls] = xr
                gi_v[rows, ls] = xi
                last = slice(SCAN_ROWS - 1, SCAN_ROWS)
                edge_r = jnp.where(g > 0, sr_ref[before, ls][last], jnp.where(first_chunk, 0.0, pr_ref[:, ls][last]))
                edge_i = jnp.where(g > 0, si_ref[before, ls][last], jnp.where(first_chunk, 0.0, pi_ref[:, ls][last]))
                spr = jnp.where(row >= 1, pltpu.roll(sr_ref[rows, ls], 1, 0), edge_r)
                spi = jnp.where(row >= 1, pltpu.roll(si_ref[rows, ls], 1, 0), edge_i)
                first = slice(0, 1)
                return (jnp.broadcast_to(xr[first], xr.shape), jnp.broadcast_to(xi[first], xi.shape),
                        carry[2] + xr * spr + xi * spi, carry[3] + xi * spr - xr * spi)

            zero = jnp.zeros((SCAN_ROWS, SCAN_LANES), F32)
            cr, ci, dar, dai = lax.fori_loop(0, rows_per_chunk, step, (car_r[:, ls], car_i[:, ls], zero, zero))
            car_r[:, ls] = cr
            car_i[:, ls] = ci
            da_out[0, :, ls] += dar
            da_out[1, :, ls] += dai

        du_ref[...] = dsk_ref[...] * dy + jnp.concatenate(
            [_dot_nt(gr_v[:, ss], bdr_v[h]) + _dot_nt(gi_v[:, ss], bdi_v[h]) for h, (_, ss) in enumerate(SSM_HALVES)], axis=1)
        for h, (cs, ss) in enumerate(SSM_HALVES):
            dbdr_v[h] += _dot_tn(uv[:, cs], gr_v[:, ss])
            dbdi_v[h] += _dot_tn(uv[:, cs], gi_v[:, ss])

        @pl.when(i == nt - 1)
        def _():
            per_half = N_GROUPS // len(SSM_HALVES)
            for grp in range(N_GROUPS):
                h, at = divmod(grp, per_half)
                ch, stt = pl.ds(at * GROUP_CH, GROUP_CH), pl.ds(at * STATE, STATE)
                dbdr_out[grp], dbdi_out[grp] = dbdr_v[h, ch, stt], dbdi_v[h, ch, stt]
                dcdr_out[grp], dcdi_out[grp] = dcdr_v[h, stt, ch], dcdi_v[h, stt, ch]

    rev = lambda i: (nt - 1 - i, 0)
    col = pl.BlockSpec((tc, SSM_W), rev)
    st = pl.BlockSpec((tc, N_STATE), rev)
    st_before = pl.BlockSpec((SCAN_ROWS, N_STATE), lambda i: (jnp.maximum((nt - 1 - i) * rows_per_chunk - 1, 0), 0))
    vec = _full((1, SSM_W))
    bd_blocks, cd_blocks = (N_GROUPS, GROUP_CH, STATE), (N_GROUPS, STATE, GROUP_CH)
    bd, cd = jax.ShapeDtypeStruct(bd_blocks, F32), jax.ShapeDtypeStruct(cd_blocks, F32)
    n_half = len(SSM_HALVES)
    bd_half, cd_half = (n_half, SSM_W // n_half, N_STATE // n_half), (n_half, N_STATE // n_half, SSM_W // n_half)
    return _pallas(
        body, name="ssm_bwd", grid=(nt,),
        in_specs=[col, col, col, st, st, st_before, st_before, _full(tab.shape), vec, _full(w_glu.shape), vec,
                  ANY, ANY, ANY, ANY],
        out_specs=[col, vec, vec, _full(w_glu.shape), _full((2, SCAN_ROWS, N_STATE)),
                   _full(bd_blocks), _full(bd_blocks), _full(cd_blocks), _full(cd_blocks)],
        out_shape=[jax.ShapeDtypeStruct((T, SSM_W), F32), jax.ShapeDtypeStruct((1, SSM_W), F32),
                   jax.ShapeDtypeStruct((1, SSM_W), F32), jax.ShapeDtypeStruct(w_glu.shape, F32),
                   jax.ShapeDtypeStruct((2, SCAN_ROWS, N_STATE), F32), bd, bd, cd, cd],
        scratch_shapes=[pltpu.VMEM(bd_half, BF16)] * 2 + [pltpu.VMEM(cd_half, BF16)] * 2
        + [pltpu.VMEM(bd_half, F32)] * 2 + [pltpu.VMEM(cd_half, F32)] * 2
        + [pltpu.VMEM((tc, N_STATE), F32)] * 2 + [pltpu.VMEM((SCAN_ROWS, N_STATE), F32)] * 2
        + [pltpu.SemaphoreType.DMA((4 * len(SSM_HALVES),))],
        semantics=("arbitrary",), rider=rider,
        args=(dout, u, yp, s_re, s_im, s_re, s_im, tab, d_skip, w_glu, b_glu, bd_re, bd_im, cd_re, cd_im))


def _block_diagonals(parts):
    def body(*refs):
        for t_ref, o_ref in zip(refs[:len(parts)], refs[len(parts):]):
            _, a, b = t_ref.shape
            o_ref[...] = jnp.zeros_like(o_ref)
            for grp in range(N_GROUPS):
                o_ref[grp * a:(grp + 1) * a, grp * b:(grp + 1) * b] = t_ref[grp].astype(BF16)

    return pl.pallas_call(body, name="block_diagonals",
                          out_shape=[jax.ShapeDtypeStruct((N_GROUPS * t.shape[1], N_GROUPS * t.shape[2]), BF16) for t in parts])(*parts)


def _ssm_prepare(a_re, a_im, log_dt, b_re, b_im, c_re, c_im):
    col = lambda t: t.reshape(N_STATE, 1)
    ldt = jnp.broadcast_to(log_dt.reshape(N_GROUPS, 1), (N_GROUPS, STATE)).reshape(N_STATE, 1)
    b2r, b2i = b_re.reshape(N_STATE, GROUP_CH), b_im.reshape(N_STATE, GROUP_CH)
    ab_r, ab_i, e_r, e_i, bb_r, bb_i, p_r, p_i = _ssm_discretize(col(a_re), col(a_im), ldt, b2r, b2i)
    diag = _block_diagonals([jnp.swapaxes(t.reshape(N_GROUPS, STATE, GROUP_CH), 1, 2) for t in (bb_r, bb_i)]
                            + [jnp.swapaxes(t, 1, 2) for t in (c_re, c_im)])
    bd, cd = diag[:2], diag[2:]
    saved = dict(a_re=col(a_re), a_im=col(a_im), log_dt=ldt, b_re=b2r, b_im=b2i, ab_re=ab_r, ab_im=ab_i, e_re=e_r, e_im=e_i)
    return _scan_tables(p_r, p_i, False), _scan_tables(p_r, p_i, True), bd, cd, saved


BIG = ("ffn1_w_in", "ffn1_w_out", "w_mix_in", "w_glu", "w_mix_out", "ffn2_w_in", "ffn2_w_out")
SMALL = ("ffn1_pre_g", "ffn1_post_g", "mix_pre_g", "a_re", "a_im", "log_dt", "b_re", "b_im", "c_re", "c_im",
         "d_skip", "b_glu", "mix_post_g", "ffn2_pre_g", "ffn2_post_g")


FIRST = ("ffn1_w_in", "ffn1_w_out")
REST = ("w_mix_in", "w_glu", "w_mix_out", "ffn2_w_in", "ffn2_w_out")
LATE = ("w_mix_in", "w_glu", "w_mix_out", "ffn1_w_out")
SHARD_SHAPE = {"ffn1_w_in": (D_MODEL, FF_BLK), "ffn2_w_in": (D_MODEL, FF_BLK), "ffn1_w_out": (D_FF // N_CHIPS, D_MODEL),
               "ffn2_w_out": (D_FF // N_CHIPS, D_MODEL), "w_mix_in": (D_MODEL, ATTN_W), "w_glu": (SSM_W // N_CHIPS, SSM_W),
               "w_mix_out": (2 * ATTN_W // N_CHIPS, D_MODEL)}


class _Reduction:
    def __init__(self, names, grads, place):
        self.names, self.local, self.place = list(names), list(grads), place

    def exchange(self):
        return _pair_exchange(self.local)

    def first(self, got):
        both = [_pair_sum(a, b, self.place["core"], name=f"pair_sum_{n}") for n, a, b in zip(self.names, self.local, got)]
        self.pair = [f32 for f32, _ in both]
        return _reduce_first([wire for _, wire in both])

    def second(self, got):
        both = [_reduce_sum_first(a, b, self.place["sel_first"], name=f"sum_first_{n}")
                for n, a, b in zip(self.names, self.pair, got)]
        self.sums = [f32 for f32, _ in both]
        return _reduce_second([wire for _, wire in both])

    def swap(self, got):
        self.halves = [_reduce_sum_second(a, b, self.place["sel_second"], name=f"sum_second_{n}").reshape(2 * a.shape[2], a.shape[3])
                       for n, a, b in zip(self.names, self.sums, got)]
        return _pair_swap(self.halves)

    def done(self, got):
        return {n: (mine, theirs) for n, mine, theirs in zip(self.names, self.halves, got)}


def _local_step(x, target, p, w, place=None):
    vec = lambda t: t.reshape(1, -1)
    w1_in, w1_out = w["ffn1_w_in"], w["ffn1_w_out"].reshape(2, FF_BLK, D_MODEL)
    blocks = lambda n, t: t.reshape((N_CHIPS,) + SHARD_SHAPE[n])

    ffn1 = functools.partial(_ffn_fwd, x, vec(p["ffn1_pre_g"]), w1_in, w1_out, vec(p["ffn1_post_g"]), None, name="ffn1_fwd")
    if place is None:
        x1, z1, f1 = ffn1()
    else:
        (x1, z1, f1), rest = ffn1(rider=_gather_weights([w[n] for n in REST]))
        w = dict(w, **dict(zip(REST, rest)))
    w2_in, w2_out = w["ffn2_w_in"], w["ffn2_w_out"].reshape(2, FF_BLK, D_MODEL)
    w_mi, w_glu, w_mo = w["w_mix_in"], w["w_glu"].reshape(SSM_W, SSM_W), w["w_mix_out"].reshape(2, ATTN_W, D_MODEL)
    q, k, v, u = _mix_in_fwd(x1, vec(p["mix_pre_g"]), w_mi)
    branches = [_attn_branch_fwd(q, k, v, d) for d in DILATIONS]
    tab_f, tab_b, bd, cd, sv = _ssm_prepare(p["a_re"], p["a_im"], p["log_dt"], p["b_re"], p["b_im"], p["c_re"], p["c_im"])
    ssm_args = (bd[0], bd[1], cd[0], cd[1], vec(p["d_skip"]), w_glu, vec(p["b_glu"]))
    s_re, s_im, yp, ssm = _ssm_fwd(u, tab_f, *ssm_args)
    x2, mixed, attn, lse = _mix_out_fwd(x1, [o for o, _ in branches], [l for _, l in branches], ssm, w_mo, vec(p["mix_post_g"]))
    dx3, loss_rows, z2, f2 = _ffn_fwd(x2, vec(p["ffn2_pre_g"]), w2_in, w2_out, vec(p["ffn2_post_g"]), target, name="ffn2_fwd")

    g = {}
    ride = (lambda call, exchange: call(rider=exchange)) if place else (lambda call, exchange: (call(), None))
    dz2, g["ffn2_post_g"], dw2_out = _ffn_bwd_out(dx3, f2, z2, w2_out, vec(p["ffn2_post_g"]), name="ffn2_bwd_out")
    dw2_in = _norm_matmul_dw(dz2, x2, vec(p["ffn2_pre_g"]), N_CHIPS, name="ffn2_bwd_dw", together=2)
    early = _Reduction(("ffn2_w_in", "ffn2_w_out"), [dw2_in, blocks("ffn2_w_out", dw2_out)], place) if place else None
    (dx2, g["ffn2_pre_g"]), got = ride(
        functools.partial(_norm_matmul_dx, dz2, x2, dx3, vec(p["ffn2_pre_g"]), w2_in, name="ffn2_bwd_dx"), early and early.exchange())
    dattn, dssm, dw_mo, g["mix_post_g"] = _mix_out_bwd(dx2, mixed, attn, ssm, w_mo, vec(p["mix_post_g"]))
    (du, g["d_skip"], g["b_glu"], dw_glu, da, dbd_re, dbd_im, dcd_re, dcd_im), got = ride(
        functools.partial(_ssm_bwd, dssm, u, yp, s_re, s_im, tab_b, *ssm_args), early and early.first(got))
    branch = lambda d: functools.partial(_attn_branch_bwd, q, k, v, attn, lse, dattn, d)
    parts = [None] * 3
    parts[0], got = ride(branch(DILATIONS[0]), early and early.second(got))
    parts[1], early_theirs = ride(branch(DILATIONS[1]), early and early.swap(got))
    parts[2] = branch(DILATIONS[2])()
    dproj, dw_mi = _mix_in_dw([r[0] for r in parts], [r[1] for r in parts], [r[2] for r in parts], du, x1, vec(p["mix_pre_g"]))
    dx1, g["mix_pre_g"] = _norm_matmul_dx(dproj, x1, dx2, vec(p["mix_pre_g"]), w_mi, name="mix_bwd_dx")
    dz1, g["ffn1_post_g"], dw1_out = _ffn_bwd_out(dx1, f1, z1, w1_out, vec(p["ffn1_post_g"]), name="ffn1_bwd_out")
    big = {"ffn2_w_in": dw2_in, "ffn2_w_out": dw2_out, "w_mix_in": dw_mi, "w_glu": dw_glu, "w_mix_out": dw_mo, "ffn1_w_out": dw1_out}
    late = _Reduction(LATE, [blocks(n, big[n]) for n in LATE], place) if place else None
    big["ffn1_w_in"], got = ride(
        functools.partial(_norm_matmul_dw, dz1, x, vec(p["ffn1_pre_g"]), N_CHIPS, name="ffn1_bwd_dw", together=2),
        late and late.exchange())
    dx_call = functools.partial(_norm_matmul_dx, dz1, x, dx1, vec(p["ffn1_pre_g"]), w1_in)
    if place:
        last = _Reduction(("ffn1_w_in",), [big["ffn1_w_in"]], place)
        n_late = len(LATE)
        (dx_part, dg_a), got = dx_call(name="ffn1_bwd_dx_a", part=(0, 0.375), rider=_merged(late.first(got), last.exchange()))
        (dx_part, dg_b), got = dx_call(name="ffn1_bwd_dx_b", part=(0.375, 0.75), into=dx_part,
                                       rider=_merged(late.second(got[:n_late]), last.first(got[n_late:])))
        grad_x, dg_c = dx_call(name="ffn1_bwd_dx_c", part=(0.75, 1), into=dx_part)
        g["ffn1_pre_g"] = dg_a + dg_b + dg_c
        got = _merged(late.swap(got[:n_late]), last.second(got[n_late:])).run("tail_second")
        g.update(late.done(got[:n_late]))
        g.update(last.done(last.swap(got[n_late:]).run("tail_swap")))
        g.update(early.done(early_theirs))
    else:
        grad_x, g["ffn1_pre_g"] = dx_call(name="ffn1_bwd_dx")
        g.update({n: blocks(n, big[n]) for n in BIG})

    g["c_re"], g["c_im"] = (jnp.swapaxes(m, 1, 2) for m in (dcd_re, dcd_im))
    dbb = [jnp.swapaxes(m, 1, 2).reshape(N_STATE, GROUP_CH) for m in (dbd_re, dbd_im)]
    dab = [jnp.sum(da[n], axis=0).reshape(N_STATE, 1) for n in range(2)]
    da_re, da_im, dldt, db_re, db_im = _ssm_discretize_bwd(
        sv["a_re"], sv["a_im"], sv["log_dt"], sv["b_re"], sv["b_im"], sv["ab_re"], sv["ab_im"], sv["e_re"], sv["e_im"],
        dab[0], dab[1], dbb[0], dbb[1])
    g["a_re"], g["a_im"] = da_re.reshape(N_GROUPS, STATE), da_im.reshape(N_GROUPS, STATE)
    g["log_dt"] = jnp.sum(dldt.reshape(N_GROUPS, STATE), axis=-1)
    g["b_re"], g["b_im"] = (t.reshape(N_GROUPS, STATE, GROUP_CH) for t in (db_re, db_im))
    return loss_rows, grad_x, g


MESH = pl.DeviceIdType.MESH
N_REL = 3


def _place():
    x, y, c = lax.axis_index("x"), lax.axis_index("y"), lax.axis_index("c")
    return x, y, c, [(1 - x, y), (x, 1 - y), (1 - x, 1 - y)]


def _remote(src, dst, send_sems, recv_sems, idx, to):
    return pltpu.make_async_remote_copy(src_ref=src, dst_ref=dst, send_sem=send_sems.at[idx], recv_sem=recv_sems.at[idx],
                                        device_id=to, device_id_type=MESH)


def _half(rows, who):
    return pl.ds(who * (rows // 2), rows // 2)


class _Copies:
    def __init__(self, send_sems, recv_sems, local_sems):
        self.send_sems, self.recv_sems, self.local_sems = send_sems, recv_sems, local_sems
        self.n_remote = self.n_local = 0

    def remote(self, src, dst, to):
        k, self.n_remote = self.n_remote, self.n_remote + 1
        return pltpu.make_async_remote_copy(src_ref=src, dst_ref=dst, send_sem=self.send_sems.at[k],
                                            recv_sem=self.recv_sems.at[k], device_id=to, device_id_type=MESH)

    def local(self, src, dst):
        k, self.n_local = self.n_local, self.n_local + 1
        return pltpu.make_async_copy(src, dst, self.local_sems.at[k])


class _Exchange:
    def __init__(self, plan, ins, out_shapes, n_remote):
        self.plan, self.ins, self.out_shapes, self.n_remote = plan, list(ins), list(out_shapes), n_remote
        self.sems = [pltpu.SemaphoreType.DMA((n_remote,)), pltpu.SemaphoreType.DMA((n_remote,)), pltpu.SemaphoreType.DMA((1,))]

    def run(self, name):
        n_in = len(self.ins)

        def body(*refs):
            for phase in self.plan(refs[:n_in], refs[n_in:-3], _Copies(*refs[-3:])):
                for cp in phase:
                    cp.start()
                for cp in phase:
                    cp.wait()

        return pl.pallas_call(body, name=name, in_specs=[ANY] * n_in, out_specs=[ANY] * len(self.out_shapes),
                              out_shape=self.out_shapes, scratch_shapes=self.sems)(*self.ins)


def _merged(a, b):
    n_in, n_out = len(a.ins), len(a.out_shapes)

    def plan(ins, outs, mk):
        (phase_a,), (phase_b,) = a.plan(ins[:n_in], outs[:n_out], mk), b.plan(ins[n_in:], outs[n_out:], mk)
        return [phase_a + phase_b]

    return _Exchange(plan, a.ins + b.ins, a.out_shapes + b.out_shapes, a.n_remote + b.n_remote)


def _pallas(body, *, name, grid, in_specs, out_specs, out_shape, args, semantics, scratch_shapes=(), rider=None, aliases=None):
    aliases = aliases or {}
    if rider is None:
        return pl.pallas_call(body, name=name, grid=grid, in_specs=in_specs, out_specs=out_specs, out_shape=out_shape,
                              scratch_shapes=list(scratch_shapes), input_output_aliases=aliases,
                              compiler_params=_params(*semantics))(*args)
    n_in, n_out, r_in, r_out = len(in_specs), len(out_specs), len(rider.ins), len(rider.out_shapes)
    n_steps = math.prod(grid)

    def carrier(*refs):
        ins, rider_ins = refs[:n_in], refs[n_in:n_in + r_in]
        outs = refs[n_in + r_in:n_in + r_in + n_out]
        rider_outs = refs[n_in + r_in + n_out:n_in + r_in + n_out + r_out]
        scratch, sems = refs[n_in + r_in + n_out + r_out:-3], refs[-3:]
        step = 0
        for axis, size in enumerate(grid):
            step = step * size + pl.program_id(axis)
        phases = rider.plan(rider_ins, rider_outs, _Copies(*sems))

        def start_phase(p):
            for cp in (phases[p - 1] if p else []):
                cp.wait()
            for cp in phases[p]:
                cp.start()

        for p in range(len(phases)):
            pl.when(step == p * n_steps // len(phases))(functools.partial(start_phase, p))
        body(*ins, *outs, *scratch)

        @pl.when(step == n_steps - 1)
        def _():
            for cp in phases[-1]:
                cp.wait()

    results = pl.pallas_call(
        carrier, name=name, grid=grid, in_specs=list(in_specs) + [ANY] * r_in, out_specs=list(out_specs) + [ANY] * r_out,
        out_shape=list(out_shape) + rider.out_shapes, scratch_shapes=list(scratch_shapes) + rider.sems,
        input_output_aliases=aliases, compiler_params=_params(*["arbitrary"] * len(grid)))(*args, *rider.ins)
    return results[:n_out], results[n_out:]


def _gather_weights(shards):
    def plan(ins, outs, mk):
        x, y, c = lax.axis_index("x"), lax.axis_index("y"), lax.axis_index("c")
        me, sibling = 2 * x + y, (x, y, 1 - c)
        x_nb, y_nb, diag = (1 - x, y), (x, 1 - y), (1 - x, 1 - y)
        index = lambda chip: 2 * chip[0] + chip[1]
        first, second, third = [], [], []
        for i, shard in enumerate(shards):
            rows = shard.shape[0]
            mine = _half(rows, c)
            quarter = lambda which: pl.ds(c * (rows // 2) + which * (rows // 4), rows // 4)
            first.append(mk.remote(ins[i], outs[i].at[me], sibling))
            for nb in (x_nb, y_nb):
                first.append(mk.remote(ins[i].at[mine], outs[i].at[me, mine], (*nb, c)))
            for nb in (x_nb, y_nb):
                landed = outs[i].at[index(nb), mine]
                second.append(mk.remote(landed, landed, sibling))
            for nb, other, which in ((x_nb, y_nb, 0), (y_nb, x_nb, 1)):
                landed = outs[i].at[index(nb), quarter(which)]
                second.append(mk.remote(landed, landed, (*other, c)))
            landed = outs[i].at[index(diag), mine]
            third.append(mk.remote(landed, landed, sibling))
        return [first, second, third]

    return _Exchange(plan, shards, [jax.ShapeDtypeStruct((N_CHIPS,) + s.shape, s.dtype) for s in shards], 8 * len(shards))


def _pair_exchange(grads):
    def plan(ins, outs, mk):
        x, y, c = lax.axis_index("x"), lax.axis_index("y"), lax.axis_index("c")
        return [[mk.remote(ins[i].at[:, _half(g.shape[1], 1 - c)], outs[i], (x, y, 1 - c)) for i, g in enumerate(grads)]]

    return _Exchange(plan, grads, [jax.ShapeDtypeStruct((N_CHIPS, g.shape[1] // 2, g.shape[2]), g.dtype) for g in grads], len(grads))


def _reduce_first(pair):
    def plan(ins, outs, mk):
        x, y, c = lax.axis_index("x"), lax.axis_index("y"), lax.axis_index("c")
        phase = []
        for i, p in enumerate(pair):
            q = p.shape[1] // 2
            phase.append(mk.remote(ins[i].at[pl.ds(2 * (1 - x), 2), pl.ds(0, q)], outs[i].at[0], (1 - x, y, c)))
            for jx in range(2):
                phase.append(mk.remote(ins[i].at[2 * jx + 1 - y, pl.ds(q, q)], outs[i].at[1, jx], (x, 1 - y, c)))
        return [phase]

    return _Exchange(plan, pair, [jax.ShapeDtypeStruct((2, 2, p.shape[1] // 2, p.shape[2]), p.dtype) for p in pair], 3 * len(pair))


def _reduce_second(sums):
    def plan(ins, outs, mk):
        x, y, c = lax.axis_index("x"), lax.axis_index("y"), lax.axis_index("c")
        phase = []
        for i in range(len(sums)):
            phase.append(mk.remote(ins[i].at[0, 1 - y], outs[i].at[0], (x, 1 - y, c)))
            phase.append(mk.remote(ins[i].at[1, 1 - x], outs[i].at[1], (1 - x, y, c)))
        return [phase]

    return _Exchange(plan, sums, [jax.ShapeDtypeStruct((2,) + s.shape[2:], s.dtype) for s in sums], 2 * len(sums))


def _pair_swap(halves):
    def plan(ins, outs, mk):
        x, y, c = lax.axis_index("x"), lax.axis_index("y"), lax.axis_index("c")
        return [[mk.remote(ins[i], outs[i], (x, y, 1 - c)) for i in range(len(halves))]]

    return _Exchange(plan, halves, [jax.ShapeDtypeStruct(h.shape, h.dtype) for h in halves], len(halves))


def _allreduce_small(packed):
    rows = packed.shape[0]
    n_dev = 2 * N_CHIPS

    def body(x_ref, o_ref, buf, send_sems, recv_sems):
        x, y, c, chips = _place()
        sibling = (x, y, 1 - c)

        def slot(px, py, pc):
            return buf.at[4 * px + 2 * py + pc]

        buf[4 * x + 2 * y + c] = x_ref[...]
        first = [_remote(x_ref, slot(x, y, c), send_sems, recv_sems, 0, sibling)]
        first += [_remote(x_ref, slot(x, y, c), send_sems, recv_sems, 1 + k, (*chip, c)) for k, chip in enumerate(chips)]
        for cp in first:
            cp.start()
        passed = []
        for k, chip in enumerate(chips):
            landed = slot(*chip, c)
            _remote(landed, landed, send_sems, recv_sems, 1 + k, (*chip, c)).wait_recv()
            passed.append(_remote(landed, landed, send_sems, recv_sems, 1 + N_REL + k, sibling))
            passed[-1].start()
        _remote(slot(*sibling), slot(*sibling), send_sems, recv_sems, 0, sibling).wait_recv()
        for k, chip in enumerate(chips):
            landed = slot(*chip, 1 - c)
            _remote(landed, landed, send_sems, recv_sems, 1 + N_REL + k, sibling).wait_recv()
        for cp in first + passed:
            cp.wait_send()
        total = buf[0]
        for d in range(1, n_dev):
            total = total + buf[d]
        o_ref[...] = total

    vm = pl.BlockSpec(memory_space=pltpu.VMEM)
    return pl.pallas_call(
        body, name="allreduce_small", in_specs=[vm], out_specs=vm, out_shape=jax.ShapeDtypeStruct(packed.shape, F32),
        scratch_shapes=[pltpu.VMEM((n_dev, rows, 128), F32), pltpu.SemaphoreType.DMA((1 + 2 * N_REL,)),
                        pltpu.SemaphoreType.DMA((1 + 2 * N_REL,))],
    )(packed)


def _row_tile(rows, cap=256):
    return max(t for t in range(8, cap + 1, 8) if rows % t == 0)


def _pair_sum(grad, got, c, name):
    _, half, cols = got.shape
    tr = _row_tile(half)
    nt = half // tr

    def body(c_ref, g_ref, r_ref, o_ref, wire_ref):
        total = g_ref[...] + r_ref[...]
        o_ref[...] = total
        wire_ref[...] = total.astype(BF16)

    blk = (1, tr, cols)
    out = pl.BlockSpec(blk, lambda j, t, c_ref: (j, t, 0))
    return pl.pallas_call(
        body, name=name,
        grid_spec=pltpu.PrefetchScalarGridSpec(
            num_scalar_prefetch=1, grid=(N_CHIPS, nt),
            in_specs=[pl.BlockSpec(blk, lambda j, t, c_ref: (j, c_ref[0] * nt + t, 0)), out], out_specs=[out, out]),
        out_shape=[jax.ShapeDtypeStruct(got.shape, F32), jax.ShapeDtypeStruct(got.shape, BF16)],
        compiler_params=_params("parallel", "parallel"),
    )(c, grad, got)


def _reduce_sum_first(pair, got, sel, name):
    _, _, q, cols = got.shape
    tr = _row_tile(q)
    nt = q // tr

    def body(sel_ref, p_ref, r_ref, o_ref, wire_ref):
        total = p_ref[0] + r_ref[0, 0].astype(F32)
        o_ref[0, 0] = total
        wire_ref[0, 0] = total.astype(BF16)

    blk = pl.BlockSpec((1, 1, tr, cols), lambda p, k, t, s: (p, k, t, 0))
    return pl.pallas_call(
        body, name=name,
        grid_spec=pltpu.PrefetchScalarGridSpec(
            num_scalar_prefetch=1, grid=(2, 2, nt),
            in_specs=[pl.BlockSpec((1, tr, cols), lambda p, k, t, s: (s[2 * p] + s[2 * p + 1] * k, p * nt + t, 0)), blk],
            out_specs=[blk, blk]),
        out_shape=[jax.ShapeDtypeStruct(got.shape, F32), jax.ShapeDtypeStruct(got.shape, BF16)],
        compiler_params=_params("parallel", "parallel", "parallel"),
    )(sel, pair, got)


def _reduce_sum_second(sums, got, sel, name):
    _, q, cols = got.shape
    tr = _row_tile(q)

    def body(sel_ref, s_ref, r_ref, o_ref):
        o_ref[0] = s_ref[0, 0] + r_ref[0].astype(F32)

    blk = (1, tr, cols)
    return pl.pallas_call(
        body, name=name,
        grid_spec=pltpu.PrefetchScalarGridSpec(
            num_scalar_prefetch=1, grid=(2, q // tr),
            in_specs=[pl.BlockSpec((1, 1, tr, cols), lambda p, t, s: (p, s[p], t, 0)),
                      pl.BlockSpec(blk, lambda p, t, s: (p, t, 0))],
            out_specs=pl.BlockSpec(blk, lambda p, t, s: (p, t, 0))),
        out_shape=jax.ShapeDtypeStruct(got.shape, F32), compiler_params=_params("parallel", "parallel"),
    )(sel, sums, got)


def _adamw_update(w, g, m, v):
    m2 = ADAM_B1 * m + (1.0 - ADAM_B1) * g
    v2 = ADAM_B2 * v + (1.0 - ADAM_B2) * (g * g)
    m_hat = m2 / (1.0 - ADAM_B1 ** ADAM_STEP)
    v_hat = v2 / (1.0 - ADAM_B2 ** ADAM_STEP)
    return -ADAM_LR * (m_hat / (jnp.sqrt(v_hat) + ADAM_EPS) + ADAM_WD * w), m2, v2


def _adamw_small(ws, gs, ms, vs):
    n = len(ws)

    def body(*refs):
        w, g, m, v, d, mo, vo = (refs[k * n:(k + 1) * n] for k in range(7))
        for i in range(n):
            d[i][...], mo[i][...], vo[i][...] = _adamw_update(w[i][...], g[i][...], m[i][...], v[i][...])

    shapes = [jax.ShapeDtypeStruct(t.shape, F32) for t in ws]
    outs = pl.pallas_call(body, name="adamw_small", out_shape=shapes * 3,
                          compiler_params=pltpu.CompilerParams(vmem_limit_bytes=VMEM_LIMIT_V7X))(*ws, *gs, *ms, *vs)
    return outs[:n], outs[n:2 * n], outs[2 * n:]


def _adamw_halves(w, mine, theirs, m, v, core, name):
    rows, cols = w.shape
    tr = _row_tile(rows // 2)
    per_half = rows // 2 // tr

    def body(core_ref, w_ref, a_ref, b_ref, m_ref, v_ref, g_ref, d_ref, mo_ref, vo_ref):
        g = jnp.where(pl.program_id(0) // per_half == core_ref[0], a_ref[...], b_ref[...])
        g_ref[...] = g
        d_ref[...], mo_ref[...], vo_ref[...] = _adamw_update(w_ref[...], g, m_ref[...], v_ref[...])

    blk = pl.BlockSpec((tr, cols), lambda t, c: (t, 0))
    half = lambda own: pl.BlockSpec(
        (tr, cols), lambda t, c: (jnp.clip(t - (c[0] if own else 1 - c[0]) * per_half, 0, per_half - 1), 0))
    return pl.pallas_call(
        body, name=name,
        grid_spec=pltpu.PrefetchScalarGridSpec(
            num_scalar_prefetch=1, grid=(2 * per_half,), in_specs=[blk, half(True), half(False), blk, blk], out_specs=[blk] * 4),
        out_shape=[jax.ShapeDtypeStruct(w.shape, F32)] * 4, compiler_params=_params("arbitrary"),
    )(core, w, mine, theirs, m, v)


def _pack(parts):
    flat = []
    for t in parts:
        t = t.reshape(-1).astype(F32)
        flat.append(jnp.pad(t, (0, -t.shape[0] % 128)))
    flat = jnp.concatenate(flat)
    return jnp.pad(flat, (0, -flat.shape[0] % 1024)).reshape(-1, 128)


def _unpack(buf, shapes):
    flat, out, at = buf.reshape(-1), [], 0
    for s in shapes:
        size = math.prod(s)
        out.append(flat[at:at + size].reshape(s))
        at += size + (-size % 128)
    return out


def kernel(x, ffn1_pre_g, ffn1_w_in, ffn1_w_out, ffn1_post_g, mix_pre_g, w_mix_in, a_re, a_im, log_dt, b_re, b_im, c_re, c_im, d_skip, w_glu, b_glu, w_mix_out, mix_post_g, ffn2_pre_g, ffn2_w_in, ffn2_w_out, ffn2_post_g, loss_target, m_ffn1_pre_g, m_ffn1_w_in, m_ffn1_w_out, m_ffn1_post_g, m_mix_pre_g, m_w_mix_in, m_a_re, m_a_im, m_log_dt, m_b_re, m_b_im, m_c_re, m_c_im, m_d_skip, m_w_glu, m_b_glu, m_w_mix_out, m_mix_post_g, m_ffn2_pre_g, m_ffn2_w_in, m_ffn2_w_out, m_ffn2_post_g, v_ffn1_pre_g, v_ffn1_w_in, v_ffn1_w_out, v_ffn1_post_g, v_mix_pre_g, v_w_mix_in, v_a_re, v_a_im, v_log_dt, v_b_re, v_b_im, v_c_re, v_c_im, v_d_skip, v_w_glu, v_b_glu, v_w_mix_out, v_mix_post_g, v_ffn2_pre_g, v_ffn2_w_in, v_ffn2_w_out, v_ffn2_post_g):
    given = dict(locals())
    order = ("ffn1_pre_g", "ffn1_w_in", "ffn1_w_out", "ffn1_post_g", "mix_pre_g", "w_mix_in", "a_re", "a_im", "log_dt",
             "b_re", "b_im", "c_re", "c_im", "d_skip", "w_glu", "b_glu", "w_mix_out", "mix_post_g", "ffn2_pre_g",
             "ffn2_w_in", "ffn2_w_out", "ffn2_post_g")
    at_x, at_y, at_c = (lax.axis_index(a).astype(jnp.int32) for a in ("x", "y", "c"))
    place = dict(core=at_c.reshape(1), sel_first=jnp.stack([2 * at_x, jnp.int32(1), at_y, jnp.int32(2)]),
                 sel_second=jnp.stack([at_y, at_x]))

    shards = {n: given[n][0] for n in BIG}
    w = {n: shards[n].astype(BF16) for n in REST}
    w.update(zip(FIRST, _gather_weights([shards[n].astype(BF16) for n in FIRST]).run("gather_first")))
    small = {n: given[n][0] for n in SMALL}
    loss_rows, grad_x, g = _local_step(x[0], loss_target[0], small, w, place)

    total = _allreduce_small(_pack([g[n] for n in SMALL] + [loss_rows[0, :1]]))
    parts = _unpack(total, [small[n].shape for n in SMALL] + [(1,)])
    grads = dict(zip(SMALL, parts[:-1]))
    loss = parts[-1][0]

    delta, new_m, new_v = {}, {}, {}
    for n in BIG:
        grads[n], delta[n], new_m[n], new_v[n] = _adamw_halves(
            shards[n], *g[n], given["m_" + n][0], given["v_" + n][0], place["core"], name=f"adamw_{n}")
    take = lambda pre: [given[pre + n] for n in SMALL]
    outs = _adamw_small(take(""), [grads[n][None] for n in SMALL], take("m_"), take("v_"))
    for store, arrays in zip((delta, new_m, new_v), outs):
        store.update({n: t[0] for n, t in zip(SMALL, arrays)})

    lead = lambda d: [d[n][None] for n in order]
    return (loss, grad_x[None], *lead(grads), *lead(delta), *lead(new_m), *lead(new_v))
```

```python
import functools
import math

import jax
import jax.numpy as jnp
from jax import lax
from jax.experimental import pallas as pl
from jax.experimental.pallas import tpu as pltpu

F32, BF16 = jnp.float32, jnp.bfloat16

D_MODEL = 1024
D_FF = 2816
N_CHIPS = 4
FF_BLK = 2 * D_FF // N_CHIPS
ATTN_W = 512
SSM_W = 512
HEAD_DIM = 64
N_HEADS = ATTN_W // HEAD_DIM
DILATIONS = (1, 4, 16)
N_BACK = 128
QBLK = 128
N_GROUPS = 32
GROUP_CH = 16
STATE = 64
N_STATE = N_GROUPS * STATE
EPS = 1e-6
NEG = -1e30
GELU_C = math.sqrt(2.0 / math.pi)

ADAM_LR, ADAM_B1, ADAM_B2, ADAM_EPS, ADAM_WD, ADAM_STEP = 0.001, 0.9, 0.999, 1e-08, 0.01, 10

VMEM_LIMIT_V7X = 60 * 1024 * 1024
ROW_TILE = 512
FFN_ROW_TILE = 512
FFN_BWD_OUT_TILE = 256
DW_ROW_TILE = 1024


def _params(*sem):
    return pltpu.CompilerParams(dimension_semantics=sem, vmem_limit_bytes=VMEM_LIMIT_V7X)


def _dot(a, b):
    return jnp.dot(a.astype(BF16), b.astype(BF16), preferred_element_type=F32)


def _dot_nt(a, b):
    return lax.dot_general(a.astype(BF16), b.astype(BF16), (((1,), (1,)), ((), ())), preferred_element_type=F32)


def _dot_tn(a, b):
    return lax.dot_general(a.astype(BF16), b.astype(BF16), (((0,), (0,)), ((), ())), preferred_element_type=F32)


def _full(shape):
    return pl.BlockSpec(shape, lambda *_: (0,) * len(shape))


def _rows(tm, width):
    return pl.BlockSpec((tm, width), lambda i: (i, 0))


ANY = pl.BlockSpec(memory_space=pl.ANY)


def _load_once(pairs, sems):
    copies = [pltpu.make_async_copy(src, dst, sems.at[k]) for k, (src, dst) in enumerate(pairs)]
    for c in copies:
        c.start()
    for c in copies:
        c.wait()


def _rms(x):
    return lax.rsqrt(jnp.mean(x * x, axis=-1, keepdims=True) + EPS)


def _rms_bwd(dy_g, xn, r):
    return r * (dy_g - xn * jnp.mean(dy_g * xn, axis=-1, keepdims=True))


def _ffn_fwd(x, g_pre, w_in, w_out, g_post, target, *, name, rider=None):
    T = x.shape[0]
    tm = FFN_ROW_TILE
    with_loss = target is not None

    def body(*refs):
        if with_loss:
            x_ref, gpre_ref, gpost_ref, tgt_ref, win_hbm, wout_hbm, o_ref, loss_ref, z_ref, f_ref, win_v, wout_v, sems = refs
        else:
            x_ref, gpre_ref, gpost_ref, win_hbm, wout_hbm, o_ref, z_ref, f_ref, win_v, wout_v, sems = refs

        @pl.when(pl.program_id(0) == 0)
        def _():
            _load_once([(win_hbm, win_v), (wout_hbm, wout_v)], sems)
            if with_loss:
                loss_ref[...] = jnp.zeros_like(loss_ref)

        xv = x_ref[...]
        h = (xv * _rms(xv) * gpre_ref[...]).astype(BF16)
        f = jnp.zeros((tm, D_MODEL), F32)
        for k in range(2):
            gate = _dot(h, win_v[k])
            up = _dot(h, win_v[k + 2])
            z_ref[:, k * FF_BLK:(k + 1) * FF_BLK] = gate.astype(BF16)
            z_ref[:, D_FF + k * FF_BLK:D_FF + (k + 1) * FF_BLK] = up.astype(BF16)
            f = f + _dot(gate * jax.nn.sigmoid(gate) * up, wout_v[k])
        f_ref[...] = f
        out = xv + 0.5 * (f * _rms(f) * gpost_ref[...])
        if with_loss:
            err = out - tgt_ref[...]
            o_ref[...] = err * (1.0 / D_MODEL)
            loss_ref[...] += jnp.sum(err * err) * (0.5 / D_MODEL)
        else:
            o_ref[...] = out

    row = _rows(tm, D_MODEL)
    vec = _full((1, D_MODEL))
    in_specs = [row, vec, vec] + ([row] if with_loss else []) + [ANY, ANY]
    out_shape = [jax.ShapeDtypeStruct((T, D_MODEL), F32)]
    out_specs = [row]
    if with_loss:
        out_shape.append(jax.ShapeDtypeStruct((8, 128), F32))
        out_specs.append(_full((8, 128)))
    out_shape += [jax.ShapeDtypeStruct((T, 2 * D_FF), BF16), jax.ShapeDtypeStruct((T, D_MODEL), F32)]
    out_specs += [_rows(tm, 2 * D_FF), row]
    args = (x, g_pre, g_post) + ((target,) if with_loss else ()) + (w_in, w_out)
    return _pallas(
        body, name=name, grid=(T // tm,), in_specs=in_specs, out_specs=out_specs, out_shape=out_shape,
        scratch_shapes=[pltpu.VMEM(w_in.shape, BF16), pltpu.VMEM(w_out.shape, BF16), pltpu.SemaphoreType.DMA((2,))],
        semantics=("arbitrary",), args=args, rider=rider)


def _ffn_bwd_out(dout, f, z, w_out, g_post, *, name):
    T = dout.shape[0]
    tm = FFN_BWD_OUT_TILE
    nt = T // tm

    def body(dout_ref, f_ref, z_ref, gpost_ref, wout_hbm, dz_ref, dgpost_ref, dwout_hbm, wout_v, dwout_v, sems):
        i = pl.program_id(0)

        @pl.when(i == 0)
        def _():
            _load_once([(wout_hbm, wout_v)], sems)
            dwout_v[...] = jnp.zeros_like(dwout_v)
            dgpost_ref[...] = jnp.zeros_like(dgpost_ref)

        dy = 0.5 * dout_ref[...]
        f = f_ref[...]
        r = _rms(f)
        fn = f * r
        dgpost_ref[...] += jnp.sum(dy * fn, axis=0, keepdims=True)
        df = _rms_bwd(dy * gpost_ref[...], fn, r).astype(BF16)
        for k in range(2):
            gate = z_ref[:, k * FF_BLK:(k + 1) * FF_BLK].astype(F32)
            up = z_ref[:, D_FF + k * FF_BLK:D_FF + (k + 1) * FF_BLK].astype(F32)
            sg = jax.nn.sigmoid(gate)
            silu = gate * sg
            dwout_v[k] += _dot_tn(silu * up, df)
            da = _dot_nt(df, wout_v[k])
            dz_ref[:, k * FF_BLK:(k + 1) * FF_BLK] = (da * up * (sg * (1.0 + gate * (1.0 - sg)))).astype(BF16)
            dz_ref[:, D_FF + k * FF_BLK:D_FF + (k + 1) * FF_BLK] = (da * silu).astype(BF16)

        @pl.when(i == nt - 1)
        def _():
            c = pltpu.make_async_copy(dwout_v, dwout_hbm, sems.at[0])
            c.start()
            c.wait()

    row = _rows(tm, D_MODEL)
    return pl.pallas_call(
        body, name=name, grid=(nt,),
        in_specs=[row, row, _rows(tm, 2 * D_FF), _full((1, D_MODEL)), ANY],
        out_specs=[_rows(tm, 2 * D_FF), _full((1, D_MODEL)), ANY],
        out_shape=[jax.ShapeDtypeStruct((T, 2 * D_FF), BF16), jax.ShapeDtypeStruct((1, D_MODEL), F32),
                   jax.ShapeDtypeStruct(w_out.shape, F32)],
        scratch_shapes=[pltpu.VMEM(w_out.shape, BF16), pltpu.VMEM(w_out.shape, F32), pltpu.SemaphoreType.DMA((1,))],
        compiler_params=_params("arbitrary"),
    )(dout, f, z, g_post, w_out)


def _norm_matmul_dw(dz, x, g, n_blocks, *, name, rider=None, together=1):
    T = x.shape[0]
    bw = dz.shape[1] // n_blocks
    tm = DW_ROW_TILE

    def body(dz_ref, x_ref, g_ref, dw_ref):
        @pl.when(pl.program_id(1) == 0)
        def _():
            dw_ref[...] = jnp.zeros_like(dw_ref)

        xv = x_ref[...]
        h_t = (xv * _rms(xv) * g_ref[...]).T.astype(BF16)
        for b in range(together):
            dw_ref[b] += jnp.dot(h_t, dz_ref[:, b * bw:(b + 1) * bw], preferred_element_type=F32)

    res = _pallas(
        body, name=name, grid=(n_blocks // together, T // tm),
        in_specs=[pl.BlockSpec((tm, together * bw), lambda j, t: (t, j)), pl.BlockSpec((tm, D_MODEL), lambda j, t: (t, 0)),
                  _full((1, D_MODEL))],
        out_specs=[pl.BlockSpec((together, D_MODEL, bw), lambda j, t: (j, 0, 0))],
        out_shape=[jax.ShapeDtypeStruct((n_blocks, D_MODEL, bw), F32)], semantics=("parallel", "arbitrary"),
        args=(dz, x, g), rider=rider)
    return res[0] if rider is None else (res[0][0], res[1])


def _norm_matmul_dx(dz, x, dres, g, w, *, name, rider=None, part=(0, 1), into=None):
    T = x.shape[0]
    nb, _, bw = w.shape
    tm = FFN_ROW_TILE
    first, steps = round(part[0] * T) // tm, round((part[1] - part[0]) * T) // tm

    def body(dz_ref, x_ref, dres_ref, g_ref, w_hbm, *rest):
        dx_ref, dg_ref, w_v, sems = rest[-4:]

        @pl.when(pl.program_id(0) == 0)
        def _():
            _load_once([(w_hbm, w_v)], sems)
            dg_ref[...] = jnp.zeros_like(dg_ref)

        xv = x_ref[...]
        r = _rms(xv)
        xn = xv * r
        gv = g_ref[...]
        dh = jnp.zeros((tm, D_MODEL), F32)
        for j in range(nb):
            dh = dh + _dot_nt(dz_ref[:, j * bw:(j + 1) * bw], w_v[j])
        dg_ref[...] += jnp.sum(dh * xn, axis=0, keepdims=True)
        dx_ref[...] = _rms_bwd(dh * gv, xn, r) + dres_ref[...]

    rows = lambda width: pl.BlockSpec((tm, width), lambda i: (first + i, 0))
    row = rows(D_MODEL)
    return _pallas(
        body, name=name, grid=(steps,),
        in_specs=[rows(nb * bw), row, row, _full((1, D_MODEL)), ANY] + ([] if into is None else [ANY]),
        out_specs=[row, _full((1, D_MODEL))],
        out_shape=[jax.ShapeDtypeStruct((T, D_MODEL), F32), jax.ShapeDtypeStruct((1, D_MODEL), F32)],
        scratch_shapes=[pltpu.VMEM(w.shape, BF16), pltpu.SemaphoreType.DMA((1,))],
        semantics=("arbitrary",), args=(dz, x, dres, g, w) + (() if into is None else (into,)), rider=rider,
        aliases={} if into is None else {5: 0})


def _mix_in_fwd(x, g, w):
    T = x.shape[0]
    tm = ROW_TILE

    def body(x_ref, g_ref, w_ref, q_ref, k_ref, v_ref, u_ref):
        xv = x_ref[...]
        h = (xv * _rms(xv) * g_ref[...]).astype(BF16)
        for j, o_ref in enumerate((q_ref, k_ref, v_ref, u_ref)):
            o_ref[...] = _dot(h, w_ref[j])

    col = _rows(tm, ATTN_W)
    return pl.pallas_call(
        body, name="mix_in_fwd", grid=(T // tm,),
        in_specs=[_rows(tm, D_MODEL), _full((1, D_MODEL)), _full(w.shape)],
        out_specs=[col] * 4, out_shape=[jax.ShapeDtypeStruct((T, ATTN_W), F32)] * 4,
        compiler_params=_params("parallel"),
    )(x, g, w)


def _mix_out_fwd(x, outs, lses, ssm, w, g):
    T = x.shape[0]
    tm = ROW_TILE

    def body(x_ref, o1, o2, o3, l1, l2, l3, s_ref, w_ref, g_ref, o_ref, m_ref, a_ref, lse_ref):
        ls = [l1[...], l2[...], l3[...]]
        top = jnp.maximum(jnp.maximum(ls[0], ls[1]), ls[2])
        lse = top + jnp.log(jnp.exp(ls[0] - top) + jnp.exp(ls[1] - top) + jnp.exp(ls[2] - top))
        attn = jnp.exp(ls[0] - lse) * o1[...] + jnp.exp(ls[1] - lse) * o2[...] + jnp.exp(ls[2] - lse) * o3[...]
        lse_ref[...] = lse
        a_ref[...] = attn
        mixed = _dot(attn, w_ref[0]) + _dot(s_ref[...], w_ref[1])
        m_ref[...] = mixed
        o_ref[...] = x_ref[...] + mixed * _rms(mixed) * g_ref[...]

    row, col = _rows(tm, D_MODEL), _rows(tm, ATTN_W)
    return pl.pallas_call(
        body, name="mix_out_fwd", grid=(T // tm,),
        in_specs=[row] + [col] * 7 + [_full(w.shape), _full((1, D_MODEL))],
        out_specs=[row, row, col, col],
        out_shape=[jax.ShapeDtypeStruct((T, D_MODEL), F32)] * 2 + [jax.ShapeDtypeStruct((T, ATTN_W), F32)] * 2,
        compiler_params=_params("parallel"),
    )(x, *outs, *lses, ssm, w, g)


def _mix_out_bwd(dout, mixed, attn, ssm, w, g, rider=None):
    T = dout.shape[0]
    tm = ROW_TILE

    def body(dout_ref, m_ref, a_ref, s_ref, w_ref, g_ref, da_ref, ds_ref, dw_ref, dg_ref):
        @pl.when(pl.program_id(0) == 0)
        def _():
            dw_ref[...] = jnp.zeros_like(dw_ref)
            dg_ref[...] = jnp.zeros_like(dg_ref)

        dy = dout_ref[...]
        mixed = m_ref[...]
        r = _rms(mixed)
        mn = mixed * r
        dg_ref[...] += jnp.sum(dy * mn, axis=0, keepdims=True)
        dm = _rms_bwd(dy * g_ref[...], mn, r).astype(BF16)
        da_ref[...] = _dot_nt(dm, w_ref[0])
        ds_ref[...] = _dot_nt(dm, w_ref[1])
        dw_ref[0] += _dot_tn(a_ref[...], dm)
        dw_ref[1] += _dot_tn(s_ref[...], dm)

    row, col = _rows(tm, D_MODEL), _rows(tm, ATTN_W)
    return _pallas(
        body, name="mix_out_bwd", grid=(T // tm,),
        in_specs=[row, row, col, col, _full(w.shape), _full((1, D_MODEL))],
        out_specs=[col, col, _full(w.shape), _full((1, D_MODEL))],
        out_shape=[jax.ShapeDtypeStruct((T, ATTN_W), F32)] * 2
        + [jax.ShapeDtypeStruct(w.shape, F32), jax.ShapeDtypeStruct((1, D_MODEL), F32)],
        semantics=("arbitrary",), args=(dout, mixed, attn, ssm, w, g), rider=rider)


ATTN_TILING = {1: (ATTN_W, 1), 4: (2 * HEAD_DIM, 4), 16: (2 * HEAD_DIM, 4)}


def _class_rows(d, r):
    return (pl.ds(r, QBLK, stride=d), slice(None)) if d > 1 else (slice(None), slice(None))


def _for_class_groups(d, group, fn):
    if d == group:
        fn(0)
    else:
        lax.fori_loop(0, d // group, lambda n, carry: (fn(n * group), carry)[1], 0)


def _block_slopes(lanes, lane_block):
    heads = lanes // HEAD_DIM
    first = lane_block * heads
    return [jnp.exp2(-jnp.full((1, 1), first + hh + 1, jnp.int32).astype(F32)) for hh in range(heads)]


def _attn_specs(d, nb, lanes):
    blk = (QBLK * d, lanes)
    cur = pl.BlockSpec(blk, lambda j, lb: (j, lb))
    prev = pl.BlockSpec(blk, lambda j, lb: (jnp.maximum(j - 1, 0), lb))
    nxt = pl.BlockSpec(blk, lambda j, lb: (jnp.minimum(j + 1, nb - 1), lb))
    return cur, prev, nxt


def _attn_branch_fwd(q, k, v, d):
    T = q.shape[0]
    nb = T // (d * QBLK)
    lanes, group = ATTN_TILING[d]
    scale = HEAD_DIM ** -0.5

    def body(q_ref, kc_ref, kp_ref, vc_ref, vp_ref, o_ref, l_ref, q_s, kk_s, vv_s, o_s, l_s):
        j = pl.program_id(0)
        qi = lax.broadcasted_iota(jnp.int32, (QBLK, 2 * QBLK), 0)
        ci = lax.broadcasted_iota(jnp.int32, (QBLK, 2 * QBLK), 1)
        steps = QBLK + qi - ci
        valid = (steps >= 0) & (steps <= N_BACK) & ((ci >= QBLK) | (j > 0))
        dist = (steps * d).astype(F32)
        slopes = _block_slopes(lanes, pl.program_id(1))

        def classes(first):
            for n in range(group):
                rows = _class_rows(d, first + n)
                q_s[n] = q_ref[rows]
                kk_s[n, :QBLK], kk_s[n, QBLK:] = kp_ref[rows], kc_ref[rows]
                vv_s[n, :QBLK], vv_s[n, QBLK:] = vp_ref[rows], vc_ref[rows]
            for n in range(group):
                for hh in range(lanes // HEAD_DIM):
                    sl = slice(hh * HEAD_DIM, (hh + 1) * HEAD_DIM)
                    s = _dot_nt(q_s[n, :, sl], kk_s[n, :, sl]) * scale - slopes[hh] * dist
                    s = jnp.where(valid, s, NEG)
                    m = jnp.max(s, axis=-1, keepdims=True)
                    p = jnp.exp(s - m)
                    den = jnp.sum(p, axis=-1, keepdims=True)
                    o_s[n, :, sl] = _dot(p, vv_s[n, :, sl]) / den
                    l_s[n, :, sl] = jnp.broadcast_to(m + jnp.log(den), (QBLK, HEAD_DIM))
            for n in range(group):
                rows = _class_rows(d, first + n)
                o_ref[rows] = o_s[n]
                l_ref[rows] = l_s[n]

        _for_class_groups(d, group, classes)

    cur, prev, _ = _attn_specs(d, nb, lanes)
    shape = jax.ShapeDtypeStruct((T, ATTN_W), F32)
    one, two = pltpu.VMEM((group, QBLK, lanes), F32), pltpu.VMEM((group, 2 * QBLK, lanes), F32)
    return pl.pallas_call(
        body, name=f"attn_fwd_d{d}", grid=(nb, ATTN_W // lanes),
        in_specs=[cur, cur, prev, cur, prev], out_specs=[cur, cur], out_shape=[shape, shape],
        scratch_shapes=[one, two, two, one, one], compiler_params=_params("parallel", "parallel"),
    )(q, k, k, v, v)


def _attn_branch_bwd(q, k, v, o, lse, do, d, rider=None):
    T = q.shape[0]
    nb = T // (d * QBLK)
    lanes, group = ATTN_TILING[d]
    scale = HEAD_DIM ** -0.5

    def body(q_ref, kc_ref, kp_ref, vc_ref, vp_ref, o_ref, l_ref, do_ref, dq_ref, dk_ref, dv_ref,
             q_s, o_s, l_s, do_s, kk_s, vv_s, dq_s, dk_s, dv_s, ck_s, cv_s):
        j = pl.program_id(1)

        @pl.when(j == 0)
        def _():
            ck_s[...] = jnp.zeros_like(ck_s)
            cv_s[...] = jnp.zeros_like(cv_s)

        qi = lax.broadcasted_iota(jnp.int32, (QBLK, 2 * QBLK), 0)
        ci = lax.broadcasted_iota(jnp.int32, (QBLK, 2 * QBLK), 1)
        steps = QBLK + qi - ci
        valid = (steps >= 0) & (steps <= N_BACK) & ((ci >= QBLK) | (j > 0))
        dist = (steps * d).astype(F32)
        lo, hi = slice(0, QBLK), slice(QBLK, 2 * QBLK)
        slopes = _block_slopes(lanes, pl.program_id(0))

        def classes(first):
            for n in range(group):
                rows = _class_rows(d, first + n)
                q_s[n], o_s[n], l_s[n], do_s[n] = q_ref[rows], o_ref[rows], l_ref[rows], do_ref[rows]
                kk_s[n, lo], kk_s[n, hi] = kp_ref[rows], kc_ref[rows]
                vv_s[n, lo], vv_s[n, hi] = vp_ref[rows], vc_ref[rows]
            for n in range(group):
                for hh in range(lanes // HEAD_DIM):
                    sl = slice(hh * HEAD_DIM, (hh + 1) * HEAD_DIM)
                    qh, doh, kk, vv = q_s[n, :, sl], do_s[n, :, sl], kk_s[n, :, sl], vv_s[n, :, sl]
                    delta = jnp.sum(doh * o_s[n, :, sl], axis=-1, keepdims=True)
                    s = jnp.where(valid, _dot_nt(qh, kk) * scale - slopes[hh] * dist, NEG)
                    p = jnp.exp(s - l_s[n, :, hh * HEAD_DIM:hh * HEAD_DIM + 1])
                    ds = p * (_dot_nt(doh, vv) - delta)
                    dq_s[n, :, sl] = _dot(ds, kk) * scale
                    dkk = _dot_tn(ds, qh) * scale
                    dvv = _dot_tn(p, doh)
                    dk_s[n, :, sl] = ck_s[first + n, :, sl] + dkk[lo]
                    dv_s[n, :, sl] = cv_s[first + n, :, sl] + dvv[lo]
                    ck_s[first + n, :, sl] = dkk[hi]
                    cv_s[first + n, :, sl] = dvv[hi]
            for n in range(group):
                rows = _class_rows(d, first + n)
                dq_ref[rows] = dq_s[n]
                dk_ref[rows] = dk_s[n]
                dv_ref[rows] = dv_s[n]

        @pl.when(j < nb)
        def _():
            _for_class_groups(d, group, classes)

        @pl.when(j == nb)
        def _():
            for r in range(d):
                dk_ref[_class_rows(d, r)] = ck_s[r]
                dv_ref[_class_rows(d, r)] = cv_s[r]

    blk = (QBLK * d, lanes)
    here = lambda j: jnp.minimum(j, nb - 1)
    cur = pl.BlockSpec(blk, lambda lb, j: (here(j), lb))
    prev = pl.BlockSpec(blk, lambda lb, j: (jnp.maximum(here(j) - 1, 0), lb))
    behind = pl.BlockSpec(blk, lambda lb, j: (jnp.maximum(j - 1, 0), lb))
    shape = jax.ShapeDtypeStruct((T, ATTN_W), F32)
    one, two = pltpu.VMEM((group, QBLK, lanes), F32), pltpu.VMEM((group, 2 * QBLK, lanes), F32)
    carry = pltpu.VMEM((d, QBLK, lanes), F32)
    return _pallas(
        body, name=f"attn_bwd_d{d}", grid=(ATTN_W // lanes, nb + 1),
        in_specs=[cur, cur, prev, cur, prev, cur, cur, cur], out_specs=[cur, behind, behind], out_shape=[shape] * 3,
        scratch_shapes=[one] * 4 + [two] * 2 + [one] * 3 + [carry] * 2,
        semantics=("parallel", "arbitrary"), args=(q, k, k, v, v, o, lse, do), rider=rider)


def _mix_in_dw(dqs, dks, dvs, du, x, g):
    T = x.shape[0]
    tm = ROW_TILE

    def body(*refs):
        x_ref, g_ref, dz_ref, dw_ref = refs[10:]

        @pl.when(pl.program_id(0) == 0)
        def _():
            dw_ref[...] = jnp.zeros_like(dw_ref)

        xv = x_ref[...]
        h_t = (xv * _rms(xv) * g_ref[...]).T.astype(BF16)
        for part in range(4):
            a, b, c = refs[3 * part:3 * part + 3] if part < 3 else (refs[9], None, None)
            dz = (a[...] if b is None else a[...] + b[...] + c[...]).astype(BF16)
            dz_ref[:, part * ATTN_W:(part + 1) * ATTN_W] = dz
            dw_ref[part] += jnp.dot(h_t, dz, preferred_element_type=F32)

    col = _rows(tm, ATTN_W)
    return pl.pallas_call(
        body, name="mix_bwd_dw", grid=(T // tm,),
        in_specs=[col] * 10 + [_rows(tm, D_MODEL), _full((1, D_MODEL))],
        out_specs=[_rows(tm, 4 * ATTN_W), _full((4, D_MODEL, ATTN_W))],
        out_shape=[jax.ShapeDtypeStruct((T, 4 * ATTN_W), BF16), jax.ShapeDtypeStruct((4, D_MODEL, ATTN_W), F32)],
        compiler_params=_params("arbitrary"),
    )(*dqs, *dks, *dvs, du, x, g)


SCAN_ROWS = 8
SCAN_LANES = 512
SSM_CHUNK = 512
SSM_CHUNK_BWD = 512
SSM_HALVES = tuple((slice(h * SSM_W // 2, (h + 1) * SSM_W // 2), slice(h * N_STATE // 2, (h + 1) * N_STATE // 2)) for h in range(2))


def _cmul(ar, ai, br, bi):
    return ar * br - ai * bi, ar * bi + ai * br


def _ssm_discretize(a_re, a_im, log_dt, b_re, b_im):
    def body(ar_ref, ai_ref, ldt_ref, br_ref, bi_ref, abr_ref, abi_ref, er_ref, ei_ref, bbr_ref, bbi_ref, pr_ref, pi_ref):
        ar, ai = ar_ref[...], ai_ref[...]
        dt = jnp.exp(ldt_ref[...])
        n = lax.broadcasted_iota(jnp.int32, (1, SCAN_ROWS), 1).astype(F32) + 1.0
        mag, ang = jnp.exp(dt * ar), dt * ai
        abr, abi = mag * jnp.cos(ang), mag * jnp.sin(ang)
        abr_ref[...], abi_ref[...] = abr, abi
        pr_ref[...] = jnp.exp(dt * ar * n) * jnp.cos(ang * n)
        pi_ref[...] = jnp.exp(dt * ar * n) * jnp.sin(ang * n)
        den = ar * ar + ai * ai
        er = ((abr - 1.0) * ar + abi * ai) / den
        ei = (abi * ar - (abr - 1.0) * ai) / den
        er_ref[...], ei_ref[...] = er, ei
        bbr_ref[...], bbi_ref[...] = _cmul(er, ei, br_ref[...], bi_ref[...])

    col = jax.ShapeDtypeStruct((N_STATE, 1), F32)
    mat = jax.ShapeDtypeStruct((N_STATE, GROUP_CH), F32)
    pw = jax.ShapeDtypeStruct((N_STATE, SCAN_ROWS), F32)
    return pl.pallas_call(body, name="ssm_discretize", out_shape=[col] * 4 + [mat] * 2 + [pw] * 2)(
        a_re, a_im, log_dt, b_re, b_im)


def _ssm_discretize_bwd(a_re, a_im, log_dt, b_re, b_im, ab_re, ab_im, e_re, e_im, dab_re, dab_im, dbb_re, dbb_im):
    def body(ar_ref, ai_ref, ldt_ref, br_ref, bi_ref, abr_ref, abi_ref, er_ref, ei_ref, dabr_ref, dabi_ref,
             dbbr_ref, dbbi_ref, dar_ref, dai_ref, ddt_ref, dbr_ref, dbi_ref):
        ar, ai, dt = ar_ref[...], ai_ref[...], jnp.exp(ldt_ref[...])
        er, ei = er_ref[...], ei_ref[...]
        gbr, gbi = dbbr_ref[...], dbbi_ref[...]
        dbr_ref[...], dbi_ref[...] = _cmul(er, -ei, gbr, gbi)
        br, bi = br_ref[...], bi_ref[...]
        der = jnp.sum(br * gbr + bi * gbi, axis=-1, keepdims=True)
        dei = jnp.sum(br * gbi - bi * gbr, axis=-1, keepdims=True)
        den = ar * ar + ai * ai
        inv_r, inv_i = ar / den, -ai / den
        t_r, t_i = _cmul(der, dei, inv_r, -inv_i)
        gab_r, gab_i = dabr_ref[...] + t_r, dabi_ref[...] + t_i
        q_r, q_i = _cmul(er, ei, inv_r, inv_i)
        dl_r, dl_i = _cmul(der, dei, q_r, -q_i)
        dl_r, dl_i = -dl_r, -dl_i
        gw_r, gw_i = _cmul(gab_r, gab_i, abr_ref[...], -abi_ref[...])
        dar_ref[...] = dl_r + dt * gw_r
        dai_ref[...] = dl_i + dt * gw_i
        ddt_ref[...] = (gw_r * ar + gw_i * ai) * dt

    col = jax.ShapeDtypeStruct((N_STATE, 1), F32)
    mat = jax.ShapeDtypeStruct((N_STATE, GROUP_CH), F32)
    return pl.pallas_call(body, name="ssm_discretize_bwd", out_shape=[col] * 3 + [mat] * 2)(
        a_re, a_im, log_dt, b_re, b_im, ab_re, ab_im, e_re, e_im, dab_re, dab_im, dbb_re, dbb_im)


def _scan_tables(p_re, p_im, reverse):
    pr, pi = p_re.T, p_im.T
    if reverse:
        pi = -pi
    row = jnp.arange(SCAN_ROWS)[:, None]
    level = lambda t, s: jnp.where((row < SCAN_ROWS - s) if reverse else (row >= s), t[s - 1][None, :], 0.0)
    carry = (pr[::-1], pi[::-1]) if reverse else (pr, pi)
    return jnp.stack([level(pr, 1), level(pi, 1), level(pr, 2), level(pi, 2), level(pr, 4), level(pi, 4), carry[0], carry[1]])


def _scan_group(xr, xi, tab_ref, ls, carry_r, carry_i, reverse):
    for n, s in enumerate((1, 2, 4)):
        shift = SCAN_ROWS - s if reverse else s
        mr, mi = _cmul(tab_ref[2 * n, :, ls], tab_ref[2 * n + 1, :, ls], pltpu.roll(xr, shift, 0), pltpu.roll(xi, shift, 0))
        xr, xi = xr + mr, xi + mi
    mr, mi = _cmul(tab_ref[6, :, ls], tab_ref[7, :, ls], carry_r, carry_i)
    return xr + mr, xi + mi


def _gelu(y):
    t = jnp.tanh(GELU_C * (y + 0.044715 * y * y * y))
    return 0.5 * y * (1.0 + t), t


def _ssm_fwd(u, tab, bd_re, bd_im, cd_re, cd_im, d_skip, w_glu, b_glu):
    T = u.shape[0]
    tc = SSM_CHUNK

    def body(u_ref, tab_ref, bdr_ref, bdi_ref, cdr_ref, cdi_ref, dsk_ref, wg_ref, bg_ref,
             sr_ref, si_ref, yp_ref, o_ref, car_r, car_i):
        @pl.when(pl.program_id(0) == 0)
        def _():
            car_r[...] = jnp.zeros_like(car_r)
            car_i[...] = jnp.zeros_like(car_i)

        uv = u_ref[...]
        for cs, ss in SSM_HALVES:
            sr_ref[:, ss] = _dot(uv[:, cs], bdr_ref[cs, ss])
            si_ref[:, ss] = _dot(uv[:, cs], bdi_ref[cs, ss])
        for lb in range(N_STATE // SCAN_LANES):
            ls = pl.ds(lb * SCAN_LANES, SCAN_LANES)

            def step(g, carry):
                rows = pl.ds(pl.multiple_of(g * SCAN_ROWS, SCAN_ROWS), SCAN_ROWS)
                xr, xi = _scan_group(sr_ref[rows, ls], si_ref[rows, ls], tab_ref, ls, carry[0], carry[1], False)
                sr_ref[rows, ls] = xr
                si_ref[rows, ls] = xi
                last = slice(SCAN_ROWS - 1, SCAN_ROWS)
                return (jnp.broadcast_to(xr[last], xr.shape), jnp.broadcast_to(xi[last], xi.shape))

            cr, ci = lax.fori_loop(0, tc // SCAN_ROWS, step, (car_r[:, ls], car_i[:, ls]))
            car_r[:, ls] = cr
            car_i[:, ls] = ci
        y = jnp.concatenate([_dot(sr_ref[:, ss], cdr_ref[ss, cs]) - _dot(si_ref[:, ss], cdi_ref[ss, cs])
                             for cs, ss in SSM_HALVES], axis=1) + dsk_ref[...] * uv
        yp_ref[...] = y
        gy, _ = _gelu(y)
        o_ref[...] = gy * jax.nn.sigmoid(_dot(gy, wg_ref[...]) + bg_ref[...])

    col, st = _rows(tc, SSM_W), _rows(tc, N_STATE)
    vec = _full((1, SSM_W))
    return pl.pallas_call(
        body, name="ssm_fwd", grid=(T // tc,),
        in_specs=[col, _full(tab.shape), _full(bd_re.shape), _full(bd_im.shape), _full(cd_re.shape), _full(cd_im.shape),
                  vec, _full(w_glu.shape), vec],
        out_specs=[st, st, col, col],
        out_shape=[jax.ShapeDtypeStruct((T, N_STATE), F32)] * 2 + [jax.ShapeDtypeStruct((T, SSM_W), F32)] * 2,
        scratch_shapes=[pltpu.VMEM((SCAN_ROWS, N_STATE), F32)] * 2,
        compiler_params=_params("arbitrary"),
    )(u, tab, bd_re, bd_im, cd_re, cd_im, d_skip, w_glu, b_glu)


def _ssm_bwd(dout, u, yp, s_re, s_im, tab, bd_re, bd_im, cd_re, cd_im, d_skip, w_glu, b_glu, rider=None):
    T = u.shape[0]
    tc = SSM_CHUNK_BWD
    nt = T // tc
    rows_per_chunk = tc // SCAN_ROWS

    def body(do_ref, u_ref, yp_ref, sr_ref, si_ref, pr_ref, pi_ref, tab_ref, dsk_ref, wg_ref, bg_ref,
             bdr_hbm, bdi_hbm, cdr_hbm, cdi_hbm,
             du_ref, dsk_out, dbg_out, dwg_out, da_out, dbdr_out, dbdi_out, dcdr_out, dcdi_out,
             bdr_v, bdi_v, cdr_v, cdi_v, dbdr_v, dbdi_v, dcdr_v, dcdi_v, gr_v, gi_v, car_r, car_i, sems):
        i = pl.program_id(0)

        @pl.when(i == 0)
        def _():
            _load_once([(hbm.at[(ss, cs) if by_state else (cs, ss)], vmem.at[h])
                        for hbm, vmem, by_state in ((bdr_hbm, bdr_v, False), (bdi_hbm, bdi_v, False),
                                                    (cdr_hbm, cdr_v, True), (cdi_hbm, cdi_v, True))
                        for h, (cs, ss) in enumerate(SSM_HALVES)], sems)
            for ref in (dbdr_v, dbdi_v, dcdr_v, dcdi_v, car_r, car_i, dsk_out, dbg_out, dwg_out, da_out):
                ref[...] = jnp.zeros_like(ref)

        uv, y, dout_v = u_ref[...], yp_ref[...], do_ref[...]
        gy, t = _gelu(y)
        sg = jax.nn.sigmoid(_dot(gy, wg_ref[...]) + bg_ref[...])
        dzg = dout_v * gy * sg * (1.0 - sg)
        dgy = dout_v * sg + _dot_nt(dzg, wg_ref[...])
        dwg_out[...] += _dot_tn(gy, dzg)
        dbg_out[...] += jnp.sum(dzg, axis=0, keepdims=True)
        dy = dgy * (0.5 * (1.0 + t) + 0.5 * y * (1.0 - t * t) * GELU_C * (1.0 + 3 * 0.044715 * y * y))
        dsk_out[...] += jnp.sum(dy * uv, axis=0, keepdims=True)

        for h, (cs, ss) in enumerate(SSM_HALVES):
            gr_v[:, ss] = _dot_nt(dy[:, cs], cdr_v[h])
            gi_v[:, ss] = -_dot_nt(dy[:, cs], cdi_v[h])
            dcdr_v[h] += _dot_tn(sr_ref[:, ss], dy[:, cs])
            dcdi_v[h] -= _dot_tn(si_ref[:, ss], dy[:, cs])

        row = lax.broadcasted_iota(jnp.int32, (SCAN_ROWS, SCAN_LANES), 0)
        first_chunk = i == nt - 1
        for lb in range(N_STATE // SCAN_LANES):
            ls = pl.ds(lb * SCAN_LANES, SCAN_LANES)

            def step(n, carry):
                g = rows_per_chunk - 1 - n
                rows = pl.ds(pl.multiple_of(g * SCAN_ROWS, SCAN_ROWS), SCAN_ROWS)
                before = pl.ds(pl.multiple_of(jnp.maximum(g - 1, 0) * SCAN_ROWS, SCAN_ROWS), SCAN_ROWS)
                xr, xi = _scan_group(gr_v[rows, ls], gi_v[rows, ls], tab_ref, ls, carry[0], carry[1], True)
                gr_v[rows, ls] = xr
                gi_v[rows, ls] = xi
                last = slice(SCAN_ROWS - 1, SCAN_ROWS)
                edge_r = jnp.where(g > 0, sr_ref[before, ls][last], jnp.where(first_chunk, 0.0, pr_ref[:, ls][last]))
                edge_i = jnp.where(g > 0, si_ref[before, ls][last], jnp.where(first_chunk, 0.0, pi_ref[:, ls][last]))
                spr = jnp.where(row >= 1, pltpu.roll(sr_ref[rows, ls], 1, 0), edge_r)
                spi = jnp.where(row >= 1, pltpu.roll(si_ref[rows, ls], 1, 0), edge_i)
                first = slice(0, 1)
                return (jnp.broadcast_to(xr[first], xr.shape), jnp.broadcast_to(xi[first], xi.shape),
                        carry[2] + xr * spr + xi * spi, carry[3] + xi * spr - xr * spi)

            zero = jnp.zeros((SCAN_ROWS, SCAN_LANES), F32)
            cr, ci, dar, dai = lax.fori_loop(0, rows_per_chunk, step, (car_r[:, ls], car_i[:, ls], zero, zero))
            car_r[:, ls] = cr
            car_i[:, ls] = ci
            da_out[0, :, ls] += dar
            da_out[1, :, ls] += dai

        du_ref[...] = dsk_ref[...] * dy + jnp.concatenate(
            [_dot_nt(gr_v[:, ss], bdr_v[h]) + _dot_nt(gi_v[:, ss], bdi_v[h]) for h, (_, ss) in enumerate(SSM_HALVES)], axis=1)
        for h, (cs, ss) in enumerate(SSM_HALVES):
            dbdr_v[h] += _dot_tn(uv[:, cs], gr_v[:, ss])
            dbdi_v[h] += _dot_tn(uv[:, cs], gi_v[:, ss])

        @pl.when(i == nt - 1)
        def _():
            per_half = N_GROUPS // len(SSM_HALVES)
            for grp in range(N_GROUPS):
                h, at = divmod(grp, per_half)
                ch, stt = pl.ds(at * GROUP_CH, GROUP_CH), pl.ds(at * STATE, STATE)
                dbdr_out[grp], dbdi_out[grp] = dbdr_v[h, ch, stt], dbdi_v[h, ch, stt]
                dcdr_out[grp], dcdi_out[grp] = dcdr_v[h, stt, ch], dcdi_v[h, stt, ch]

    rev = lambda i: (nt - 1 - i, 0)
    col = pl.BlockSpec((tc, SSM_W), rev)
    st = pl.BlockSpec((tc, N_STATE), rev)
    st_before = pl.BlockSpec((SCAN_ROWS, N_STATE), lambda i: (jnp.maximum((nt - 1 - i) * rows_per_chunk - 1, 0), 0))
    vec = _full((1, SSM_W))
    bd_blocks, cd_blocks = (N_GROUPS, GROUP_CH, STATE), (N_GROUPS, STATE, GROUP_CH)
    bd, cd = jax.ShapeDtypeStruct(bd_blocks, F32), jax.ShapeDtypeStruct(cd_blocks, F32)
    n_half = len(SSM_HALVES)
    bd_half, cd_half = (n_half, SSM_W // n_half, N_STATE // n_half), (n_half, N_STATE // n_half, SSM_W // n_half)
    return _pallas(
        body, name="ssm_bwd", grid=(nt,),
        in_specs=[col, col, col, st, st, st_before, st_before, _full(tab.shape), vec, _full(w_glu.shape), vec,
                  ANY, ANY, ANY, ANY],
        out_specs=[col, vec, vec, _full(w_glu.shape), _full((2, SCAN_ROWS, N_STATE)),
                   _full(bd_blocks), _full(bd_blocks), _full(cd_blocks), _full(cd_blocks)],
        out_shape=[jax.ShapeDtypeStruct((T, SSM_W), F32), jax.ShapeDtypeStruct((1, SSM_W), F32),
                   jax.ShapeDtypeStruct((1, SSM_W), F32), jax.ShapeDtypeStruct(w_glu.shape, F32),
                   jax.ShapeDtypeStruct((2, SCAN_ROWS, N_STATE), F32), bd, bd, cd, cd],
        scratch_shapes=[pltpu.VMEM(bd_half, BF16)] * 2 + [pltpu.VMEM(cd_half, BF16)] * 2
        + [pltpu.VMEM(bd_half, F32)] * 2 + [pltpu.VMEM(cd_half, F32)] * 2
        + [pltpu.VMEM((tc, N_STATE), F32)] * 2 + [pltpu.VMEM((SCAN_ROWS, N_STATE), F32)] * 2
        + [pltpu.SemaphoreType.DMA((4 * len(SSM_HALVES),))],
        semantics=("arbitrary",), rider=rider,
        args=(dout, u, yp, s_re, s_im, s_re, s_im, tab, d_skip, w_glu, b_glu, bd_re, bd_im, cd_re, cd_im))


def _block_diagonals(parts):
    def body(*refs):
        for t_ref, o_ref in zip(refs[:len(parts)], refs[len(parts):]):
            _, a, b = t_ref.shape
            o_ref[...] = jnp.zeros_like(o_ref)
            for grp in range(N_GROUPS):
                o_ref[grp * a:(grp + 1) * a, grp * b:(grp + 1) * b] = t_ref[grp].astype(BF16)

    return pl.pallas_call(body, name="block_diagonals",
                          out_shape=[jax.ShapeDtypeStruct((N_GROUPS * t.shape[1], N_GROUPS * t.shape[2]), BF16) for t in parts])(*parts)


def _ssm_prepare(a_re, a_im, log_dt, b_re, b_im, c_re, c_im):
    col = lambda t: t.reshape(N_STATE, 1)
    ldt = jnp.broadcast_to(log_dt.reshape(N_GROUPS, 1), (N_GROUPS, STATE)).reshape(N_STATE, 1)
    b2r, b2i = b_re.reshape(N_STATE, GROUP_CH), b_im.reshape(N_STATE, GROUP_CH)
    ab_r, ab_i, e_r, e_i, bb_r, bb_i, p_r, p_i = _ssm_discretize(col(a_re), col(a_im), ldt, b2r, b2i)
    diag = _block_diagonals([jnp.swapaxes(t.reshape(N_GROUPS, STATE, GROUP_CH), 1, 2) for t in (bb_r, bb_i)]
                            + [jnp.swapaxes(t, 1, 2) for t in (c_re, c_im)])
    bd, cd = diag[:2], diag[2:]
    saved = dict(a_re=col(a_re), a_im=col(a_im), log_dt=ldt, b_re=b2r, b_im=b2i, ab_re=ab_r, ab_im=ab_i, e_re=e_r, e_im=e_i)
    return _scan_tables(p_r, p_i, False), _scan_tables(p_r, p_i, True), bd, cd, saved


BIG = ("ffn1_w_in", "ffn1_w_out", "w_mix_in", "w_glu", "w_mix_out", "ffn2_w_in", "ffn2_w_out")
SMALL = ("ffn1_pre_g", "ffn1_post_g", "mix_pre_g", "a_re", "a_im", "log_dt", "b_re", "b_im", "c_re", "c_im",
         "d_skip", "b_glu", "mix_post_g", "ffn2_pre_g", "ffn2_post_g")


FIRST = ("ffn1_w_in", "ffn1_w_out")
REST = ("w_mix_in", "w_glu", "w_mix_out", "ffn2_w_in", "ffn2_w_out")
LATE = ("w_mix_in", "w_glu", "w_mix_out", "ffn1_w_out")
SHARD_SHAPE = {"ffn1_w_in": (D_MODEL, FF_BLK), "ffn2_w_in": (D_MODEL, FF_BLK), "ffn1_w_out": (D_FF // N_CHIPS, D_MODEL),
               "ffn2_w_out": (D_FF // N_CHIPS, D_MODEL), "w_mix_in": (D_MODEL, ATTN_W), "w_glu": (SSM_W // N_CHIPS, SSM_W),
               "w_mix_out": (2 * ATTN_W // N_CHIPS, D_MODEL)}


class _Reduction:
    def __init__(self, names, grads, place):
        self.names, self.local, self.place = list(names), list(grads), place

    def exchange(self):
        return _pair_exchange(self.local)

    def first(self, got):
        both = [_pair_sum(a, b, self.place["core"], name=f"pair_sum_{n}") for n, a, b in zip(self.names, self.local, got)]
        self.pair = [f32 for f32, _ in both]
        return _reduce_first([wire for _, wire in both])

    def second(self, got):
        both = [_reduce_sum_first(a, b, self.place["sel_first"], name=f"sum_first_{n}")
                for n, a, b in zip(self.names, self.pair, got)]
        self.sums = [f32 for f32, _ in both]
        return _reduce_second([wire for _, wire in both])

    def swap(self, got):
        self.halves = [_reduce_sum_second(a, b, self.place["sel_second"], name=f"sum_second_{n}").reshape(2 * a.shape[2], a.shape[3])
                       for n, a, b in zip(self.names, self.sums, got)]
        return _pair_swap(self.halves)

    def done(self, got):
        return {n: (mine, theirs) for n, mine, theirs in zip(self.names, self.halves, got)}


def _local_step(x, target, p, w, place=None):
    vec = lambda t: t.reshape(1, -1)
    w1_in, w1_out = w["ffn1_w_in"], w["ffn1_w_out"].reshape(2, FF_BLK, D_MODEL)
    blocks = lambda n, t: t.reshape((N_CHIPS,) + SHARD_SHAPE[n])

    ffn1 = functools.partial(_ffn_fwd, x, vec(p["ffn1_pre_g"]), w1_in, w1_out, vec(p["ffn1_post_g"]), None, name="ffn1_fwd")
    if place is None:
        x1, z1, f1 = ffn1()
    else:
        (x1, z1, f1), rest = ffn1(rider=_gather_weights([w[n] for n in REST]))
        w = dict(w, **dict(zip(REST, rest)))
    w2_in, w2_out = w["ffn2_w_in"], w["ffn2_w_out"].reshape(2, FF_BLK, D_MODEL)
    w_mi, w_glu, w_mo = w["w_mix_in"], w["w_glu"].reshape(SSM_W, SSM_W), w["w_mix_out"].reshape(2, ATTN_W, D_MODEL)
    q, k, v, u = _mix_in_fwd(x1, vec(p["mix_pre_g"]), w_mi)
    branches = [_attn_branch_fwd(q, k, v, d) for d in DILATIONS]
    tab_f, tab_b, bd, cd, sv = _ssm_prepare(p["a_re"], p["a_im"], p["log_dt"], p["b_re"], p["b_im"], p["c_re"], p["c_im"])
    ssm_args = (bd[0], bd[1], cd[0], cd[1], vec(p["d_skip"]), w_glu, vec(p["b_glu"]))
    s_re, s_im, yp, ssm = _ssm_fwd(u, tab_f, *ssm_args)
    x2, mixed, attn, lse = _mix_out_fwd(x1, [o for o, _ in branches], [l for _, l in branches], ssm, w_mo, vec(p["mix_post_g"]))
    dx3, loss_rows, z2, f2 = _ffn_fwd(x2, vec(p["ffn2_pre_g"]), w2_in, w2_out, vec(p["ffn2_post_g"]), target, name="ffn2_fwd")

    g = {}
    ride = (lambda call, exchange: call(rider=exchange)) if place else (lambda call, exchange: (call(), None))
    dz2, g["ffn2_post_g"], dw2_out = _ffn_bwd_out(dx3, f2, z2, w2_out, vec(p["ffn2_post_g"]), name="ffn2_bwd_out")
    dw2_in = _norm_matmul_dw(dz2, x2, vec(p["ffn2_pre_g"]), N_CHIPS, name="ffn2_bwd_dw", together=2)
    early = _Reduction(("ffn2_w_in", "ffn2_w_out"), [dw2_in, blocks("ffn2_w_out", dw2_out)], place) if place else None
    (dx2, g["ffn2_pre_g"]), got = ride(
        functools.partial(_norm_matmul_dx, dz2, x2, dx3, vec(p["ffn2_pre_g"]), w2_in, name="ffn2_bwd_dx"), early and early.exchange())
    dattn, dssm, dw_mo, g["mix_post_g"] = _mix_out_bwd(dx2, mixed, attn, ssm, w_mo, vec(p["mix_post_g"]))
    (du, g["d_skip"], g["b_glu"], dw_glu, da, dbd_re, dbd_im, dcd_re, dcd_im), got = ride(
        functools.partial(_ssm_bwd, dssm, u, yp, s_re, s_im, tab_b, *ssm_args), early and early.first(got))
    branch = lambda d: functools.partial(_attn_branch_bwd, q, k, v, attn, lse, dattn, d)
    parts = [None] * 3
    parts[0], got = ride(branch(DILATIONS[0]), early and early.second(got))
    parts[1], early_theirs = ride(branch(DILATIONS[1]), early and early.swap(got))
    parts[2] = branch(DILATIONS[2])()
    dproj, dw_mi = _mix_in_dw([r[0] for r in parts], [r[1] for r in parts], [r[2] for r in parts], du, x1, vec(p["mix_pre_g"]))
    dx1, g["mix_pre_g"] = _norm_matmul_dx(dproj, x1, dx2, vec(p["mix_pre_g"]), w_mi, name="mix_bwd_dx")
    dz1, g["ffn1_post_g"], dw1_out = _ffn_bwd_out(dx1, f1, z1, w1_out, vec(p["ffn1_post_g"]), name="ffn1_bwd_out")
    big = {"ffn2_w_in": dw2_in, "ffn2_w_out": dw2_out, "w_mix_in": dw_mi, "w_glu": dw_glu, "w_mix_out": dw_mo, "ffn1_w_out": dw1_out}
    late = _Reduction(LATE, [blocks(n, big[n]) for n in LATE], place) if place else None
    big["ffn1_w_in"], got = ride(
        functools.partial(_norm_matmul_dw, dz1, x, vec(p["ffn1_pre_g"]), N_CHIPS, name="ffn1_bwd_dw", together=2),
        late and late.exchange())
    dx_call = functools.partial(_norm_matmul_dx, dz1, x, dx1, vec(p["ffn1_pre_g"]), w1_in)
    if place:
        last = _Reduction(("ffn1_w_in",), [big["ffn1_w_in"]], place)
        n_late = len(LATE)
        (dx_part, dg_a), got = dx_call(name="ffn1_bwd_dx_a", part=(0, 0.375), rider=_merged(late.first(got), last.exchange()))
        (dx_part, dg_b), got = dx_call(name="ffn1_bwd_dx_b", part=(0.375, 0.75), into=dx_part,
                                       rider=_merged(late.second(got[:n_late]), last.first(got[n_late:])))
        grad_x, dg_c = dx_call(name="ffn1_bwd_dx_c", part=(0.75, 1), into=dx_part)
        g["ffn1_pre_g"] = dg_a + dg_b + dg_c
        got = _merged(late.swap(got[:n_late]), last.second(got[n_late:])).run("tail_second")
        g.update(late.done(got[:n_late]))
        g.update(last.done(last.swap(got[n_late:]).run("tail_swap")))
        g.update(early.done(early_theirs))
    else:
        grad_x, g["ffn1_pre_g"] = dx_call(name="ffn1_bwd_dx")
        g.update({n: blocks(n, big[n]) for n in BIG})

    g["c_re"], g["c_im"] = (jnp.swapaxes(m, 1, 2) for m in (dcd_re, dcd_im))
    dbb = [jnp.swapaxes(m, 1, 2).reshape(N_STATE, GROUP_CH) for m in (dbd_re, dbd_im)]
    dab = [jnp.sum(da[n], axis=0).reshape(N_STATE, 1) for n in range(2)]
    da_re, da_im, dldt, db_re, db_im = _ssm_discretize_bwd(
        sv["a_re"], sv["a_im"], sv["log_dt"], sv["b_re"], sv["b_im"], sv["ab_re"], sv["ab_im"], sv["e_re"], sv["e_im"],
        dab[0], dab[1], dbb[0], dbb[1])
    g["a_re"], g["a_im"] = da_re.reshape(N_GROUPS, STATE), da_im.reshape(N_GROUPS, STATE)
    g["log_dt"] = jnp.sum(dldt.reshape(N_GROUPS, STATE), axis=-1)
    g["b_re"], g["b_im"] = (t.reshape(N_GROUPS, STATE, GROUP_CH) for t in (db_re, db_im))
    return loss_rows, grad_x, g


MESH = pl.DeviceIdType.MESH
N_REL = 3


def _place():
    x, y, c = lax.axis_index("x"), lax.axis_index("y"), lax.axis_index("c")
    return x, y, c, [(1 - x, y), (x, 1 - y), (1 - x, 1 - y)]


def _remote(src, dst, send_sems, recv_sems, idx, to):
    return pltpu.make_async_remote_copy(src_ref=src, dst_ref=dst, send_sem=send_sems.at[idx], recv_sem=recv_sems.at[idx],
                                        device_id=to, device_id_type=MESH)


def _half(rows, who):
    return pl.ds(who * (rows // 2), rows // 2)


class _Copies:
    def __init__(self, send_sems, recv_sems, local_sems):
        self.send_sems, self.recv_sems, self.local_sems = send_sems, recv_sems, local_sems
        self.n_remote = self.n_local = 0

    def remote(self, src, dst, to):
        k, self.n_remote = self.n_remote, self.n_remote + 1
        return pltpu.make_async_remote_copy(src_ref=src, dst_ref=dst, send_sem=self.send_sems.at[k],
                                            recv_sem=self.recv_sems.at[k], device_id=to, device_id_type=MESH)

    def local(self, src, dst):
        k, self.n_local = self.n_local, self.n_local + 1
        return pltpu.make_async_copy(src, dst, self.local_sems.at[k])


class _Exchange:
    def __init__(self, plan, ins, out_shapes, n_remote):
        self.plan, self.ins, self.out_shapes, self.n_remote = plan, list(ins), list(out_shapes), n_remote
        self.sems = [pltpu.SemaphoreType.DMA((n_remote,)), pltpu.SemaphoreType.DMA((n_remote,)), pltpu.SemaphoreType.DMA((1,))]

    def run(self, name):
        n_in = len(self.ins)

        def body(*refs):
            for phase in self.plan(refs[:n_in], refs[n_in:-3], _Copies(*refs[-3:])):
                for cp in phase:
                    cp.start()
                for cp in phase:
                    cp.wait()

        return pl.pallas_call(body, name=name, in_specs=[ANY] * n_in, out_specs=[ANY] * len(self.out_shapes),
                              out_shape=self.out_shapes, scratch_shapes=self.sems)(*self.ins)


def _merged(a, b):
    n_in, n_out = len(a.ins), len(a.out_shapes)

    def plan(ins, outs, mk):
        (phase_a,), (phase_b,) = a.plan(ins[:n_in], outs[:n_out], mk), b.plan(ins[n_in:], outs[n_out:], mk)
        return [phase_a + phase_b]

    return _Exchange(plan, a.ins + b.ins, a.out_shapes + b.out_shapes, a.n_remote + b.n_remote)


def _pallas(body, *, name, grid, in_specs, out_specs, out_shape, args, semantics, scratch_shapes=(), rider=None, aliases=None):
    aliases = aliases or {}
    if rider is None:
        return pl.pallas_call(body, name=name, grid=grid, in_specs=in_specs, out_specs=out_specs, out_shape=out_shape,
                              scratch_shapes=list(scratch_shapes), input_output_aliases=aliases,
                              compiler_params=_params(*semantics))(*args)
    n_in, n_out, r_in, r_out = len(in_specs), len(out_specs), len(rider.ins), len(rider.out_shapes)
    n_steps = math.prod(grid)

    def carrier(*refs):
        ins, rider_ins = refs[:n_in], refs[n_in:n_in + r_in]
        outs = refs[n_in + r_in:n_in + r_in + n_out]
        rider_outs = refs[n_in + r_in + n_out:n_in + r_in + n_out + r_out]
        scratch, sems = refs[n_in + r_in + n_out + r_out:-3], refs[-3:]
        step = 0
        for axis, size in enumerate(grid):
            step = step * size + pl.program_id(axis)
        phases = rider.plan(rider_ins, rider_outs, _Copies(*sems))

        def start_phase(p):
            for cp in (phases[p - 1] if p else []):
                cp.wait()
            for cp in phases[p]:
                cp.start()

        for p in range(len(phases)):
            pl.when(step == p * n_steps // len(phases))(functools.partial(start_phase, p))
        body(*ins, *outs, *scratch)

        @pl.when(step == n_steps - 1)
        def _():
            for cp in phases[-1]:
                cp.wait()

    results = pl.pallas_call(
        carrier, name=name, grid=grid, in_specs=list(in_specs) + [ANY] * r_in, out_specs=list(out_specs) + [ANY] * r_out,
        out_shape=list(out_shape) + rider.out_shapes, scratch_shapes=list(scratch_shapes) + rider.sems,
        input_output_aliases=aliases, compiler_params=_params(*["arbitrary"] * len(grid)))(*args, *rider.ins)
    return results[:n_out], results[n_out:]


def _gather_weights(shards):
    def plan(ins, outs, mk):
        x, y, c = lax.axis_index("x"), lax.axis_index("y"), lax.axis_index("c")
        me, sibling = 2 * x + y, (x, y, 1 - c)
        x_nb, y_nb, diag = (1 - x, y), (x, 1 - y), (1 - x, 1 - y)
        index = lambda chip: 2 * chip[0] + chip[1]
        first, second, third = [], [], []
        for i, shard in enumerate(shards):
            rows = shard.shape[0]
            mine = _half(rows, c)
            quarter = lambda which: pl.ds(c * (rows // 2) + which * (rows // 4), rows // 4)
            first.append(mk.remote(ins[i], outs[i].at[me], sibling))
            for nb in (x_nb, y_nb):
                first.append(mk.remote(ins[i].at[mine], outs[i].at[me, mine], (*nb, c)))
            for nb in (x_nb, y_nb):
                landed = outs[i].at[index(nb), mine]
                second.append(mk.remote(landed, landed, sibling))
            for nb, other, which in ((x_nb, y_nb, 0), (y_nb, x_nb, 1)):
                landed = outs[i].at[index(nb), quarter(which)]
                second.append(mk.remote(landed, landed, (*other, c)))
            landed = outs[i].at[index(diag), mine]
            third.append(mk.remote(landed, landed, sibling))
        return [first, second, third]

    return _Exchange(plan, shards, [jax.ShapeDtypeStruct((N_CHIPS,) + s.shape, s.dtype) for s in shards], 8 * len(shards))


def _pair_exchange(grads):
    def plan(ins, outs, mk):
        x, y, c = lax.axis_index("x"), lax.axis_index("y"), lax.axis_index("c")
        return [[mk.remote(ins[i].at[:, _half(g.shape[1], 1 - c)], outs[i], (x, y, 1 - c)) for i, g in enumerate(grads)]]

    return _Exchange(plan, grads, [jax.ShapeDtypeStruct((N_CHIPS, g.shape[1] // 2, g.shape[2]), g.dtype) for g in grads], len(grads))


def _reduce_first(pair):
    def plan(ins, outs, mk):
        x, y, c = lax.axis_index("x"), lax.axis_index("y"), lax.axis_index("c")
        phase = []
        for i, p in enumerate(pair):
            q = p.shape[1] // 2
            phase.append(mk.remote(ins[i].at[pl.ds(2 * (1 - x), 2), pl.ds(0, q)], outs[i].at[0], (1 - x, y, c)))
            for jx in range(2):
                phase.append(mk.remote(ins[i].at[2 * jx + 1 - y, pl.ds(q, q)], outs[i].at[1, jx], (x, 1 - y, c)))
        return [phase]

    return _Exchange(plan, pair, [jax.ShapeDtypeStruct((2, 2, p.shape[1] // 2, p.shape[2]), p.dtype) for p in pair], 3 * len(pair))


def _reduce_second(sums):
    def plan(ins, outs, mk):
        x, y, c = lax.axis_index("x"), lax.axis_index("y"), lax.axis_index("c")
        phase = []
        for i in range(len(sums)):
            phase.append(mk.remote(ins[i].at[0, 1 - y], outs[i].at[0], (x, 1 - y, c)))
            phase.append(mk.remote(ins[i].at[1, 1 - x], outs[i].at[1], (1 - x, y, c)))
        return [phase]

    return _Exchange(plan, sums, [jax.ShapeDtypeStruct((2,) + s.shape[2:], s.dtype) for s in sums], 2 * len(sums))


def _pair_swap(halves):
    def plan(ins, outs, mk):
        x, y, c = lax.axis_index("x"), lax.axis_index("y"), lax.axis_index("c")
        return [[mk.remote(ins[i], outs[i], (x, y, 1 - c)) for i in range(len(halves))]]

    return _Exchange(plan, halves, [jax.ShapeDtypeStruct(h.shape, h.dtype) for h in halves], len(halves))


def _allreduce_small(packed):
    rows = packed.shape[0]
    n_dev = 2 * N_CHIPS

    def body(x_ref, o_ref, buf, send_sems, recv_sems):
        x, y, c, chips = _place()
        sibling = (x, y, 1 - c)

        def slot(px, py, pc):
            return buf.at[4 * px + 2 * py + pc]

        buf[4 * x + 2 * y + c] = x_ref[...]
        first = [_remote(x_ref, slot(x, y, c), send_sems, recv_sems, 0, sibling)]
        first += [_remote(x_ref, slot(x, y, c), send_sems, recv_sems, 1 + k, (*chip, c)) for k, chip in enumerate(chips)]
        for cp in first:
            cp.start()
        passed = []
        for k, chip in enumerate(chips):
            landed = slot(*chip, c)
            _remote(landed, landed, send_sems, recv_sems, 1 + k, (*chip, c)).wait_recv()
            passed.append(_remote(landed, landed, send_sems, recv_sems, 1 + N_REL + k, sibling))
            passed[-1].start()
        _remote(slot(*sibling), slot(*sibling), send_sems, recv_sems, 0, sibling).wait_recv()
        for k, chip in enumerate(chips):
            landed = slot(*chip, 1 - c)
            _remote(landed, landed, send_sems, recv_sems, 1 + N_REL + k, sibling).wait_recv()
        for cp in first + passed:
            cp.wait_send()
        total = buf[0]
        for d in range(1, n_dev):
            total = total + buf[d]
        o_ref[...] = total

    vm = pl.BlockSpec(memory_space=pltpu.VMEM)
    return pl.pallas_call(
        body, name="allreduce_small", in_specs=[vm], out_specs=vm, out_shape=jax.ShapeDtypeStruct(packed.shape, F32),
        scratch_shapes=[pltpu.VMEM((n_dev, rows, 128), F32), pltpu.SemaphoreType.DMA((1 + 2 * N_REL,)),
                        pltpu.SemaphoreType.DMA((1 + 2 * N_REL,))],
    )(packed)


def _row_tile(rows, cap=256):
    return max(t for t in range(8, cap + 1, 8) if rows % t == 0)


def _pair_sum(grad, got, c, name):
    _, half, cols = got.shape
    tr = _row_tile(half)
    nt = half // tr

    def body(c_ref, g_ref, r_ref, o_ref, wire_ref):
        total = g_ref[...] + r_ref[...]
        o_ref[...] = total
        wire_ref[...] = total.astype(BF16)

    blk = (1, tr, cols)
    out = pl.BlockSpec(blk, lambda j, t, c_ref: (j, t, 0))
    return pl.pallas_call(
        body, name=name,
        grid_spec=pltpu.PrefetchScalarGridSpec(
            num_scalar_prefetch=1, grid=(N_CHIPS, nt),
            in_specs=[pl.BlockSpec(blk, lambda j, t, c_ref: (j, c_ref[0] * nt + t, 0)), out], out_specs=[out, out]),
        out_shape=[jax.ShapeDtypeStruct(got.shape, F32), jax.ShapeDtypeStruct(got.shape, BF16)],
        compiler_params=_params("parallel", "parallel"),
    )(c, grad, got)


def _reduce_sum_first(pair, got, sel, name):
    _, _, q, cols = got.shape
    tr = _row_tile(q)
    nt = q // tr

    def body(sel_ref, p_ref, r_ref, o_ref, wire_ref):
        total = p_ref[0] + r_ref[0, 0].astype(F32)
        o_ref[0, 0] = total
        wire_ref[0, 0] = total.astype(BF16)

    blk = pl.BlockSpec((1, 1, tr, cols), lambda p, k, t, s: (p, k, t, 0))
    return pl.pallas_call(
        body, name=name,
        grid_spec=pltpu.PrefetchScalarGridSpec(
            num_scalar_prefetch=1, grid=(2, 2, nt),
            in_specs=[pl.BlockSpec((1, tr, cols), lambda p, k, t, s: (s[2 * p] + s[2 * p + 1] * k, p * nt + t, 0)), blk],
            out_specs=[blk, blk]),
        out_shape=[jax.ShapeDtypeStruct(got.shape, F32), jax.ShapeDtypeStruct(got.shape, BF16)],
        compiler_params=_params("parallel", "parallel", "parallel"),
    )(sel, pair, got)


def _reduce_sum_second(sums, got, sel, name):
    _, q, cols = got.shape
    tr = _row_tile(q)

    def body(sel_ref, s_ref, r_ref, o_ref):
        o_ref[0] = s_ref[0, 0] + r_ref[0].astype(F32)

    blk = (1, tr, cols)
    return pl.pallas_call(
        body, name=name,
        grid_spec=pltpu.PrefetchScalarGridSpec(
            num_scalar_prefetch=1, grid=(2, q // tr),
            in_specs=[pl.BlockSpec((1, 1, tr, cols), lambda p, t, s: (p, s[p], t, 0)),
                      pl.BlockSpec(blk, lambda p, t, s: (p, t, 0))],
            out_specs=pl.BlockSpec(blk, lambda p, t, s: (p, t, 0))),
        out_shape=jax.ShapeDtypeStruct(got.shape, F32), compiler_params=_params("parallel", "parallel"),
    )(sel, sums, got)


def _adamw_update(w, g, m, v):
    m2 = ADAM_B1 * m + (1.0 - ADAM_B1) * g
    v2 = ADAM_B2 * v + (1.0 - ADAM_B2) * (g * g)
    m_hat = m2 / (1.0 - ADAM_B1 ** ADAM_STEP)
    v_hat = v2 / (1.0 - ADAM_B2 ** ADAM_STEP)
    return -ADAM_LR * (m_hat / (jnp.sqrt(v_hat) + ADAM_EPS) + ADAM_WD * w), m2, v2


def _adamw_small(ws, gs, ms, vs):
    n = len(ws)

    def body(*refs):
        w, g, m, v, d, mo, vo = (refs[k * n:(k + 1) * n] for k in range(7))
        for i in range(n):
            d[i][...], mo[i][...], vo[i][...] = _adamw_update(w[i][...], g[i][...], m[i][...], v[i][...])

    shapes = [jax.ShapeDtypeStruct(t.shape, F32) for t in ws]
    outs = pl.pallas_call(body, name="adamw_small", out_shape=shapes * 3,
                          compiler_params=pltpu.CompilerParams(vmem_limit_bytes=VMEM_LIMIT_V7X))(*ws, *gs, *ms, *vs)
    return outs[:n], outs[n:2 * n], outs[2 * n:]


def _adamw_halves(w, mine, theirs, m, v, core, name):
    rows, cols = w.shape
    tr = _row_tile(rows // 2)
    per_half = rows // 2 // tr

    def body(core_ref, w_ref, a_ref, b_ref, m_ref, v_ref, g_ref, d_ref, mo_ref, vo_ref):
        g = jnp.where(pl.program_id(0) // per_half == core_ref[0], a_ref[...], b_ref[...])
        g_ref[...] = g
        d_ref[...], mo_ref[...], vo_ref[...] = _adamw_update(w_ref[...], g, m_ref[...], v_ref[...])

    blk = pl.BlockSpec((tr, cols), lambda t, c: (t, 0))
    half = lambda own: pl.BlockSpec(
        (tr, cols), lambda t, c: (jnp.clip(t - (c[0] if own else 1 - c[0]) * per_half, 0, per_half - 1), 0))
    return pl.pallas_call(
        body, name=name,
        grid_spec=pltpu.PrefetchScalarGridSpec(
            num_scalar_prefetch=1, grid=(2 * per_half,), in_specs=[blk, half(True), half(False), blk, blk], out_specs=[blk] * 4),
        out_shape=[jax.ShapeDtypeStruct(w.shape, F32)] * 4, compiler_params=_params("arbitrary"),
    )(core, w, mine, theirs, m, v)


def _pack(parts):
    flat = []
    for t in parts:
        t = t.reshape(-1).astype(F32)
        flat.append(jnp.pad(t, (0, -t.shape[0] % 128)))
    flat = jnp.concatenate(flat)
    return jnp.pad(flat, (0, -flat.shape[0] % 1024)).reshape(-1, 128)


def _unpack(buf, shapes):
    flat, out, at = buf.reshape(-1), [], 0
    for s in shapes:
        size = math.prod(s)
        out.append(flat[at:at + size].reshape(s))
        at += size + (-size % 128)
    return out


def kernel(x, ffn1_pre_g, ffn1_w_in, ffn1_w_out, ffn1_post_g, mix_pre_g, w_mix_in, a_re, a_im, log_dt, b_re, b_im, c_re, c_im, d_skip, w_glu, b_glu, w_mix_out, mix_post_g, ffn2_pre_g, ffn2_w_in, ffn2_w_out, ffn2_post_g, loss_target, m_ffn1_pre_g, m_ffn1_w_in, m_ffn1_w_out, m_ffn1_post_g, m_mix_pre_g, m_w_mix_in, m_a_re, m_a_im, m_log_dt, m_b_re, m_b_im, m_c_re, m_c_im, m_d_skip, m_w_glu, m_b_glu, m_w_mix_out, m_mix_post_g, m_ffn2_pre_g, m_ffn2_w_in, m_ffn2_w_out, m_ffn2_post_g, v_ffn1_pre_g, v_ffn1_w_in, v_ffn1_w_out, v_ffn1_post_g, v_mix_pre_g, v_w_mix_in, v_a_re, v_a_im, v_log_dt, v_b_re, v_b_im, v_c_re, v_c_im, v_d_skip, v_w_glu, v_b_glu, v_w_mix_out, v_mix_post_g, v_ffn2_pre_g, v_ffn2_w_in, v_ffn2_w_out, v_ffn2_post_g):
    given = dict(locals())
    order = ("ffn1_pre_g", "ffn1_w_in", "ffn1_w_out", "ffn1_post_g", "mix_pre_g", "w_mix_in", "a_re", "a_im", "log_dt",
             "b_re", "b_im", "c_re", "c_im", "d_skip", "w_glu", "b_glu", "w_mix_out", "mix_post_g", "ffn2_pre_g",
             "ffn2_w_in", "ffn2_w_out", "ffn2_post_g")
    at_x, at_y, at_c = (lax.axis_index(a).astype(jnp.int32) for a in ("x", "y", "c"))
    place = dict(core=at_c.reshape(1), sel_first=jnp.stack([2 * at_x, jnp.int32(1), at_y, jnp.int32(2)]),
                 sel_second=jnp.stack([at_y, at_x]))

    shards = {n: given[n][0] for n in BIG}
    w = {n: shards[n].astype(BF16) for n in REST}
    w.update(zip(FIRST, _gather_weights([shards[n].astype(BF16) for n in FIRST]).run("gather_first")))
    small = {n: given[n][0] for n in SMALL}
    loss_rows, grad_x, g = _local_step(x[0], loss_target[0], small, w, place)

    total = _allreduce_small(_pack([g[n] for n in SMALL] + [loss_rows[0, :1]]))
    parts = _unpack(total, [small[n].shape for n in SMALL] + [(1,)])
    grads = dict(zip(SMALL, parts[:-1]))
    loss = parts[-1][0]

    delta, new_m, new_v = {}, {}, {}
    for n in BIG:
        grads[n], delta[n], new_m[n], new_v[n] = _adamw_halves(
            shards[n], *g[n], given["m_" + n][0], given["v_" + n][0], place["core"], name=f"adamw_{n}")
    take = lambda pre: [given[pre + n] for n in SMALL]
    outs = _adamw_small(take(""), [grads[n][None] for n in SMALL], take("m_"), take("v_"))
    for store, arrays in zip((delta, new_m, new_v), outs):
        store.update({n: t[0] for n, t in zip(SMALL, arrays)})

    lead = lambda d: [d[n][None] for n in order]
    return (loss, grad_x[None], *lead(grads), *lead(delta), *lead(new_m), *lead(new_v))
```
